```python
import math
import jax, jax.numpy as jnp
from jax import lax
import numpy as np

D_MODEL = 2048
BATCH = 8
SEQ = 4096
DEPTH = 2

HEAD_DIM = 128
N_HEADS = D_MODEL // HEAD_DIM
N_HEADS_FOX = N_HEADS // 2
N_HEADS_DIL = N_HEADS - N_HEADS_FOX
D_FOX = N_HEADS_FOX * HEAD_DIM
D_DIL = N_HEADS_DIL * HEAD_DIM
D_MIX = D_FOX + D_DIL
SPLIT_SIZES = (D_FOX, D_FOX, D_FOX, N_HEADS_FOX, D_DIL, D_DIL, D_DIL)
SPLIT_POINTS = tuple(int(s) for s in np.cumsum(SPLIT_SIZES)[:-1])
N_IN = int(sum(SPLIT_SIZES))
D_FF = 4 * D_MODEL
Q_BLOCK = 128
DIL_PATTERNS = ((128, 1), (512, 4), (2048, 16))
DIL_BLOCK = 128
REL_BUCKETS = 32
REL_MAX_DISTANCE = 2048
NORM_EPS = 1e-6
NEG_INF = -1e30

kernel_name = "hymba_fox_dilated_hybrid"


def rms_norm(x, g):
    xf = x.astype(jnp.float32)
    y = xf * lax.rsqrt(jnp.mean(xf * xf, axis=-1, keepdims=True) + NORM_EPS)
    return (y * g.astype(jnp.float32)).astype(x.dtype)


def fox_attention(q, k, v, log_f):
    B, S, H, E = q.shape
    scale = E ** -0.5
    c = jnp.cumsum(log_f, axis=1).transpose(0, 2, 1)
    qh = q.transpose(0, 2, 1, 3)
    kh = k.transpose(0, 2, 1, 3)
    vh = v.transpose(0, 2, 1, 3)
    k_pos = jnp.arange(S)
    n_blk = S // Q_BLOCK

    def block(i):
        start = i * Q_BLOCK
        qb = lax.dynamic_slice_in_dim(qh, start, Q_BLOCK, axis=2)
        cb = lax.dynamic_slice_in_dim(c, start, Q_BLOCK, axis=2)
        logits = jnp.einsum('bhqe,bhke->bhqk', qb, kh,
                            preferred_element_type=jnp.float32) * scale
        logits = logits + (cb[..., :, None] - c[..., None, :])
        q_pos = start + jnp.arange(Q_BLOCK)
        causal = k_pos[None, :] <= q_pos[:, None]
        logits = jnp.where(causal, logits, NEG_INF)
        p = jax.nn.softmax(logits, axis=-1)
        return jnp.einsum('bhqk,bhke->bqhe', p.astype(v.dtype), vh)

    out = lax.map(block, jnp.arange(n_blk))
    return out.transpose(1, 0, 2, 3, 4).reshape(B, S, H, E)


def rel_bucket(dist):
    max_exact = REL_BUCKETS // 2
    d = jnp.maximum(dist.astype(jnp.float32), 1.0)
    large = max_exact + (jnp.log(d / max_exact) / math.log(REL_MAX_DISTANCE / max_exact)
                         * (REL_BUCKETS - max_exact)).astype(jnp.int32)
    large = jnp.minimum(large, REL_BUCKETS - 1)
    return jnp.where(dist < max_exact, dist, large)


def dilated_pattern(q, k, v, rel_bias, window, dilation):
    B, S, H, E = q.shape
    scale = E ** -0.5
    span = window // dilation
    group = dilation * DIL_BLOCK
    s_pad = -(-S // group) * group
    pad = s_pad - S
    n_sub = s_pad // dilation
    n_blk = n_sub // DIL_BLOCK

    def to_blocks(t):
        t = jnp.pad(t, ((0, 0), (0, pad), (0, 0), (0, 0)))
        t = t.reshape(B, n_sub, dilation, H, E).transpose(0, 2, 3, 1, 4)
        return t.reshape(B, dilation, H, n_blk, DIL_BLOCK, E)

    def with_prev(t):
        prev = jnp.pad(t, ((0, 0), (0, 0), (0, 0), (1, 0), (0, 0), (0, 0)))[:, :, :, :-1]
        return jnp.concatenate([prev, t], axis=4)

    qb = to_blocks(q)
    kc = with_prev(to_blocks(k))
    vc = with_prev(to_blocks(v))
    logits = jnp.einsum('bdhnqe,bdhnke->bdhnqk', qb, kc,
                        preferred_element_type=jnp.float32) * scale
    i = jnp.arange(DIL_BLOCK)[:, None]
    j = jnp.arange(2 * DIL_BLOCK)[None, :]
    rel = DIL_BLOCK + i - j
    in_band = (rel >= 0) & (rel <= span)
    bias = rel_bias.astype(jnp.float32)[rel_bucket(jnp.clip(rel, 0, span) * dilation)]
    bias = bias.transpose(2, 0, 1)
    key_valid = (jnp.arange(n_blk)[:, None, None] > 0) | (j[None] >= DIL_BLOCK)
    mask = in_band[None] & key_valid
    logits = logits + bias[None, None, :, None]
    logits = jnp.where(mask[None, None, None], logits, NEG_INF)
    m = jnp.max(logits, axis=-1, keepdims=True)
    p = jnp.exp(logits - m)
    s = jnp.sum(p, axis=-1, keepdims=True)
    o = jnp.einsum('bdhnqk,bdhnke->bdhnqe', p, vc.astype(jnp.float32)) / s
    lse = (m + jnp.log(s))[..., 0]

    def from_blocks(t):
        tail = t.shape[5:]
        t = t.reshape((B, dilation, H, n_sub) + tail)
        t = jnp.moveaxis(t, 3, 1)
        return t.reshape((B, s_pad, H) + tail)[:, :S]

    return from_blocks(o), from_blocks(lse)


def dilated_attention(q, k, v, rel_bias):
    outs, lses = [], []
    for window, dilation in DIL_PATTERNS:
        o, l = dilated_pattern(q, k, v, rel_bias, window, dilation)
        outs.append(o)
        lses.append(l)
    alpha = jax.nn.softmax(jnp.stack(lses, axis=0), axis=0)
    return jnp.einsum('pbsh,pbshe->bshe', alpha, jnp.stack(outs, axis=0))


def _fwd_setup_inputs(seed: int = 0) -> dict:
    key = jax.random.key(seed)
    ks = jax.random.split(key, 12)
    f32 = jnp.float32
    return {
        "x": jax.random.normal(ks[0], (BATCH, SEQ, D_MODEL), f32),
        "norm1_g": 1.0 + 0.02 * jax.random.normal(ks[1], (DEPTH, D_MODEL), f32),
        "w_in": jax.random.normal(ks[2], (DEPTH, D_MODEL, N_IN), f32) * D_MODEL ** -0.5,
        "forget_b": jax.random.uniform(ks[3], (DEPTH, N_HEADS_FOX), f32, 1.0, 4.0),
        "rel_bias": 0.5 * jax.random.normal(ks[4], (REL_BUCKETS, N_HEADS_DIL), f32),
        "outnorm_a_g": 1.0 + 0.02 * jax.random.normal(ks[5], (DEPTH, D_FOX), f32),
        "outnorm_b_g": 1.0 + 0.02 * jax.random.normal(ks[6], (DEPTH, D_DIL), f32),
        "w_out": jax.random.normal(ks[7], (DEPTH, D_MIX, D_MODEL), f32) * D_MIX ** -0.5,
        "norm2_g": 1.0 + 0.02 * jax.random.normal(ks[8], (DEPTH, D_MODEL), f32),
        "w_mlp_in": jax.random.normal(ks[9], (DEPTH, D_MODEL, D_FF), f32) * D_MODEL ** -0.5,
        "w_mlp_out": jax.random.normal(ks[10], (DEPTH, D_FF, D_MODEL), f32) * D_FF ** -0.5,
        "final_norm_g": 1.0 + 0.02 * jax.random.normal(ks[11], (D_MODEL,), f32),
    }


def _fwd_reference(x, norm1_g, w_in, forget_b, rel_bias, outnorm_a_g, outnorm_b_g, w_out,
              norm2_g, w_mlp_in, w_mlp_out, final_norm_g):
    B, S, _ = x.shape
    for l in range(DEPTH):
        h = rms_norm(x, norm1_g[l])
        proj = h @ w_in[l]
        q_a, k_a, v_a, f_a, q_b, k_b, v_b = jnp.split(proj, SPLIT_POINTS, axis=-1)
        heads_a = lambda t: t.reshape(B, S, N_HEADS_FOX, HEAD_DIM)
        heads_b = lambda t: t.reshape(B, S, N_HEADS_DIL, HEAD_DIM)
        log_f = jax.nn.log_sigmoid(f_a.astype(jnp.float32) + forget_b[l].astype(jnp.float32))
        y_a = fox_attention(heads_a(q_a), heads_a(k_a), heads_a(v_a), log_f)
        y_a = y_a.reshape(B, S, D_FOX)
        y_b = dilated_attention(heads_b(q_b), heads_b(k_b), heads_b(v_b), rel_bias)
        y_b = y_b.reshape(B, S, D_DIL).astype(x.dtype)
        mixed = jnp.concatenate([rms_norm(y_a, outnorm_a_g[l]),
                                 rms_norm(y_b, outnorm_b_g[l])], axis=-1)
        x = x + mixed @ w_out[l]
        h = rms_norm(x, norm2_g[l])
        x = x + jnp.square(jax.nn.relu(h @ w_mlp_in[l])) @ w_mlp_out[l]
    return rms_norm(x, final_norm_g)


import jax as _jax
import jax.numpy as _jnp

TWIN_FORMAT = 'train_step'
FWD_PARAMS = ['x', 'norm1_g', 'w_in', 'forget_b', 'rel_bias', 'outnorm_a_g', 'outnorm_b_g', 'w_out', 'norm2_g', 'w_mlp_in', 'w_mlp_out', 'final_norm_g']
TWIN_WEIGHTS = ['norm1_g', 'w_in', 'forget_b', 'rel_bias', 'outnorm_a_g', 'outnorm_b_g', 'w_out', 'norm2_g', 'w_mlp_in', 'w_mlp_out', 'final_norm_g']
TWIN_DIFF_INPUT = 'x'
TWIN_INPUTS = ['x', 'norm1_g', 'w_in', 'forget_b', 'rel_bias', 'outnorm_a_g', 'outnorm_b_g', 'w_out', 'norm2_g', 'w_mlp_in', 'w_mlp_out', 'final_norm_g', 'loss_target', 'm_norm1_g', 'm_w_in', 'm_forget_b', 'm_rel_bias', 'm_outnorm_a_g', 'm_outnorm_b_g', 'm_w_out', 'm_norm2_g', 'm_w_mlp_in', 'm_w_mlp_out', 'm_final_norm_g', 'v_norm1_g', 'v_w_in', 'v_forget_b', 'v_rel_bias', 'v_outnorm_a_g', 'v_outnorm_b_g', 'v_w_out', 'v_norm2_g', 'v_w_mlp_in', 'v_w_mlp_out', 'v_final_norm_g']
TWIN_OUTPUTS = ['loss', 'grad_x', 'grad_norm1_g', 'grad_w_in', 'grad_forget_b', 'grad_rel_bias', 'grad_outnorm_a_g', 'grad_outnorm_b_g', 'grad_w_out', 'grad_norm2_g', 'grad_w_mlp_in', 'grad_w_mlp_out', 'grad_final_norm_g', 'delta_norm1_g', 'delta_w_in', 'delta_forget_b', 'delta_rel_bias', 'delta_outnorm_a_g', 'delta_outnorm_b_g', 'delta_w_out', 'delta_norm2_g', 'delta_w_mlp_in', 'delta_w_mlp_out', 'delta_final_norm_g', 'new_m_norm1_g', 'new_m_w_in', 'new_m_forget_b', 'new_m_rel_bias', 'new_m_outnorm_a_g', 'new_m_outnorm_b_g', 'new_m_w_out', 'new_m_norm2_g', 'new_m_w_mlp_in', 'new_m_w_mlp_out', 'new_m_final_norm_g', 'new_v_norm1_g', 'new_v_w_in', 'new_v_forget_b', 'new_v_rel_bias', 'new_v_outnorm_a_g', 'new_v_outnorm_b_g', 'new_v_w_out', 'new_v_norm2_g', 'new_v_w_mlp_in', 'new_v_w_mlp_out', 'new_v_final_norm_g']
TWIN_LEAF_KINDS = {'loss': 'loss', 'grad_x': 'grad_x', 'grad_norm1_g': 'grad_w', 'grad_w_in': 'grad_w', 'grad_forget_b': 'grad_w', 'grad_rel_bias': 'grad_w', 'grad_outnorm_a_g': 'grad_w', 'grad_outnorm_b_g': 'grad_w', 'grad_w_out': 'grad_w', 'grad_norm2_g': 'grad_w', 'grad_w_mlp_in': 'grad_w', 'grad_w_mlp_out': 'grad_w', 'grad_final_norm_g': 'grad_w', 'delta_norm1_g': 'delta_w', 'delta_w_in': 'delta_w', 'delta_forget_b': 'delta_w', 'delta_rel_bias': 'delta_w', 'delta_outnorm_a_g': 'delta_w', 'delta_outnorm_b_g': 'delta_w', 'delta_w_out': 'delta_w', 'delta_norm2_g': 'delta_w', 'delta_w_mlp_in': 'delta_w', 'delta_w_mlp_out': 'delta_w', 'delta_final_norm_g': 'delta_w', 'new_m_norm1_g': 'new_m', 'new_m_w_in': 'new_m', 'new_m_forget_b': 'new_m', 'new_m_rel_bias': 'new_m', 'new_m_outnorm_a_g': 'new_m', 'new_m_outnorm_b_g': 'new_m', 'new_m_w_out': 'new_m', 'new_m_norm2_g': 'new_m', 'new_m_w_mlp_in': 'new_m', 'new_m_w_mlp_out': 'new_m', 'new_m_final_norm_g': 'new_m', 'new_v_norm1_g': 'new_v', 'new_v_w_in': 'new_v', 'new_v_forget_b': 'new_v', 'new_v_rel_bias': 'new_v', 'new_v_outnorm_a_g': 'new_v', 'new_v_outnorm_b_g': 'new_v', 'new_v_w_out': 'new_v', 'new_v_norm2_g': 'new_v', 'new_v_w_mlp_in': 'new_v', 'new_v_w_mlp_out': 'new_v', 'new_v_final_norm_g': 'new_v'}


def _forward(args):
    return _fwd_reference(*[args[k] for k in FWD_PARAMS])


def _output_shape():
    def fwd():
        inp = _fwd_setup_inputs(0)
        return _fwd_reference(*[inp[k] for k in FWD_PARAMS])
    out = _jax.eval_shape(fwd)
    return out.shape, out.dtype

N_MICROBATCH = 1
ADAM_LR = 0.001
ADAM_B1 = 0.9
ADAM_B2 = 0.999
ADAM_EPS = 1e-08
ADAM_WD = 0.01
ADAM_STEP = 10
PER_EXAMPLE_BATCH_AXIS = {'x': 0, 'loss_target': 0}
SHARED_INPUTS = []
_WEIGHT_DTYPES = {'norm1_g': _jnp.float32, 'w_in': _jnp.float32, 'forget_b': _jnp.float32, 'rel_bias': _jnp.float32, 'outnorm_a_g': _jnp.float32, 'outnorm_b_g': _jnp.float32, 'w_out': _jnp.float32, 'norm2_g': _jnp.float32, 'w_mlp_in': _jnp.float32, 'w_mlp_out': _jnp.float32, 'final_norm_g': _jnp.float32}
MOMENT_SCALE = {'norm1_g': 9.370122e-02, 'w_in': 5.359586e-02, 'forget_b': 4.217834e-01, 'rel_bias': 1.294711e-01, 'outnorm_a_g': 6.573366e-02, 'outnorm_b_g': 7.705354e-02, 'w_out': 6.923966e-02, 'norm2_g': 6.177186e-02, 'w_mlp_in': 3.138835e-02, 'w_mlp_out': 8.249054e-02, 'final_norm_g': 1.647361e+01}


def _to_microbatches(a, axis):
    t = _jnp.moveaxis(a, axis, 0)
    t = t.reshape((N_MICROBATCH, t.shape[0] // N_MICROBATCH) + t.shape[1:])
    return _jnp.moveaxis(t, 1, axis + 1)


def setup_inputs(seed: int = 0) -> dict:
    inp = _fwd_setup_inputs(seed)
    key = _jax.random.fold_in(_jax.random.key(seed), 7919)
    shape, _ = _output_shape()
    out = dict(inp)
    out["loss_target"] = _jax.random.normal(_jax.random.fold_in(key, 0), shape, _jnp.float32)
    for i, name in enumerate(TWIN_WEIGHTS):
        w = inp[name].astype(_jnp.float32)
        if MOMENT_SCALE is None:
            s = _jnp.sqrt(_jnp.mean(_jnp.square(w)) + 1e-30)
        else:
            s = MOMENT_SCALE[name]
        km, kv = _jax.random.split(_jax.random.fold_in(key, i + 1))
        out[name] = w
        out["m_" + name] = s * _jax.random.normal(km, w.shape, _jnp.float32)
        out["v_" + name] = (s * s) * _jax.random.uniform(kv, w.shape, _jnp.float32, 0.5, 1.5)
    if N_MICROBATCH > 1:
        for name, axis in PER_EXAMPLE_BATCH_AXIS.items():
            out[name] = _to_microbatches(out[name], axis)
    return {'x': out['x'], 'norm1_g': out['norm1_g'], 'w_in': out['w_in'], 'forget_b': out['forget_b'], 'rel_bias': out['rel_bias'], 'outnorm_a_g': out['outnorm_a_g'], 'outnorm_b_g': out['outnorm_b_g'], 'w_out': out['w_out'], 'norm2_g': out['norm2_g'], 'w_mlp_in': out['w_mlp_in'], 'w_mlp_out': out['w_mlp_out'], 'final_norm_g': out['final_norm_g'], 'loss_target': out['loss_target'], 'm_norm1_g': out['m_norm1_g'], 'm_w_in': out['m_w_in'], 'm_forget_b': out['m_forget_b'], 'm_rel_bias': out['m_rel_bias'], 'm_outnorm_a_g': out['m_outnorm_a_g'], 'm_outnorm_b_g': out['m_outnorm_b_g'], 'm_w_out': out['m_w_out'], 'm_norm2_g': out['m_norm2_g'], 'm_w_mlp_in': out['m_w_mlp_in'], 'm_w_mlp_out': out['m_w_mlp_out'], 'm_final_norm_g': out['m_final_norm_g'], 'v_norm1_g': out['v_norm1_g'], 'v_w_in': out['v_w_in'], 'v_forget_b': out['v_forget_b'], 'v_rel_bias': out['v_rel_bias'], 'v_outnorm_a_g': out['v_outnorm_a_g'], 'v_outnorm_b_g': out['v_outnorm_b_g'], 'v_w_out': out['v_w_out'], 'v_norm2_g': out['v_norm2_g'], 'v_w_mlp_in': out['v_w_mlp_in'], 'v_w_mlp_out': out['v_w_mlp_out'], 'v_final_norm_g': out['v_final_norm_g']}


def _loss(weights, diff, rest, loss_target):
    with _jax.named_scope("forward"):
        args = {**rest, TWIN_DIFF_INPUT: diff, **{k: w.astype(_WEIGHT_DTYPES[k]) for k, w in weights.items()}}
        y = _forward(args)
    with _jax.named_scope("loss_head"):
        err = _jnp.square(y.astype(_jnp.float32) - loss_target)
        return 0.5 * _jnp.sum(_jnp.mean(err, axis=-1)) if err.ndim else 0.5 * err


def _adamw(w, g, m, v):
    m = ADAM_B1 * m + (1.0 - ADAM_B1) * g
    v = ADAM_B2 * v + (1.0 - ADAM_B2) * _jnp.square(g)
    m_hat = m / (1.0 - ADAM_B1 ** ADAM_STEP)
    v_hat = v / (1.0 - ADAM_B2 ** ADAM_STEP)
    delta = -ADAM_LR * (m_hat / (_jnp.sqrt(v_hat) + ADAM_EPS) + ADAM_WD * w)
    return delta, m, v


def reference(x, norm1_g, w_in, forget_b, rel_bias, outnorm_a_g, outnorm_b_g, w_out, norm2_g, w_mlp_in, w_mlp_out, final_norm_g, loss_target, m_norm1_g, m_w_in, m_forget_b, m_rel_bias, m_outnorm_a_g, m_outnorm_b_g, m_w_out, m_norm2_g, m_w_mlp_in, m_w_mlp_out, m_final_norm_g, v_norm1_g, v_w_in, v_forget_b, v_rel_bias, v_outnorm_a_g, v_outnorm_b_g, v_w_out, v_norm2_g, v_w_mlp_in, v_w_mlp_out, v_final_norm_g):
    given = dict(x=x, norm1_g=norm1_g, w_in=w_in, forget_b=forget_b, rel_bias=rel_bias, outnorm_a_g=outnorm_a_g, outnorm_b_g=outnorm_b_g, w_out=w_out, norm2_g=norm2_g, w_mlp_in=w_mlp_in, w_mlp_out=w_mlp_out, final_norm_g=final_norm_g, loss_target=loss_target, m_norm1_g=m_norm1_g, m_w_in=m_w_in, m_forget_b=m_forget_b, m_rel_bias=m_rel_bias, m_outnorm_a_g=m_outnorm_a_g, m_outnorm_b_g=m_outnorm_b_g, m_w_out=m_w_out, m_norm2_g=m_norm2_g, m_w_mlp_in=m_w_mlp_in, m_w_mlp_out=m_w_mlp_out, m_final_norm_g=m_final_norm_g, v_norm1_g=v_norm1_g, v_w_in=v_w_in, v_forget_b=v_forget_b, v_rel_bias=v_rel_bias, v_outnorm_a_g=v_outnorm_a_g, v_outnorm_b_g=v_outnorm_b_g, v_w_out=v_w_out, v_norm2_g=v_norm2_g, v_w_mlp_in=v_w_mlp_in, v_w_mlp_out=v_w_mlp_out, v_final_norm_g=v_final_norm_g)
    weights = {n: given[n] for n in TWIN_WEIGHTS}
    shared = {n: given[n] for n in SHARED_INPUTS}
    per_example = {n: given[n] for n in ['x']}
    grad_fn = _jax.value_and_grad(_loss, argnums=(0, 1))

    def one_microbatch(ex, loss_target):
        ex = dict(ex)
        diff = ex.pop(TWIN_DIFF_INPUT)
        return grad_fn(weights, diff, {**shared, **ex}, loss_target)

    if N_MICROBATCH == 1:
        loss, (grad_w, grad_x) = one_microbatch(per_example, given["loss_target"])
    else:
        def body(carry, xs):
            loss_sum, grad_sum = carry
            l_k, (gw_k, gx_k) = one_microbatch(xs[0], xs[1])
            with _jax.named_scope("update"):
                return (loss_sum + l_k, _jax.tree.map(_jnp.add, grad_sum, gw_k)), gx_k

        init = (_jnp.zeros((), _jnp.float32), _jax.tree.map(_jnp.zeros_like, weights))
        (loss, grad_w), grad_x = _jax.lax.scan(body, init, (per_example, given["loss_target"]))
    with _jax.named_scope("update"):
        delta_w, new_m, new_v = {}, {}, {}
        for n in TWIN_WEIGHTS:
            delta_w[n], new_m[n], new_v[n] = _adamw(weights[n], grad_w[n], given["m_" + n], given["v_" + n])
    return (loss, grad_x, *[grad_w[n] for n in TWIN_WEIGHTS], *[delta_w[n] for n in TWIN_WEIGHTS],
            *[new_m[n] for n in TWIN_WEIGHTS], *[new_v[n] for n in TWIN_WEIGHTS])
```

```python
import functools

import jax
import jax.numpy as jnp
from jax import lax
from jax.experimental import pallas as pl
from jax.experimental.pallas import tpu as pltpu

F32 = jnp.float32
CDT = jnp.bfloat16
HEAD_DIM = 128
NORM_EPS = 1e-6
NEG_INF = -1e30
REL_BUCKETS = 32
REL_MAX_DISTANCE = 2048
DIL_PATTERNS = ((128, 1), (512, 4), (2048, 16))
DIL_BLOCK = 128
ADAM_LR, ADAM_B1, ADAM_B2, ADAM_EPS, ADAM_WD, ADAM_STEP = 0.001, 0.9, 0.999, 1e-08, 0.01, 10
N_CHIPS = 4
N_DEV = 8
VMEM_LIMIT_BYTES = 56 * 1024 * 1024
SMALL_COLS = 1024
MESH = pl.DeviceIdType.MESH


def _cparams(sem=None):
    return pltpu.CompilerParams(dimension_semantics=sem, vmem_limit_bytes=VMEM_LIMIT_BYTES)


def _tile(dim, pref):
    t = min(pref, dim)
    t -= t % 128
    while t >= 128:
        if dim % t == 0:
            return t
        t -= 128
    return dim


def _rowwise(fn, ins, out_dtypes, name, bs=256, consts=()):
    R, C = ins[0].shape
    bs = min(bs, R)
    n_in, n_c = len(ins), len(consts)

    def body(*refs):
        vals = [r[...] for r in refs[:n_in + n_c]]
        res = fn(*vals)
        for o, r in zip(refs[n_in + n_c:], res):
            o[...] = r.astype(o.dtype)

    row = pl.BlockSpec((bs, C), lambda i: (i, 0))
    return pl.pallas_call(
        body, grid=(R // bs,),
        in_specs=[row] * n_in + [pl.BlockSpec((1, c.shape[-1]), lambda i: (0, 0)) for c in consts],
        out_specs=[row] * len(out_dtypes),
        out_shape=[jax.ShapeDtypeStruct((R, C), d) for d in out_dtypes],
        name=name, compiler_params=_cparams(("parallel",)),
    )(*ins, *[c.reshape(1, -1) for c in consts])


def _rms_fwd(x, g, name):
    def fn(xf, gg):
        r = lax.rsqrt(jnp.mean(xf * xf, axis=-1, keepdims=True) + NORM_EPS)
        return ((xf * r) * gg,)
    return _rowwise(fn, [x], [CDT], name, consts=[g])[0]


def _rms_bwd(x, g, dh, dres, name, bs=256):
    S, D = x.shape
    bs = min(bs, S)
    has_res = dres is not None

    def body(*refs):
        x_ref, g_ref, dh_ref = refs[:3]
        dx_ref, dxc_ref, dg_ref = refs[-3:]
        xf = x_ref[...]
        r = lax.rsqrt(jnp.mean(xf * xf, axis=-1, keepdims=True) + NORM_EPS)
        xhat = xf * r
        dh_ = dh_ref[...].astype(F32)
        dxhat = dh_ * g_ref[...]
        dx = r * (dxhat - xhat * jnp.mean(dxhat * xhat, axis=-1, keepdims=True))
        if has_res:
            dx = dx + refs[3][...]
        dx_ref[...] = dx
        dxc_ref[...] = dx.astype(dxc_ref.dtype)
        part = jnp.sum(dh_ * xhat, axis=0, keepdims=True)

        @pl.when(pl.program_id(0) == 0)
        def _():
            dg_ref[...] = part

        @pl.when(pl.program_id(0) > 0)
        def _():
            dg_ref[...] += part

    row = pl.BlockSpec((bs, D), lambda i: (i, 0))
    one = pl.BlockSpec((1, D), lambda i: (0, 0))
    ins = [x, g.reshape(1, D), dh] + ([dres] if has_res else [])
    return pl.pallas_call(
        body, grid=(S // bs,),
        in_specs=[row, one, row] + ([row] if has_res else []),
        out_specs=[row, row, one],
        out_shape=[jax.ShapeDtypeStruct((S, D), F32), jax.ShapeDtypeStruct((S, D), CDT),
                   jax.ShapeDtypeStruct((1, D), F32)],
        name=name, compiler_params=_cparams(("arbitrary",)),
    )(*ins)


def _loss_bwd(x, g, target, name, bs=256):
    S, D = x.shape
    bs = min(bs, S)

    def body(x_ref, g_ref, t_ref, dx_ref, dxc_ref, dg_ref, loss_ref):
        xf = x_ref[...]
        r = lax.rsqrt(jnp.mean(xf * xf, axis=-1, keepdims=True) + NORM_EPS)
        xhat = xf * r
        err = xhat * g_ref[...] - t_ref[...]
        lpart = 0.5 * jnp.sum(jnp.mean(err * err, axis=-1, keepdims=True), axis=0, keepdims=True)
        dy = err / D
        dxhat = dy * g_ref[...]
        dx = r * (dxhat - xhat * jnp.mean(dxhat * xhat, axis=-1, keepdims=True))
        dx_ref[...] = dx
        dxc_ref[...] = dx.astype(dxc_ref.dtype)
        gpart = jnp.sum(dy * xhat, axis=0, keepdims=True)

        @pl.when(pl.program_id(0) == 0)
        def _():
            dg_ref[...] = gpart
            loss_ref[...] = lpart

        @pl.when(pl.program_id(0) > 0)
        def _():
            dg_ref[...] += gpart
            loss_ref[...] += lpart

    row = pl.BlockSpec((bs, D), lambda i: (i, 0))
    one = pl.BlockSpec((1, D), lambda i: (0, 0))
    return pl.pallas_call(
        body, grid=(S // bs,),
        in_specs=[row, one, row],
        out_specs=[row, row, one, pl.BlockSpec((1, 1), lambda i: (0, 0))],
        out_shape=[jax.ShapeDtypeStruct((S, D), F32), jax.ShapeDtypeStruct((S, D), CDT),
                   jax.ShapeDtypeStruct((1, D), F32), jax.ShapeDtypeStruct((1, 1), F32)],
        name=name, compiler_params=_cparams(("arbitrary",)),
    )(x, g.reshape(1, D), target)


_NN = (((1,), (0,)), ((), ()))
_NT = (((1,), (1,)), ((), ()))
_TN = (((0,), (0,)), ((), ()))


def _mm(a, b, *, M, N, K, a_spec, b_spec, o_spec, dims, tm, tn, tk, name, out_shapes, extras=(), epi=None):
    nk = K // tk
    n_ex, n_out = len(extras), len(out_shapes)
    if epi is None:
        epi = lambda acc: (acc,)

    def body(*refs):
        a_ref, b_ref = refs[0], refs[1]
        ex = refs[2:2 + n_ex]
        outs = refs[2 + n_ex:2 + n_ex + n_out]
        part = lax.dot_general(a_ref[...], b_ref[...], dims, preferred_element_type=F32)

        def finish(acc):
            for o, r in zip(outs, epi(acc, *[e[...] for e in ex])):
                o[...] = r.astype(o.dtype)

        if nk == 1:
            finish(part)
        else:
            acc_ref = refs[-1]
            k = pl.program_id(2)

            @pl.when(k == 0)
            def _():
                acc_ref[...] = part

            @pl.when(k > 0)
            def _():
                acc_ref[...] += part

            @pl.when(k == nk - 1)
            def _():
                finish(acc_ref[...])

    ex_spec = pl.BlockSpec((tm, tn), lambda i, j, k: (i, j))
    return pl.pallas_call(
        body, grid=(M // tm, N // tn, nk),
        in_specs=[a_spec, b_spec] + [ex_spec] * n_ex,
        out_specs=[o_spec] * n_out,
        out_shape=out_shapes,
        scratch_shapes=[pltpu.VMEM((tm, tn), F32)] if nk > 1 else [],
        name=name, compiler_params=_cparams(("parallel", "parallel", "arbitrary")),
    )(a, b, *extras)


def _mm_nn(a, b, name, out_dtypes, extras=(), epi=None, b_slots=False, tm=1024, tn=512, tk=2048):
    M, K = a.shape
    if b_slots:
        ns, _, Ns = b.shape
        N = ns * Ns
        tn = _tile(Ns, tn)
        npb = Ns // tn
        tk_ = _tile(K, tk)
        b_spec = pl.BlockSpec((None, tk_, tn), lambda i, j, k: (j // npb, k, j % npb))
    else:
        N = b.shape[1]
        tn = _tile(N, tn)
        tk_ = _tile(K, tk)
        b_spec = pl.BlockSpec((tk_, tn), lambda i, j, k: (k, j))
    tm = _tile(M, tm)
    return _mm(a, b, M=M, N=N, K=K, a_spec=pl.BlockSpec((tm, tk_), lambda i, j, k: (i, k)), b_spec=b_spec,
               o_spec=pl.BlockSpec((tm, tn), lambda i, j, k: (i, j)), dims=_NN, tm=tm, tn=tn, tk=tk_, name=name,
               out_shapes=[jax.ShapeDtypeStruct((M, N), d) for d in out_dtypes], extras=extras, epi=epi)


def _mm_nt(a, b, name, out_dtypes, extras=(), epi=None, b_slots=False, tm=1024, tn=512, tk=2048):
    M, K = a.shape
    tm = _tile(M, tm)
    if b_slots:
        ns, N, Ks = b.shape
        tk_ = _tile(Ks, tk)
        kpb = Ks // tk_
        tn = _tile(N, tn)
        b_spec = pl.BlockSpec((None, tn, tk_), lambda i, j, k: (k // kpb, j, k % kpb))
    else:
        N = b.shape[0]
        tk_ = _tile(K, tk)
        tn = _tile(N, tn)
        b_spec = pl.BlockSpec((tn, tk_), lambda i, j, k: (j, k))
    return _mm(a, b, M=M, N=N, K=K, a_spec=pl.BlockSpec((tm, tk_), lambda i, j, k: (i, k)), b_spec=b_spec,
               o_spec=pl.BlockSpec((tm, tn), lambda i, j, k: (i, j)), dims=_NT, tm=tm, tn=tn, tk=tk_, name=name,
               out_shapes=[jax.ShapeDtypeStruct((M, N), d) for d in out_dtypes], extras=extras, epi=epi)


def _mm_tn(a, b, name, out_slots=0, tm=1024, tn=1024, tk=1024):
    K, M = a.shape
    N = b.shape[1]
    tm, tk_ = _tile(M, tm), _tile(K, tk)
    if out_slots:
        Ns = N // out_slots
        tn = _tile(Ns, tn)
        npb = Ns // tn
        o_spec = pl.BlockSpec((None, tm, tn), lambda i, j, k: (j // npb, i, j % npb))
        out_shape = jax.ShapeDtypeStruct((out_slots, M, Ns), F32)
    else:
        tn = _tile(N, tn)
        o_spec = pl.BlockSpec((tm, tn), lambda i, j, k: (i, j))
        out_shape = jax.ShapeDtypeStruct((M, N), F32)
    return _mm(a, b, M=M, N=N, K=K, a_spec=pl.BlockSpec((tk_, tm), lambda i, j, k: (k, i)),
               b_spec=pl.BlockSpec((tk_, tn), lambda i, j, k: (k, j)), o_spec=o_spec, dims=_TN,
               tm=tm, tn=tn, tk=tk_, name=name, out_shapes=[out_shape])[0]


GATE_BLOCK = 512


def _exact_dot(v, tri):
    hi = v.astype(jnp.bfloat16)
    r1 = v - hi.astype(F32)
    mid = r1.astype(jnp.bfloat16)
    lo = (r1 - mid.astype(F32)).astype(jnp.bfloat16)
    dot = lambda t: jnp.dot(t, tri, preferred_element_type=F32)
    return dot(hi) + dot(mid) + dot(lo)


def _gates_fwd(f_t, b, name):
    H, S = f_t.shape
    nb = _tile(S, GATE_BLOCK)

    def body(f_ref, b_ref, c_ref):
        upper = (lax.broadcasted_iota(jnp.int32, (nb, nb), 0)
                 <= lax.broadcasted_iota(jnp.int32, (nb, nb), 1)).astype(jnp.bfloat16)
        carry = jnp.zeros((H, 1), F32)
        for i in range(S // nb):
            z = f_ref[:, i * nb:(i + 1) * nb] + b_ref[...]
            logf = jnp.minimum(z, 0.0) - jnp.log1p(jnp.exp(-jnp.abs(z)))
            cs = _exact_dot(logf, upper) + carry
            c_ref[:, i * nb:(i + 1) * nb] = cs
            carry = cs[:, nb - 1:nb]

    return pl.pallas_call(body, out_shape=jax.ShapeDtypeStruct((H, S), F32), name=name,
                          compiler_params=_cparams())(f_t, b.reshape(H, 1))


def _gates_bwd(f_t, b, dcq, dck, name):
    H, S = f_t.shape
    nb = _tile(S, GATE_BLOCK)

    def body(f_ref, b_ref, dcq_ref, dck_ref, df_ref, dfc_ref, db_ref):
        lower = (lax.broadcasted_iota(jnp.int32, (nb, nb), 0)
                 >= lax.broadcasted_iota(jnp.int32, (nb, nb), 1)).astype(jnp.bfloat16)
        carry = jnp.zeros((H, 1), F32)
        db = jnp.zeros((H, 1), F32)
        for i in reversed(range(S // nb)):
            sl = slice(i * nb, (i + 1) * nb)
            dc = dcq_ref[:, sl] - dck_ref[:, sl]
            dlogf = _exact_dot(dc, lower) + carry
            carry = dlogf[:, 0:1]
            z = f_ref[:, sl] + b_ref[...]
            df = dlogf / (1.0 + jnp.exp(z))
            df_ref[:, sl] = df
            dfc_ref[:, sl] = df.astype(dfc_ref.dtype)
            db = db + jnp.sum(df, axis=1, keepdims=True)
        db_ref[...] = db

    return pl.pallas_call(
        body, out_shape=[jax.ShapeDtypeStruct((H, S), F32), jax.ShapeDtypeStruct((H, S), CDT),
                         jax.ShapeDtypeStruct((H, 1), F32)],
        name=name, compiler_params=_cparams())(f_t, b.reshape(H, 1), dcq, dck)


FOX_BLOCK = 512


def _fox_logits(q, k, cc, cr, qi, kj, blk, scale):
    s = lax.dot_general(q, k, _NT, preferred_element_type=F32) * scale
    s = s + (cc - cr)
    row = qi * blk + lax.broadcasted_iota(jnp.int32, (blk, blk), 0)
    col = kj * blk + lax.broadcasted_iota(jnp.int32, (blk, blk), 1)
    return jnp.where(col <= row, s, NEG_INF)


def _fox_fwd(proj, c_row, c_col, H, name):
    S = proj.shape[0]
    E = HEAD_DIM
    blk = _tile(S, FOX_BLOCK)
    nq = S // blk
    scale = E ** -0.5

    def body(q_ref, k_ref, v_ref, cr_ref, cc_ref, o_ref, lse_ref, m_s, l_s, acc_s):
        h, qi, kj = pl.program_id(0), pl.program_id(1), pl.program_id(2)

        @pl.when(kj == 0)
        def _():
            m_s[...] = jnp.full(m_s.shape, NEG_INF, F32)
            l_s[...] = jnp.zeros(l_s.shape, F32)
            acc_s[...] = jnp.zeros(acc_s.shape, F32)

        @pl.when(kj <= qi)
        def _():
            s = _fox_logits(q_ref[...], k_ref[...], cc_ref[:, 0:1], cr_ref[pl.ds(h, 1), :], qi, kj, blk, scale)
            m_prev = m_s[...]
            m_new = jnp.maximum(m_prev, jnp.max(s, axis=-1, keepdims=True))
            alpha = jnp.exp(m_prev - m_new)
            p = jnp.exp(s - m_new)
            l_s[...] = alpha * l_s[...] + jnp.sum(p, axis=-1, keepdims=True)
            acc_s[...] = alpha * acc_s[...] + jnp.dot(p.astype(CDT), v_ref[...], preferred_element_type=F32)
            m_s[...] = m_new

        @pl.when(kj == nq - 1)
        def _():
            o_ref[...] = acc_s[...] / l_s[...]
            lse_ref[...] = jnp.broadcast_to(m_s[...] + jnp.log(l_s[...]), lse_ref.shape)

    kmap = lambda off: (lambda h, i, j: (jnp.minimum(j, i), off + h))
    return pl.pallas_call(
        body, grid=(H, nq, nq),
        in_specs=[pl.BlockSpec((blk, E), lambda h, i, j: (i, h)),
                  pl.BlockSpec((blk, E), kmap(H)),
                  pl.BlockSpec((blk, E), kmap(2 * H)),
                  pl.BlockSpec((H, blk), lambda h, i, j: (0, jnp.minimum(j, i))),
                  pl.BlockSpec((blk, E), lambda h, i, j: (i, h))],
        out_specs=[pl.BlockSpec((blk, E), lambda h, i, j: (i, h))] * 2,
        out_shape=[jax.ShapeDtypeStruct((S, H * E), F32)] * 2,
        scratch_shapes=[pltpu.VMEM((blk, 1), F32), pltpu.VMEM((blk, 1), F32), pltpu.VMEM((blk, E), F32)],
        name=name, compiler_params=_cparams(("parallel", "parallel", "arbitrary")),
    )(proj, proj, proj, c_row, c_col)


def _fox_bwd_q(proj, c_row, c_col, lse, o, do, H, name):
    S = proj.shape[0]
    E = HEAD_DIM
    blk = _tile(S, FOX_BLOCK)
    nq = S // blk
    scale = E ** -0.5

    def body(q_ref, k_ref, v_ref, cr_ref, cc_ref, lse_ref, o_ref, do_ref, dq_ref, dc_ref, dq_s, dc_s):
        h, qi, kj = pl.program_id(0), pl.program_id(1), pl.program_id(2)

        @pl.when(kj == 0)
        def _():
            dq_s[...] = jnp.zeros(dq_s.shape, F32)
            dc_s[...] = jnp.zeros(dc_s.shape, F32)

        @pl.when(kj <= qi)
        def _():
            do = do_ref[...]
            delta = jnp.sum(do * o_ref[...], axis=-1, keepdims=True)
            s = _fox_logits(q_ref[...], k_ref[...], cc_ref[:, 0:1], cr_ref[pl.ds(h, 1), :], qi, kj, blk, scale)
            p = jnp.exp(s - lse_ref[:, 0:1])
            dp = lax.dot_general(do.astype(CDT), v_ref[...], _NT, preferred_element_type=F32)
            ds = p * (dp - delta)
            dc_s[...] += jnp.sum(ds, axis=-1, keepdims=True)
            dq_s[...] += jnp.dot((ds * scale).astype(CDT), k_ref[...], preferred_element_type=F32)

        @pl.when(kj == nq - 1)
        def _():
            dq_ref[...] = dq_s[...].astype(dq_ref.dtype)
            dc_ref[...] = jnp.broadcast_to(dc_s[...], dc_ref.shape)

    kmap = lambda off: (lambda h, i, j: (jnp.minimum(j, i), off + h))
    qspec = pl.BlockSpec((blk, E), lambda h, i, j: (i, h))
    return pl.pallas_call(
        body, grid=(H, nq, nq),
        in_specs=[qspec, pl.BlockSpec((blk, E), kmap(H)), pl.BlockSpec((blk, E), kmap(2 * H)),
                  pl.BlockSpec((H, blk), lambda h, i, j: (0, jnp.minimum(j, i))),
                  qspec, qspec, qspec, qspec],
        out_specs=[qspec, qspec],
        out_shape=[jax.ShapeDtypeStruct((S, H * E), CDT), jax.ShapeDtypeStruct((S, H * E), F32)],
        scratch_shapes=[pltpu.VMEM((blk, E), F32), pltpu.VMEM((blk, 1), F32)],
        name=name, compiler_params=_cparams(("parallel", "parallel", "arbitrary")),
    )(proj, proj, proj, c_row, c_col, lse, o, do)


def _fox_bwd_kv(proj, c_row, c_col, lse, o, do, H, name):
    S = proj.shape[0]
    E = HEAD_DIM
    blk = _tile(S, FOX_BLOCK)
    nq = S // blk
    scale = E ** -0.5

    def body(q_ref, k_ref, v_ref, cr_ref, cc_ref, lse_ref, o_ref, do_ref, dk_ref, dv_ref, dc_ref, dk_s, dv_s, dc_s):
        h, kj, qi = pl.program_id(0), pl.program_id(1), pl.program_id(2)

        @pl.when(qi == 0)
        def _():
            dk_s[...] = jnp.zeros(dk_s.shape, F32)
            dv_s[...] = jnp.zeros(dv_s.shape, F32)
            dc_s[...] = jnp.zeros(dc_s.shape, F32)

        @pl.when(qi >= kj)
        def _():
            do = do_ref[...]
            doc = do.astype(CDT)
            delta = jnp.sum(do * o_ref[...], axis=-1, keepdims=True)
            s = _fox_logits(q_ref[...], k_ref[...], cc_ref[:, 0:1], cr_ref[pl.ds(h, 1), :], qi, kj, blk, scale)
            p = jnp.exp(s - lse_ref[:, 0:1])
            dp = lax.dot_general(doc, v_ref[...], _NT, preferred_element_type=F32)
            ds = p * (dp - delta)
            dc_s[...] += jnp.sum(ds, axis=0, keepdims=True)
            dv_s[...] += jnp.dot(p.T.astype(CDT), doc, preferred_element_type=F32)
            dk_s[...] += jnp.dot((ds * scale).T.astype(CDT), q_ref[...], preferred_element_type=F32)

        @pl.when(qi == nq - 1)
        def _():
            dk_ref[...] = dk_s[...].astype(dk_ref.dtype)
            dv_ref[...] = dv_s[...].astype(dv_ref.dtype)
            dc_ref[...] = dc_s[...].reshape(dc_ref.shape)

    qspec = pl.BlockSpec((blk, E), lambda h, j, i: (jnp.maximum(i, j), h))
    kspec = lambda off: pl.BlockSpec((blk, E), lambda h, j, i: (j, off + h))
    ospec = pl.BlockSpec((blk, E), lambda h, j, i: (j, h))
    return pl.pallas_call(
        body, grid=(H, nq, nq),
        in_specs=[qspec, kspec(H), kspec(2 * H), pl.BlockSpec((H, blk), lambda h, j, i: (0, j)),
                  qspec, qspec, qspec, qspec],
        out_specs=[ospec, ospec, pl.BlockSpec((1, 1, blk), lambda h, j, i: (h, 0, j))],
        out_shape=[jax.ShapeDtypeStruct((S, H * E), CDT), jax.ShapeDtypeStruct((S, H * E), CDT),
                   jax.ShapeDtypeStruct((H, 1, S), F32)],
        scratch_shapes=[pltpu.VMEM((blk, E), F32), pltpu.VMEM((blk, E), F32), pltpu.VMEM((1, blk), F32)],
        name=name, compiler_params=_cparams(("parallel", "parallel", "arbitrary")),
    )(proj, proj, proj, c_row, c_col, lse, o, do)


DIL_ROWS = 512


def _rel_bucket(dist):
    max_exact = REL_BUCKETS // 2
    d = jnp.maximum(dist.astype(F32), 1.0)
    large = max_exact + (jnp.log(d / max_exact) / jnp.log(jnp.float32(REL_MAX_DISTANCE / max_exact))
                         * (REL_BUCKETS - max_exact)).astype(jnp.int32)
    large = jnp.minimum(large, REL_BUCKETS - 1)
    return jnp.where(dist < max_exact, dist, large)


def _bucket_table():
    i = jnp.arange(DIL_BLOCK)[:, None]
    j = jnp.arange(2 * DIL_BLOCK)[None, :]
    rel = DIL_BLOCK + i - j
    tabs = [_rel_bucket(jnp.clip(rel, 0, w // d) * d) for w, d in DIL_PATTERNS]
    return jnp.stack(tabs).astype(jnp.int32)


def _bias_table(rel_bias, buckets, name):
    P = buckets.shape[0]
    H = rel_bias.shape[1]

    def body(rb_ref, bk_ref, out_ref):
        h = pl.program_id(1)
        bk = bk_ref[0]
        val = jnp.zeros(bk.shape, F32)
        for b in range(REL_BUCKETS):
            val = jnp.where(bk == b, rb_ref[b, h], val)
        out_ref[0, 0] = val

    return pl.pallas_call(
        body, grid=(P, H),
        in_specs=[pl.BlockSpec(memory_space=pltpu.SMEM),
                  pl.BlockSpec((1, DIL_BLOCK, 2 * DIL_BLOCK), lambda p, h: (p, 0, 0))],
        out_specs=pl.BlockSpec((1, 1, DIL_BLOCK, 2 * DIL_BLOCK), lambda p, h: (p, h, 0, 0)),
        out_shape=jax.ShapeDtypeStruct((P, H, DIL_BLOCK, 2 * DIL_BLOCK), F32),
        name=name, compiler_params=_cparams(("parallel", "parallel")),
    )(rel_bias, buckets)


def _bias_table_bwd(dbias, buckets, name):
    P, H = dbias.shape[:2]

    def body(db_ref, bk_ref, out_ref):
        lane = lax.broadcasted_iota(jnp.int32, (1, REL_BUCKETS), 1)
        acc = jnp.zeros((1, REL_BUCKETS), F32)
        bk = bk_ref[...]
        db = db_ref[:, 0]
        for b in range(REL_BUCKETS):
            tot = jnp.sum(jnp.where(bk == b, db, 0.0))
            acc = jnp.where(lane == b, tot, acc)
        out_ref[0] = acc

    return pl.pallas_call(
        body, grid=(H,),
        in_specs=[pl.BlockSpec((P, 1, DIL_BLOCK, 2 * DIL_BLOCK), lambda h: (0, h, 0, 0)),
                  pl.BlockSpec((P, DIL_BLOCK, 2 * DIL_BLOCK), lambda h: (0, 0, 0))],
        out_specs=pl.BlockSpec((1, 1, REL_BUCKETS), lambda h: (h, 0, 0)),
        out_shape=jax.ShapeDtypeStruct((H, 1, REL_BUCKETS), F32),
        name=name, compiler_params=_cparams(("parallel",)),
    )(dbias, buckets)


def _dil_masks(T, first):
    B = DIL_BLOCK
    ii = lax.broadcasted_iota(jnp.int32, (T, B, B), 1)
    jj = lax.broadcasted_iota(jnp.int32, (T, B, B), 2)
    tt = lax.broadcasted_iota(jnp.int32, (T, B, B), 0)
    return jj <= ii, (jj >= ii) & ((first + tt) > 0)


def _bdot(a, b, contract_b):
    return lax.dot_general(a, b, (((2,), (contract_b,)), ((0,), (0,))), preferred_element_type=F32)


def _dil_geometry(S, d):
    n_sub = S // d
    rows = _tile(n_sub, DIL_ROWS)
    return n_sub, rows, rows // DIL_BLOCK, n_sub // rows


def _dil_fwd(proj, bias, d, H, col0, name):
    S, C = proj.shape
    E = B = DIL_BLOCK
    n_sub, rows, T, nc = _dil_geometry(S, d)
    cb = C // E
    scale = E ** -0.5

    def body(q_ref, k_ref, v_ref, kh_ref, vh_ref, b_ref, o_ref, lse_ref):
        n = pl.program_id(2)
        q = q_ref[...].reshape(T, B, E)
        kc = k_ref[...].reshape(T, B, E)
        vc = v_ref[...].reshape(T, B, E)
        kp = jnp.concatenate([kh_ref[...][None], kc[:T - 1]], axis=0) if T > 1 else kh_ref[...][None]
        vp = jnp.concatenate([vh_ref[...][None], vc[:T - 1]], axis=0) if T > 1 else vh_ref[...][None]
        mask_c, mask_p = _dil_masks(T, n * T)
        bias_pc = b_ref[0]
        s_c = jnp.where(mask_c, _bdot(q, kc, 2) * scale + bias_pc[:, B:][None], NEG_INF)
        s_p = jnp.where(mask_p, _bdot(q, kp, 2) * scale + bias_pc[:, :B][None], NEG_INF)
        m = jnp.maximum(jnp.max(s_c, axis=-1, keepdims=True), jnp.max(s_p, axis=-1, keepdims=True))
        p_c = jnp.exp(s_c - m)
        p_p = jnp.exp(s_p - m)
        ssum = jnp.sum(p_c, axis=-1, keepdims=True) + jnp.sum(p_p, axis=-1, keepdims=True)
        o = (_bdot(p_c.astype(CDT), vc, 1) + _bdot(p_p.astype(CDT), vp, 1)) / ssum
        o_ref[...] = o.reshape(rows, E)
        lse_ref[...] = jnp.broadcast_to(m + jnp.log(ssum), (T, B, E)).reshape(rows, E)

    view = proj.reshape(n_sub, d * C)
    cur = lambda off: pl.BlockSpec((rows, E), lambda h, r, n: (n, r * cb + off + h))
    halo = lambda off: pl.BlockSpec((B, E), lambda h, r, n: (jnp.maximum(n * T - 1, 0), r * cb + off + h))
    ospec = pl.BlockSpec((rows, E), lambda h, r, n: (n, r * H + h))
    o, lse = pl.pallas_call(
        body, grid=(H, d, nc),
        in_specs=[cur(col0), cur(col0 + H), cur(col0 + 2 * H), halo(col0 + H), halo(col0 + 2 * H),
                  pl.BlockSpec((1, B, 2 * B), lambda h, r, n: (h, 0, 0))],
        out_specs=[ospec, ospec],
        out_shape=[jax.ShapeDtypeStruct((n_sub, d * H * E), F32)] * 2,
        name=name, compiler_params=_cparams(("parallel", "parallel", "parallel")),
    )(view, view, view, view, view, bias)
    return o.reshape(S, H * E), lse.reshape(S, H * E)


def _dil_bwd(proj, bias, y, dy, lse, d, H, col0, name):
    S, C = proj.shape
    E = B = DIL_BLOCK
    n_sub, rows, T, nc = _dil_geometry(S, d)
    cb = C // E
    scale = E ** -0.5

    def body(q_ref, k_ref, v_ref, kh_ref, vh_ref, qa_ref, b_ref, y_ref, dy_ref, lse_ref, ya_ref, dya_ref, lsea_ref,
             dq_ref, dk_ref, dv_ref, db_ref):
        r, n = pl.program_id(1), pl.program_id(2)
        q = q_ref[...].reshape(T, B, E)
        kc = k_ref[...].reshape(T, B, E)
        vc = v_ref[...].reshape(T, B, E)
        kp = jnp.concatenate([kh_ref[...][None], kc[:T - 1]], axis=0) if T > 1 else kh_ref[...][None]
        vp = jnp.concatenate([vh_ref[...][None], vc[:T - 1]], axis=0) if T > 1 else vh_ref[...][None]
        dy_ = dy_ref[...].reshape(T, B, E)
        dyc = dy_.astype(CDT)
        delta = jnp.sum(dy_ * y_ref[...].reshape(T, B, E), axis=-1, keepdims=True)
        lse_ = lse_ref[...].reshape(T, B, E)[:, :, 0:1]
        mask_c, mask_p = _dil_masks(T, n * T)
        bias_pc = b_ref[0]
        s_c = jnp.where(mask_c, _bdot(q, kc, 2) * scale + bias_pc[:, B:][None], NEG_INF)
        s_p = jnp.where(mask_p, _bdot(q, kp, 2) * scale + bias_pc[:, :B][None], NEG_INF)
        p_c = jnp.exp(s_c - lse_)
        p_p = jnp.exp(s_p - lse_)
        ds_c = p_c * (_bdot(dyc, vc, 2) - delta)
        ds_p = p_p * (_bdot(dyc, vp, 2) - delta)
        dsc_s = (ds_c * scale).astype(CDT)
        dsp_s = (ds_p * scale).astype(CDT)
        dq_ref[...] = (_bdot(dsc_s, kc, 1) + _bdot(dsp_s, kp, 1)).reshape(rows, E)
        tr = lambda t: jnp.swapaxes(t, 1, 2)
        dk = _bdot(tr(ds_c * scale).astype(CDT), q, 1)
        dv = _bdot(tr(p_c).astype(CDT), dyc, 1)
        dk_prev = _bdot(tr(ds_p * scale).astype(CDT), q, 1)
        dv_prev = _bdot(tr(p_p).astype(CDT), dyc, 1)

        qa = qa_ref[...]
        dya = dya_ref[...]
        dyac = dya.astype(CDT)
        delta_a = jnp.sum(dya * ya_ref[...], axis=-1, keepdims=True)
        ii = lax.broadcasted_iota(jnp.int32, (B, B), 0)
        jj = lax.broadcasted_iota(jnp.int32, (B, B), 1)
        mask_a = (jj >= ii) & (n < nc - 1)
        s_a = lax.dot_general(qa, kc[T - 1], _NT, preferred_element_type=F32) * scale + bias_pc[:, :B]
        p_a = jnp.exp(jnp.where(mask_a, s_a, NEG_INF) - lsea_ref[:, 0:1])
        ds_a = p_a * (lax.dot_general(dyac, vc[T - 1], _NT, preferred_element_type=F32) - delta_a)
        dk_a = jnp.dot((ds_a * scale).T.astype(CDT), qa, preferred_element_type=F32)
        dv_a = jnp.dot(p_a.T.astype(CDT), dyac, preferred_element_type=F32)
        if T > 1:
            dk = dk + jnp.concatenate([dk_prev[1:], dk_a[None]], axis=0)
            dv = dv + jnp.concatenate([dv_prev[1:], dv_a[None]], axis=0)
        else:
            dk = dk + dk_a[None]
            dv = dv + dv_a[None]
        dk_ref[...] = dk.reshape(rows, E)
        dv_ref[...] = dv.reshape(rows, E)

        dbias = jnp.concatenate([jnp.sum(ds_p, axis=0), jnp.sum(ds_c, axis=0)], axis=1)

        @pl.when((r == 0) & (n == 0))
        def _():
            db_ref[0] = dbias

        @pl.when((r > 0) | (n > 0))
        def _():
            db_ref[0] += dbias

    view = proj.reshape(n_sub, d * C)
    act = lambda t: t.reshape(n_sub, d * H * E)
    last = n_sub // B - 1
    cur = lambda off: pl.BlockSpec((rows, E), lambda h, r, n: (n, r * cb + off + h))
    before = lambda off: pl.BlockSpec((B, E), lambda h, r, n: (jnp.maximum(n * T - 1, 0), r * cb + off + h))
    after_q = pl.BlockSpec((B, E), lambda h, r, n: (jnp.minimum((n + 1) * T, last), r * cb + col0 + h))
    acur = pl.BlockSpec((rows, E), lambda h, r, n: (n, r * H + h))
    aafter = pl.BlockSpec((B, E), lambda h, r, n: (jnp.minimum((n + 1) * T, last), r * H + h))
    dq, dk, dv, db = pl.pallas_call(
        body, grid=(H, d, nc),
        in_specs=[cur(col0), cur(col0 + H), cur(col0 + 2 * H), before(col0 + H), before(col0 + 2 * H), after_q,
                  pl.BlockSpec((1, B, 2 * B), lambda h, r, n: (h, 0, 0)),
                  acur, acur, acur, aafter, aafter, aafter],
        out_specs=[acur, acur, acur, pl.BlockSpec((1, B, 2 * B), lambda h, r, n: (h, 0, 0))],
        out_shape=[jax.ShapeDtypeStruct((n_sub, d * H * E), F32)] * 3
        + [jax.ShapeDtypeStruct((H, B, 2 * B), F32)],
        name=name, compiler_params=_cparams(("parallel", "arbitrary", "arbitrary")),
    )(view, view, view, view, view, view, bias, act(y), act(dy), act(lse), act(y), act(dy), act(lse))
    return dq.reshape(S, H * E), dk.reshape(S, H * E), dv.reshape(S, H * E), db


def _dil_combine(outs, lses, name):
    n = len(outs)

    def fn(*t):
        o, l = t[:n], t[n:]
        m = functools.reduce(jnp.maximum, l)
        w = [jnp.exp(li - m) for li in l]
        tot = functools.reduce(jnp.add, w)
        y = functools.reduce(jnp.add, [(wi / tot) * oi for wi, oi in zip(w, o)])
        return y, m + jnp.log(tot)

    return _rowwise(fn, list(outs) + list(lses), [F32, F32], name)


def _sum_cast(parts, name):
    return _rowwise(lambda *t: (functools.reduce(jnp.add, t),), list(parts), [CDT], name)[0]


def _adamw(w, g, m, v, name, br=128):
    R, C = w.shape
    br = br if R % br == 0 else R

    def body(w_ref, g_ref, m_ref, v_ref, d_ref, nm_ref, nv_ref):
        g_ = g_ref[...]
        m_ = ADAM_B1 * m_ref[...] + (1.0 - ADAM_B1) * g_
        v_ = ADAM_B2 * v_ref[...] + (1.0 - ADAM_B2) * jnp.square(g_)
        m_hat = m_ / (1.0 - ADAM_B1 ** ADAM_STEP)
        v_hat = v_ / (1.0 - ADAM_B2 ** ADAM_STEP)
        d_ref[...] = -ADAM_LR * (m_hat / (jnp.sqrt(v_hat) + ADAM_EPS) + ADAM_WD * w_ref[...])
        nm_ref[...] = m_
        nv_ref[...] = v_

    blk = pl.BlockSpec((br, C), lambda i: (i, 0))
    return pl.pallas_call(
        body, grid=(R // br,), in_specs=[blk] * 4, out_specs=[blk] * 3,
        out_shape=[jax.ShapeDtypeStruct((R, C), F32)] * 3,
        name=name, compiler_params=_cparams(("parallel",)),
    )(w, g, m, v)


_HBM = pl.BlockSpec(memory_space=pltpu.HBM)


def _place():
    x, y, c = lax.axis_index("x"), lax.axis_index("y"), lax.axis_index("c")
    chips = [(1 - x, y), (x, 1 - y), (1 - x, 1 - y)]
    return x, y, c, chips


def _all_gather_chips(w, name):
    R, C = w.shape
    Rh = R // 2

    def body(w_ref, out_ref, send_sems, recv_sems, local_sem):
        x, y, c, chips = _place()
        slot = 2 * x + y

        def half(s, hf):
            return out_ref.at[s, pl.ds(hf * Rh, Rh), :]

        def copy(k, src, dst, to):
            return pltpu.make_async_remote_copy(src_ref=src, dst_ref=dst, send_sem=send_sems.at[k],
                                                recv_sem=recv_sems.at[k], device_id=to, device_id_type=MESH)

        mine = pltpu.make_async_copy(w_ref, out_ref.at[slot], local_sem)
        mine.start()
        first = [copy(j, w_ref.at[pl.ds(c * Rh, Rh), :], half(slot, c), (cx, cy, c)) for j, (cx, cy) in enumerate(chips)]
        for cp in first:
            cp.start()
        passed = []
        for j, (cx, cy) in enumerate(chips):
            s = 2 * cx + cy
            copy(j, half(s, c), half(s, c), (cx, cy, c)).wait_recv()
            fwd = copy(3 + j, half(s, c), half(s, c), (x, y, 1 - c))
            fwd.start()
            passed.append(fwd)
        for j, (cx, cy) in enumerate(chips):
            s = 2 * cx + cy
            copy(3 + j, half(s, 1 - c), half(s, 1 - c), (x, y, 1 - c)).wait_recv()
        for cp in first + passed:
            cp.wait_send()
        mine.wait()

    return pl.pallas_call(
        body, in_specs=[_HBM], out_specs=_HBM,
        out_shape=jax.ShapeDtypeStruct((N_CHIPS, R, C), w.dtype),
        scratch_shapes=[pltpu.SemaphoreType.DMA((6,)), pltpu.SemaphoreType.DMA((6,)), pltpu.SemaphoreType.DMA],
        name=name, compiler_params=pltpu.CompilerParams(),
    )(w)


def _sibling_halves(g, name):
    ns, R, C = g.shape
    Rh = R // 2

    def body(g_ref, out_ref, send_sem, recv_sem):
        x, y, c, _ = _place()
        cp = pltpu.make_async_remote_copy(src_ref=g_ref.at[:, pl.ds((1 - c) * Rh, Rh), :], dst_ref=out_ref,
                                          send_sem=send_sem, recv_sem=recv_sem, device_id=(x, y, 1 - c),
                                          device_id_type=MESH)
        cp.start()
        cp.wait()

    return pl.pallas_call(
        body, in_specs=[_HBM], out_specs=_HBM, out_shape=jax.ShapeDtypeStruct((ns, Rh, C), g.dtype),
        scratch_shapes=[pltpu.SemaphoreType.DMA, pltpu.SemaphoreType.DMA],
        name=name, compiler_params=pltpu.CompilerParams(),
    )(g)


def _chip_sum(g, other, core, name, br=256):
    ns, R, C = g.shape
    Rh = R // 2
    br = _tile(Rh, br) if Rh % 128 == 0 else Rh
    nb = Rh // br

    def body(c_ref, g_ref, o_ref, out_ref):
        out_ref[...] = (g_ref[...] + o_ref[...]).astype(out_ref.dtype)

    return pl.pallas_call(
        body,
        grid_spec=pltpu.PrefetchScalarGridSpec(
            num_scalar_prefetch=1, grid=(ns, nb),
            in_specs=[pl.BlockSpec((None, br, C), lambda s, i, c_ref: (s, c_ref[0] * nb + i, 0)),
                      pl.BlockSpec((None, br, C), lambda s, i, c_ref: (s, i, 0))],
            out_specs=pl.BlockSpec((None, br, C), lambda s, i, c_ref: (s, i, 0))),
        out_shape=jax.ShapeDtypeStruct((ns, Rh, C), CDT),
        name=name, compiler_params=_cparams(("parallel", "parallel")),
    )(core, g, other)


def _scatter_chips(q, name):
    ns, Rh, C = q.shape

    def body(q_ref, out_ref, send_sems, recv_sems, local_sem):
        x, y, c, chips = _place()
        slot = 2 * x + y
        mine = pltpu.make_async_copy(q_ref.at[slot], out_ref.at[slot], local_sem)
        mine.start()
        sends = []
        for j, (cx, cy) in enumerate(chips):
            cp = pltpu.make_async_remote_copy(src_ref=q_ref.at[2 * cx + cy], dst_ref=out_ref.at[slot],
                                              send_sem=send_sems.at[j], recv_sem=recv_sems.at[j],
                                              device_id=(cx, cy, c), device_id_type=MESH)
            cp.start()
            sends.append(cp)
        for j, (cx, cy) in enumerate(chips):
            pltpu.make_async_remote_copy(src_ref=q_ref.at[slot], dst_ref=out_ref.at[2 * cx + cy],
                                         send_sem=send_sems.at[j], recv_sem=recv_sems.at[j],
                                         device_id=(cx, cy, c), device_id_type=MESH).wait_recv()
        for cp in sends:
            cp.wait_send()
        mine.wait()

    return pl.pallas_call(
        body, in_specs=[_HBM], out_specs=_HBM, out_shape=jax.ShapeDtypeStruct((ns, Rh, C), q.dtype),
        scratch_shapes=[pltpu.SemaphoreType.DMA((3,)), pltpu.SemaphoreType.DMA((3,)), pltpu.SemaphoreType.DMA],
        name=name, compiler_params=pltpu.CompilerParams(),
    )(q)


def _slot_sum(b, name, br=256):
    ns, Rh, C = b.shape
    br = _tile(Rh, br) if Rh % 128 == 0 else Rh

    def body(b_ref, out_ref):
        acc = b_ref[0].astype(F32)
        for s in range(1, ns):
            acc = acc + b_ref[s].astype(F32)
        out_ref[...] = acc

    return pl.pallas_call(
        body, grid=(Rh // br,), in_specs=[pl.BlockSpec((ns, br, C), lambda i: (0, i, 0))],
        out_specs=pl.BlockSpec((br, C), lambda i: (i, 0)), out_shape=jax.ShapeDtypeStruct((Rh, C), F32),
        name=name, compiler_params=_cparams(("parallel",)),
    )(b)


def _join_halves(f, layer, n_layers, prev, name):
    Rh, C = f.shape

    def body(*refs):
        f_ref, out_ref = refs[0], refs[-4]
        send_sem, recv_sem, local_sem = refs[-3:]
        x, y, c, _ = _place()
        dst = out_ref.at[layer, pl.ds(c * Rh, Rh), :]
        mine = pltpu.make_async_copy(f_ref, dst, local_sem)
        mine.start()
        cp = pltpu.make_async_remote_copy(src_ref=f_ref, dst_ref=dst, send_sem=send_sem, recv_sem=recv_sem,
                                          device_id=(x, y, 1 - c), device_id_type=MESH)
        cp.start()
        other = out_ref.at[layer, pl.ds((1 - c) * Rh, Rh), :]
        pltpu.make_async_remote_copy(src_ref=f_ref, dst_ref=other, send_sem=send_sem, recv_sem=recv_sem,
                                     device_id=(x, y, 1 - c), device_id_type=MESH).wait_recv()
        cp.wait_send()
        mine.wait()

    ins = [f] + ([prev] if prev is not None else [])
    return pl.pallas_call(
        body, in_specs=[_HBM] * len(ins), out_specs=_HBM,
        out_shape=jax.ShapeDtypeStruct((n_layers, 2 * Rh, C), f.dtype),
        input_output_aliases={1: 0} if prev is not None else {},
        scratch_shapes=[pltpu.SemaphoreType.DMA, pltpu.SemaphoreType.DMA, pltpu.SemaphoreType.DMA],
        name=name, compiler_params=pltpu.CompilerParams(),
    )(*ins)


def _all_reduce_small(v, name):
    rows, cols = v.shape

    def body(v_ref, out_ref, buf, send_sems, recv_sems):
        x, y, c, _ = _place()
        me = 4 * x + 2 * y + c
        buf[me] = v_ref[...]
        peers = []
        for k in range(1, N_DEV):
            px, py, pc = (x + (k >> 2)) % 2, (y + ((k >> 1) & 1)) % 2, (c + (k & 1)) % 2
            peers.append((px, py, pc))
        sends = []
        for k, peer in enumerate(peers):
            cp = pltpu.make_async_remote_copy(src_ref=v_ref, dst_ref=buf.at[me], send_sem=send_sems.at[k],
                                              recv_sem=recv_sems.at[k], device_id=peer, device_id_type=MESH)
            cp.start()
            sends.append(cp)
        for k, (px, py, pc) in enumerate(peers):
            pltpu.make_async_remote_copy(src_ref=v_ref, dst_ref=buf.at[4 * px + 2 * py + pc], send_sem=send_sems.at[k],
                                         recv_sem=recv_sems.at[k], device_id=(px, py, pc),
                                         device_id_type=MESH).wait_recv()
        for cp in sends:
            cp.wait_send()
        acc = buf[0]
        for i in range(1, N_DEV):
            acc = acc + buf[i]
        out_ref[...] = acc

    vmem = pl.BlockSpec(memory_space=pltpu.VMEM)
    return pl.pallas_call(
        body, in_specs=[vmem], out_specs=vmem, out_shape=jax.ShapeDtypeStruct((rows, cols), F32),
        scratch_shapes=[pltpu.VMEM((N_DEV, rows, cols), F32), pltpu.SemaphoreType.DMA((N_DEV - 1,)),
                        pltpu.SemaphoreType.DMA((N_DEV - 1,))],
        name=name, compiler_params=pltpu.CompilerParams(),
    )(v)


def _reduce_scatter(g, core, layer, n_layers, prev, tag):
    other = _sibling_halves(g, f"rs_sibling_{tag}")
    q = _chip_sum(g, other, core, f"rs_chipsum_{tag}")
    b = _scatter_chips(q, f"rs_scatter_{tag}")
    f = _slot_sum(b, f"rs_slotsum_{tag}")
    return _join_halves(f, layer, n_layers, prev, f"rs_join_{tag}")


def _split_w_in(wg, Hf):
    ns, D, cols = wg.shape
    nat = wg.transpose(1, 0, 2).reshape(D, ns * cols)
    a = 3 * Hf * HEAD_DIM
    return jnp.concatenate([nat[:, :a], nat[:, a + Hf:]], axis=1), nat[:, a:a + Hf].T


def _join_dw_in(dw6, dwf_t, Hf):
    D = dw6.shape[0]
    a = 3 * Hf * HEAD_DIM
    nat = jnp.concatenate([dw6[:, :a], dwf_t.T, dw6[:, a:]], axis=1)
    return nat.reshape(D, N_CHIPS, nat.shape[1] // N_CHIPS).transpose(1, 0, 2)


def _lane_broadcast(t):
    return jnp.repeat(t.T, HEAD_DIM, axis=1)


def _layer_fwd(x, p, bias, tag):
    Hf, Hd = p["forget_b"].shape[0], bias.shape[1]
    h1 = _rms_fwd(x, p["norm1_g"], f"norm1_{tag}")
    proj = _mm_nn(h1, p["w6"], f"proj_{tag}", [CDT])[0]
    f_t = _mm_nt(p["wf_t"], h1, f"fproj_{tag}", [F32])[0]
    c_row = _gates_fwd(f_t, p["forget_b"], f"gates_{tag}")
    c_col = _lane_broadcast(c_row)
    y_a, lse_a = _fox_fwd(proj, c_row, c_col, Hf, f"fox_{tag}")
    outs, lses = [], []
    for i, (_, d) in enumerate(DIL_PATTERNS):
        o, l = _dil_fwd(proj, bias[i], d, Hd, 3 * Hf, f"dil{i}_{tag}")
        outs.append(o)
        lses.append(l)
    y_b, lse_b = _dil_combine(outs, lses, f"dilmix_{tag}")
    mixed = jnp.concatenate([_rms_fwd(y_a, p["outnorm_a_g"], f"norm_a_{tag}"),
                             _rms_fwd(y_b, p["outnorm_b_g"], f"norm_b_{tag}")], axis=1)
    x1 = _mm_nn(mixed, p["w_out"], f"attn_out_{tag}", [F32], extras=[x], epi=lambda acc, r: (r + acc,))[0]
    h2 = _rms_fwd(x1, p["norm2_g"], f"norm2_{tag}")
    u, act = _mm_nn(h2, p["w_mi"], f"mlp_in_{tag}", [F32, CDT], b_slots=True,
                    epi=lambda acc: (acc, jnp.square(jnp.maximum(acc, 0.0))))
    x2 = _mm_nn(act, p["w_mo"], f"mlp_out_{tag}", [F32], extras=[x1], epi=lambda acc, r: (r + acc,))[0]
    saved = dict(x=x, h1=h1, proj=proj, f_t=f_t, c_row=c_row, c_col=c_col, y_a=y_a, lse_a=lse_a, y_b=y_b,
                 lse_b=lse_b, mixed=mixed, x1=x1, h2=h2, u=u, act=act)
    return x2, saved


def _layer_bwd(dx2, dx2c, p, bias, sv, tag):
    Hf, Hd = p["forget_b"].shape[0], bias.shape[1]
    E = HEAD_DIM
    du = _mm_nt(dx2c, p["w_mo"], f"d_act_{tag}", [CDT], extras=[sv["u"]],
                epi=lambda acc, u: (acc * (2.0 * jnp.maximum(u, 0.0)),))[0]
    g_w_mo = _mm_tn(sv["act"], dx2c, f"dw_mlp_out_{tag}")
    dh2 = _mm_nt(du, p["w_mi"], f"d_h2_{tag}", [F32], b_slots=True)[0]
    g_w_mi = _mm_tn(sv["h2"], du, f"dw_mlp_in_{tag}", out_slots=N_CHIPS)
    dx1, dx1c, g_norm2 = _rms_bwd(sv["x1"], p["norm2_g"], dh2, dx2, f"d_norm2_{tag}")
    dmixed = _mm_nt(dx1c, p["w_out"], f"d_mixed_{tag}", [F32])[0]
    g_w_out = _mm_tn(sv["mixed"], dx1c, f"dw_out_{tag}")
    Da = Hf * E
    dy_a, _, g_na = _rms_bwd(sv["y_a"], p["outnorm_a_g"], dmixed[:, :Da], None, f"d_norm_a_{tag}")
    dy_b, _, g_nb = _rms_bwd(sv["y_b"], p["outnorm_b_g"], dmixed[:, Da:], None, f"d_norm_b_{tag}")
    proj = sv["proj"]
    dq_a, dcq = _fox_bwd_q(proj, sv["c_row"], sv["c_col"], sv["lse_a"], sv["y_a"], dy_a, Hf, f"fox_dq_{tag}")
    dk_a, dv_a, dck = _fox_bwd_kv(proj, sv["c_row"], sv["c_col"], sv["lse_a"], sv["y_a"], dy_a, Hf, f"fox_dkv_{tag}")
    df, dfc, g_fb = _gates_bwd(sv["f_t"], p["forget_b"], dcq[:, ::E].T, dck.reshape(Hf, -1), f"d_gates_{tag}")
    parts, dbias = [], []
    for i, (_, d) in enumerate(DIL_PATTERNS):
        dq, dk, dv, db = _dil_bwd(proj, bias[i], sv["y_b"], dy_b, sv["lse_b"], d, Hd, 3 * Hf, f"dil{i}_bwd_{tag}")
        parts.append((dq, dk, dv))
        dbias.append(db)
    dqkv_b = [_sum_cast([parts[i][j] for i in range(len(DIL_PATTERNS))], f"dil_sum{j}_{tag}") for j in range(3)]
    dproj = jnp.concatenate([dq_a, dk_a, dv_a] + dqkv_b, axis=1)
    dh1_f = _mm_tn(dfc, p["wf_t"], f"d_h1_f_{tag}")
    dh1 = _mm_nt(dproj, p["w6"], f"d_h1_{tag}", [F32], extras=[dh1_f], epi=lambda acc, r: (acc + r,))[0]
    g_w6 = _mm_tn(sv["h1"], dproj, f"dw_in_{tag}")
    g_wf_t = _mm_nn(dfc, sv["h1"], f"dw_f_{tag}", [F32])[0]
    dx, dxc, g_norm1 = _rms_bwd(sv["x"], p["norm1_g"], dh1, dx1, f"d_norm1_{tag}")
    grads = dict(w_in=_join_dw_in(g_w6, g_wf_t, Hf), w_out=g_w_out.reshape(N_CHIPS, -1, g_w_out.shape[1]),
                 w_mlp_in=g_w_mi, w_mlp_out=g_w_mo.reshape(N_CHIPS, -1, g_w_mo.shape[1]),
                 norm1_g=g_norm1[0], norm2_g=g_norm2[0], outnorm_a_g=g_na[0], outnorm_b_g=g_nb[0],
                 forget_b=g_fb[:, 0], dbias=jnp.stack(dbias))
    return dx, dxc, grads


def _local_step(x, target, small, gathered):
    depth = len(gathered)
    Hf = small["forget_b"].shape[1]
    buckets = _bucket_table()
    bias = _bias_table(small["rel_bias"], buckets, "bias_table")
    layers, saved = [], []
    for l in range(depth):
        g = gathered[l]
        w6, wf_t = _split_w_in(g["w_in"], Hf)
        p = dict(w6=w6, wf_t=wf_t, w_out=g["w_out"].reshape(-1, g["w_out"].shape[2]), w_mi=g["w_mlp_in"],
                 w_mo=g["w_mlp_out"].reshape(-1, g["w_mlp_out"].shape[2]),
                 **{k: small[k][l] for k in ("norm1_g", "forget_b", "outnorm_a_g", "outnorm_b_g", "norm2_g")})
        layers.append(p)
        x, sv = _layer_fwd(x, p, bias, f"l{l}")
        saved.append(sv)
    dx, dxc, g_final, loss = _loss_bwd(x, small["final_norm_g"], target, "loss")
    layer_grads = [None] * depth
    for l in reversed(range(depth)):
        dx, dxc, layer_grads[l] = _layer_bwd(dx, dxc, layers[l], bias, saved[l], f"l{l}")
    dbias = functools.reduce(jnp.add, [g["dbias"] for g in layer_grads])
    g_rel = _bias_table_bwd(dbias, buckets, "d_bias_table")[:, 0, :].T
    small_grads = dict(final_norm_g=g_final[0], rel_bias=g_rel,
                       **{k: jnp.stack([g[k] for g in layer_grads])
                          for k in ("norm1_g", "forget_b", "outnorm_a_g", "outnorm_b_g", "norm2_g")})
    return loss[0, 0], dx, layer_grads, small_grads


_BIG = ("w_in", "w_out", "w_mlp_in", "w_mlp_out")
_SMALL = ("norm1_g", "forget_b", "rel_bias", "outnorm_a_g", "outnorm_b_g", "norm2_g", "final_norm_g")
_ORDER = ("norm1_g", "w_in", "forget_b", "rel_bias", "outnorm_a_g", "outnorm_b_g", "w_out", "norm2_g", "w_mlp_in",
          "w_mlp_out", "final_norm_g")


def _pack_small(d):
    flat = jnp.concatenate([d[k].reshape(-1) for k in _SMALL])
    rows = -(-flat.shape[0] // (8 * SMALL_COLS)) * 8
    return jnp.pad(flat, (0, rows * SMALL_COLS - flat.shape[0])).reshape(rows, SMALL_COLS)


def _unpack_small(packed, like):
    flat, out, at = packed.reshape(-1), {}, 0
    for k in _SMALL:
        n = like[k].size
        out[k] = flat[at:at + n].reshape(like[k].shape)
        at += n
    return out


def kernel(x, norm1_g, w_in, forget_b, rel_bias, outnorm_a_g, outnorm_b_g, w_out, norm2_g, w_mlp_in, w_mlp_out, final_norm_g, loss_target, m_norm1_g, m_w_in, m_forget_b, m_rel_bias, m_outnorm_a_g, m_outnorm_b_g, m_w_out, m_norm2_g, m_w_mlp_in, m_w_mlp_out, m_final_norm_g, v_norm1_g, v_w_in, v_forget_b, v_rel_bias, v_outnorm_a_g, v_outnorm_b_g, v_w_out, v_norm2_g, v_w_mlp_in, v_w_mlp_out, v_final_norm_g):
    w = dict(norm1_g=norm1_g, w_in=w_in, forget_b=forget_b, rel_bias=rel_bias, outnorm_a_g=outnorm_a_g,
             outnorm_b_g=outnorm_b_g, w_out=w_out, norm2_g=norm2_g, w_mlp_in=w_mlp_in, w_mlp_out=w_mlp_out,
             final_norm_g=final_norm_g)
    m = dict(norm1_g=m_norm1_g, w_in=m_w_in, forget_b=m_forget_b, rel_bias=m_rel_bias, outnorm_a_g=m_outnorm_a_g,
             outnorm_b_g=m_outnorm_b_g, w_out=m_w_out, norm2_g=m_norm2_g, w_mlp_in=m_w_mlp_in,
             w_mlp_out=m_w_mlp_out, final_norm_g=m_final_norm_g)
    v = dict(norm1_g=v_norm1_g, w_in=v_w_in, forget_b=v_forget_b, rel_bias=v_rel_bias, outnorm_a_g=v_outnorm_a_g,
             outnorm_b_g=v_outnorm_b_g, w_out=v_w_out, norm2_g=v_norm2_g, w_mlp_in=v_w_mlp_in,
             w_mlp_out=v_w_mlp_out, final_norm_g=v_final_norm_g)
    depth = w_in.shape[0]
    small = {k: w[k] for k in _SMALL}

    gathered = [{k: _all_gather_chips(w[k][l].astype(CDT), f"gather_{k}_l{l}") for k in _BIG} for l in range(depth)]
    loss, grad_x, layer_grads, small_grads = _local_step(x[0], loss_target[0], small, gathered)
    loss = lax.psum(loss, ("x", "y", "c"))

    core = lax.axis_index("c").astype(jnp.int32).reshape(1)
    grads = {}
    for k in _BIG:
        acc = None
        for l in reversed(range(depth)):
            acc = _reduce_scatter(layer_grads[l][k], core, l, depth, acc, f"{k}_l{l}")
        grads[k] = acc
    grads.update(_unpack_small(_all_reduce_small(_pack_small(small_grads), "small_all_reduce"), small))

    delta, new_m, new_v = {}, {}, {}
    for k in _BIG:
        shape = w[k].shape
        flat = lambda t: t.reshape(-1, shape[-1])
        d_, m_, v_ = _adamw(flat(w[k]), flat(grads[k]), flat(m[k]), flat(v[k]), f"adamw_{k}")
        delta[k], new_m[k], new_v[k] = d_.reshape(shape), m_.reshape(shape), v_.reshape(shape)
    d_, m_, v_ = _adamw(_pack_small(small), _pack_small({k: grads[k] for k in _SMALL}),
                        _pack_small({k: m[k] for k in _SMALL}), _pack_small({k: v[k] for k in _SMALL}), "adamw_small")
    delta.update(_unpack_small(d_, small))
    new_m.update(_unpack_small(m_, small))
    new_v.update(_unpack_small(v_, small))

    return (loss, grad_x[None], *[grads[k] for k in _ORDER], *[delta[k] for k in _ORDER],
            *[new_m[k] for k in _ORDER], *[new_v[k] for k in _ORDER])
```

```python
import functools

import jax
import jax.numpy as jnp
from jax import lax
from jax.experimental import pallas as pl
from jax.experimental.pallas import tpu as pltpu

F32 = jnp.float32
CDT = jnp.bfloat16
HEAD_DIM = 128
NORM_EPS = 1e-6
NEG_INF = -1e30
LOG2E = 1.4426950408889634
REL_BUCKETS = 32
REL_MAX_DISTANCE = 2048
DIL_PATTERNS = ((128, 1), (512, 4), (2048, 16))
DIL_BLOCK = 128
ADAM_LR, ADAM_B1, ADAM_B2, ADAM_EPS, ADAM_WD, ADAM_STEP = 0.001, 0.9, 0.999, 1e-08, 0.01, 10
N_CHIPS = 4
N_DEV = 8
VMEM_LIMIT_BYTES = 56 * 1024 * 1024
SMALL_COLS = 1024
MESH = pl.DeviceIdType.MESH


def _cparams(sem=None):
    return pltpu.CompilerParams(dimension_semantics=sem, vmem_limit_bytes=VMEM_LIMIT_BYTES)


def _tile(dim, pref):
    t = min(pref, dim)
    t -= t % 128
    while t >= 128:
        if dim % t == 0:
            return t
        t -= 128
    return dim


def _rowwise(fn, ins, out_dtypes, name, bs=256, consts=()):
    R, C = ins[0].shape
    bs = min(bs, R)
    n_in, n_c = len(ins), len(consts)

    def body(*refs):
        vals = [r[...] for r in refs[:n_in + n_c]]
        res = fn(*vals)
        for o, r in zip(refs[n_in + n_c:], res):
            o[...] = r.astype(o.dtype)

    row = pl.BlockSpec((bs, C), lambda i: (i, 0))
    return pl.pallas_call(
        body, grid=(R // bs,),
        in_specs=[row] * n_in + [pl.BlockSpec((1, c.shape[-1]), lambda i: (0, 0)) for c in consts],
        out_specs=[row] * len(out_dtypes),
        out_shape=[jax.ShapeDtypeStruct((R, C), d) for d in out_dtypes],
        name=name, compiler_params=_cparams(("parallel",)),
    )(*ins, *[c.reshape(1, -1) for c in consts])


def _rms_fwd(x, g, name):
    def fn(xf, gg):
        r = lax.rsqrt(jnp.mean(xf * xf, axis=-1, keepdims=True) + NORM_EPS)
        return ((xf * r) * gg,)
    return _rowwise(fn, [x], [CDT], name, consts=[g])[0]


def _rms_bwd(x, g, dh, dres, name, bs=256, dh_col=0):
    S, D = x.shape
    bs = min(bs, S)
    has_res = dres is not None

    def body(*refs):
        x_ref, g_ref, dh_ref = refs[:3]
        dx_ref, dxc_ref, dg_ref = refs[-3:]
        xf = x_ref[...]
        r = lax.rsqrt(jnp.mean(xf * xf, axis=-1, keepdims=True) + NORM_EPS)
        xhat = xf * r
        dh_ = dh_ref[...].astype(F32)
        dxhat = dh_ * g_ref[...]
        dx = r * (dxhat - xhat * jnp.mean(dxhat * xhat, axis=-1, keepdims=True))
        if has_res:
            dx = dx + refs[3][...]
        dx_ref[...] = dx
        dxc_ref[...] = dx.astype(dxc_ref.dtype)
        part = jnp.sum(dh_ * xhat, axis=0, keepdims=True)

        @pl.when(pl.program_id(0) == 0)
        def _():
            dg_ref[...] = part

        @pl.when(pl.program_id(0) > 0)
        def _():
            dg_ref[...] += part

    row = pl.BlockSpec((bs, D), lambda i: (i, 0))
    one = pl.BlockSpec((1, D), lambda i: (0, 0))
    ins = [x, g.reshape(1, D), dh] + ([dres] if has_res else [])
    return pl.pallas_call(
        body, grid=(S // bs,),
        in_specs=[row, one, pl.BlockSpec((bs, D), lambda i: (i, dh_col))] + ([row] if has_res else []),
        out_specs=[row, row, one],
        out_shape=[jax.ShapeDtypeStruct((S, D), F32), jax.ShapeDtypeStruct((S, D), CDT),
                   jax.ShapeDtypeStruct((1, D), F32)],
        name=name, compiler_params=_cparams(("arbitrary",)),
    )(*ins)


def _loss_bwd(x, g, target, name, bs=256):
    S, D = x.shape
    bs = min(bs, S)

    def body(x_ref, g_ref, t_ref, dx_ref, dxc_ref, dg_ref, loss_ref):
        xf = x_ref[...]
        r = lax.rsqrt(jnp.mean(xf * xf, axis=-1, keepdims=True) + NORM_EPS)
        xhat = xf * r
        err = xhat * g_ref[...] - t_ref[...]
        lpart = 0.5 * jnp.sum(jnp.mean(err * err, axis=-1, keepdims=True), axis=0, keepdims=True)
        dy = err / D
        dxhat = dy * g_ref[...]
        dx = r * (dxhat - xhat * jnp.mean(dxhat * xhat, axis=-1, keepdims=True))
        dx_ref[...] = dx
        dxc_ref[...] = dx.astype(dxc_ref.dtype)
        gpart = jnp.sum(dy * xhat, axis=0, keepdims=True)

        @pl.when(pl.program_id(0) == 0)
        def _():
            dg_ref[...] = gpart
            loss_ref[...] = lpart

        @pl.when(pl.program_id(0) > 0)
        def _():
            dg_ref[...] += gpart
            loss_ref[...] += lpart

    row = pl.BlockSpec((bs, D), lambda i: (i, 0))
    one = pl.BlockSpec((1, D), lambda i: (0, 0))
    return pl.pallas_call(
        body, grid=(S // bs,),
        in_specs=[row, one, row],
        out_specs=[row, row, one, pl.BlockSpec((1, 1), lambda i: (0, 0))],
        out_shape=[jax.ShapeDtypeStruct((S, D), F32), jax.ShapeDtypeStruct((S, D), CDT),
                   jax.ShapeDtypeStruct((1, D), F32), jax.ShapeDtypeStruct((1, 1), F32)],
        name=name, compiler_params=_cparams(("arbitrary",)),
    )(x, g.reshape(1, D), target)


_NN = (((1,), (0,)), ((), ()))
_NT = (((1,), (1,)), ((), ()))
_TN = (((0,), (0,)), ((), ()))


def _mm(a, b, *, M, N, K, a_spec, b_spec, o_spec, dims, tm, tn, tk, name, out_shapes, extras=(), epi=None):
    nk = K // tk
    n_ex, n_out = len(extras), len(out_shapes)
    in_place = epi is None
    if in_place:
        assert n_out == 1 and n_ex <= 1 and out_shapes[0].dtype == F32
        epi = lambda acc, *r: (acc + r[0] if r else acc,)

    def body(*refs):
        a_ref, b_ref = refs[0], refs[1]
        ex = refs[2:2 + n_ex]
        outs = refs[2 + n_ex:2 + n_ex + n_out]
        part = lax.dot_general(a_ref[...], b_ref[...], dims, preferred_element_type=F32)

        def finish(acc):
            for o, r in zip(outs, epi(acc, *[e[...] for e in ex])):
                o[...] = r.astype(o.dtype)

        if nk == 1:
            finish(part)
        elif in_place:
            k = pl.program_id(2)

            @pl.when(k == 0)
            def _():
                finish(part)

            @pl.when(k > 0)
            def _():
                outs[0][...] += part
        else:
            acc_ref = refs[-1]
            k = pl.program_id(2)

            @pl.when(k == 0)
            def _():
                acc_ref[...] = part

            @pl.when(k > 0)
            def _():
                acc_ref[...] += part

            @pl.when(k == nk - 1)
            def _():
                finish(acc_ref[...])

    ex_spec = pl.BlockSpec((tm, tn), lambda i, j, k: (i, j))
    return pl.pallas_call(
        body, grid=(M // tm, N // tn, nk),
        in_specs=[a_spec, b_spec] + [ex_spec] * n_ex,
        out_specs=[o_spec] * n_out,
        out_shape=out_shapes,
        scratch_shapes=[pltpu.VMEM((tm, tn), F32)] if nk > 1 and not in_place else [],
        name=name, compiler_params=_cparams(("parallel", "parallel", "arbitrary")),
    )(a, b, *extras)


def _mm_tiles(K):
    return (2048, 512, 2048) if K <= 2048 else (2048, 1024, 512)


def _mm_nn(a, b, name, out_dtypes, extras=(), epi=None, b_slots=False):
    M, K = a.shape
    tm, tn, tk = _mm_tiles(K)
    if b_slots:
        ns, _, Ns = b.shape
        N = ns * Ns
        tn = _tile(Ns, tn)
        npb = Ns // tn
        tk_ = _tile(K, tk)
        b_spec = pl.BlockSpec((None, tk_, tn), lambda i, j, k: (j // npb, k, j % npb))
    else:
        N = b.shape[1]
        tn = _tile(N, tn)
        tk_ = _tile(K, tk)
        b_spec = pl.BlockSpec((tk_, tn), lambda i, j, k: (k, j))
    tm = _tile(M, tm)
    return _mm(a, b, M=M, N=N, K=K, a_spec=pl.BlockSpec((tm, tk_), lambda i, j, k: (i, k)), b_spec=b_spec,
               o_spec=pl.BlockSpec((tm, tn), lambda i, j, k: (i, j)), dims=_NN, tm=tm, tn=tn, tk=tk_, name=name,
               out_shapes=[jax.ShapeDtypeStruct((M, N), d) for d in out_dtypes], extras=extras, epi=epi)


def _mm_nt(a, b, name, out_dtypes, extras=(), epi=None, b_slots=False):
    M, K = a.shape
    tm, tn, tk = _mm_tiles(K)
    tm = _tile(M, tm)
    if b_slots:
        ns, N, Ks = b.shape
        tk_ = _tile(Ks, tk)
        kpb = Ks // tk_
        tn = _tile(N, tn)
        b_spec = pl.BlockSpec((None, tn, tk_), lambda i, j, k: (k // kpb, j, k % kpb))
    else:
        N = b.shape[0]
        tk_ = _tile(K, tk)
        tn = _tile(N, tn)
        b_spec = pl.BlockSpec((tn, tk_), lambda i, j, k: (j, k))
    return _mm(a, b, M=M, N=N, K=K, a_spec=pl.BlockSpec((tm, tk_), lambda i, j, k: (i, k)), b_spec=b_spec,
               o_spec=pl.BlockSpec((tm, tn), lambda i, j, k: (i, j)), dims=_NT, tm=tm, tn=tn, tk=tk_, name=name,
               out_shapes=[jax.ShapeDtypeStruct((M, N), d) for d in out_dtypes], extras=extras, epi=epi)


def _mm_tn(a, b, name, out_slots=0, tm=2048, tn=1024, tk=512):
    K, M = a.shape
    N = b.shape[1]
    tm, tk_ = _tile(M, tm), _tile(K, tk)
    if out_slots:
        Ns = N // out_slots
        tn = _tile(Ns, tn)
        npb = Ns // tn
        o_spec = pl.BlockSpec((None, tm, tn), lambda i, j, k: (j // npb, i, j % npb))
        out_shape = jax.ShapeDtypeStruct((out_slots, M, Ns), F32)
    else:
        tn = _tile(N, tn)
        o_spec = pl.BlockSpec((tm, tn), lambda i, j, k: (i, j))
        out_shape = jax.ShapeDtypeStruct((M, N), F32)
    return _mm(a, b, M=M, N=N, K=K, a_spec=pl.BlockSpec((tk_, tm), lambda i, j, k: (k, i)),
               b_spec=pl.BlockSpec((tk_, tn), lambda i, j, k: (k, j)), o_spec=o_spec, dims=_TN,
               tm=tm, tn=tn, tk=tk_, name=name, out_shapes=[out_shape])[0]


GATE_BLOCK = 512


def _split3(v):
    hi = v.astype(jnp.bfloat16)
    r1 = v - hi.astype(F32)
    mid = r1.astype(jnp.bfloat16)
    lo = (r1 - mid.astype(F32)).astype(jnp.bfloat16)
    return hi, mid, lo


def _exact_dot(v, tri):
    return functools.reduce(jnp.add, [jnp.dot(t, tri, preferred_element_type=F32) for t in _split3(v)])


def _gates_fwd(f_t, b, name):
    H, S = f_t.shape
    nb = _tile(S, GATE_BLOCK)
    inv_scale = HEAD_DIM ** 0.5

    def body(f_ref, b_ref, c_ref):
        upper = (lax.broadcasted_iota(jnp.int32, (nb, nb), 0)
                 <= lax.broadcasted_iota(jnp.int32, (nb, nb), 1)).astype(jnp.bfloat16)
        carry = jnp.zeros((H, 1), F32)
        for i in range(S // nb):
            z = f_ref[:, i * nb:(i + 1) * nb] + b_ref[...]
            logf = jnp.minimum(z, 0.0) - jnp.log1p(jnp.exp(-jnp.abs(z)))
            cs = _exact_dot(logf, upper) + carry
            for j, t in enumerate(_split3(cs * inv_scale)):
                c_ref[j, :, i * nb:(i + 1) * nb] = t
            carry = cs[:, nb - 1:nb]

    return pl.pallas_call(body, out_shape=jax.ShapeDtypeStruct((3, H, S), jnp.bfloat16), name=name,
                          compiler_params=_cparams())(f_t, b.reshape(H, 1))


def _gates_bwd(f_t, b, dcq, dck, name):
    H, S = f_t.shape
    nb = _tile(S, GATE_BLOCK)

    def body(f_ref, b_ref, dcq_ref, dck_ref, df_ref, dfc_ref, db_ref):
        lower = (lax.broadcasted_iota(jnp.int32, (nb, nb), 0)
                 >= lax.broadcasted_iota(jnp.int32, (nb, nb), 1)).astype(jnp.bfloat16)
        carry = jnp.zeros((H, 1), F32)
        db = jnp.zeros((H, 1), F32)
        for i in reversed(range(S // nb)):
            sl = slice(i * nb, (i + 1) * nb)
            dc = dcq_ref[:, sl] - dck_ref[:, sl]
            dlogf = _exact_dot(dc, lower) + carry
            carry = dlogf[:, 0:1]
            z = f_ref[:, sl] + b_ref[...]
            df = dlogf / (1.0 + jnp.exp(z))
            df_ref[:, sl] = df
            dfc_ref[:, sl] = df.astype(dfc_ref.dtype)
            db = db + jnp.sum(df, axis=1, keepdims=True)
        db_ref[...] = db

    return pl.pallas_call(
        body, out_shape=[jax.ShapeDtypeStruct((H, S), F32), jax.ShapeDtypeStruct((H, S), CDT),
                         jax.ShapeDtypeStruct((H, 1), F32)],
        name=name, compiler_params=_cparams())(f_t, b.reshape(H, 1), dcq, dck)


FOX_BLOCK = 512


def _fox_bias_operands(csplit):
    parts = csplit.transpose(2, 1, 0)
    S, H, _ = parts.shape
    ones = jnp.ones_like(parts)
    zeros = jnp.zeros((S, H, HEAD_DIM - 6), parts.dtype)
    qc = jnp.concatenate([parts, ones, zeros], axis=-1).reshape(S, H * HEAD_DIM)
    kc = jnp.concatenate([ones, -parts, zeros], axis=-1).reshape(S, H * HEAD_DIM)
    return qc, kc


def _fox_logits2(q_ref, qc_ref, k_ref, kc_ref, diag):
    q, k = q_ref[...], k_ref[...]
    qa = jnp.concatenate([q, qc_ref[...].astype(q.dtype)], axis=1)
    ka = jnp.concatenate([k, kc_ref[...].astype(k.dtype)], axis=1)
    s = lax.dot_general(qa, ka, _NT, preferred_element_type=F32) * (HEAD_DIM ** -0.5 * LOG2E)
    if diag:
        row = lax.broadcasted_iota(jnp.int32, s.shape, 0)
        col = lax.broadcasted_iota(jnp.int32, s.shape, 1)
        s = jnp.where(col <= row, s, NEG_INF)
    return s


def _fox_fwd(proj, qc, kc, H, name):
    S = proj.shape[0]
    E = HEAD_DIM
    blk = _tile(S, FOX_BLOCK)
    nq = S // blk

    def body(q_ref, qc_ref, k_ref, kc_ref, v_ref, o_ref, lse_ref, m_s, l_s, acc_s):
        qi, kj = pl.program_id(1), pl.program_id(2)

        @pl.when(kj == 0)
        def _():
            m_s[...] = jnp.full(m_s.shape, NEG_INF, F32)
            l_s[...] = jnp.zeros(l_s.shape, F32)
            acc_s[...] = jnp.zeros(acc_s.shape, F32)

        def step(diag):
            s = _fox_logits2(q_ref, qc_ref, k_ref, kc_ref, diag)
            m_prev = m_s[...]
            m_new = jnp.maximum(m_prev, jnp.max(s, axis=-1, keepdims=True))
            alpha = jnp.exp2(m_prev - m_new)
            p = jnp.exp2(s - m_new)
            l_s[...] = alpha * l_s[...] + jnp.sum(p, axis=-1, keepdims=True)
            acc_s[...] = alpha * acc_s[...] + jnp.dot(p.astype(CDT), v_ref[...], preferred_element_type=F32)
            m_s[...] = m_new

        pl.when(kj < qi)(lambda: step(False))
        pl.when(kj == qi)(lambda: step(True))

        @pl.when(kj == nq - 1)
        def _():
            o_ref[...] = acc_s[...] / l_s[...]
            lse_ref[...] = jnp.broadcast_to(m_s[...] + jnp.log2(l_s[...]), lse_ref.shape)

    qspec = lambda off: pl.BlockSpec((blk, E), lambda h, i, j: (i, off + h))
    kspec = lambda off: pl.BlockSpec((blk, E), lambda h, i, j: (jnp.minimum(j, i), off + h))
    return pl.pallas_call(
        body, grid=(H, nq, nq),
        in_specs=[qspec(0), qspec(0), kspec(H), kspec(0), kspec(2 * H)],
        out_specs=[qspec(0)] * 2,
        out_shape=[jax.ShapeDtypeStruct((S, H * E), F32)] * 2,
        scratch_shapes=[pltpu.VMEM((blk, 1), F32), pltpu.VMEM((blk, 1), F32), pltpu.VMEM((blk, E), F32)],
        name=name, compiler_params=_cparams(("parallel", "parallel", "arbitrary")),
    )(proj, qc, proj, kc, proj)


def _fox_bwd(proj, qc, kc, lse, o, do, H, name):
    S = proj.shape[0]
    E = HEAD_DIM
    blk = _tile(S, FOX_BLOCK)
    nq = S // blk
    scale = E ** -0.5

    def body(q_ref, qc_ref, k_ref, kc_ref, v_ref, lse_ref, o_ref, do_ref,
             dq_ref, dcq_ref, dk_ref, dv_ref, dck_ref, dq_s, dcq_s, dk_s, dv_s, dck_s):
        kj, qi = pl.program_id(1), pl.program_id(2)

        @pl.when(qi == 0)
        def _():
            dk_s[...] = jnp.zeros(dk_s.shape, F32)
            dv_s[...] = jnp.zeros(dv_s.shape, F32)
            dck_s[...] = jnp.zeros(dck_s.shape, F32)

        def step(diag):
            do = do_ref[...]
            doc = do.astype(CDT)
            delta = jnp.sum(do * o_ref[...], axis=-1, keepdims=True)
            p = jnp.exp2(_fox_logits2(q_ref, qc_ref, k_ref, kc_ref, diag) - lse_ref[:, 0:1])
            dp = lax.dot_general(doc, v_ref[...], _NT, preferred_element_type=F32)
            ds = p * (dp - delta)
            dss = ds * scale
            dck_s[...] += jnp.sum(ds, axis=0, keepdims=True)
            dv_s[...] += jnp.dot(p.T.astype(CDT), doc, preferred_element_type=F32)
            dk_s[...] += jnp.dot(dss.T.astype(CDT), q_ref[...], preferred_element_type=F32)
            dq_part = jnp.dot(dss.astype(CDT), k_ref[...], preferred_element_type=F32)
            dc_part = jnp.sum(ds, axis=-1, keepdims=True)
            rows = pl.ds(pl.multiple_of(qi * blk, blk), blk)

            @pl.when(kj == 0)
            def _():
                dq_s[rows, :] = dq_part
                dcq_s[rows, :] = dc_part

            @pl.when(kj > 0)
            def _():
                dq_s[rows, :] += dq_part
                dcq_s[rows, :] += dc_part

        pl.when(qi > kj)(lambda: step(False))
        pl.when(qi == kj)(lambda: step(True))

        @pl.when(qi == nq - 1)
        def _():
            dk_ref[...] = dk_s[...].astype(dk_ref.dtype)
            dv_ref[...] = dv_s[...].astype(dv_ref.dtype)
            dck_ref[...] = dck_s[...].reshape(dck_ref.shape)

        @pl.when((qi == nq - 1) & (kj == nq - 1))
        def _():
            dq_ref[...] = dq_s[...].astype(dq_ref.dtype)
            dcq_ref[...] = jnp.broadcast_to(dcq_s[...], dcq_ref.shape)

    qspec = lambda off: pl.BlockSpec((blk, E), lambda h, j, i: (jnp.maximum(i, j), off + h))
    kspec = lambda off: pl.BlockSpec((blk, E), lambda h, j, i: (j, off + h))
    head = pl.BlockSpec((S, E), lambda h, j, i: (0, h))
    return pl.pallas_call(
        body, grid=(H, nq, nq),
        in_specs=[qspec(0), qspec(0), kspec(H), kspec(0), kspec(2 * H), qspec(0), qspec(0), qspec(0)],
        out_specs=[head, head, kspec(0), kspec(0), pl.BlockSpec((1, 1, blk), lambda h, j, i: (h, 0, j))],
        out_shape=[jax.ShapeDtypeStruct((S, H * E), CDT), jax.ShapeDtypeStruct((S, H * E), F32),
                   jax.ShapeDtypeStruct((S, H * E), CDT), jax.ShapeDtypeStruct((S, H * E), CDT),
                   jax.ShapeDtypeStruct((H, 1, S), F32)],
        scratch_shapes=[pltpu.VMEM((S, E), F32), pltpu.VMEM((S, 1), F32), pltpu.VMEM((blk, E), F32),
                        pltpu.VMEM((blk, E), F32), pltpu.VMEM((1, blk), F32)],
        name=name, compiler_params=_cparams(("parallel", "arbitrary", "arbitrary")),
    )(proj, qc, proj, kc, proj, lse, o, do)


DIL_ROWS = 512


def _rel_bucket(dist):
    max_exact = REL_BUCKETS // 2
    d = jnp.maximum(dist.astype(F32), 1.0)
    large = max_exact + (jnp.log(d / max_exact) / jnp.log(jnp.float32(REL_MAX_DISTANCE / max_exact))
                         * (REL_BUCKETS - max_exact)).astype(jnp.int32)
    large = jnp.minimum(large, REL_BUCKETS - 1)
    return jnp.where(dist < max_exact, dist, large)


def _bucket_table():
    i = jnp.arange(DIL_BLOCK)[:, None]
    j = jnp.arange(2 * DIL_BLOCK)[None, :]
    rel = DIL_BLOCK + i - j
    tabs = [_rel_bucket(jnp.clip(rel, 0, w // d) * d) for w, d in DIL_PATTERNS]
    return jnp.stack(tabs).astype(jnp.int32)


def _bias_table(rel_bias, buckets, name):
    P = buckets.shape[0]
    H = rel_bias.shape[1]

    def body(rb_ref, bk_ref, out_ref):
        h = pl.program_id(1)
        bk = bk_ref[0]
        val = jnp.zeros(bk.shape, F32)
        for b in range(REL_BUCKETS):
            val = jnp.where(bk == b, rb_ref[b, h], val)
        out_ref[0, 0] = val

    return pl.pallas_call(
        body, grid=(P, H),
        in_specs=[pl.BlockSpec(memory_space=pltpu.SMEM),
                  pl.BlockSpec((1, DIL_BLOCK, 2 * DIL_BLOCK), lambda p, h: (p, 0, 0))],
        out_specs=pl.BlockSpec((1, 1, DIL_BLOCK, 2 * DIL_BLOCK), lambda p, h: (p, h, 0, 0)),
        out_shape=jax.ShapeDtypeStruct((P, H, DIL_BLOCK, 2 * DIL_BLOCK), F32),
        name=name, compiler_params=_cparams(("parallel", "parallel")),
    )(rel_bias, buckets)


def _bias_table_bwd(dbias, buckets, name):
    P, H = dbias.shape[:2]

    def body(db_ref, bk_ref, out_ref):
        lane = lax.broadcasted_iota(jnp.int32, (1, REL_BUCKETS), 1)
        acc = jnp.zeros((1, REL_BUCKETS), F32)
        bk = bk_ref[...]
        db = db_ref[:, 0]
        for b in range(REL_BUCKETS):
            tot = jnp.sum(jnp.where(bk == b, db, 0.0))
            acc = jnp.where(lane == b, tot, acc)
        out_ref[0] = acc

    return pl.pallas_call(
        body, grid=(H,),
        in_specs=[pl.BlockSpec((P, 1, DIL_BLOCK, 2 * DIL_BLOCK), lambda h: (0, h, 0, 0)),
                  pl.BlockSpec((P, DIL_BLOCK, 2 * DIL_BLOCK), lambda h: (0, 0, 0))],
        out_specs=pl.BlockSpec((1, 1, REL_BUCKETS), lambda h: (h, 0, 0)),
        out_shape=jax.ShapeDtypeStruct((H, 1, REL_BUCKETS), F32),
        name=name, compiler_params=_cparams(("parallel",)),
    )(dbias, buckets)


def _dil_masks(T, first):
    B = DIL_BLOCK
    ii = lax.broadcasted_iota(jnp.int32, (T, B, B), 1)
    jj = lax.broadcasted_iota(jnp.int32, (T, B, B), 2)
    tt = lax.broadcasted_iota(jnp.int32, (T, B, B), 0)
    return jj <= ii, (jj >= ii) & ((first + tt) > 0)


def _bdot(a, b, contract_b):
    return lax.dot_general(a, b, (((2,), (contract_b,)), ((0,), (0,))), preferred_element_type=F32)


def _dil_geometry(S, d):
    n_sub = S // d
    rows = _tile(n_sub, DIL_ROWS)
    return n_sub, rows, rows // DIL_BLOCK, n_sub // rows


def _dil_fwd(proj, bias, d, H, col0, name):
    S, C = proj.shape
    E = B = DIL_BLOCK
    n_sub, rows, T, nc = _dil_geometry(S, d)
    cb = C // E
    scale = E ** -0.5

    def body(q_ref, k_ref, v_ref, kh_ref, vh_ref, b_ref, o_ref, lse_ref):
        n = pl.program_id(2)
        q = q_ref[...].reshape(T, B, E)
        kc = k_ref[...].reshape(T, B, E)
        vc = v_ref[...].reshape(T, B, E)
        kp = jnp.concatenate([kh_ref[...][None], kc[:T - 1]], axis=0) if T > 1 else kh_ref[...][None]
        vp = jnp.concatenate([vh_ref[...][None], vc[:T - 1]], axis=0) if T > 1 else vh_ref[...][None]
        mask_c, mask_p = _dil_masks(T, n * T)
        bias_pc = b_ref[0]
        s_c = jnp.where(mask_c, _bdot(q, kc, 2) * scale + bias_pc[:, B:][None], NEG_INF)
        s_p = jnp.where(mask_p, _bdot(q, kp, 2) * scale + bias_pc[:, :B][None], NEG_INF)
        m = jnp.maximum(jnp.max(s_c, axis=-1, keepdims=True), jnp.max(s_p, axis=-1, keepdims=True))
        p_c = jnp.exp(s_c - m)
        p_p = jnp.exp(s_p - m)
        ssum = jnp.sum(p_c, axis=-1, keepdims=True) + jnp.sum(p_p, axis=-1, keepdims=True)
        o = (_bdot(p_c.astype(CDT), vc, 1) + _bdot(p_p.astype(CDT), vp, 1)) / ssum
        o_ref[...] = o.reshape(rows, E)
        lse_ref[...] = jnp.broadcast_to(m + jnp.log(ssum), (T, B, E)).reshape(rows, E)

    view = proj.reshape(n_sub, d * C)
    cur = lambda off: pl.BlockSpec((rows, E), lambda h, r, n: (n, r * cb + off + h))
    halo = lambda off: pl.BlockSpec((B, E), lambda h, r, n: (jnp.maximum(n * T - 1, 0), r * cb + off + h))
    ospec = pl.BlockSpec((rows, E), lambda h, r, n: (n, r * H + h))
    o, lse = pl.pallas_call(
        body, grid=(H, d, nc),
        in_specs=[cur(col0), cur(col0 + H), cur(col0 + 2 * H), halo(col0 + H), halo(col0 + 2 * H),
                  pl.BlockSpec((1, B, 2 * B), lambda h, r, n: (h, 0, 0))],
        out_specs=[ospec, ospec],
        out_shape=[jax.ShapeDtypeStruct((n_sub, d * H * E), F32)] * 2,
        name=name, compiler_params=_cparams(("parallel", "parallel", "parallel")),
    )(view, view, view, view, view, bias)
    return o.reshape(S, H * E), lse.reshape(S, H * E)


def _dil_bwd(proj, bias, y, dy, lse, d, H, col0, name):
    S, C = proj.shape
    E = B = DIL_BLOCK
    n_sub, rows, T, nc = _dil_geometry(S, d)
    cb = C // E
    scale = E ** -0.5

    def body(q_ref, k_ref, v_ref, kh_ref, vh_ref, qa_ref, b_ref, y_ref, dy_ref, lse_ref, ya_ref, dya_ref, lsea_ref,
             dq_ref, dk_ref, dv_ref, db_ref):
        r, n = pl.program_id(1), pl.program_id(2)
        q = q_ref[...].reshape(T, B, E)
        kc = k_ref[...].reshape(T, B, E)
        vc = v_ref[...].reshape(T, B, E)
        kp = jnp.concatenate([kh_ref[...][None], kc[:T - 1]], axis=0) if T > 1 else kh_ref[...][None]
        vp = jnp.concatenate([vh_ref[...][None], vc[:T - 1]], axis=0) if T > 1 else vh_ref[...][None]
        dy_ = dy_ref[...].reshape(T, B, E)
        dyc = dy_.astype(CDT)
        delta = jnp.sum(dy_ * y_ref[...].reshape(T, B, E), axis=-1, keepdims=True)
        lse_ = lse_ref[...].reshape(T, B, E)[:, :, 0:1]
        mask_c, mask_p = _dil_masks(T, n * T)
        bias_pc = b_ref[0]
        s_c = jnp.where(mask_c, _bdot(q, kc, 2) * scale + bias_pc[:, B:][None], NEG_INF)
        s_p = jnp.where(mask_p, _bdot(q, kp, 2) * scale + bias_pc[:, :B][None], NEG_INF)
        p_c = jnp.exp(s_c - lse_)
        p_p = jnp.exp(s_p - lse_)
        ds_c = p_c * (_bdot(dyc, vc, 2) - delta)
        ds_p = p_p * (_bdot(dyc, vp, 2) - delta)
        dsc_s = (ds_c * scale).astype(CDT)
        dsp_s = (ds_p * scale).astype(CDT)
        dq_ref[...] = (_bdot(dsc_s, kc, 1) + _bdot(dsp_s, kp, 1)).reshape(rows, E)
        tr = lambda t: jnp.swapaxes(t, 1, 2)
        dk = _bdot(tr(ds_c * scale).astype(CDT), q, 1)
        dv = _bdot(tr(p_c).astype(CDT), dyc, 1)
        dk_prev = _bdot(tr(ds_p * scale).astype(CDT), q, 1)
        dv_prev = _bdot(tr(p_p).astype(CDT), dyc, 1)

        qa = qa_ref[...]
        dya = dya_ref[...]
        dyac = dya.astype(CDT)
        delta_a = jnp.sum(dya * ya_ref[...], axis=-1, keepdims=True)
        ii = lax.broadcasted_iota(jnp.int32, (B, B), 0)
        jj = lax.broadcasted_iota(jnp.int32, (B, B), 1)
        mask_a = (jj >= ii) & (n < nc - 1)
        s_a = lax.dot_general(qa, kc[T - 1], _NT, preferred_element_type=F32) * scale + bias_pc[:, :B]
        p_a = jnp.exp(jnp.where(mask_a, s_a, NEG_INF) - lsea_ref[:, 0:1])
        ds_a = p_a * (lax.dot_general(dyac, vc[T - 1], _NT, preferred_element_type=F32) - delta_a)
        dk_a = jnp.dot((ds_a * scale).T.astype(CDT), qa, preferred_element_type=F32)
        dv_a = jnp.dot(p_a.T.astype(CDT), dyac, preferred_element_type=F32)
        if T > 1:
            dk = dk + jnp.concatenate([dk_prev[1:], dk_a[None]], axis=0)
            dv = dv + jnp.concatenate([dv_prev[1:], dv_a[None]], axis=0)
        else:
            dk = dk + dk_a[None]
            dv = dv + dv_a[None]
        dk_ref[...] = dk.reshape(rows, E)
        dv_ref[...] = dv.reshape(rows, E)

        dbias = jnp.concatenate([jnp.sum(ds_p, axis=0), jnp.sum(ds_c, axis=0)], axis=1)

        @pl.when((r == 0) & (n == 0))
        def _():
            db_ref[0] = dbias

        @pl.when((r > 0) | (n > 0))
        def _():
            db_ref[0] += dbias

    view = proj.reshape(n_sub, d * C)
    act = lambda t: t.reshape(n_sub, d * H * E)
    last = n_sub // B - 1
    cur = lambda off: pl.BlockSpec((rows, E), lambda h, r, n: (n, r * cb + off + h))
    before = lambda off: pl.BlockSpec((B, E), lambda h, r, n: (jnp.maximum(n * T - 1, 0), r * cb + off + h))
    after_q = pl.BlockSpec((B, E), lambda h, r, n: (jnp.minimum((n + 1) * T, last), r * cb + col0 + h))
    acur = pl.BlockSpec((rows, E), lambda h, r, n: (n, r * H + h))
    aafter = pl.BlockSpec((B, E), lambda h, r, n: (jnp.minimum((n + 1) * T, last), r * H + h))
    dq, dk, dv, db = pl.pallas_call(
        body, grid=(H, d, nc),
        in_specs=[cur(col0), cur(col0 + H), cur(col0 + 2 * H), before(col0 + H), before(col0 + 2 * H), after_q,
                  pl.BlockSpec((1, B, 2 * B), lambda h, r, n: (h, 0, 0)),
                  acur, acur, acur, aafter, aafter, aafter],
        out_specs=[acur, acur, acur, pl.BlockSpec((1, B, 2 * B), lambda h, r, n: (h, 0, 0))],
        out_shape=[jax.ShapeDtypeStruct((n_sub, d * H * E), F32)] * 3
        + [jax.ShapeDtypeStruct((H, B, 2 * B), F32)],
        name=name, compiler_params=_cparams(("parallel", "arbitrary", "arbitrary")),
    )(view, view, view, view, view, view, bias, act(y), act(dy), act(lse), act(y), act(dy), act(lse))
    return dq.reshape(S, H * E), dk.reshape(S, H * E), dv.reshape(S, H * E), db


def _dil_combine(outs, lses, name):
    n = len(outs)

    def fn(*t):
        o, l = t[:n], t[n:]
        m = functools.reduce(jnp.maximum, l)
        w = [jnp.exp(li - m) for li in l]
        tot = functools.reduce(jnp.add, w)
        y = functools.reduce(jnp.add, [(wi / tot) * oi for wi, oi in zip(w, o)])
        return y, m + jnp.log(tot)

    return _rowwise(fn, list(outs) + list(lses), [F32, F32], name)


def _sum_cast(parts, name):
    return _rowwise(lambda *t: (functools.reduce(jnp.add, t),), list(parts), [CDT], name)[0]


def _adamw(w, g, m, v, name, br=128):
    R, C = w.shape
    br = br if R % br == 0 else R

    def body(w_ref, g_ref, m_ref, v_ref, d_ref, nm_ref, nv_ref):
        g_ = g_ref[...]
        m_ = ADAM_B1 * m_ref[...] + (1.0 - ADAM_B1) * g_
        v_ = ADAM_B2 * v_ref[...] + (1.0 - ADAM_B2) * jnp.square(g_)
        m_hat = m_ / (1.0 - ADAM_B1 ** ADAM_STEP)
        v_hat = v_ / (1.0 - ADAM_B2 ** ADAM_STEP)
        d_ref[...] = -ADAM_LR * (m_hat / (jnp.sqrt(v_hat) + ADAM_EPS) + ADAM_WD * w_ref[...])
        nm_ref[...] = m_
        nv_ref[...] = v_

    blk = pl.BlockSpec((br, C), lambda i: (i, 0))
    return pl.pallas_call(
        body, grid=(R // br,), in_specs=[blk] * 4, out_specs=[blk] * 3,
        out_shape=[jax.ShapeDtypeStruct((R, C), F32)] * 3,
        name=name, compiler_params=_cparams(("parallel",)),
    )(w, g, m, v)


_HBM = pl.BlockSpec(memory_space=pltpu.HBM)


def _place():
    x, y, c = lax.axis_index("x"), lax.axis_index("y"), lax.axis_index("c")
    chips = [(1 - x, y), (x, 1 - y), (1 - x, 1 - y)]
    return x, y, c, chips


def _row_block(R, pref=256):
    return _tile(R, pref) if R % 128 == 0 else R


def _cast_into_slot(w, ids, name):
    R, C = w.shape
    br = _row_block(R)

    def body(ids_ref, w_ref, out_ref):
        out_ref[...] = w_ref[...].astype(out_ref.dtype)

    return pl.pallas_call(
        body,
        grid_spec=pltpu.PrefetchScalarGridSpec(
            num_scalar_prefetch=1, grid=(R // br,),
            in_specs=[pl.BlockSpec((br, C), lambda i, ids_ref: (i, 0))],
            out_specs=pl.BlockSpec((None, br, C), lambda i, ids_ref: (ids_ref[0], i, 0))),
        out_shape=jax.ShapeDtypeStruct((N_CHIPS, R, C), CDT),
        name=name, compiler_params=_cparams(("parallel",)),
    )(ids, w)


def _all_gather_chips(buf, name):
    _, R, C = buf.shape
    Rh = R // 2

    def body(buf_ref, out_ref, send_sems, recv_sems):
        x, y, c, chips = _place()
        slot = 2 * x + y

        def half(ref, s, hf):
            return ref.at[s, pl.ds(hf * Rh, Rh), :]

        def copy(k, src, dst, to):
            return pltpu.make_async_remote_copy(src_ref=src, dst_ref=dst, send_sem=send_sems.at[k],
                                                recv_sem=recv_sems.at[k], device_id=to, device_id_type=MESH)

        first = [copy(j, half(buf_ref, slot, c), half(out_ref, slot, c), (cx, cy, c))
                 for j, (cx, cy) in enumerate(chips)]
        for cp in first:
            cp.start()
        passed = []
        for j, (cx, cy) in enumerate(chips):
            s = 2 * cx + cy
            copy(j, half(out_ref, s, c), half(out_ref, s, c), (cx, cy, c)).wait_recv()
            fwd = copy(3 + j, half(out_ref, s, c), half(out_ref, s, c), (x, y, 1 - c))
            fwd.start()
            passed.append(fwd)
        for j, (cx, cy) in enumerate(chips):
            s = 2 * cx + cy
            copy(3 + j, half(out_ref, s, 1 - c), half(out_ref, s, 1 - c), (x, y, 1 - c)).wait_recv()
        for cp in first + passed:
            cp.wait_send()

    return pl.pallas_call(
        body, in_specs=[_HBM], out_specs=_HBM, out_shape=jax.ShapeDtypeStruct(buf.shape, buf.dtype),
        input_output_aliases={0: 0},
        scratch_shapes=[pltpu.SemaphoreType.DMA((6,)), pltpu.SemaphoreType.DMA((6,))],
        name=name, compiler_params=pltpu.CompilerParams(),
    )(buf)


def _sibling_halves(g, name):
    ns, R, C = g.shape
    Rh = R // 2

    def body(g_ref, out_ref, send_sem, recv_sem):
        x, y, c, _ = _place()
        cp = pltpu.make_async_remote_copy(src_ref=g_ref.at[:, pl.ds((1 - c) * Rh, Rh), :], dst_ref=out_ref,
                                          send_sem=send_sem, recv_sem=recv_sem, device_id=(x, y, 1 - c),
                                          device_id_type=MESH)
        cp.start()
        cp.wait()

    return pl.pallas_call(
        body, in_specs=[_HBM], out_specs=_HBM, out_shape=jax.ShapeDtypeStruct((ns, Rh, C), g.dtype),
        scratch_shapes=[pltpu.SemaphoreType.DMA, pltpu.SemaphoreType.DMA],
        name=name, compiler_params=pltpu.CompilerParams(),
    )(g)


def _chip_sum(g, other, ids, name):
    ns, R, C = g.shape
    Rh = R // 2
    br = _row_block(Rh)
    nb = Rh // br

    def body(ids_ref, g_ref, o_ref, out_ref):
        out_ref[...] = (g_ref[...] + o_ref[...]).astype(out_ref.dtype)

    return pl.pallas_call(
        body,
        grid_spec=pltpu.PrefetchScalarGridSpec(
            num_scalar_prefetch=1, grid=(ns, nb),
            in_specs=[pl.BlockSpec((None, br, C), lambda s, i, ids_ref: (s, ids_ref[1] * nb + i, 0)),
                      pl.BlockSpec((None, br, C), lambda s, i, ids_ref: (s, i, 0))],
            out_specs=pl.BlockSpec((None, br, C), lambda s, i, ids_ref: (s, i, 0))),
        out_shape=jax.ShapeDtypeStruct((ns, Rh, C), CDT),
        name=name, compiler_params=_cparams(("parallel", "parallel")),
    )(ids, g, other)


def _scatter_chips(q, name):
    ns, Rh, C = q.shape

    def body(q_ref, out_ref, send_sems, recv_sems):
        x, y, c, chips = _place()
        slot = 2 * x + y
        sends = []
        for j, (cx, cy) in enumerate(chips):
            cp = pltpu.make_async_remote_copy(src_ref=q_ref.at[2 * cx + cy], dst_ref=out_ref.at[slot],
                                              send_sem=send_sems.at[j], recv_sem=recv_sems.at[j],
                                              device_id=(cx, cy, c), device_id_type=MESH)
            cp.start()
            sends.append(cp)
        for j, (cx, cy) in enumerate(chips):
            pltpu.make_async_remote_copy(src_ref=q_ref.at[slot], dst_ref=out_ref.at[2 * cx + cy],
                                         send_sem=send_sems.at[j], recv_sem=recv_sems.at[j],
                                         device_id=(cx, cy, c), device_id_type=MESH).wait_recv()
        for cp in sends:
            cp.wait_send()

    return pl.pallas_call(
        body, in_specs=[_HBM], out_specs=_HBM, out_shape=jax.ShapeDtypeStruct((ns, Rh, C), q.dtype),
        scratch_shapes=[pltpu.SemaphoreType.DMA((3,)), pltpu.SemaphoreType.DMA((3,))],
        name=name, compiler_params=pltpu.CompilerParams(),
    )(q)


def _slot_sum(b, q, ids, layer, n_layers, prev, name):
    ns, Rh, C = b.shape
    br = _row_block(Rh)
    nb = Rh // br

    def body(ids_ref, *refs):
        slot = ids_ref[0]
        own = refs[ns][...]
        acc = None
        for s in range(ns):
            t = jnp.where(slot == s, own, refs[s][...]).astype(F32)
            acc = t if acc is None else acc + t
        refs[-1][...] = acc

    def piece(s):
        return pl.BlockSpec((None, br, C), lambda i, ids_ref: (jnp.where(ids_ref[0] == s, (s + 1) % ns, s), i, 0))

    ins = [ids] + [b] * ns + [q] + ([prev] if prev is not None else [])
    return pl.pallas_call(
        body,
        grid_spec=pltpu.PrefetchScalarGridSpec(
            num_scalar_prefetch=1, grid=(nb,),
            in_specs=[piece(s) for s in range(ns)]
            + [pl.BlockSpec((None, br, C), lambda i, ids_ref: (ids_ref[0], i, 0))]
            + ([pl.BlockSpec(memory_space=pl.ANY)] if prev is not None else []),
            out_specs=pl.BlockSpec((None, br, C), lambda i, ids_ref: (layer, ids_ref[1] * nb + i, 0))),
        out_shape=jax.ShapeDtypeStruct((n_layers, 2 * Rh, C), F32),
        input_output_aliases={ns + 2: 0} if prev is not None else {},
        name=name, compiler_params=_cparams(("parallel",)),
    )(*ins)


def _join_halves(g, layer, name):
    _, R, C = g.shape
    Rh = R // 2

    def body(g_ref, out_ref, send_sem, recv_sem):
        x, y, c, _ = _place()
        mine = pl.ds(c * Rh, Rh)
        cp = pltpu.make_async_remote_copy(src_ref=g_ref.at[layer, mine, :], dst_ref=out_ref.at[layer, mine, :],
                                          send_sem=send_sem, recv_sem=recv_sem, device_id=(x, y, 1 - c),
                                          device_id_type=MESH)
        cp.start()
        other = out_ref.at[layer, pl.ds((1 - c) * Rh, Rh), :]
        pltpu.make_async_remote_copy(src_ref=other, dst_ref=other, send_sem=send_sem, recv_sem=recv_sem,
                                     device_id=(x, y, 1 - c), device_id_type=MESH).wait_recv()
        cp.wait_send()

    return pl.pallas_call(
        body, in_specs=[_HBM], out_specs=_HBM, out_shape=jax.ShapeDtypeStruct(g.shape, g.dtype),
        input_output_aliases={0: 0},
        scratch_shapes=[pltpu.SemaphoreType.DMA, pltpu.SemaphoreType.DMA],
        name=name, compiler_params=pltpu.CompilerParams(),
    )(g)


def _all_reduce_small(v, name):
    rows, cols = v.shape

    def body(v_ref, out_ref, buf, send_sems, recv_sems):
        x, y, c, _ = _place()
        me = 4 * x + 2 * y + c
        buf[me] = v_ref[...]
        peers = []
        for k in range(1, N_DEV):
            px, py, pc = (x + (k >> 2)) % 2, (y + ((k >> 1) & 1)) % 2, (c + (k & 1)) % 2
            peers.append((px, py, pc))
        sends = []
        for k, peer in enumerate(peers):
            cp = pltpu.make_async_remote_copy(src_ref=v_ref, dst_ref=buf.at[me], send_sem=send_sems.at[k],
                                              recv_sem=recv_sems.at[k], device_id=peer, device_id_type=MESH)
            cp.start()
            sends.append(cp)
        for k, (px, py, pc) in enumerate(peers):
            pltpu.make_async_remote_copy(src_ref=v_ref, dst_ref=buf.at[4 * px + 2 * py + pc], send_sem=send_sems.at[k],
                                         recv_sem=recv_sems.at[k], device_id=(px, py, pc),
                                         device_id_type=MESH).wait_recv()
        for cp in sends:
            cp.wait_send()
        acc = buf[0]
        for i in range(1, N_DEV):
            acc = acc + buf[i]
        out_ref[...] = acc

    vmem = pl.BlockSpec(memory_space=pltpu.VMEM)
    return pl.pallas_call(
        body, in_specs=[vmem], out_specs=vmem, out_shape=jax.ShapeDtypeStruct((rows, cols), F32),
        scratch_shapes=[pltpu.VMEM((N_DEV, rows, cols), F32), pltpu.SemaphoreType.DMA((N_DEV - 1,)),
                        pltpu.SemaphoreType.DMA((N_DEV - 1,))],
        name=name, compiler_params=pltpu.CompilerParams(),
    )(v)


def _reduce_scatter(g, ids, layer, n_layers, prev, tag):
    other = _sibling_halves(g, f"rs_sibling_{tag}")
    q = _chip_sum(g, other, ids, f"rs_chipsum_{tag}")
    b = _scatter_chips(q, f"rs_scatter_{tag}")
    f = _slot_sum(b, q, ids, layer, n_layers, prev, f"rs_slotsum_{tag}")
    return _join_halves(f, layer, f"rs_join_{tag}")


def _split_w_in(wg, Hf):
    ns, D, cols = wg.shape
    nat = wg.transpose(1, 0, 2).reshape(D, ns * cols)
    a = 3 * Hf * HEAD_DIM
    return jnp.concatenate([nat[:, :a], nat[:, a + Hf:]], axis=1), nat[:, a:a + Hf].T


def _join_dw_in(dw6, dwf_t, Hf):
    D = dw6.shape[0]
    a = 3 * Hf * HEAD_DIM
    nat = jnp.concatenate([dw6[:, :a], dwf_t.T, dw6[:, a:]], axis=1)
    return nat.reshape(D, N_CHIPS, nat.shape[1] // N_CHIPS).transpose(1, 0, 2)


def _layer_fwd(x, p, bias, tag):
    Hf, Hd = p["forget_b"].shape[0], bias.shape[1]
    h1 = _rms_fwd(x, p["norm1_g"], f"norm1_{tag}")
    proj = _mm_nn(h1, p["w6"], f"proj_{tag}", [CDT], epi=lambda acc: (acc,))[0]
    f_t = _mm_nt(p["wf_t"], h1, f"fproj_{tag}", [F32])[0]
    qc, kc = _fox_bias_operands(_gates_fwd(f_t, p["forget_b"], f"gates_{tag}"))
    y_a, lse_a = _fox_fwd(proj, qc, kc, Hf, f"fox_{tag}")
    outs, lses = [], []
    for i, (_, d) in enumerate(DIL_PATTERNS):
        o, l = _dil_fwd(proj, bias[i], d, Hd, 3 * Hf, f"dil{i}_{tag}")
        outs.append(o)
        lses.append(l)
    y_b, lse_b = _dil_combine(outs, lses, f"dilmix_{tag}")
    mixed = jnp.concatenate([_rms_fwd(y_a, p["outnorm_a_g"], f"norm_a_{tag}"),
                             _rms_fwd(y_b, p["outnorm_b_g"], f"norm_b_{tag}")], axis=1)
    x1 = _mm_nn(mixed, p["w_out"], f"attn_out_{tag}", [F32], extras=[x])[0]
    h2 = _rms_fwd(x1, p["norm2_g"], f"norm2_{tag}")
    u, act = _mm_nn(h2, p["w_mi"], f"mlp_in_{tag}", [CDT, CDT], b_slots=True,
                    epi=lambda acc: (acc, jnp.square(jnp.maximum(acc, 0.0))))
    x2 = _mm_nn(act, p["w_mo"], f"mlp_out_{tag}", [F32], extras=[x1])[0]
    saved = dict(x=x, h1=h1, proj=proj, f_t=f_t, qc=qc, kc=kc, y_a=y_a, lse_a=lse_a, y_b=y_b,
                 lse_b=lse_b, mixed=mixed, x1=x1, h2=h2, u=u, act=act)
    return x2, saved


def _layer_bwd(dx2, dx2c, p, bias, sv, tag):
    Hf, Hd = p["forget_b"].shape[0], bias.shape[1]
    E = HEAD_DIM
    du = _mm_nt(dx2c, p["w_mo"], f"d_act_{tag}", [CDT], extras=[sv["u"]],
                epi=lambda acc, u: (acc * (2.0 * jnp.maximum(u.astype(F32), 0.0)),))[0]
    g_w_mo = _mm_tn(sv["act"], dx2c, f"dw_mlp_out_{tag}")
    dh2 = _mm_nt(du, p["w_mi"], f"d_h2_{tag}", [F32], b_slots=True)[0]
    g_w_mi = _mm_tn(sv["h2"], du, f"dw_mlp_in_{tag}", out_slots=N_CHIPS)
    dx1, dx1c, g_norm2 = _rms_bwd(sv["x1"], p["norm2_g"], dh2, dx2, f"d_norm2_{tag}")
    dmixed = _mm_nt(dx1c, p["w_out"], f"d_mixed_{tag}", [F32])[0]
    g_w_out = _mm_tn(sv["mixed"], dx1c, f"dw_out_{tag}")
    dy_a, _, g_na = _rms_bwd(sv["y_a"], p["outnorm_a_g"], dmixed, None, f"d_norm_a_{tag}", dh_col=0)
    dy_b, _, g_nb = _rms_bwd(sv["y_b"], p["outnorm_b_g"], dmixed, None, f"d_norm_b_{tag}", dh_col=1)
    proj = sv["proj"]
    dq_a, dcq, dk_a, dv_a, dck = _fox_bwd(proj, sv["qc"], sv["kc"], sv["lse_a"], sv["y_a"], dy_a, Hf, f"fox_bwd_{tag}")
    df, dfc, g_fb = _gates_bwd(sv["f_t"], p["forget_b"], dcq[:, ::E].T, dck.reshape(Hf, -1), f"d_gates_{tag}")
    parts, dbias = [], []
    for i, (_, d) in enumerate(DIL_PATTERNS):
        dq, dk, dv, db = _dil_bwd(proj, bias[i], sv["y_b"], dy_b, sv["lse_b"], d, Hd, 3 * Hf, f"dil{i}_bwd_{tag}")
        parts.append((dq, dk, dv))
        dbias.append(db)
    dqkv_b = [_sum_cast([parts[i][j] for i in range(len(DIL_PATTERNS))], f"dil_sum{j}_{tag}") for j in range(3)]
    dproj = jnp.concatenate([dq_a, dk_a, dv_a] + dqkv_b, axis=1)
    dh1_f = _mm_tn(dfc, p["wf_t"], f"d_h1_f_{tag}")
    dh1 = _mm_nt(dproj, p["w6"], f"d_h1_{tag}", [F32], extras=[dh1_f])[0]
    g_w6 = _mm_tn(sv["h1"], dproj, f"dw_in_{tag}")
    g_wf_t = _mm_nn(dfc, sv["h1"], f"dw_f_{tag}", [F32])[0]
    dx, dxc, g_norm1 = _rms_bwd(sv["x"], p["norm1_g"], dh1, dx1, f"d_norm1_{tag}")
    grads = dict(w_in=_join_dw_in(g_w6, g_wf_t, Hf), w_out=g_w_out.reshape(N_CHIPS, -1, g_w_out.shape[1]),
                 w_mlp_in=g_w_mi, w_mlp_out=g_w_mo.reshape(N_CHIPS, -1, g_w_mo.shape[1]),
                 norm1_g=g_norm1[0], norm2_g=g_norm2[0], outnorm_a_g=g_na[0], outnorm_b_g=g_nb[0],
                 forget_b=g_fb[:, 0], dbias=jnp.stack(dbias))
    return dx, dxc, grads


def _local_step(x, target, small, gathered):
    depth = len(gathered)
    Hf = small["forget_b"].shape[1]
    buckets = _bucket_table()
    bias = _bias_table(small["rel_bias"], buckets, "bias_table")
    layers, saved = [], []
    for l in range(depth):
        g = gathered[l]
        w6, wf_t = _split_w_in(g["w_in"], Hf)
        p = dict(w6=w6, wf_t=wf_t, w_out=g["w_out"].reshape(-1, g["w_out"].shape[2]), w_mi=g["w_mlp_in"],
                 w_mo=g["w_mlp_out"].reshape(-1, g["w_mlp_out"].shape[2]),
                 **{k: small[k][l] for k in ("norm1_g", "forget_b", "outnorm_a_g", "outnorm_b_g", "norm2_g")})
        layers.append(p)
        x, sv = _layer_fwd(x, p, bias, f"l{l}")
        saved.append(sv)
    dx, dxc, g_final, loss = _loss_bwd(x, small["final_norm_g"], target, "loss")
    layer_grads = [None] * depth
    for l in reversed(range(depth)):
        dx, dxc, layer_grads[l] = _layer_bwd(dx, dxc, layers[l], bias, saved[l], f"l{l}")
    dbias = functools.reduce(jnp.add, [g["dbias"] for g in layer_grads])
    g_rel = _bias_table_bwd(dbias, buckets, "d_bias_table")[:, 0, :].T
    small_grads = dict(final_norm_g=g_final[0], rel_bias=g_rel,
                       **{k: jnp.stack([g[k] for g in layer_grads])
                          for k in ("norm1_g", "forget_b", "outnorm_a_g", "outnorm_b_g", "norm2_g")})
    return loss[0, 0], dx, layer_grads, small_grads


_BIG = ("w_in", "w_out", "w_mlp_in", "w_mlp_out")
_SMALL = ("norm1_g", "forget_b", "rel_bias", "outnorm_a_g", "outnorm_b_g", "norm2_g", "final_norm_g")
_ORDER = ("norm1_g", "w_in", "forget_b", "rel_bias", "outnorm_a_g", "outnorm_b_g", "w_out", "norm2_g", "w_mlp_in",
          "w_mlp_out", "final_norm_g")


def _pack_small(d):
    flat = jnp.concatenate([d[k].reshape(-1) for k in _SMALL])
    rows = -(-flat.shape[0] // (8 * SMALL_COLS)) * 8
    return jnp.pad(flat, (0, rows * SMALL_COLS - flat.shape[0])).reshape(rows, SMALL_COLS)


def _unpack_small(packed, like):
    flat, out, at = packed.reshape(-1), {}, 0
    for k in _SMALL:
        n = like[k].size
        out[k] = flat[at:at + n].reshape(like[k].shape)
        at += n
    return out


def kernel(x, norm1_g, w_in, forget_b, rel_bias, outnorm_a_g, outnorm_b_g, w_out, norm2_g, w_mlp_in, w_mlp_out, final_norm_g, loss_target, m_norm1_g, m_w_in, m_forget_b, m_rel_bias, m_outnorm_a_g, m_outnorm_b_g, m_w_out, m_norm2_g, m_w_mlp_in, m_w_mlp_out, m_final_norm_g, v_norm1_g, v_w_in, v_forget_b, v_rel_bias, v_outnorm_a_g, v_outnorm_b_g, v_w_out, v_norm2_g, v_w_mlp_in, v_w_mlp_out, v_final_norm_g):
    w = dict(norm1_g=norm1_g, w_in=w_in, forget_b=forget_b, rel_bias=rel_bias, outnorm_a_g=outnorm_a_g,
             outnorm_b_g=outnorm_b_g, w_out=w_out, norm2_g=norm2_g, w_mlp_in=w_mlp_in, w_mlp_out=w_mlp_out,
             final_norm_g=final_norm_g)
    m = dict(norm1_g=m_norm1_g, w_in=m_w_in, forget_b=m_forget_b, rel_bias=m_rel_bias, outnorm_a_g=m_outnorm_a_g,
             outnorm_b_g=m_outnorm_b_g, w_out=m_w_out, norm2_g=m_norm2_g, w_mlp_in=m_w_mlp_in,
             w_mlp_out=m_w_mlp_out, final_norm_g=m_final_norm_g)
    v = dict(norm1_g=v_norm1_g, w_in=v_w_in, forget_b=v_forget_b, rel_bias=v_rel_bias, outnorm_a_g=v_outnorm_a_g,
             outnorm_b_g=v_outnorm_b_g, w_out=v_w_out, norm2_g=v_norm2_g, w_mlp_in=v_w_mlp_in,
             w_mlp_out=v_w_mlp_out, final_norm_g=v_final_norm_g)
    depth = w_in.shape[0]
    small = {k: w[k] for k in _SMALL}
    ids = jnp.stack([2 * lax.axis_index("x") + lax.axis_index("y"), lax.axis_index("c")]).astype(jnp.int32)

    gathered = [{k: _all_gather_chips(_cast_into_slot(w[k][l], ids, f"cast_{k}_l{l}"), f"gather_{k}_l{l}")
                 for k in _BIG} for l in range(depth)]
    loss, grad_x, layer_grads, small_grads = _local_step(x[0], loss_target[0], small, gathered)
    loss = lax.psum(loss, ("x", "y", "c"))

    grads = {}
    for k in _BIG:
        acc = None
        for l in reversed(range(depth)):
            acc = _reduce_scatter(layer_grads[l][k], ids, l, depth, acc, f"{k}_l{l}")
        grads[k] = acc
    grads.update(_unpack_small(_all_reduce_small(_pack_small(small_grads), "small_all_reduce"), small))

    delta, new_m, new_v = {}, {}, {}
    for k in _BIG:
        shape = w[k].shape
        flat = lambda t: t.reshape(-1, shape[-1])
        d_, m_, v_ = _adamw(flat(w[k]), flat(grads[k]), flat(m[k]), flat(v[k]), f"adamw_{k}")
        delta[k], new_m[k], new_v[k] = d_.reshape(shape), m_.reshape(shape), v_.reshape(shape)
    d_, m_, v_ = _adamw(_pack_small(small), _pack_small({k: grads[k] for k in _SMALL}),
                        _pack_small({k: m[k] for k in _SMALL}), _pack_small({k: v[k] for k in _SMALL}), "adamw_small")
    delta.update(_unpack_small(d_, small))
    new_m.update(_unpack_small(m_, small))
    new_v.update(_unpack_small(v_, small))

    return (loss, grad_x[None], *[grads[k] for k in _ORDER], *[delta[k] for k in _ORDER],
            *[new_m[k] for k in _ORDER], *[new_v[k] for k in _ORDER])
```

```python
import functools

import jax
import jax.numpy as jnp
from jax import lax
from jax.experimental import pallas as pl
from jax.experimental.pallas import tpu as pltpu

F32 = jnp.float32
CDT = jnp.bfloat16
HEAD_DIM = 128
NORM_EPS = 1e-6
NEG_INF = -1e30
LOG2E = 1.4426950408889634
REL_BUCKETS = 32
REL_MAX_DISTANCE = 2048
DIL_PATTERNS = ((128, 1), (512, 4), (2048, 16))
DIL_BLOCK = 128
ADAM_LR, ADAM_B1, ADAM_B2, ADAM_EPS, ADAM_WD, ADAM_STEP = 0.001, 0.9, 0.999, 1e-08, 0.01, 10
N_CHIPS = 4
N_DEV = 8
VMEM_LIMIT_BYTES = 56 * 1024 * 1024
SMALL_COLS = 1024
MESH = pl.DeviceIdType.MESH


def _cparams(sem=None):
    return pltpu.CompilerParams(dimension_semantics=sem, vmem_limit_bytes=VMEM_LIMIT_BYTES)


def _tile(dim, pref):
    t = min(pref, dim)
    t -= t % 128
    while t >= 128:
        if dim % t == 0:
            return t
        t -= 128
    return dim


def _rowwise(fn, ins, out_dtypes, name, bs=256, consts=()):
    R, C = ins[0].shape
    bs = min(bs, R)
    n_in, n_c = len(ins), len(consts)

    def body(*refs):
        vals = [r[...] for r in refs[:n_in + n_c]]
        res = fn(*vals)
        for o, r in zip(refs[n_in + n_c:], res):
            o[...] = r.astype(o.dtype)

    row = pl.BlockSpec((bs, C), lambda i: (i, 0))
    return pl.pallas_call(
        body, grid=(R // bs,),
        in_specs=[row] * n_in + [pl.BlockSpec((1, c.shape[-1]), lambda i: (0, 0)) for c in consts],
        out_specs=[row] * len(out_dtypes),
        out_shape=[jax.ShapeDtypeStruct((R, C), d) for d in out_dtypes],
        name=name, compiler_params=_cparams(("parallel",)),
    )(*ins, *[c.reshape(1, -1) for c in consts])


def _rms_fwd(x, g, name):
    def fn(xf, gg):
        r = lax.rsqrt(jnp.mean(xf * xf, axis=-1, keepdims=True) + NORM_EPS)
        return ((xf * r) * gg,)
    return _rowwise(fn, [x], [CDT], name, consts=[g])[0]


def _rms_bwd(x, g, dh, dres, name, bs=256, dh_col=0):
    S, D = x.shape
    bs = min(bs, S)
    has_res = dres is not None

    def body(*refs):
        x_ref, g_ref, dh_ref = refs[:3]
        dx_ref, dxc_ref, dg_ref = refs[-3:]
        xf = x_ref[...]
        r = lax.rsqrt(jnp.mean(xf * xf, axis=-1, keepdims=True) + NORM_EPS)
        xhat = xf * r
        dh_ = dh_ref[...].astype(F32)
        dxhat = dh_ * g_ref[...]
        dx = r * (dxhat - xhat * jnp.mean(dxhat * xhat, axis=-1, keepdims=True))
        if has_res:
            dx = dx + refs[3][...]
        dx_ref[...] = dx
        dxc_ref[...] = dx.astype(dxc_ref.dtype)
        part = jnp.sum(dh_ * xhat, axis=0, keepdims=True)

        @pl.when(pl.program_id(0) == 0)
        def _():
            dg_ref[...] = part

        @pl.when(pl.program_id(0) > 0)
        def _():
            dg_ref[...] += part

    row = pl.BlockSpec((bs, D), lambda i: (i, 0))
    one = pl.BlockSpec((1, D), lambda i: (0, 0))
    ins = [x, g.reshape(1, D), dh] + ([dres] if has_res else [])
    return pl.pallas_call(
        body, grid=(S // bs,),
        in_specs=[row, one, pl.BlockSpec((bs, D), lambda i: (i, dh_col))] + ([row] if has_res else []),
        out_specs=[row, row, one],
        out_shape=[jax.ShapeDtypeStruct((S, D), F32), jax.ShapeDtypeStruct((S, D), CDT),
                   jax.ShapeDtypeStruct((1, D), F32)],
        name=name, compiler_params=_cparams(("arbitrary",)),
    )(*ins)


def _loss_bwd(x, g, target, name, bs=256):
    S, D = x.shape
    bs = min(bs, S)

    def body(x_ref, g_ref, t_ref, dx_ref, dxc_ref, dg_ref, loss_ref):
        xf = x_ref[...]
        r = lax.rsqrt(jnp.mean(xf * xf, axis=-1, keepdims=True) + NORM_EPS)
        xhat = xf * r
        err = xhat * g_ref[...] - t_ref[...]
        lpart = 0.5 * jnp.sum(jnp.mean(err * err, axis=-1, keepdims=True), axis=0, keepdims=True)
        dy = err / D
        dxhat = dy * g_ref[...]
        dx = r * (dxhat - xhat * jnp.mean(dxhat * xhat, axis=-1, keepdims=True))
        dx_ref[...] = dx
        dxc_ref[...] = dx.astype(dxc_ref.dtype)
        gpart = jnp.sum(dy * xhat, axis=0, keepdims=True)

        @pl.when(pl.program_id(0) == 0)
        def _():
            dg_ref[...] = gpart
            loss_ref[...] = lpart

        @pl.when(pl.program_id(0) > 0)
        def _():
            dg_ref[...] += gpart
            loss_ref[...] += lpart

    row = pl.BlockSpec((bs, D), lambda i: (i, 0))
    one = pl.BlockSpec((1, D), lambda i: (0, 0))
    return pl.pallas_call(
        body, grid=(S // bs,),
        in_specs=[row, one, row],
        out_specs=[row, row, one, pl.BlockSpec((1, 1), lambda i: (0, 0))],
        out_shape=[jax.ShapeDtypeStruct((S, D), F32), jax.ShapeDtypeStruct((S, D), CDT),
                   jax.ShapeDtypeStruct((1, D), F32), jax.ShapeDtypeStruct((1, 1), F32)],
        name=name, compiler_params=_cparams(("arbitrary",)),
    )(x, g.reshape(1, D), target)


_NN = (((1,), (0,)), ((), ()))
_NT = (((1,), (1,)), ((), ()))
_TN = (((0,), (0,)), ((), ()))


def _mm(a, b, *, M, N, K, a_spec, b_spec, o_spec, dims, tm, tn, tk, name, out_shapes, extras=(), epi=None):
    nk = K // tk
    n_ex, n_out = len(extras), len(out_shapes)
    in_place = epi is None
    if in_place:
        assert n_out == 1 and n_ex <= 1 and out_shapes[0].dtype == F32
        epi = lambda acc, *r: (acc + r[0] if r else acc,)

    def body(*refs):
        a_ref, b_ref = refs[0], refs[1]
        ex = refs[2:2 + n_ex]
        outs = refs[2 + n_ex:2 + n_ex + n_out]
        part = lax.dot_general(a_ref[...], b_ref[...], dims, preferred_element_type=F32)

        def finish(acc):
            for o, r in zip(outs, epi(acc, *[e[...] for e in ex])):
                o[...] = r.astype(o.dtype)

        if nk == 1:
            finish(part)
        elif in_place:
            k = pl.program_id(2)

            @pl.when(k == 0)
            def _():
                finish(part)

            @pl.when(k > 0)
            def _():
                outs[0][...] += part
        else:
            acc_ref = refs[-1]
            k = pl.program_id(2)

            @pl.when(k == 0)
            def _():
                acc_ref[...] = part

            @pl.when(k > 0)
            def _():
                acc_ref[...] += part

            @pl.when(k == nk - 1)
            def _():
                finish(acc_ref[...])

    ex_spec = pl.BlockSpec((tm, tn), lambda i, j, k: (i, j))
    return pl.pallas_call(
        body, grid=(M // tm, N // tn, nk),
        in_specs=[a_spec, b_spec] + [ex_spec] * n_ex,
        out_specs=[o_spec] * n_out,
        out_shape=out_shapes,
        scratch_shapes=[pltpu.VMEM((tm, tn), F32)] if nk > 1 and not in_place else [],
        name=name, compiler_params=_cparams(("parallel", "parallel", "arbitrary")),
    )(a, b, *extras)


def _mm_tiles(K):
    return (2048, 512, 2048) if K <= 2048 else (1024, 1024, 2048)


def _mm_nn(a, b, name, out_dtypes, extras=(), epi=None, b_slots=False):
    M, K = a.shape
    tm, tn, tk = _mm_tiles(K)
    if b_slots:
        ns, _, Ns = b.shape
        N = ns * Ns
        tn = _tile(Ns, tn)
        npb = Ns // tn
        tk_ = _tile(K, tk)
        b_spec = pl.BlockSpec((None, tk_, tn), lambda i, j, k: (j // npb, k, j % npb))
    else:
        N = b.shape[1]
        tn = _tile(N, tn)
        tk_ = _tile(K, tk)
        b_spec = pl.BlockSpec((tk_, tn), lambda i, j, k: (k, j))
    tm = _tile(M, tm)
    return _mm(a, b, M=M, N=N, K=K, a_spec=pl.BlockSpec((tm, tk_), lambda i, j, k: (i, k)), b_spec=b_spec,
               o_spec=pl.BlockSpec((tm, tn), lambda i, j, k: (i, j)), dims=_NN, tm=tm, tn=tn, tk=tk_, name=name,
               out_shapes=[jax.ShapeDtypeStruct((M, N), d) for d in out_dtypes], extras=extras, epi=epi)


def _mm_nt(a, b, name, out_dtypes, extras=(), epi=None, b_slots=False):
    M, K = a.shape
    tm, tn, tk = _mm_tiles(K)
    tm = _tile(M, tm)
    if b_slots:
        ns, N, Ks = b.shape
        tk_ = _tile(Ks, tk)
        kpb = Ks // tk_
        tn = _tile(N, tn)
        b_spec = pl.BlockSpec((None, tn, tk_), lambda i, j, k: (k // kpb, j, k % kpb))
    else:
        N = b.shape[0]
        tk_ = _tile(K, tk)
        tn = _tile(N, tn)
        b_spec = pl.BlockSpec((tn, tk_), lambda i, j, k: (j, k))
    return _mm(a, b, M=M, N=N, K=K, a_spec=pl.BlockSpec((tm, tk_), lambda i, j, k: (i, k)), b_spec=b_spec,
               o_spec=pl.BlockSpec((tm, tn), lambda i, j, k: (i, j)), dims=_NT, tm=tm, tn=tn, tk=tk_, name=name,
               out_shapes=[jax.ShapeDtypeStruct((M, N), d) for d in out_dtypes], extras=extras, epi=epi)


def _mm_tn(a, b, name, out_dtype, out_slots=0, tm=2048, tn=1024, tk=2048):
    K, M = a.shape
    N = b.shape[1]
    tm, tk_ = _tile(M, tm), _tile(K, tk)
    if out_slots:
        Ns = N // out_slots
        tn = _tile(Ns, tn)
        npb = Ns // tn
        o_spec = pl.BlockSpec((None, tm, tn), lambda i, j, k: (j // npb, i, j % npb))
        out_shape = jax.ShapeDtypeStruct((out_slots, M, Ns), out_dtype)
    else:
        tn = _tile(N, tn)
        o_spec = pl.BlockSpec((tm, tn), lambda i, j, k: (i, j))
        out_shape = jax.ShapeDtypeStruct((M, N), out_dtype)
    return _mm(a, b, M=M, N=N, K=K, a_spec=pl.BlockSpec((tk_, tm), lambda i, j, k: (k, i)),
               b_spec=pl.BlockSpec((tk_, tn), lambda i, j, k: (k, j)), o_spec=o_spec, dims=_TN,
               tm=tm, tn=tn, tk=tk_, name=name, out_shapes=[out_shape],
               epi=None if out_dtype == F32 else (lambda acc: (acc,)))[0]


GATE_BLOCK = 512


def _split3(v):
    hi = v.astype(jnp.bfloat16)
    r1 = v - hi.astype(F32)
    mid = r1.astype(jnp.bfloat16)
    lo = (r1 - mid.astype(F32)).astype(jnp.bfloat16)
    return hi, mid, lo


def _exact_dot(v, tri):
    return functools.reduce(jnp.add, [jnp.dot(t, tri, preferred_element_type=F32) for t in _split3(v)])


def _gates_fwd(f_t, b, name):
    H, S = f_t.shape
    nb = _tile(S, GATE_BLOCK)
    inv_scale = HEAD_DIM ** 0.5

    def body(f_ref, b_ref, c_ref):
        upper = (lax.broadcasted_iota(jnp.int32, (nb, nb), 0)
                 <= lax.broadcasted_iota(jnp.int32, (nb, nb), 1)).astype(jnp.bfloat16)
        carry = jnp.zeros((H, 1), F32)
        for i in range(S // nb):
            z = f_ref[:, i * nb:(i + 1) * nb] + b_ref[...]
            logf = jnp.minimum(z, 0.0) - jnp.log1p(jnp.exp(-jnp.abs(z)))
            cs = _exact_dot(logf, upper) + carry
            for j, t in enumerate(_split3(cs * inv_scale)):
                c_ref[j, :, i * nb:(i + 1) * nb] = t
            carry = cs[:, nb - 1:nb]

    return pl.pallas_call(body, out_shape=jax.ShapeDtypeStruct((3, H, S), jnp.bfloat16), name=name,
                          compiler_params=_cparams())(f_t, b.reshape(H, 1))


def _gates_bwd(f_t, b, dcq, dck, name):
    H, S = f_t.shape
    nb = _tile(S, GATE_BLOCK)

    def body(f_ref, b_ref, dcq_ref, dck_ref, df_ref, dfc_ref, db_ref):
        lower = (lax.broadcasted_iota(jnp.int32, (nb, nb), 0)
                 >= lax.broadcasted_iota(jnp.int32, (nb, nb), 1)).astype(jnp.bfloat16)
        carry = jnp.zeros((H, 1), F32)
        db = jnp.zeros((H, 1), F32)
        for i in reversed(range(S // nb)):
            sl = slice(i * nb, (i + 1) * nb)
            dc = dcq_ref[:, sl] - dck_ref[:, sl]
            dlogf = _exact_dot(dc, lower) + carry
            carry = dlogf[:, 0:1]
            z = f_ref[:, sl] + b_ref[...]
            df = dlogf / (1.0 + jnp.exp(z))
            df_ref[:, sl] = df
            dfc_ref[:, sl] = df.astype(dfc_ref.dtype)
            db = db + jnp.sum(df, axis=1, keepdims=True)
        db_ref[...] = db

    return pl.pallas_call(
        body, out_shape=[jax.ShapeDtypeStruct((H, S), F32), jax.ShapeDtypeStruct((H, S), CDT),
                         jax.ShapeDtypeStruct((H, 1), F32)],
        name=name, compiler_params=_cparams())(f_t, b.reshape(H, 1), dcq, dck)


FOX_BLOCK = 512


def _fox_bias_operands(csplit):
    parts = csplit.transpose(2, 1, 0)
    S, H, _ = parts.shape
    ones = jnp.ones_like(parts)
    zeros = jnp.zeros((S, H, HEAD_DIM - 6), parts.dtype)
    qc = jnp.concatenate([parts, ones, zeros], axis=-1).reshape(S, H * HEAD_DIM)
    kc = jnp.concatenate([ones, -parts, zeros], axis=-1).reshape(S, H * HEAD_DIM)
    return qc, kc


def _fox_logits2(q_ref, qc_ref, k_ref, kc_ref, diag):
    q, k = q_ref[...], k_ref[...]
    qa = jnp.concatenate([q, qc_ref[...].astype(q.dtype)], axis=1)
    ka = jnp.concatenate([k, kc_ref[...].astype(k.dtype)], axis=1)
    s = lax.dot_general(qa, ka, _NT, preferred_element_type=F32) * (HEAD_DIM ** -0.5 * LOG2E)
    if diag:
        row = lax.broadcasted_iota(jnp.int32, s.shape, 0)
        col = lax.broadcasted_iota(jnp.int32, s.shape, 1)
        s = jnp.where(col <= row, s, NEG_INF)
    return s


def _fox_fwd(proj, qc, kc, H, name):
    S = proj.shape[0]
    E = HEAD_DIM
    blk = _tile(S, FOX_BLOCK)
    nq = S // blk

    def body(q_ref, qc_ref, k_ref, kc_ref, v_ref, o_ref, lse_ref, m_s, l_s, acc_s):
        qi, kj = pl.program_id(1), pl.program_id(2)

        @pl.when(kj == 0)
        def _():
            m_s[...] = jnp.full(m_s.shape, NEG_INF, F32)
            l_s[...] = jnp.zeros(l_s.shape, F32)
            acc_s[...] = jnp.zeros(acc_s.shape, F32)

        def step(diag):
            s = _fox_logits2(q_ref, qc_ref, k_ref, kc_ref, diag)
            m_prev = m_s[...]
            m_new = jnp.maximum(m_prev, jnp.max(s, axis=-1, keepdims=True))
            alpha = jnp.exp2(m_prev - m_new)
            p = jnp.exp2(s - m_new)
            l_s[...] = alpha * l_s[...] + jnp.sum(p, axis=-1, keepdims=True)
            acc_s[...] = alpha * acc_s[...] + jnp.dot(p.astype(CDT), v_ref[...], preferred_element_type=F32)
            m_s[...] = m_new

        pl.when(kj < qi)(lambda: step(False))
        pl.when(kj == qi)(lambda: step(True))

        @pl.when(kj == nq - 1)
        def _():
            o_ref[...] = acc_s[...] / l_s[...]
            lse_ref[...] = jnp.broadcast_to(m_s[...] + jnp.log2(l_s[...]), lse_ref.shape)

    qspec = lambda off: pl.BlockSpec((blk, E), lambda h, i, j: (i, off + h))
    kspec = lambda off: pl.BlockSpec((blk, E), lambda h, i, j: (jnp.minimum(j, i), off + h))
    return pl.pallas_call(
        body, grid=(H, nq, nq),
        in_specs=[qspec(0), qspec(0), kspec(H), kspec(0), kspec(2 * H)],
        out_specs=[qspec(0)] * 2,
        out_shape=[jax.ShapeDtypeStruct((S, H * E), F32)] * 2,
        scratch_shapes=[pltpu.VMEM((blk, 1), F32), pltpu.VMEM((blk, 1), F32), pltpu.VMEM((blk, E), F32)],
        name=name, compiler_params=_cparams(("parallel", "parallel", "arbitrary")),
    )(proj, qc, proj, kc, proj)


def _fox_bwd(proj, qc, kc, lse, o, do, H, name):
    S = proj.shape[0]
    E = HEAD_DIM
    blk = _tile(S, FOX_BLOCK)
    nq = S // blk
    scale = E ** -0.5

    def body(q_ref, qc_ref, k_ref, kc_ref, v_ref, lse_ref, o_ref, do_ref,
             dq_ref, dcq_ref, dk_ref, dv_ref, dck_ref, dq_s, dcq_s, dk_s, dv_s, dck_s):
        kj, qi = pl.program_id(1), pl.program_id(2)

        @pl.when(qi == 0)
        def _():
            dk_s[...] = jnp.zeros(dk_s.shape, F32)
            dv_s[...] = jnp.zeros(dv_s.shape, F32)
            dck_s[...] = jnp.zeros(dck_s.shape, F32)

        def step(diag):
            do = do_ref[...]
            doc = do.astype(CDT)
            delta = jnp.sum(do * o_ref[...], axis=-1, keepdims=True)
            p = jnp.exp2(_fox_logits2(q_ref, qc_ref, k_ref, kc_ref, diag) - lse_ref[:, 0:1])
            dp = lax.dot_general(doc, v_ref[...], _NT, preferred_element_type=F32)
            ds = p * (dp - delta)
            dss = ds * scale
            dck_s[...] += jnp.sum(ds, axis=0, keepdims=True)
            dv_s[...] += jnp.dot(p.T.astype(CDT), doc, preferred_element_type=F32)
            dk_s[...] += jnp.dot(dss.T.astype(CDT), q_ref[...], preferred_element_type=F32)
            dq_part = jnp.dot(dss.astype(CDT), k_ref[...], preferred_element_type=F32)
            dc_part = jnp.sum(ds, axis=-1, keepdims=True)
            rows = pl.ds(pl.multiple_of(qi * blk, blk), blk)

            @pl.when(kj == 0)
            def _():
                dq_s[rows, :] = dq_part
                dcq_s[rows, :] = dc_part

            @pl.when(kj > 0)
            def _():
                dq_s[rows, :] += dq_part
                dcq_s[rows, :] += dc_part

        pl.when(qi > kj)(lambda: step(False))
        pl.when(qi == kj)(lambda: step(True))

        @pl.when(qi == nq - 1)
        def _():
            dk_ref[...] = dk_s[...].astype(dk_ref.dtype)
            dv_ref[...] = dv_s[...].astype(dv_ref.dtype)
            dck_ref[...] = dck_s[...].reshape(dck_ref.shape)

        @pl.when((qi == nq - 1) & (kj == nq - 1))
        def _():
            dq_ref[...] = dq_s[...].astype(dq_ref.dtype)
            dcq_ref[...] = jnp.broadcast_to(dcq_s[...], dcq_ref.shape)

    qspec = lambda off: pl.BlockSpec((blk, E), lambda h, j, i: (jnp.maximum(i, j), off + h))
    kspec = lambda off: pl.BlockSpec((blk, E), lambda h, j, i: (j, off + h))
    head = pl.BlockSpec((S, E), lambda h, j, i: (0, h))
    return pl.pallas_call(
        body, grid=(H, nq, nq),
        in_specs=[qspec(0), qspec(0), kspec(H), kspec(0), kspec(2 * H), qspec(0), qspec(0), qspec(0)],
        out_specs=[head, head, kspec(0), kspec(0), pl.BlockSpec((1, 1, blk), lambda h, j, i: (h, 0, j))],
        out_shape=[jax.ShapeDtypeStruct((S, H * E), CDT), jax.ShapeDtypeStruct((S, H * E), F32),
                   jax.ShapeDtypeStruct((S, H * E), CDT), jax.ShapeDtypeStruct((S, H * E), CDT),
                   jax.ShapeDtypeStruct((H, 1, S), F32)],
        scratch_shapes=[pltpu.VMEM((S, E), F32), pltpu.VMEM((S, 1), F32), pltpu.VMEM((blk, E), F32),
                        pltpu.VMEM((blk, E), F32), pltpu.VMEM((1, blk), F32)],
        name=name, compiler_params=_cparams(("parallel", "arbitrary", "arbitrary")),
    )(proj, qc, proj, kc, proj, lse, o, do)


DIL_ROWS = 512


def _rel_bucket(dist):
    max_exact = REL_BUCKETS // 2
    d = jnp.maximum(dist.astype(F32), 1.0)
    large = max_exact + (jnp.log(d / max_exact) / jnp.log(jnp.float32(REL_MAX_DISTANCE / max_exact))
                         * (REL_BUCKETS - max_exact)).astype(jnp.int32)
    large = jnp.minimum(large, REL_BUCKETS - 1)
    return jnp.where(dist < max_exact, dist, large)


def _bucket_table():
    i = jnp.arange(DIL_BLOCK)[:, None]
    j = jnp.arange(2 * DIL_BLOCK)[None, :]
    rel = DIL_BLOCK + i - j
    tabs = [_rel_bucket(jnp.clip(rel, 0, w // d) * d) for w, d in DIL_PATTERNS]
    return jnp.stack(tabs).astype(jnp.int32)


def _bias_table(rel_bias, buckets, name):
    P = buckets.shape[0]
    H = rel_bias.shape[1]

    def body(rb_ref, bk_ref, out_ref):
        h = pl.program_id(1)
        bk = bk_ref[0]
        val = jnp.zeros(bk.shape, F32)
        for b in range(REL_BUCKETS):
            val = jnp.where(bk == b, rb_ref[b, h], val)
        out_ref[0, 0] = val

    return pl.pallas_call(
        body, grid=(P, H),
        in_specs=[pl.BlockSpec(memory_space=pltpu.SMEM),
                  pl.BlockSpec((1, DIL_BLOCK, 2 * DIL_BLOCK), lambda p, h: (p, 0, 0))],
        out_specs=pl.BlockSpec((1, 1, DIL_BLOCK, 2 * DIL_BLOCK), lambda p, h: (p, h, 0, 0)),
        out_shape=jax.ShapeDtypeStruct((P, H, DIL_BLOCK, 2 * DIL_BLOCK), F32),
        name=name, compiler_params=_cparams(("parallel", "parallel")),
    )(rel_bias, buckets)


def _bias_table_bwd(dbias, buckets, name):
    P, H = dbias.shape[:2]

    def body(db_ref, bk_ref, out_ref):
        lane = lax.broadcasted_iota(jnp.int32, (1, REL_BUCKETS), 1)
        acc = jnp.zeros((1, REL_BUCKETS), F32)
        bk = bk_ref[...]
        db = db_ref[:, 0]
        for b in range(REL_BUCKETS):
            tot = jnp.sum(jnp.where(bk == b, db, 0.0))
            acc = jnp.where(lane == b, tot, acc)
        out_ref[0] = acc

    return pl.pallas_call(
        body, grid=(H,),
        in_specs=[pl.BlockSpec((P, 1, DIL_BLOCK, 2 * DIL_BLOCK), lambda h: (0, h, 0, 0)),
                  pl.BlockSpec((P, DIL_BLOCK, 2 * DIL_BLOCK), lambda h: (0, 0, 0))],
        out_specs=pl.BlockSpec((1, 1, REL_BUCKETS), lambda h: (h, 0, 0)),
        out_shape=jax.ShapeDtypeStruct((H, 1, REL_BUCKETS), F32),
        name=name, compiler_params=_cparams(("parallel",)),
    )(dbias, buckets)


def _dil_masks(T, first):
    B = DIL_BLOCK
    ii = lax.broadcasted_iota(jnp.int32, (T, B, B), 1)
    jj = lax.broadcasted_iota(jnp.int32, (T, B, B), 2)
    tt = lax.broadcasted_iota(jnp.int32, (T, B, B), 0)
    return jj <= ii, (jj >= ii) & ((first + tt) > 0)


def _bdot(a, b, contract_b):
    return lax.dot_general(a, b, (((2,), (contract_b,)), ((0,), (0,))), preferred_element_type=F32)


def _dil_geometry(S, d):
    n_sub = S // d
    rows = _tile(n_sub, DIL_ROWS)
    return n_sub, rows, rows // DIL_BLOCK, n_sub // rows


def _dil_fwd(proj, bias, d, H, col0, name):
    S, C = proj.shape
    E = B = DIL_BLOCK
    n_sub, rows, T, nc = _dil_geometry(S, d)
    cb = C // E
    scale = E ** -0.5

    def body(q_ref, k_ref, v_ref, kh_ref, vh_ref, b_ref, o_ref, lse_ref):
        n = pl.program_id(2)
        q = q_ref[...].reshape(T, B, E)
        kc = k_ref[...].reshape(T, B, E)
        vc = v_ref[...].reshape(T, B, E)
        kp = jnp.concatenate([kh_ref[...][None], kc[:T - 1]], axis=0) if T > 1 else kh_ref[...][None]
        vp = jnp.concatenate([vh_ref[...][None], vc[:T - 1]], axis=0) if T > 1 else vh_ref[...][None]
        mask_c, mask_p = _dil_masks(T, n * T)
        bias_pc = b_ref[0]
        s_c = jnp.where(mask_c, _bdot(q, kc, 2) * scale + bias_pc[:, B:][None], NEG_INF)
        s_p = jnp.where(mask_p, _bdot(q, kp, 2) * scale + bias_pc[:, :B][None], NEG_INF)
        m = jnp.maximum(jnp.max(s_c, axis=-1, keepdims=True), jnp.max(s_p, axis=-1, keepdims=True))
        p_c = jnp.exp(s_c - m)
        p_p = jnp.exp(s_p - m)
        ssum = jnp.sum(p_c, axis=-1, keepdims=True) + jnp.sum(p_p, axis=-1, keepdims=True)
        o = (_bdot(p_c.astype(CDT), vc, 1) + _bdot(p_p.astype(CDT), vp, 1)) / ssum
        o_ref[...] = o.reshape(rows, E)
        lse_ref[...] = jnp.broadcast_to(m + jnp.log(ssum), (T, B, E)).reshape(rows, E)

    view = proj.reshape(n_sub, d * C)
    cur = lambda off: pl.BlockSpec((rows, E), lambda h, r, n: (n, r * cb + off + h))
    halo = lambda off: pl.BlockSpec((B, E), lambda h, r, n: (jnp.maximum(n * T - 1, 0), r * cb + off + h))
    ospec = pl.BlockSpec((rows, E), lambda h, r, n: (n, r * H + h))
    o, lse = pl.pallas_call(
        body, grid=(H, d, nc),
        in_specs=[cur(col0), cur(col0 + H), cur(col0 + 2 * H), halo(col0 + H), halo(col0 + 2 * H),
                  pl.BlockSpec((1, B, 2 * B), lambda h, r, n: (h, 0, 0))],
        out_specs=[ospec, ospec],
        out_shape=[jax.ShapeDtypeStruct((n_sub, d * H * E), F32)] * 2,
        name=name, compiler_params=_cparams(("parallel", "parallel", "parallel")),
    )(view, view, view, view, view, bias)
    return o.reshape(S, H * E), lse.reshape(S, H * E)


def _dil_bwd(proj, bias, y, dy, lse, d, H, col0, name):
    S, C = proj.shape
    E = B = DIL_BLOCK
    n_sub, rows, T, nc = _dil_geometry(S, d)
    cb = C // E
    scale = E ** -0.5

    def body(q_ref, k_ref, v_ref, kh_ref, vh_ref, qa_ref, b_ref, y_ref, dy_ref, lse_ref, ya_ref, dya_ref, lsea_ref,
             dq_ref, dk_ref, dv_ref, db_ref):
        r, n = pl.program_id(1), pl.program_id(2)
        q = q_ref[...].reshape(T, B, E)
        kc = k_ref[...].reshape(T, B, E)
        vc = v_ref[...].reshape(T, B, E)
        kp = jnp.concatenate([kh_ref[...][None], kc[:T - 1]], axis=0) if T > 1 else kh_ref[...][None]
        vp = jnp.concatenate([vh_ref[...][None], vc[:T - 1]], axis=0) if T > 1 else vh_ref[...][None]
        dy_ = dy_ref[...].reshape(T, B, E)
        dyc = dy_.astype(CDT)
        delta = jnp.sum(dy_ * y_ref[...].reshape(T, B, E), axis=-1, keepdims=True)
        lse_ = lse_ref[...].reshape(T, B, E)[:, :, 0:1]
        mask_c, mask_p = _dil_masks(T, n * T)
        bias_pc = b_ref[0]
        s_c = jnp.where(mask_c, _bdot(q, kc, 2) * scale + bias_pc[:, B:][None], NEG_INF)
        s_p = jnp.where(mask_p, _bdot(q, kp, 2) * scale + bias_pc[:, :B][None], NEG_INF)
        p_c = jnp.exp(s_c - lse_)
        p_p = jnp.exp(s_p - lse_)
        ds_c = p_c * (_bdot(dyc, vc, 2) - delta)
        ds_p = p_p * (_bdot(dyc, vp, 2) - delta)
        dsc_s = (ds_c * scale).astype(CDT)
        dsp_s = (ds_p * scale).astype(CDT)
        dq_ref[...] = (_bdot(dsc_s, kc, 1) + _bdot(dsp_s, kp, 1)).reshape(rows, E)
        tr = lambda t: jnp.swapaxes(t, 1, 2)
        dk = _bdot(tr(ds_c * scale).astype(CDT), q, 1)
        dv = _bdot(tr(p_c).astype(CDT), dyc, 1)
        dk_prev = _bdot(tr(ds_p * scale).astype(CDT), q, 1)
        dv_prev = _bdot(tr(p_p).astype(CDT), dyc, 1)

        qa = qa_ref[...]
        dya = dya_ref[...]
        dyac = dya.astype(CDT)
        delta_a = jnp.sum(dya * ya_ref[...], axis=-1, keepdims=True)
        ii = lax.broadcasted_iota(jnp.int32, (B, B), 0)
        jj = lax.broadcasted_iota(jnp.int32, (B, B), 1)
        mask_a = (jj >= ii) & (n < nc - 1)
        s_a = lax.dot_general(qa, kc[T - 1], _NT, preferred_element_type=F32) * scale + bias_pc[:, :B]
        p_a = jnp.exp(jnp.where(mask_a, s_a, NEG_INF) - lsea_ref[:, 0:1])
        ds_a = p_a * (lax.dot_general(dyac, vc[T - 1], _NT, preferred_element_type=F32) - delta_a)
        dk_a = jnp.dot((ds_a * scale).T.astype(CDT), qa, preferred_element_type=F32)
        dv_a = jnp.dot(p_a.T.astype(CDT), dyac, preferred_element_type=F32)
        if T > 1:
            dk = dk + jnp.concatenate([dk_prev[1:], dk_a[None]], axis=0)
            dv = dv + jnp.concatenate([dv_prev[1:], dv_a[None]], axis=0)
        else:
            dk = dk + dk_a[None]
            dv = dv + dv_a[None]
        dk_ref[...] = dk.reshape(rows, E)
        dv_ref[...] = dv.reshape(rows, E)

        dbias = jnp.concatenate([jnp.sum(ds_p, axis=0), jnp.sum(ds_c, axis=0)], axis=1)

        @pl.when((r == 0) & (n == 0))
        def _():
            db_ref[0] = dbias

        @pl.when((r > 0) | (n > 0))
        def _():
            db_ref[0] += dbias

    view = proj.reshape(n_sub, d * C)
    act = lambda t: t.reshape(n_sub, d * H * E)
    last = n_sub // B - 1
    cur = lambda off: pl.BlockSpec((rows, E), lambda h, r, n: (n, r * cb + off + h))
    before = lambda off: pl.BlockSpec((B, E), lambda h, r, n: (jnp.maximum(n * T - 1, 0), r * cb + off + h))
    after_q = pl.BlockSpec((B, E), lambda h, r, n: (jnp.minimum((n + 1) * T, last), r * cb + col0 + h))
    acur = pl.BlockSpec((rows, E), lambda h, r, n: (n, r * H + h))
    aafter = pl.BlockSpec((B, E), lambda h, r, n: (jnp.minimum((n + 1) * T, last), r * H + h))
    dq, dk, dv, db = pl.pallas_call(
        body, grid=(H, d, nc),
        in_specs=[cur(col0), cur(col0 + H), cur(col0 + 2 * H), before(col0 + H), before(col0 + 2 * H), after_q,
                  pl.BlockSpec((1, B, 2 * B), lambda h, r, n: (h, 0, 0)),
                  acur, acur, acur, aafter, aafter, aafter],
        out_specs=[acur, acur, acur, pl.BlockSpec((1, B, 2 * B), lambda h, r, n: (h, 0, 0))],
        out_shape=[jax.ShapeDtypeStruct((n_sub, d * H * E), F32)] * 3
        + [jax.ShapeDtypeStruct((H, B, 2 * B), F32)],
        name=name, compiler_params=_cparams(("parallel", "arbitrary", "arbitrary")),
    )(view, view, view, view, view, view, bias, act(y), act(dy), act(lse), act(y), act(dy), act(lse))
    return dq.reshape(S, H * E), dk.reshape(S, H * E), dv.reshape(S, H * E), db


def _dil_combine(outs, lses, name):
    n = len(outs)

    def fn(*t):
        o, l = t[:n], t[n:]
        m = functools.reduce(jnp.maximum, l)
        w = [jnp.exp(li - m) for li in l]
        tot = functools.reduce(jnp.add, w)
        y = functools.reduce(jnp.add, [(wi / tot) * oi for wi, oi in zip(w, o)])
        return y, m + jnp.log(tot)

    return _rowwise(fn, list(outs) + list(lses), [F32, F32], name)


def _sum_cast(parts, name):
    return _rowwise(lambda *t: (functools.reduce(jnp.add, t),), list(parts), [CDT], name)[0]


def _adamw(w, g, m, v, name, br=128):
    R, C = w.shape
    br = br if R % br == 0 else R

    def body(w_ref, g_ref, m_ref, v_ref, d_ref, nm_ref, nv_ref):
        g_ = g_ref[...]
        m_ = ADAM_B1 * m_ref[...] + (1.0 - ADAM_B1) * g_
        v_ = ADAM_B2 * v_ref[...] + (1.0 - ADAM_B2) * jnp.square(g_)
        m_hat = m_ / (1.0 - ADAM_B1 ** ADAM_STEP)
        v_hat = v_ / (1.0 - ADAM_B2 ** ADAM_STEP)
        d_ref[...] = -ADAM_LR * (m_hat / (jnp.sqrt(v_hat) + ADAM_EPS) + ADAM_WD * w_ref[...])
        nm_ref[...] = m_
        nv_ref[...] = v_

    blk = pl.BlockSpec((br, C), lambda i: (i, 0))
    return pl.pallas_call(
        body, grid=(R // br,), in_specs=[blk] * 4, out_specs=[blk] * 3,
        out_shape=[jax.ShapeDtypeStruct((R, C), F32)] * 3,
        name=name, compiler_params=_cparams(("parallel",)),
    )(w, g, m, v)


_HBM = pl.BlockSpec(memory_space=pltpu.HBM)
_SEM = pl.BlockSpec(memory_space=pltpu.SEMAPHORE)
_ANY = pl.BlockSpec(memory_space=pl.ANY)
_VMEM = pl.BlockSpec(memory_space=pltpu.VMEM)
_TOKEN = jax.ShapeDtypeStruct((8, 128), F32)


def _split_params():
    return pltpu.CompilerParams(has_side_effects=pltpu.SideEffectType.DATAFLOW_SIDE_EFFECTING)


def _place():
    x, y, c = lax.axis_index("x"), lax.axis_index("y"), lax.axis_index("c")
    chips = [(1 - x, y), (x, 1 - y), (1 - x, 1 - y)]
    return x, y, c, chips


def _tie(v, tokens, name):
    flat = v.reshape(1, -1)

    def body(v_ref, *rest):
        rest[-1][...] = v_ref[...]

    return pl.pallas_call(body, in_specs=[_VMEM] + [_ANY] * len(tokens), out_specs=_VMEM,
                          out_shape=jax.ShapeDtypeStruct(flat.shape, flat.dtype), name=name,
                          compiler_params=_cparams())(flat, *tokens).reshape(v.shape)


def _row_block(R, pref=256):
    return _tile(R, pref) if R % 128 == 0 else R


def _slot():
    return 2 * lax.axis_index("x") + lax.axis_index("y")


def _cast_into_slot(w, name):
    R, C = w.shape
    br = _row_block(R)

    def body(w_ref, out_ref):
        out_ref[...] = w_ref[...].astype(out_ref.dtype)

    return pl.pallas_call(
        body, grid=(R // br,),
        in_specs=[pl.BlockSpec((br, C), lambda i: (i, 0))],
        out_specs=pl.BlockSpec((None, br, C), lambda i: (_slot(), i, 0)),
        out_shape=jax.ShapeDtypeStruct((N_CHIPS, R, C), CDT),
        name=name, compiler_params=_cparams(("parallel",)),
    )(w)


def _gather_copies(src_ref, dst_ref, send_sems, recv_sems, incoming):
    Rh = src_ref.shape[1] // 2
    x, y, c, chips = _place()
    slot = 2 * x + y

    def half(ref, s, hf):
        return ref.at[s, pl.ds(hf * Rh, Rh), :]

    copies = []
    for j, (cx, cy) in enumerate(chips):
        for e in range(2):
            copies.append(pltpu.make_async_remote_copy(
                src_ref=half(src_ref, slot, c), dst_ref=half(dst_ref, 2 * cx + cy, e) if incoming else half(dst_ref, slot, c),
                send_sem=send_sems.at[2 * j + e], recv_sem=recv_sems.at[2 * j + (e if incoming else c)],
                device_id=(cx, cy, e), device_id_type=MESH))
    return copies


def _gather_start(buf, after, name):
    n_after = len(after)

    def body(*refs):
        buf_ref = refs[0]
        send_sems, recv_sems, out_ref, token = refs[1 + n_after:]
        for cp in _gather_copies(buf_ref, out_ref, send_sems, recv_sems, incoming=False):
            cp.start()
        token[...] = jnp.zeros(token.shape, token.dtype)

    return pl.pallas_call(
        body, in_specs=[_HBM] + [_ANY] * n_after, out_specs=(_SEM, _SEM, _HBM, _VMEM),
        out_shape=(pltpu.SemaphoreType.DMA((6,)), pltpu.SemaphoreType.DMA((6,)), pltpu.HBM(buf.shape, buf.dtype), _TOKEN),
        input_output_aliases={0: 2}, name=name, compiler_params=_split_params(),
    )(pltpu.with_memory_space_constraint(buf, pltpu.HBM), *after)


def _gather_wait(send_sems, recv_sems, buf, after, name):
    def body(buf_ref, send_sems, recv_sems, after_ref, out_ref):
        for cp in _gather_copies(buf_ref, out_ref, send_sems, recv_sems, incoming=False):
            cp.wait_send()
        for cp in _gather_copies(buf_ref, out_ref, send_sems, recv_sems, incoming=True):
            cp.wait_recv()

    return pl.pallas_call(
        body, in_specs=[_HBM, _SEM, _SEM, _ANY], out_specs=_HBM, out_shape=pltpu.HBM(buf.shape, buf.dtype),
        input_output_aliases={0: 0}, name=name, compiler_params=_split_params(),
    )(buf, send_sems, recv_sems, after)


def _scatter_copies(g_ref, land_ref, send_sems, recv_sems, incoming):
    Rh = g_ref.shape[1] // 2
    x, y, c, _ = _place()
    me = 4 * x + 2 * y + c
    copies = []
    for k in range(1, N_DEV):
        px, py, pc = (x + (k >> 2)) % 2, (y + ((k >> 1) & 1)) % 2, (c + (k & 1)) % 2
        copies.append(pltpu.make_async_remote_copy(
            src_ref=g_ref.at[2 * px + py, pl.ds(pc * Rh, Rh), :],
            dst_ref=land_ref.at[4 * px + 2 * py + pc if incoming else me],
            send_sem=send_sems.at[k - 1], recv_sem=recv_sems.at[k - 1], device_id=(px, py, pc), device_id_type=MESH))
    return copies


def _scatter_start(g, name):
    ns, R, C = g.shape

    def body(g_ref, land_ref, send_sems, recv_sems, g_thru, land_thru, token):
        for cp in _scatter_copies(g_ref, land_thru, send_sems, recv_sems, incoming=False):
            cp.start()
        token[...] = jnp.zeros(token.shape, token.dtype)

    land = lax.empty((N_DEV, R // 2, C), g.dtype)
    n = N_DEV - 1
    return pl.pallas_call(
        body, in_specs=[_HBM, _HBM], out_specs=(_SEM, _SEM, _HBM, _HBM, _VMEM),
        out_shape=(pltpu.SemaphoreType.DMA((n,)), pltpu.SemaphoreType.DMA((n,)), pltpu.HBM(g.shape, g.dtype),
                   pltpu.HBM(land.shape, land.dtype), _TOKEN),
        input_output_aliases={0: 2, 1: 3}, name=name, compiler_params=_split_params(),
    )(pltpu.with_memory_space_constraint(g, pltpu.HBM), pltpu.with_memory_space_constraint(land, pltpu.HBM))


def _scatter_wait(send_sems, recv_sems, g, land, name):
    def body(g_ref, land_ref, send_sems, recv_sems, g_out, land_out):
        for cp in _scatter_copies(g_ref, land_out, send_sems, recv_sems, incoming=False):
            cp.wait_send()
        for cp in _scatter_copies(g_ref, land_out, send_sems, recv_sems, incoming=True):
            cp.wait_recv()

    return pl.pallas_call(
        body, in_specs=[_HBM, _HBM, _SEM, _SEM], out_specs=(_HBM, _HBM),
        out_shape=(pltpu.HBM(g.shape, g.dtype), pltpu.HBM(land.shape, land.dtype)),
        input_output_aliases={0: 0, 1: 1}, name=name, compiler_params=_split_params(),
    )(g, land, send_sems, recv_sems)


def _device_sum(land, g, layer, n_layers, prev, name):
    nd, Rh, C = land.shape
    br = _row_block(Rh)
    nb = Rh // br
    core = lambda: lax.axis_index("c")
    me = lambda: 2 * _slot() + core()

    def body(*refs):
        own = refs[nd][...]
        acc = None
        for d in range(nd):
            t = jnp.where(me() == d, own, refs[d][...]).astype(F32)
            acc = t if acc is None else acc + t
        refs[-1][...] = acc

    def piece(d):
        return pl.BlockSpec((None, br, C), lambda i: (jnp.where(me() == d, (d + 1) % nd, d), i, 0))

    ins = [land] * nd + [g] + ([prev] if prev is not None else [])
    return pl.pallas_call(
        body, grid=(nb,),
        in_specs=[piece(d) for d in range(nd)]
        + [pl.BlockSpec((None, br, C), lambda i: (_slot(), core() * nb + i, 0))]
        + ([_ANY] if prev is not None else []),
        out_specs=pl.BlockSpec((None, br, C), lambda i: (layer, core() * nb + i, 0)),
        out_shape=jax.ShapeDtypeStruct((n_layers, 2 * Rh, C), F32),
        input_output_aliases={nd + 1: 0} if prev is not None else {},
        name=name, compiler_params=_cparams(("parallel",)),
    )(*ins)


def _join_halves(g, layer, name):
    _, R, C = g.shape
    Rh = R // 2

    def body(g_ref, out_ref, send_sem, recv_sem):
        x, y, c, _ = _place()
        mine = pl.ds(c * Rh, Rh)
        cp = pltpu.make_async_remote_copy(src_ref=g_ref.at[layer, mine, :], dst_ref=out_ref.at[layer, mine, :],
                                          send_sem=send_sem, recv_sem=recv_sem, device_id=(x, y, 1 - c),
                                          device_id_type=MESH)
        cp.start()
        other = out_ref.at[layer, pl.ds((1 - c) * Rh, Rh), :]
        pltpu.make_async_remote_copy(src_ref=other, dst_ref=other, send_sem=send_sem, recv_sem=recv_sem,
                                     device_id=(x, y, 1 - c), device_id_type=MESH).wait_recv()
        cp.wait_send()

    return pl.pallas_call(
        body, in_specs=[_HBM], out_specs=_HBM, out_shape=jax.ShapeDtypeStruct(g.shape, g.dtype),
        input_output_aliases={0: 0},
        scratch_shapes=[pltpu.SemaphoreType.DMA, pltpu.SemaphoreType.DMA],
        name=name, compiler_params=pltpu.CompilerParams(),
    )(g)


def _all_reduce_small(v, name):
    rows, cols = v.shape

    def body(v_ref, out_ref, buf, send_sems, recv_sems):
        x, y, c, _ = _place()
        me = 4 * x + 2 * y + c
        buf[me] = v_ref[...]
        peers = []
        for k in range(1, N_DEV):
            px, py, pc = (x + (k >> 2)) % 2, (y + ((k >> 1) & 1)) % 2, (c + (k & 1)) % 2
            peers.append((px, py, pc))
        sends = []
        for k, peer in enumerate(peers):
            cp = pltpu.make_async_remote_copy(src_ref=v_ref, dst_ref=buf.at[me], send_sem=send_sems.at[k],
                                              recv_sem=recv_sems.at[k], device_id=peer, device_id_type=MESH)
            cp.start()
            sends.append(cp)
        for k, (px, py, pc) in enumerate(peers):
            pltpu.make_async_remote_copy(src_ref=v_ref, dst_ref=buf.at[4 * px + 2 * py + pc], send_sem=send_sems.at[k],
                                         recv_sem=recv_sems.at[k], device_id=(px, py, pc),
                                         device_id_type=MESH).wait_recv()
        for cp in sends:
            cp.wait_send()
        acc = buf[0]
        for i in range(1, N_DEV):
            acc = acc + buf[i]
        out_ref[...] = acc

    vmem = pl.BlockSpec(memory_space=pltpu.VMEM)
    return pl.pallas_call(
        body, in_specs=[vmem], out_specs=vmem, out_shape=jax.ShapeDtypeStruct((rows, cols), F32),
        scratch_shapes=[pltpu.VMEM((N_DEV, rows, cols), F32), pltpu.SemaphoreType.DMA((N_DEV - 1,)),
                        pltpu.SemaphoreType.DMA((N_DEV - 1,))],
        name=name, compiler_params=pltpu.CompilerParams(),
    )(v)


def _reduce_scatter_finish(started, layer, n_layers, prev, tag):
    send_sems, recv_sems, g, land, _ = started
    g, land = _scatter_wait(send_sems, recv_sems, g, land, f"rs_wait_{tag}")
    f = _device_sum(land, g, layer, n_layers, prev, f"rs_sum_{tag}")
    return _join_halves(f, layer, f"rs_join_{tag}")


def _split_w_in(wg, Hf):
    ns, D, cols = wg.shape
    nat = wg.transpose(1, 0, 2).reshape(D, ns * cols)
    a = 3 * Hf * HEAD_DIM
    return jnp.concatenate([nat[:, :a], nat[:, a + Hf:]], axis=1), nat[:, a:a + Hf].T


def _join_dw_in(dw6, dwf_t, Hf):
    D = dw6.shape[0]
    a = 3 * Hf * HEAD_DIM
    nat = jnp.concatenate([dw6[:, :a], dwf_t.T.astype(dw6.dtype), dw6[:, a:]], axis=1)
    return nat.reshape(D, N_CHIPS, nat.shape[1] // N_CHIPS).transpose(1, 0, 2)


def _tied(v, tokens, name):
    return _tie(v, tokens, name) if tokens else v


def _layer_fwd(x, p, weight, bias, tokens, tag):
    Hf, Hd = p["forget_b"].shape[0], bias.shape[1]
    h1 = _rms_fwd(x, _tied(p["norm1_g"], tokens, f"tie_norm1_{tag}"), f"norm1_{tag}")
    w6, wf_t = _split_w_in(weight("w_in", h1), Hf)
    proj = _mm_nn(h1, w6, f"proj_{tag}", [CDT], epi=lambda acc: (acc,))[0]
    f_t = _mm_nt(wf_t, h1, f"fproj_{tag}", [F32])[0]
    qc, kc = _fox_bias_operands(_gates_fwd(f_t, p["forget_b"], f"gates_{tag}"))
    y_a, lse_a = _fox_fwd(proj, qc, kc, Hf, f"fox_{tag}")
    outs, lses = [], []
    for i, (_, d) in enumerate(DIL_PATTERNS):
        o, l = _dil_fwd(proj, bias[i], d, Hd, 3 * Hf, f"dil{i}_{tag}")
        outs.append(o)
        lses.append(l)
    y_b, lse_b = _dil_combine(outs, lses, f"dilmix_{tag}")
    mixed = jnp.concatenate([_rms_fwd(y_a, p["outnorm_a_g"], f"norm_a_{tag}"),
                             _rms_fwd(y_b, p["outnorm_b_g"], f"norm_b_{tag}")], axis=1)
    w_out = weight("w_out", mixed)
    w_out = w_out.reshape(-1, w_out.shape[2])
    x1 = _mm_nn(mixed, w_out, f"attn_out_{tag}", [F32], extras=[x])[0]
    h2 = _rms_fwd(x1, p["norm2_g"], f"norm2_{tag}")
    w_mi = weight("w_mlp_in", h2)
    u, act = _mm_nn(h2, w_mi, f"mlp_in_{tag}", [CDT, CDT], b_slots=True,
                    epi=lambda acc: (acc, jnp.square(jnp.maximum(acc, 0.0))))
    w_mo = weight("w_mlp_out", act)
    w_mo = w_mo.reshape(-1, w_mo.shape[2])
    x2 = _mm_nn(act, w_mo, f"mlp_out_{tag}", [F32], extras=[x1])[0]
    saved = dict(x=x, h1=h1, proj=proj, f_t=f_t, qc=qc, kc=kc, y_a=y_a, lse_a=lse_a, y_b=y_b,
                 lse_b=lse_b, mixed=mixed, x1=x1, h2=h2, u=u, act=act, w6=w6, wf_t=wf_t, w_out=w_out, w_mi=w_mi,
                 w_mo=w_mo)
    return x2, saved


def _layer_bwd(dx2, dx2c, p, send, bias, sv, tag):
    Hf, Hd = p["forget_b"].shape[0], bias.shape[1]
    E = HEAD_DIM
    rows = lambda g: g.reshape(N_CHIPS, -1, g.shape[1])
    du = _mm_nt(dx2c, sv["w_mo"], f"d_act_{tag}", [CDT], extras=[sv["u"]],
                epi=lambda acc, u: (acc * (2.0 * jnp.maximum(u.astype(F32), 0.0)),))[0]
    tokens = send("w_mlp_out", rows(_mm_tn(sv["act"], dx2c, f"dw_mlp_out_{tag}", CDT)))
    dh2 = _mm_nt(du, sv["w_mi"], f"d_h2_{tag}", [F32], b_slots=True)[0]
    tokens = tokens + send("w_mlp_in", _mm_tn(sv["h2"], du, f"dw_mlp_in_{tag}", CDT, out_slots=N_CHIPS))
    dx1, dx1c, g_norm2 = _rms_bwd(sv["x1"], _tied(p["norm2_g"], tokens, f"tie_norm2_{tag}"), dh2, dx2,
                                  f"d_norm2_{tag}")
    dmixed = _mm_nt(dx1c, sv["w_out"], f"d_mixed_{tag}", [F32])[0]
    tokens = send("w_out", rows(_mm_tn(sv["mixed"], dx1c, f"dw_out_{tag}", CDT)))
    dy_a, _, g_na = _rms_bwd(sv["y_a"], _tied(p["outnorm_a_g"], tokens, f"tie_norm_a_{tag}"), dmixed, None,
                             f"d_norm_a_{tag}", dh_col=0)
    dy_b, _, g_nb = _rms_bwd(sv["y_b"], p["outnorm_b_g"], dmixed, None, f"d_norm_b_{tag}", dh_col=1)
    proj = sv["proj"]
    dq_a, dcq, dk_a, dv_a, dck = _fox_bwd(proj, sv["qc"], sv["kc"], sv["lse_a"], sv["y_a"], dy_a, Hf, f"fox_bwd_{tag}")
    df, dfc, g_fb = _gates_bwd(sv["f_t"], p["forget_b"], dcq[:, ::E].T, dck.reshape(Hf, -1), f"d_gates_{tag}")
    parts, dbias = [], []
    for i, (_, d) in enumerate(DIL_PATTERNS):
        dq, dk, dv, db = _dil_bwd(proj, bias[i], sv["y_b"], dy_b, sv["lse_b"], d, Hd, 3 * Hf, f"dil{i}_bwd_{tag}")
        parts.append((dq, dk, dv))
        dbias.append(db)
    dqkv_b = [_sum_cast([parts[i][j] for i in range(len(DIL_PATTERNS))], f"dil_sum{j}_{tag}") for j in range(3)]
    dproj = jnp.concatenate([dq_a, dk_a, dv_a] + dqkv_b, axis=1)
    dh1_f = _mm_tn(dfc, sv["wf_t"], f"d_h1_f_{tag}", F32)
    dh1 = _mm_nt(dproj, sv["w6"], f"d_h1_{tag}", [F32], extras=[dh1_f])[0]
    g_w6 = _mm_tn(sv["h1"], dproj, f"dw_in_{tag}", CDT)
    g_wf_t = _mm_nn(dfc, sv["h1"], f"dw_f_{tag}", [F32])[0]
    tokens = send("w_in", _join_dw_in(g_w6, g_wf_t, Hf))
    dx, dxc, g_norm1 = _rms_bwd(sv["x"], _tied(p["norm1_g"], tokens, f"tie_norm1_bwd_{tag}"), dh1, dx1,
                                f"d_norm1_{tag}")
    grads = dict(norm1_g=g_norm1[0], norm2_g=g_norm2[0], outnorm_a_g=g_na[0], outnorm_b_g=g_nb[0],
                 forget_b=g_fb[:, 0], dbias=jnp.stack(dbias))
    return dx, dxc, grads


_LAYER_SMALL = ("norm1_g", "forget_b", "outnorm_a_g", "outnorm_b_g", "norm2_g")


def _local_step(x, target, small, weight, send, tokens):
    depth = small["norm1_g"].shape[0]
    buckets = _bucket_table()
    bias = _bias_table(small["rel_bias"], buckets, "bias_table")
    layers, saved = [], []
    for l in range(depth):
        p = {k: small[k][l] for k in _LAYER_SMALL}
        layers.append(p)
        x, sv = _layer_fwd(x, p, functools.partial(weight, l), bias, tokens if l == 0 else [], f"l{l}")
        saved.append(sv)
    dx, dxc, g_final, loss = _loss_bwd(x, small["final_norm_g"], target, "loss")
    layer_grads = [None] * depth
    for l in reversed(range(depth)):
        dx, dxc, layer_grads[l] = _layer_bwd(dx, dxc, layers[l], functools.partial(send, l), bias, saved[l], f"l{l}")
    dbias = functools.reduce(jnp.add, [g["dbias"] for g in layer_grads])
    g_rel = _bias_table_bwd(dbias, buckets, "d_bias_table")[:, 0, :].T
    small_grads = dict(final_norm_g=g_final[0], rel_bias=g_rel,
                       **{k: jnp.stack([g[k] for g in layer_grads]) for k in _LAYER_SMALL})
    return loss[0, 0], dx, small_grads


_BIG = ("w_in", "w_out", "w_mlp_in", "w_mlp_out")
_SMALL = ("norm1_g", "forget_b", "rel_bias", "outnorm_a_g", "outnorm_b_g", "norm2_g", "final_norm_g")
_ORDER = ("norm1_g", "w_in", "forget_b", "rel_bias", "outnorm_a_g", "outnorm_b_g", "w_out", "norm2_g", "w_mlp_in",
          "w_mlp_out", "final_norm_g")


def _pack_small(d):
    flat = jnp.concatenate([d[k].reshape(-1) for k in _SMALL])
    rows = -(-flat.shape[0] // (8 * SMALL_COLS)) * 8
    return jnp.pad(flat, (0, rows * SMALL_COLS - flat.shape[0])).reshape(rows, SMALL_COLS)


def _unpack_small(packed, like):
    flat, out, at = packed.reshape(-1), {}, 0
    for k in _SMALL:
        n = like[k].size
        out[k] = flat[at:at + n].reshape(like[k].shape)
        at += n
    return out


def kernel(x, norm1_g, w_in, forget_b, rel_bias, outnorm_a_g, outnorm_b_g, w_out, norm2_g, w_mlp_in, w_mlp_out, final_norm_g, loss_target, m_norm1_g, m_w_in, m_forget_b, m_rel_bias, m_outnorm_a_g, m_outnorm_b_g, m_w_out, m_norm2_g, m_w_mlp_in, m_w_mlp_out, m_final_norm_g, v_norm1_g, v_w_in, v_forget_b, v_rel_bias, v_outnorm_a_g, v_outnorm_b_g, v_w_out, v_norm2_g, v_w_mlp_in, v_w_mlp_out, v_final_norm_g):
    w = dict(norm1_g=norm1_g, w_in=w_in, forget_b=forget_b, rel_bias=rel_bias, outnorm_a_g=outnorm_a_g,
             outnorm_b_g=outnorm_b_g, w_out=w_out, norm2_g=norm2_g, w_mlp_in=w_mlp_in, w_mlp_out=w_mlp_out,
             final_norm_g=final_norm_g)
    m = dict(norm1_g=m_norm1_g, w_in=m_w_in, forget_b=m_forget_b, rel_bias=m_rel_bias, outnorm_a_g=m_outnorm_a_g,
             outnorm_b_g=m_outnorm_b_g, w_out=m_w_out, norm2_g=m_norm2_g, w_mlp_in=m_w_mlp_in,
             w_mlp_out=m_w_mlp_out, final_norm_g=m_final_norm_g)
    v = dict(norm1_g=v_norm1_g, w_in=v_w_in, forget_b=v_forget_b, rel_bias=v_rel_bias, outnorm_a_g=v_outnorm_a_g,
             outnorm_b_g=v_outnorm_b_g, w_out=v_w_out, norm2_g=v_norm2_g, w_mlp_in=v_w_mlp_in,
             w_mlp_out=v_w_mlp_out, final_norm_g=v_final_norm_g)
    depth = w_in.shape[0]
    small = {k: w[k] for k in _SMALL}

    gathers, tokens = {}, []
    for l in range(depth):
        for k in _BIG:
            buf = _cast_into_slot(w[k][l], f"cast_{k}_l{l}")
            send_sems, recv_sems, buf, token = _gather_start(buf, tokens, f"gather_start_{k}_l{l}")
            gathers[l, k], tokens = (send_sems, recv_sems, buf), [token]

    def weight(l, k, after):
        return _gather_wait(*gathers[l, k], after, f"gather_wait_{k}_l{l}")

    scatters = {}

    def send(l, k, g):
        scatters[l, k] = _scatter_start(g, f"rs_start_{k}_l{l}")
        return [scatters[l, k][4]]

    loss, grad_x, small_grads = _local_step(x[0], loss_target[0], small, weight, send, tokens)
    loss = lax.psum(loss, ("x", "y", "c"))

    grads = {}
    for (l, k), started in scatters.items():
        grads[k] = _reduce_scatter_finish(started, l, depth, grads.get(k), f"{k}_l{l}")
    grads.update(_unpack_small(_all_reduce_small(_pack_small(small_grads), "small_all_reduce"), small))

    delta, new_m, new_v = {}, {}, {}
    for k in _BIG:
        shape = w[k].shape
        flat = lambda t: t.reshape(-1, shape[-1])
        d_, m_, v_ = _adamw(flat(w[k]), flat(grads[k]), flat(m[k]), flat(v[k]), f"adamw_{k}")
        delta[k], new_m[k], new_v[k] = d_.reshape(shape), m_.reshape(shape), v_.reshape(shape)
    d_, m_, v_ = _adamw(_pack_small(small), _pack_small({k: grads[k] for k in _SMALL}),
                        _pack_small({k: m[k] for k in _SMALL}), _pack_small({k: v[k] for k in _SMALL}), "adamw_small")
    delta.update(_unpack_small(d_, small))
    new_m.update(_unpack_small(m_, small))
    new_v.update(_unpack_small(v_, small))

    return (loss, grad_x[None], *[grads[k] for k in _ORDER], *[delta[k] for k in _ORDER],
            *[new_m[k] for k in _ORDER], *[new_v[k] for k in _ORDER])
```

```python
import functools

import jax
import jax.numpy as jnp
from jax import lax
from jax.experimental import pallas as pl
from jax.experimental.pallas import tpu as pltpu

F32 = jnp.float32
CDT = jnp.bfloat16
HEAD_DIM = 128
NORM_EPS = 1e-6
NEG_INF = -1e30
LOG2E = 1.4426950408889634
REL_BUCKETS = 32
REL_MAX_DISTANCE = 2048
DIL_PATTERNS = ((128, 1), (512, 4), (2048, 16))
DIL_BLOCK = 128
ADAM_LR, ADAM_B1, ADAM_B2, ADAM_EPS, ADAM_WD, ADAM_STEP = 0.001, 0.9, 0.999, 1e-08, 0.01, 10
N_CHIPS = 4
N_DEV = 8
VMEM_LIMIT_BYTES = 56 * 1024 * 1024
SMALL_COLS = 1024
MESH = pl.DeviceIdType.MESH


def _cparams(sem=None):
    return pltpu.CompilerParams(dimension_semantics=sem, vmem_limit_bytes=VMEM_LIMIT_BYTES)


def _tile(dim, pref):
    t = min(pref, dim)
    t -= t % 128
    while t >= 128:
        if dim % t == 0:
            return t
        t -= 128
    return dim


def _rowwise(fn, ins, out_dtypes, name, bs=256, consts=()):
    R, C = ins[0].shape
    bs = min(bs, R)
    n_in, n_c = len(ins), len(consts)

    def body(*refs):
        vals = [r[...] for r in refs[:n_in + n_c]]
        res = fn(*vals)
        for o, r in zip(refs[n_in + n_c:], res):
            o[...] = r.astype(o.dtype)

    row = pl.BlockSpec((bs, C), lambda i: (i, 0))
    return pl.pallas_call(
        body, grid=(R // bs,),
        in_specs=[row] * n_in + [pl.BlockSpec((1, c.shape[-1]), lambda i: (0, 0)) for c in consts],
        out_specs=[row] * len(out_dtypes),
        out_shape=[jax.ShapeDtypeStruct((R, C), d) for d in out_dtypes],
        name=name, compiler_params=_cparams(("parallel",)),
    )(*ins, *[c.reshape(1, -1) for c in consts])


def _rms_fwd(x, g, name):
    def fn(xf, gg):
        r = lax.rsqrt(jnp.mean(xf * xf, axis=-1, keepdims=True) + NORM_EPS)
        return ((xf * r) * gg,)
    return _rowwise(fn, [x], [CDT], name, consts=[g])[0]


def _rms_bwd(x, g, dh, dres, name, bs=256, dh_col=0):
    S, D = x.shape
    bs = min(bs, S)
    has_res = dres is not None

    def body(*refs):
        x_ref, g_ref, dh_ref = refs[:3]
        dx_ref, dxc_ref, dg_ref = refs[-3:]
        xf = x_ref[...]
        r = lax.rsqrt(jnp.mean(xf * xf, axis=-1, keepdims=True) + NORM_EPS)
        xhat = xf * r
        dh_ = dh_ref[...].astype(F32)
        dxhat = dh_ * g_ref[...]
        dx = r * (dxhat - xhat * jnp.mean(dxhat * xhat, axis=-1, keepdims=True))
        if has_res:
            dx = dx + refs[3][...]
        dx_ref[...] = dx
        dxc_ref[...] = dx.astype(dxc_ref.dtype)
        part = jnp.sum(dh_ * xhat, axis=0, keepdims=True)

        @pl.when(pl.program_id(0) == 0)
        def _():
            dg_ref[...] = part

        @pl.when(pl.program_id(0) > 0)
        def _():
            dg_ref[...] += part

    row = pl.BlockSpec((bs, D), lambda i: (i, 0))
    one = pl.BlockSpec((1, D), lambda i: (0, 0))
    ins = [x, g.reshape(1, D), dh] + ([dres] if has_res else [])
    return pl.pallas_call(
        body, grid=(S // bs,),
        in_specs=[row, one, pl.BlockSpec((bs, D), lambda i: (i, dh_col))] + ([row] if has_res else []),
        out_specs=[row, row, one],
        out_shape=[jax.ShapeDtypeStruct((S, D), F32), jax.ShapeDtypeStruct((S, D), CDT),
                   jax.ShapeDtypeStruct((1, D), F32)],
        name=name, compiler_params=_cparams(("arbitrary",)),
    )(*ins)


def _loss_bwd(x, g, target, name, bs=256):
    S, D = x.shape
    bs = min(bs, S)

    def body(x_ref, g_ref, t_ref, dx_ref, dxc_ref, dg_ref, loss_ref):
        xf = x_ref[...]
        r = lax.rsqrt(jnp.mean(xf * xf, axis=-1, keepdims=True) + NORM_EPS)
        xhat = xf * r
        err = xhat * g_ref[...] - t_ref[...]
        lpart = 0.5 * jnp.sum(jnp.mean(err * err, axis=-1, keepdims=True), axis=0, keepdims=True)
        dy = err / D
        dxhat = dy * g_ref[...]
        dx = r * (dxhat - xhat * jnp.mean(dxhat * xhat, axis=-1, keepdims=True))
        dx_ref[...] = dx
        dxc_ref[...] = dx.astype(dxc_ref.dtype)
        gpart = jnp.sum(dy * xhat, axis=0, keepdims=True)

        @pl.when(pl.program_id(0) == 0)
        def _():
            dg_ref[...] = gpart
            loss_ref[...] = lpart

        @pl.when(pl.program_id(0) > 0)
        def _():
            dg_ref[...] += gpart
            loss_ref[...] += lpart

    row = pl.BlockSpec((bs, D), lambda i: (i, 0))
    one = pl.BlockSpec((1, D), lambda i: (0, 0))
    return pl.pallas_call(
        body, grid=(S // bs,),
        in_specs=[row, one, row],
        out_specs=[row, row, one, pl.BlockSpec((1, 1), lambda i: (0, 0))],
        out_shape=[jax.ShapeDtypeStruct((S, D), F32), jax.ShapeDtypeStruct((S, D), CDT),
                   jax.ShapeDtypeStruct((1, D), F32), jax.ShapeDtypeStruct((1, 1), F32)],
        name=name, compiler_params=_cparams(("arbitrary",)),
    )(x, g.reshape(1, D), target)


_NN = (((1,), (0,)), ((), ()))
_NT = (((1,), (1,)), ((), ()))
_TN = (((0,), (0,)), ((), ()))


def _mm(a, b, *, M, N, K, a_spec, b_spec, o_spec, dims, tm, tn, tk, name, out_shapes, extras=(), epi=None):
    nk = K // tk
    n_ex, n_out = len(extras), len(out_shapes)
    in_place = epi is None
    if in_place:
        assert n_out == 1 and n_ex <= 1 and out_shapes[0].dtype == F32
        epi = lambda acc, *r: (acc + r[0] if r else acc,)

    def body(*refs):
        a_ref, b_ref = refs[0], refs[1]
        ex = refs[2:2 + n_ex]
        outs = refs[2 + n_ex:2 + n_ex + n_out]
        part = lax.dot_general(a_ref[...], b_ref[...], dims, preferred_element_type=F32)

        def finish(acc):
            for o, r in zip(outs, epi(acc, *[e[...] for e in ex])):
                o[...] = r.astype(o.dtype)

        if nk == 1:
            finish(part)
        elif in_place:
            k = pl.program_id(2)

            @pl.when(k == 0)
            def _():
                finish(part)

            @pl.when(k > 0)
            def _():
                outs[0][...] += part
        else:
            acc_ref = refs[-1]
            k = pl.program_id(2)

            @pl.when(k == 0)
            def _():
                acc_ref[...] = part

            @pl.when(k > 0)
            def _():
                acc_ref[...] += part

            @pl.when(k == nk - 1)
            def _():
                finish(acc_ref[...])

    ex_spec = pl.BlockSpec((tm, tn), lambda i, j, k: (i, j))
    return pl.pallas_call(
        body, grid=(M // tm, N // tn, nk),
        in_specs=[a_spec, b_spec] + [ex_spec] * n_ex,
        out_specs=[o_spec] * n_out,
        out_shape=out_shapes,
        scratch_shapes=[pltpu.VMEM((tm, tn), F32)] if nk > 1 and not in_place else [],
        name=name, compiler_params=_cparams(("parallel", "parallel", "arbitrary")),
    )(a, b, *extras)


def _mm_tiles(K):
    return (2048, 512, 2048) if K <= 2048 else (1024, 1024, 2048)


def _mm_nn(a, b, name, out_dtypes, extras=(), epi=None, b_slots=False, b_cols=None):
    M, K = a.shape
    tm, tn, tk = _mm_tiles(K)
    if b_slots:
        ns, _, Ns = b.shape
        N = ns * Ns
        tn = _tile(Ns, tn)
        npb = Ns // tn
        tk_ = _tile(K, tk)
        b_spec = pl.BlockSpec((None, tk_, tn), lambda i, j, k: (j // npb, k, j % npb))
    else:
        first, N = b_cols if b_cols is not None else (0, b.shape[1])
        tn = _tile(N, tn)
        assert first % tn == 0
        tk_ = _tile(K, tk)
        b_spec = pl.BlockSpec((tk_, tn), lambda i, j, k: (k, first // tn + j))
    tm = _tile(M, tm)
    return _mm(a, b, M=M, N=N, K=K, a_spec=pl.BlockSpec((tm, tk_), lambda i, j, k: (i, k)), b_spec=b_spec,
               o_spec=pl.BlockSpec((tm, tn), lambda i, j, k: (i, j)), dims=_NN, tm=tm, tn=tn, tk=tk_, name=name,
               out_shapes=[jax.ShapeDtypeStruct((M, N), d) for d in out_dtypes], extras=extras, epi=epi)


def _mm_nt(a, b, name, out_dtypes, extras=(), epi=None, b_slots=False):
    M, K = a.shape
    tm, tn, tk = _mm_tiles(K)
    tm = _tile(M, tm)
    if b_slots:
        ns, N, Ks = b.shape
        tk_ = _tile(Ks, tk)
        kpb = Ks // tk_
        tn = _tile(N, tn)
        b_spec = pl.BlockSpec((None, tn, tk_), lambda i, j, k: (k // kpb, j, k % kpb))
    else:
        N = b.shape[0]
        tk_ = _tile(K, tk)
        tn = _tile(N, tn)
        b_spec = pl.BlockSpec((tn, tk_), lambda i, j, k: (j, k))
    return _mm(a, b, M=M, N=N, K=K, a_spec=pl.BlockSpec((tm, tk_), lambda i, j, k: (i, k)), b_spec=b_spec,
               o_spec=pl.BlockSpec((tm, tn), lambda i, j, k: (i, j)), dims=_NT, tm=tm, tn=tn, tk=tk_, name=name,
               out_shapes=[jax.ShapeDtypeStruct((M, N), d) for d in out_dtypes], extras=extras, epi=epi)


def _mm_tn(a, b, name, out_dtype, out_slots=0, tm=2048, tn=1024, tk=2048):
    K, M = a.shape
    N = b.shape[1]
    tm, tk_ = _tile(M, tm), _tile(K, tk)
    if out_slots:
        Ns = N // out_slots
        tn = _tile(Ns, tn)
        npb = Ns // tn
        o_spec = pl.BlockSpec((None, tm, tn), lambda i, j, k: (j // npb, i, j % npb))
        out_shape = jax.ShapeDtypeStruct((out_slots, M, Ns), out_dtype)
    else:
        tn = _tile(N, tn)
        o_spec = pl.BlockSpec((tm, tn), lambda i, j, k: (i, j))
        out_shape = jax.ShapeDtypeStruct((M, N), out_dtype)
    return _mm(a, b, M=M, N=N, K=K, a_spec=pl.BlockSpec((tk_, tm), lambda i, j, k: (k, i)),
               b_spec=pl.BlockSpec((tk_, tn), lambda i, j, k: (k, j)), o_spec=o_spec, dims=_TN,
               tm=tm, tn=tn, tk=tk_, name=name, out_shapes=[out_shape],
               epi=None if out_dtype == F32 else (lambda acc: (acc,)))[0]


GATE_BLOCK = 512


def _split3(v):
    hi = v.astype(jnp.bfloat16)
    r1 = v - hi.astype(F32)
    mid = r1.astype(jnp.bfloat16)
    lo = (r1 - mid.astype(F32)).astype(jnp.bfloat16)
    return hi, mid, lo


def _exact_dot(v, tri):
    return functools.reduce(jnp.add, [jnp.dot(t, tri, preferred_element_type=F32) for t in _split3(v)])


def _gates_fwd(f_t, b, name):
    H, S = f_t.shape
    nb = _tile(S, GATE_BLOCK)
    inv_scale = HEAD_DIM ** 0.5

    def body(f_ref, b_ref, c_ref):
        upper = (lax.broadcasted_iota(jnp.int32, (nb, nb), 0)
                 <= lax.broadcasted_iota(jnp.int32, (nb, nb), 1)).astype(jnp.bfloat16)
        carry = jnp.zeros((H, 1), F32)
        for i in range(S // nb):
            z = f_ref[:, i * nb:(i + 1) * nb] + b_ref[...]
            logf = jnp.minimum(z, 0.0) - jnp.log1p(jnp.exp(-jnp.abs(z)))
            cs = _exact_dot(logf, upper) + carry
            for j, t in enumerate(_split3(cs * inv_scale)):
                c_ref[j, :, i * nb:(i + 1) * nb] = t
            carry = cs[:, nb - 1:nb]

    return pl.pallas_call(body, out_shape=jax.ShapeDtypeStruct((3, H, S), jnp.bfloat16), name=name,
                          compiler_params=_cparams())(f_t, b.reshape(H, 1))


def _gates_bwd(f_t, b, dcq, dck, name):
    H, S = f_t.shape
    nb = _tile(S, GATE_BLOCK)

    def body(f_ref, b_ref, dcq_ref, dck_ref, df_ref, dfc_ref, db_ref):
        lower = (lax.broadcasted_iota(jnp.int32, (nb, nb), 0)
                 >= lax.broadcasted_iota(jnp.int32, (nb, nb), 1)).astype(jnp.bfloat16)
        carry = jnp.zeros((H, 1), F32)
        db = jnp.zeros((H, 1), F32)
        for i in reversed(range(S // nb)):
            sl = slice(i * nb, (i + 1) * nb)
            dc = dcq_ref[:, sl] - dck_ref[:, sl]
            dlogf = _exact_dot(dc, lower) + carry
            carry = dlogf[:, 0:1]
            z = f_ref[:, sl] + b_ref[...]
            df = dlogf / (1.0 + jnp.exp(z))
            df_ref[:, sl] = df
            dfc_ref[:, sl] = df.astype(dfc_ref.dtype)
            db = db + jnp.sum(df, axis=1, keepdims=True)
        db_ref[...] = db

    return pl.pallas_call(
        body, out_shape=[jax.ShapeDtypeStruct((H, S), F32), jax.ShapeDtypeStruct((H, S), CDT),
                         jax.ShapeDtypeStruct((H, 1), F32)],
        name=name, compiler_params=_cparams())(f_t, b.reshape(H, 1), dcq, dck)


FOX_BLOCK = 512


def _fox_bias_operands(csplit):
    parts = csplit.transpose(2, 1, 0)
    S, H, _ = parts.shape
    ones = jnp.ones_like(parts)
    zeros = jnp.zeros((S, H, HEAD_DIM - 6), parts.dtype)
    qc = jnp.concatenate([parts, ones, zeros], axis=-1).reshape(S, H * HEAD_DIM)
    kc = jnp.concatenate([ones, -parts, zeros], axis=-1).reshape(S, H * HEAD_DIM)
    return qc, kc


def _fox_logits2(q_ref, qc_ref, k_ref, kc_ref, diag):
    q, k = q_ref[...], k_ref[...]
    qa = jnp.concatenate([q, qc_ref[...].astype(q.dtype)], axis=1)
    ka = jnp.concatenate([k, kc_ref[...].astype(k.dtype)], axis=1)
    s = lax.dot_general(qa, ka, _NT, preferred_element_type=F32) * (HEAD_DIM ** -0.5 * LOG2E)
    if diag:
        row = lax.broadcasted_iota(jnp.int32, s.shape, 0)
        col = lax.broadcasted_iota(jnp.int32, s.shape, 1)
        s = jnp.where(col <= row, s, NEG_INF)
    return s


def _fox_fwd(proj, qc, kc, H, name):
    S = proj.shape[0]
    E = HEAD_DIM
    blk = _tile(S, FOX_BLOCK)
    nq = S // blk

    def body(q_ref, qc_ref, k_ref, kc_ref, v_ref, o_ref, lse_ref, m_s, l_s, acc_s):
        qi, kj = pl.program_id(1), pl.program_id(2)

        @pl.when(kj == 0)
        def _():
            m_s[...] = jnp.full(m_s.shape, NEG_INF, F32)
            l_s[...] = jnp.zeros(l_s.shape, F32)
            acc_s[...] = jnp.zeros(acc_s.shape, F32)

        def step(diag):
            s = _fox_logits2(q_ref, qc_ref, k_ref, kc_ref, diag)
            m_prev = m_s[...]
            m_new = jnp.maximum(m_prev, jnp.max(s, axis=-1, keepdims=True))
            alpha = jnp.exp2(m_prev - m_new)
            p = jnp.exp2(s - m_new)
            l_s[...] = alpha * l_s[...] + jnp.sum(p, axis=-1, keepdims=True)
            acc_s[...] = alpha * acc_s[...] + jnp.dot(p.astype(CDT), v_ref[...], preferred_element_type=F32)
            m_s[...] = m_new

        pl.when(kj < qi)(lambda: step(False))
        pl.when(kj == qi)(lambda: step(True))

        @pl.when(kj == nq - 1)
        def _():
            o_ref[...] = acc_s[...] / l_s[...]
            lse_ref[...] = jnp.broadcast_to(m_s[...] + jnp.log2(l_s[...]), lse_ref.shape)

    qspec = lambda off: pl.BlockSpec((blk, E), lambda h, i, j: (i, off + h))
    kspec = lambda off: pl.BlockSpec((blk, E), lambda h, i, j: (jnp.minimum(j, i), off + h))
    return pl.pallas_call(
        body, grid=(H, nq, nq),
        in_specs=[qspec(0), qspec(0), kspec(H), kspec(0), kspec(2 * H)],
        out_specs=[qspec(0)] * 2,
        out_shape=[jax.ShapeDtypeStruct((S, H * E), F32)] * 2,
        scratch_shapes=[pltpu.VMEM((blk, 1), F32), pltpu.VMEM((blk, 1), F32), pltpu.VMEM((blk, E), F32)],
        name=name, compiler_params=_cparams(("parallel", "parallel", "arbitrary")),
    )(proj, qc, proj, kc, proj)


def _fox_bwd(proj, qc, kc, lse, o, do, H, name):
    S = proj.shape[0]
    E = HEAD_DIM
    blk = _tile(S, FOX_BLOCK)
    nq = S // blk
    scale = E ** -0.5

    def body(q_ref, qc_ref, k_ref, kc_ref, v_ref, lse_ref, o_ref, do_ref,
             dq_ref, dcq_ref, dk_ref, dv_ref, dck_ref, dq_s, dcq_s, dk_s, dv_s, dck_s):
        kj, qi = pl.program_id(1), pl.program_id(2)

        @pl.when(qi == 0)
        def _():
            dk_s[...] = jnp.zeros(dk_s.shape, F32)
            dv_s[...] = jnp.zeros(dv_s.shape, F32)
            dck_s[...] = jnp.zeros(dck_s.shape, F32)

        def step(diag):
            do = do_ref[...]
            doc = do.astype(CDT)
            delta = jnp.sum(do * o_ref[...], axis=-1, keepdims=True)
            p = jnp.exp2(_fox_logits2(q_ref, qc_ref, k_ref, kc_ref, diag) - lse_ref[:, 0:1])
            dp = lax.dot_general(doc, v_ref[...], _NT, preferred_element_type=F32)
            ds = p * (dp - delta)
            dss = ds * scale
            dck_s[...] += jnp.sum(ds, axis=0, keepdims=True)
            dv_s[...] += jnp.dot(p.T.astype(CDT), doc, preferred_element_type=F32)
            dk_s[...] += jnp.dot(dss.T.astype(CDT), q_ref[...], preferred_element_type=F32)
            dq_part = jnp.dot(dss.astype(CDT), k_ref[...], preferred_element_type=F32)
            dc_part = jnp.sum(ds, axis=-1, keepdims=True)
            rows = pl.ds(pl.multiple_of(qi * blk, blk), blk)

            @pl.when(kj == 0)
            def _():
                dq_s[rows, :] = dq_part
                dcq_s[rows, :] = dc_part

            @pl.when(kj > 0)
            def _():
                dq_s[rows, :] += dq_part
                dcq_s[rows, :] += dc_part

        pl.when(qi > kj)(lambda: step(False))
        pl.when(qi == kj)(lambda: step(True))

        @pl.when(qi == nq - 1)
        def _():
            dk_ref[...] = dk_s[...].astype(dk_ref.dtype)
            dv_ref[...] = dv_s[...].astype(dv_ref.dtype)
            dck_ref[...] = dck_s[...].reshape(dck_ref.shape)

        @pl.when((qi == nq - 1) & (kj == nq - 1))
        def _():
            dq_ref[...] = dq_s[...].astype(dq_ref.dtype)
            dcq_ref[...] = jnp.broadcast_to(dcq_s[...], dcq_ref.shape)

    qspec = lambda off: pl.BlockSpec((blk, E), lambda h, j, i: (jnp.maximum(i, j), off + h))
    kspec = lambda off: pl.BlockSpec((blk, E), lambda h, j, i: (j, off + h))
    head = pl.BlockSpec((S, E), lambda h, j, i: (0, h))
    return pl.pallas_call(
        body, grid=(H, nq, nq),
        in_specs=[qspec(0), qspec(0), kspec(H), kspec(0), kspec(2 * H), qspec(0), qspec(0), qspec(0)],
        out_specs=[head, head, kspec(0), kspec(0), pl.BlockSpec((1, 1, blk), lambda h, j, i: (h, 0, j))],
        out_shape=[jax.ShapeDtypeStruct((S, H * E), CDT), jax.ShapeDtypeStruct((S, H * E), F32),
                   jax.ShapeDtypeStruct((S, H * E), CDT), jax.ShapeDtypeStruct((S, H * E), CDT),
                   jax.ShapeDtypeStruct((H, 1, S), F32)],
        scratch_shapes=[pltpu.VMEM((S, E), F32), pltpu.VMEM((S, 1), F32), pltpu.VMEM((blk, E), F32),
                        pltpu.VMEM((blk, E), F32), pltpu.VMEM((1, blk), F32)],
        name=name, compiler_params=_cparams(("parallel", "arbitrary", "arbitrary")),
    )(proj, qc, proj, kc, proj, lse, o, do)


DIL_SLAB = 16 * DIL_BLOCK


def _rel_bucket(dist):
    max_exact = REL_BUCKETS // 2
    d = jnp.maximum(dist.astype(F32), 1.0)
    large = max_exact + (jnp.log(d / max_exact) / jnp.log(jnp.float32(REL_MAX_DISTANCE / max_exact))
                         * (REL_BUCKETS - max_exact)).astype(jnp.int32)
    large = jnp.minimum(large, REL_BUCKETS - 1)
    return jnp.where(dist < max_exact, dist, large)


def _bucket_table():
    i = jnp.arange(DIL_BLOCK)[:, None]
    j = jnp.arange(2 * DIL_BLOCK)[None, :]
    rel = DIL_BLOCK + i - j
    tabs = [_rel_bucket(jnp.clip(rel, 0, w // d) * d) for w, d in DIL_PATTERNS]
    return jnp.stack(tabs).astype(jnp.int32)


def _bias_table(rel_bias, buckets, name):
    P = buckets.shape[0]
    H = rel_bias.shape[1]

    def body(rb_ref, bk_ref, out_ref):
        h = pl.program_id(1)
        bk = bk_ref[0]
        val = jnp.zeros(bk.shape, F32)
        for b in range(REL_BUCKETS):
            val = jnp.where(bk == b, rb_ref[b, h], val)
        out_ref[0, 0] = val

    return pl.pallas_call(
        body, grid=(P, H),
        in_specs=[pl.BlockSpec(memory_space=pltpu.SMEM),
                  pl.BlockSpec((1, DIL_BLOCK, 2 * DIL_BLOCK), lambda p, h: (p, 0, 0))],
        out_specs=pl.BlockSpec((1, 1, DIL_BLOCK, 2 * DIL_BLOCK), lambda p, h: (p, h, 0, 0)),
        out_shape=jax.ShapeDtypeStruct((P, H, DIL_BLOCK, 2 * DIL_BLOCK), F32),
        name=name, compiler_params=_cparams(("parallel", "parallel")),
    )(rel_bias, buckets)


def _bias_table_bwd(dbias, buckets, name):
    P, H = dbias.shape[:2]

    def body(db_ref, bk_ref, out_ref):
        lane = lax.broadcasted_iota(jnp.int32, (1, REL_BUCKETS), 1)
        acc = jnp.zeros((1, REL_BUCKETS), F32)
        bk = bk_ref[...]
        db = db_ref[:, 0]
        for b in range(REL_BUCKETS):
            tot = jnp.sum(jnp.where(bk == b, db, 0.0))
            acc = jnp.where(lane == b, tot, acc)
        out_ref[0] = acc

    return pl.pallas_call(
        body, grid=(H,),
        in_specs=[pl.BlockSpec((P, 1, DIL_BLOCK, 2 * DIL_BLOCK), lambda h: (0, h, 0, 0)),
                  pl.BlockSpec((P, DIL_BLOCK, 2 * DIL_BLOCK), lambda h: (0, 0, 0))],
        out_specs=pl.BlockSpec((1, 1, REL_BUCKETS), lambda h: (h, 0, 0)),
        out_shape=jax.ShapeDtypeStruct((H, 1, REL_BUCKETS), F32),
        name=name, compiler_params=_cparams(("parallel",)),
    )(dbias, buckets)


def _dil_logits(q, kc, kp, bias_pc, prev_valid):
    B = DIL_BLOCK
    ii = lax.broadcasted_iota(jnp.int32, (B, B), 0)
    jj = lax.broadcasted_iota(jnp.int32, (B, B), 1)
    scale = HEAD_DIM ** -0.5
    s_c = lax.dot_general(q, kc, _NT, preferred_element_type=F32) * scale + bias_pc[:, B:]
    s_p = lax.dot_general(q, kp, _NT, preferred_element_type=F32) * scale + bias_pc[:, :B]
    return jnp.where(jj <= ii, s_c, NEG_INF), jnp.where((jj >= ii) & prev_valid, s_p, NEG_INF)


def _dil_specs(H, n_slabs):
    E, SL = DIL_BLOCK, DIL_SLAB
    cur = lambda off: pl.BlockSpec((SL, E), lambda h, g: (g, off + h))
    prev = lambda off: pl.BlockSpec((SL, E), lambda h, g: (jnp.maximum(g - 1, 0), off + h))
    nxt = lambda off: pl.BlockSpec((SL, E), lambda h, g: (jnp.minimum(g + 1, n_slabs - 1), off + h))
    bias = pl.BlockSpec((len(DIL_PATTERNS), 1, E, 2 * E), lambda h, g: (0, h, 0, 0))
    return cur, prev, nxt, bias


def _dil_fwd(proj, bias, H, name):
    S = proj.shape[0]
    E = B = DIL_BLOCK
    SL = DIL_SLAB
    P = len(DIL_PATTERNS)
    assert S % SL == 0
    n_slabs = S // SL

    def body(q_ref, kc_ref, kp_ref, vc_ref, vp_ref, b_ref, y_ref, lse_ref, kj, vj, o_s, l_s):
        g = pl.program_id(1)
        kj[0:SL, :] = kp_ref[...]
        kj[SL:2 * SL, :] = kc_ref[...]
        vj[0:SL, :] = vp_ref[...]
        vj[SL:2 * SL, :] = vc_ref[...]
        for p, (_, d) in enumerate(DIL_PATTERNS):
            def unit(u, carry, p=p, d=d):
                sg, r = u // d, u % d
                base = sg * (B * d) + r
                rows = pl.ds(base, B, stride=d)
                own = pl.ds(SL + base, B, stride=d)
                before = pl.ds(SL + base - B * d, B, stride=d)
                q = q_ref[rows, :].astype(CDT)
                s_c, s_p = _dil_logits(q, kj[own, :].astype(CDT), kj[before, :].astype(CDT), b_ref[p, 0],
                                       (g > 0) | (sg > 0))
                m = jnp.maximum(jnp.max(s_c, axis=-1, keepdims=True), jnp.max(s_p, axis=-1, keepdims=True))
                p_c = jnp.exp(s_c - m)
                p_p = jnp.exp(s_p - m)
                ssum = jnp.sum(p_c, axis=-1, keepdims=True) + jnp.sum(p_p, axis=-1, keepdims=True)
                o = (jnp.dot(p_c.astype(CDT), vj[own, :].astype(CDT), preferred_element_type=F32)
                     + jnp.dot(p_p.astype(CDT), vj[before, :].astype(CDT), preferred_element_type=F32)) / ssum
                o_s[p, rows, :] = o
                l_s[p, rows, :] = jnp.broadcast_to(m + jnp.log(ssum), (B, E))
                return carry

            lax.fori_loop(0, SL // B, unit, 0)
        ls = [l_s[p] for p in range(P)]
        m = functools.reduce(jnp.maximum, ls)
        w = [jnp.exp(l - m) for l in ls]
        tot = functools.reduce(jnp.add, w)
        y_ref[...] = functools.reduce(jnp.add, [(w[p] / tot) * o_s[p] for p in range(P)])
        lse_ref[...] = m + jnp.log(tot)

    cur, prev, _, bspec = _dil_specs(H, n_slabs)
    return pl.pallas_call(
        body, grid=(H, n_slabs),
        in_specs=[cur(0), cur(H), prev(H), cur(2 * H), prev(2 * H), bspec],
        out_specs=[cur(0), cur(0)],
        out_shape=[jax.ShapeDtypeStruct((S, H * E), F32)] * 2,
        scratch_shapes=[pltpu.VMEM((2 * SL, E), F32), pltpu.VMEM((2 * SL, E), F32),
                        pltpu.VMEM((P, SL, E), F32), pltpu.VMEM((P, SL, E), F32)],
        name=name, compiler_params=_cparams(("parallel", "parallel")),
    )(proj, proj, proj, proj, proj, bias)


def _dil_bwd(proj, bias, y, dy, lse, H, name):
    S = proj.shape[0]
    E = B = DIL_BLOCK
    SL = DIL_SLAB
    P = len(DIL_PATTERNS)
    assert S % SL == 0
    n_slabs = S // SL
    scale = E ** -0.5
    nt = lambda a, b: lax.dot_general(a, b, _NT, preferred_element_type=F32)
    tn = lambda a, b: jnp.dot(a.T.astype(CDT), b, preferred_element_type=F32)

    def body(q_ref, kc_ref, kp_ref, vc_ref, vp_ref, b_ref, y_ref, dy_ref, lse_ref, qn_ref, yn_ref, dyn_ref, lsen_ref,
             dq_ref, dk_ref, dv_ref, db_ref, kj, vj, dq_s, dk_s, dv_s, dl_s, dln_s):
        g = pl.program_id(1)
        kj[0:SL, :] = kp_ref[...]
        kj[SL:2 * SL, :] = kc_ref[...]
        vj[0:SL, :] = vp_ref[...]
        vj[SL:2 * SL, :] = vc_ref[...]
        dq_s[...] = jnp.zeros(dq_s.shape, F32)
        dk_s[...] = jnp.zeros(dk_s.shape, F32)
        dv_s[...] = jnp.zeros(dv_s.shape, F32)
        dl_s[...] = jnp.broadcast_to(jnp.sum(dy_ref[...] * y_ref[...], axis=-1, keepdims=True), (SL, E))
        dln_s[...] = jnp.broadcast_to(jnp.sum(dyn_ref[...] * yn_ref[...], axis=-1, keepdims=True), (SL, E))

        @pl.when(g == 0)
        def _():
            db_ref[...] = jnp.zeros(db_ref.shape, F32)

        for p, (_, d) in enumerate(DIL_PATTERNS):
            def unit(u, carry, p=p, d=d):
                sg, r = u // d, u % d
                base = sg * (B * d) + r
                rows = pl.ds(base, B, stride=d)
                own = pl.ds(SL + base, B, stride=d)
                before = pl.ds(SL + base - B * d, B, stride=d)
                q = q_ref[rows, :].astype(CDT)
                dyc = dy_ref[rows, :].astype(CDT)
                delta = dl_s[rows, :][:, 0:1]
                kc, kp = kj[own, :].astype(CDT), kj[before, :].astype(CDT)
                vc, vp = vj[own, :].astype(CDT), vj[before, :].astype(CDT)
                s_c, s_p = _dil_logits(q, kc, kp, b_ref[p, 0], (g > 0) | (sg > 0))
                ls = lse_ref[rows, :][:, 0:1]
                p_c = jnp.exp(s_c - ls)
                p_p = jnp.exp(s_p - ls)
                ds_c = p_c * (nt(dyc, vc) - delta)
                ds_p = p_p * (nt(dyc, vp) - delta)
                dsc, dsp = ds_c * scale, ds_p * scale
                dq_s[rows, :] += (jnp.dot(dsc.astype(CDT), kc, preferred_element_type=F32)
                                  + jnp.dot(dsp.astype(CDT), kp, preferred_element_type=F32))
                dk_s[rows, :] += tn(dsc, q)
                dv_s[rows, :] += tn(p_c, dyc)

                @pl.when(sg > 0)
                def _():
                    prows = pl.ds(base - B * d, B, stride=d)
                    dk_s[prows, :] += tn(dsp, q)
                    dv_s[prows, :] += tn(p_p, dyc)

                db_ref[p, 0] += jnp.concatenate([ds_p, ds_c], axis=1)
                return carry

            lax.fori_loop(0, SL // B, unit, 0)

            def after(r, carry, p=p, d=d):
                rows = pl.ds(r, B, stride=d)
                krows = pl.ds(SL - B * d + r, B, stride=d)
                q = qn_ref[rows, :].astype(CDT)
                dyc = dyn_ref[rows, :].astype(CDT)
                k, v = kc_ref[krows, :].astype(CDT), vc_ref[krows, :].astype(CDT)
                ii = lax.broadcasted_iota(jnp.int32, (B, B), 0)
                jj = lax.broadcasted_iota(jnp.int32, (B, B), 1)
                s = jnp.where((jj >= ii) & (g < n_slabs - 1), nt(q, k) * scale + b_ref[p, 0][:, :B], NEG_INF)
                p_ = jnp.exp(s - lsen_ref[rows, :][:, 0:1])
                ds = p_ * (nt(dyc, v) - dln_s[rows, :][:, 0:1])
                dk_s[krows, :] += tn(ds * scale, q)
                dv_s[krows, :] += tn(p_, dyc)
                return carry

            lax.fori_loop(0, d, after, 0)

        dq_ref[...] = dq_s[...].astype(dq_ref.dtype)
        dk_ref[...] = dk_s[...].astype(dk_ref.dtype)
        dv_ref[...] = dv_s[...].astype(dv_ref.dtype)

    cur, prev, nxt, bspec = _dil_specs(H, n_slabs)
    slab = pltpu.VMEM((SL, E), F32)
    return pl.pallas_call(
        body, grid=(H, n_slabs),
        in_specs=[cur(0), cur(H), prev(H), cur(2 * H), prev(2 * H), bspec, cur(0), cur(0), cur(0),
                  nxt(0), nxt(0), nxt(0), nxt(0)],
        out_specs=[cur(0), cur(0), cur(0), bspec],
        out_shape=[jax.ShapeDtypeStruct((S, H * E), CDT)] * 3 + [jax.ShapeDtypeStruct((P, H, B, 2 * B), F32)],
        scratch_shapes=[pltpu.VMEM((2 * SL, E), F32), pltpu.VMEM((2 * SL, E), F32), slab, slab, slab, slab, slab],
        name=name, compiler_params=_cparams(("parallel", "arbitrary")),
    )(proj, proj, proj, proj, proj, bias, y, dy, lse, proj, y, dy, lse)


def _adamw(w, g, m, v, name, br=128):
    R, C = w.shape
    br = br if R % br == 0 else R

    def body(w_ref, g_ref, m_ref, v_ref, g_out, d_ref, nm_ref, nv_ref):
        g_ = g_ref[...]
        g_out[...] = g_
        m_ = ADAM_B1 * m_ref[...] + (1.0 - ADAM_B1) * g_
        v_ = ADAM_B2 * v_ref[...] + (1.0 - ADAM_B2) * jnp.square(g_)
        m_hat = m_ / (1.0 - ADAM_B1 ** ADAM_STEP)
        v_hat = v_ / (1.0 - ADAM_B2 ** ADAM_STEP)
        d_ref[...] = -ADAM_LR * (m_hat / (jnp.sqrt(v_hat) + ADAM_EPS) + ADAM_WD * w_ref[...])
        nm_ref[...] = m_
        nv_ref[...] = v_

    blk = pl.BlockSpec((br, C), lambda i: (i, 0))
    return pl.pallas_call(
        body, grid=(R // br,), in_specs=[blk] * 4, out_specs=[blk] * 4,
        out_shape=[jax.ShapeDtypeStruct((R, C), F32)] * 4,
        name=name, compiler_params=_cparams(("parallel",)),
    )(w, g, m, v)


_HBM = pl.BlockSpec(memory_space=pltpu.HBM)
_SEM = pl.BlockSpec(memory_space=pltpu.SEMAPHORE)
_ANY = pl.BlockSpec(memory_space=pl.ANY)
_VMEM = pl.BlockSpec(memory_space=pltpu.VMEM)
_TOKEN = jax.ShapeDtypeStruct((8, 128), F32)


def _split_params():
    return pltpu.CompilerParams(has_side_effects=pltpu.SideEffectType.DATAFLOW_SIDE_EFFECTING)


def _place():
    x, y, c = lax.axis_index("x"), lax.axis_index("y"), lax.axis_index("c")
    chips = [(1 - x, y), (x, 1 - y), (1 - x, 1 - y)]
    return x, y, c, chips


def _tie(v, tokens, name):
    flat = v.reshape(1, -1)

    def body(v_ref, *rest):
        rest[-1][...] = v_ref[...]

    return pl.pallas_call(body, in_specs=[_VMEM] + [_ANY] * len(tokens), out_specs=_VMEM,
                          out_shape=jax.ShapeDtypeStruct(flat.shape, flat.dtype), name=name,
                          compiler_params=_cparams())(flat, *tokens).reshape(v.shape)


def _row_block(R, pref=256):
    return _tile(R, pref) if R % 128 == 0 else R


def _slot():
    return 2 * lax.axis_index("x") + lax.axis_index("y")


def _cast_into_slot(w, name):
    R, C = w.shape
    br = _row_block(R)

    def body(w_ref, out_ref):
        out_ref[...] = w_ref[...].astype(out_ref.dtype)

    return pl.pallas_call(
        body, grid=(R // br,),
        in_specs=[pl.BlockSpec((br, C), lambda i: (i, 0))],
        out_specs=pl.BlockSpec((None, br, C), lambda i: (_slot(), i, 0)),
        out_shape=jax.ShapeDtypeStruct((N_CHIPS, R, C), CDT),
        name=name, compiler_params=_cparams(("parallel",)),
    )(w)


def _gather_copies(src_ref, dst_ref, send_sems, recv_sems, incoming):
    Rh = src_ref.shape[1] // 2
    x, y, c, chips = _place()
    slot = 2 * x + y

    def half(ref, s, hf):
        return ref.at[s, pl.ds(hf * Rh, Rh), :]

    copies = []
    for j, (cx, cy) in enumerate(chips):
        for e in range(2):
            copies.append(pltpu.make_async_remote_copy(
                src_ref=half(src_ref, slot, c), dst_ref=half(dst_ref, 2 * cx + cy, e) if incoming else half(dst_ref, slot, c),
                send_sem=send_sems.at[2 * j + e], recv_sem=recv_sems.at[2 * j + (e if incoming else c)],
                device_id=(cx, cy, e), device_id_type=MESH))
    return copies


def _gather_start(buf, after, name):
    n_after = len(after)

    def body(*refs):
        buf_ref = refs[0]
        send_sems, recv_sems, out_ref, token = refs[1 + n_after:]
        for cp in _gather_copies(buf_ref, out_ref, send_sems, recv_sems, incoming=False):
            cp.start()
        token[...] = jnp.zeros(token.shape, token.dtype)

    return pl.pallas_call(
        body, in_specs=[_HBM] + [_ANY] * n_after, out_specs=(_SEM, _SEM, _HBM, _VMEM),
        out_shape=(pltpu.SemaphoreType.DMA((6,)), pltpu.SemaphoreType.DMA((6,)), pltpu.HBM(buf.shape, buf.dtype), _TOKEN),
        input_output_aliases={0: 2}, name=name, compiler_params=_split_params(),
    )(pltpu.with_memory_space_constraint(buf, pltpu.HBM), *after)


def _gather_wait(send_sems, recv_sems, buf, after, name):
    def body(buf_ref, send_sems, recv_sems, after_ref, out_ref):
        for cp in _gather_copies(buf_ref, out_ref, send_sems, recv_sems, incoming=False):
            cp.wait_send()
        for cp in _gather_copies(buf_ref, out_ref, send_sems, recv_sems, incoming=True):
            cp.wait_recv()

    return pl.pallas_call(
        body, in_specs=[_HBM, _SEM, _SEM, _ANY], out_specs=_HBM, out_shape=pltpu.HBM(buf.shape, buf.dtype),
        input_output_aliases={0: 0}, name=name, compiler_params=_split_params(),
    )(buf, send_sems, recv_sems, after)


def _scatter_copies(g_ref, land_ref, send_sems, recv_sems, incoming):
    Rh = g_ref.shape[1] // 2
    x, y, c, _ = _place()
    me = 4 * x + 2 * y + c
    copies = []
    for k in range(1, N_DEV):
        px, py, pc = (x + (k >> 2)) % 2, (y + ((k >> 1) & 1)) % 2, (c + (k & 1)) % 2
        copies.append(pltpu.make_async_remote_copy(
            src_ref=g_ref.at[2 * px + py, pl.ds(pc * Rh, Rh), :],
            dst_ref=land_ref.at[4 * px + 2 * py + pc if incoming else me],
            send_sem=send_sems.at[k - 1], recv_sem=recv_sems.at[k - 1], device_id=(px, py, pc), device_id_type=MESH))
    return copies


def _scatter_start(g, name):
    ns, R, C = g.shape

    def body(g_ref, land_ref, send_sems, recv_sems, g_thru, land_thru, token):
        for cp in _scatter_copies(g_ref, land_thru, send_sems, recv_sems, incoming=False):
            cp.start()
        token[...] = jnp.zeros(token.shape, token.dtype)

    land = lax.empty((N_DEV, R // 2, C), g.dtype)
    n = N_DEV - 1
    return pl.pallas_call(
        body, in_specs=[_HBM, _HBM], out_specs=(_SEM, _SEM, _HBM, _HBM, _VMEM),
        out_shape=(pltpu.SemaphoreType.DMA((n,)), pltpu.SemaphoreType.DMA((n,)), pltpu.HBM(g.shape, g.dtype),
                   pltpu.HBM(land.shape, land.dtype), _TOKEN),
        input_output_aliases={0: 2, 1: 3}, name=name, compiler_params=_split_params(),
    )(pltpu.with_memory_space_constraint(g, pltpu.HBM), pltpu.with_memory_space_constraint(land, pltpu.HBM))


def _scatter_wait(send_sems, recv_sems, g, land, name):
    def body(g_ref, land_ref, send_sems, recv_sems, g_out, land_out):
        for cp in _scatter_copies(g_ref, land_out, send_sems, recv_sems, incoming=False):
            cp.wait_send()
        for cp in _scatter_copies(g_ref, land_out, send_sems, recv_sems, incoming=True):
            cp.wait_recv()

    return pl.pallas_call(
        body, in_specs=[_HBM, _HBM, _SEM, _SEM], out_specs=(_HBM, _HBM),
        out_shape=(pltpu.HBM(g.shape, g.dtype), pltpu.HBM(land.shape, land.dtype)),
        input_output_aliases={0: 0, 1: 1}, name=name, compiler_params=_split_params(),
    )(g, land, send_sems, recv_sems)


def _device_sum(land, g, layer, n_layers, prev, name):
    nd, Rh, C = land.shape
    br = _row_block(Rh)
    nb = Rh // br
    core = lambda: lax.axis_index("c")
    me = lambda: 2 * _slot() + core()

    def body(*refs):
        own = refs[nd][...]
        acc = None
        for d in range(nd):
            t = jnp.where(me() == d, own, refs[d][...]).astype(F32)
            acc = t if acc is None else acc + t
        refs[-1][...] = acc

    def piece(d):
        return pl.BlockSpec((None, br, C), lambda i: (jnp.where(me() == d, (d + 1) % nd, d), i, 0))

    ins = [land] * nd + [g] + ([prev] if prev is not None else [])
    return pl.pallas_call(
        body, grid=(nb,),
        in_specs=[piece(d) for d in range(nd)]
        + [pl.BlockSpec((None, br, C), lambda i: (_slot(), core() * nb + i, 0))]
        + ([_ANY] if prev is not None else []),
        out_specs=pl.BlockSpec((None, br, C), lambda i: (layer, core() * nb + i, 0)),
        out_shape=jax.ShapeDtypeStruct((n_layers, 2 * Rh, C), F32),
        input_output_aliases={nd + 1: 0} if prev is not None else {},
        name=name, compiler_params=_cparams(("parallel",)),
    )(*ins)


def _join_halves(g, layer, name):
    _, R, C = g.shape
    Rh = R // 2

    def body(g_ref, out_ref, send_sem, recv_sem):
        x, y, c, _ = _place()
        mine = pl.ds(c * Rh, Rh)
        cp = pltpu.make_async_remote_copy(src_ref=g_ref.at[layer, mine, :], dst_ref=out_ref.at[layer, mine, :],
                                          send_sem=send_sem, recv_sem=recv_sem, device_id=(x, y, 1 - c),
                                          device_id_type=MESH)
        cp.start()
        other = out_ref.at[layer, pl.ds((1 - c) * Rh, Rh), :]
        pltpu.make_async_remote_copy(src_ref=other, dst_ref=other, send_sem=send_sem, recv_sem=recv_sem,
                                     device_id=(x, y, 1 - c), device_id_type=MESH).wait_recv()
        cp.wait_send()

    return pl.pallas_call(
        body, in_specs=[_HBM], out_specs=_HBM, out_shape=jax.ShapeDtypeStruct(g.shape, g.dtype),
        input_output_aliases={0: 0},
        scratch_shapes=[pltpu.SemaphoreType.DMA, pltpu.SemaphoreType.DMA],
        name=name, compiler_params=pltpu.CompilerParams(),
    )(g)


def _all_reduce_small(v, name):
    rows, cols = v.shape

    def body(v_ref, out_ref, buf, send_sems, recv_sems):
        x, y, c, _ = _place()
        me = 4 * x + 2 * y + c
        buf[me] = v_ref[...]
        peers = []
        for k in range(1, N_DEV):
            px, py, pc = (x + (k >> 2)) % 2, (y + ((k >> 1) & 1)) % 2, (c + (k & 1)) % 2
            peers.append((px, py, pc))
        sends = []
        for k, peer in enumerate(peers):
            cp = pltpu.make_async_remote_copy(src_ref=v_ref, dst_ref=buf.at[me], send_sem=send_sems.at[k],
                                              recv_sem=recv_sems.at[k], device_id=peer, device_id_type=MESH)
            cp.start()
            sends.append(cp)
        for k, (px, py, pc) in enumerate(peers):
            pltpu.make_async_remote_copy(src_ref=v_ref, dst_ref=buf.at[4 * px + 2 * py + pc], send_sem=send_sems.at[k],
                                         recv_sem=recv_sems.at[k], device_id=(px, py, pc),
                                         device_id_type=MESH).wait_recv()
        for cp in sends:
            cp.wait_send()
        acc = buf[0]
        for i in range(1, N_DEV):
            acc = acc + buf[i]
        out_ref[...] = acc

    vmem = pl.BlockSpec(memory_space=pltpu.VMEM)
    return pl.pallas_call(
        body, in_specs=[vmem], out_specs=vmem, out_shape=jax.ShapeDtypeStruct((rows, cols), F32),
        scratch_shapes=[pltpu.VMEM((N_DEV, rows, cols), F32), pltpu.SemaphoreType.DMA((N_DEV - 1,)),
                        pltpu.SemaphoreType.DMA((N_DEV - 1,))],
        name=name, compiler_params=pltpu.CompilerParams(),
    )(v)


def _reduce_scatter_finish(started, layer, n_layers, prev, tag):
    send_sems, recv_sems, g, land, _ = started
    g, land = _scatter_wait(send_sems, recv_sems, g, land, f"rs_wait_{tag}")
    f = _device_sum(land, g, layer, n_layers, prev, f"rs_sum_{tag}")
    return _join_halves(f, layer, f"rs_join_{tag}")


def _split_w_in(wg, Hf):
    ns, D, cols = wg.shape
    nat = wg.transpose(1, 0, 2).reshape(D, ns * cols)
    a = 3 * Hf * HEAD_DIM
    return jnp.concatenate([nat[:, :a], nat[:, a + Hf:]], axis=1), nat[:, a:a + Hf].T


def _join_dw_in(dw6, dwf_t, Hf):
    D = dw6.shape[0]
    a = 3 * Hf * HEAD_DIM
    nat = jnp.concatenate([dw6[:, :a], dwf_t.T.astype(dw6.dtype), dw6[:, a:]], axis=1)
    return nat.reshape(D, N_CHIPS, nat.shape[1] // N_CHIPS).transpose(1, 0, 2)


def _tied(v, tokens, name):
    return _tie(v, tokens, name) if tokens else v


def _layer_fwd(x, p, weight, bias, tokens, tag):
    Hf, Hd = p["forget_b"].shape[0], bias.shape[1]
    h1 = _rms_fwd(x, _tied(p["norm1_g"], tokens, f"tie_norm1_{tag}"), f"norm1_{tag}")
    w6, wf_t = _split_w_in(weight("w_in", h1), Hf)
    n_a = 3 * Hf * HEAD_DIM
    proj_a = _mm_nn(h1, w6, f"proj_a_{tag}", [CDT], epi=lambda acc: (acc,), b_cols=(0, n_a))[0]
    proj_b = _mm_nn(h1, w6, f"proj_b_{tag}", [F32], b_cols=(n_a, w6.shape[1] - n_a))[0]
    f_t = _mm_nt(wf_t, h1, f"fproj_{tag}", [F32])[0]
    qc, kc = _fox_bias_operands(_gates_fwd(f_t, p["forget_b"], f"gates_{tag}"))
    y_a, lse_a = _fox_fwd(proj_a, qc, kc, Hf, f"fox_{tag}")
    y_b, lse_b = _dil_fwd(proj_b, bias, Hd, f"dil_{tag}")
    mixed = jnp.concatenate([_rms_fwd(y_a, p["outnorm_a_g"], f"norm_a_{tag}"),
                             _rms_fwd(y_b, p["outnorm_b_g"], f"norm_b_{tag}")], axis=1)
    w_out = weight("w_out", mixed)
    w_out = w_out.reshape(-1, w_out.shape[2])
    x1 = _mm_nn(mixed, w_out, f"attn_out_{tag}", [F32], extras=[x])[0]
    h2 = _rms_fwd(x1, p["norm2_g"], f"norm2_{tag}")
    w_mi = weight("w_mlp_in", h2)
    u, act = _mm_nn(h2, w_mi, f"mlp_in_{tag}", [CDT, CDT], b_slots=True,
                    epi=lambda acc: (acc, jnp.square(jnp.maximum(acc, 0.0))))
    w_mo = weight("w_mlp_out", act)
    w_mo = w_mo.reshape(-1, w_mo.shape[2])
    x2 = _mm_nn(act, w_mo, f"mlp_out_{tag}", [F32], extras=[x1])[0]
    saved = dict(x=x, h1=h1, proj_a=proj_a, proj_b=proj_b, f_t=f_t, qc=qc, kc=kc, y_a=y_a, lse_a=lse_a, y_b=y_b,
                 lse_b=lse_b, mixed=mixed, x1=x1, h2=h2, u=u, act=act, w6=w6, wf_t=wf_t, w_out=w_out, w_mi=w_mi,
                 w_mo=w_mo)
    return x2, saved


def _layer_bwd(dx2, dx2c, p, send, bias, sv, defer_w_out, tag):
    Hf, Hd = p["forget_b"].shape[0], bias.shape[1]
    E = HEAD_DIM
    rows = lambda g: g.reshape(N_CHIPS, -1, g.shape[1])
    du = _mm_nt(dx2c, sv["w_mo"], f"d_act_{tag}", [CDT], extras=[sv["u"]],
                epi=lambda acc, u: (acc * (2.0 * jnp.maximum(u.astype(F32), 0.0)),))[0]
    tokens = send("w_mlp_out", rows(_mm_tn(sv["act"], dx2c, f"dw_mlp_out_{tag}", CDT)))
    dh2 = _mm_nt(du, sv["w_mi"], f"d_h2_{tag}", [F32], b_slots=True)[0]
    tokens = tokens + send("w_mlp_in", _mm_tn(sv["h2"], du, f"dw_mlp_in_{tag}", CDT, out_slots=N_CHIPS))
    dx1, dx1c, g_norm2 = _rms_bwd(sv["x1"], _tied(p["norm2_g"], tokens, f"tie_norm2_{tag}"), dh2, dx2,
                                  f"d_norm2_{tag}")
    dmixed = _mm_nt(dx1c, sv["w_out"], f"d_mixed_{tag}", [F32])[0]
    send_w_out = lambda: send("w_out", rows(_mm_tn(sv["mixed"], dx1c, f"dw_out_{tag}", CDT)))
    tokens = [] if defer_w_out else send_w_out()
    dy_a, _, g_na = _rms_bwd(sv["y_a"], _tied(p["outnorm_a_g"], tokens, f"tie_norm_a_{tag}"), dmixed, None,
                             f"d_norm_a_{tag}", dh_col=0)
    dy_b, _, g_nb = _rms_bwd(sv["y_b"], p["outnorm_b_g"], dmixed, None, f"d_norm_b_{tag}", dh_col=1)
    dq_a, dcq, dk_a, dv_a, dck = _fox_bwd(sv["proj_a"], sv["qc"], sv["kc"], sv["lse_a"], sv["y_a"], dy_a, Hf,
                                          f"fox_bwd_{tag}")
    df, dfc, g_fb = _gates_bwd(sv["f_t"], p["forget_b"], dcq[:, ::E].T, dck.reshape(Hf, -1), f"d_gates_{tag}")
    dq_b, dk_b, dv_b, dbias = _dil_bwd(sv["proj_b"], bias, sv["y_b"], dy_b, sv["lse_b"], Hd, f"dil_bwd_{tag}")
    dproj = jnp.concatenate([dq_a, dk_a, dv_a, dq_b, dk_b, dv_b], axis=1)
    g_w6 = _mm_tn(sv["h1"], dproj, f"dw_in_{tag}", CDT)
    g_wf_t = _mm_nn(dfc, sv["h1"], f"dw_f_{tag}", [F32])[0]
    tokens = send("w_in", _join_dw_in(g_w6, g_wf_t, Hf))
    dh1_f = _mm_tn(dfc, _tied(sv["wf_t"], tokens, f"tie_wf_{tag}"), f"d_h1_f_{tag}", F32)
    dh1 = _mm_nt(dproj, sv["w6"], f"d_h1_{tag}", [F32], extras=[dh1_f])[0]
    dx, dxc, g_norm1 = _rms_bwd(sv["x"], p["norm1_g"], dh1, dx1, f"d_norm1_{tag}")
    grads = dict(norm1_g=g_norm1[0], norm2_g=g_norm2[0], outnorm_a_g=g_na[0], outnorm_b_g=g_nb[0],
                 forget_b=g_fb[:, 0], dbias=dbias)
    return dx, dxc, grads, (send_w_out if defer_w_out else None)


_LAYER_SMALL = ("norm1_g", "forget_b", "outnorm_a_g", "outnorm_b_g", "norm2_g")


def _local_step(x, target, small, weight, send, tokens):
    depth = small["norm1_g"].shape[0]
    buckets = _bucket_table()
    bias = _bias_table(small["rel_bias"], buckets, "bias_table")
    layers, saved = [], []
    for l in range(depth):
        p = {k: small[k][l] for k in _LAYER_SMALL}
        layers.append(p)
        x, sv = _layer_fwd(x, p, functools.partial(weight, l), bias, tokens if l == 0 else [], f"l{l}")
        saved.append(sv)
    dx, dxc, g_final, loss = _loss_bwd(x, small["final_norm_g"], target, "loss")
    layer_grads = [None] * depth
    for l in reversed(range(depth)):
        dx, dxc, layer_grads[l], last = _layer_bwd(dx, dxc, layers[l], functools.partial(send, l), bias, saved[l],
                                                   l == 0, f"l{l}")
    last()
    dbias = functools.reduce(jnp.add, [g["dbias"] for g in layer_grads])
    g_rel = _bias_table_bwd(dbias, buckets, "d_bias_table")[:, 0, :].T
    small_grads = dict(final_norm_g=g_final[0], rel_bias=g_rel,
                       **{k: jnp.stack([g[k] for g in layer_grads]) for k in _LAYER_SMALL})
    return loss[0, 0], dx, small_grads


_BIG = ("w_in", "w_out", "w_mlp_in", "w_mlp_out")
_SMALL = ("norm1_g", "forget_b", "rel_bias", "outnorm_a_g", "outnorm_b_g", "norm2_g", "final_norm_g")
_ORDER = ("norm1_g", "w_in", "forget_b", "rel_bias", "outnorm_a_g", "outnorm_b_g", "w_out", "norm2_g", "w_mlp_in",
          "w_mlp_out", "final_norm_g")


def _pack_small(d):
    flat = jnp.concatenate([d[k].reshape(-1) for k in _SMALL])
    rows = -(-flat.shape[0] // (8 * SMALL_COLS)) * 8
    return jnp.pad(flat, (0, rows * SMALL_COLS - flat.shape[0])).reshape(rows, SMALL_COLS)


def _unpack_small(packed, like):
    flat, out, at = packed.reshape(-1), {}, 0
    for k in _SMALL:
        n = like[k].size
        out[k] = flat[at:at + n].reshape(like[k].shape)
        at += n
    return out


def kernel(x, norm1_g, w_in, forget_b, rel_bias, outnorm_a_g, outnorm_b_g, w_out, norm2_g, w_mlp_in, w_mlp_out, final_norm_g, loss_target, m_norm1_g, m_w_in, m_forget_b, m_rel_bias, m_outnorm_a_g, m_outnorm_b_g, m_w_out, m_norm2_g, m_w_mlp_in, m_w_mlp_out, m_final_norm_g, v_norm1_g, v_w_in, v_forget_b, v_rel_bias, v_outnorm_a_g, v_outnorm_b_g, v_w_out, v_norm2_g, v_w_mlp_in, v_w_mlp_out, v_final_norm_g):
    w = dict(norm1_g=norm1_g, w_in=w_in, forget_b=forget_b, rel_bias=rel_bias, outnorm_a_g=outnorm_a_g,
             outnorm_b_g=outnorm_b_g, w_out=w_out, norm2_g=norm2_g, w_mlp_in=w_mlp_in, w_mlp_out=w_mlp_out,
             final_norm_g=final_norm_g)
    m = dict(norm1_g=m_norm1_g, w_in=m_w_in, forget_b=m_forget_b, rel_bias=m_rel_bias, outnorm_a_g=m_outnorm_a_g,
             outnorm_b_g=m_outnorm_b_g, w_out=m_w_out, norm2_g=m_norm2_g, w_mlp_in=m_w_mlp_in,
             w_mlp_out=m_w_mlp_out, final_norm_g=m_final_norm_g)
    v = dict(norm1_g=v_norm1_g, w_in=v_w_in, forget_b=v_forget_b, rel_bias=v_rel_bias, outnorm_a_g=v_outnorm_a_g,
             outnorm_b_g=v_outnorm_b_g, w_out=v_w_out, norm2_g=v_norm2_g, w_mlp_in=v_w_mlp_in,
             w_mlp_out=v_w_mlp_out, final_norm_g=v_final_norm_g)
    depth = w_in.shape[0]
    small = {k: w[k] for k in _SMALL}

    gathers, tokens = {}, []
    for l in range(depth):
        for k in _BIG:
            buf = _cast_into_slot(w[k][l], f"cast_{k}_l{l}")
            send_sems, recv_sems, buf, token = _gather_start(buf, tokens, f"gather_start_{k}_l{l}")
            gathers[l, k], tokens = (send_sems, recv_sems, buf), [token]

    def weight(l, k, after):
        return _gather_wait(*gathers[l, k], after, f"gather_wait_{k}_l{l}")

    scatters = {}

    def send(l, k, g):
        scatters[l, k] = _scatter_start(g, f"rs_start_{k}_l{l}")
        return [scatters[l, k][4]]

    loss, grad_x, small_grads = _local_step(x[0], loss_target[0], small, weight, send, tokens)
    loss = lax.psum(loss, ("x", "y", "c"))

    grads = {}
    for (l, k), started in scatters.items():
        grads[k] = _reduce_scatter_finish(started, l, depth, grads.get(k), f"{k}_l{l}")
    grads.update(_unpack_small(_all_reduce_small(_pack_small(small_grads), "small_all_reduce"), small))

    delta, new_m, new_v = {}, {}, {}
    for k in _BIG:
        shape = w[k].shape
        flat = lambda t: t.reshape(-1, shape[-1])
        g_, d_, m_, v_ = _adamw(flat(w[k]), flat(grads[k]), flat(m[k]), flat(v[k]), f"adamw_{k}")
        grads[k], delta[k], new_m[k], new_v[k] = g_.reshape(shape), d_.reshape(shape), m_.reshape(shape), v_.reshape(shape)
    _, d_, m_, v_ = _adamw(_pack_small(small), _pack_small({k: grads[k] for k in _SMALL}),
                           _pack_small({k: m[k] for k in _SMALL}), _pack_small({k: v[k] for k in _SMALL}), "adamw_small")
    delta.update(_unpack_small(d_, small))
    new_m.update(_unpack_small(m_, small))
    new_v.update(_unpack_small(v_, small))

    return (loss, grad_x[None], *[grads[k] for k in _ORDER], *[delta[k] for k in _ORDER],
            *[new_m[k] for k in _ORDER], *[new_v[k] for k in _ORDER])
```

```python
import functools

import jax
import jax.numpy as jnp
from jax import lax
from jax.experimental import pallas as pl
from jax.experimental.pallas import tpu as pltpu

F32 = jnp.float32
CDT = jnp.bfloat16
HEAD_DIM = 128
NORM_EPS = 1e-6
NEG_INF = -1e30
LOG2E = 1.4426950408889634
REL_BUCKETS = 32
REL_MAX_DISTANCE = 2048
DIL_PATTERNS = ((128, 1), (512, 4), (2048, 16))
DIL_BLOCK = 128
ADAM_LR, ADAM_B1, ADAM_B2, ADAM_EPS, ADAM_WD, ADAM_STEP = 0.001, 0.9, 0.999, 1e-08, 0.01, 10
N_CHIPS = 4
N_DEV = 8
VMEM_LIMIT_BYTES = 56 * 1024 * 1024
SMALL_COLS = 1024
MESH = pl.DeviceIdType.MESH


def _cparams(sem=None):
    return pltpu.CompilerParams(dimension_semantics=sem, vmem_limit_bytes=VMEM_LIMIT_BYTES)


def _tile(dim, pref):
    t = min(pref, dim)
    t -= t % 128
    while t >= 128:
        if dim % t == 0:
            return t
        t -= 128
    return dim


def _rowwise(fn, ins, out_dtypes, name, bs=256, consts=()):
    R, C = ins[0].shape
    bs = min(bs, R)
    n_in, n_c = len(ins), len(consts)

    def body(*refs):
        vals = [r[...] for r in refs[:n_in + n_c]]
        res = fn(*vals)
        for o, r in zip(refs[n_in + n_c:], res):
            o[...] = r.astype(o.dtype)

    row = pl.BlockSpec((bs, C), lambda i: (i, 0))
    return pl.pallas_call(
        body, grid=(R // bs,),
        in_specs=[row] * n_in + [pl.BlockSpec((1, c.shape[-1]), lambda i: (0, 0)) for c in consts],
        out_specs=[row] * len(out_dtypes),
        out_shape=[jax.ShapeDtypeStruct((R, C), d) for d in out_dtypes],
        name=name, compiler_params=_cparams(("parallel",)),
    )(*ins, *[c.reshape(1, -1) for c in consts])


def _rms_fwd(x, g, name):
    def fn(xf, gg):
        r = lax.rsqrt(jnp.mean(xf * xf, axis=-1, keepdims=True) + NORM_EPS)
        return ((xf * r) * gg,)
    return _rowwise(fn, [x], [CDT], name, consts=[g])[0]


def _rms_bwd(x, g, dh, dres, name, bs=256, dh_col=0):
    S, D = x.shape
    bs = min(bs, S)
    has_res = dres is not None

    def body(*refs):
        x_ref, g_ref, dh_ref = refs[:3]
        dx_ref, dxc_ref, dg_ref = refs[-3:]
        xf = x_ref[...]
        r = lax.rsqrt(jnp.mean(xf * xf, axis=-1, keepdims=True) + NORM_EPS)
        xhat = xf * r
        dh_ = dh_ref[...].astype(F32)
        dxhat = dh_ * g_ref[...]
        dx = r * (dxhat - xhat * jnp.mean(dxhat * xhat, axis=-1, keepdims=True))
        if has_res:
            dx = dx + refs[3][...]
        dx_ref[...] = dx
        dxc_ref[...] = dx.astype(dxc_ref.dtype)
        part = jnp.sum(dh_ * xhat, axis=0, keepdims=True)

        @pl.when(pl.program_id(0) == 0)
        def _():
            dg_ref[...] = part

        @pl.when(pl.program_id(0) > 0)
        def _():
            dg_ref[...] += part

    row = pl.BlockSpec((bs, D), lambda i: (i, 0))
    one = pl.BlockSpec((1, D), lambda i: (0, 0))
    ins = [x, g.reshape(1, D), dh] + ([dres] if has_res else [])
    return pl.pallas_call(
        body, grid=(S // bs,),
        in_specs=[row, one, pl.BlockSpec((bs, D), lambda i: (i, dh_col))] + ([row] if has_res else []),
        out_specs=[row, row, one],
        out_shape=[jax.ShapeDtypeStruct((S, D), F32), jax.ShapeDtypeStruct((S, D), CDT),
                   jax.ShapeDtypeStruct((1, D), F32)],
        name=name, compiler_params=_cparams(("arbitrary",)),
    )(*ins)


def _loss_bwd(x, g, target, name, bs=256):
    S, D = x.shape
    bs = min(bs, S)

    def body(x_ref, g_ref, t_ref, dx_ref, dxc_ref, dg_ref, loss_ref):
        xf = x_ref[...]
        r = lax.rsqrt(jnp.mean(xf * xf, axis=-1, keepdims=True) + NORM_EPS)
        xhat = xf * r
        err = xhat * g_ref[...] - t_ref[...]
        lpart = 0.5 * jnp.sum(jnp.mean(err * err, axis=-1, keepdims=True), axis=0, keepdims=True)
        dy = err / D
        dxhat = dy * g_ref[...]
        dx = r * (dxhat - xhat * jnp.mean(dxhat * xhat, axis=-1, keepdims=True))
        dx_ref[...] = dx
        dxc_ref[...] = dx.astype(dxc_ref.dtype)
        gpart = jnp.sum(dy * xhat, axis=0, keepdims=True)

        @pl.when(pl.program_id(0) == 0)
        def _():
            dg_ref[...] = gpart
            loss_ref[...] = lpart

        @pl.when(pl.program_id(0) > 0)
        def _():
            dg_ref[...] += gpart
            loss_ref[...] += lpart

    row = pl.BlockSpec((bs, D), lambda i: (i, 0))
    one = pl.BlockSpec((1, D), lambda i: (0, 0))
    return pl.pallas_call(
        body, grid=(S // bs,),
        in_specs=[row, one, row],
        out_specs=[row, row, one, pl.BlockSpec((1, 1), lambda i: (0, 0))],
        out_shape=[jax.ShapeDtypeStruct((S, D), F32), jax.ShapeDtypeStruct((S, D), CDT),
                   jax.ShapeDtypeStruct((1, D), F32), jax.ShapeDtypeStruct((1, 1), F32)],
        name=name, compiler_params=_cparams(("arbitrary",)),
    )(x, g.reshape(1, D), target)


_NN = (((1,), (0,)), ((), ()))
_NT = (((1,), (1,)), ((), ()))
_TN = (((0,), (0,)), ((), ()))


def _mm(a, b, *, M, N, K, a_spec, b_spec, o_spec, dims, tm, tn, tk, name, out_shapes, extras=(), epi=None):
    nk = K // tk
    n_ex, n_out = len(extras), len(out_shapes)
    in_place = epi is None
    if in_place:
        assert n_out == 1 and n_ex <= 1 and out_shapes[0].dtype == F32
        epi = lambda acc, *r: (acc + r[0] if r else acc,)

    def body(*refs):
        a_ref, b_ref = refs[0], refs[1]
        ex = refs[2:2 + n_ex]
        outs = refs[2 + n_ex:2 + n_ex + n_out]
        part = lax.dot_general(a_ref[...], b_ref[...], dims, preferred_element_type=F32)

        def finish(acc):
            for o, r in zip(outs, epi(acc, *[e[...] for e in ex])):
                o[...] = r.astype(o.dtype)

        if nk == 1:
            finish(part)
        elif in_place:
            k = pl.program_id(2)

            @pl.when(k == 0)
            def _():
                finish(part)

            @pl.when(k > 0)
            def _():
                outs[0][...] += part
        else:
            acc_ref = refs[-1]
            k = pl.program_id(2)

            @pl.when(k == 0)
            def _():
                acc_ref[...] = part

            @pl.when(k > 0)
            def _():
                acc_ref[...] += part

            @pl.when(k == nk - 1)
            def _():
                finish(acc_ref[...])

    ex_spec = pl.BlockSpec((tm, tn), lambda i, j, k: (i, j))
    return pl.pallas_call(
        body, grid=(M // tm, N // tn, nk),
        in_specs=[a_spec, b_spec] + [ex_spec] * n_ex,
        out_specs=[o_spec] * n_out,
        out_shape=out_shapes,
        scratch_shapes=[pltpu.VMEM((tm, tn), F32)] if nk > 1 and not in_place else [],
        name=name, compiler_params=_cparams(("parallel", "parallel", "arbitrary")),
    )(a, b, *extras)


def _mm_tiles(K):
    return (2048, 512, 2048) if K <= 2048 else (1024, 1024, 2048)


def _mm_nn(a, b, name, out_dtypes, extras=(), epi=None, b_slots=False, b_cols=None):
    M, K = a.shape
    tm, tn, tk = _mm_tiles(K)
    if b_slots:
        ns, _, Ns = b.shape
        N = ns * Ns
        tn = _tile(Ns, tn)
        npb = Ns // tn
        tk_ = _tile(K, tk)
        b_spec = pl.BlockSpec((None, tk_, tn), lambda i, j, k: (j // npb, k, j % npb))
    else:
        first, N = b_cols if b_cols is not None else (0, b.shape[1])
        tn = _tile(N, tn)
        assert first % tn == 0
        tk_ = _tile(K, tk)
        b_spec = pl.BlockSpec((tk_, tn), lambda i, j, k: (k, first // tn + j))
    tm = _tile(M, tm)
    return _mm(a, b, M=M, N=N, K=K, a_spec=pl.BlockSpec((tm, tk_), lambda i, j, k: (i, k)), b_spec=b_spec,
               o_spec=pl.BlockSpec((tm, tn), lambda i, j, k: (i, j)), dims=_NN, tm=tm, tn=tn, tk=tk_, name=name,
               out_shapes=[jax.ShapeDtypeStruct((M, N), d) for d in out_dtypes], extras=extras, epi=epi)


def _mm_nt(a, b, name, out_dtypes, extras=(), epi=None, b_slots=False):
    M, K = a.shape
    tm, tn, tk = _mm_tiles(K)
    tm = _tile(M, tm)
    if b_slots:
        ns, N, Ks = b.shape
        tk_ = _tile(Ks, tk)
        kpb = Ks // tk_
        tn = _tile(N, tn)
        b_spec = pl.BlockSpec((None, tn, tk_), lambda i, j, k: (k // kpb, j, k % kpb))
    else:
        N = b.shape[0]
        tk_ = _tile(K, tk)
        tn = _tile(N, tn)
        b_spec = pl.BlockSpec((tn, tk_), lambda i, j, k: (j, k))
    return _mm(a, b, M=M, N=N, K=K, a_spec=pl.BlockSpec((tm, tk_), lambda i, j, k: (i, k)), b_spec=b_spec,
               o_spec=pl.BlockSpec((tm, tn), lambda i, j, k: (i, j)), dims=_NT, tm=tm, tn=tn, tk=tk_, name=name,
               out_shapes=[jax.ShapeDtypeStruct((M, N), d) for d in out_dtypes], extras=extras, epi=epi)


def _mm_tn(a, b, name, out_dtype, out_slots=0, tm=2048, tn=1024, tk=2048):
    K, M = a.shape
    N = b.shape[1]
    tm, tk_ = _tile(M, tm), _tile(K, tk)
    if out_slots:
        Ns = N // out_slots
        tn = _tile(Ns, tn)
        npb = Ns // tn
        o_spec = pl.BlockSpec((None, tm, tn), lambda i, j, k: (j // npb, i, j % npb))
        out_shape = jax.ShapeDtypeStruct((out_slots, M, Ns), out_dtype)
    else:
        tn = _tile(N, tn)
        o_spec = pl.BlockSpec((tm, tn), lambda i, j, k: (i, j))
        out_shape = jax.ShapeDtypeStruct((M, N), out_dtype)
    return _mm(a, b, M=M, N=N, K=K, a_spec=pl.BlockSpec((tk_, tm), lambda i, j, k: (k, i)),
               b_spec=pl.BlockSpec((tk_, tn), lambda i, j, k: (k, j)), o_spec=o_spec, dims=_TN,
               tm=tm, tn=tn, tk=tk_, name=name, out_shapes=[out_shape],
               epi=None if out_dtype == F32 else (lambda acc: (acc,)))[0]


GATE_BLOCK = 512


def _split3(v):
    hi = v.astype(jnp.bfloat16)
    r1 = v - hi.astype(F32)
    mid = r1.astype(jnp.bfloat16)
    lo = (r1 - mid.astype(F32)).astype(jnp.bfloat16)
    return hi, mid, lo


def _exact_dot(v, tri):
    return functools.reduce(jnp.add, [jnp.dot(t, tri, preferred_element_type=F32) for t in _split3(v)])


def _gates_fwd(f_t, b, name):
    H, S = f_t.shape
    nb = _tile(S, GATE_BLOCK)
    inv_scale = HEAD_DIM ** 0.5

    def body(f_ref, b_ref, c_ref):
        upper = (lax.broadcasted_iota(jnp.int32, (nb, nb), 0)
                 <= lax.broadcasted_iota(jnp.int32, (nb, nb), 1)).astype(jnp.bfloat16)
        carry = jnp.zeros((H, 1), F32)
        for i in range(S // nb):
            z = f_ref[:, i * nb:(i + 1) * nb] + b_ref[...]
            logf = jnp.minimum(z, 0.0) - jnp.log1p(jnp.exp(-jnp.abs(z)))
            cs = _exact_dot(logf, upper) + carry
            for j, t in enumerate(_split3(cs * inv_scale)):
                c_ref[j, :, i * nb:(i + 1) * nb] = t
            carry = cs[:, nb - 1:nb]

    return pl.pallas_call(body, out_shape=jax.ShapeDtypeStruct((3, H, S), jnp.bfloat16), name=name,
                          compiler_params=_cparams())(f_t, b.reshape(H, 1))


def _gates_bwd(f_t, b, dcq, dck, name):
    H, S = f_t.shape
    nb = _tile(S, GATE_BLOCK)

    def body(f_ref, b_ref, dcq_ref, dck_ref, df_ref, dfc_ref, db_ref):
        lower = (lax.broadcasted_iota(jnp.int32, (nb, nb), 0)
                 >= lax.broadcasted_iota(jnp.int32, (nb, nb), 1)).astype(jnp.bfloat16)
        carry = jnp.zeros((H, 1), F32)
        db = jnp.zeros((H, 1), F32)
        for i in reversed(range(S // nb)):
            sl = slice(i * nb, (i + 1) * nb)
            dc = dcq_ref[:, sl] - dck_ref[:, sl]
            dlogf = _exact_dot(dc, lower) + carry
            carry = dlogf[:, 0:1]
            z = f_ref[:, sl] + b_ref[...]
            df = dlogf / (1.0 + jnp.exp(z))
            df_ref[:, sl] = df
            dfc_ref[:, sl] = df.astype(dfc_ref.dtype)
            db = db + jnp.sum(df, axis=1, keepdims=True)
        db_ref[...] = db

    return pl.pallas_call(
        body, out_shape=[jax.ShapeDtypeStruct((H, S), F32), jax.ShapeDtypeStruct((H, S), CDT),
                         jax.ShapeDtypeStruct((H, 1), F32)],
        name=name, compiler_params=_cparams())(f_t, b.reshape(H, 1), dcq, dck)


FOX_BLOCK = 512


def _fox_bias_operands(csplit):
    parts = csplit.transpose(2, 1, 0)
    S, H, _ = parts.shape
    ones = jnp.ones_like(parts)
    zeros = jnp.zeros((S, H, HEAD_DIM - 6), parts.dtype)
    qc = jnp.concatenate([parts, ones, zeros], axis=-1).reshape(S, H * HEAD_DIM)
    kc = jnp.concatenate([ones, -parts, zeros], axis=-1).reshape(S, H * HEAD_DIM)
    return qc, kc


def _fox_logits2(q_ref, qc_ref, k_ref, kc_ref, diag):
    q, k = q_ref[...], k_ref[...]
    qa = jnp.concatenate([q, qc_ref[...].astype(q.dtype)], axis=1)
    ka = jnp.concatenate([k, kc_ref[...].astype(k.dtype)], axis=1)
    s = lax.dot_general(qa, ka, _NT, preferred_element_type=F32) * (HEAD_DIM ** -0.5 * LOG2E)
    if diag:
        row = lax.broadcasted_iota(jnp.int32, s.shape, 0)
        col = lax.broadcasted_iota(jnp.int32, s.shape, 1)
        s = jnp.where(col <= row, s, NEG_INF)
    return s


def _fox_fwd(proj, qc, kc, H, name):
    S = proj.shape[0]
    E = HEAD_DIM
    blk = _tile(S, FOX_BLOCK)
    nq = S // blk

    def body(q_ref, qc_ref, k_ref, kc_ref, v_ref, o_ref, lse_ref, m_s, l_s, acc_s):
        qi, kj = pl.program_id(1), pl.program_id(2)

        @pl.when(kj == 0)
        def _():
            m_s[...] = jnp.full(m_s.shape, NEG_INF, F32)
            l_s[...] = jnp.zeros(l_s.shape, F32)
            acc_s[...] = jnp.zeros(acc_s.shape, F32)

        def step(diag):
            s = _fox_logits2(q_ref, qc_ref, k_ref, kc_ref, diag)
            m_prev = m_s[...]
            m_new = jnp.maximum(m_prev, jnp.max(s, axis=-1, keepdims=True))
            alpha = jnp.exp2(m_prev - m_new)
            p = jnp.exp2(s - m_new)
            l_s[...] = alpha * l_s[...] + jnp.sum(p, axis=-1, keepdims=True)
            acc_s[...] = alpha * acc_s[...] + jnp.dot(p.astype(CDT), v_ref[...], preferred_element_type=F32)
            m_s[...] = m_new

        pl.when(kj < qi)(lambda: step(False))
        pl.when(kj == qi)(lambda: step(True))

        @pl.when(kj == nq - 1)
        def _():
            o_ref[...] = acc_s[...] / l_s[...]
            lse_ref[...] = jnp.broadcast_to(m_s[...] + jnp.log2(l_s[...]), lse_ref.shape)

    qspec = lambda off: pl.BlockSpec((blk, E), lambda h, i, j: (i, off + h))
    kspec = lambda off: pl.BlockSpec((blk, E), lambda h, i, j: (jnp.minimum(j, i), off + h))
    return pl.pallas_call(
        body, grid=(H, nq, nq),
        in_specs=[qspec(0), qspec(0), kspec(H), kspec(0), kspec(2 * H)],
        out_specs=[qspec(0)] * 2,
        out_shape=[jax.ShapeDtypeStruct((S, H * E), F32)] * 2,
        scratch_shapes=[pltpu.VMEM((blk, 1), F32), pltpu.VMEM((blk, 1), F32), pltpu.VMEM((blk, E), F32)],
        name=name, compiler_params=_cparams(("parallel", "parallel", "arbitrary")),
    )(proj, qc, proj, kc, proj)


def _fox_bwd(proj, qc, kc, lse, o, do, H, name):
    S = proj.shape[0]
    E = HEAD_DIM
    blk = _tile(S, FOX_BLOCK)
    nq = S // blk
    scale = E ** -0.5

    def body(q_ref, qc_ref, k_ref, kc_ref, v_ref, lse_ref, o_ref, do_ref,
             dq_ref, dcq_ref, dk_ref, dv_ref, dck_ref, dq_s, dcq_s, dk_s, dv_s, dck_s):
        kj, qi = pl.program_id(1), pl.program_id(2)

        @pl.when(qi == 0)
        def _():
            dk_s[...] = jnp.zeros(dk_s.shape, F32)
            dv_s[...] = jnp.zeros(dv_s.shape, F32)
            dck_s[...] = jnp.zeros(dck_s.shape, F32)

        def step(diag):
            do = do_ref[...]
            doc = do.astype(CDT)
            delta = jnp.sum(do * o_ref[...], axis=-1, keepdims=True)
            p = jnp.exp2(_fox_logits2(q_ref, qc_ref, k_ref, kc_ref, diag) - lse_ref[:, 0:1])
            dp = lax.dot_general(doc, v_ref[...], _NT, preferred_element_type=F32)
            ds = p * (dp - delta)
            dss = ds * scale
            dck_s[...] += jnp.sum(ds, axis=0, keepdims=True)
            dv_s[...] += jnp.dot(p.T.astype(CDT), doc, preferred_element_type=F32)
            dk_s[...] += jnp.dot(dss.T.astype(CDT), q_ref[...], preferred_element_type=F32)
            dq_part = jnp.dot(dss.astype(CDT), k_ref[...], preferred_element_type=F32)
            dc_part = jnp.sum(ds, axis=-1, keepdims=True)
            rows = pl.ds(pl.multiple_of(qi * blk, blk), blk)

            @pl.when(kj == 0)
            def _():
                dq_s[rows, :] = dq_part
                dcq_s[rows, :] = dc_part

            @pl.when(kj > 0)
            def _():
                dq_s[rows, :] += dq_part
                dcq_s[rows, :] += dc_part

        pl.when(qi > kj)(lambda: step(False))
        pl.when(qi == kj)(lambda: step(True))

        @pl.when(qi == nq - 1)
        def _():
            dk_ref[...] = dk_s[...].astype(dk_ref.dtype)
            dv_ref[...] = dv_s[...].astype(dv_ref.dtype)
            dck_ref[...] = dck_s[...].reshape(dck_ref.shape)

        @pl.when((qi == nq - 1) & (kj == nq - 1))
        def _():
            dq_ref[...] = dq_s[...].astype(dq_ref.dtype)
            dcq_ref[...] = jnp.broadcast_to(dcq_s[...], dcq_ref.shape)

    qspec = lambda off: pl.BlockSpec((blk, E), lambda h, j, i: (jnp.maximum(i, j), off + h))
    kspec = lambda off: pl.BlockSpec((blk, E), lambda h, j, i: (j, off + h))
    head = pl.BlockSpec((S, E), lambda h, j, i: (0, h))
    return pl.pallas_call(
        body, grid=(H, nq, nq),
        in_specs=[qspec(0), qspec(0), kspec(H), kspec(0), kspec(2 * H), qspec(0), qspec(0), qspec(0)],
        out_specs=[head, head, kspec(0), kspec(0), pl.BlockSpec((1, 1, blk), lambda h, j, i: (h, 0, j))],
        out_shape=[jax.ShapeDtypeStruct((S, H * E), CDT), jax.ShapeDtypeStruct((S, H * E), F32),
                   jax.ShapeDtypeStruct((S, H * E), CDT), jax.ShapeDtypeStruct((S, H * E), CDT),
                   jax.ShapeDtypeStruct((H, 1, S), F32)],
        scratch_shapes=[pltpu.VMEM((S, E), F32), pltpu.VMEM((S, 1), F32), pltpu.VMEM((blk, E), F32),
                        pltpu.VMEM((blk, E), F32), pltpu.VMEM((1, blk), F32)],
        name=name, compiler_params=_cparams(("parallel", "arbitrary", "arbitrary")),
    )(proj, qc, proj, kc, proj, lse, o, do)


DIL_SLAB = 16 * DIL_BLOCK
DIL_UNROLL = 4


def _rel_bucket(dist):
    max_exact = REL_BUCKETS // 2
    d = jnp.maximum(dist.astype(F32), 1.0)
    large = max_exact + (jnp.log(d / max_exact) / jnp.log(jnp.float32(REL_MAX_DISTANCE / max_exact))
                         * (REL_BUCKETS - max_exact)).astype(jnp.int32)
    large = jnp.minimum(large, REL_BUCKETS - 1)
    return jnp.where(dist < max_exact, dist, large)


def _bucket_table():
    i = jnp.arange(DIL_BLOCK)[:, None]
    j = jnp.arange(2 * DIL_BLOCK)[None, :]
    rel = DIL_BLOCK + i - j
    tabs = [_rel_bucket(jnp.clip(rel, 0, w // d) * d) for w, d in DIL_PATTERNS]
    return jnp.stack(tabs).astype(jnp.int32)


def _bias_table(rel_bias, buckets, name):
    P = buckets.shape[0]
    H = rel_bias.shape[1]

    def body(rb_ref, bk_ref, out_ref):
        h = pl.program_id(1)
        bk = bk_ref[0]
        val = jnp.zeros(bk.shape, F32)
        for b in range(REL_BUCKETS):
            val = jnp.where(bk == b, rb_ref[b, h], val)
        out_ref[0, 0] = val

    return pl.pallas_call(
        body, grid=(P, H),
        in_specs=[pl.BlockSpec(memory_space=pltpu.SMEM),
                  pl.BlockSpec((1, DIL_BLOCK, 2 * DIL_BLOCK), lambda p, h: (p, 0, 0))],
        out_specs=pl.BlockSpec((1, 1, DIL_BLOCK, 2 * DIL_BLOCK), lambda p, h: (p, h, 0, 0)),
        out_shape=jax.ShapeDtypeStruct((P, H, DIL_BLOCK, 2 * DIL_BLOCK), F32),
        name=name, compiler_params=_cparams(("parallel", "parallel")),
    )(rel_bias, buckets)


def _bias_table_bwd(dbias, buckets, name):
    P, H = dbias.shape[:2]

    def body(db_ref, bk_ref, out_ref):
        lane = lax.broadcasted_iota(jnp.int32, (1, REL_BUCKETS), 1)
        acc = jnp.zeros((1, REL_BUCKETS), F32)
        bk = bk_ref[...]
        db = db_ref[:, 0]
        for b in range(REL_BUCKETS):
            tot = jnp.sum(jnp.where(bk == b, db, 0.0))
            acc = jnp.where(lane == b, tot, acc)
        out_ref[0] = acc

    return pl.pallas_call(
        body, grid=(H,),
        in_specs=[pl.BlockSpec((P, 1, DIL_BLOCK, 2 * DIL_BLOCK), lambda h: (0, h, 0, 0)),
                  pl.BlockSpec((P, DIL_BLOCK, 2 * DIL_BLOCK), lambda h: (0, 0, 0))],
        out_specs=pl.BlockSpec((1, 1, REL_BUCKETS), lambda h: (h, 0, 0)),
        out_shape=jax.ShapeDtypeStruct((H, 1, REL_BUCKETS), F32),
        name=name, compiler_params=_cparams(("parallel",)),
    )(dbias, buckets)


def _bdot(a, b, contract_b):
    return lax.dot_general(a, b, (((2,), (contract_b,)), ((0,), (0,))), preferred_element_type=F32)


def _dil_units(first, d):
    units = []
    for t in range(DIL_UNROLL):
        u = first + t
        sg = u // d
        units.append((sg, sg * (DIL_BLOCK * d) + u % d))
    return units


def _dil_rows(ref, starts, d, dtype=None):
    t = jnp.stack([ref[pl.ds(s, DIL_BLOCK, stride=d), :] for s in starts])
    return t if dtype is None else t.astype(dtype)


def _dil_keys(ref, units, d):
    B, SL = DIL_BLOCK, DIL_SLAB
    return jnp.stack([jnp.concatenate([ref[pl.ds(SL + b - B * d, B, stride=d), :], ref[pl.ds(SL + b, B, stride=d), :]],
                                      axis=0) for _, b in units]).astype(CDT)


def _dil_logits(q, keys, bias_pc, first, d, has_before):
    T, B = q.shape[0], DIL_BLOCK
    ii = lax.broadcasted_iota(jnp.int32, (T, B, 2 * B), 1)
    jj = lax.broadcasted_iota(jnp.int32, (T, B, 2 * B), 2)
    sg = (first + lax.broadcasted_iota(jnp.int32, (T, B, 2 * B), 0)) // d
    mask = (jj >= ii) & (jj <= ii + B) & ((jj >= B) | (sg > 0) | has_before)
    return jnp.where(mask, _bdot(q, keys, 2) * HEAD_DIM ** -0.5 + bias_pc[None], NEG_INF)


def _dil_specs(H, n_slabs):
    E, SL = DIL_BLOCK, DIL_SLAB
    cur = lambda off: pl.BlockSpec((SL, E), lambda h, g: (g, off + h))
    prev = lambda off: pl.BlockSpec((SL, E), lambda h, g: (jnp.maximum(g - 1, 0), off + h))
    nxt = lambda off: pl.BlockSpec((SL, E), lambda h, g: (jnp.minimum(g + 1, n_slabs - 1), off + h))
    bias = pl.BlockSpec((len(DIL_PATTERNS), 1, E, 2 * E), lambda h, g: (0, h, 0, 0))
    return cur, prev, nxt, bias


def _dil_fwd(proj, bias, H, name):
    S = proj.shape[0]
    E = B = DIL_BLOCK
    SL = DIL_SLAB
    P = len(DIL_PATTERNS)
    assert S % SL == 0
    n_slabs = S // SL

    def body(q_ref, kc_ref, kp_ref, vc_ref, vp_ref, b_ref, y_ref, lse_ref, kj, vj, o_s, l_s):
        g = pl.program_id(1)
        kj[0:SL, :] = kp_ref[...]
        kj[SL:2 * SL, :] = kc_ref[...]
        vj[0:SL, :] = vp_ref[...]
        vj[SL:2 * SL, :] = vc_ref[...]
        for p, (_, d) in enumerate(DIL_PATTERNS):
            def batch(it, carry, p=p, d=d):
                first = it * DIL_UNROLL
                units = _dil_units(first, d)
                q = _dil_rows(q_ref, [b for _, b in units], d, CDT)
                s = _dil_logits(q, _dil_keys(kj, units, d), b_ref[p, 0], first, d, g > 0)
                m = jnp.max(s, axis=-1, keepdims=True)
                e = jnp.exp(s - m)
                ssum = jnp.sum(e, axis=-1, keepdims=True)
                o = _bdot(e.astype(CDT), _dil_keys(vj, units, d), 1) / ssum
                lse = jnp.broadcast_to(m + jnp.log(ssum), o.shape)
                for t, (_, b) in enumerate(units):
                    o_s[p, pl.ds(b, B, stride=d), :] = o[t]
                    l_s[p, pl.ds(b, B, stride=d), :] = lse[t]
                return carry

            lax.fori_loop(0, SL // B // DIL_UNROLL, batch, 0)
        ls = [l_s[p] for p in range(P)]
        m = functools.reduce(jnp.maximum, ls)
        w = [jnp.exp(l - m) for l in ls]
        tot = functools.reduce(jnp.add, w)
        y_ref[...] = functools.reduce(jnp.add, [(w[p] / tot) * o_s[p] for p in range(P)])
        lse_ref[...] = m + jnp.log(tot)

    cur, prev, _, bspec = _dil_specs(H, n_slabs)
    return pl.pallas_call(
        body, grid=(H, n_slabs),
        in_specs=[cur(0), cur(H), prev(H), cur(2 * H), prev(2 * H), bspec],
        out_specs=[cur(0), cur(0)],
        out_shape=[jax.ShapeDtypeStruct((S, H * E), F32)] * 2,
        scratch_shapes=[pltpu.VMEM((2 * SL, E), F32), pltpu.VMEM((2 * SL, E), F32),
                        pltpu.VMEM((P, SL, E), F32), pltpu.VMEM((P, SL, E), F32)],
        name=name, compiler_params=_cparams(("parallel", "parallel")),
    )(proj, proj, proj, proj, proj, bias)


def _dil_bwd(proj, bias, y, dy, lse, H, name):
    S = proj.shape[0]
    E = B = DIL_BLOCK
    SL = DIL_SLAB
    P = len(DIL_PATTERNS)
    assert S % SL == 0
    n_slabs = S // SL
    scale = E ** -0.5

    def body(q_ref, kc_ref, kp_ref, vc_ref, vp_ref, b_ref, y_ref, dy_ref, lse_ref, qn_ref, yn_ref, dyn_ref, lsen_ref,
             dq_ref, dk_ref, dv_ref, db_ref, kj, vj, dq_s, dk_s, dv_s, dl_s, dln_s):
        g = pl.program_id(1)
        kj[0:SL, :] = kp_ref[...]
        kj[SL:2 * SL, :] = kc_ref[...]
        vj[0:SL, :] = vp_ref[...]
        vj[SL:2 * SL, :] = vc_ref[...]
        dq_s[...] = jnp.zeros(dq_s.shape, F32)
        dk_s[...] = jnp.zeros(dk_s.shape, F32)
        dv_s[...] = jnp.zeros(dv_s.shape, F32)
        dl_s[...] = jnp.broadcast_to(jnp.sum(dy_ref[...] * y_ref[...], axis=-1, keepdims=True), (SL, E))
        dln_s[...] = jnp.broadcast_to(jnp.sum(dyn_ref[...] * yn_ref[...], axis=-1, keepdims=True), (SL, E))

        @pl.when(g == 0)
        def _():
            db_ref[...] = jnp.zeros(db_ref.shape, F32)

        tr = lambda t: jnp.swapaxes(t, 1, 2).astype(CDT)
        for p, (_, d) in enumerate(DIL_PATTERNS):
            def batch(it, carry, p=p, d=d):
                first = it * DIL_UNROLL
                units = _dil_units(first, d)
                starts = [b for _, b in units]
                q = _dil_rows(q_ref, starts, d, CDT)
                dyc = _dil_rows(dy_ref, starts, d, CDT)
                keys, vals = _dil_keys(kj, units, d), _dil_keys(vj, units, d)
                s = _dil_logits(q, keys, b_ref[p, 0], first, d, g > 0)
                e = jnp.exp(s - _dil_rows(lse_ref, starts, d)[:, :, 0:1])
                ds = e * (_bdot(dyc, vals, 2) - _dil_rows(dl_s, starts, d)[:, :, 0:1])
                dss = ds * scale
                dq = _bdot(dss.astype(CDT), keys, 1)
                dk = _bdot(tr(dss), q, 1)
                dv = _bdot(tr(e), dyc, 1)
                for t, (sg, b) in enumerate(units):
                    rows = pl.ds(b, B, stride=d)
                    dq_s[rows, :] += dq[t]
                    dk_s[rows, :] += dk[t, B:]
                    dv_s[rows, :] += dv[t, B:]

                    @pl.when(sg > 0)
                    def _(t=t, b=b):
                        before = pl.ds(b - B * d, B, stride=d)
                        dk_s[before, :] += dk[t, :B]
                        dv_s[before, :] += dv[t, :B]

                db_ref[p, 0] += jnp.sum(ds, axis=0)
                return carry

            lax.fori_loop(0, SL // B // DIL_UNROLL, batch, 0)

            n_after = min(d, DIL_UNROLL)

            def after(it, carry, p=p, d=d, n_after=n_after):
                starts = [it * n_after + t for t in range(n_after)]
                kstarts = [SL - B * d + s for s in starts]
                q = _dil_rows(qn_ref, starts, d, CDT)
                dyc = _dil_rows(dyn_ref, starts, d, CDT)
                k, v = _dil_rows(kc_ref, kstarts, d, CDT), _dil_rows(vc_ref, kstarts, d, CDT)
                ii = lax.broadcasted_iota(jnp.int32, (n_after, B, B), 1)
                jj = lax.broadcasted_iota(jnp.int32, (n_after, B, B), 2)
                s = jnp.where((jj >= ii) & (g < n_slabs - 1), _bdot(q, k, 2) * scale + b_ref[p, 0][:, :B][None], NEG_INF)
                e = jnp.exp(s - _dil_rows(lsen_ref, starts, d)[:, :, 0:1])
                ds = e * (_bdot(dyc, v, 2) - _dil_rows(dln_s, starts, d)[:, :, 0:1])
                dk = _bdot(tr(ds * scale), q, 1)
                dv = _bdot(tr(e), dyc, 1)
                for t, ks in enumerate(kstarts):
                    dk_s[pl.ds(ks, B, stride=d), :] += dk[t]
                    dv_s[pl.ds(ks, B, stride=d), :] += dv[t]
                return carry

            lax.fori_loop(0, d // n_after, after, 0)

        dq_ref[...] = dq_s[...].astype(dq_ref.dtype)
        dk_ref[...] = dk_s[...].astype(dk_ref.dtype)
        dv_ref[...] = dv_s[...].astype(dv_ref.dtype)

    cur, prev, nxt, bspec = _dil_specs(H, n_slabs)
    slab = pltpu.VMEM((SL, E), F32)
    return pl.pallas_call(
        body, grid=(H, n_slabs),
        in_specs=[cur(0), cur(H), prev(H), cur(2 * H), prev(2 * H), bspec, cur(0), cur(0), cur(0),
                  nxt(0), nxt(0), nxt(0), nxt(0)],
        out_specs=[cur(0), cur(0), cur(0), bspec],
        out_shape=[jax.ShapeDtypeStruct((S, H * E), CDT)] * 3 + [jax.ShapeDtypeStruct((P, H, B, 2 * B), F32)],
        scratch_shapes=[pltpu.VMEM((2 * SL, E), F32), pltpu.VMEM((2 * SL, E), F32), slab, slab, slab, slab, slab],
        name=name, compiler_params=_cparams(("parallel", "arbitrary")),
    )(proj, proj, proj, proj, proj, bias, y, dy, lse, proj, y, dy, lse)


def _adamw(w, g, m, v, name, br=128):
    R, C = w.shape
    br = br if R % br == 0 else R

    def body(w_ref, g_ref, m_ref, v_ref, g_out, d_ref, nm_ref, nv_ref):
        g_ = g_ref[...]
        g_out[...] = g_
        m_ = ADAM_B1 * m_ref[...] + (1.0 - ADAM_B1) * g_
        v_ = ADAM_B2 * v_ref[...] + (1.0 - ADAM_B2) * jnp.square(g_)
        m_hat = m_ / (1.0 - ADAM_B1 ** ADAM_STEP)
        v_hat = v_ / (1.0 - ADAM_B2 ** ADAM_STEP)
        d_ref[...] = -ADAM_LR * (m_hat / (jnp.sqrt(v_hat) + ADAM_EPS) + ADAM_WD * w_ref[...])
        nm_ref[...] = m_
        nv_ref[...] = v_

    blk = pl.BlockSpec((br, C), lambda i: (i, 0))
    return pl.pallas_call(
        body, grid=(R // br,), in_specs=[blk] * 4, out_specs=[blk] * 4,
        out_shape=[jax.ShapeDtypeStruct((R, C), F32)] * 4,
        name=name, compiler_params=_cparams(("parallel",)),
    )(w, g, m, v)


_HBM = pl.BlockSpec(memory_space=pltpu.HBM)
_SEM = pl.BlockSpec(memory_space=pltpu.SEMAPHORE)
_ANY = pl.BlockSpec(memory_space=pl.ANY)
_VMEM = pl.BlockSpec(memory_space=pltpu.VMEM)
_TOKEN = jax.ShapeDtypeStruct((8, 128), F32)


def _split_params():
    return pltpu.CompilerParams(has_side_effects=pltpu.SideEffectType.DATAFLOW_SIDE_EFFECTING)


def _place():
    x, y, c = lax.axis_index("x"), lax.axis_index("y"), lax.axis_index("c")
    chips = [(1 - x, y), (x, 1 - y), (1 - x, 1 - y)]
    return x, y, c, chips


def _tie(v, tokens, name):
    flat = v.reshape(1, -1)

    def body(v_ref, *rest):
        rest[-1][...] = v_ref[...]

    return pl.pallas_call(body, in_specs=[_VMEM] + [_ANY] * len(tokens), out_specs=_VMEM,
                          out_shape=jax.ShapeDtypeStruct(flat.shape, flat.dtype), name=name,
                          compiler_params=_cparams())(flat, *tokens).reshape(v.shape)


def _row_block(R, pref=256):
    return _tile(R, pref) if R % 128 == 0 else R


def _slot():
    return 2 * lax.axis_index("x") + lax.axis_index("y")


def _cast_into_slot(w, name):
    R, C = w.shape
    br = _row_block(R)

    def body(w_ref, out_ref):
        out_ref[...] = w_ref[...].astype(out_ref.dtype)

    return pl.pallas_call(
        body, grid=(R // br,),
        in_specs=[pl.BlockSpec((br, C), lambda i: (i, 0))],
        out_specs=pl.BlockSpec((None, br, C), lambda i: (_slot(), i, 0)),
        out_shape=jax.ShapeDtypeStruct((N_CHIPS, R, C), CDT),
        name=name, compiler_params=_cparams(("parallel",)),
    )(w)


def _gather_copies(src_ref, dst_ref, send_sems, recv_sems, incoming):
    Rh = src_ref.shape[1] // 2
    x, y, c, chips = _place()
    slot = 2 * x + y

    def half(ref, s, hf):
        return ref.at[s, pl.ds(hf * Rh, Rh), :]

    copies = []
    for j, (cx, cy) in enumerate(chips):
        for e in range(2):
            copies.append(pltpu.make_async_remote_copy(
                src_ref=half(src_ref, slot, c), dst_ref=half(dst_ref, 2 * cx + cy, e) if incoming else half(dst_ref, slot, c),
                send_sem=send_sems.at[2 * j + e], recv_sem=recv_sems.at[2 * j + (e if incoming else c)],
                device_id=(cx, cy, e), device_id_type=MESH))
    return copies


def _gather_start(buf, after, name):
    n_after = len(after)

    def body(*refs):
        buf_ref = refs[0]
        send_sems, recv_sems, out_ref, token = refs[1 + n_after:]
        for cp in _gather_copies(buf_ref, out_ref, send_sems, recv_sems, incoming=False):
            cp.start()
        token[...] = jnp.zeros(token.shape, token.dtype)

    return pl.pallas_call(
        body, in_specs=[_HBM] + [_ANY] * n_after, out_specs=(_SEM, _SEM, _HBM, _VMEM),
        out_shape=(pltpu.SemaphoreType.DMA((6,)), pltpu.SemaphoreType.DMA((6,)), pltpu.HBM(buf.shape, buf.dtype), _TOKEN),
        input_output_aliases={0: 2}, name=name, compiler_params=_split_params(),
    )(pltpu.with_memory_space_constraint(buf, pltpu.HBM), *after)


def _gather_wait(send_sems, recv_sems, buf, after, name):
    def body(buf_ref, send_sems, recv_sems, after_ref, out_ref):
        for cp in _gather_copies(buf_ref, out_ref, send_sems, recv_sems, incoming=False):
            cp.wait_send()
        for cp in _gather_copies(buf_ref, out_ref, send_sems, recv_sems, incoming=True):
            cp.wait_recv()

    return pl.pallas_call(
        body, in_specs=[_HBM, _SEM, _SEM, _ANY], out_specs=_HBM, out_shape=pltpu.HBM(buf.shape, buf.dtype),
        input_output_aliases={0: 0}, name=name, compiler_params=_split_params(),
    )(buf, send_sems, recv_sems, after)


def _scatter_copies(g_ref, land_ref, send_sems, recv_sems, incoming):
    Rh = g_ref.shape[1] // 2
    x, y, c, _ = _place()
    me = 4 * x + 2 * y + c
    copies = []
    for k in range(1, N_DEV):
        px, py, pc = (x + (k >> 2)) % 2, (y + ((k >> 1) & 1)) % 2, (c + (k & 1)) % 2
        copies.append(pltpu.make_async_remote_copy(
            src_ref=g_ref.at[2 * px + py, pl.ds(pc * Rh, Rh), :],
            dst_ref=land_ref.at[4 * px + 2 * py + pc if incoming else me],
            send_sem=send_sems.at[k - 1], recv_sem=recv_sems.at[k - 1], device_id=(px, py, pc), device_id_type=MESH))
    return copies


def _scatter_start(g, name):
    ns, R, C = g.shape

    def body(g_ref, land_ref, send_sems, recv_sems, g_thru, land_thru, token):
        for cp in _scatter_copies(g_ref, land_thru, send_sems, recv_sems, incoming=False):
            cp.start()
        token[...] = jnp.zeros(token.shape, token.dtype)

    land = lax.empty((N_DEV, R // 2, C), g.dtype)
    n = N_DEV - 1
    return pl.pallas_call(
        body, in_specs=[_HBM, _HBM], out_specs=(_SEM, _SEM, _HBM, _HBM, _VMEM),
        out_shape=(pltpu.SemaphoreType.DMA((n,)), pltpu.SemaphoreType.DMA((n,)), pltpu.HBM(g.shape, g.dtype),
                   pltpu.HBM(land.shape, land.dtype), _TOKEN),
        input_output_aliases={0: 2, 1: 3}, name=name, compiler_params=_split_params(),
    )(pltpu.with_memory_space_constraint(g, pltpu.HBM), pltpu.with_memory_space_constraint(land, pltpu.HBM))


def _scatter_wait(send_sems, recv_sems, g, land, after, name):
    def body(g_ref, land_ref, send_sems, recv_sems, after_ref, g_out, land_out):
        for cp in _scatter_copies(g_ref, land_out, send_sems, recv_sems, incoming=False):
            cp.wait_send()
        for cp in _scatter_copies(g_ref, land_out, send_sems, recv_sems, incoming=True):
            cp.wait_recv()

    return pl.pallas_call(
        body, in_specs=[_HBM, _HBM, _SEM, _SEM, _ANY], out_specs=(_HBM, _HBM),
        out_shape=(pltpu.HBM(g.shape, g.dtype), pltpu.HBM(land.shape, land.dtype)),
        input_output_aliases={0: 0, 1: 1}, name=name, compiler_params=_split_params(),
    )(g, land, send_sems, recv_sems, after)


def _device_sum(land, g, layer, n_layers, prev, name):
    nd, Rh, C = land.shape
    br = _row_block(Rh)
    nb = Rh // br
    core = lambda: lax.axis_index("c")
    me = lambda: 2 * _slot() + core()

    def body(*refs):
        own = refs[nd][...]
        acc = None
        for d in range(nd):
            t = jnp.where(me() == d, own, refs[d][...]).astype(F32)
            acc = t if acc is None else acc + t
        refs[-1][...] = acc

    def piece(d):
        return pl.BlockSpec((None, br, C), lambda i: (jnp.where(me() == d, (d + 1) % nd, d), i, 0))

    ins = [land] * nd + [g] + ([prev] if prev is not None else [])
    return pl.pallas_call(
        body, grid=(nb,),
        in_specs=[piece(d) for d in range(nd)]
        + [pl.BlockSpec((None, br, C), lambda i: (_slot(), core() * nb + i, 0))]
        + ([_ANY] if prev is not None else []),
        out_specs=pl.BlockSpec((None, br, C), lambda i: (layer, core() * nb + i, 0)),
        out_shape=jax.ShapeDtypeStruct((n_layers, 2 * Rh, C), F32),
        input_output_aliases={nd + 1: 0} if prev is not None else {},
        name=name, compiler_params=_cparams(("parallel",)),
    )(*ins)


def _join_halves(g, layer, name):
    _, R, C = g.shape
    Rh = R // 2

    def body(g_ref, out_ref, send_sem, recv_sem):
        x, y, c, _ = _place()
        mine = pl.ds(c * Rh, Rh)
        cp = pltpu.make_async_remote_copy(src_ref=g_ref.at[layer, mine, :], dst_ref=out_ref.at[layer, mine, :],
                                          send_sem=send_sem, recv_sem=recv_sem, device_id=(x, y, 1 - c),
                                          device_id_type=MESH)
        cp.start()
        other = out_ref.at[layer, pl.ds((1 - c) * Rh, Rh), :]
        pltpu.make_async_remote_copy(src_ref=other, dst_ref=other, send_sem=send_sem, recv_sem=recv_sem,
                                     device_id=(x, y, 1 - c), device_id_type=MESH).wait_recv()
        cp.wait_send()

    return pl.pallas_call(
        body, in_specs=[_HBM], out_specs=_HBM, out_shape=jax.ShapeDtypeStruct(g.shape, g.dtype),
        input_output_aliases={0: 0},
        scratch_shapes=[pltpu.SemaphoreType.DMA, pltpu.SemaphoreType.DMA],
        name=name, compiler_params=pltpu.CompilerParams(),
    )(g)


def _all_reduce_small(v, name):
    rows, cols = v.shape

    def body(v_ref, out_ref, buf, send_sems, recv_sems):
        x, y, c, _ = _place()
        me = 4 * x + 2 * y + c
        buf[me] = v_ref[...]
        peers = []
        for k in range(1, N_DEV):
            px, py, pc = (x + (k >> 2)) % 2, (y + ((k >> 1) & 1)) % 2, (c + (k & 1)) % 2
            peers.append((px, py, pc))
        sends = []
        for k, peer in enumerate(peers):
            cp = pltpu.make_async_remote_copy(src_ref=v_ref, dst_ref=buf.at[me], send_sem=send_sems.at[k],
                                              recv_sem=recv_sems.at[k], device_id=peer, device_id_type=MESH)
            cp.start()
            sends.append(cp)
        for k, (px, py, pc) in enumerate(peers):
            pltpu.make_async_remote_copy(src_ref=v_ref, dst_ref=buf.at[4 * px + 2 * py + pc], send_sem=send_sems.at[k],
                                         recv_sem=recv_sems.at[k], device_id=(px, py, pc),
                                         device_id_type=MESH).wait_recv()
        for cp in sends:
            cp.wait_send()
        acc = buf[0]
        for i in range(1, N_DEV):
            acc = acc + buf[i]
        out_ref[...] = acc

    vmem = pl.BlockSpec(memory_space=pltpu.VMEM)
    return pl.pallas_call(
        body, in_specs=[vmem], out_specs=vmem, out_shape=jax.ShapeDtypeStruct((rows, cols), F32),
        scratch_shapes=[pltpu.VMEM((N_DEV, rows, cols), F32), pltpu.SemaphoreType.DMA((N_DEV - 1,)),
                        pltpu.SemaphoreType.DMA((N_DEV - 1,))],
        name=name, compiler_params=pltpu.CompilerParams(),
    )(v)


def _reduce_scatter_finish(started, after, layer, n_layers, prev, tag):
    send_sems, recv_sems, g, land, _ = started
    g, land = _scatter_wait(send_sems, recv_sems, g, land, after, f"rs_wait_{tag}")
    f = _device_sum(land, g, layer, n_layers, prev, f"rs_sum_{tag}")
    return _join_halves(f, layer, f"rs_join_{tag}")


def _split_w_in(wg, Hf):
    ns, D, cols = wg.shape
    nat = wg.transpose(1, 0, 2).reshape(D, ns * cols)
    a = 3 * Hf * HEAD_DIM
    return jnp.concatenate([nat[:, :a], nat[:, a + Hf:]], axis=1), nat[:, a:a + Hf].T


def _join_dw_in(dw6, dwf_t, Hf):
    D = dw6.shape[0]
    a = 3 * Hf * HEAD_DIM
    nat = jnp.concatenate([dw6[:, :a], dwf_t.T.astype(dw6.dtype), dw6[:, a:]], axis=1)
    return nat.reshape(D, N_CHIPS, nat.shape[1] // N_CHIPS).transpose(1, 0, 2)


def _tied(v, tokens, name):
    return _tie(v, tokens, name) if tokens else v


def _layer_fwd(x, p, weight, bias, tokens, tag):
    Hf, Hd = p["forget_b"].shape[0], bias.shape[1]
    h1 = _rms_fwd(x, _tied(p["norm1_g"], tokens, f"tie_norm1_{tag}"), f"norm1_{tag}")
    w6, wf_t = _split_w_in(weight("w_in", h1), Hf)
    n_a = 3 * Hf * HEAD_DIM
    proj_a = _mm_nn(h1, w6, f"proj_a_{tag}", [CDT], epi=lambda acc: (acc,), b_cols=(0, n_a))[0]
    proj_b = _mm_nn(h1, w6, f"proj_b_{tag}", [F32], b_cols=(n_a, w6.shape[1] - n_a))[0]
    f_t = _mm_nt(wf_t, h1, f"fproj_{tag}", [F32])[0]
    qc, kc = _fox_bias_operands(_gates_fwd(f_t, p["forget_b"], f"gates_{tag}"))
    y_a, lse_a = _fox_fwd(proj_a, qc, kc, Hf, f"fox_{tag}")
    y_b, lse_b = _dil_fwd(proj_b, bias, Hd, f"dil_{tag}")
    mixed = jnp.concatenate([_rms_fwd(y_a, p["outnorm_a_g"], f"norm_a_{tag}"),
                             _rms_fwd(y_b, p["outnorm_b_g"], f"norm_b_{tag}")], axis=1)
    w_out = weight("w_out", mixed)
    w_out = w_out.reshape(-1, w_out.shape[2])
    x1 = _mm_nn(mixed, w_out, f"attn_out_{tag}", [F32], extras=[x])[0]
    h2 = _rms_fwd(x1, p["norm2_g"], f"norm2_{tag}")
    w_mi = weight("w_mlp_in", h2)
    u, act = _mm_nn(h2, w_mi, f"mlp_in_{tag}", [CDT, CDT], b_slots=True,
                    epi=lambda acc: (acc, jnp.square(jnp.maximum(acc, 0.0))))
    w_mo = weight("w_mlp_out", act)
    w_mo = w_mo.reshape(-1, w_mo.shape[2])
    x2 = _mm_nn(act, w_mo, f"mlp_out_{tag}", [F32], extras=[x1])[0]
    saved = dict(x=x, h1=h1, proj_a=proj_a, proj_b=proj_b, f_t=f_t, qc=qc, kc=kc, y_a=y_a, lse_a=lse_a, y_b=y_b,
                 lse_b=lse_b, mixed=mixed, x1=x1, h2=h2, u=u, act=act, w6=w6, wf_t=wf_t, w_out=w_out, w_mi=w_mi,
                 w_mo=w_mo)
    return x2, saved


def _layer_bwd(dx2, dx2c, p, send, bias, sv, defer_w_out, tag):
    Hf, Hd = p["forget_b"].shape[0], bias.shape[1]
    E = HEAD_DIM
    rows = lambda g: g.reshape(N_CHIPS, -1, g.shape[1])
    du = _mm_nt(dx2c, sv["w_mo"], f"d_act_{tag}", [CDT], extras=[sv["u"]],
                epi=lambda acc, u: (acc * (2.0 * jnp.maximum(u.astype(F32), 0.0)),))[0]
    tokens = send("w_mlp_out", rows(_mm_tn(sv["act"], dx2c, f"dw_mlp_out_{tag}", CDT)))
    dh2 = _mm_nt(du, sv["w_mi"], f"d_h2_{tag}", [F32], b_slots=True)[0]
    tokens = tokens + send("w_mlp_in", _mm_tn(sv["h2"], du, f"dw_mlp_in_{tag}", CDT, out_slots=N_CHIPS))
    dx1, dx1c, g_norm2 = _rms_bwd(sv["x1"], _tied(p["norm2_g"], tokens, f"tie_norm2_{tag}"), dh2, dx2,
                                  f"d_norm2_{tag}")
    dmixed = _mm_nt(dx1c, sv["w_out"], f"d_mixed_{tag}", [F32])[0]
    send_w_out = lambda: send("w_out", rows(_mm_tn(sv["mixed"], dx1c, f"dw_out_{tag}", CDT)))
    tokens = [] if defer_w_out else send_w_out()
    dy_a, _, g_na = _rms_bwd(sv["y_a"], _tied(p["outnorm_a_g"], tokens, f"tie_norm_a_{tag}"), dmixed, None,
                             f"d_norm_a_{tag}", dh_col=0)
    dy_b, _, g_nb = _rms_bwd(sv["y_b"], p["outnorm_b_g"], dmixed, None, f"d_norm_b_{tag}", dh_col=1)
    dq_a, dcq, dk_a, dv_a, dck = _fox_bwd(sv["proj_a"], sv["qc"], sv["kc"], sv["lse_a"], sv["y_a"], dy_a, Hf,
                                          f"fox_bwd_{tag}")
    df, dfc, g_fb = _gates_bwd(sv["f_t"], p["forget_b"], dcq[:, ::E].T, dck.reshape(Hf, -1), f"d_gates_{tag}")
    dq_b, dk_b, dv_b, dbias = _dil_bwd(sv["proj_b"], bias, sv["y_b"], dy_b, sv["lse_b"], Hd, f"dil_bwd_{tag}")
    dproj = jnp.concatenate([dq_a, dk_a, dv_a, dq_b, dk_b, dv_b], axis=1)
    g_w6 = _mm_tn(sv["h1"], dproj, f"dw_in_{tag}", CDT)
    g_wf_t = _mm_nn(dfc, sv["h1"], f"dw_f_{tag}", [F32])[0]
    tokens = send("w_in", _join_dw_in(g_w6, g_wf_t, Hf))
    dh1_f = _mm_tn(dfc, _tied(sv["wf_t"], tokens, f"tie_wf_{tag}"), f"d_h1_f_{tag}", F32)
    dh1 = _mm_nt(dproj, sv["w6"], f"d_h1_{tag}", [F32], extras=[dh1_f])[0]
    dx, dxc, g_norm1 = _rms_bwd(sv["x"], p["norm1_g"], dh1, dx1, f"d_norm1_{tag}")
    grads = dict(norm1_g=g_norm1[0], norm2_g=g_norm2[0], outnorm_a_g=g_na[0], outnorm_b_g=g_nb[0],
                 forget_b=g_fb[:, 0], dbias=dbias)
    return dx, dxc, grads, (send_w_out if defer_w_out else None)


_LAYER_SMALL = ("norm1_g", "forget_b", "outnorm_a_g", "outnorm_b_g", "norm2_g")


def _local_step(x, target, small, weight, send, tokens):
    depth = small["norm1_g"].shape[0]
    buckets = _bucket_table()
    bias = _bias_table(small["rel_bias"], buckets, "bias_table")
    layers, saved = [], []
    for l in range(depth):
        p = {k: small[k][l] for k in _LAYER_SMALL}
        layers.append(p)
        x, sv = _layer_fwd(x, p, functools.partial(weight, l), bias, tokens if l == 0 else [], f"l{l}")
        saved.append(sv)
    dx, dxc, g_final, loss = _loss_bwd(x, small["final_norm_g"], target, "loss")
    layer_grads = [None] * depth
    for l in reversed(range(depth)):
        dx, dxc, layer_grads[l], last = _layer_bwd(dx, dxc, layers[l], functools.partial(send, l), bias, saved[l],
                                                   l == 0, f"l{l}")
    last()
    dbias = functools.reduce(jnp.add, [g["dbias"] for g in layer_grads])
    g_rel = _bias_table_bwd(dbias, buckets, "d_bias_table")[:, 0, :].T
    small_grads = dict(final_norm_g=g_final[0], rel_bias=g_rel,
                       **{k: jnp.stack([g[k] for g in layer_grads]) for k in _LAYER_SMALL})
    return loss[0, 0], dx, small_grads


_BIG = ("w_in", "w_out", "w_mlp_in", "w_mlp_out")
_SMALL = ("norm1_g", "forget_b", "rel_bias", "outnorm_a_g", "outnorm_b_g", "norm2_g", "final_norm_g")
_ORDER = ("norm1_g", "w_in", "forget_b", "rel_bias", "outnorm_a_g", "outnorm_b_g", "w_out", "norm2_g", "w_mlp_in",
          "w_mlp_out", "final_norm_g")


def _pack_small(d):
    flat = jnp.concatenate([d[k].reshape(-1) for k in _SMALL])
    rows = -(-flat.shape[0] // (8 * SMALL_COLS)) * 8
    return jnp.pad(flat, (0, rows * SMALL_COLS - flat.shape[0])).reshape(rows, SMALL_COLS)


def _unpack_small(packed, like):
    flat, out, at = packed.reshape(-1), {}, 0
    for k in _SMALL:
        n = like[k].size
        out[k] = flat[at:at + n].reshape(like[k].shape)
        at += n
    return out


def kernel(x, norm1_g, w_in, forget_b, rel_bias, outnorm_a_g, outnorm_b_g, w_out, norm2_g, w_mlp_in, w_mlp_out, final_norm_g, loss_target, m_norm1_g, m_w_in, m_forget_b, m_rel_bias, m_outnorm_a_g, m_outnorm_b_g, m_w_out, m_norm2_g, m_w_mlp_in, m_w_mlp_out, m_final_norm_g, v_norm1_g, v_w_in, v_forget_b, v_rel_bias, v_outnorm_a_g, v_outnorm_b_g, v_w_out, v_norm2_g, v_w_mlp_in, v_w_mlp_out, v_final_norm_g):
    w = dict(norm1_g=norm1_g, w_in=w_in, forget_b=forget_b, rel_bias=rel_bias, outnorm_a_g=outnorm_a_g,
             outnorm_b_g=outnorm_b_g, w_out=w_out, norm2_g=norm2_g, w_mlp_in=w_mlp_in, w_mlp_out=w_mlp_out,
             final_norm_g=final_norm_g)
    m = dict(norm1_g=m_norm1_g, w_in=m_w_in, forget_b=m_forget_b, rel_bias=m_rel_bias, outnorm_a_g=m_outnorm_a_g,
             outnorm_b_g=m_outnorm_b_g, w_out=m_w_out, norm2_g=m_norm2_g, w_mlp_in=m_w_mlp_in,
             w_mlp_out=m_w_mlp_out, final_norm_g=m_final_norm_g)
    v = dict(norm1_g=v_norm1_g, w_in=v_w_in, forget_b=v_forget_b, rel_bias=v_rel_bias, outnorm_a_g=v_outnorm_a_g,
             outnorm_b_g=v_outnorm_b_g, w_out=v_w_out, norm2_g=v_norm2_g, w_mlp_in=v_w_mlp_in,
             w_mlp_out=v_w_mlp_out, final_norm_g=v_final_norm_g)
    depth = w_in.shape[0]
    small = {k: w[k] for k in _SMALL}

    gathers, tokens = {}, []
    for l in range(depth):
        for k in _BIG:
            buf = _cast_into_slot(w[k][l], f"cast_{k}_l{l}")
            send_sems, recv_sems, buf, token = _gather_start(buf, tokens, f"gather_start_{k}_l{l}")
            gathers[l, k], tokens = (send_sems, recv_sems, buf), [token]

    def weight(l, k, after):
        return _gather_wait(*gathers[l, k], after, f"gather_wait_{k}_l{l}")

    scatters = {}

    def send(l, k, g):
        scatters[l, k] = _scatter_start(g, f"rs_start_{k}_l{l}")
        return [scatters[l, k][4]]

    loss, grad_x, small_grads = _local_step(x[0], loss_target[0], small, weight, send, tokens)
    loss = lax.psum(loss, ("x", "y", "c"))

    grads, delta, new_m, new_v = {}, {}, {}, {}
    after, seen = grad_x, {k: 0 for k in _BIG}
    for (l, k), started in scatters.items():
        grads[k] = _reduce_scatter_finish(started, after, l, depth, grads.get(k), f"{k}_l{l}")
        seen[k] += 1
        if seen[k] == depth:
            shape = w[k].shape
            flat = lambda t: t.reshape(-1, shape[-1])
            g_, d_, m_, v_ = _adamw(flat(w[k]), flat(grads[k]), flat(m[k]), flat(v[k]), f"adamw_{k}")
            grads[k], delta[k], new_m[k], new_v[k] = (t.reshape(shape) for t in (g_, d_, m_, v_))
            after = d_
    grads.update(_unpack_small(_all_reduce_small(_pack_small(small_grads), "small_all_reduce"), small))
    _, d_, m_, v_ = _adamw(_pack_small(small), _pack_small({k: grads[k] for k in _SMALL}),
                           _pack_small({k: m[k] for k in _SMALL}), _pack_small({k: v[k] for k in _SMALL}), "adamw_small")
    delta.update(_unpack_small(d_, small))
    new_m.update(_unpack_small(m_, small))
    new_v.update(_unpack_small(v_, small))

    return (loss, grad_x[None], *[grads[k] for k in _ORDER], *[delta[k] for k in _ORDER],
            *[new_m[k] for k in _ORDER], *[new_v[k] for k in _ORDER])
```

```python
import functools

import jax
import jax.numpy as jnp
from jax import lax
from jax.experimental import pallas as pl
from jax.experimental.pallas import tpu as pltpu

F32 = jnp.float32
CDT = jnp.bfloat16
HEAD_DIM = 128
NORM_EPS = 1e-6
NEG_INF = -1e30
LOG2E = 1.4426950408889634
REL_BUCKETS = 32
REL_MAX_DISTANCE = 2048
DIL_PATTERNS = ((128, 1), (512, 4), (2048, 16))
DIL_BLOCK = 128
ADAM_LR, ADAM_B1, ADAM_B2, ADAM_EPS, ADAM_WD, ADAM_STEP = 0.001, 0.9, 0.999, 1e-08, 0.01, 10
N_CHIPS = 4
N_DEV = 8
VMEM_LIMIT_BYTES = 56 * 1024 * 1024
SMALL_COLS = 1024
MESH = pl.DeviceIdType.MESH


def _cparams(sem=None):
    return pltpu.CompilerParams(dimension_semantics=sem, vmem_limit_bytes=VMEM_LIMIT_BYTES)


def _tile(dim, pref):
    t = min(pref, dim)
    t -= t % 128
    while t >= 128:
        if dim % t == 0:
            return t
        t -= 128
    return dim


def _rowwise(fn, ins, out_dtypes, name, bs=256, consts=()):
    R, C = ins[0].shape
    bs = min(bs, R)
    n_in, n_c = len(ins), len(consts)

    def body(*refs):
        vals = [r[...] for r in refs[:n_in + n_c]]
        res = fn(*vals)
        for o, r in zip(refs[n_in + n_c:], res):
            o[...] = r.astype(o.dtype)

    row = pl.BlockSpec((bs, C), lambda i: (i, 0))
    return pl.pallas_call(
        body, grid=(R // bs,),
        in_specs=[row] * n_in + [pl.BlockSpec((1, c.shape[-1]), lambda i: (0, 0)) for c in consts],
        out_specs=[row] * len(out_dtypes),
        out_shape=[jax.ShapeDtypeStruct((R, C), d) for d in out_dtypes],
        name=name, compiler_params=_cparams(("parallel",)),
    )(*ins, *[c.reshape(1, -1) for c in consts])


def _rms_fwd(x, g, name):
    def fn(xf, gg):
        r = lax.rsqrt(jnp.mean(xf * xf, axis=-1, keepdims=True) + NORM_EPS)
        return ((xf * r) * gg,)
    return _rowwise(fn, [x], [CDT], name, consts=[g])[0]


def _rms_bwd(x, g, dh, dres, name, bs=256, dh_col=0):
    S, D = x.shape
    bs = min(bs, S)
    has_res = dres is not None

    def body(*refs):
        x_ref, g_ref, dh_ref = refs[:3]
        dx_ref, dxc_ref, dg_ref = refs[-3:]
        xf = x_ref[...]
        r = lax.rsqrt(jnp.mean(xf * xf, axis=-1, keepdims=True) + NORM_EPS)
        xhat = xf * r
        dh_ = dh_ref[...].astype(F32)
        dxhat = dh_ * g_ref[...]
        dx = r * (dxhat - xhat * jnp.mean(dxhat * xhat, axis=-1, keepdims=True))
        if has_res:
            dx = dx + refs[3][...]
        dx_ref[...] = dx
        dxc_ref[...] = dx.astype(dxc_ref.dtype)
        part = jnp.sum(dh_ * xhat, axis=0, keepdims=True)

        @pl.when(pl.program_id(0) == 0)
        def _():
            dg_ref[...] = part

        @pl.when(pl.program_id(0) > 0)
        def _():
            dg_ref[...] += part

    row = pl.BlockSpec((bs, D), lambda i: (i, 0))
    one = pl.BlockSpec((1, D), lambda i: (0, 0))
    ins = [x, g.reshape(1, D), dh] + ([dres] if has_res else [])
    return pl.pallas_call(
        body, grid=(S // bs,),
        in_specs=[row, one, pl.BlockSpec((bs, D), lambda i: (i, dh_col))] + ([row] if has_res else []),
        out_specs=[row, row, one],
        out_shape=[jax.ShapeDtypeStruct((S, D), F32), jax.ShapeDtypeStruct((S, D), CDT),
                   jax.ShapeDtypeStruct((1, D), F32)],
        name=name, compiler_params=_cparams(("arbitrary",)),
    )(*ins)


def _loss_bwd(x, g, target, name, bs=256):
    S, D = x.shape
    bs = min(bs, S)

    def body(x_ref, g_ref, t_ref, dx_ref, dxc_ref, dg_ref, loss_ref):
        xf = x_ref[...]
        r = lax.rsqrt(jnp.mean(xf * xf, axis=-1, keepdims=True) + NORM_EPS)
        xhat = xf * r
        err = xhat * g_ref[...] - t_ref[...]
        lpart = 0.5 * jnp.sum(jnp.mean(err * err, axis=-1, keepdims=True), axis=0, keepdims=True)
        dy = err / D
        dxhat = dy * g_ref[...]
        dx = r * (dxhat - xhat * jnp.mean(dxhat * xhat, axis=-1, keepdims=True))
        dx_ref[...] = dx
        dxc_ref[...] = dx.astype(dxc_ref.dtype)
        gpart = jnp.sum(dy * xhat, axis=0, keepdims=True)

        @pl.when(pl.program_id(0) == 0)
        def _():
            dg_ref[...] = gpart
            loss_ref[...] = lpart

        @pl.when(pl.program_id(0) > 0)
        def _():
            dg_ref[...] += gpart
            loss_ref[...] += lpart

    row = pl.BlockSpec((bs, D), lambda i: (i, 0))
    one = pl.BlockSpec((1, D), lambda i: (0, 0))
    return pl.pallas_call(
        body, grid=(S // bs,),
        in_specs=[row, one, row],
        out_specs=[row, row, one, pl.BlockSpec((1, 1), lambda i: (0, 0))],
        out_shape=[jax.ShapeDtypeStruct((S, D), F32), jax.ShapeDtypeStruct((S, D), CDT),
                   jax.ShapeDtypeStruct((1, D), F32), jax.ShapeDtypeStruct((1, 1), F32)],
        name=name, compiler_params=_cparams(("arbitrary",)),
    )(x, g.reshape(1, D), target)


_NN = (((1,), (0,)), ((), ()))
_NT = (((1,), (1,)), ((), ()))
_TN = (((0,), (0,)), ((), ()))


def _mm(a, b, *, M, N, K, a_spec, b_spec, o_spec, dims, tm, tn, tk, name, out_shapes, extras=(), epi=None):
    nk = K // tk
    n_ex, n_out = len(extras), len(out_shapes)
    in_place = epi is None
    if in_place:
        assert n_out == 1 and n_ex <= 1 and out_shapes[0].dtype == F32
        epi = lambda acc, *r: (acc + r[0] if r else acc,)

    def body(*refs):
        a_ref, b_ref = refs[0], refs[1]
        ex = refs[2:2 + n_ex]
        outs = refs[2 + n_ex:2 + n_ex + n_out]
        part = lax.dot_general(a_ref[...], b_ref[...], dims, preferred_element_type=F32)

        def finish(acc):
            for o, r in zip(outs, epi(acc, *[e[...] for e in ex])):
                o[...] = r.astype(o.dtype)

        if nk == 1:
            finish(part)
        elif in_place:
            k = pl.program_id(2)

            @pl.when(k == 0)
            def _():
                finish(part)

            @pl.when(k > 0)
            def _():
                outs[0][...] += part
        else:
            acc_ref = refs[-1]
            k = pl.program_id(2)

            @pl.when(k == 0)
            def _():
                acc_ref[...] = part

            @pl.when(k > 0)
            def _():
                acc_ref[...] += part

            @pl.when(k == nk - 1)
            def _():
                finish(acc_ref[...])

    ex_spec = pl.BlockSpec((tm, tn), lambda i, j, k: (i, j))
    return pl.pallas_call(
        body, grid=(M // tm, N // tn, nk),
        in_specs=[a_spec, b_spec] + [ex_spec] * n_ex,
        out_specs=[o_spec] * n_out,
        out_shape=out_shapes,
        scratch_shapes=[pltpu.VMEM((tm, tn), F32)] if nk > 1 and not in_place else [],
        name=name, compiler_params=_cparams(("parallel", "parallel", "arbitrary")),
    )(a, b, *extras)


def _mm_tiles(K):
    return (2048, 512, 2048) if K <= 2048 else (1024, 1024, 2048)


def _mm_nn(a, b, name, out_dtypes, extras=(), epi=None, b_slots=False, b_cols=None):
    M, K = a.shape
    tm, tn, tk = _mm_tiles(K)
    if b_slots:
        ns, _, Ns = b.shape
        N = ns * Ns
        tn = _tile(Ns, tn)
        npb = Ns // tn
        tk_ = _tile(K, tk)
        b_spec = pl.BlockSpec((None, tk_, tn), lambda i, j, k: (j // npb, k, j % npb))
    else:
        first, N = b_cols if b_cols is not None else (0, b.shape[1])
        tn = _tile(N, tn)
        assert first % tn == 0
        tk_ = _tile(K, tk)
        b_spec = pl.BlockSpec((tk_, tn), lambda i, j, k: (k, first // tn + j))
    tm = _tile(M, tm)
    return _mm(a, b, M=M, N=N, K=K, a_spec=pl.BlockSpec((tm, tk_), lambda i, j, k: (i, k)), b_spec=b_spec,
               o_spec=pl.BlockSpec((tm, tn), lambda i, j, k: (i, j)), dims=_NN, tm=tm, tn=tn, tk=tk_, name=name,
               out_shapes=[jax.ShapeDtypeStruct((M, N), d) for d in out_dtypes], extras=extras, epi=epi)


def _mm_nt(a, b, name, out_dtypes, extras=(), epi=None, b_slots=False):
    M, K = a.shape
    tm, tn, tk = _mm_tiles(K)
    tm = _tile(M, tm)
    if b_slots:
        ns, N, Ks = b.shape
        tk_ = _tile(Ks, tk)
        kpb = Ks // tk_
        tn = _tile(N, tn)
        b_spec = pl.BlockSpec((None, tn, tk_), lambda i, j, k: (k // kpb, j, k % kpb))
    else:
        N = b.shape[0]
        tk_ = _tile(K, tk)
        tn = _tile(N, tn)
        b_spec = pl.BlockSpec((tn, tk_), lambda i, j, k: (j, k))
    return _mm(a, b, M=M, N=N, K=K, a_spec=pl.BlockSpec((tm, tk_), lambda i, j, k: (i, k)), b_spec=b_spec,
               o_spec=pl.BlockSpec((tm, tn), lambda i, j, k: (i, j)), dims=_NT, tm=tm, tn=tn, tk=tk_, name=name,
               out_shapes=[jax.ShapeDtypeStruct((M, N), d) for d in out_dtypes], extras=extras, epi=epi)


def _mm_tn(a, b, name, out_dtype, out_slots=0, tm=2048, tn=1024, tk=2048):
    K, M = a.shape
    N = b.shape[1]
    tm, tk_ = _tile(M, tm), _tile(K, tk)
    if out_slots:
        Ns = N // out_slots
        tn = _tile(Ns, tn)
        npb = Ns // tn
        o_spec = pl.BlockSpec((None, tm, tn), lambda i, j, k: (j // npb, i, j % npb))
        out_shape = jax.ShapeDtypeStruct((out_slots, M, Ns), out_dtype)
    else:
        tn = _tile(N, tn)
        o_spec = pl.BlockSpec((tm, tn), lambda i, j, k: (i, j))
        out_shape = jax.ShapeDtypeStruct((M, N), out_dtype)
    return _mm(a, b, M=M, N=N, K=K, a_spec=pl.BlockSpec((tk_, tm), lambda i, j, k: (k, i)),
               b_spec=pl.BlockSpec((tk_, tn), lambda i, j, k: (k, j)), o_spec=o_spec, dims=_TN,
               tm=tm, tn=tn, tk=tk_, name=name, out_shapes=[out_shape],
               epi=None if out_dtype == F32 else (lambda acc: (acc,)))[0]


GATE_BLOCK = 512


def _split3(v):
    hi = v.astype(jnp.bfloat16)
    r1 = v - hi.astype(F32)
    mid = r1.astype(jnp.bfloat16)
    lo = (r1 - mid.astype(F32)).astype(jnp.bfloat16)
    return hi, mid, lo


def _exact_dot(v, tri):
    return functools.reduce(jnp.add, [jnp.dot(t, tri, preferred_element_type=F32) for t in _split3(v)])


def _gates_fwd(f_t, b, name):
    H, S = f_t.shape
    nb = _tile(S, GATE_BLOCK)
    inv_scale = HEAD_DIM ** 0.5

    def body(f_ref, b_ref, c_ref):
        upper = (lax.broadcasted_iota(jnp.int32, (nb, nb), 0)
                 <= lax.broadcasted_iota(jnp.int32, (nb, nb), 1)).astype(jnp.bfloat16)
        carry = jnp.zeros((H, 1), F32)
        for i in range(S // nb):
            z = f_ref[:, i * nb:(i + 1) * nb] + b_ref[...]
            logf = jnp.minimum(z, 0.0) - jnp.log1p(jnp.exp(-jnp.abs(z)))
            cs = _exact_dot(logf, upper) + carry
            for j, t in enumerate(_split3(cs * inv_scale)):
                c_ref[j, :, i * nb:(i + 1) * nb] = t
            carry = cs[:, nb - 1:nb]

    return pl.pallas_call(body, out_shape=jax.ShapeDtypeStruct((3, H, S), jnp.bfloat16), name=name,
                          compiler_params=_cparams())(f_t, b.reshape(H, 1))


def _gates_bwd(f_t, b, dcq, dck, name):
    H, S = f_t.shape
    nb = _tile(S, GATE_BLOCK)

    def body(f_ref, b_ref, dcq_ref, dck_ref, df_ref, dfc_ref, db_ref):
        lower = (lax.broadcasted_iota(jnp.int32, (nb, nb), 0)
                 >= lax.broadcasted_iota(jnp.int32, (nb, nb), 1)).astype(jnp.bfloat16)
        carry = jnp.zeros((H, 1), F32)
        db = jnp.zeros((H, 1), F32)
        for i in reversed(range(S // nb)):
            sl = slice(i * nb, (i + 1) * nb)
            dc = dcq_ref[:, sl] - dck_ref[:, sl]
            dlogf = _exact_dot(dc, lower) + carry
            carry = dlogf[:, 0:1]
            z = f_ref[:, sl] + b_ref[...]
            df = dlogf / (1.0 + jnp.exp(z))
            df_ref[:, sl] = df
            dfc_ref[:, sl] = df.astype(dfc_ref.dtype)
            db = db + jnp.sum(df, axis=1, keepdims=True)
        db_ref[...] = db

    return pl.pallas_call(
        body, out_shape=[jax.ShapeDtypeStruct((H, S), F32), jax.ShapeDtypeStruct((H, S), CDT),
                         jax.ShapeDtypeStruct((H, 1), F32)],
        name=name, compiler_params=_cparams())(f_t, b.reshape(H, 1), dcq, dck)


FOX_BLOCK = 1024


def _fox_bias_operands(csplit):
    parts = csplit.transpose(2, 1, 0)
    S, H, _ = parts.shape
    ones = jnp.ones_like(parts)
    zeros = jnp.zeros((S, H, HEAD_DIM - 6), parts.dtype)
    qc = jnp.concatenate([parts, ones, zeros], axis=-1).reshape(S, H * HEAD_DIM)
    kc = jnp.concatenate([ones, -parts, zeros], axis=-1).reshape(S, H * HEAD_DIM)
    return qc, kc


def _fox_logits2(q_ref, qc_ref, k_ref, kc_ref, diag):
    q, k = q_ref[...], k_ref[...]
    qa = jnp.concatenate([q, qc_ref[...].astype(q.dtype)], axis=1)
    ka = jnp.concatenate([k, kc_ref[...].astype(k.dtype)], axis=1)
    s = lax.dot_general(qa, ka, _NT, preferred_element_type=F32) * (HEAD_DIM ** -0.5 * LOG2E)
    if diag:
        row = lax.broadcasted_iota(jnp.int32, s.shape, 0)
        col = lax.broadcasted_iota(jnp.int32, s.shape, 1)
        s = jnp.where(col <= row, s, NEG_INF)
    return s


def _fox_fwd(proj, qc, kc, H, name):
    S = proj.shape[0]
    E = HEAD_DIM
    blk = _tile(S, FOX_BLOCK)
    nq = S // blk

    def body(q_ref, qc_ref, k_ref, kc_ref, v_ref, o_ref, lse_ref, m_s, l_s, acc_s):
        qi, kj = pl.program_id(1), pl.program_id(2)

        @pl.when(kj == 0)
        def _():
            m_s[...] = jnp.full(m_s.shape, NEG_INF, F32)
            l_s[...] = jnp.zeros(l_s.shape, F32)
            acc_s[...] = jnp.zeros(acc_s.shape, F32)

        def step(diag):
            s = _fox_logits2(q_ref, qc_ref, k_ref, kc_ref, diag)
            m_prev = m_s[...]
            m_new = jnp.maximum(m_prev, jnp.max(s, axis=-1, keepdims=True))
            alpha = jnp.exp2(m_prev - m_new)
            p = jnp.exp2(s - m_new)
            l_s[...] = alpha * l_s[...] + jnp.sum(p, axis=-1, keepdims=True)
            acc_s[...] = alpha * acc_s[...] + jnp.dot(p.astype(CDT), v_ref[...], preferred_element_type=F32)
            m_s[...] = m_new

        pl.when(kj < qi)(lambda: step(False))
        pl.when(kj == qi)(lambda: step(True))

        @pl.when(kj == nq - 1)
        def _():
            o_ref[...] = acc_s[...] / l_s[...]
            lse_ref[...] = jnp.broadcast_to(m_s[...] + jnp.log2(l_s[...]), lse_ref.shape)

    qspec = lambda off: pl.BlockSpec((blk, E), lambda h, i, j: (i, off + h))
    kspec = lambda off: pl.BlockSpec((blk, E), lambda h, i, j: (jnp.minimum(j, i), off + h))
    return pl.pallas_call(
        body, grid=(H, nq, nq),
        in_specs=[qspec(0), qspec(0), kspec(H), kspec(0), kspec(2 * H)],
        out_specs=[qspec(0)] * 2,
        out_shape=[jax.ShapeDtypeStruct((S, H * E), F32)] * 2,
        scratch_shapes=[pltpu.VMEM((blk, 1), F32), pltpu.VMEM((blk, 1), F32), pltpu.VMEM((blk, E), F32)],
        name=name, compiler_params=_cparams(("parallel", "parallel", "arbitrary")),
    )(proj, qc, proj, kc, proj)


def _fox_bwd(proj, qc, kc, lse, o, do, H, name):
    S = proj.shape[0]
    E = HEAD_DIM
    blk = _tile(S, FOX_BLOCK)
    nq = S // blk
    scale = E ** -0.5

    def body(q_ref, qc_ref, k_ref, kc_ref, v_ref, lse_ref, o_ref, do_ref,
             dq_ref, dcq_ref, dk_ref, dv_ref, dck_ref, dq_s, dcq_s, dk_s, dv_s, dck_s):
        kj, qi = pl.program_id(1), pl.program_id(2)

        @pl.when(qi == 0)
        def _():
            dk_s[...] = jnp.zeros(dk_s.shape, F32)
            dv_s[...] = jnp.zeros(dv_s.shape, F32)
            dck_s[...] = jnp.zeros(dck_s.shape, F32)

        def step(diag):
            do = do_ref[...]
            doc = do.astype(CDT)
            delta = jnp.sum(do * o_ref[...], axis=-1, keepdims=True)
            p = jnp.exp2(_fox_logits2(q_ref, qc_ref, k_ref, kc_ref, diag) - lse_ref[:, 0:1])
            dp = lax.dot_general(doc, v_ref[...], _NT, preferred_element_type=F32)
            ds = p * (dp - delta)
            dss = ds * scale
            dck_s[...] += jnp.sum(ds, axis=0, keepdims=True)
            dv_s[...] += jnp.dot(p.T.astype(CDT), doc, preferred_element_type=F32)
            dk_s[...] += jnp.dot(dss.T.astype(CDT), q_ref[...], preferred_element_type=F32)
            dq_part = jnp.dot(dss.astype(CDT), k_ref[...], preferred_element_type=F32)
            dc_part = jnp.sum(ds, axis=-1, keepdims=True)
            rows = pl.ds(pl.multiple_of(qi * blk, blk), blk)

            @pl.when(kj == 0)
            def _():
                dq_s[rows, :] = dq_part
                dcq_s[rows, :] = dc_part

            @pl.when(kj > 0)
            def _():
                dq_s[rows, :] += dq_part
                dcq_s[rows, :] += dc_part

        pl.when(qi > kj)(lambda: step(False))
        pl.when(qi == kj)(lambda: step(True))

        @pl.when(qi == nq - 1)
        def _():
            dk_ref[...] = dk_s[...].astype(dk_ref.dtype)
            dv_ref[...] = dv_s[...].astype(dv_ref.dtype)
            dck_ref[...] = dck_s[...].reshape(dck_ref.shape)

        @pl.when((qi == nq - 1) & (kj == nq - 1))
        def _():
            dq_ref[...] = dq_s[...].astype(dq_ref.dtype)
            dcq_ref[...] = jnp.broadcast_to(dcq_s[...], dcq_ref.shape)

    qspec = lambda off: pl.BlockSpec((blk, E), lambda h, j, i: (jnp.maximum(i, j), off + h))
    kspec = lambda off: pl.BlockSpec((blk, E), lambda h, j, i: (j, off + h))
    head = pl.BlockSpec((S, E), lambda h, j, i: (0, h))
    return pl.pallas_call(
        body, grid=(H, nq, nq),
        in_specs=[qspec(0), qspec(0), kspec(H), kspec(0), kspec(2 * H), qspec(0), qspec(0), qspec(0)],
        out_specs=[head, head, kspec(0), kspec(0), pl.BlockSpec((1, 1, blk), lambda h, j, i: (h, 0, j))],
        out_shape=[jax.ShapeDtypeStruct((S, H * E), CDT), jax.ShapeDtypeStruct((S, H * E), F32),
                   jax.ShapeDtypeStruct((S, H * E), CDT), jax.ShapeDtypeStruct((S, H * E), CDT),
                   jax.ShapeDtypeStruct((H, 1, S), F32)],
        scratch_shapes=[pltpu.VMEM((S, E), F32), pltpu.VMEM((S, 1), F32), pltpu.VMEM((blk, E), F32),
                        pltpu.VMEM((blk, E), F32), pltpu.VMEM((1, blk), F32)],
        name=name, compiler_params=_cparams(("parallel", "arbitrary", "arbitrary")),
    )(proj, qc, proj, kc, proj, lse, o, do)


DIL_SLAB = 16 * DIL_BLOCK
DIL_UNROLL = 4


def _rel_bucket(dist):
    max_exact = REL_BUCKETS // 2
    d = jnp.maximum(dist.astype(F32), 1.0)
    large = max_exact + (jnp.log(d / max_exact) / jnp.log(jnp.float32(REL_MAX_DISTANCE / max_exact))
                         * (REL_BUCKETS - max_exact)).astype(jnp.int32)
    large = jnp.minimum(large, REL_BUCKETS - 1)
    return jnp.where(dist < max_exact, dist, large)


def _bucket_table():
    i = jnp.arange(DIL_BLOCK)[:, None]
    j = jnp.arange(2 * DIL_BLOCK)[None, :]
    rel = DIL_BLOCK + i - j
    tabs = [_rel_bucket(jnp.clip(rel, 0, w // d) * d) for w, d in DIL_PATTERNS]
    return jnp.stack(tabs).astype(jnp.int32)


def _bias_table(rel_bias, buckets, name):
    P = buckets.shape[0]
    H = rel_bias.shape[1]

    def body(rb_ref, bk_ref, out_ref):
        h = pl.program_id(1)
        bk = bk_ref[0]
        val = jnp.zeros(bk.shape, F32)
        for b in range(REL_BUCKETS):
            val = jnp.where(bk == b, rb_ref[b, h], val)
        out_ref[0, 0] = val

    return pl.pallas_call(
        body, grid=(P, H),
        in_specs=[pl.BlockSpec(memory_space=pltpu.SMEM),
                  pl.BlockSpec((1, DIL_BLOCK, 2 * DIL_BLOCK), lambda p, h: (p, 0, 0))],
        out_specs=pl.BlockSpec((1, 1, DIL_BLOCK, 2 * DIL_BLOCK), lambda p, h: (p, h, 0, 0)),
        out_shape=jax.ShapeDtypeStruct((P, H, DIL_BLOCK, 2 * DIL_BLOCK), F32),
        name=name, compiler_params=_cparams(("parallel", "parallel")),
    )(rel_bias, buckets)


def _bias_table_bwd(dbias, buckets, name):
    P, H = dbias.shape[:2]

    def body(db_ref, bk_ref, out_ref):
        lane = lax.broadcasted_iota(jnp.int32, (1, REL_BUCKETS), 1)
        acc = jnp.zeros((1, REL_BUCKETS), F32)
        bk = bk_ref[...]
        db = db_ref[:, 0]
        for b in range(REL_BUCKETS):
            tot = jnp.sum(jnp.where(bk == b, db, 0.0))
            acc = jnp.where(lane == b, tot, acc)
        out_ref[0] = acc

    return pl.pallas_call(
        body, grid=(H,),
        in_specs=[pl.BlockSpec((P, 1, DIL_BLOCK, 2 * DIL_BLOCK), lambda h: (0, h, 0, 0)),
                  pl.BlockSpec((P, DIL_BLOCK, 2 * DIL_BLOCK), lambda h: (0, 0, 0))],
        out_specs=pl.BlockSpec((1, 1, REL_BUCKETS), lambda h: (h, 0, 0)),
        out_shape=jax.ShapeDtypeStruct((H, 1, REL_BUCKETS), F32),
        name=name, compiler_params=_cparams(("parallel",)),
    )(dbias, buckets)


def _bdot(a, b, contract_b):
    return lax.dot_general(a, b, (((2,), (contract_b,)), ((0,), (0,))), preferred_element_type=F32)


def _dil_units(first, d):
    units = []
    for t in range(DIL_UNROLL):
        u = first + t
        sg = u // d
        units.append((sg, sg * (DIL_BLOCK * d) + u % d))
    return units


def _dil_rows(ref, starts, d, dtype=None):
    t = jnp.stack([ref[pl.ds(s, DIL_BLOCK, stride=d), :] for s in starts])
    return t if dtype is None else t.astype(dtype)


def _dil_keys(ref, units, d):
    B, SL = DIL_BLOCK, DIL_SLAB
    return jnp.stack([jnp.concatenate([ref[pl.ds(SL + b - B * d, B, stride=d), :], ref[pl.ds(SL + b, B, stride=d), :]],
                                      axis=0) for _, b in units]).astype(CDT)


def _dil_logits(q, keys, bias_pc, first, d, has_before):
    T, B = q.shape[0], DIL_BLOCK
    ii = lax.broadcasted_iota(jnp.int32, (T, B, 2 * B), 1)
    jj = lax.broadcasted_iota(jnp.int32, (T, B, 2 * B), 2)
    sg = (first + lax.broadcasted_iota(jnp.int32, (T, B, 2 * B), 0)) // d
    mask = (jj >= ii) & (jj <= ii + B) & ((jj >= B) | (sg > 0) | has_before)
    return jnp.where(mask, _bdot(q, keys, 2) * HEAD_DIM ** -0.5 + bias_pc[None], NEG_INF)


def _dil_specs(H, n_slabs):
    E, SL = DIL_BLOCK, DIL_SLAB
    cur = lambda off: pl.BlockSpec((SL, E), lambda h, g: (g, off + h))
    prev = lambda off: pl.BlockSpec((SL, E), lambda h, g: (jnp.maximum(g - 1, 0), off + h))
    nxt = lambda off: pl.BlockSpec((SL, E), lambda h, g: (jnp.minimum(g + 1, n_slabs - 1), off + h))
    bias = pl.BlockSpec((len(DIL_PATTERNS), 1, E, 2 * E), lambda h, g: (0, h, 0, 0))
    return cur, prev, nxt, bias


def _dil_fwd(proj, bias, H, name):
    S = proj.shape[0]
    E = B = DIL_BLOCK
    SL = DIL_SLAB
    P = len(DIL_PATTERNS)
    assert S % SL == 0
    n_slabs = S // SL

    def body(q_ref, kc_ref, kp_ref, vc_ref, vp_ref, b_ref, y_ref, lse_ref, kj, vj, o_s, l_s):
        g = pl.program_id(1)
        kj[0:SL, :] = kp_ref[...]
        kj[SL:2 * SL, :] = kc_ref[...]
        vj[0:SL, :] = vp_ref[...]
        vj[SL:2 * SL, :] = vc_ref[...]
        for p, (_, d) in enumerate(DIL_PATTERNS):
            def batch(it, carry, p=p, d=d):
                first = it * DIL_UNROLL
                units = _dil_units(first, d)
                q = _dil_rows(q_ref, [b for _, b in units], d, CDT)
                s = _dil_logits(q, _dil_keys(kj, units, d), b_ref[p, 0], first, d, g > 0)
                m = jnp.max(s, axis=-1, keepdims=True)
                e = jnp.exp(s - m)
                ssum = jnp.sum(e, axis=-1, keepdims=True)
                o = _bdot(e.astype(CDT), _dil_keys(vj, units, d), 1) / ssum
                lse = jnp.broadcast_to(m + jnp.log(ssum), o.shape)
                for t, (_, b) in enumerate(units):
                    o_s[p, pl.ds(b, B, stride=d), :] = o[t]
                    l_s[p, pl.ds(b, B, stride=d), :] = lse[t]
                return carry

            lax.fori_loop(0, SL // B // DIL_UNROLL, batch, 0)
        ls = [l_s[p] for p in range(P)]
        m = functools.reduce(jnp.maximum, ls)
        w = [jnp.exp(l - m) for l in ls]
        tot = functools.reduce(jnp.add, w)
        y_ref[...] = functools.reduce(jnp.add, [(w[p] / tot) * o_s[p] for p in range(P)])
        lse_ref[...] = m + jnp.log(tot)

    cur, prev, _, bspec = _dil_specs(H, n_slabs)
    return pl.pallas_call(
        body, grid=(H, n_slabs),
        in_specs=[cur(0), cur(H), prev(H), cur(2 * H), prev(2 * H), bspec],
        out_specs=[cur(0), cur(0)],
        out_shape=[jax.ShapeDtypeStruct((S, H * E), F32)] * 2,
        scratch_shapes=[pltpu.VMEM((2 * SL, E), F32), pltpu.VMEM((2 * SL, E), F32),
                        pltpu.VMEM((P, SL, E), F32), pltpu.VMEM((P, SL, E), F32)],
        name=name, compiler_params=_cparams(("parallel", "parallel")),
    )(proj, proj, proj, proj, proj, bias)


def _dil_bwd(proj, bias, y, dy, lse, H, name):
    S = proj.shape[0]
    E = B = DIL_BLOCK
    SL = DIL_SLAB
    P = len(DIL_PATTERNS)
    assert S % SL == 0
    n_slabs = S // SL
    scale = E ** -0.5

    def body(q_ref, kc_ref, kp_ref, vc_ref, vp_ref, b_ref, y_ref, dy_ref, lse_ref, qn_ref, yn_ref, dyn_ref, lsen_ref,
             dq_ref, dk_ref, dv_ref, db_ref, kj, vj, dq_s, dk_s, dv_s, dl_s, dln_s):
        g = pl.program_id(1)
        kj[0:SL, :] = kp_ref[...]
        kj[SL:2 * SL, :] = kc_ref[...]
        vj[0:SL, :] = vp_ref[...]
        vj[SL:2 * SL, :] = vc_ref[...]
        dq_s[...] = jnp.zeros(dq_s.shape, F32)
        dk_s[...] = jnp.zeros(dk_s.shape, F32)
        dv_s[...] = jnp.zeros(dv_s.shape, F32)
        dl_s[...] = jnp.broadcast_to(jnp.sum(dy_ref[...] * y_ref[...], axis=-1, keepdims=True), (SL, E))
        dln_s[...] = jnp.broadcast_to(jnp.sum(dyn_ref[...] * yn_ref[...], axis=-1, keepdims=True), (SL, E))

        @pl.when(g == 0)
        def _():
            db_ref[...] = jnp.zeros(db_ref.shape, F32)

        tr = lambda t: jnp.swapaxes(t, 1, 2).astype(CDT)
        for p, (_, d) in enumerate(DIL_PATTERNS):
            def batch(it, carry, p=p, d=d):
                first = it * DIL_UNROLL
                units = _dil_units(first, d)
                starts = [b for _, b in units]
                q = _dil_rows(q_ref, starts, d, CDT)
                dyc = _dil_rows(dy_ref, starts, d, CDT)
                keys, vals = _dil_keys(kj, units, d), _dil_keys(vj, units, d)
                s = _dil_logits(q, keys, b_ref[p, 0], first, d, g > 0)
                e = jnp.exp(s - _dil_rows(lse_ref, starts, d)[:, :, 0:1])
                ds = e * (_bdot(dyc, vals, 2) - _dil_rows(dl_s, starts, d)[:, :, 0:1])
                dss = ds * scale
                dq = _bdot(dss.astype(CDT), keys, 1)
                dk = _bdot(tr(dss), q, 1)
                dv = _bdot(tr(e), dyc, 1)
                for t, (sg, b) in enumerate(units):
                    rows = pl.ds(b, B, stride=d)
                    dq_s[rows, :] += dq[t]
                    dk_s[rows, :] += dk[t, B:]
                    dv_s[rows, :] += dv[t, B:]

                    @pl.when(sg > 0)
                    def _(t=t, b=b):
                        before = pl.ds(b - B * d, B, stride=d)
                        dk_s[before, :] += dk[t, :B]
                        dv_s[before, :] += dv[t, :B]

                db_ref[p, 0] += jnp.sum(ds, axis=0)
                return carry

            lax.fori_loop(0, SL // B // DIL_UNROLL, batch, 0)

            n_after = min(d, DIL_UNROLL)

            def after(it, carry, p=p, d=d, n_after=n_after):
                starts = [it * n_after + t for t in range(n_after)]
                kstarts = [SL - B * d + s for s in starts]
                q = _dil_rows(qn_ref, starts, d, CDT)
                dyc = _dil_rows(dyn_ref, starts, d, CDT)
                k, v = _dil_rows(kc_ref, kstarts, d, CDT), _dil_rows(vc_ref, kstarts, d, CDT)
                ii = lax.broadcasted_iota(jnp.int32, (n_after, B, B), 1)
                jj = lax.broadcasted_iota(jnp.int32, (n_after, B, B), 2)
                s = jnp.where((jj >= ii) & (g < n_slabs - 1), _bdot(q, k, 2) * scale + b_ref[p, 0][:, :B][None], NEG_INF)
                e = jnp.exp(s - _dil_rows(lsen_ref, starts, d)[:, :, 0:1])
                ds = e * (_bdot(dyc, v, 2) - _dil_rows(dln_s, starts, d)[:, :, 0:1])
                dk = _bdot(tr(ds * scale), q, 1)
                dv = _bdot(tr(e), dyc, 1)
                for t, ks in enumerate(kstarts):
                    dk_s[pl.ds(ks, B, stride=d), :] += dk[t]
                    dv_s[pl.ds(ks, B, stride=d), :] += dv[t]
                return carry

            lax.fori_loop(0, d // n_after, after, 0)

        dq_ref[...] = dq_s[...].astype(dq_ref.dtype)
        dk_ref[...] = dk_s[...].astype(dk_ref.dtype)
        dv_ref[...] = dv_s[...].astype(dv_ref.dtype)

    cur, prev, nxt, bspec = _dil_specs(H, n_slabs)
    slab = pltpu.VMEM((SL, E), F32)
    return pl.pallas_call(
        body, grid=(H, n_slabs),
        in_specs=[cur(0), cur(H), prev(H), cur(2 * H), prev(2 * H), bspec, cur(0), cur(0), cur(0),
                  nxt(0), nxt(0), nxt(0), nxt(0)],
        out_specs=[cur(0), cur(0), cur(0), bspec],
        out_shape=[jax.ShapeDtypeStruct((S, H * E), CDT)] * 3 + [jax.ShapeDtypeStruct((P, H, B, 2 * B), F32)],
        scratch_shapes=[pltpu.VMEM((2 * SL, E), F32), pltpu.VMEM((2 * SL, E), F32), slab, slab, slab, slab, slab],
        name=name, compiler_params=_cparams(("parallel", "arbitrary")),
    )(proj, proj, proj, proj, proj, bias, y, dy, lse, proj, y, dy, lse)


def _adamw(w, g, m, v, name, br=128):
    R, C = w.shape
    br = br if R % br == 0 else R

    def body(w_ref, g_ref, m_ref, v_ref, g_out, d_ref, nm_ref, nv_ref):
        g_ = g_ref[...]
        g_out[...] = g_
        m_ = ADAM_B1 * m_ref[...] + (1.0 - ADAM_B1) * g_
        v_ = ADAM_B2 * v_ref[...] + (1.0 - ADAM_B2) * jnp.square(g_)
        m_hat = m_ / (1.0 - ADAM_B1 ** ADAM_STEP)
        v_hat = v_ / (1.0 - ADAM_B2 ** ADAM_STEP)
        d_ref[...] = -ADAM_LR * (m_hat / (jnp.sqrt(v_hat) + ADAM_EPS) + ADAM_WD * w_ref[...])
        nm_ref[...] = m_
        nv_ref[...] = v_

    blk = pl.BlockSpec((br, C), lambda i: (i, 0))
    return pl.pallas_call(
        body, grid=(R // br,), in_specs=[blk] * 4, out_specs=[blk] * 4,
        out_shape=[jax.ShapeDtypeStruct((R, C), F32)] * 4,
        name=name, compiler_params=_cparams(("parallel",)),
    )(w, g, m, v)


_HBM = pl.BlockSpec(memory_space=pltpu.HBM)
_SEM = pl.BlockSpec(memory_space=pltpu.SEMAPHORE)
_ANY = pl.BlockSpec(memory_space=pl.ANY)
_VMEM = pl.BlockSpec(memory_space=pltpu.VMEM)
_TOKEN = jax.ShapeDtypeStruct((8, 128), F32)


def _split_params():
    return pltpu.CompilerParams(has_side_effects=pltpu.SideEffectType.DATAFLOW_SIDE_EFFECTING)


def _place():
    x, y, c = lax.axis_index("x"), lax.axis_index("y"), lax.axis_index("c")
    chips = [(1 - x, y), (x, 1 - y), (1 - x, 1 - y)]
    return x, y, c, chips


def _tie(v, tokens, name):
    flat = v.reshape(1, -1)

    def body(v_ref, *rest):
        rest[-1][...] = v_ref[...]

    return pl.pallas_call(body, in_specs=[_VMEM] + [_ANY] * len(tokens), out_specs=_VMEM,
                          out_shape=jax.ShapeDtypeStruct(flat.shape, flat.dtype), name=name,
                          compiler_params=_cparams())(flat, *tokens).reshape(v.shape)


def _row_block(R, pref=256):
    return _tile(R, pref) if R % 128 == 0 else R


def _slot():
    return 2 * lax.axis_index("x") + lax.axis_index("y")


def _cast_into_slot(w, layer, name):
    _, R, C = w.shape
    br = _row_block(R)

    def body(w_ref, out_ref):
        out_ref[...] = w_ref[...].astype(out_ref.dtype)

    return pl.pallas_call(
        body, grid=(R // br,),
        in_specs=[pl.BlockSpec((None, br, C), lambda i: (layer, i, 0))],
        out_specs=pl.BlockSpec((None, br, C), lambda i: (_slot(), i, 0)),
        out_shape=jax.ShapeDtypeStruct((N_CHIPS, R, C), CDT),
        name=name, compiler_params=_cparams(("parallel",)),
    )(w)


def _gather_copies(src_ref, dst_ref, send_sems, recv_sems, incoming):
    Rh = src_ref.shape[1] // 2
    x, y, c, chips = _place()
    slot = 2 * x + y

    def half(ref, s, hf):
        return ref.at[s, pl.ds(hf * Rh, Rh), :]

    copies = []
    for j, (cx, cy) in enumerate(chips):
        for e in range(2):
            copies.append(pltpu.make_async_remote_copy(
                src_ref=half(src_ref, slot, c), dst_ref=half(dst_ref, 2 * cx + cy, e) if incoming else half(dst_ref, slot, c),
                send_sem=send_sems.at[2 * j + e], recv_sem=recv_sems.at[2 * j + (e if incoming else c)],
                device_id=(cx, cy, e), device_id_type=MESH))
    return copies


def _gather_start(buf, after, name):
    n_after = len(after)

    def body(*refs):
        buf_ref = refs[0]
        send_sems, recv_sems, out_ref, token = refs[1 + n_after:]
        for cp in _gather_copies(buf_ref, out_ref, send_sems, recv_sems, incoming=False):
            cp.start()
        token[...] = jnp.zeros(token.shape, token.dtype)

    return pl.pallas_call(
        body, in_specs=[_HBM] + [_ANY] * n_after, out_specs=(_SEM, _SEM, _HBM, _VMEM),
        out_shape=(pltpu.SemaphoreType.DMA((6,)), pltpu.SemaphoreType.DMA((6,)), pltpu.HBM(buf.shape, buf.dtype), _TOKEN),
        input_output_aliases={0: 2}, name=name, compiler_params=_split_params(),
    )(pltpu.with_memory_space_constraint(buf, pltpu.HBM), *after)


def _gather_wait(send_sems, recv_sems, buf, after, name):
    def body(buf_ref, send_sems, recv_sems, after_ref, out_ref):
        for cp in _gather_copies(buf_ref, out_ref, send_sems, recv_sems, incoming=False):
            cp.wait_send()
        for cp in _gather_copies(buf_ref, out_ref, send_sems, recv_sems, incoming=True):
            cp.wait_recv()

    return pl.pallas_call(
        body, in_specs=[_HBM, _SEM, _SEM, _ANY], out_specs=_HBM, out_shape=pltpu.HBM(buf.shape, buf.dtype),
        input_output_aliases={0: 0}, name=name, compiler_params=_split_params(),
    )(buf, send_sems, recv_sems, after)


def _scatter_copies(g_ref, land_ref, send_sems, recv_sems, incoming):
    Rh = g_ref.shape[1] // 2
    x, y, c, _ = _place()
    me = 4 * x + 2 * y + c
    copies = []
    for k in range(1, N_DEV):
        px, py, pc = (x + (k >> 2)) % 2, (y + ((k >> 1) & 1)) % 2, (c + (k & 1)) % 2
        copies.append(pltpu.make_async_remote_copy(
            src_ref=g_ref.at[2 * px + py, pl.ds(pc * Rh, Rh), :],
            dst_ref=land_ref.at[4 * px + 2 * py + pc if incoming else me],
            send_sem=send_sems.at[k - 1], recv_sem=recv_sems.at[k - 1], device_id=(px, py, pc), device_id_type=MESH))
    return copies


def _scatter_start(g, name):
    ns, R, C = g.shape

    def body(g_ref, land_ref, send_sems, recv_sems, g_thru, land_thru, token):
        for cp in _scatter_copies(g_ref, land_thru, send_sems, recv_sems, incoming=False):
            cp.start()
        token[...] = jnp.zeros(token.shape, token.dtype)

    land = lax.empty((N_DEV, R // 2, C), g.dtype)
    n = N_DEV - 1
    return pl.pallas_call(
        body, in_specs=[_HBM, _HBM], out_specs=(_SEM, _SEM, _HBM, _HBM, _VMEM),
        out_shape=(pltpu.SemaphoreType.DMA((n,)), pltpu.SemaphoreType.DMA((n,)), pltpu.HBM(g.shape, g.dtype),
                   pltpu.HBM(land.shape, land.dtype), _TOKEN),
        input_output_aliases={0: 2, 1: 3}, name=name, compiler_params=_split_params(),
    )(pltpu.with_memory_space_constraint(g, pltpu.HBM), pltpu.with_memory_space_constraint(land, pltpu.HBM))


def _scatter_wait(send_sems, recv_sems, g, land, after, name):
    def body(g_ref, land_ref, send_sems, recv_sems, after_ref, g_out, land_out):
        for cp in _scatter_copies(g_ref, land_out, send_sems, recv_sems, incoming=False):
            cp.wait_send()
        for cp in _scatter_copies(g_ref, land_out, send_sems, recv_sems, incoming=True):
            cp.wait_recv()

    return pl.pallas_call(
        body, in_specs=[_HBM, _HBM, _SEM, _SEM, _ANY], out_specs=(_HBM, _HBM),
        out_shape=(pltpu.HBM(g.shape, g.dtype), pltpu.HBM(land.shape, land.dtype)),
        input_output_aliases={0: 0, 1: 1}, name=name, compiler_params=_split_params(),
    )(g, land, send_sems, recv_sems, after)


def _device_sum(land, g, layer, n_layers, prev, name):
    nd, Rh, C = land.shape
    br = _row_block(Rh)
    nb = Rh // br
    core = lambda: lax.axis_index("c")
    me = lambda: 2 * _slot() + core()

    def body(*refs):
        own = refs[nd][...]
        acc = None
        for d in range(nd):
            t = jnp.where(me() == d, own, refs[d][...]).astype(F32)
            acc = t if acc is None else acc + t
        refs[-1][...] = acc

    def piece(d):
        return pl.BlockSpec((None, br, C), lambda i: (jnp.where(me() == d, (d + 1) % nd, d), i, 0))

    ins = [land] * nd + [g] + ([prev] if prev is not None else [])
    return pl.pallas_call(
        body, grid=(nb,),
        in_specs=[piece(d) for d in range(nd)]
        + [pl.BlockSpec((None, br, C), lambda i: (_slot(), core() * nb + i, 0))]
        + ([_ANY] if prev is not None else []),
        out_specs=pl.BlockSpec((None, br, C), lambda i: (layer, core() * nb + i, 0)),
        out_shape=jax.ShapeDtypeStruct((n_layers, 2 * Rh, C), F32),
        input_output_aliases={nd + 1: 0} if prev is not None else {},
        name=name, compiler_params=_cparams(("parallel",)),
    )(*ins)


def _join_halves(g, layer, name):
    _, R, C = g.shape
    Rh = R // 2

    def body(g_ref, out_ref, send_sem, recv_sem):
        x, y, c, _ = _place()
        mine = pl.ds(c * Rh, Rh)
        cp = pltpu.make_async_remote_copy(src_ref=g_ref.at[layer, mine, :], dst_ref=out_ref.at[layer, mine, :],
                                          send_sem=send_sem, recv_sem=recv_sem, device_id=(x, y, 1 - c),
                                          device_id_type=MESH)
        cp.start()
        other = out_ref.at[layer, pl.ds((1 - c) * Rh, Rh), :]
        pltpu.make_async_remote_copy(src_ref=other, dst_ref=other, send_sem=send_sem, recv_sem=recv_sem,
                                     device_id=(x, y, 1 - c), device_id_type=MESH).wait_recv()
        cp.wait_send()

    return pl.pallas_call(
        body, in_specs=[_HBM], out_specs=_HBM, out_shape=jax.ShapeDtypeStruct(g.shape, g.dtype),
        input_output_aliases={0: 0},
        scratch_shapes=[pltpu.SemaphoreType.DMA, pltpu.SemaphoreType.DMA],
        name=name, compiler_params=pltpu.CompilerParams(),
    )(g)


def _all_reduce_small(v, name):
    rows, cols = v.shape

    def body(v_ref, out_ref, buf, send_sems, recv_sems):
        x, y, c, _ = _place()
        me = 4 * x + 2 * y + c
        buf[me] = v_ref[...]
        peers = []
        for k in range(1, N_DEV):
            px, py, pc = (x + (k >> 2)) % 2, (y + ((k >> 1) & 1)) % 2, (c + (k & 1)) % 2
            peers.append((px, py, pc))
        sends = []
        for k, peer in enumerate(peers):
            cp = pltpu.make_async_remote_copy(src_ref=v_ref, dst_ref=buf.at[me], send_sem=send_sems.at[k],
                                              recv_sem=recv_sems.at[k], device_id=peer, device_id_type=MESH)
            cp.start()
            sends.append(cp)
        for k, (px, py, pc) in enumerate(peers):
            pltpu.make_async_remote_copy(src_ref=v_ref, dst_ref=buf.at[4 * px + 2 * py + pc], send_sem=send_sems.at[k],
                                         recv_sem=recv_sems.at[k], device_id=(px, py, pc),
                                         device_id_type=MESH).wait_recv()
        for cp in sends:
            cp.wait_send()
        acc = buf[0]
        for i in range(1, N_DEV):
            acc = acc + buf[i]
        out_ref[...] = acc

    vmem = pl.BlockSpec(memory_space=pltpu.VMEM)
    return pl.pallas_call(
        body, in_specs=[vmem], out_specs=vmem, out_shape=jax.ShapeDtypeStruct((rows, cols), F32),
        scratch_shapes=[pltpu.VMEM((N_DEV, rows, cols), F32), pltpu.SemaphoreType.DMA((N_DEV - 1,)),
                        pltpu.SemaphoreType.DMA((N_DEV - 1,))],
        name=name, compiler_params=pltpu.CompilerParams(),
    )(v)


def _reduce_scatter_finish(started, after, layer, n_layers, prev, tag):
    send_sems, recv_sems, g, land, _ = started
    g, land = _scatter_wait(send_sems, recv_sems, g, land, after, f"rs_wait_{tag}")
    f = _device_sum(land, g, layer, n_layers, prev, f"rs_sum_{tag}")
    return _join_halves(f, layer, f"rs_join_{tag}")


def _split_w_in(wg, Hf):
    ns, D, cols = wg.shape
    nat = wg.transpose(1, 0, 2).reshape(D, ns * cols)
    a = 3 * Hf * HEAD_DIM
    return jnp.concatenate([nat[:, :a], nat[:, a + Hf:]], axis=1), nat[:, a:a + Hf].T


def _join_dw_in(dw6, dwf_t, Hf):
    D = dw6.shape[0]
    a = 3 * Hf * HEAD_DIM
    nat = jnp.concatenate([dw6[:, :a], dwf_t.T.astype(dw6.dtype), dw6[:, a:]], axis=1)
    return nat.reshape(D, N_CHIPS, nat.shape[1] // N_CHIPS).transpose(1, 0, 2)


def _tied(v, tokens, name):
    return _tie(v, tokens, name) if tokens else v


def _layer_fwd(x, p, weight, bias, tokens, tag):
    Hf, Hd = p["forget_b"].shape[0], bias.shape[1]
    h1 = _rms_fwd(x, _tied(p["norm1_g"], tokens, f"tie_norm1_{tag}"), f"norm1_{tag}")
    w6, wf_t = _split_w_in(weight("w_in", h1), Hf)
    n_a = 3 * Hf * HEAD_DIM
    proj_a = _mm_nn(h1, w6, f"proj_a_{tag}", [CDT], epi=lambda acc: (acc,), b_cols=(0, n_a))[0]
    proj_b = _mm_nn(h1, w6, f"proj_b_{tag}", [F32], b_cols=(n_a, w6.shape[1] - n_a))[0]
    f_t = _mm_nt(wf_t, h1, f"fproj_{tag}", [F32])[0]
    qc, kc = _fox_bias_operands(_gates_fwd(f_t, p["forget_b"], f"gates_{tag}"))
    y_a, lse_a = _fox_fwd(proj_a, qc, kc, Hf, f"fox_{tag}")
    y_b, lse_b = _dil_fwd(proj_b, bias, Hd, f"dil_{tag}")
    mixed = jnp.concatenate([_rms_fwd(y_a, p["outnorm_a_g"], f"norm_a_{tag}"),
                             _rms_fwd(y_b, p["outnorm_b_g"], f"norm_b_{tag}")], axis=1)
    w_out = weight("w_out", mixed)
    w_out = w_out.reshape(-1, w_out.shape[2])
    x1 = _mm_nn(mixed, w_out, f"attn_out_{tag}", [F32], extras=[x])[0]
    h2 = _rms_fwd(x1, p["norm2_g"], f"norm2_{tag}")
    w_mi = weight("w_mlp_in", h2)
    u, act = _mm_nn(h2, w_mi, f"mlp_in_{tag}", [CDT, CDT], b_slots=True,
                    epi=lambda acc: (acc, jnp.square(jnp.maximum(acc, 0.0))))
    w_mo = weight("w_mlp_out", act)
    w_mo = w_mo.reshape(-1, w_mo.shape[2])
    x2 = _mm_nn(act, w_mo, f"mlp_out_{tag}", [F32], extras=[x1])[0]
    saved = dict(x=x, h1=h1, proj_a=proj_a, proj_b=proj_b, f_t=f_t, qc=qc, kc=kc, y_a=y_a, lse_a=lse_a, y_b=y_b,
                 lse_b=lse_b, mixed=mixed, x1=x1, h2=h2, u=u, act=act, w6=w6, wf_t=wf_t, w_out=w_out, w_mi=w_mi,
                 w_mo=w_mo)
    return x2, saved


def _layer_bwd(dx2, dx2c, p, send, bias, sv, defer_w_out, tag):
    Hf, Hd = p["forget_b"].shape[0], bias.shape[1]
    E = HEAD_DIM
    rows = lambda g: g.reshape(N_CHIPS, -1, g.shape[1])
    du = _mm_nt(dx2c, sv["w_mo"], f"d_act_{tag}", [CDT], extras=[sv["u"]],
                epi=lambda acc, u: (acc * (2.0 * jnp.maximum(u.astype(F32), 0.0)),))[0]
    tokens = send("w_mlp_out", rows(_mm_tn(sv["act"], dx2c, f"dw_mlp_out_{tag}", CDT)))
    dh2 = _mm_nt(du, sv["w_mi"], f"d_h2_{tag}", [F32], b_slots=True)[0]
    tokens = tokens + send("w_mlp_in", _mm_tn(sv["h2"], du, f"dw_mlp_in_{tag}", CDT, out_slots=N_CHIPS))
    dx1, dx1c, g_norm2 = _rms_bwd(sv["x1"], _tied(p["norm2_g"], tokens, f"tie_norm2_{tag}"), dh2, dx2,
                                  f"d_norm2_{tag}")
    dmixed = _mm_nt(dx1c, sv["w_out"], f"d_mixed_{tag}", [F32])[0]
    send_w_out = lambda: send("w_out", rows(_mm_tn(sv["mixed"], dx1c, f"dw_out_{tag}", CDT)))
    tokens = [] if defer_w_out else send_w_out()
    dy_a, _, g_na = _rms_bwd(sv["y_a"], _tied(p["outnorm_a_g"], tokens, f"tie_norm_a_{tag}"), dmixed, None,
                             f"d_norm_a_{tag}", dh_col=0)
    dy_b, _, g_nb = _rms_bwd(sv["y_b"], p["outnorm_b_g"], dmixed, None, f"d_norm_b_{tag}", dh_col=1)
    dq_a, dcq, dk_a, dv_a, dck = _fox_bwd(sv["proj_a"], sv["qc"], sv["kc"], sv["lse_a"], sv["y_a"], dy_a, Hf,
                                          f"fox_bwd_{tag}")
    df, dfc, g_fb = _gates_bwd(sv["f_t"], p["forget_b"], dcq[:, ::E].T, dck.reshape(Hf, -1), f"d_gates_{tag}")
    dq_b, dk_b, dv_b, dbias = _dil_bwd(sv["proj_b"], bias, sv["y_b"], dy_b, sv["lse_b"], Hd, f"dil_bwd_{tag}")
    dproj = jnp.concatenate([dq_a, dk_a, dv_a, dq_b, dk_b, dv_b], axis=1)
    g_w6 = _mm_tn(sv["h1"], dproj, f"dw_in_{tag}", CDT)
    g_wf_t = _mm_nn(dfc, sv["h1"], f"dw_f_{tag}", [F32])[0]
    tokens = send("w_in", _join_dw_in(g_w6, g_wf_t, Hf))
    dh1_f = _mm_tn(dfc, _tied(sv["wf_t"], tokens, f"tie_wf_{tag}"), f"d_h1_f_{tag}", F32)
    dh1 = _mm_nt(dproj, sv["w6"], f"d_h1_{tag}", [F32], extras=[dh1_f])[0]
    dx, dxc, g_norm1 = _rms_bwd(sv["x"], p["norm1_g"], dh1, dx1, f"d_norm1_{tag}")
    grads = dict(norm1_g=g_norm1[0], norm2_g=g_norm2[0], outnorm_a_g=g_na[0], outnorm_b_g=g_nb[0],
                 forget_b=g_fb[:, 0], dbias=dbias)
    return dx, dxc, grads, (send_w_out if defer_w_out else None)


_LAYER_SMALL = ("norm1_g", "forget_b", "outnorm_a_g", "outnorm_b_g", "norm2_g")


def _local_step(x, target, small, weight, send, tokens):
    depth = small["norm1_g"].shape[0]
    buckets = _bucket_table()
    bias = _bias_table(small["rel_bias"], buckets, "bias_table")
    layers, saved = [], []
    for l in range(depth):
        p = {k: small[k][l] for k in _LAYER_SMALL}
        layers.append(p)
        x, sv = _layer_fwd(x, p, functools.partial(weight, l), bias, tokens if l == 0 else [], f"l{l}")
        saved.append(sv)
    dx, dxc, g_final, loss = _loss_bwd(x, small["final_norm_g"], target, "loss")
    layer_grads = [None] * depth
    for l in reversed(range(depth)):
        dx, dxc, layer_grads[l], last = _layer_bwd(dx, dxc, layers[l], functools.partial(send, l), bias, saved[l],
                                                   l == 0, f"l{l}")
    last()
    dbias = functools.reduce(jnp.add, [g["dbias"] for g in layer_grads])
    g_rel = _bias_table_bwd(dbias, buckets, "d_bias_table")[:, 0, :].T
    small_grads = dict(final_norm_g=g_final[0], rel_bias=g_rel,
                       **{k: jnp.stack([g[k] for g in layer_grads]) for k in _LAYER_SMALL})
    return loss[0, 0], dx, small_grads


_BIG = ("w_in", "w_out", "w_mlp_in", "w_mlp_out")
_SMALL = ("norm1_g", "forget_b", "rel_bias", "outnorm_a_g", "outnorm_b_g", "norm2_g", "final_norm_g")
_ORDER = ("norm1_g", "w_in", "forget_b", "rel_bias", "outnorm_a_g", "outnorm_b_g", "w_out", "norm2_g", "w_mlp_in",
          "w_mlp_out", "final_norm_g")


def _pack_small(d):
    flat = jnp.concatenate([d[k].reshape(-1) for k in _SMALL])
    rows = -(-flat.shape[0] // (8 * SMALL_COLS)) * 8
    return jnp.pad(flat, (0, rows * SMALL_COLS - flat.shape[0])).reshape(rows, SMALL_COLS)


def _unpack_small(packed, like):
    flat, out, at = packed.reshape(-1), {}, 0
    for k in _SMALL:
        n = like[k].size
        out[k] = flat[at:at + n].reshape(like[k].shape)
        at += n
    return out


def kernel(x, norm1_g, w_in, forget_b, rel_bias, outnorm_a_g, outnorm_b_g, w_out, norm2_g, w_mlp_in, w_mlp_out, final_norm_g, loss_target, m_norm1_g, m_w_in, m_forget_b, m_rel_bias, m_outnorm_a_g, m_outnorm_b_g, m_w_out, m_norm2_g, m_w_mlp_in, m_w_mlp_out, m_final_norm_g, v_norm1_g, v_w_in, v_forget_b, v_rel_bias, v_outnorm_a_g, v_outnorm_b_g, v_w_out, v_norm2_g, v_w_mlp_in, v_w_mlp_out, v_final_norm_g):
    w = dict(norm1_g=norm1_g, w_in=w_in, forget_b=forget_b, rel_bias=rel_bias, outnorm_a_g=outnorm_a_g,
             outnorm_b_g=outnorm_b_g, w_out=w_out, norm2_g=norm2_g, w_mlp_in=w_mlp_in, w_mlp_out=w_mlp_out,
             final_norm_g=final_norm_g)
    m = dict(norm1_g=m_norm1_g, w_in=m_w_in, forget_b=m_forget_b, rel_bias=m_rel_bias, outnorm_a_g=m_outnorm_a_g,
             outnorm_b_g=m_outnorm_b_g, w_out=m_w_out, norm2_g=m_norm2_g, w_mlp_in=m_w_mlp_in,
             w_mlp_out=m_w_mlp_out, final_norm_g=m_final_norm_g)
    v = dict(norm1_g=v_norm1_g, w_in=v_w_in, forget_b=v_forget_b, rel_bias=v_rel_bias, outnorm_a_g=v_outnorm_a_g,
             outnorm_b_g=v_outnorm_b_g, w_out=v_w_out, norm2_g=v_norm2_g, w_mlp_in=v_w_mlp_in,
             w_mlp_out=v_w_mlp_out, final_norm_g=v_final_norm_g)
    depth = w_in.shape[0]
    small = {k: w[k] for k in _SMALL}

    gathers, tokens = {}, []
    for l in range(depth):
        for k in _BIG:
            buf = _cast_into_slot(w[k], l, f"cast_{k}_l{l}")
            send_sems, recv_sems, buf, token = _gather_start(buf, tokens, f"gather_start_{k}_l{l}")
            gathers[l, k], tokens = (send_sems, recv_sems, buf), [token]

    def weight(l, k, after):
        return _gather_wait(*gathers[l, k], after, f"gather_wait_{k}_l{l}")

    scatters = {}

    def send(l, k, g):
        scatters[l, k] = _scatter_start(g, f"rs_start_{k}_l{l}")
        return [scatters[l, k][4]]

    loss, grad_x, small_grads = _local_step(x[0], loss_target[0], small, weight, send, tokens)
    loss = lax.psum(loss, ("x", "y", "c"))
    small_sums = _all_reduce_small(_pack_small(small_grads), "small_all_reduce")

    grads, delta, new_m, new_v = {}, {}, {}, {}
    after, seen = small_sums, {k: 0 for k in _BIG}
    for (l, k), started in scatters.items():
        grads[k] = _reduce_scatter_finish(started, after, l, depth, grads.get(k), f"{k}_l{l}")
        seen[k] += 1
        if seen[k] == depth:
            shape = w[k].shape
            flat = lambda t: t.reshape(-1, shape[-1])
            g_, d_, m_, v_ = _adamw(flat(w[k]), flat(grads[k]), flat(m[k]), flat(v[k]), f"adamw_{k}")
            grads[k], delta[k], new_m[k], new_v[k] = (t.reshape(shape) for t in (g_, d_, m_, v_))
            after = d_
    grads.update(_unpack_small(small_sums, small))
    _, d_, m_, v_ = _adamw(_pack_small(small), _pack_small({k: grads[k] for k in _SMALL}),
                           _pack_small({k: m[k] for k in _SMALL}), _pack_small({k: v[k] for k in _SMALL}), "adamw_small")
    delta.update(_unpack_small(d_, small))
    new_m.update(_unpack_small(m_, small))
    new_v.update(_unpack_small(v_, small))

    return (loss, grad_x[None], *[grads[k] for k in _ORDER], *[delta[k] for k in _ORDER],
            *[new_m[k] for k in _ORDER], *[new_v[k] for k in _ORDER])
```

```python
import functools

import jax
import jax.numpy as jnp
from jax import lax
from jax.experimental import pallas as pl
from jax.experimental.pallas import tpu as pltpu

F32 = jnp.float32
CDT = jnp.bfloat16
HEAD_DIM = 128
NORM_EPS = 1e-6
NEG_INF = -1e30
LOG2E = 1.4426950408889634
REL_BUCKETS = 32
REL_MAX_DISTANCE = 2048
DIL_PATTERNS = ((128, 1), (512, 4), (2048, 16))
DIL_BLOCK = 128
ADAM_LR, ADAM_B1, ADAM_B2, ADAM_EPS, ADAM_WD, ADAM_STEP = 0.001, 0.9, 0.999, 1e-08, 0.01, 10
N_CHIPS = 4
N_DEV = 8
VMEM_LIMIT_BYTES = 56 * 1024 * 1024
SMALL_COLS = 1024
MESH = pl.DeviceIdType.MESH


def _cparams(sem=None):
    return pltpu.CompilerParams(dimension_semantics=sem, vmem_limit_bytes=VMEM_LIMIT_BYTES)


def _tile(dim, pref):
    t = min(pref, dim)
    t -= t % 128
    while t >= 128:
        if dim % t == 0:
            return t
        t -= 128
    return dim


def _rowwise(fn, ins, out_dtypes, name, bs=256, consts=()):
    R, C = ins[0].shape
    bs = min(bs, R)
    n_in, n_c = len(ins), len(consts)

    def body(*refs):
        vals = [r[...] for r in refs[:n_in + n_c]]
        res = fn(*vals)
        for o, r in zip(refs[n_in + n_c:], res):
            o[...] = r.astype(o.dtype)

    row = pl.BlockSpec((bs, C), lambda i: (i, 0))
    return pl.pallas_call(
        body, grid=(R // bs,),
        in_specs=[row] * n_in + [pl.BlockSpec((1, c.shape[-1]), lambda i: (0, 0)) for c in consts],
        out_specs=[row] * len(out_dtypes),
        out_shape=[jax.ShapeDtypeStruct((R, C), d) for d in out_dtypes],
        name=name, compiler_params=_cparams(("parallel",)),
    )(*ins, *[c.reshape(1, -1) for c in consts])


def _rms_fwd(x, g, name):
    def fn(xf, gg):
        r = lax.rsqrt(jnp.mean(xf * xf, axis=-1, keepdims=True) + NORM_EPS)
        return ((xf * r) * gg,)
    return _rowwise(fn, [x], [CDT], name, consts=[g])[0]


def _rms_bwd(x, g, dh, dres, name, bs=256):
    S, D = x.shape
    bs = min(bs, S)
    has_res = dres is not None

    def body(*refs):
        x_ref, g_ref, dh_ref = refs[:3]
        dx_ref, dxc_ref, dg_ref = refs[-3:]
        xf = x_ref[...]
        r = lax.rsqrt(jnp.mean(xf * xf, axis=-1, keepdims=True) + NORM_EPS)
        xhat = xf * r
        dh_ = dh_ref[...].astype(F32)
        dxhat = dh_ * g_ref[...]
        dx = r * (dxhat - xhat * jnp.mean(dxhat * xhat, axis=-1, keepdims=True))
        if has_res:
            dx = dx + refs[3][...]
        dx_ref[...] = dx
        dxc_ref[...] = dx.astype(dxc_ref.dtype)
        part = jnp.sum(dh_ * xhat, axis=0, keepdims=True)

        @pl.when(pl.program_id(0) == 0)
        def _():
            dg_ref[...] = part

        @pl.when(pl.program_id(0) > 0)
        def _():
            dg_ref[...] += part

    row = pl.BlockSpec((bs, D), lambda i: (i, 0))
    one = pl.BlockSpec((1, D), lambda i: (0, 0))
    ins = [x, g.reshape(1, D), dh] + ([dres] if has_res else [])
    return pl.pallas_call(
        body, grid=(S // bs,),
        in_specs=[row, one, row] + ([row] if has_res else []),
        out_specs=[row, row, one],
        out_shape=[jax.ShapeDtypeStruct((S, D), F32), jax.ShapeDtypeStruct((S, D), CDT),
                   jax.ShapeDtypeStruct((1, D), F32)],
        name=name, compiler_params=_cparams(("arbitrary",)),
    )(*ins)


def _pair_norm_fwd(y_a, y_b, g_a, g_b, name, bs=256):
    S, Da = y_a.shape
    Db = y_b.shape[1]
    bs = min(bs, S)

    def body(a_ref, b_ref, ga_ref, gb_ref, o_ref):
        def norm(x, g):
            r = lax.rsqrt(jnp.mean(x * x, axis=-1, keepdims=True) + NORM_EPS)
            return ((x * r) * g).astype(o_ref.dtype)
        o_ref[:, :Da] = norm(a_ref[...], ga_ref[...])
        o_ref[:, Da:] = norm(b_ref[...], gb_ref[...])

    row = lambda n: pl.BlockSpec((bs, n), lambda i: (i, 0))
    one = lambda n: pl.BlockSpec((1, n), lambda i: (0, 0))
    return pl.pallas_call(
        body, grid=(S // bs,), in_specs=[row(Da), row(Db), one(Da), one(Db)], out_specs=row(Da + Db),
        out_shape=jax.ShapeDtypeStruct((S, Da + Db), CDT), name=name, compiler_params=_cparams(("parallel",)),
    )(y_a, y_b, g_a.reshape(1, Da), g_b.reshape(1, Db))


def _pair_norm_bwd(y_a, y_b, g_a, g_b, dmixed, name, bs=256):
    S, Da = y_a.shape
    Db = y_b.shape[1]
    bs = min(bs, S)

    def body(a_ref, b_ref, ga_ref, gb_ref, dm_ref, da_ref, db_ref, dga_ref, dgb_ref):
        def one(x_ref, g_ref, dh, dx_ref, dg_ref):
            xf = x_ref[...]
            r = lax.rsqrt(jnp.mean(xf * xf, axis=-1, keepdims=True) + NORM_EPS)
            xhat = xf * r
            dxhat = dh * g_ref[...]
            dx_ref[...] = r * (dxhat - xhat * jnp.mean(dxhat * xhat, axis=-1, keepdims=True))
            part = jnp.sum(dh * xhat, axis=0, keepdims=True)

            @pl.when(pl.program_id(0) == 0)
            def _():
                dg_ref[...] = part

            @pl.when(pl.program_id(0) > 0)
            def _():
                dg_ref[...] += part

        dm = dm_ref[...]
        one(a_ref, ga_ref, dm[:, :Da], da_ref, dga_ref)
        one(b_ref, gb_ref, dm[:, Da:], db_ref, dgb_ref)

    row = lambda n: pl.BlockSpec((bs, n), lambda i: (i, 0))
    one_ = lambda n: pl.BlockSpec((1, n), lambda i: (0, 0))
    return pl.pallas_call(
        body, grid=(S // bs,), in_specs=[row(Da), row(Db), one_(Da), one_(Db), row(Da + Db)],
        out_specs=[row(Da), row(Db), one_(Da), one_(Db)],
        out_shape=[jax.ShapeDtypeStruct((S, Da), F32), jax.ShapeDtypeStruct((S, Db), F32),
                   jax.ShapeDtypeStruct((1, Da), F32), jax.ShapeDtypeStruct((1, Db), F32)],
        name=name, compiler_params=_cparams(("arbitrary",)),
    )(y_a, y_b, g_a.reshape(1, Da), g_b.reshape(1, Db), dmixed)


def _loss_bwd(x, g, target, name, bs=256):
    S, D = x.shape
    bs = min(bs, S)

    def body(x_ref, g_ref, t_ref, dx_ref, dxc_ref, dg_ref, loss_ref):
        xf = x_ref[...]
        r = lax.rsqrt(jnp.mean(xf * xf, axis=-1, keepdims=True) + NORM_EPS)
        xhat = xf * r
        err = xhat * g_ref[...] - t_ref[...]
        lpart = 0.5 * jnp.sum(jnp.mean(err * err, axis=-1, keepdims=True), axis=0, keepdims=True)
        dy = err / D
        dxhat = dy * g_ref[...]
        dx = r * (dxhat - xhat * jnp.mean(dxhat * xhat, axis=-1, keepdims=True))
        dx_ref[...] = dx
        dxc_ref[...] = dx.astype(dxc_ref.dtype)
        gpart = jnp.sum(dy * xhat, axis=0, keepdims=True)

        @pl.when(pl.program_id(0) == 0)
        def _():
            dg_ref[...] = gpart
            loss_ref[...] = lpart

        @pl.when(pl.program_id(0) > 0)
        def _():
            dg_ref[...] += gpart
            loss_ref[...] += lpart

    row = pl.BlockSpec((bs, D), lambda i: (i, 0))
    one = pl.BlockSpec((1, D), lambda i: (0, 0))
    return pl.pallas_call(
        body, grid=(S // bs,),
        in_specs=[row, one, row],
        out_specs=[row, row, one, pl.BlockSpec((1, 1), lambda i: (0, 0))],
        out_shape=[jax.ShapeDtypeStruct((S, D), F32), jax.ShapeDtypeStruct((S, D), CDT),
                   jax.ShapeDtypeStruct((1, D), F32), jax.ShapeDtypeStruct((1, 1), F32)],
        name=name, compiler_params=_cparams(("arbitrary",)),
    )(x, g.reshape(1, D), target)


_NN = (((1,), (0,)), ((), ()))
_NT = (((1,), (1,)), ((), ()))
_TN = (((0,), (0,)), ((), ()))


def _mm(a, b, *, M, N, K, a_spec, b_spec, o_spec, dims, tm, tn, tk, name, out_shapes, extras=(), epi=None):
    nk = K // tk
    n_ex, n_out = len(extras), len(out_shapes)
    in_place = epi is None
    if in_place:
        assert n_out == 1 and n_ex <= 1 and out_shapes[0].dtype == F32
        epi = lambda acc, *r: (acc + r[0] if r else acc,)

    def body(*refs):
        a_ref, b_ref = refs[0], refs[1]
        ex = refs[2:2 + n_ex]
        outs = refs[2 + n_ex:2 + n_ex + n_out]
        part = lax.dot_general(a_ref[...], b_ref[...], dims, preferred_element_type=F32)

        def finish(acc):
            for o, r in zip(outs, epi(acc, *[e[...] for e in ex])):
                o[...] = r.astype(o.dtype)

        if nk == 1:
            finish(part)
        elif in_place:
            k = pl.program_id(2)

            @pl.when(k == 0)
            def _():
                finish(part)

            @pl.when(k > 0)
            def _():
                outs[0][...] += part
        else:
            acc_ref = refs[-1]
            k = pl.program_id(2)

            @pl.when(k == 0)
            def _():
                acc_ref[...] = part

            @pl.when(k > 0)
            def _():
                acc_ref[...] += part

            @pl.when(k == nk - 1)
            def _():
                finish(acc_ref[...])

    ex_spec = pl.BlockSpec((tm, tn), lambda i, j, k: (i, j))
    return pl.pallas_call(
        body, grid=(M // tm, N // tn, nk),
        in_specs=[a_spec, b_spec] + [ex_spec] * n_ex,
        out_specs=[o_spec] * n_out,
        out_shape=out_shapes,
        scratch_shapes=[pltpu.VMEM((tm, tn), F32)] if nk > 1 and not in_place else [],
        name=name, compiler_params=_cparams(("parallel", "parallel", "arbitrary")),
    )(a, b, *extras)


def _mm_tiles(K):
    return (2048, 512, 2048) if K <= 2048 else (1024, 1024, 2048)


def _mm_nn(a, b, name, out_dtypes, extras=(), epi=None, b_slots=False, b_cols=None):
    M, K = a.shape
    tm, tn, tk = _mm_tiles(K)
    if b_slots:
        ns, _, Ns = b.shape
        N = ns * Ns
        tn = _tile(Ns, tn)
        npb = Ns // tn
        tk_ = _tile(K, tk)
        b_spec = pl.BlockSpec((None, tk_, tn), lambda i, j, k: (j // npb, k, j % npb))
    else:
        first, N = b_cols if b_cols is not None else (0, b.shape[1])
        tn = _tile(N, tn)
        assert first % tn == 0
        tk_ = _tile(K, tk)
        b_spec = pl.BlockSpec((tk_, tn), lambda i, j, k: (k, first // tn + j))
    tm = _tile(M, tm)
    return _mm(a, b, M=M, N=N, K=K, a_spec=pl.BlockSpec((tm, tk_), lambda i, j, k: (i, k)), b_spec=b_spec,
               o_spec=pl.BlockSpec((tm, tn), lambda i, j, k: (i, j)), dims=_NN, tm=tm, tn=tn, tk=tk_, name=name,
               out_shapes=[jax.ShapeDtypeStruct((M, N), d) for d in out_dtypes], extras=extras, epi=epi)


def _mm_nt(a, b, name, out_dtypes, extras=(), epi=None, b_slots=False):
    M, K = a.shape
    tm, tn, tk = _mm_tiles(K)
    tm = _tile(M, tm)
    if b_slots:
        ns, N, Ks = b.shape
        tk_ = _tile(Ks, tk)
        kpb = Ks // tk_
        tn = _tile(N, tn)
        b_spec = pl.BlockSpec((None, tn, tk_), lambda i, j, k: (k // kpb, j, k % kpb))
    else:
        N = b.shape[0]
        tk_ = _tile(K, tk)
        tn = _tile(N, tn)
        b_spec = pl.BlockSpec((tn, tk_), lambda i, j, k: (j, k))
    return _mm(a, b, M=M, N=N, K=K, a_spec=pl.BlockSpec((tm, tk_), lambda i, j, k: (i, k)), b_spec=b_spec,
               o_spec=pl.BlockSpec((tm, tn), lambda i, j, k: (i, j)), dims=_NT, tm=tm, tn=tn, tk=tk_, name=name,
               out_shapes=[jax.ShapeDtypeStruct((M, N), d) for d in out_dtypes], extras=extras, epi=epi)


def _mm_tn(a, b, name, out_dtype, out_slots=0, tm=2048, tn=1024, tk=2048):
    K, M = a.shape
    N = b.shape[1]
    tm, tk_ = _tile(M, tm), _tile(K, tk)
    if out_slots:
        Ns = N // out_slots
        tn = _tile(Ns, tn)
        npb = Ns // tn
        o_spec = pl.BlockSpec((None, tm, tn), lambda i, j, k: (j // npb, i, j % npb))
        out_shape = jax.ShapeDtypeStruct((out_slots, M, Ns), out_dtype)
    else:
        tn = _tile(N, tn)
        o_spec = pl.BlockSpec((tm, tn), lambda i, j, k: (i, j))
        out_shape = jax.ShapeDtypeStruct((M, N), out_dtype)
    return _mm(a, b, M=M, N=N, K=K, a_spec=pl.BlockSpec((tk_, tm), lambda i, j, k: (k, i)),
               b_spec=pl.BlockSpec((tk_, tn), lambda i, j, k: (k, j)), o_spec=o_spec, dims=_TN,
               tm=tm, tn=tn, tk=tk_, name=name, out_shapes=[out_shape],
               epi=None if out_dtype == F32 else (lambda acc: (acc,)))[0]


GATE_BLOCK = 512


def _split3(v):
    hi = v.astype(jnp.bfloat16)
    r1 = v - hi.astype(F32)
    mid = r1.astype(jnp.bfloat16)
    lo = (r1 - mid.astype(F32)).astype(jnp.bfloat16)
    return hi, mid, lo


def _exact_dot(v, tri):
    return functools.reduce(jnp.add, [jnp.dot(t, tri, preferred_element_type=F32) for t in _split3(v)])


def _gates_fwd(f_t, b, name):
    H, S = f_t.shape
    nb = _tile(S, GATE_BLOCK)
    inv_scale = HEAD_DIM ** 0.5

    def body(f_ref, b_ref, c_ref):
        upper = (lax.broadcasted_iota(jnp.int32, (nb, nb), 0)
                 <= lax.broadcasted_iota(jnp.int32, (nb, nb), 1)).astype(jnp.bfloat16)
        carry = jnp.zeros((H, 1), F32)
        for i in range(S // nb):
            z = f_ref[:, i * nb:(i + 1) * nb] + b_ref[...]
            logf = jnp.minimum(z, 0.0) - jnp.log1p(jnp.exp(-jnp.abs(z)))
            cs = _exact_dot(logf, upper) + carry
            for j, t in enumerate(_split3(cs * inv_scale)):
                c_ref[j, :, i * nb:(i + 1) * nb] = t
            carry = cs[:, nb - 1:nb]

    return pl.pallas_call(body, out_shape=jax.ShapeDtypeStruct((3, H, S), jnp.bfloat16), name=name,
                          compiler_params=_cparams())(f_t, b.reshape(H, 1))


def _gates_bwd(f_t, b, dcq, dck, name):
    H, S = f_t.shape
    nb = _tile(S, GATE_BLOCK)

    def body(f_ref, b_ref, dcq_ref, dck_ref, df_ref, dfc_ref, db_ref):
        lower = (lax.broadcasted_iota(jnp.int32, (nb, nb), 0)
                 >= lax.broadcasted_iota(jnp.int32, (nb, nb), 1)).astype(jnp.bfloat16)
        carry = jnp.zeros((H, 1), F32)
        db = jnp.zeros((H, 1), F32)
        for i in reversed(range(S // nb)):
            sl = slice(i * nb, (i + 1) * nb)
            dc = dcq_ref[:, sl] - dck_ref[:, sl]
            dlogf = _exact_dot(dc, lower) + carry
            carry = dlogf[:, 0:1]
            z = f_ref[:, sl] + b_ref[...]
            df = dlogf / (1.0 + jnp.exp(z))
            df_ref[:, sl] = df
            dfc_ref[:, sl] = df.astype(dfc_ref.dtype)
            db = db + jnp.sum(df, axis=1, keepdims=True)
        db_ref[...] = db

    return pl.pallas_call(
        body, out_shape=[jax.ShapeDtypeStruct((H, S), F32), jax.ShapeDtypeStruct((H, S), CDT),
                         jax.ShapeDtypeStruct((H, 1), F32)],
        name=name, compiler_params=_cparams())(f_t, b.reshape(H, 1), dcq, dck)


FOX_BLOCK = 1024


def _fox_bias_operands(csplit):
    parts = csplit.transpose(2, 1, 0)
    S, H, _ = parts.shape
    ones = jnp.ones_like(parts)
    zeros = jnp.zeros((S, H, HEAD_DIM - 6), parts.dtype)
    qc = jnp.concatenate([parts, ones, zeros], axis=-1).reshape(S, H * HEAD_DIM)
    kc = jnp.concatenate([ones, -parts, zeros], axis=-1).reshape(S, H * HEAD_DIM)
    return qc, kc


def _fox_logits2(q_ref, qc_ref, k_ref, kc_ref, diag):
    q, k = q_ref[...], k_ref[...]
    qa = jnp.concatenate([q, qc_ref[...].astype(q.dtype)], axis=1)
    ka = jnp.concatenate([k, kc_ref[...].astype(k.dtype)], axis=1)
    s = lax.dot_general(qa, ka, _NT, preferred_element_type=F32) * (HEAD_DIM ** -0.5 * LOG2E)
    if diag:
        row = lax.broadcasted_iota(jnp.int32, s.shape, 0)
        col = lax.broadcasted_iota(jnp.int32, s.shape, 1)
        s = jnp.where(col <= row, s, NEG_INF)
    return s


def _fox_fwd(proj, qc, kc, H, name):
    S = proj.shape[0]
    E = HEAD_DIM
    blk = _tile(S, FOX_BLOCK)
    nq = S // blk

    def body(q_ref, qc_ref, k_ref, kc_ref, v_ref, o_ref, lse_ref, m_s, l_s, acc_s):
        qi, kj = pl.program_id(1), pl.program_id(2)

        @pl.when(kj == 0)
        def _():
            m_s[...] = jnp.full(m_s.shape, NEG_INF, F32)
            l_s[...] = jnp.zeros(l_s.shape, F32)
            acc_s[...] = jnp.zeros(acc_s.shape, F32)

        def step(diag):
            s = _fox_logits2(q_ref, qc_ref, k_ref, kc_ref, diag)
            m_prev = m_s[...]
            m_new = jnp.maximum(m_prev, jnp.max(s, axis=-1, keepdims=True))
            alpha = jnp.exp2(m_prev - m_new)
            p = jnp.exp2(s - m_new)
            l_s[...] = alpha * l_s[...] + jnp.sum(p, axis=-1, keepdims=True)
            acc_s[...] = alpha * acc_s[...] + jnp.dot(p.astype(CDT), v_ref[...], preferred_element_type=F32)
            m_s[...] = m_new

        pl.when(kj < qi)(lambda: step(False))
        pl.when(kj == qi)(lambda: step(True))

        @pl.when(kj == nq - 1)
        def _():
            o_ref[...] = acc_s[...] / l_s[...]
            lse_ref[...] = jnp.broadcast_to(m_s[...] + jnp.log2(l_s[...]), lse_ref.shape)

    qspec = lambda off: pl.BlockSpec((blk, E), lambda h, i, j: (i, off + h))
    kspec = lambda off: pl.BlockSpec((blk, E), lambda h, i, j: (jnp.minimum(j, i), off + h))
    return pl.pallas_call(
        body, grid=(H, nq, nq),
        in_specs=[qspec(0), qspec(0), kspec(H), kspec(0), kspec(2 * H)],
        out_specs=[qspec(0)] * 2,
        out_shape=[jax.ShapeDtypeStruct((S, H * E), F32)] * 2,
        scratch_shapes=[pltpu.VMEM((blk, 1), F32), pltpu.VMEM((blk, 1), F32), pltpu.VMEM((blk, E), F32)],
        name=name, compiler_params=_cparams(("parallel", "parallel", "arbitrary")),
    )(proj, qc, proj, kc, proj)


def _fox_bwd(proj, qc, kc, lse, o, do, H, name):
    S = proj.shape[0]
    E = HEAD_DIM
    blk = _tile(S, FOX_BLOCK)
    nq = S // blk
    scale = E ** -0.5

    def body(q_ref, qc_ref, k_ref, kc_ref, v_ref, lse_ref, o_ref, do_ref,
             dq_ref, dcq_ref, dk_ref, dv_ref, dck_ref, dq_s, dcq_s, dk_s, dv_s, dck_s):
        kj, qi = pl.program_id(1), pl.program_id(2)

        @pl.when(qi == 0)
        def _():
            dk_s[...] = jnp.zeros(dk_s.shape, F32)
            dv_s[...] = jnp.zeros(dv_s.shape, F32)
            dck_s[...] = jnp.zeros(dck_s.shape, F32)

        def step(diag):
            do = do_ref[...]
            doc = do.astype(CDT)
            delta = jnp.sum(do * o_ref[...], axis=-1, keepdims=True)
            p = jnp.exp2(_fox_logits2(q_ref, qc_ref, k_ref, kc_ref, diag) - lse_ref[:, 0:1])
            dp = lax.dot_general(doc, v_ref[...], _NT, preferred_element_type=F32)
            ds = p * (dp - delta)
            dss = ds * scale
            dck_s[...] += jnp.sum(ds, axis=0, keepdims=True)
            dv_s[...] += jnp.dot(p.T.astype(CDT), doc, preferred_element_type=F32)
            dk_s[...] += jnp.dot(dss.T.astype(CDT), q_ref[...], preferred_element_type=F32)
            dq_part = jnp.dot(dss.astype(CDT), k_ref[...], preferred_element_type=F32)
            dc_part = jnp.sum(ds, axis=-1, keepdims=True)
            rows = pl.ds(pl.multiple_of(qi * blk, blk), blk)

            @pl.when(kj == 0)
            def _():
                dq_s[rows, :] = dq_part
                dcq_s[rows, :] = dc_part

            @pl.when(kj > 0)
            def _():
                dq_s[rows, :] += dq_part
                dcq_s[rows, :] += dc_part

        pl.when(qi > kj)(lambda: step(False))
        pl.when(qi == kj)(lambda: step(True))

        @pl.when(qi == nq - 1)
        def _():
            dk_ref[...] = dk_s[...].astype(dk_ref.dtype)
            dv_ref[...] = dv_s[...].astype(dv_ref.dtype)
            dck_ref[...] = dck_s[...].reshape(dck_ref.shape)

        @pl.when((qi == nq - 1) & (kj == nq - 1))
        def _():
            dq_ref[...] = dq_s[...].astype(dq_ref.dtype)
            dcq_ref[...] = jnp.broadcast_to(dcq_s[...], dcq_ref.shape)

    qspec = lambda off: pl.BlockSpec((blk, E), lambda h, j, i: (jnp.maximum(i, j), off + h))
    kspec = lambda off: pl.BlockSpec((blk, E), lambda h, j, i: (j, off + h))
    head = pl.BlockSpec((S, E), lambda h, j, i: (0, h))
    return pl.pallas_call(
        body, grid=(H, nq, nq),
        in_specs=[qspec(0), qspec(0), kspec(H), kspec(0), kspec(2 * H), qspec(0), qspec(0), qspec(0)],
        out_specs=[head, head, kspec(0), kspec(0), pl.BlockSpec((1, 1, blk), lambda h, j, i: (h, 0, j))],
        out_shape=[jax.ShapeDtypeStruct((S, H * E), CDT), jax.ShapeDtypeStruct((S, H * E), F32),
                   jax.ShapeDtypeStruct((S, H * E), CDT), jax.ShapeDtypeStruct((S, H * E), CDT),
                   jax.ShapeDtypeStruct((H, 1, S), F32)],
        scratch_shapes=[pltpu.VMEM((S, E), F32), pltpu.VMEM((S, 1), F32), pltpu.VMEM((blk, E), F32),
                        pltpu.VMEM((blk, E), F32), pltpu.VMEM((1, blk), F32)],
        name=name, compiler_params=_cparams(("parallel", "arbitrary", "arbitrary")),
    )(proj, qc, proj, kc, proj, lse, o, do)


DIL_SLAB = 16 * DIL_BLOCK
DIL_UNROLL = 8


def _rel_bucket(dist):
    max_exact = REL_BUCKETS // 2
    d = jnp.maximum(dist.astype(F32), 1.0)
    large = max_exact + (jnp.log(d / max_exact) / jnp.log(jnp.float32(REL_MAX_DISTANCE / max_exact))
                         * (REL_BUCKETS - max_exact)).astype(jnp.int32)
    large = jnp.minimum(large, REL_BUCKETS - 1)
    return jnp.where(dist < max_exact, dist, large)


def _bucket_table():
    i = jnp.arange(DIL_BLOCK)[:, None]
    j = jnp.arange(2 * DIL_BLOCK)[None, :]
    rel = DIL_BLOCK + i - j
    tabs = [_rel_bucket(jnp.clip(rel, 0, w // d) * d) for w, d in DIL_PATTERNS]
    return jnp.stack(tabs).astype(jnp.int32)


def _bias_table(rel_bias, buckets, name):
    P = buckets.shape[0]
    H = rel_bias.shape[1]

    def body(rb_ref, bk_ref, out_ref):
        h = pl.program_id(1)
        bk = bk_ref[0]
        val = jnp.zeros(bk.shape, F32)
        for b in range(REL_BUCKETS):
            val = jnp.where(bk == b, rb_ref[b, h], val)
        out_ref[0, 0] = val

    return pl.pallas_call(
        body, grid=(P, H),
        in_specs=[pl.BlockSpec(memory_space=pltpu.SMEM),
                  pl.BlockSpec((1, DIL_BLOCK, 2 * DIL_BLOCK), lambda p, h: (p, 0, 0))],
        out_specs=pl.BlockSpec((1, 1, DIL_BLOCK, 2 * DIL_BLOCK), lambda p, h: (p, h, 0, 0)),
        out_shape=jax.ShapeDtypeStruct((P, H, DIL_BLOCK, 2 * DIL_BLOCK), F32),
        name=name, compiler_params=_cparams(("parallel", "parallel")),
    )(rel_bias, buckets)


def _bias_table_bwd(dbias, buckets, name):
    P, H = dbias.shape[:2]

    def body(db_ref, bk_ref, out_ref):
        lane = lax.broadcasted_iota(jnp.int32, (1, REL_BUCKETS), 1)
        acc = jnp.zeros((1, REL_BUCKETS), F32)
        bk = bk_ref[...]
        db = db_ref[:, 0]
        for b in range(REL_BUCKETS):
            tot = jnp.sum(jnp.where(bk == b, db, 0.0))
            acc = jnp.where(lane == b, tot, acc)
        out_ref[0] = acc

    return pl.pallas_call(
        body, grid=(H,),
        in_specs=[pl.BlockSpec((P, 1, DIL_BLOCK, 2 * DIL_BLOCK), lambda h: (0, h, 0, 0)),
                  pl.BlockSpec((P, DIL_BLOCK, 2 * DIL_BLOCK), lambda h: (0, 0, 0))],
        out_specs=pl.BlockSpec((1, 1, REL_BUCKETS), lambda h: (h, 0, 0)),
        out_shape=jax.ShapeDtypeStruct((H, 1, REL_BUCKETS), F32),
        name=name, compiler_params=_cparams(("parallel",)),
    )(dbias, buckets)


def _bdot(a, b, contract_b):
    return lax.dot_general(a, b, (((2,), (contract_b,)), ((0,), (0,))), preferred_element_type=F32)


def _dil_units(first, d):
    units = []
    for t in range(DIL_UNROLL):
        u = first + t
        sg = u // d
        units.append((sg, sg * (DIL_BLOCK * d) + u % d))
    return units


def _dil_rows(ref, starts, d, dtype=None):
    t = jnp.stack([ref[pl.ds(s, DIL_BLOCK, stride=d), :] for s in starts])
    return t if dtype is None else t.astype(dtype)


def _dil_keys(ref, units, d):
    B, SL = DIL_BLOCK, DIL_SLAB
    return jnp.stack([jnp.concatenate([ref[pl.ds(SL + b - B * d, B, stride=d), :], ref[pl.ds(SL + b, B, stride=d), :]],
                                      axis=0) for _, b in units]).astype(CDT)


def _dil_logits(q, keys, bias_pc, first, d, has_before):
    T, B = q.shape[0], DIL_BLOCK
    ii = lax.broadcasted_iota(jnp.int32, (T, B, 2 * B), 1)
    jj = lax.broadcasted_iota(jnp.int32, (T, B, 2 * B), 2)
    sg = (first + lax.broadcasted_iota(jnp.int32, (T, B, 2 * B), 0)) // d
    mask = (jj >= ii) & (jj <= ii + B) & ((jj >= B) | (sg > 0) | has_before)
    return jnp.where(mask, _bdot(q, keys, 2) * HEAD_DIM ** -0.5 + bias_pc[None], NEG_INF)


def _dil_specs(H, n_slabs):
    E, SL = DIL_BLOCK, DIL_SLAB
    cur = lambda off: pl.BlockSpec((SL, E), lambda h, g: (g, off + h))
    prev = lambda off: pl.BlockSpec((SL, E), lambda h, g: (jnp.maximum(g - 1, 0), off + h))
    nxt = lambda off: pl.BlockSpec((SL, E), lambda h, g: (jnp.minimum(g + 1, n_slabs - 1), off + h))
    bias = pl.BlockSpec((len(DIL_PATTERNS), 1, E, 2 * E), lambda h, g: (0, h, 0, 0))
    return cur, prev, nxt, bias


def _dil_fwd(proj, bias, H, name):
    S = proj.shape[0]
    E = B = DIL_BLOCK
    SL = DIL_SLAB
    P = len(DIL_PATTERNS)
    assert S % SL == 0
    n_slabs = S // SL

    def body(q_ref, kc_ref, kp_ref, vc_ref, vp_ref, b_ref, y_ref, lse_ref, kj, vj, o_s, l_s):
        g = pl.program_id(1)
        kj[0:SL, :] = kp_ref[...]
        kj[SL:2 * SL, :] = kc_ref[...]
        vj[0:SL, :] = vp_ref[...]
        vj[SL:2 * SL, :] = vc_ref[...]
        for p, (_, d) in enumerate(DIL_PATTERNS):
            def batch(it, carry, p=p, d=d):
                first = it * DIL_UNROLL
                units = _dil_units(first, d)
                q = _dil_rows(q_ref, [b for _, b in units], d, CDT)
                s = _dil_logits(q, _dil_keys(kj, units, d), b_ref[p, 0], first, d, g > 0)
                m = jnp.max(s, axis=-1, keepdims=True)
                e = jnp.exp(s - m)
                ssum = jnp.sum(e, axis=-1, keepdims=True)
                o = _bdot(e.astype(CDT), _dil_keys(vj, units, d), 1) / ssum
                lse = jnp.broadcast_to(m + jnp.log(ssum), o.shape)
                for t, (_, b) in enumerate(units):
                    o_s[p, pl.ds(b, B, stride=d), :] = o[t]
                    l_s[p, pl.ds(b, B, stride=d), :] = lse[t]
                return carry

            lax.fori_loop(0, SL // B // DIL_UNROLL, batch, 0)
        ls = [l_s[p] for p in range(P)]
        m = functools.reduce(jnp.maximum, ls)
        w = [jnp.exp(l - m) for l in ls]
        tot = functools.reduce(jnp.add, w)
        y_ref[...] = functools.reduce(jnp.add, [(w[p] / tot) * o_s[p] for p in range(P)])
        lse_ref[...] = m + jnp.log(tot)

    cur, prev, _, bspec = _dil_specs(H, n_slabs)
    return pl.pallas_call(
        body, grid=(H, n_slabs),
        in_specs=[cur(0), cur(H), prev(H), cur(2 * H), prev(2 * H), bspec],
        out_specs=[cur(0), cur(0)],
        out_shape=[jax.ShapeDtypeStruct((S, H * E), F32)] * 2,
        scratch_shapes=[pltpu.VMEM((2 * SL, E), F32), pltpu.VMEM((2 * SL, E), F32),
                        pltpu.VMEM((P, SL, E), F32), pltpu.VMEM((P, SL, E), F32)],
        name=name, compiler_params=_cparams(("parallel", "parallel")),
    )(proj, proj, proj, proj, proj, bias)


def _dil_bwd(proj, bias, y, dy, lse, H, name):
    S = proj.shape[0]
    E = B = DIL_BLOCK
    SL = DIL_SLAB
    P = len(DIL_PATTERNS)
    assert S % SL == 0
    n_slabs = S // SL
    scale = E ** -0.5

    def body(q_ref, kc_ref, kp_ref, vc_ref, vp_ref, b_ref, y_ref, dy_ref, lse_ref, qn_ref, yn_ref, dyn_ref, lsen_ref,
             dq_ref, dk_ref, dv_ref, db_ref, kj, vj, dq_s, dk_s, dv_s, dl_s, dln_s):
        g = pl.program_id(1)
        kj[0:SL, :] = kp_ref[...]
        kj[SL:2 * SL, :] = kc_ref[...]
        vj[0:SL, :] = vp_ref[...]
        vj[SL:2 * SL, :] = vc_ref[...]
        dq_s[...] = jnp.zeros(dq_s.shape, F32)
        dk_s[...] = jnp.zeros(dk_s.shape, F32)
        dv_s[...] = jnp.zeros(dv_s.shape, F32)
        dl_s[...] = jnp.broadcast_to(jnp.sum(dy_ref[...] * y_ref[...], axis=-1, keepdims=True), (SL, E))
        dln_s[...] = jnp.broadcast_to(jnp.sum(dyn_ref[...] * yn_ref[...], axis=-1, keepdims=True), (SL, E))

        @pl.when(g == 0)
        def _():
            db_ref[...] = jnp.zeros(db_ref.shape, F32)

        tr = lambda t: jnp.swapaxes(t, 1, 2).astype(CDT)
        for p, (_, d) in enumerate(DIL_PATTERNS):
            def batch(it, carry, p=p, d=d):
                first = it * DIL_UNROLL
                units = _dil_units(first, d)
                starts = [b for _, b in units]
                q = _dil_rows(q_ref, starts, d, CDT)
                dyc = _dil_rows(dy_ref, starts, d, CDT)
                keys, vals = _dil_keys(kj, units, d), _dil_keys(vj, units, d)
                s = _dil_logits(q, keys, b_ref[p, 0], first, d, g > 0)
                e = jnp.exp(s - _dil_rows(lse_ref, starts, d)[:, :, 0:1])
                ds = e * (_bdot(dyc, vals, 2) - _dil_rows(dl_s, starts, d)[:, :, 0:1])
                dss = ds * scale
                dq = _bdot(dss.astype(CDT), keys, 1)
                dk = _bdot(tr(dss), q, 1)
                dv = _bdot(tr(e), dyc, 1)
                for t, (sg, b) in enumerate(units):
                    rows = pl.ds(b, B, stride=d)
                    dq_s[rows, :] += dq[t]
                    dk_s[rows, :] += dk[t, B:]
                    dv_s[rows, :] += dv[t, B:]

                    @pl.when(sg > 0)
                    def _(t=t, b=b):
                        before = pl.ds(b - B * d, B, stride=d)
                        dk_s[before, :] += dk[t, :B]
                        dv_s[before, :] += dv[t, :B]

                db_ref[p, 0] += jnp.sum(ds, axis=0)
                return carry

            lax.fori_loop(0, SL // B // DIL_UNROLL, batch, 0)

            n_after = min(d, DIL_UNROLL)

            def after(it, carry, p=p, d=d, n_after=n_after):
                starts = [it * n_after + t for t in range(n_after)]
                kstarts = [SL - B * d + s for s in starts]
                q = _dil_rows(qn_ref, starts, d, CDT)
                dyc = _dil_rows(dyn_ref, starts, d, CDT)
                k, v = _dil_rows(kc_ref, kstarts, d, CDT), _dil_rows(vc_ref, kstarts, d, CDT)
                ii = lax.broadcasted_iota(jnp.int32, (n_after, B, B), 1)
                jj = lax.broadcasted_iota(jnp.int32, (n_after, B, B), 2)
                s = jnp.where((jj >= ii) & (g < n_slabs - 1), _bdot(q, k, 2) * scale + b_ref[p, 0][:, :B][None], NEG_INF)
                e = jnp.exp(s - _dil_rows(lsen_ref, starts, d)[:, :, 0:1])
                ds = e * (_bdot(dyc, v, 2) - _dil_rows(dln_s, starts, d)[:, :, 0:1])
                dk = _bdot(tr(ds * scale), q, 1)
                dv = _bdot(tr(e), dyc, 1)
                for t, ks in enumerate(kstarts):
                    dk_s[pl.ds(ks, B, stride=d), :] += dk[t]
                    dv_s[pl.ds(ks, B, stride=d), :] += dv[t]
                return carry

            lax.fori_loop(0, d // n_after, after, 0)

        dq_ref[...] = dq_s[...].astype(dq_ref.dtype)
        dk_ref[...] = dk_s[...].astype(dk_ref.dtype)
        dv_ref[...] = dv_s[...].astype(dv_ref.dtype)

    cur, prev, nxt, bspec = _dil_specs(H, n_slabs)
    slab = pltpu.VMEM((SL, E), F32)
    return pl.pallas_call(
        body, grid=(H, n_slabs),
        in_specs=[cur(0), cur(H), prev(H), cur(2 * H), prev(2 * H), bspec, cur(0), cur(0), cur(0),
                  nxt(0), nxt(0), nxt(0), nxt(0)],
        out_specs=[cur(0), cur(0), cur(0), bspec],
        out_shape=[jax.ShapeDtypeStruct((S, H * E), CDT)] * 3 + [jax.ShapeDtypeStruct((P, H, B, 2 * B), F32)],
        scratch_shapes=[pltpu.VMEM((2 * SL, E), F32), pltpu.VMEM((2 * SL, E), F32), slab, slab, slab, slab, slab],
        name=name, compiler_params=_cparams(("parallel", "arbitrary")),
    )(proj, proj, proj, proj, proj, bias, y, dy, lse, proj, y, dy, lse)


def _adamw(w, g, m, v, name, br=128):
    R, C = w.shape
    br = br if R % br == 0 else R

    def body(w_ref, g_ref, m_ref, v_ref, g_out, d_ref, nm_ref, nv_ref):
        g_ = g_ref[...]
        g_out[...] = g_
        m_ = ADAM_B1 * m_ref[...] + (1.0 - ADAM_B1) * g_
        v_ = ADAM_B2 * v_ref[...] + (1.0 - ADAM_B2) * jnp.square(g_)
        m_hat = m_ / (1.0 - ADAM_B1 ** ADAM_STEP)
        v_hat = v_ / (1.0 - ADAM_B2 ** ADAM_STEP)
        d_ref[...] = -ADAM_LR * (m_hat / (jnp.sqrt(v_hat) + ADAM_EPS) + ADAM_WD * w_ref[...])
        nm_ref[...] = m_
        nv_ref[...] = v_

    blk = pl.BlockSpec((br, C), lambda i: (i, 0))
    return pl.pallas_call(
        body, grid=(R // br,), in_specs=[blk] * 4, out_specs=[blk] * 4,
        out_shape=[jax.ShapeDtypeStruct((R, C), F32)] * 4,
        name=name, compiler_params=_cparams(("parallel",)),
    )(w, g, m, v)


_HBM = pl.BlockSpec(memory_space=pltpu.HBM)
_SEM = pl.BlockSpec(memory_space=pltpu.SEMAPHORE)
_ANY = pl.BlockSpec(memory_space=pl.ANY)
_VMEM = pl.BlockSpec(memory_space=pltpu.VMEM)
_TOKEN = jax.ShapeDtypeStruct((8, 128), F32)


def _split_params():
    return pltpu.CompilerParams(has_side_effects=pltpu.SideEffectType.DATAFLOW_SIDE_EFFECTING)


def _place():
    x, y, c = lax.axis_index("x"), lax.axis_index("y"), lax.axis_index("c")
    chips = [(1 - x, y), (x, 1 - y), (1 - x, 1 - y)]
    return x, y, c, chips


def _tie(v, tokens, name):
    flat = v.reshape(1, -1)

    def body(v_ref, *rest):
        rest[-1][...] = v_ref[...]

    return pl.pallas_call(body, in_specs=[_VMEM] + [_ANY] * len(tokens), out_specs=_VMEM,
                          out_shape=jax.ShapeDtypeStruct(flat.shape, flat.dtype), name=name,
                          compiler_params=_cparams())(flat, *tokens).reshape(v.shape)


def _row_block(R, pref=256):
    return _tile(R, pref) if R % 128 == 0 else R


def _slot():
    return 2 * lax.axis_index("x") + lax.axis_index("y")


def _cast_into_slot(w, layer, name):
    _, R, C = w.shape
    br = _row_block(R)

    def body(w_ref, out_ref):
        out_ref[...] = w_ref[...].astype(out_ref.dtype)

    return pl.pallas_call(
        body, grid=(R // br,),
        in_specs=[pl.BlockSpec((None, br, C), lambda i: (layer, i, 0))],
        out_specs=pl.BlockSpec((None, br, C), lambda i: (_slot(), i, 0)),
        out_shape=jax.ShapeDtypeStruct((N_CHIPS, R, C), CDT),
        name=name, compiler_params=_cparams(("parallel",)),
    )(w)


def _gather_copies(src_ref, dst_ref, send_sems, recv_sems, incoming):
    Rh = src_ref.shape[1] // 2
    x, y, c, chips = _place()
    slot = 2 * x + y

    def half(ref, s, hf):
        return ref.at[s, pl.ds(hf * Rh, Rh), :]

    copies = []
    for j, (cx, cy) in enumerate(chips):
        for e in range(2):
            copies.append(pltpu.make_async_remote_copy(
                src_ref=half(src_ref, slot, c), dst_ref=half(dst_ref, 2 * cx + cy, e) if incoming else half(dst_ref, slot, c),
                send_sem=send_sems.at[2 * j + e], recv_sem=recv_sems.at[2 * j + (e if incoming else c)],
                device_id=(cx, cy, e), device_id_type=MESH))
    return copies


def _gather_start(buf, after, name):
    n_after = len(after)

    def body(*refs):
        buf_ref = refs[0]
        send_sems, recv_sems, out_ref, token = refs[1 + n_after:]
        for cp in _gather_copies(buf_ref, out_ref, send_sems, recv_sems, incoming=False):
            cp.start()
        token[...] = jnp.zeros(token.shape, token.dtype)

    return pl.pallas_call(
        body, in_specs=[_HBM] + [_ANY] * n_after, out_specs=(_SEM, _SEM, _HBM, _VMEM),
        out_shape=(pltpu.SemaphoreType.DMA((6,)), pltpu.SemaphoreType.DMA((6,)), pltpu.HBM(buf.shape, buf.dtype), _TOKEN),
        input_output_aliases={0: 2}, name=name, compiler_params=_split_params(),
    )(pltpu.with_memory_space_constraint(buf, pltpu.HBM), *after)


def _gather_wait(send_sems, recv_sems, buf, after, name):
    def body(buf_ref, send_sems, recv_sems, after_ref, out_ref):
        for cp in _gather_copies(buf_ref, out_ref, send_sems, recv_sems, incoming=False):
            cp.wait_send()
        for cp in _gather_copies(buf_ref, out_ref, send_sems, recv_sems, incoming=True):
            cp.wait_recv()

    return pl.pallas_call(
        body, in_specs=[_HBM, _SEM, _SEM, _ANY], out_specs=_HBM, out_shape=pltpu.HBM(buf.shape, buf.dtype),
        input_output_aliases={0: 0}, name=name, compiler_params=_split_params(),
    )(buf, send_sems, recv_sems, after)


def _scatter_copies(g_ref, land_ref, send_sems, recv_sems, incoming):
    Rh = g_ref.shape[1] // 2
    x, y, c, _ = _place()
    me = 4 * x + 2 * y + c
    copies = []
    for k in range(1, N_DEV):
        px, py, pc = (x + (k >> 2)) % 2, (y + ((k >> 1) & 1)) % 2, (c + (k & 1)) % 2
        copies.append(pltpu.make_async_remote_copy(
            src_ref=g_ref.at[2 * px + py, pl.ds(pc * Rh, Rh), :],
            dst_ref=land_ref.at[4 * px + 2 * py + pc if incoming else me],
            send_sem=send_sems.at[k - 1], recv_sem=recv_sems.at[k - 1], device_id=(px, py, pc), device_id_type=MESH))
    return copies


def _scatter_start(g, name):
    ns, R, C = g.shape

    def body(g_ref, land_ref, send_sems, recv_sems, g_thru, land_thru, token):
        for cp in _scatter_copies(g_ref, land_thru, send_sems, recv_sems, incoming=False):
            cp.start()
        token[...] = jnp.zeros(token.shape, token.dtype)

    land = lax.empty((N_DEV, R // 2, C), g.dtype)
    n = N_DEV - 1
    return pl.pallas_call(
        body, in_specs=[_HBM, _HBM], out_specs=(_SEM, _SEM, _HBM, _HBM, _VMEM),
        out_shape=(pltpu.SemaphoreType.DMA((n,)), pltpu.SemaphoreType.DMA((n,)), pltpu.HBM(g.shape, g.dtype),
                   pltpu.HBM(land.shape, land.dtype), _TOKEN),
        input_output_aliases={0: 2, 1: 3}, name=name, compiler_params=_split_params(),
    )(pltpu.with_memory_space_constraint(g, pltpu.HBM), pltpu.with_memory_space_constraint(land, pltpu.HBM))


def _scatter_wait(send_sems, recv_sems, g, land, after, name):
    def body(g_ref, land_ref, send_sems, recv_sems, after_ref, g_out, land_out):
        for cp in _scatter_copies(g_ref, land_out, send_sems, recv_sems, incoming=False):
            cp.wait_send()
        for cp in _scatter_copies(g_ref, land_out, send_sems, recv_sems, incoming=True):
            cp.wait_recv()

    return pl.pallas_call(
        body, in_specs=[_HBM, _HBM, _SEM, _SEM, _ANY], out_specs=(_HBM, _HBM),
        out_shape=(pltpu.HBM(g.shape, g.dtype), pltpu.HBM(land.shape, land.dtype)),
        input_output_aliases={0: 0, 1: 1}, name=name, compiler_params=_split_params(),
    )(g, land, send_sems, recv_sems, after)


def _device_sum(land, g, layer, n_layers, prev, name):
    nd, Rh, C = land.shape
    br = _row_block(Rh)
    nb = Rh // br
    core = lambda: lax.axis_index("c")
    me = lambda: 2 * _slot() + core()

    def body(*refs):
        own = refs[nd][...]
        acc = None
        for d in range(nd):
            t = jnp.where(me() == d, own, refs[d][...]).astype(F32)
            acc = t if acc is None else acc + t
        refs[-1][...] = acc

    def piece(d):
        return pl.BlockSpec((None, br, C), lambda i: (jnp.where(me() == d, (d + 1) % nd, d), i, 0))

    ins = [land] * nd + [g] + ([prev] if prev is not None else [])
    return pl.pallas_call(
        body, grid=(nb,),
        in_specs=[piece(d) for d in range(nd)]
        + [pl.BlockSpec((None, br, C), lambda i: (_slot(), core() * nb + i, 0))]
        + ([_ANY] if prev is not None else []),
        out_specs=pl.BlockSpec((None, br, C), lambda i: (layer, core() * nb + i, 0)),
        out_shape=jax.ShapeDtypeStruct((n_layers, 2 * Rh, C), F32),
        input_output_aliases={nd + 1: 0} if prev is not None else {},
        name=name, compiler_params=_cparams(("parallel",)),
    )(*ins)


def _join_halves(g, layer, name):
    _, R, C = g.shape
    Rh = R // 2

    def body(g_ref, out_ref, send_sem, recv_sem):
        x, y, c, _ = _place()
        mine = pl.ds(c * Rh, Rh)
        cp = pltpu.make_async_remote_copy(src_ref=g_ref.at[layer, mine, :], dst_ref=out_ref.at[layer, mine, :],
                                          send_sem=send_sem, recv_sem=recv_sem, device_id=(x, y, 1 - c),
                                          device_id_type=MESH)
        cp.start()
        other = out_ref.at[layer, pl.ds((1 - c) * Rh, Rh), :]
        pltpu.make_async_remote_copy(src_ref=other, dst_ref=other, send_sem=send_sem, recv_sem=recv_sem,
                                     device_id=(x, y, 1 - c), device_id_type=MESH).wait_recv()
        cp.wait_send()

    return pl.pallas_call(
        body, in_specs=[_HBM], out_specs=_HBM, out_shape=jax.ShapeDtypeStruct(g.shape, g.dtype),
        input_output_aliases={0: 0},
        scratch_shapes=[pltpu.SemaphoreType.DMA, pltpu.SemaphoreType.DMA],
        name=name, compiler_params=pltpu.CompilerParams(),
    )(g)


def _all_reduce_small(v, name):
    rows, cols = v.shape

    def body(v_ref, out_ref, buf, send_sems, recv_sems):
        x, y, c, _ = _place()
        me = 4 * x + 2 * y + c
        buf[me] = v_ref[...]
        peers = []
        for k in range(1, N_DEV):
            px, py, pc = (x + (k >> 2)) % 2, (y + ((k >> 1) & 1)) % 2, (c + (k & 1)) % 2
            peers.append((px, py, pc))
        sends = []
        for k, peer in enumerate(peers):
            cp = pltpu.make_async_remote_copy(src_ref=v_ref, dst_ref=buf.at[me], send_sem=send_sems.at[k],
                                              recv_sem=recv_sems.at[k], device_id=peer, device_id_type=MESH)
            cp.start()
            sends.append(cp)
        for k, (px, py, pc) in enumerate(peers):
            pltpu.make_async_remote_copy(src_ref=v_ref, dst_ref=buf.at[4 * px + 2 * py + pc], send_sem=send_sems.at[k],
                                         recv_sem=recv_sems.at[k], device_id=(px, py, pc),
                                         device_id_type=MESH).wait_recv()
        for cp in sends:
            cp.wait_send()
        acc = buf[0]
        for i in range(1, N_DEV):
            acc = acc + buf[i]
        out_ref[...] = acc

    vmem = pl.BlockSpec(memory_space=pltpu.VMEM)
    return pl.pallas_call(
        body, in_specs=[vmem], out_specs=vmem, out_shape=jax.ShapeDtypeStruct((rows, cols), F32),
        scratch_shapes=[pltpu.VMEM((N_DEV, rows, cols), F32), pltpu.SemaphoreType.DMA((N_DEV - 1,)),
                        pltpu.SemaphoreType.DMA((N_DEV - 1,))],
        name=name, compiler_params=pltpu.CompilerParams(),
    )(v)


def _reduce_scatter_finish(started, after, layer, n_layers, prev, tag):
    send_sems, recv_sems, g, land, _ = started
    g, land = _scatter_wait(send_sems, recv_sems, g, land, after, f"rs_wait_{tag}")
    f = _device_sum(land, g, layer, n_layers, prev, f"rs_sum_{tag}")
    return _join_halves(f, layer, f"rs_join_{tag}")


def _split_w_in(wg, Hf, name):
    ns, D, cols = wg.shape
    a = 3 * Hf * HEAD_DIM
    n6 = ns * cols - Hf
    br = _row_block(D)

    def body(w_ref, w6_ref, wf_ref):
        nat = jnp.concatenate([w_ref[s] for s in range(ns)], axis=1)
        w6_ref[...] = jnp.concatenate([nat[:, :a], nat[:, a + Hf:]], axis=1)
        wf_ref[...] = nat[:, a:a + Hf]

    w6, wf = pl.pallas_call(
        body, grid=(D // br,), in_specs=[pl.BlockSpec((ns, br, cols), lambda i: (0, i, 0))],
        out_specs=[pl.BlockSpec((br, n6), lambda i: (i, 0)), pl.BlockSpec((br, Hf), lambda i: (i, 0))],
        out_shape=[jax.ShapeDtypeStruct((D, n6), wg.dtype), jax.ShapeDtypeStruct((D, Hf), wg.dtype)],
        name=name, compiler_params=_cparams(("parallel",)),
    )(wg)
    return w6, wf.T


def _join_dw_in(dw6, dwf_t, Hf, name):
    D, n6 = dw6.shape
    a = 3 * Hf * HEAD_DIM
    cols = (n6 + Hf) // N_CHIPS
    br = _row_block(D)

    def body(w6_ref, wf_ref, out_ref):
        w6 = w6_ref[...]
        nat = jnp.concatenate([w6[:, :a], wf_ref[...], w6[:, a:]], axis=1)
        for s in range(N_CHIPS):
            out_ref[s] = nat[:, s * cols:(s + 1) * cols]

    return pl.pallas_call(
        body, grid=(D // br,),
        in_specs=[pl.BlockSpec((br, n6), lambda i: (i, 0)), pl.BlockSpec((br, Hf), lambda i: (i, 0))],
        out_specs=pl.BlockSpec((N_CHIPS, br, cols), lambda i: (0, i, 0)),
        out_shape=jax.ShapeDtypeStruct((N_CHIPS, D, cols), dw6.dtype),
        name=name, compiler_params=_cparams(("parallel",)),
    )(dw6, dwf_t.T.astype(dw6.dtype))


def _tied(v, tokens, name):
    return _tie(v, tokens, name) if tokens else v


def _layer_fwd(x, p, weight, bias, tokens, tag):
    Hf, Hd = p["forget_b"].shape[0], bias.shape[1]
    h1 = _rms_fwd(x, _tied(p["norm1_g"], tokens, f"tie_norm1_{tag}"), f"norm1_{tag}")
    w6, wf_t = _split_w_in(weight("w_in", h1), Hf, f"split_w_in_{tag}")
    n_a = 3 * Hf * HEAD_DIM
    proj_a = _mm_nn(h1, w6, f"proj_a_{tag}", [CDT], epi=lambda acc: (acc,), b_cols=(0, n_a))[0]
    proj_b = _mm_nn(h1, w6, f"proj_b_{tag}", [F32], b_cols=(n_a, w6.shape[1] - n_a))[0]
    f_t = _mm_nt(wf_t, h1, f"fproj_{tag}", [F32])[0]
    qc, kc = _fox_bias_operands(_gates_fwd(f_t, p["forget_b"], f"gates_{tag}"))
    y_a, lse_a = _fox_fwd(proj_a, qc, kc, Hf, f"fox_{tag}")
    y_b, lse_b = _dil_fwd(proj_b, bias, Hd, f"dil_{tag}")
    mixed = _pair_norm_fwd(y_a, y_b, p["outnorm_a_g"], p["outnorm_b_g"], f"norm_ab_{tag}")
    w_out = weight("w_out", mixed)
    w_out = w_out.reshape(-1, w_out.shape[2])
    x1 = _mm_nn(mixed, w_out, f"attn_out_{tag}", [F32], extras=[x])[0]
    h2 = _rms_fwd(x1, p["norm2_g"], f"norm2_{tag}")
    w_mi = weight("w_mlp_in", h2)
    u, act = _mm_nn(h2, w_mi, f"mlp_in_{tag}", [CDT, CDT], b_slots=True,
                    epi=lambda acc: (acc, jnp.square(jnp.maximum(acc, 0.0))))
    w_mo = weight("w_mlp_out", act)
    w_mo = w_mo.reshape(-1, w_mo.shape[2])
    x2 = _mm_nn(act, w_mo, f"mlp_out_{tag}", [F32], extras=[x1])[0]
    saved = dict(x=x, h1=h1, proj_a=proj_a, proj_b=proj_b, f_t=f_t, qc=qc, kc=kc, y_a=y_a, lse_a=lse_a, y_b=y_b,
                 lse_b=lse_b, mixed=mixed, x1=x1, h2=h2, u=u, act=act, w6=w6, wf_t=wf_t, w_out=w_out, w_mi=w_mi,
                 w_mo=w_mo)
    return x2, saved


def _layer_bwd(dx2, dx2c, p, send, bias, sv, defer_w_out, tag):
    Hf, Hd = p["forget_b"].shape[0], bias.shape[1]
    E = HEAD_DIM
    rows = lambda g: g.reshape(N_CHIPS, -1, g.shape[1])
    du = _mm_nt(dx2c, sv["w_mo"], f"d_act_{tag}", [CDT], extras=[sv["u"]],
                epi=lambda acc, u: (acc * (2.0 * jnp.maximum(u.astype(F32), 0.0)),))[0]
    tokens = send("w_mlp_out", rows(_mm_tn(sv["act"], dx2c, f"dw_mlp_out_{tag}", CDT)))
    dh2 = _mm_nt(du, sv["w_mi"], f"d_h2_{tag}", [F32], b_slots=True)[0]
    tokens = tokens + send("w_mlp_in", _mm_tn(sv["h2"], du, f"dw_mlp_in_{tag}", CDT, out_slots=N_CHIPS))
    dx1, dx1c, g_norm2 = _rms_bwd(sv["x1"], _tied(p["norm2_g"], tokens, f"tie_norm2_{tag}"), dh2, dx2,
                                  f"d_norm2_{tag}")
    dmixed = _mm_nt(dx1c, sv["w_out"], f"d_mixed_{tag}", [F32])[0]
    send_w_out = lambda: send("w_out", rows(_mm_tn(sv["mixed"], dx1c, f"dw_out_{tag}", CDT)))
    tokens = [] if defer_w_out else send_w_out()
    dy_a, dy_b, g_na, g_nb = _pair_norm_bwd(sv["y_a"], sv["y_b"], _tied(p["outnorm_a_g"], tokens, f"tie_norm_a_{tag}"),
                                            p["outnorm_b_g"], dmixed, f"d_norm_ab_{tag}")
    dq_a, dcq, dk_a, dv_a, dck = _fox_bwd(sv["proj_a"], sv["qc"], sv["kc"], sv["lse_a"], sv["y_a"], dy_a, Hf,
                                          f"fox_bwd_{tag}")
    df, dfc, g_fb = _gates_bwd(sv["f_t"], p["forget_b"], dcq[:, ::E].T, dck.reshape(Hf, -1), f"d_gates_{tag}")
    dq_b, dk_b, dv_b, dbias = _dil_bwd(sv["proj_b"], bias, sv["y_b"], dy_b, sv["lse_b"], Hd, f"dil_bwd_{tag}")
    dproj = jnp.concatenate([dq_a, dk_a, dv_a, dq_b, dk_b, dv_b], axis=1)
    g_w6 = _mm_tn(sv["h1"], dproj, f"dw_in_{tag}", CDT)
    g_wf_t = _mm_nn(dfc, sv["h1"], f"dw_f_{tag}", [F32])[0]
    tokens = send("w_in", _join_dw_in(g_w6, g_wf_t, Hf, f"join_dw_in_{tag}"))
    dh1_f = _mm_tn(dfc, _tied(sv["wf_t"], tokens, f"tie_wf_{tag}"), f"d_h1_f_{tag}", F32)
    dh1 = _mm_nt(dproj, sv["w6"], f"d_h1_{tag}", [F32], extras=[dh1_f])[0]
    dx, dxc, g_norm1 = _rms_bwd(sv["x"], p["norm1_g"], dh1, dx1, f"d_norm1_{tag}")
    grads = dict(norm1_g=g_norm1[0], norm2_g=g_norm2[0], outnorm_a_g=g_na[0], outnorm_b_g=g_nb[0],
                 forget_b=g_fb[:, 0], dbias=dbias)
    return dx, dxc, grads, (send_w_out if defer_w_out else None)


_LAYER_SMALL = ("norm1_g", "forget_b", "outnorm_a_g", "outnorm_b_g", "norm2_g")


def _local_step(x, target, small, weight, send, tokens):
    depth = small["norm1_g"].shape[0]
    buckets = _bucket_table()
    bias = _bias_table(small["rel_bias"], buckets, "bias_table")
    layers, saved = [], []
    for l in range(depth):
        p = {k: small[k][l] for k in _LAYER_SMALL}
        layers.append(p)
        x, sv = _layer_fwd(x, p, functools.partial(weight, l), bias, tokens if l == 0 else [], f"l{l}")
        saved.append(sv)
    dx, dxc, g_final, loss = _loss_bwd(x, small["final_norm_g"], target, "loss")
    layer_grads = [None] * depth
    for l in reversed(range(depth)):
        dx, dxc, layer_grads[l], last = _layer_bwd(dx, dxc, layers[l], functools.partial(send, l), bias, saved[l],
                                                   l == 0, f"l{l}")
    last()
    dbias = functools.reduce(jnp.add, [g["dbias"] for g in layer_grads])
    g_rel = _bias_table_bwd(dbias, buckets, "d_bias_table")[:, 0, :].T
    small_grads = dict(final_norm_g=g_final[0], rel_bias=g_rel,
                       **{k: jnp.stack([g[k] for g in layer_grads]) for k in _LAYER_SMALL})
    return loss[0, 0], dx, small_grads


_BIG = ("w_in", "w_out", "w_mlp_in", "w_mlp_out")
_SMALL = ("norm1_g", "forget_b", "rel_bias", "outnorm_a_g", "outnorm_b_g", "norm2_g", "final_norm_g")
_ORDER = ("norm1_g", "w_in", "forget_b", "rel_bias", "outnorm_a_g", "outnorm_b_g", "w_out", "norm2_g", "w_mlp_in",
          "w_mlp_out", "final_norm_g")


def _pack_small(d):
    flat = jnp.concatenate([d[k].reshape(-1) for k in _SMALL])
    rows = -(-flat.shape[0] // (8 * SMALL_COLS)) * 8
    return jnp.pad(flat, (0, rows * SMALL_COLS - flat.shape[0])).reshape(rows, SMALL_COLS)


def _unpack_small(packed, like):
    flat, out, at = packed.reshape(-1), {}, 0
    for k in _SMALL:
        n = like[k].size
        out[k] = flat[at:at + n].reshape(like[k].shape)
        at += n
    return out


def kernel(x, norm1_g, w_in, forget_b, rel_bias, outnorm_a_g, outnorm_b_g, w_out, norm2_g, w_mlp_in, w_mlp_out, final_norm_g, loss_target, m_norm1_g, m_w_in, m_forget_b, m_rel_bias, m_outnorm_a_g, m_outnorm_b_g, m_w_out, m_norm2_g, m_w_mlp_in, m_w_mlp_out, m_final_norm_g, v_norm1_g, v_w_in, v_forget_b, v_rel_bias, v_outnorm_a_g, v_outnorm_b_g, v_w_out, v_norm2_g, v_w_mlp_in, v_w_mlp_out, v_final_norm_g):
    w = dict(norm1_g=norm1_g, w_in=w_in, forget_b=forget_b, rel_bias=rel_bias, outnorm_a_g=outnorm_a_g,
             outnorm_b_g=outnorm_b_g, w_out=w_out, norm2_g=norm2_g, w_mlp_in=w_mlp_in, w_mlp_out=w_mlp_out,
             final_norm_g=final_norm_g)
    m = dict(norm1_g=m_norm1_g, w_in=m_w_in, forget_b=m_forget_b, rel_bias=m_rel_bias, outnorm_a_g=m_outnorm_a_g,
             outnorm_b_g=m_outnorm_b_g, w_out=m_w_out, norm2_g=m_norm2_g, w_mlp_in=m_w_mlp_in,
             w_mlp_out=m_w_mlp_out, final_norm_g=m_final_norm_g)
    v = dict(norm1_g=v_norm1_g, w_in=v_w_in, forget_b=v_forget_b, rel_bias=v_rel_bias, outnorm_a_g=v_outnorm_a_g,
             outnorm_b_g=v_outnorm_b_g, w_out=v_w_out, norm2_g=v_norm2_g, w_mlp_in=v_w_mlp_in,
             w_mlp_out=v_w_mlp_out, final_norm_g=v_final_norm_g)
    depth = w_in.shape[0]
    small = {k: w[k] for k in _SMALL}

    gathers, tokens = {}, []
    for l in range(depth):
        for k in _BIG:
            buf = _cast_into_slot(w[k], l, f"cast_{k}_l{l}")
            send_sems, recv_sems, buf, token = _gather_start(buf, tokens, f"gather_start_{k}_l{l}")
            gathers[l, k], tokens = (send_sems, recv_sems, buf), [token]

    def weight(l, k, after):
        return _gather_wait(*gathers[l, k], after, f"gather_wait_{k}_l{l}")

    scatters = {}

    def send(l, k, g):
        scatters[l, k] = _scatter_start(g, f"rs_start_{k}_l{l}")
        return [scatters[l, k][4]]

    loss, grad_x, small_grads = _local_step(x[0], loss_target[0], small, weight, send, tokens)
    loss = lax.psum(loss, ("x", "y", "c"))
    small_sums = _all_reduce_small(_pack_small(small_grads), "small_all_reduce")

    grads, delta, new_m, new_v = {}, {}, {}, {}
    after, seen = small_sums, {k: 0 for k in _BIG}
    for (l, k), started in scatters.items():
        grads[k] = _reduce_scatter_finish(started, after, l, depth, grads.get(k), f"{k}_l{l}")
        seen[k] += 1
        if seen[k] == depth:
            shape = w[k].shape
            flat = lambda t: t.reshape(-1, shape[-1])
            g_, d_, m_, v_ = _adamw(flat(w[k]), flat(grads[k]), flat(m[k]), flat(v[k]), f"adamw_{k}")
            grads[k], delta[k], new_m[k], new_v[k] = (t.reshape(shape) for t in (g_, d_, m_, v_))
            after = d_
    grads.update(_unpack_small(small_sums, small))
    _, d_, m_, v_ = _adamw(_pack_small(small), _pack_small({k: grads[k] for k in _SMALL}),
                           _pack_small({k: m[k] for k in _SMALL}), _pack_small({k: v[k] for k in _SMALL}), "adamw_small")
    delta.update(_unpack_small(d_, small))
    new_m.update(_unpack_small(m_, small))
    new_v.update(_unpack_small(v_, small))

    return (loss, grad_x[None], *[grads[k] for k in _ORDER], *[delta[k] for k in _ORDER],
            *[new_m[k] for k in _ORDER], *[new_v[k] for k in _ORDER])
```

```python
import functools

import jax
import jax.numpy as jnp
from jax import lax
from jax.experimental import pallas as pl
from jax.experimental.pallas import tpu as pltpu

F32 = jnp.float32
CDT = jnp.bfloat16
HEAD_DIM = 128
NORM_EPS = 1e-6
NEG_INF = -1e30
LOG2E = 1.4426950408889634
REL_BUCKETS = 32
REL_MAX_DISTANCE = 2048
DIL_PATTERNS = ((128, 1), (512, 4), (2048, 16))
DIL_BLOCK = 128
ADAM_LR, ADAM_B1, ADAM_B2, ADAM_EPS, ADAM_WD, ADAM_STEP = 0.001, 0.9, 0.999, 1e-08, 0.01, 10
N_CHIPS = 4
N_DEV = 8
VMEM_LIMIT_BYTES = 56 * 1024 * 1024
SMALL_COLS = 1024
MESH = pl.DeviceIdType.MESH


def _cparams(sem=None):
    return pltpu.CompilerParams(dimension_semantics=sem, vmem_limit_bytes=VMEM_LIMIT_BYTES)


def _tile(dim, pref):
    t = min(pref, dim)
    t -= t % 128
    while t >= 128:
        if dim % t == 0:
            return t
        t -= 128
    return dim


def _rowwise(fn, ins, out_dtypes, name, bs=256, consts=()):
    R, C = ins[0].shape
    bs = min(bs, R)
    n_in, n_c = len(ins), len(consts)

    def body(*refs):
        vals = [r[...] for r in refs[:n_in + n_c]]
        res = fn(*vals)
        for o, r in zip(refs[n_in + n_c:], res):
            o[...] = r.astype(o.dtype)

    row = pl.BlockSpec((bs, C), lambda i: (i, 0))
    return pl.pallas_call(
        body, grid=(R // bs,),
        in_specs=[row] * n_in + [pl.BlockSpec((1, c.shape[-1]), lambda i: (0, 0)) for c in consts],
        out_specs=[row] * len(out_dtypes),
        out_shape=[jax.ShapeDtypeStruct((R, C), d) for d in out_dtypes],
        name=name, compiler_params=_cparams(("parallel",)),
    )(*ins, *[c.reshape(1, -1) for c in consts])


def _rms_fwd(x, g, name):
    def fn(xf, gg):
        r = lax.rsqrt(jnp.mean(xf * xf, axis=-1, keepdims=True) + NORM_EPS)
        return ((xf * r) * gg,)
    return _rowwise(fn, [x], [CDT], name, consts=[g])[0]


def _rms_bwd(x, g, dh, dres, name, bs=256):
    S, D = x.shape
    bs = min(bs, S)
    has_res = dres is not None

    def body(*refs):
        x_ref, g_ref, dh_ref = refs[:3]
        dx_ref, dxc_ref, dg_ref = refs[-3:]
        xf = x_ref[...]
        r = lax.rsqrt(jnp.mean(xf * xf, axis=-1, keepdims=True) + NORM_EPS)
        xhat = xf * r
        dh_ = dh_ref[...].astype(F32)
        dxhat = dh_ * g_ref[...]
        dx = r * (dxhat - xhat * jnp.mean(dxhat * xhat, axis=-1, keepdims=True))
        if has_res:
            dx = dx + refs[3][...]
        dx_ref[...] = dx
        dxc_ref[...] = dx.astype(dxc_ref.dtype)
        part = jnp.sum(dh_ * xhat, axis=0, keepdims=True)

        @pl.when(pl.program_id(0) == 0)
        def _():
            dg_ref[...] = part

        @pl.when(pl.program_id(0) > 0)
        def _():
            dg_ref[...] += part

    row = pl.BlockSpec((bs, D), lambda i: (i, 0))
    one = pl.BlockSpec((1, D), lambda i: (0, 0))
    ins = [x, g.reshape(1, D), dh] + ([dres] if has_res else [])
    return pl.pallas_call(
        body, grid=(S // bs,),
        in_specs=[row, one, row] + ([row] if has_res else []),
        out_specs=[row, row, one],
        out_shape=[jax.ShapeDtypeStruct((S, D), F32), jax.ShapeDtypeStruct((S, D), CDT),
                   jax.ShapeDtypeStruct((1, D), F32)],
        name=name, compiler_params=_cparams(("arbitrary",)),
    )(*ins)


def _pair_norm_fwd(y_a, y_b, g_a, g_b, name, bs=256):
    S, Da = y_a.shape
    Db = y_b.shape[1]
    bs = min(bs, S)

    def body(a_ref, b_ref, ga_ref, gb_ref, o_ref):
        def norm(x, g):
            r = lax.rsqrt(jnp.mean(x * x, axis=-1, keepdims=True) + NORM_EPS)
            return ((x * r) * g).astype(o_ref.dtype)
        o_ref[:, :Da] = norm(a_ref[...], ga_ref[...])
        o_ref[:, Da:] = norm(b_ref[...], gb_ref[...])

    row = lambda n: pl.BlockSpec((bs, n), lambda i: (i, 0))
    one = lambda n: pl.BlockSpec((1, n), lambda i: (0, 0))
    return pl.pallas_call(
        body, grid=(S // bs,), in_specs=[row(Da), row(Db), one(Da), one(Db)], out_specs=row(Da + Db),
        out_shape=jax.ShapeDtypeStruct((S, Da + Db), CDT), name=name, compiler_params=_cparams(("parallel",)),
    )(y_a, y_b, g_a.reshape(1, Da), g_b.reshape(1, Db))


def _pair_norm_bwd(y_a, y_b, g_a, g_b, dmixed, name, bs=256):
    S, Da = y_a.shape
    Db = y_b.shape[1]
    bs = min(bs, S)

    def body(a_ref, b_ref, ga_ref, gb_ref, dm_ref, da_ref, db_ref, dga_ref, dgb_ref):
        def one(x_ref, g_ref, dh, dx_ref, dg_ref):
            xf = x_ref[...]
            r = lax.rsqrt(jnp.mean(xf * xf, axis=-1, keepdims=True) + NORM_EPS)
            xhat = xf * r
            dxhat = dh * g_ref[...]
            dx_ref[...] = r * (dxhat - xhat * jnp.mean(dxhat * xhat, axis=-1, keepdims=True))
            part = jnp.sum(dh * xhat, axis=0, keepdims=True)

            @pl.when(pl.program_id(0) == 0)
            def _():
                dg_ref[...] = part

            @pl.when(pl.program_id(0) > 0)
            def _():
                dg_ref[...] += part

        dm = dm_ref[...]
        one(a_ref, ga_ref, dm[:, :Da], da_ref, dga_ref)
        one(b_ref, gb_ref, dm[:, Da:], db_ref, dgb_ref)

    row = lambda n: pl.BlockSpec((bs, n), lambda i: (i, 0))
    one_ = lambda n: pl.BlockSpec((1, n), lambda i: (0, 0))
    return pl.pallas_call(
        body, grid=(S // bs,), in_specs=[row(Da), row(Db), one_(Da), one_(Db), row(Da + Db)],
        out_specs=[row(Da), row(Db), one_(Da), one_(Db)],
        out_shape=[jax.ShapeDtypeStruct((S, Da), F32), jax.ShapeDtypeStruct((S, Db), F32),
                   jax.ShapeDtypeStruct((1, Da), F32), jax.ShapeDtypeStruct((1, Db), F32)],
        name=name, compiler_params=_cparams(("arbitrary",)),
    )(y_a, y_b, g_a.reshape(1, Da), g_b.reshape(1, Db), dmixed)


def _loss_bwd(x, g, target, name, bs=256):
    S, D = x.shape
    bs = min(bs, S)

    def body(x_ref, g_ref, t_ref, dx_ref, dxc_ref, dg_ref, loss_ref):
        xf = x_ref[...]
        r = lax.rsqrt(jnp.mean(xf * xf, axis=-1, keepdims=True) + NORM_EPS)
        xhat = xf * r
        err = xhat * g_ref[...] - t_ref[...]
        lpart = 0.5 * jnp.sum(jnp.mean(err * err, axis=-1, keepdims=True), axis=0, keepdims=True)
        dy = err / D
        dxhat = dy * g_ref[...]
        dx = r * (dxhat - xhat * jnp.mean(dxhat * xhat, axis=-1, keepdims=True))
        dx_ref[...] = dx
        dxc_ref[...] = dx.astype(dxc_ref.dtype)
        gpart = jnp.sum(dy * xhat, axis=0, keepdims=True)

        @pl.when(pl.program_id(0) == 0)
        def _():
            dg_ref[...] = gpart
            loss_ref[...] = lpart

        @pl.when(pl.program_id(0) > 0)
        def _():
            dg_ref[...] += gpart
            loss_ref[...] += lpart

    row = pl.BlockSpec((bs, D), lambda i: (i, 0))
    one = pl.BlockSpec((1, D), lambda i: (0, 0))
    return pl.pallas_call(
        body, grid=(S // bs,),
        in_specs=[row, one, row],
        out_specs=[row, row, one, pl.BlockSpec((1, 1), lambda i: (0, 0))],
        out_shape=[jax.ShapeDtypeStruct((S, D), F32), jax.ShapeDtypeStruct((S, D), CDT),
                   jax.ShapeDtypeStruct((1, D), F32), jax.ShapeDtypeStruct((1, 1), F32)],
        name=name, compiler_params=_cparams(("arbitrary",)),
    )(x, g.reshape(1, D), target)


_NN = (((1,), (0,)), ((), ()))
_NT = (((1,), (1,)), ((), ()))
_TN = (((0,), (0,)), ((), ()))


def _mm(a, b, *, M, N, K, a_spec, b_spec, o_spec, dims, tm, tn, tk, name, out_shapes, extras=(), epi=None):
    nk = K // tk
    n_ex, n_out = len(extras), len(out_shapes)
    in_place = epi is None
    if in_place:
        assert n_out == 1 and n_ex <= 1 and out_shapes[0].dtype == F32
        epi = lambda acc, *r: (acc + r[0] if r else acc,)

    def body(*refs):
        a_ref, b_ref = refs[0], refs[1]
        ex = refs[2:2 + n_ex]
        outs = refs[2 + n_ex:2 + n_ex + n_out]
        part = lax.dot_general(a_ref[...], b_ref[...], dims, preferred_element_type=F32)

        def finish(acc):
            for o, r in zip(outs, epi(acc, *[e[...] for e in ex])):
                o[...] = r.astype(o.dtype)

        if nk == 1:
            finish(part)
        elif in_place:
            k = pl.program_id(2)

            @pl.when(k == 0)
            def _():
                finish(part)

            @pl.when(k > 0)
            def _():
                outs[0][...] += part
        else:
            acc_ref = refs[-1]
            k = pl.program_id(2)

            @pl.when(k == 0)
            def _():
                acc_ref[...] = part

            @pl.when(k > 0)
            def _():
                acc_ref[...] += part

            @pl.when(k == nk - 1)
            def _():
                finish(acc_ref[...])

    ex_spec = pl.BlockSpec((tm, tn), lambda i, j, k: (i, j))
    return pl.pallas_call(
        body, grid=(M // tm, N // tn, nk),
        in_specs=[a_spec, b_spec] + [ex_spec] * n_ex,
        out_specs=[o_spec] * n_out,
        out_shape=out_shapes,
        scratch_shapes=[pltpu.VMEM((tm, tn), F32)] if nk > 1 and not in_place else [],
        name=name, compiler_params=_cparams(("parallel", "parallel", "arbitrary")),
    )(a, b, *extras)


def _mm_tiles(K):
    return (2048, 512, 2048) if K <= 2048 else (1024, 1024, 2048)


def _mm_nn(a, b, name, out_dtypes, extras=(), epi=None, b_slots=False, b_cols=None):
    M, K = a.shape
    tm, tn, tk = _mm_tiles(K)
    if b_slots:
        ns, _, Ns = b.shape
        N = ns * Ns
        tn = _tile(Ns, tn)
        npb = Ns // tn
        tk_ = _tile(K, tk)
        b_spec = pl.BlockSpec((None, tk_, tn), lambda i, j, k: (j // npb, k, j % npb))
    else:
        first, N = b_cols if b_cols is not None else (0, b.shape[1])
        tn = _tile(N, tn)
        assert first % tn == 0
        tk_ = _tile(K, tk)
        b_spec = pl.BlockSpec((tk_, tn), lambda i, j, k: (k, first // tn + j))
    tm = _tile(M, tm)
    return _mm(a, b, M=M, N=N, K=K, a_spec=pl.BlockSpec((tm, tk_), lambda i, j, k: (i, k)), b_spec=b_spec,
               o_spec=pl.BlockSpec((tm, tn), lambda i, j, k: (i, j)), dims=_NN, tm=tm, tn=tn, tk=tk_, name=name,
               out_shapes=[jax.ShapeDtypeStruct((M, N), d) for d in out_dtypes], extras=extras, epi=epi)


def _mm_nt(a, b, name, out_dtypes, extras=(), epi=None, b_slots=False):
    M, K = a.shape
    tm, tn, tk = _mm_tiles(K)
    tm = _tile(M, tm)
    if b_slots:
        ns, N, Ks = b.shape
        tk_ = _tile(Ks, tk)
        kpb = Ks // tk_
        tn = _tile(N, tn)
        b_spec = pl.BlockSpec((None, tn, tk_), lambda i, j, k: (k // kpb, j, k % kpb))
    else:
        N = b.shape[0]
        tk_ = _tile(K, tk)
        tn = _tile(N, tn)
        b_spec = pl.BlockSpec((tn, tk_), lambda i, j, k: (j, k))
    return _mm(a, b, M=M, N=N, K=K, a_spec=pl.BlockSpec((tm, tk_), lambda i, j, k: (i, k)), b_spec=b_spec,
               o_spec=pl.BlockSpec((tm, tn), lambda i, j, k: (i, j)), dims=_NT, tm=tm, tn=tn, tk=tk_, name=name,
               out_shapes=[jax.ShapeDtypeStruct((M, N), d) for d in out_dtypes], extras=extras, epi=epi)


def _mm_tn(a, b, name, out_dtype, out_slots=0, tm=2048, tn=1024, tk=2048):
    K, M = a.shape
    N = b.shape[1]
    tm, tk_ = _tile(M, tm), _tile(K, tk)
    if out_slots:
        Ns = N // out_slots
        tn = _tile(Ns, tn)
        npb = Ns // tn
        o_spec = pl.BlockSpec((None, tm, tn), lambda i, j, k: (j // npb, i, j % npb))
        out_shape = jax.ShapeDtypeStruct((out_slots, M, Ns), out_dtype)
    else:
        tn = _tile(N, tn)
        o_spec = pl.BlockSpec((tm, tn), lambda i, j, k: (i, j))
        out_shape = jax.ShapeDtypeStruct((M, N), out_dtype)
    return _mm(a, b, M=M, N=N, K=K, a_spec=pl.BlockSpec((tk_, tm), lambda i, j, k: (k, i)),
               b_spec=pl.BlockSpec((tk_, tn), lambda i, j, k: (k, j)), o_spec=o_spec, dims=_TN,
               tm=tm, tn=tn, tk=tk_, name=name, out_shapes=[out_shape],
               epi=None if out_dtype == F32 else (lambda acc: (acc,)))[0]


GATE_BLOCK = 512


def _split3(v):
    hi = v.astype(jnp.bfloat16)
    r1 = v - hi.astype(F32)
    mid = r1.astype(jnp.bfloat16)
    lo = (r1 - mid.astype(F32)).astype(jnp.bfloat16)
    return hi, mid, lo


def _exact_dot(v, tri):
    return functools.reduce(jnp.add, [jnp.dot(t, tri, preferred_element_type=F32) for t in _split3(v)])


def _gates_fwd(f_t, b, name):
    H, S = f_t.shape
    nb = _tile(S, GATE_BLOCK)
    inv_scale = HEAD_DIM ** 0.5

    def body(f_ref, b_ref, c_ref):
        upper = (lax.broadcasted_iota(jnp.int32, (nb, nb), 0)
                 <= lax.broadcasted_iota(jnp.int32, (nb, nb), 1)).astype(jnp.bfloat16)
        carry = jnp.zeros((H, 1), F32)
        for i in range(S // nb):
            z = f_ref[:, i * nb:(i + 1) * nb] + b_ref[...]
            logf = jnp.minimum(z, 0.0) - jnp.log1p(jnp.exp(-jnp.abs(z)))
            cs = _exact_dot(logf, upper) + carry
            for j, t in enumerate(_split3(cs * inv_scale)):
                c_ref[j, :, i * nb:(i + 1) * nb] = t
            carry = cs[:, nb - 1:nb]

    return pl.pallas_call(body, out_shape=jax.ShapeDtypeStruct((3, H, S), jnp.bfloat16), name=name,
                          compiler_params=_cparams())(f_t, b.reshape(H, 1))


def _gates_bwd(f_t, b, dcq, dck, name):
    H, S = f_t.shape
    nb = _tile(S, GATE_BLOCK)

    def body(f_ref, b_ref, dcq_ref, dck_ref, df_ref, dfc_ref, db_ref):
        lower = (lax.broadcasted_iota(jnp.int32, (nb, nb), 0)
                 >= lax.broadcasted_iota(jnp.int32, (nb, nb), 1)).astype(jnp.bfloat16)
        carry = jnp.zeros((H, 1), F32)
        db = jnp.zeros((H, 1), F32)
        for i in reversed(range(S // nb)):
            sl = slice(i * nb, (i + 1) * nb)
            dc = dcq_ref[:, sl] - dck_ref[:, sl]
            dlogf = _exact_dot(dc, lower) + carry
            carry = dlogf[:, 0:1]
            z = f_ref[:, sl] + b_ref[...]
            df = dlogf / (1.0 + jnp.exp(z))
            df_ref[:, sl] = df
            dfc_ref[:, sl] = df.astype(dfc_ref.dtype)
            db = db + jnp.sum(df, axis=1, keepdims=True)
        db_ref[...] = db

    return pl.pallas_call(
        body, out_shape=[jax.ShapeDtypeStruct((H, S), F32), jax.ShapeDtypeStruct((H, S), CDT),
                         jax.ShapeDtypeStruct((H, 1), F32)],
        name=name, compiler_params=_cparams())(f_t, b.reshape(H, 1), dcq, dck)


FOX_BLOCK = 1024


def _fox_bias_operands(csplit, name, bs=512):
    _, H, S = csplit.shape
    E = HEAD_DIM
    bs = _tile(S, bs)
    part = jnp.arange(3 * H)[:, None] // H
    head = jnp.arange(3 * H)[:, None] % H
    lane = jnp.arange(H * E)[None, :]
    place_q = (lane == head * E + part).astype(csplit.dtype)
    place_k = -(lane == head * E + 3 + part).astype(csplit.dtype)
    ones_q = ((lane % E >= 3) & (lane % E < 6)).astype(F32)
    ones_k = (lane % E < 3).astype(F32)

    def body(c_ref, pq_ref, pk_ref, oq_ref, ok_ref, qc_ref, kc_ref):
        c = c_ref[...]
        qc_ref[...] = (lax.dot_general(c, pq_ref[...], _TN, preferred_element_type=F32) + oq_ref[...]).astype(qc_ref.dtype)
        kc_ref[...] = (lax.dot_general(c, pk_ref[...], _TN, preferred_element_type=F32) + ok_ref[...]).astype(kc_ref.dtype)

    full = lambda a: pl.BlockSpec(a.shape, lambda i: (0, 0))
    out = pl.BlockSpec((bs, H * E), lambda i: (i, 0))
    return pl.pallas_call(
        body, grid=(S // bs,),
        in_specs=[pl.BlockSpec((3 * H, bs), lambda i: (0, i)), full(place_q), full(place_k), full(ones_q), full(ones_k)],
        out_specs=[out, out], out_shape=[jax.ShapeDtypeStruct((S, H * E), csplit.dtype)] * 2,
        name=name, compiler_params=_cparams(("parallel",)),
    )(csplit.reshape(3 * H, S), place_q, place_k, ones_q, ones_k)


def _fox_logits2(q_ref, qc_ref, k_ref, kc_ref, diag):
    q, k = q_ref[...], k_ref[...]
    qa = jnp.concatenate([q, qc_ref[...].astype(q.dtype)], axis=1)
    ka = jnp.concatenate([k, kc_ref[...].astype(k.dtype)], axis=1)
    s = lax.dot_general(qa, ka, _NT, preferred_element_type=F32) * (HEAD_DIM ** -0.5 * LOG2E)
    if diag:
        row = lax.broadcasted_iota(jnp.int32, s.shape, 0)
        col = lax.broadcasted_iota(jnp.int32, s.shape, 1)
        s = jnp.where(col <= row, s, NEG_INF)
    return s


def _fox_fwd(proj, qc, kc, H, name):
    S = proj.shape[0]
    E = HEAD_DIM
    blk = _tile(S, FOX_BLOCK)
    nq = S // blk

    def body(q_ref, qc_ref, k_ref, kc_ref, v_ref, o_ref, lse_ref, m_s, l_s, acc_s):
        qi, kj = pl.program_id(1), pl.program_id(2)

        @pl.when(kj == 0)
        def _():
            m_s[...] = jnp.full(m_s.shape, NEG_INF, F32)
            l_s[...] = jnp.zeros(l_s.shape, F32)
            acc_s[...] = jnp.zeros(acc_s.shape, F32)

        def step(diag):
            s = _fox_logits2(q_ref, qc_ref, k_ref, kc_ref, diag)
            m_prev = m_s[...]
            m_new = jnp.maximum(m_prev, jnp.max(s, axis=-1, keepdims=True))
            alpha = jnp.exp2(m_prev - m_new)
            p = jnp.exp2(s - m_new)
            l_s[...] = alpha * l_s[...] + jnp.sum(p, axis=-1, keepdims=True)
            acc_s[...] = alpha * acc_s[...] + jnp.dot(p.astype(CDT), v_ref[...], preferred_element_type=F32)
            m_s[...] = m_new

        pl.when(kj < qi)(lambda: step(False))
        pl.when(kj == qi)(lambda: step(True))

        @pl.when(kj == nq - 1)
        def _():
            o_ref[...] = acc_s[...] / l_s[...]
            lse_ref[...] = jnp.broadcast_to(m_s[...] + jnp.log2(l_s[...]), lse_ref.shape)

    qspec = lambda off: pl.BlockSpec((blk, E), lambda h, i, j: (i, off + h))
    kspec = lambda off: pl.BlockSpec((blk, E), lambda h, i, j: (jnp.minimum(j, i), off + h))
    return pl.pallas_call(
        body, grid=(H, nq, nq),
        in_specs=[qspec(0), qspec(0), kspec(H), kspec(0), kspec(2 * H)],
        out_specs=[qspec(0)] * 2,
        out_shape=[jax.ShapeDtypeStruct((S, H * E), F32)] * 2,
        scratch_shapes=[pltpu.VMEM((blk, 1), F32), pltpu.VMEM((blk, 1), F32), pltpu.VMEM((blk, E), F32)],
        name=name, compiler_params=_cparams(("parallel", "parallel", "arbitrary")),
    )(proj, qc, proj, kc, proj)


def _fox_bwd(proj, qc, kc, lse, o, do, H, name):
    S = proj.shape[0]
    E = HEAD_DIM
    blk = _tile(S, FOX_BLOCK)
    nq = S // blk
    scale = E ** -0.5

    def body(q_ref, qc_ref, k_ref, kc_ref, v_ref, lse_ref, o_ref, do_ref,
             dq_ref, dcq_ref, dk_ref, dv_ref, dck_ref, dq_s, dcq_s, dk_s, dv_s, dck_s):
        kj, qi = pl.program_id(1), pl.program_id(2)

        @pl.when(qi == 0)
        def _():
            dk_s[...] = jnp.zeros(dk_s.shape, F32)
            dv_s[...] = jnp.zeros(dv_s.shape, F32)
            dck_s[...] = jnp.zeros(dck_s.shape, F32)

        def step(diag):
            do = do_ref[...]
            doc = do.astype(CDT)
            delta = jnp.sum(do * o_ref[...], axis=-1, keepdims=True)
            p = jnp.exp2(_fox_logits2(q_ref, qc_ref, k_ref, kc_ref, diag) - lse_ref[:, 0:1])
            dp = lax.dot_general(doc, v_ref[...], _NT, preferred_element_type=F32)
            ds = p * (dp - delta)
            dss = ds * scale
            dck_s[...] += jnp.sum(ds, axis=0, keepdims=True)
            dv_s[...] += jnp.dot(p.T.astype(CDT), doc, preferred_element_type=F32)
            dk_s[...] += jnp.dot(dss.T.astype(CDT), q_ref[...], preferred_element_type=F32)
            dq_part = jnp.dot(dss.astype(CDT), k_ref[...], preferred_element_type=F32)
            dc_part = jnp.sum(ds, axis=-1, keepdims=True)
            rows = pl.ds(pl.multiple_of(qi * blk, blk), blk)

            @pl.when(kj == 0)
            def _():
                dq_s[rows, :] = dq_part
                dcq_s[rows, :] = dc_part

            @pl.when(kj > 0)
            def _():
                dq_s[rows, :] += dq_part
                dcq_s[rows, :] += dc_part

        pl.when(qi > kj)(lambda: step(False))
        pl.when(qi == kj)(lambda: step(True))

        @pl.when(qi == nq - 1)
        def _():
            dk_ref[...] = dk_s[...].astype(dk_ref.dtype)
            dv_ref[...] = dv_s[...].astype(dv_ref.dtype)
            dck_ref[...] = dck_s[...].reshape(dck_ref.shape)

        @pl.when((qi == nq - 1) & (kj == nq - 1))
        def _():
            dq_ref[...] = dq_s[...].astype(dq_ref.dtype)
            dcq_ref[...] = jnp.broadcast_to(dcq_s[...], dcq_ref.shape)

    qspec = lambda off: pl.BlockSpec((blk, E), lambda h, j, i: (jnp.maximum(i, j), off + h))
    kspec = lambda off: pl.BlockSpec((blk, E), lambda h, j, i: (j, off + h))
    head = pl.BlockSpec((S, E), lambda h, j, i: (0, h))
    return pl.pallas_call(
        body, grid=(H, nq, nq),
        in_specs=[qspec(0), qspec(0), kspec(H), kspec(0), kspec(2 * H), qspec(0), qspec(0), qspec(0)],
        out_specs=[head, head, kspec(0), kspec(0), pl.BlockSpec((1, 1, blk), lambda h, j, i: (h, 0, j))],
        out_shape=[jax.ShapeDtypeStruct((S, H * E), CDT), jax.ShapeDtypeStruct((S, H * E), F32),
                   jax.ShapeDtypeStruct((S, H * E), CDT), jax.ShapeDtypeStruct((S, H * E), CDT),
                   jax.ShapeDtypeStruct((H, 1, S), F32)],
        scratch_shapes=[pltpu.VMEM((S, E), F32), pltpu.VMEM((S, 1), F32), pltpu.VMEM((blk, E), F32),
                        pltpu.VMEM((blk, E), F32), pltpu.VMEM((1, blk), F32)],
        name=name, compiler_params=_cparams(("parallel", "arbitrary", "arbitrary")),
    )(proj, qc, proj, kc, proj, lse, o, do)


DIL_SLAB = 16 * DIL_BLOCK
DIL_UNROLL = 8


def _rel_bucket(dist):
    max_exact = REL_BUCKETS // 2
    d = jnp.maximum(dist.astype(F32), 1.0)
    large = max_exact + (jnp.log(d / max_exact) / jnp.log(jnp.float32(REL_MAX_DISTANCE / max_exact))
                         * (REL_BUCKETS - max_exact)).astype(jnp.int32)
    large = jnp.minimum(large, REL_BUCKETS - 1)
    return jnp.where(dist < max_exact, dist, large)


def _bucket_table():
    i = jnp.arange(DIL_BLOCK)[:, None]
    j = jnp.arange(2 * DIL_BLOCK)[None, :]
    rel = DIL_BLOCK + i - j
    tabs = [_rel_bucket(jnp.clip(rel, 0, w // d) * d) for w, d in DIL_PATTERNS]
    return jnp.stack(tabs).astype(jnp.int32)


def _bias_table(rel_bias, buckets, name):
    P = buckets.shape[0]
    H = rel_bias.shape[1]

    def body(rb_ref, bk_ref, out_ref):
        h = pl.program_id(1)
        bk = bk_ref[0]
        val = jnp.zeros(bk.shape, F32)
        for b in range(REL_BUCKETS):
            val = jnp.where(bk == b, rb_ref[b, h], val)
        out_ref[0, 0] = val

    return pl.pallas_call(
        body, grid=(P, H),
        in_specs=[pl.BlockSpec(memory_space=pltpu.SMEM),
                  pl.BlockSpec((1, DIL_BLOCK, 2 * DIL_BLOCK), lambda p, h: (p, 0, 0))],
        out_specs=pl.BlockSpec((1, 1, DIL_BLOCK, 2 * DIL_BLOCK), lambda p, h: (p, h, 0, 0)),
        out_shape=jax.ShapeDtypeStruct((P, H, DIL_BLOCK, 2 * DIL_BLOCK), F32),
        name=name, compiler_params=_cparams(("parallel", "parallel")),
    )(rel_bias, buckets)


def _bias_table_bwd(dbias, buckets, name):
    P, H = dbias.shape[:2]

    def body(db_ref, bk_ref, out_ref):
        lane = lax.broadcasted_iota(jnp.int32, (1, REL_BUCKETS), 1)
        acc = jnp.zeros((1, REL_BUCKETS), F32)
        bk = bk_ref[...]
        db = db_ref[:, 0]
        for b in range(REL_BUCKETS):
            tot = jnp.sum(jnp.where(bk == b, db, 0.0))
            acc = jnp.where(lane == b, tot, acc)
        out_ref[0] = acc

    return pl.pallas_call(
        body, grid=(H,),
        in_specs=[pl.BlockSpec((P, 1, DIL_BLOCK, 2 * DIL_BLOCK), lambda h: (0, h, 0, 0)),
                  pl.BlockSpec((P, DIL_BLOCK, 2 * DIL_BLOCK), lambda h: (0, 0, 0))],
        out_specs=pl.BlockSpec((1, 1, REL_BUCKETS), lambda h: (h, 0, 0)),
        out_shape=jax.ShapeDtypeStruct((H, 1, REL_BUCKETS), F32),
        name=name, compiler_params=_cparams(("parallel",)),
    )(dbias, buckets)


def _bdot(a, b, contract_b):
    return lax.dot_general(a, b, (((2,), (contract_b,)), ((0,), (0,))), preferred_element_type=F32)


def _dil_units(first, d):
    units = []
    for t in range(DIL_UNROLL):
        u = first + t
        sg = u // d
        units.append((sg, sg * (DIL_BLOCK * d) + u % d))
    return units


def _dil_rows(ref, starts, d, dtype=None):
    t = jnp.stack([ref[pl.ds(s, DIL_BLOCK, stride=d), :] for s in starts])
    return t if dtype is None else t.astype(dtype)


def _dil_keys(ref, units, d):
    B, SL = DIL_BLOCK, DIL_SLAB
    return jnp.stack([jnp.concatenate([ref[pl.ds(SL + b - B * d, B, stride=d), :], ref[pl.ds(SL + b, B, stride=d), :]],
                                      axis=0) for _, b in units]).astype(CDT)


def _dil_logits(q, keys, bias_pc, first, d, has_before):
    T, B = q.shape[0], DIL_BLOCK
    ii = lax.broadcasted_iota(jnp.int32, (T, B, 2 * B), 1)
    jj = lax.broadcasted_iota(jnp.int32, (T, B, 2 * B), 2)
    sg = (first + lax.broadcasted_iota(jnp.int32, (T, B, 2 * B), 0)) // d
    mask = (jj >= ii) & (jj <= ii + B) & ((jj >= B) | (sg > 0) | has_before)
    return jnp.where(mask, _bdot(q, keys, 2) * HEAD_DIM ** -0.5 + bias_pc[None], NEG_INF)


def _dil_specs(H, n_slabs):
    E, SL = DIL_BLOCK, DIL_SLAB
    cur = lambda off: pl.BlockSpec((SL, E), lambda h, g: (g, off + h))
    prev = lambda off: pl.BlockSpec((SL, E), lambda h, g: (jnp.maximum(g - 1, 0), off + h))
    nxt = lambda off: pl.BlockSpec((SL, E), lambda h, g: (jnp.minimum(g + 1, n_slabs - 1), off + h))
    bias = pl.BlockSpec((len(DIL_PATTERNS), 1, E, 2 * E), lambda h, g: (0, h, 0, 0))
    return cur, prev, nxt, bias


def _dil_fwd(proj, bias, H, name):
    S = proj.shape[0]
    E = B = DIL_BLOCK
    SL = DIL_SLAB
    P = len(DIL_PATTERNS)
    assert S % SL == 0
    n_slabs = S // SL

    def body(q_ref, kc_ref, kp_ref, vc_ref, vp_ref, b_ref, y_ref, lse_ref, kj, vj, o_s, l_s):
        g = pl.program_id(1)
        kj[0:SL, :] = kp_ref[...]
        kj[SL:2 * SL, :] = kc_ref[...]
        vj[0:SL, :] = vp_ref[...]
        vj[SL:2 * SL, :] = vc_ref[...]
        for p, (_, d) in enumerate(DIL_PATTERNS):
            def batch(it, carry, p=p, d=d):
                first = it * DIL_UNROLL
                units = _dil_units(first, d)
                q = _dil_rows(q_ref, [b for _, b in units], d, CDT)
                s = _dil_logits(q, _dil_keys(kj, units, d), b_ref[p, 0], first, d, g > 0)
                m = jnp.max(s, axis=-1, keepdims=True)
                e = jnp.exp(s - m)
                ssum = jnp.sum(e, axis=-1, keepdims=True)
                o = _bdot(e.astype(CDT), _dil_keys(vj, units, d), 1) / ssum
                lse = jnp.broadcast_to(m + jnp.log(ssum), o.shape)
                for t, (_, b) in enumerate(units):
                    o_s[p, pl.ds(b, B, stride=d), :] = o[t]
                    l_s[p, pl.ds(b, B, stride=d), :] = lse[t]
                return carry

            lax.fori_loop(0, SL // B // DIL_UNROLL, batch, 0)
        ls = [l_s[p] for p in range(P)]
        m = functools.reduce(jnp.maximum, ls)
        w = [jnp.exp(l - m) for l in ls]
        tot = functools.reduce(jnp.add, w)
        y_ref[...] = functools.reduce(jnp.add, [(w[p] / tot) * o_s[p] for p in range(P)])
        lse_ref[...] = m + jnp.log(tot)

    cur, prev, _, bspec = _dil_specs(H, n_slabs)
    return pl.pallas_call(
        body, grid=(H, n_slabs),
        in_specs=[cur(0), cur(H), prev(H), cur(2 * H), prev(2 * H), bspec],
        out_specs=[cur(0), cur(0)],
        out_shape=[jax.ShapeDtypeStruct((S, H * E), F32)] * 2,
        scratch_shapes=[pltpu.VMEM((2 * SL, E), F32), pltpu.VMEM((2 * SL, E), F32),
                        pltpu.VMEM((P, SL, E), F32), pltpu.VMEM((P, SL, E), F32)],
        name=name, compiler_params=_cparams(("parallel", "parallel")),
    )(proj, proj, proj, proj, proj, bias)


def _dil_bwd(proj, bias, y, dy, lse, H, name):
    S = proj.shape[0]
    E = B = DIL_BLOCK
    SL = DIL_SLAB
    P = len(DIL_PATTERNS)
    assert S % SL == 0
    n_slabs = S // SL
    scale = E ** -0.5

    def body(q_ref, kc_ref, kp_ref, vc_ref, vp_ref, b_ref, y_ref, dy_ref, lse_ref, qn_ref, yn_ref, dyn_ref, lsen_ref,
             dq_ref, dk_ref, dv_ref, db_ref, kj, vj, dq_s, dk_s, dv_s, dl_s, dln_s):
        g = pl.program_id(1)
        kj[0:SL, :] = kp_ref[...]
        kj[SL:2 * SL, :] = kc_ref[...]
        vj[0:SL, :] = vp_ref[...]
        vj[SL:2 * SL, :] = vc_ref[...]
        dq_s[...] = jnp.zeros(dq_s.shape, F32)
        dk_s[...] = jnp.zeros(dk_s.shape, F32)
        dv_s[...] = jnp.zeros(dv_s.shape, F32)
        dl_s[...] = jnp.broadcast_to(jnp.sum(dy_ref[...] * y_ref[...], axis=-1, keepdims=True), (SL, E))
        dln_s[...] = jnp.broadcast_to(jnp.sum(dyn_ref[...] * yn_ref[...], axis=-1, keepdims=True), (SL, E))

        @pl.when(g == 0)
        def _():
            db_ref[...] = jnp.zeros(db_ref.shape, F32)

        tr = lambda t: jnp.swapaxes(t, 1, 2).astype(CDT)
        for p, (_, d) in enumerate(DIL_PATTERNS):
            def batch(it, carry, p=p, d=d):
                first = it * DIL_UNROLL
                units = _dil_units(first, d)
                starts = [b for _, b in units]
                q = _dil_rows(q_ref, starts, d, CDT)
                dyc = _dil_rows(dy_ref, starts, d, CDT)
                keys, vals = _dil_keys(kj, units, d), _dil_keys(vj, units, d)
                s = _dil_logits(q, keys, b_ref[p, 0], first, d, g > 0)
                e = jnp.exp(s - _dil_rows(lse_ref, starts, d)[:, :, 0:1])
                ds = e * (_bdot(dyc, vals, 2) - _dil_rows(dl_s, starts, d)[:, :, 0:1])
                dss = ds * scale
                dq = _bdot(dss.astype(CDT), keys, 1)
                dk = _bdot(tr(dss), q, 1)
                dv = _bdot(tr(e), dyc, 1)
                for t, (sg, b) in enumerate(units):
                    rows = pl.ds(b, B, stride=d)
                    dq_s[rows, :] += dq[t]
                    dk_s[rows, :] += dk[t, B:]
                    dv_s[rows, :] += dv[t, B:]

                    @pl.when(sg > 0)
                    def _(t=t, b=b):
                        before = pl.ds(b - B * d, B, stride=d)
                        dk_s[before, :] += dk[t, :B]
                        dv_s[before, :] += dv[t, :B]

                db_ref[p, 0] += jnp.sum(ds, axis=0)
                return carry

            lax.fori_loop(0, SL // B // DIL_UNROLL, batch, 0)

            n_after = min(d, DIL_UNROLL)

            def after(it, carry, p=p, d=d, n_after=n_after):
                starts = [it * n_after + t for t in range(n_after)]
                kstarts = [SL - B * d + s for s in starts]
                q = _dil_rows(qn_ref, starts, d, CDT)
                dyc = _dil_rows(dyn_ref, starts, d, CDT)
                k, v = _dil_rows(kc_ref, kstarts, d, CDT), _dil_rows(vc_ref, kstarts, d, CDT)
                ii = lax.broadcasted_iota(jnp.int32, (n_after, B, B), 1)
                jj = lax.broadcasted_iota(jnp.int32, (n_after, B, B), 2)
                s = jnp.where((jj >= ii) & (g < n_slabs - 1), _bdot(q, k, 2) * scale + b_ref[p, 0][:, :B][None], NEG_INF)
                e = jnp.exp(s - _dil_rows(lsen_ref, starts, d)[:, :, 0:1])
                ds = e * (_bdot(dyc, v, 2) - _dil_rows(dln_s, starts, d)[:, :, 0:1])
                dk = _bdot(tr(ds * scale), q, 1)
                dv = _bdot(tr(e), dyc, 1)
                for t, ks in enumerate(kstarts):
                    dk_s[pl.ds(ks, B, stride=d), :] += dk[t]
                    dv_s[pl.ds(ks, B, stride=d), :] += dv[t]
                return carry

            lax.fori_loop(0, d // n_after, after, 0)

        dq_ref[...] = dq_s[...].astype(dq_ref.dtype)
        dk_ref[...] = dk_s[...].astype(dk_ref.dtype)
        dv_ref[...] = dv_s[...].astype(dv_ref.dtype)

    cur, prev, nxt, bspec = _dil_specs(H, n_slabs)
    slab = pltpu.VMEM((SL, E), F32)
    return pl.pallas_call(
        body, grid=(H, n_slabs),
        in_specs=[cur(0), cur(H), prev(H), cur(2 * H), prev(2 * H), bspec, cur(0), cur(0), cur(0),
                  nxt(0), nxt(0), nxt(0), nxt(0)],
        out_specs=[cur(0), cur(0), cur(0), bspec],
        out_shape=[jax.ShapeDtypeStruct((S, H * E), CDT)] * 3 + [jax.ShapeDtypeStruct((P, H, B, 2 * B), F32)],
        scratch_shapes=[pltpu.VMEM((2 * SL, E), F32), pltpu.VMEM((2 * SL, E), F32), slab, slab, slab, slab, slab],
        name=name, compiler_params=_cparams(("parallel", "arbitrary")),
    )(proj, proj, proj, proj, proj, bias, y, dy, lse, proj, y, dy, lse)


def _adamw(w, g, m, v, name, br=128):
    R, C = w.shape
    br = br if R % br == 0 else R

    def body(w_ref, g_ref, m_ref, v_ref, g_out, d_ref, nm_ref, nv_ref):
        g_ = g_ref[...]
        g_out[...] = g_
        m_ = ADAM_B1 * m_ref[...] + (1.0 - ADAM_B1) * g_
        v_ = ADAM_B2 * v_ref[...] + (1.0 - ADAM_B2) * jnp.square(g_)
        m_hat = m_ / (1.0 - ADAM_B1 ** ADAM_STEP)
        v_hat = v_ / (1.0 - ADAM_B2 ** ADAM_STEP)
        d_ref[...] = -ADAM_LR * (m_hat / (jnp.sqrt(v_hat) + ADAM_EPS) + ADAM_WD * w_ref[...])
        nm_ref[...] = m_
        nv_ref[...] = v_

    blk = pl.BlockSpec((br, C), lambda i: (i, 0))
    return pl.pallas_call(
        body, grid=(R // br,), in_specs=[blk] * 4, out_specs=[blk] * 4,
        out_shape=[jax.ShapeDtypeStruct((R, C), F32)] * 4,
        name=name, compiler_params=_cparams(("parallel",)),
    )(w, g, m, v)


_HBM = pl.BlockSpec(memory_space=pltpu.HBM)
_SEM = pl.BlockSpec(memory_space=pltpu.SEMAPHORE)
_ANY = pl.BlockSpec(memory_space=pl.ANY)
_VMEM = pl.BlockSpec(memory_space=pltpu.VMEM)
_TOKEN = jax.ShapeDtypeStruct((8, 128), F32)


def _split_params():
    return pltpu.CompilerParams(has_side_effects=pltpu.SideEffectType.DATAFLOW_SIDE_EFFECTING)


def _place():
    x, y, c = lax.axis_index("x"), lax.axis_index("y"), lax.axis_index("c")
    chips = [(1 - x, y), (x, 1 - y), (1 - x, 1 - y)]
    return x, y, c, chips


def _tie(v, tokens, name):
    flat = v.reshape(1, -1)

    def body(v_ref, *rest):
        rest[-1][...] = v_ref[...]

    return pl.pallas_call(body, in_specs=[_VMEM] + [_ANY] * len(tokens), out_specs=_VMEM,
                          out_shape=jax.ShapeDtypeStruct(flat.shape, flat.dtype), name=name,
                          compiler_params=_cparams())(flat, *tokens).reshape(v.shape)


def _row_block(R, pref=256):
    return _tile(R, pref) if R % 128 == 0 else R


def _slot():
    return 2 * lax.axis_index("x") + lax.axis_index("y")


def _cast_into_slot(w, layer, name):
    _, R, C = w.shape
    br = _row_block(R)

    def body(w_ref, out_ref):
        out_ref[...] = w_ref[...].astype(out_ref.dtype)

    return pl.pallas_call(
        body, grid=(R // br,),
        in_specs=[pl.BlockSpec((None, br, C), lambda i: (layer, i, 0))],
        out_specs=pl.BlockSpec((None, br, C), lambda i: (_slot(), i, 0)),
        out_shape=jax.ShapeDtypeStruct((N_CHIPS, R, C), CDT),
        name=name, compiler_params=_cparams(("parallel",)),
    )(w)


def _gather_copies(src_ref, dst_ref, send_sems, recv_sems, incoming):
    Rh = src_ref.shape[1] // 2
    x, y, c, chips = _place()
    slot = 2 * x + y

    def half(ref, s, hf):
        return ref.at[s, pl.ds(hf * Rh, Rh), :]

    copies = []
    for j, (cx, cy) in enumerate(chips):
        for e in range(2):
            copies.append(pltpu.make_async_remote_copy(
                src_ref=half(src_ref, slot, c), dst_ref=half(dst_ref, 2 * cx + cy, e) if incoming else half(dst_ref, slot, c),
                send_sem=send_sems.at[2 * j + e], recv_sem=recv_sems.at[2 * j + (e if incoming else c)],
                device_id=(cx, cy, e), device_id_type=MESH))
    return copies


def _gather_start(buf, after, name):
    n_after = len(after)

    def body(*refs):
        buf_ref = refs[0]
        send_sems, recv_sems, out_ref, token = refs[1 + n_after:]
        for cp in _gather_copies(buf_ref, out_ref, send_sems, recv_sems, incoming=False):
            cp.start()
        token[...] = jnp.zeros(token.shape, token.dtype)

    return pl.pallas_call(
        body, in_specs=[_HBM] + [_ANY] * n_after, out_specs=(_SEM, _SEM, _HBM, _VMEM),
        out_shape=(pltpu.SemaphoreType.DMA((6,)), pltpu.SemaphoreType.DMA((6,)), pltpu.HBM(buf.shape, buf.dtype), _TOKEN),
        input_output_aliases={0: 2}, name=name, compiler_params=_split_params(),
    )(pltpu.with_memory_space_constraint(buf, pltpu.HBM), *after)


def _gather_wait(send_sems, recv_sems, buf, after, name):
    def body(buf_ref, send_sems, recv_sems, after_ref, out_ref):
        for cp in _gather_copies(buf_ref, out_ref, send_sems, recv_sems, incoming=False):
            cp.wait_send()
        for cp in _gather_copies(buf_ref, out_ref, send_sems, recv_sems, incoming=True):
            cp.wait_recv()

    return pl.pallas_call(
        body, in_specs=[_HBM, _SEM, _SEM, _ANY], out_specs=_HBM, out_shape=pltpu.HBM(buf.shape, buf.dtype),
        input_output_aliases={0: 0}, name=name, compiler_params=_split_params(),
    )(buf, send_sems, recv_sems, after)


def _scatter_copies(g_ref, land_ref, send_sems, recv_sems, incoming):
    Rh = g_ref.shape[1] // 2
    x, y, c, _ = _place()
    me = 4 * x + 2 * y + c
    copies = []
    for k in range(1, N_DEV):
        px, py, pc = (x + (k >> 2)) % 2, (y + ((k >> 1) & 1)) % 2, (c + (k & 1)) % 2
        copies.append(pltpu.make_async_remote_copy(
            src_ref=g_ref.at[2 * px + py, pl.ds(pc * Rh, Rh), :],
            dst_ref=land_ref.at[4 * px + 2 * py + pc if incoming else me],
            send_sem=send_sems.at[k - 1], recv_sem=recv_sems.at[k - 1], device_id=(px, py, pc), device_id_type=MESH))
    return copies


def _scatter_start(g, name):
    ns, R, C = g.shape

    def body(g_ref, land_ref, send_sems, recv_sems, g_thru, land_thru, token):
        for cp in _scatter_copies(g_ref, land_thru, send_sems, recv_sems, incoming=False):
            cp.start()
        token[...] = jnp.zeros(token.shape, token.dtype)

    land = lax.empty((N_DEV, R // 2, C), g.dtype)
    n = N_DEV - 1
    return pl.pallas_call(
        body, in_specs=[_HBM, _HBM], out_specs=(_SEM, _SEM, _HBM, _HBM, _VMEM),
        out_shape=(pltpu.SemaphoreType.DMA((n,)), pltpu.SemaphoreType.DMA((n,)), pltpu.HBM(g.shape, g.dtype),
                   pltpu.HBM(land.shape, land.dtype), _TOKEN),
        input_output_aliases={0: 2, 1: 3}, name=name, compiler_params=_split_params(),
    )(pltpu.with_memory_space_constraint(g, pltpu.HBM), pltpu.with_memory_space_constraint(land, pltpu.HBM))


def _scatter_wait(send_sems, recv_sems, g, land, after, name):
    def body(g_ref, land_ref, send_sems, recv_sems, after_ref, g_out, land_out):
        for cp in _scatter_copies(g_ref, land_out, send_sems, recv_sems, incoming=False):
            cp.wait_send()
        for cp in _scatter_copies(g_ref, land_out, send_sems, recv_sems, incoming=True):
            cp.wait_recv()

    return pl.pallas_call(
        body, in_specs=[_HBM, _HBM, _SEM, _SEM, _ANY], out_specs=(_HBM, _HBM),
        out_shape=(pltpu.HBM(g.shape, g.dtype), pltpu.HBM(land.shape, land.dtype)),
        input_output_aliases={0: 0, 1: 1}, name=name, compiler_params=_split_params(),
    )(g, land, send_sems, recv_sems, after)


def _device_sum(land, g, layer, n_layers, prev, name):
    nd, Rh, C = land.shape
    br = _row_block(Rh)
    nb = Rh // br
    core = lambda: lax.axis_index("c")
    me = lambda: 2 * _slot() + core()

    def body(*refs):
        own = refs[nd][...]
        acc = None
        for d in range(nd):
            t = jnp.where(me() == d, own, refs[d][...]).astype(F32)
            acc = t if acc is None else acc + t
        refs[-1][...] = acc

    def piece(d):
        return pl.BlockSpec((None, br, C), lambda i: (jnp.where(me() == d, (d + 1) % nd, d), i, 0))

    ins = [land] * nd + [g] + ([prev] if prev is not None else [])
    return pl.pallas_call(
        body, grid=(nb,),
        in_specs=[piece(d) for d in range(nd)]
        + [pl.BlockSpec((None, br, C), lambda i: (_slot(), core() * nb + i, 0))]
        + ([_ANY] if prev is not None else []),
        out_specs=pl.BlockSpec((None, br, C), lambda i: (layer, core() * nb + i, 0)),
        out_shape=jax.ShapeDtypeStruct((n_layers, 2 * Rh, C), F32),
        input_output_aliases={nd + 1: 0} if prev is not None else {},
        name=name, compiler_params=_cparams(("parallel",)),
    )(*ins)


def _join_copy(src_ref, dst_ref, layer, send_sem, recv_sem, incoming):
    Rh = src_ref.shape[1] // 2
    x, y, c, _ = _place()
    mine, other = pl.ds(c * Rh, Rh), pl.ds((1 - c) * Rh, Rh)
    return pltpu.make_async_remote_copy(src_ref=src_ref.at[layer, mine, :],
                                        dst_ref=dst_ref.at[layer, other if incoming else mine, :],
                                        send_sem=send_sem, recv_sem=recv_sem, device_id=(x, y, 1 - c),
                                        device_id_type=MESH)


def _join_start(g, layer, name):
    def body(g_ref, send_sem, recv_sem, out_ref, token):
        _join_copy(g_ref, out_ref, layer, send_sem, recv_sem, incoming=False).start()
        token[...] = jnp.zeros(token.shape, token.dtype)

    return pl.pallas_call(
        body, in_specs=[_HBM], out_specs=(_SEM, _SEM, _HBM, _VMEM),
        out_shape=(pltpu.SemaphoreType.DMA(()), pltpu.SemaphoreType.DMA(()), pltpu.HBM(g.shape, g.dtype), _TOKEN),
        input_output_aliases={0: 2}, name=name, compiler_params=_split_params(),
    )(pltpu.with_memory_space_constraint(g, pltpu.HBM))


def _join_wait(send_sem, recv_sem, g, layer, after, name):
    def body(g_ref, send_sem, recv_sem, after_ref, out_ref):
        _join_copy(g_ref, out_ref, layer, send_sem, recv_sem, incoming=False).wait_send()
        _join_copy(g_ref, out_ref, layer, send_sem, recv_sem, incoming=True).wait_recv()

    return pl.pallas_call(
        body, in_specs=[_HBM, _SEM, _SEM, _ANY], out_specs=_HBM, out_shape=pltpu.HBM(g.shape, g.dtype),
        input_output_aliases={0: 0}, name=name, compiler_params=_split_params(),
    )(g, send_sem, recv_sem, after)


def _all_reduce_small(v, name):
    rows, cols = v.shape

    def body(v_ref, out_ref, buf, send_sems, recv_sems):
        x, y, c, _ = _place()
        me = 4 * x + 2 * y + c
        buf[me] = v_ref[...]
        peers = []
        for k in range(1, N_DEV):
            px, py, pc = (x + (k >> 2)) % 2, (y + ((k >> 1) & 1)) % 2, (c + (k & 1)) % 2
            peers.append((px, py, pc))
        sends = []
        for k, peer in enumerate(peers):
            cp = pltpu.make_async_remote_copy(src_ref=v_ref, dst_ref=buf.at[me], send_sem=send_sems.at[k],
                                              recv_sem=recv_sems.at[k], device_id=peer, device_id_type=MESH)
            cp.start()
            sends.append(cp)
        for k, (px, py, pc) in enumerate(peers):
            pltpu.make_async_remote_copy(src_ref=v_ref, dst_ref=buf.at[4 * px + 2 * py + pc], send_sem=send_sems.at[k],
                                         recv_sem=recv_sems.at[k], device_id=(px, py, pc),
                                         device_id_type=MESH).wait_recv()
        for cp in sends:
            cp.wait_send()
        acc = buf[0]
        for i in range(1, N_DEV):
            acc = acc + buf[i]
        out_ref[...] = acc

    vmem = pl.BlockSpec(memory_space=pltpu.VMEM)
    return pl.pallas_call(
        body, in_specs=[vmem], out_specs=vmem, out_shape=jax.ShapeDtypeStruct((rows, cols), F32),
        scratch_shapes=[pltpu.VMEM((N_DEV, rows, cols), F32), pltpu.SemaphoreType.DMA((N_DEV - 1,)),
                        pltpu.SemaphoreType.DMA((N_DEV - 1,))],
        name=name, compiler_params=pltpu.CompilerParams(),
    )(v)


def _reduce_scatter_sum(started, after, layer, n_layers, prev, tag):
    send_sems, recv_sems, g, land, _ = started
    g, land = _scatter_wait(send_sems, recv_sems, g, land, after, f"rs_wait_{tag}")
    f = _device_sum(land, g, layer, n_layers, prev, f"rs_sum_{tag}")
    return _join_start(f, layer, f"rs_join_start_{tag}")


def _split_w_in(wg, Hf, name):
    ns, D, cols = wg.shape
    a = 3 * Hf * HEAD_DIM
    n6 = ns * cols - Hf
    br = _row_block(D)

    def body(w_ref, w6_ref, wf_ref):
        nat = jnp.concatenate([w_ref[s] for s in range(ns)], axis=1)
        w6_ref[...] = jnp.concatenate([nat[:, :a], nat[:, a + Hf:]], axis=1)
        wf_ref[...] = nat[:, a:a + Hf]

    w6, wf = pl.pallas_call(
        body, grid=(D // br,), in_specs=[pl.BlockSpec((ns, br, cols), lambda i: (0, i, 0))],
        out_specs=[pl.BlockSpec((br, n6), lambda i: (i, 0)), pl.BlockSpec((br, Hf), lambda i: (i, 0))],
        out_shape=[jax.ShapeDtypeStruct((D, n6), wg.dtype), jax.ShapeDtypeStruct((D, Hf), wg.dtype)],
        name=name, compiler_params=_cparams(("parallel",)),
    )(wg)
    return w6, wf.T


def _join_dw_in(dw6, dwf_t, Hf, name):
    D, n6 = dw6.shape
    a = 3 * Hf * HEAD_DIM
    cols = (n6 + Hf) // N_CHIPS
    br = _row_block(D)

    def body(w6_ref, wf_ref, out_ref):
        w6 = w6_ref[...]
        nat = jnp.concatenate([w6[:, :a], wf_ref[...], w6[:, a:]], axis=1)
        for s in range(N_CHIPS):
            out_ref[s] = nat[:, s * cols:(s + 1) * cols]

    return pl.pallas_call(
        body, grid=(D // br,),
        in_specs=[pl.BlockSpec((br, n6), lambda i: (i, 0)), pl.BlockSpec((br, Hf), lambda i: (i, 0))],
        out_specs=pl.BlockSpec((N_CHIPS, br, cols), lambda i: (0, i, 0)),
        out_shape=jax.ShapeDtypeStruct((N_CHIPS, D, cols), dw6.dtype),
        name=name, compiler_params=_cparams(("parallel",)),
    )(dw6, dwf_t.T.astype(dw6.dtype))


def _tied(v, tokens, name):
    return _tie(v, tokens, name) if tokens else v


def _layer_fwd(x, p, weight, bias, tokens, tag):
    Hf, Hd = p["forget_b"].shape[0], bias.shape[1]
    h1 = _rms_fwd(x, _tied(p["norm1_g"], tokens, f"tie_norm1_{tag}"), f"norm1_{tag}")
    w6, wf_t = _split_w_in(weight("w_in", h1), Hf, f"split_w_in_{tag}")
    n_a = 3 * Hf * HEAD_DIM
    proj_a = _mm_nn(h1, w6, f"proj_a_{tag}", [CDT], epi=lambda acc: (acc,), b_cols=(0, n_a))[0]
    proj_b = _mm_nn(h1, w6, f"proj_b_{tag}", [F32], b_cols=(n_a, w6.shape[1] - n_a))[0]
    f_t = _mm_nt(wf_t, h1, f"fproj_{tag}", [F32])[0]
    qc, kc = _fox_bias_operands(_gates_fwd(f_t, p["forget_b"], f"gates_{tag}"), f"fox_operands_{tag}")
    y_a, lse_a = _fox_fwd(proj_a, qc, kc, Hf, f"fox_{tag}")
    y_b, lse_b = _dil_fwd(proj_b, bias, Hd, f"dil_{tag}")
    mixed = _pair_norm_fwd(y_a, y_b, p["outnorm_a_g"], p["outnorm_b_g"], f"norm_ab_{tag}")
    w_out = weight("w_out", mixed)
    w_out = w_out.reshape(-1, w_out.shape[2])
    x1 = _mm_nn(mixed, w_out, f"attn_out_{tag}", [F32], extras=[x])[0]
    h2 = _rms_fwd(x1, p["norm2_g"], f"norm2_{tag}")
    w_mi = weight("w_mlp_in", h2)
    u, act = _mm_nn(h2, w_mi, f"mlp_in_{tag}", [CDT, CDT], b_slots=True,
                    epi=lambda acc: (acc, jnp.square(jnp.maximum(acc, 0.0))))
    w_mo = weight("w_mlp_out", act)
    w_mo = w_mo.reshape(-1, w_mo.shape[2])
    x2 = _mm_nn(act, w_mo, f"mlp_out_{tag}", [F32], extras=[x1])[0]
    saved = dict(x=x, h1=h1, proj_a=proj_a, proj_b=proj_b, f_t=f_t, qc=qc, kc=kc, y_a=y_a, lse_a=lse_a, y_b=y_b,
                 lse_b=lse_b, mixed=mixed, x1=x1, h2=h2, u=u, act=act, w6=w6, wf_t=wf_t, w_out=w_out, w_mi=w_mi,
                 w_mo=w_mo)
    return x2, saved


def _layer_bwd(dx2, dx2c, p, send, bias, sv, defer_w_out, tag):
    Hf, Hd = p["forget_b"].shape[0], bias.shape[1]
    E = HEAD_DIM
    rows = lambda g: g.reshape(N_CHIPS, -1, g.shape[1])
    du = _mm_nt(dx2c, sv["w_mo"], f"d_act_{tag}", [CDT], extras=[sv["u"]],
                epi=lambda acc, u: (acc * (2.0 * jnp.maximum(u.astype(F32), 0.0)),))[0]
    tokens = send("w_mlp_out", rows(_mm_tn(sv["act"], dx2c, f"dw_mlp_out_{tag}", CDT)))
    dh2 = _mm_nt(du, sv["w_mi"], f"d_h2_{tag}", [F32], b_slots=True)[0]
    tokens = tokens + send("w_mlp_in", _mm_tn(sv["h2"], du, f"dw_mlp_in_{tag}", CDT, out_slots=N_CHIPS))
    dx1, dx1c, g_norm2 = _rms_bwd(sv["x1"], _tied(p["norm2_g"], tokens, f"tie_norm2_{tag}"), dh2, dx2,
                                  f"d_norm2_{tag}")
    dmixed = _mm_nt(dx1c, sv["w_out"], f"d_mixed_{tag}", [F32])[0]
    send_w_out = lambda: send("w_out", rows(_mm_tn(sv["mixed"], dx1c, f"dw_out_{tag}", CDT)))
    tokens = [] if defer_w_out else send_w_out()
    dy_a, dy_b, g_na, g_nb = _pair_norm_bwd(sv["y_a"], sv["y_b"], _tied(p["outnorm_a_g"], tokens, f"tie_norm_a_{tag}"),
                                            p["outnorm_b_g"], dmixed, f"d_norm_ab_{tag}")
    dq_a, dcq, dk_a, dv_a, dck = _fox_bwd(sv["proj_a"], sv["qc"], sv["kc"], sv["lse_a"], sv["y_a"], dy_a, Hf,
                                          f"fox_bwd_{tag}")
    df, dfc, g_fb = _gates_bwd(sv["f_t"], p["forget_b"], dcq[:, ::E].T, dck.reshape(Hf, -1), f"d_gates_{tag}")
    dq_b, dk_b, dv_b, dbias = _dil_bwd(sv["proj_b"], bias, sv["y_b"], dy_b, sv["lse_b"], Hd, f"dil_bwd_{tag}")
    dproj = jnp.concatenate([dq_a, dk_a, dv_a, dq_b, dk_b, dv_b], axis=1)
    g_w6 = _mm_tn(sv["h1"], dproj, f"dw_in_{tag}", CDT)
    g_wf_t = _mm_nn(dfc, sv["h1"], f"dw_f_{tag}", [F32])[0]
    tokens = send("w_in", _join_dw_in(g_w6, g_wf_t, Hf, f"join_dw_in_{tag}"))
    dh1_f = _mm_tn(dfc, _tied(sv["wf_t"], tokens, f"tie_wf_{tag}"), f"d_h1_f_{tag}", F32)
    dh1 = _mm_nt(dproj, sv["w6"], f"d_h1_{tag}", [F32], extras=[dh1_f])[0]
    dx, dxc, g_norm1 = _rms_bwd(sv["x"], p["norm1_g"], dh1, dx1, f"d_norm1_{tag}")
    grads = dict(norm1_g=g_norm1[0], norm2_g=g_norm2[0], outnorm_a_g=g_na[0], outnorm_b_g=g_nb[0],
                 forget_b=g_fb[:, 0], dbias=dbias)
    return dx, dxc, grads, (send_w_out if defer_w_out else None)


_LAYER_SMALL = ("norm1_g", "forget_b", "outnorm_a_g", "outnorm_b_g", "norm2_g")


def _local_step(x, target, small, weight, send, tokens):
    depth = small["norm1_g"].shape[0]
    buckets = _bucket_table()
    bias = _bias_table(small["rel_bias"], buckets, "bias_table")
    layers, saved = [], []
    for l in range(depth):
        p = {k: small[k][l] for k in _LAYER_SMALL}
        layers.append(p)
        x, sv = _layer_fwd(x, p, functools.partial(weight, l), bias, tokens if l == 0 else [], f"l{l}")
        saved.append(sv)
    dx, dxc, g_final, loss = _loss_bwd(x, small["final_norm_g"], target, "loss")
    layer_grads = [None] * depth
    for l in reversed(range(depth)):
        dx, dxc, layer_grads[l], last = _layer_bwd(dx, dxc, layers[l], functools.partial(send, l), bias, saved[l],
                                                   l == 0, f"l{l}")
    tokens = last()
    dbias = functools.reduce(jnp.add, [g["dbias"] for g in layer_grads])
    g_rel = _bias_table_bwd(dbias, buckets, "d_bias_table")[:, 0, :].T
    small_grads = dict(final_norm_g=g_final[0], rel_bias=g_rel,
                       **{k: jnp.stack([g[k] for g in layer_grads]) for k in _LAYER_SMALL})
    return loss[0, 0], dx, small_grads, tokens


_BIG = ("w_in", "w_out", "w_mlp_in", "w_mlp_out")
_SMALL = ("norm1_g", "forget_b", "rel_bias", "outnorm_a_g", "outnorm_b_g", "norm2_g", "final_norm_g")
_ORDER = ("norm1_g", "w_in", "forget_b", "rel_bias", "outnorm_a_g", "outnorm_b_g", "w_out", "norm2_g", "w_mlp_in",
          "w_mlp_out", "final_norm_g")


def _pack_small(d):
    flat = jnp.concatenate([d[k].reshape(-1) for k in _SMALL])
    rows = -(-flat.shape[0] // (8 * SMALL_COLS)) * 8
    return jnp.pad(flat, (0, rows * SMALL_COLS - flat.shape[0])).reshape(rows, SMALL_COLS)


def _unpack_small(packed, like):
    flat, out, at = packed.reshape(-1), {}, 0
    for k in _SMALL:
        n = like[k].size
        out[k] = flat[at:at + n].reshape(like[k].shape)
        at += n
    return out


def kernel(x, norm1_g, w_in, forget_b, rel_bias, outnorm_a_g, outnorm_b_g, w_out, norm2_g, w_mlp_in, w_mlp_out, final_norm_g, loss_target, m_norm1_g, m_w_in, m_forget_b, m_rel_bias, m_outnorm_a_g, m_outnorm_b_g, m_w_out, m_norm2_g, m_w_mlp_in, m_w_mlp_out, m_final_norm_g, v_norm1_g, v_w_in, v_forget_b, v_rel_bias, v_outnorm_a_g, v_outnorm_b_g, v_w_out, v_norm2_g, v_w_mlp_in, v_w_mlp_out, v_final_norm_g):
    w = dict(norm1_g=norm1_g, w_in=w_in, forget_b=forget_b, rel_bias=rel_bias, outnorm_a_g=outnorm_a_g,
             outnorm_b_g=outnorm_b_g, w_out=w_out, norm2_g=norm2_g, w_mlp_in=w_mlp_in, w_mlp_out=w_mlp_out,
             final_norm_g=final_norm_g)
    m = dict(norm1_g=m_norm1_g, w_in=m_w_in, forget_b=m_forget_b, rel_bias=m_rel_bias, outnorm_a_g=m_outnorm_a_g,
             outnorm_b_g=m_outnorm_b_g, w_out=m_w_out, norm2_g=m_norm2_g, w_mlp_in=m_w_mlp_in,
             w_mlp_out=m_w_mlp_out, final_norm_g=m_final_norm_g)
    v = dict(norm1_g=v_norm1_g, w_in=v_w_in, forget_b=v_forget_b, rel_bias=v_rel_bias, outnorm_a_g=v_outnorm_a_g,
             outnorm_b_g=v_outnorm_b_g, w_out=v_w_out, norm2_g=v_norm2_g, w_mlp_in=v_w_mlp_in,
             w_mlp_out=v_w_mlp_out, final_norm_g=v_final_norm_g)
    depth = w_in.shape[0]
    small = {k: w[k] for k in _SMALL}

    gathers, tokens = {}, []
    for l in range(depth):
        for k in _BIG:
            buf = _cast_into_slot(w[k], l, f"cast_{k}_l{l}")
            send_sems, recv_sems, buf, token = _gather_start(buf, tokens, f"gather_start_{k}_l{l}")
            gathers[l, k], tokens = (send_sems, recv_sems, buf), [token]

    def weight(l, k, after):
        return _gather_wait(*gathers[l, k], after, f"gather_wait_{k}_l{l}")

    scatters = {}

    def send(l, k, g):
        scatters[l, k] = _scatter_start(g, f"rs_start_{k}_l{l}")
        return [scatters[l, k][4]]

    loss, grad_x, small_grads, tokens = _local_step(x[0], loss_target[0], small, weight, send, tokens)
    loss = lax.psum(loss, ("x", "y", "c"))
    small_sums = _all_reduce_small(_tied(_pack_small(small_grads), tokens, "tie_small"), "small_all_reduce")

    grads, delta, new_m, new_v = {}, {}, {}, {}
    after, seen, joining = small_sums, {k: 0 for k in _BIG}, None

    def joined(after):
        (l, k), (send_sem, recv_sem, g) = joining
        grads[k] = _join_wait(send_sem, recv_sem, g, l, after, f"rs_join_wait_{k}_l{l}")
        seen[k] += 1
        if seen[k] < depth:
            return after
        shape = w[k].shape
        flat = lambda t: t.reshape(-1, shape[-1])
        g_, d_, m_, v_ = _adamw(flat(w[k]), flat(grads[k]), flat(m[k]), flat(v[k]), f"adamw_{k}")
        grads[k], delta[k], new_m[k], new_v[k] = (t.reshape(shape) for t in (g_, d_, m_, v_))
        return d_

    for (l, k), started in scatters.items():
        assert joining is None or joining[0][1] != k
        send_sem, recv_sem, g, token = _reduce_scatter_sum(started, after, l, depth, grads.get(k), f"{k}_l{l}")
        if joining is not None:
            after = joined(token)
        joining = ((l, k), (send_sem, recv_sem, g))
    joined(after)
    grads.update(_unpack_small(small_sums, small))
    _, d_, m_, v_ = _adamw(_pack_small(small), _pack_small({k: grads[k] for k in _SMALL}),
                           _pack_small({k: m[k] for k in _SMALL}), _pack_small({k: v[k] for k in _SMALL}), "adamw_small")
    delta.update(_unpack_small(d_, small))
    new_m.update(_unpack_small(m_, small))
    new_v.update(_unpack_small(v_, small))

    return (loss, grad_x[None], *[grads[k] for k in _ORDER], *[delta[k] for k in _ORDER],
            *[new_m[k] for k in _ORDER], *[new_v[k] for k in _ORDER])
```

```python
import functools

import jax
import jax.numpy as jnp
from jax import lax
from jax.experimental import pallas as pl
from jax.experimental.pallas import tpu as pltpu

F32 = jnp.float32
CDT = jnp.bfloat16
HEAD_DIM = 128
NORM_EPS = 1e-6
NEG_INF = -1e30
LOG2E = 1.4426950408889634
REL_BUCKETS = 32
REL_MAX_DISTANCE = 2048
DIL_PATTERNS = ((128, 1), (512, 4), (2048, 16))
DIL_BLOCK = 128
ADAM_LR, ADAM_B1, ADAM_B2, ADAM_EPS, ADAM_WD, ADAM_STEP = 0.001, 0.9, 0.999, 1e-08, 0.01, 10
N_CHIPS = 4
N_DEV = 8
VMEM_LIMIT_BYTES = 56 * 1024 * 1024
SMALL_COLS = 1024
MESH = pl.DeviceIdType.MESH


def _cparams(sem=None):
    return pltpu.CompilerParams(dimension_semantics=sem, vmem_limit_bytes=VMEM_LIMIT_BYTES)


def _tile(dim, pref):
    t = min(pref, dim)
    t -= t % 128
    while t >= 128:
        if dim % t == 0:
            return t
        t -= 128
    return dim


def _rowwise(fn, ins, out_dtypes, name, bs=256, consts=()):
    R, C = ins[0].shape
    bs = min(bs, R)
    n_in, n_c = len(ins), len(consts)

    def body(*refs):
        vals = [r[...] for r in refs[:n_in + n_c]]
        res = fn(*vals)
        for o, r in zip(refs[n_in + n_c:], res):
            o[...] = r.astype(o.dtype)

    row = pl.BlockSpec((bs, C), lambda i: (i, 0))
    return pl.pallas_call(
        body, grid=(R // bs,),
        in_specs=[row] * n_in + [pl.BlockSpec((1, c.shape[-1]), lambda i: (0, 0)) for c in consts],
        out_specs=[row] * len(out_dtypes),
        out_shape=[jax.ShapeDtypeStruct((R, C), d) for d in out_dtypes],
        name=name, compiler_params=_cparams(("parallel",)),
    )(*ins, *[c.reshape(1, -1) for c in consts])


def _rms_fwd(x, g, name):
    def fn(xf, gg):
        r = lax.rsqrt(jnp.mean(xf * xf, axis=-1, keepdims=True) + NORM_EPS)
        return ((xf * r) * gg,)
    return _rowwise(fn, [x], [CDT], name, consts=[g])[0]


def _rms_bwd(x, g, dh, dres, name, bs=256):
    S, D = x.shape
    bs = min(bs, S)
    has_res = dres is not None

    def body(*refs):
        x_ref, g_ref, dh_ref = refs[:3]
        dx_ref, dxc_ref, dg_ref = refs[-3:]
        xf = x_ref[...]
        r = lax.rsqrt(jnp.mean(xf * xf, axis=-1, keepdims=True) + NORM_EPS)
        xhat = xf * r
        dh_ = dh_ref[...].astype(F32)
        dxhat = dh_ * g_ref[...]
        dx = r * (dxhat - xhat * jnp.mean(dxhat * xhat, axis=-1, keepdims=True))
        if has_res:
            dx = dx + refs[3][...]
        dx_ref[...] = dx
        dxc_ref[...] = dx.astype(dxc_ref.dtype)
        part = jnp.sum(dh_ * xhat, axis=0, keepdims=True)

        @pl.when(pl.program_id(0) == 0)
        def _():
            dg_ref[...] = part

        @pl.when(pl.program_id(0) > 0)
        def _():
            dg_ref[...] += part

    row = pl.BlockSpec((bs, D), lambda i: (i, 0))
    one = pl.BlockSpec((1, D), lambda i: (0, 0))
    ins = [x, g.reshape(1, D), dh] + ([dres] if has_res else [])
    return pl.pallas_call(
        body, grid=(S // bs,),
        in_specs=[row, one, row] + ([row] if has_res else []),
        out_specs=[row, row, one],
        out_shape=[jax.ShapeDtypeStruct((S, D), F32), jax.ShapeDtypeStruct((S, D), CDT),
                   jax.ShapeDtypeStruct((1, D), F32)],
        name=name, compiler_params=_cparams(("arbitrary",)),
    )(*ins)


def _pair_norm_fwd(y_a, y_b, g_a, g_b, name, bs=256):
    S, Da = y_a.shape
    Db = y_b.shape[1]
    bs = min(bs, S)

    def body(a_ref, b_ref, ga_ref, gb_ref, o_ref):
        def norm(x, g):
            r = lax.rsqrt(jnp.mean(x * x, axis=-1, keepdims=True) + NORM_EPS)
            return ((x * r) * g).astype(o_ref.dtype)
        o_ref[:, :Da] = norm(a_ref[...], ga_ref[...])
        o_ref[:, Da:] = norm(b_ref[...], gb_ref[...])

    row = lambda n: pl.BlockSpec((bs, n), lambda i: (i, 0))
    one = lambda n: pl.BlockSpec((1, n), lambda i: (0, 0))
    return pl.pallas_call(
        body, grid=(S // bs,), in_specs=[row(Da), row(Db), one(Da), one(Db)], out_specs=row(Da + Db),
        out_shape=jax.ShapeDtypeStruct((S, Da + Db), CDT), name=name, compiler_params=_cparams(("parallel",)),
    )(y_a, y_b, g_a.reshape(1, Da), g_b.reshape(1, Db))


def _pair_norm_bwd(y_a, y_b, g_a, g_b, dmixed, name, bs=256):
    S, Da = y_a.shape
    Db = y_b.shape[1]
    bs = min(bs, S)

    def body(a_ref, b_ref, ga_ref, gb_ref, dm_ref, da_ref, db_ref, dga_ref, dgb_ref):
        def one(x_ref, g_ref, dh, dx_ref, dg_ref):
            xf = x_ref[...]
            r = lax.rsqrt(jnp.mean(xf * xf, axis=-1, keepdims=True) + NORM_EPS)
            xhat = xf * r
            dxhat = dh * g_ref[...]
            dx_ref[...] = r * (dxhat - xhat * jnp.mean(dxhat * xhat, axis=-1, keepdims=True))
            part = jnp.sum(dh * xhat, axis=0, keepdims=True)

            @pl.when(pl.program_id(0) == 0)
            def _():
                dg_ref[...] = part

            @pl.when(pl.program_id(0) > 0)
            def _():
                dg_ref[...] += part

        dm = dm_ref[...]
        one(a_ref, ga_ref, dm[:, :Da], da_ref, dga_ref)
        one(b_ref, gb_ref, dm[:, Da:], db_ref, dgb_ref)

    row = lambda n: pl.BlockSpec((bs, n), lambda i: (i, 0))
    one_ = lambda n: pl.BlockSpec((1, n), lambda i: (0, 0))
    return pl.pallas_call(
        body, grid=(S // bs,), in_specs=[row(Da), row(Db), one_(Da), one_(Db), row(Da + Db)],
        out_specs=[row(Da), row(Db), one_(Da), one_(Db)],
        out_shape=[jax.ShapeDtypeStruct((S, Da), F32), jax.ShapeDtypeStruct((S, Db), F32),
                   jax.ShapeDtypeStruct((1, Da), F32), jax.ShapeDtypeStruct((1, Db), F32)],
        name=name, compiler_params=_cparams(("arbitrary",)),
    )(y_a, y_b, g_a.reshape(1, Da), g_b.reshape(1, Db), dmixed)


def _loss_bwd(x, g, target, name, bs=256):
    S, D = x.shape
    bs = min(bs, S)

    def body(x_ref, g_ref, t_ref, dx_ref, dxc_ref, dg_ref, loss_ref):
        xf = x_ref[...]
        r = lax.rsqrt(jnp.mean(xf * xf, axis=-1, keepdims=True) + NORM_EPS)
        xhat = xf * r
        err = xhat * g_ref[...] - t_ref[...]
        lpart = 0.5 * jnp.sum(jnp.mean(err * err, axis=-1, keepdims=True), axis=0, keepdims=True)
        dy = err / D
        dxhat = dy * g_ref[...]
        dx = r * (dxhat - xhat * jnp.mean(dxhat * xhat, axis=-1, keepdims=True))
        dx_ref[...] = dx
        dxc_ref[...] = dx.astype(dxc_ref.dtype)
        gpart = jnp.sum(dy * xhat, axis=0, keepdims=True)

        @pl.when(pl.program_id(0) == 0)
        def _():
            dg_ref[...] = gpart
            loss_ref[...] = lpart

        @pl.when(pl.program_id(0) > 0)
        def _():
            dg_ref[...] += gpart
            loss_ref[...] += lpart

    row = pl.BlockSpec((bs, D), lambda i: (i, 0))
    one = pl.BlockSpec((1, D), lambda i: (0, 0))
    return pl.pallas_call(
        body, grid=(S // bs,),
        in_specs=[row, one, row],
        out_specs=[row, row, one, pl.BlockSpec((1, 1), lambda i: (0, 0))],
        out_shape=[jax.ShapeDtypeStruct((S, D), F32), jax.ShapeDtypeStruct((S, D), CDT),
                   jax.ShapeDtypeStruct((1, D), F32), jax.ShapeDtypeStruct((1, 1), F32)],
        name=name, compiler_params=_cparams(("arbitrary",)),
    )(x, g.reshape(1, D), target)


_NN = (((1,), (0,)), ((), ()))
_NT = (((1,), (1,)), ((), ()))
_TN = (((0,), (0,)), ((), ()))


def _mm(a, b, *, M, N, K, a_spec, b_spec, o_spec, dims, tm, tn, tk, name, out_shapes, extras=(), epi=None):
    nk = K // tk
    n_ex, n_out = len(extras), len(out_shapes)
    in_place = epi is None
    if in_place:
        assert n_out == 1 and n_ex <= 1 and out_shapes[0].dtype == F32
        epi = lambda acc, *r: (acc + r[0] if r else acc,)

    def body(*refs):
        a_ref, b_ref = refs[0], refs[1]
        ex = refs[2:2 + n_ex]
        outs = refs[2 + n_ex:2 + n_ex + n_out]
        part = lax.dot_general(a_ref[...], b_ref[...], dims, preferred_element_type=F32)

        def finish(acc):
            for o, r in zip(outs, epi(acc, *[e[...] for e in ex])):
                o[...] = r.astype(o.dtype)

        if nk == 1:
            finish(part)
        elif in_place:
            k = pl.program_id(2)

            @pl.when(k == 0)
            def _():
                finish(part)

            @pl.when(k > 0)
            def _():
                outs[0][...] += part
        else:
            acc_ref = refs[-1]
            k = pl.program_id(2)

            @pl.when(k == 0)
            def _():
                acc_ref[...] = part

            @pl.when(k > 0)
            def _():
                acc_ref[...] += part

            @pl.when(k == nk - 1)
            def _():
                finish(acc_ref[...])

    ex_spec = pl.BlockSpec((tm, tn), lambda i, j, k: (i, j))
    return pl.pallas_call(
        body, grid=(M // tm, N // tn, nk),
        in_specs=[a_spec, b_spec] + [ex_spec] * n_ex,
        out_specs=[o_spec] * n_out,
        out_shape=out_shapes,
        scratch_shapes=[pltpu.VMEM((tm, tn), F32)] if nk > 1 and not in_place else [],
        name=name, compiler_params=_cparams(("parallel", "parallel", "arbitrary")),
    )(a, b, *extras)


def _mm_tiles(K):
    return (2048, 512, 2048) if K <= 2048 else (1024, 1024, 2048)


def _mm_nn(a, b, name, out_dtypes, extras=(), epi=None, b_slots=False, b_cols=None):
    M, K = a.shape
    tm, tn, tk = _mm_tiles(K)
    if b_slots:
        ns, _, Ns = b.shape
        N = ns * Ns
        tn = _tile(Ns, tn)
        npb = Ns // tn
        tk_ = _tile(K, tk)
        b_spec = pl.BlockSpec((None, tk_, tn), lambda i, j, k: (j // npb, k, j % npb))
    else:
        first, N = b_cols if b_cols is not None else (0, b.shape[1])
        tn = _tile(N, tn)
        assert first % tn == 0
        tk_ = _tile(K, tk)
        b_spec = pl.BlockSpec((tk_, tn), lambda i, j, k: (k, first // tn + j))
    tm = _tile(M, tm)
    return _mm(a, b, M=M, N=N, K=K, a_spec=pl.BlockSpec((tm, tk_), lambda i, j, k: (i, k)), b_spec=b_spec,
               o_spec=pl.BlockSpec((tm, tn), lambda i, j, k: (i, j)), dims=_NN, tm=tm, tn=tn, tk=tk_, name=name,
               out_shapes=[jax.ShapeDtypeStruct((M, N), d) for d in out_dtypes], extras=extras, epi=epi)


def _mm_nt(a, b, name, out_dtypes, extras=(), epi=None, b_slots=False):
    M, K = a.shape
    tm, tn, tk = _mm_tiles(K)
    tm = _tile(M, tm)
    if b_slots:
        ns, N, Ks = b.shape
        tk_ = _tile(Ks, tk)
        kpb = Ks // tk_
        tn = _tile(N, tn)
        b_spec = pl.BlockSpec((None, tn, tk_), lambda i, j, k: (k // kpb, j, k % kpb))
    else:
        N = b.shape[0]
        tk_ = _tile(K, tk)
        tn = _tile(N, tn)
        b_spec = pl.BlockSpec((tn, tk_), lambda i, j, k: (j, k))
    return _mm(a, b, M=M, N=N, K=K, a_spec=pl.BlockSpec((tm, tk_), lambda i, j, k: (i, k)), b_spec=b_spec,
               o_spec=pl.BlockSpec((tm, tn), lambda i, j, k: (i, j)), dims=_NT, tm=tm, tn=tn, tk=tk_, name=name,
               out_shapes=[jax.ShapeDtypeStruct((M, N), d) for d in out_dtypes], extras=extras, epi=epi)


def _mm_tn(a, b, name, out_dtype, out_slots=0, tm=2048, tn=1024, tk=2048):
    K, M = a.shape
    N = b.shape[1]
    tm, tk_ = _tile(M, tm), _tile(K, tk)
    if out_slots:
        Ns = N // out_slots
        tn = _tile(Ns, tn)
        npb = Ns // tn
        o_spec = pl.BlockSpec((None, tm, tn), lambda i, j, k: (j // npb, i, j % npb))
        out_shape = jax.ShapeDtypeStruct((out_slots, M, Ns), out_dtype)
    else:
        tn = _tile(N, tn)
        o_spec = pl.BlockSpec((tm, tn), lambda i, j, k: (i, j))
        out_shape = jax.ShapeDtypeStruct((M, N), out_dtype)
    return _mm(a, b, M=M, N=N, K=K, a_spec=pl.BlockSpec((tk_, tm), lambda i, j, k: (k, i)),
               b_spec=pl.BlockSpec((tk_, tn), lambda i, j, k: (k, j)), o_spec=o_spec, dims=_TN,
               tm=tm, tn=tn, tk=tk_, name=name, out_shapes=[out_shape],
               epi=None if out_dtype == F32 else (lambda acc: (acc,)))[0]


GATE_BLOCK = 512


def _split3(v):
    hi = v.astype(jnp.bfloat16)
    r1 = v - hi.astype(F32)
    mid = r1.astype(jnp.bfloat16)
    lo = (r1 - mid.astype(F32)).astype(jnp.bfloat16)
    return hi, mid, lo


def _exact_dot(v, tri):
    return functools.reduce(jnp.add, [jnp.dot(t, tri, preferred_element_type=F32) for t in _split3(v)])


def _gates_fwd(f_t, b, name):
    H, S = f_t.shape
    nb = _tile(S, GATE_BLOCK)
    inv_scale = HEAD_DIM ** 0.5

    def body(f_ref, b_ref, c_ref):
        upper = (lax.broadcasted_iota(jnp.int32, (nb, nb), 0)
                 <= lax.broadcasted_iota(jnp.int32, (nb, nb), 1)).astype(jnp.bfloat16)
        carry = jnp.zeros((H, 1), F32)
        for i in range(S // nb):
            z = f_ref[:, i * nb:(i + 1) * nb] + b_ref[...]
            logf = jnp.minimum(z, 0.0) - jnp.log1p(jnp.exp(-jnp.abs(z)))
            cs = _exact_dot(logf, upper) + carry
            for j, t in enumerate(_split3(cs * inv_scale)):
                c_ref[j, :, i * nb:(i + 1) * nb] = t
            carry = cs[:, nb - 1:nb]

    return pl.pallas_call(body, out_shape=jax.ShapeDtypeStruct((3, H, S), jnp.bfloat16), name=name,
                          compiler_params=_cparams())(f_t, b.reshape(H, 1))


def _gates_bwd(f_t, b, dcq, dck, name):
    H, S = f_t.shape
    nb = _tile(S, GATE_BLOCK)

    def body(f_ref, b_ref, dcq_ref, dck_ref, df_ref, dfc_ref, db_ref):
        lower = (lax.broadcasted_iota(jnp.int32, (nb, nb), 0)
                 >= lax.broadcasted_iota(jnp.int32, (nb, nb), 1)).astype(jnp.bfloat16)
        carry = jnp.zeros((H, 1), F32)
        db = jnp.zeros((H, 1), F32)
        for i in reversed(range(S // nb)):
            sl = slice(i * nb, (i + 1) * nb)
            dc = dcq_ref[:, sl] - dck_ref[:, sl]
            dlogf = _exact_dot(dc, lower) + carry
            carry = dlogf[:, 0:1]
            z = f_ref[:, sl] + b_ref[...]
            df = dlogf / (1.0 + jnp.exp(z))
            df_ref[:, sl] = df
            dfc_ref[:, sl] = df.astype(dfc_ref.dtype)
            db = db + jnp.sum(df, axis=1, keepdims=True)
        db_ref[...] = db

    return pl.pallas_call(
        body, out_shape=[jax.ShapeDtypeStruct((H, S), F32), jax.ShapeDtypeStruct((H, S), CDT),
                         jax.ShapeDtypeStruct((H, 1), F32)],
        name=name, compiler_params=_cparams())(f_t, b.reshape(H, 1), dcq, dck)


FOX_BLOCK = 1024


def _fox_bias_operands(csplit, name, bs=512):
    _, H, S = csplit.shape
    E = HEAD_DIM
    bs = _tile(S, bs)
    part = jnp.arange(3 * H)[:, None] // H
    head = jnp.arange(3 * H)[:, None] % H
    lane = jnp.arange(H * E)[None, :]
    place_q = (lane == head * E + part).astype(csplit.dtype)
    place_k = -(lane == head * E + 3 + part).astype(csplit.dtype)
    ones_q = ((lane % E >= 3) & (lane % E < 6)).astype(F32)
    ones_k = (lane % E < 3).astype(F32)

    def body(c_ref, pq_ref, pk_ref, oq_ref, ok_ref, qc_ref, kc_ref):
        c = c_ref[...]
        qc_ref[...] = (lax.dot_general(c, pq_ref[...], _TN, preferred_element_type=F32) + oq_ref[...]).astype(qc_ref.dtype)
        kc_ref[...] = (lax.dot_general(c, pk_ref[...], _TN, preferred_element_type=F32) + ok_ref[...]).astype(kc_ref.dtype)

    full = lambda a: pl.BlockSpec(a.shape, lambda i: (0, 0))
    out = pl.BlockSpec((bs, H * E), lambda i: (i, 0))
    return pl.pallas_call(
        body, grid=(S // bs,),
        in_specs=[pl.BlockSpec((3 * H, bs), lambda i: (0, i)), full(place_q), full(place_k), full(ones_q), full(ones_k)],
        out_specs=[out, out], out_shape=[jax.ShapeDtypeStruct((S, H * E), csplit.dtype)] * 2,
        name=name, compiler_params=_cparams(("parallel",)),
    )(csplit.reshape(3 * H, S), place_q, place_k, ones_q, ones_k)


def _fox_logits2(q_ref, qc_ref, k_ref, kc_ref, diag):
    q, k = q_ref[...], k_ref[...]
    qa = jnp.concatenate([q, qc_ref[...].astype(q.dtype)], axis=1)
    ka = jnp.concatenate([k, kc_ref[...].astype(k.dtype)], axis=1)
    s = lax.dot_general(qa, ka, _NT, preferred_element_type=F32) * (HEAD_DIM ** -0.5 * LOG2E)
    if diag:
        row = lax.broadcasted_iota(jnp.int32, s.shape, 0)
        col = lax.broadcasted_iota(jnp.int32, s.shape, 1)
        s = jnp.where(col <= row, s, NEG_INF)
    return s


def _fox_fwd(proj, qc, kc, H, name):
    S = proj.shape[0]
    E = HEAD_DIM
    blk = _tile(S, FOX_BLOCK)
    nq = S // blk

    def body(q_ref, qc_ref, k_ref, kc_ref, v_ref, o_ref, lse_ref, m_s, l_s, acc_s):
        qi, kj = pl.program_id(1), pl.program_id(2)

        @pl.when(kj == 0)
        def _():
            m_s[...] = jnp.full(m_s.shape, NEG_INF, F32)
            l_s[...] = jnp.zeros(l_s.shape, F32)
            acc_s[...] = jnp.zeros(acc_s.shape, F32)

        def step(diag):
            s = _fox_logits2(q_ref, qc_ref, k_ref, kc_ref, diag)
            m_prev = m_s[...]
            m_new = jnp.maximum(m_prev, jnp.max(s, axis=-1, keepdims=True))
            alpha = jnp.exp2(m_prev - m_new)
            p = jnp.exp2(s - m_new)
            l_s[...] = alpha * l_s[...] + jnp.sum(p, axis=-1, keepdims=True)
            acc_s[...] = alpha * acc_s[...] + jnp.dot(p.astype(CDT), v_ref[...], preferred_element_type=F32)
            m_s[...] = m_new

        pl.when(kj < qi)(lambda: step(False))
        pl.when(kj == qi)(lambda: step(True))

        @pl.when(kj == nq - 1)
        def _():
            o_ref[...] = acc_s[...] / l_s[...]
            lse_ref[...] = jnp.broadcast_to(m_s[...] + jnp.log2(l_s[...]), lse_ref.shape)

    qspec = lambda off: pl.BlockSpec((blk, E), lambda h, i, j: (i, off + h))
    kspec = lambda off: pl.BlockSpec((blk, E), lambda h, i, j: (jnp.minimum(j, i), off + h))
    return pl.pallas_call(
        body, grid=(H, nq, nq),
        in_specs=[qspec(0), qspec(0), kspec(H), kspec(0), kspec(2 * H)],
        out_specs=[qspec(0)] * 2,
        out_shape=[jax.ShapeDtypeStruct((S, H * E), F32)] * 2,
        scratch_shapes=[pltpu.VMEM((blk, 1), F32), pltpu.VMEM((blk, 1), F32), pltpu.VMEM((blk, E), F32)],
        name=name, compiler_params=_cparams(("parallel", "parallel", "arbitrary")),
    )(proj, qc, proj, kc, proj)


def _fox_bwd(proj, qc, kc, lse, o, do, H, name):
    S = proj.shape[0]
    E = HEAD_DIM
    blk = _tile(S, FOX_BLOCK)
    nq = S // blk
    scale = E ** -0.5

    def body(q_ref, qc_ref, k_ref, kc_ref, v_ref, lse_ref, o_ref, do_ref,
             dq_ref, dcq_ref, dk_ref, dv_ref, dck_ref, dq_s, dcq_s, dk_s, dv_s, dck_s):
        kj, qi = pl.program_id(1), pl.program_id(2)

        @pl.when(qi == 0)
        def _():
            dk_s[...] = jnp.zeros(dk_s.shape, F32)
            dv_s[...] = jnp.zeros(dv_s.shape, F32)
            dck_s[...] = jnp.zeros(dck_s.shape, F32)

        def step(diag):
            do = do_ref[...]
            doc = do.astype(CDT)
            delta = jnp.sum(do * o_ref[...], axis=-1, keepdims=True)
            p = jnp.exp2(_fox_logits2(q_ref, qc_ref, k_ref, kc_ref, diag) - lse_ref[:, 0:1])
            dp = lax.dot_general(doc, v_ref[...], _NT, preferred_element_type=F32)
            ds = p * (dp - delta)
            dss = ds * scale
            dck_s[...] += jnp.sum(ds, axis=0, keepdims=True)
            dv_s[...] += jnp.dot(p.T.astype(CDT), doc, preferred_element_type=F32)
            dk_s[...] += jnp.dot(dss.T.astype(CDT), q_ref[...], preferred_element_type=F32)
            dq_part = jnp.dot(dss.astype(CDT), k_ref[...], preferred_element_type=F32)
            dc_part = jnp.sum(ds, axis=-1, keepdims=True)
            rows = pl.ds(pl.multiple_of(qi * blk, blk), blk)

            @pl.when(kj == 0)
            def _():
                dq_s[rows, :] = dq_part
                dcq_s[rows, :] = dc_part

            @pl.when(kj > 0)
            def _():
                dq_s[rows, :] += dq_part
                dcq_s[rows, :] += dc_part

        pl.when(qi > kj)(lambda: step(False))
        pl.when(qi == kj)(lambda: step(True))

        @pl.when(qi == nq - 1)
        def _():
            dk_ref[...] = dk_s[...].astype(dk_ref.dtype)
            dv_ref[...] = dv_s[...].astype(dv_ref.dtype)
            dck_ref[...] = dck_s[...].reshape(dck_ref.shape)

        @pl.when((qi == nq - 1) & (kj == nq - 1))
        def _():
            dq_ref[...] = dq_s[...].astype(dq_ref.dtype)
            dcq_ref[...] = jnp.broadcast_to(dcq_s[...], dcq_ref.shape)

    qspec = lambda off: pl.BlockSpec((blk, E), lambda h, j, i: (jnp.maximum(i, j), off + h))
    kspec = lambda off: pl.BlockSpec((blk, E), lambda h, j, i: (j, off + h))
    head = pl.BlockSpec((S, E), lambda h, j, i: (0, h))
    return pl.pallas_call(
        body, grid=(H, nq, nq),
        in_specs=[qspec(0), qspec(0), kspec(H), kspec(0), kspec(2 * H), qspec(0), qspec(0), qspec(0)],
        out_specs=[head, head, kspec(0), kspec(0), pl.BlockSpec((1, 1, blk), lambda h, j, i: (h, 0, j))],
        out_shape=[jax.ShapeDtypeStruct((S, H * E), CDT), jax.ShapeDtypeStruct((S, H * E), F32),
                   jax.ShapeDtypeStruct((S, H * E), CDT), jax.ShapeDtypeStruct((S, H * E), CDT),
                   jax.ShapeDtypeStruct((H, 1, S), F32)],
        scratch_shapes=[pltpu.VMEM((S, E), F32), pltpu.VMEM((S, 1), F32), pltpu.VMEM((blk, E), F32),
                        pltpu.VMEM((blk, E), F32), pltpu.VMEM((1, blk), F32)],
        name=name, compiler_params=_cparams(("parallel", "arbitrary", "arbitrary")),
    )(proj, qc, proj, kc, proj, lse, o, do)


DIL_SLAB = 16 * DIL_BLOCK
DIL_UNROLL = 8


def _rel_bucket(dist):
    max_exact = REL_BUCKETS // 2
    d = jnp.maximum(dist.astype(F32), 1.0)
    large = max_exact + (jnp.log(d / max_exact) / jnp.log(jnp.float32(REL_MAX_DISTANCE / max_exact))
                         * (REL_BUCKETS - max_exact)).astype(jnp.int32)
    large = jnp.minimum(large, REL_BUCKETS - 1)
    return jnp.where(dist < max_exact, dist, large)


def _bucket_table():
    i = jnp.arange(DIL_BLOCK)[:, None]
    j = jnp.arange(2 * DIL_BLOCK)[None, :]
    rel = DIL_BLOCK + i - j
    tabs = [_rel_bucket(jnp.clip(rel, 0, w // d) * d) for w, d in DIL_PATTERNS]
    return jnp.stack(tabs).astype(jnp.int32)


def _bias_table(rel_bias, buckets, name):
    P = buckets.shape[0]
    H = rel_bias.shape[1]

    def body(rb_ref, bk_ref, out_ref):
        h = pl.program_id(1)
        bk = bk_ref[0]
        val = jnp.zeros(bk.shape, F32)
        for b in range(REL_BUCKETS):
            val = jnp.where(bk == b, rb_ref[b, h], val)
        out_ref[0, 0] = val

    return pl.pallas_call(
        body, grid=(P, H),
        in_specs=[pl.BlockSpec(memory_space=pltpu.SMEM),
                  pl.BlockSpec((1, DIL_BLOCK, 2 * DIL_BLOCK), lambda p, h: (p, 0, 0))],
        out_specs=pl.BlockSpec((1, 1, DIL_BLOCK, 2 * DIL_BLOCK), lambda p, h: (p, h, 0, 0)),
        out_shape=jax.ShapeDtypeStruct((P, H, DIL_BLOCK, 2 * DIL_BLOCK), F32),
        name=name, compiler_params=_cparams(("parallel", "parallel")),
    )(rel_bias, buckets)


def _bias_table_bwd(dbias, buckets, name):
    P, H = dbias.shape[:2]

    def body(db_ref, bk_ref, out_ref):
        lane = lax.broadcasted_iota(jnp.int32, (1, REL_BUCKETS), 1)
        acc = jnp.zeros((1, REL_BUCKETS), F32)
        bk = bk_ref[...]
        db = db_ref[:, 0]
        for b in range(REL_BUCKETS):
            tot = jnp.sum(jnp.where(bk == b, db, 0.0))
            acc = jnp.where(lane == b, tot, acc)
        out_ref[0] = acc

    return pl.pallas_call(
        body, grid=(H,),
        in_specs=[pl.BlockSpec((P, 1, DIL_BLOCK, 2 * DIL_BLOCK), lambda h: (0, h, 0, 0)),
                  pl.BlockSpec((P, DIL_BLOCK, 2 * DIL_BLOCK), lambda h: (0, 0, 0))],
        out_specs=pl.BlockSpec((1, 1, REL_BUCKETS), lambda h: (h, 0, 0)),
        out_shape=jax.ShapeDtypeStruct((H, 1, REL_BUCKETS), F32),
        name=name, compiler_params=_cparams(("parallel",)),
    )(dbias, buckets)


def _bdot(a, b, contract_b):
    return lax.dot_general(a, b, (((2,), (contract_b,)), ((0,), (0,))), preferred_element_type=F32)


def _dil_units(first, d):
    units = []
    for t in range(DIL_UNROLL):
        u = first + t
        sg = u // d
        units.append((sg, sg * (DIL_BLOCK * d) + u % d))
    return units


def _dil_rows(ref, starts, d, dtype=None):
    t = jnp.stack([ref[pl.ds(s, DIL_BLOCK, stride=d), :] for s in starts])
    return t if dtype is None else t.astype(dtype)


def _dil_keys(ref, units, d):
    B, SL = DIL_BLOCK, DIL_SLAB
    return jnp.stack([jnp.concatenate([ref[pl.ds(SL + b - B * d, B, stride=d), :], ref[pl.ds(SL + b, B, stride=d), :]],
                                      axis=0) for _, b in units]).astype(CDT)


def _dil_logits(q, keys, bias_pc, first, d, has_before):
    T, B = q.shape[0], DIL_BLOCK
    ii = lax.broadcasted_iota(jnp.int32, (T, B, 2 * B), 1)
    jj = lax.broadcasted_iota(jnp.int32, (T, B, 2 * B), 2)
    sg = (first + lax.broadcasted_iota(jnp.int32, (T, B, 2 * B), 0)) // d
    mask = (jj >= ii) & (jj <= ii + B) & ((jj >= B) | (sg > 0) | has_before)
    return jnp.where(mask, _bdot(q, keys, 2) * HEAD_DIM ** -0.5 + bias_pc[None], NEG_INF)


def _dil_specs(H, n_slabs):
    E, SL = DIL_BLOCK, DIL_SLAB
    cur = lambda off: pl.BlockSpec((SL, E), lambda h, g: (g, off + h))
    prev = lambda off: pl.BlockSpec((SL, E), lambda h, g: (jnp.maximum(g - 1, 0), off + h))
    nxt = lambda off: pl.BlockSpec((SL, E), lambda h, g: (jnp.minimum(g + 1, n_slabs - 1), off + h))
    bias = pl.BlockSpec((len(DIL_PATTERNS), 1, E, 2 * E), lambda h, g: (0, h, 0, 0))
    return cur, prev, nxt, bias


def _dil_fwd(proj, bias, H, name):
    S = proj.shape[0]
    E = B = DIL_BLOCK
    SL = DIL_SLAB
    P = len(DIL_PATTERNS)
    assert S % SL == 0
    n_slabs = S // SL

    def body(q_ref, kc_ref, kp_ref, vc_ref, vp_ref, b_ref, y_ref, lse_ref, kj, vj, o_s, l_s):
        g = pl.program_id(1)
        kj[0:SL, :] = kp_ref[...]
        kj[SL:2 * SL, :] = kc_ref[...]
        vj[0:SL, :] = vp_ref[...]
        vj[SL:2 * SL, :] = vc_ref[...]
        for p, (_, d) in enumerate(DIL_PATTERNS):
            def batch(it, carry, p=p, d=d):
                first = it * DIL_UNROLL
                units = _dil_units(first, d)
                q = _dil_rows(q_ref, [b for _, b in units], d, CDT)
                s = _dil_logits(q, _dil_keys(kj, units, d), b_ref[p, 0], first, d, g > 0)
                m = jnp.max(s, axis=-1, keepdims=True)
                e = jnp.exp(s - m)
                ssum = jnp.sum(e, axis=-1, keepdims=True)
                o = _bdot(e.astype(CDT), _dil_keys(vj, units, d), 1) / ssum
                lse = jnp.broadcast_to(m + jnp.log(ssum), o.shape)
                for t, (_, b) in enumerate(units):
                    o_s[p, pl.ds(b, B, stride=d), :] = o[t]
                    l_s[p, pl.ds(b, B, stride=d), :] = lse[t]
                return carry

            lax.fori_loop(0, SL // B // DIL_UNROLL, batch, 0)
        ls = [l_s[p] for p in range(P)]
        m = functools.reduce(jnp.maximum, ls)
        w = [jnp.exp(l - m) for l in ls]
        tot = functools.reduce(jnp.add, w)
        y_ref[...] = functools.reduce(jnp.add, [(w[p] / tot) * o_s[p] for p in range(P)])
        lse_ref[...] = m + jnp.log(tot)

    cur, prev, _, bspec = _dil_specs(H, n_slabs)
    return pl.pallas_call(
        body, grid=(H, n_slabs),
        in_specs=[cur(0), cur(H), prev(H), cur(2 * H), prev(2 * H), bspec],
        out_specs=[cur(0), cur(0)],
        out_shape=[jax.ShapeDtypeStruct((S, H * E), F32)] * 2,
        scratch_shapes=[pltpu.VMEM((2 * SL, E), F32), pltpu.VMEM((2 * SL, E), F32),
                        pltpu.VMEM((P, SL, E), F32), pltpu.VMEM((P, SL, E), F32)],
        name=name, compiler_params=_cparams(("parallel", "parallel")),
    )(proj, proj, proj, proj, proj, bias)


def _dil_bwd(proj, bias, y, dy, lse, H, name):
    S = proj.shape[0]
    E = B = DIL_BLOCK
    SL = DIL_SLAB
    P = len(DIL_PATTERNS)
    assert S % SL == 0
    n_slabs = S // SL
    scale = E ** -0.5

    def body(q_ref, kc_ref, kp_ref, vc_ref, vp_ref, b_ref, y_ref, dy_ref, lse_ref, qn_ref, yn_ref, dyn_ref, lsen_ref,
             dq_ref, dk_ref, dv_ref, db_ref, kj, vj, dq_s, dk_s, dv_s, dl_s, dln_s):
        g = pl.program_id(1)
        kj[0:SL, :] = kp_ref[...]
        kj[SL:2 * SL, :] = kc_ref[...]
        vj[0:SL, :] = vp_ref[...]
        vj[SL:2 * SL, :] = vc_ref[...]
        dq_s[...] = jnp.zeros(dq_s.shape, F32)
        dk_s[...] = jnp.zeros(dk_s.shape, F32)
        dv_s[...] = jnp.zeros(dv_s.shape, F32)
        dl_s[...] = jnp.broadcast_to(jnp.sum(dy_ref[...] * y_ref[...], axis=-1, keepdims=True), (SL, E))
        dln_s[...] = jnp.broadcast_to(jnp.sum(dyn_ref[...] * yn_ref[...], axis=-1, keepdims=True), (SL, E))

        @pl.when(g == 0)
        def _():
            db_ref[...] = jnp.zeros(db_ref.shape, F32)

        tr = lambda t: jnp.swapaxes(t, 1, 2).astype(CDT)
        for p, (_, d) in enumerate(DIL_PATTERNS):
            def batch(it, carry, p=p, d=d):
                first = it * DIL_UNROLL
                units = _dil_units(first, d)
                starts = [b for _, b in units]
                q = _dil_rows(q_ref, starts, d, CDT)
                dyc = _dil_rows(dy_ref, starts, d, CDT)
                keys, vals = _dil_keys(kj, units, d), _dil_keys(vj, units, d)
                s = _dil_logits(q, keys, b_ref[p, 0], first, d, g > 0)
                e = jnp.exp(s - _dil_rows(lse_ref, starts, d)[:, :, 0:1])
                ds = e * (_bdot(dyc, vals, 2) - _dil_rows(dl_s, starts, d)[:, :, 0:1])
                dss = ds * scale
                dq = _bdot(dss.astype(CDT), keys, 1)
                dk = _bdot(tr(dss), q, 1)
                dv = _bdot(tr(e), dyc, 1)
                for t, (sg, b) in enumerate(units):
                    rows = pl.ds(b, B, stride=d)
                    dq_s[rows, :] += dq[t]
                    dk_s[rows, :] += dk[t, B:]
                    dv_s[rows, :] += dv[t, B:]

                    @pl.when(sg > 0)
                    def _(t=t, b=b):
                        before = pl.ds(b - B * d, B, stride=d)
                        dk_s[before, :] += dk[t, :B]
                        dv_s[before, :] += dv[t, :B]

                db_ref[p, 0] += jnp.sum(ds, axis=0)
                return carry

            lax.fori_loop(0, SL // B // DIL_UNROLL, batch, 0)

            n_after = min(d, DIL_UNROLL)

            def after(it, carry, p=p, d=d, n_after=n_after):
                starts = [it * n_after + t for t in range(n_after)]
                kstarts = [SL - B * d + s for s in starts]
                q = _dil_rows(qn_ref, starts, d, CDT)
                dyc = _dil_rows(dyn_ref, starts, d, CDT)
                k, v = _dil_rows(kc_ref, kstarts, d, CDT), _dil_rows(vc_ref, kstarts, d, CDT)
                ii = lax.broadcasted_iota(jnp.int32, (n_after, B, B), 1)
                jj = lax.broadcasted_iota(jnp.int32, (n_after, B, B), 2)
                s = jnp.where((jj >= ii) & (g < n_slabs - 1), _bdot(q, k, 2) * scale + b_ref[p, 0][:, :B][None], NEG_INF)
                e = jnp.exp(s - _dil_rows(lsen_ref, starts, d)[:, :, 0:1])
                ds = e * (_bdot(dyc, v, 2) - _dil_rows(dln_s, starts, d)[:, :, 0:1])
                dk = _bdot(tr(ds * scale), q, 1)
                dv = _bdot(tr(e), dyc, 1)
                for t, ks in enumerate(kstarts):
                    dk_s[pl.ds(ks, B, stride=d), :] += dk[t]
                    dv_s[pl.ds(ks, B, stride=d), :] += dv[t]
                return carry

            lax.fori_loop(0, d // n_after, after, 0)

        dq_ref[...] = dq_s[...].astype(dq_ref.dtype)
        dk_ref[...] = dk_s[...].astype(dk_ref.dtype)
        dv_ref[...] = dv_s[...].astype(dv_ref.dtype)

    cur, prev, nxt, bspec = _dil_specs(H, n_slabs)
    slab = pltpu.VMEM((SL, E), F32)
    return pl.pallas_call(
        body, grid=(H, n_slabs),
        in_specs=[cur(0), cur(H), prev(H), cur(2 * H), prev(2 * H), bspec, cur(0), cur(0), cur(0),
                  nxt(0), nxt(0), nxt(0), nxt(0)],
        out_specs=[cur(0), cur(0), cur(0), bspec],
        out_shape=[jax.ShapeDtypeStruct((S, H * E), CDT)] * 3 + [jax.ShapeDtypeStruct((P, H, B, 2 * B), F32)],
        scratch_shapes=[pltpu.VMEM((2 * SL, E), F32), pltpu.VMEM((2 * SL, E), F32), slab, slab, slab, slab, slab],
        name=name, compiler_params=_cparams(("parallel", "arbitrary")),
    )(proj, proj, proj, proj, proj, bias, y, dy, lse, proj, y, dy, lse)


def _adamw(w, g, m, v, name, br=128):
    R, C = w.shape
    br = br if R % br == 0 else R

    def body(w_ref, g_ref, m_ref, v_ref, g_out, d_ref, nm_ref, nv_ref):
        g_ = g_ref[...]
        g_out[...] = g_
        m_ = ADAM_B1 * m_ref[...] + (1.0 - ADAM_B1) * g_
        v_ = ADAM_B2 * v_ref[...] + (1.0 - ADAM_B2) * jnp.square(g_)
        m_hat = m_ / (1.0 - ADAM_B1 ** ADAM_STEP)
        v_hat = v_ / (1.0 - ADAM_B2 ** ADAM_STEP)
        d_ref[...] = -ADAM_LR * (m_hat / (jnp.sqrt(v_hat) + ADAM_EPS) + ADAM_WD * w_ref[...])
        nm_ref[...] = m_
        nv_ref[...] = v_

    blk = pl.BlockSpec((br, C), lambda i: (i, 0))
    return pl.pallas_call(
        body, grid=(R // br,), in_specs=[blk] * 4, out_specs=[blk] * 4,
        out_shape=[jax.ShapeDtypeStruct((R, C), F32)] * 4,
        name=name, compiler_params=_cparams(("parallel",)),
    )(w, g, m, v)


_HBM = pl.BlockSpec(memory_space=pltpu.HBM)
_SEM = pl.BlockSpec(memory_space=pltpu.SEMAPHORE)
_ANY = pl.BlockSpec(memory_space=pl.ANY)
_VMEM = pl.BlockSpec(memory_space=pltpu.VMEM)
_TOKEN = jax.ShapeDtypeStruct((8, 128), F32)


def _split_params():
    return pltpu.CompilerParams(has_side_effects=pltpu.SideEffectType.DATAFLOW_SIDE_EFFECTING)


def _place():
    x, y, c = lax.axis_index("x"), lax.axis_index("y"), lax.axis_index("c")
    chips = [(1 - x, y), (x, 1 - y), (1 - x, 1 - y)]
    return x, y, c, chips


def _tie(v, tokens, name):
    flat = v.reshape(1, -1)

    def body(v_ref, *rest):
        rest[-1][...] = v_ref[...]

    return pl.pallas_call(body, in_specs=[_VMEM] + [_ANY] * len(tokens), out_specs=_VMEM,
                          out_shape=jax.ShapeDtypeStruct(flat.shape, flat.dtype), name=name,
                          compiler_params=_cparams())(flat, *tokens).reshape(v.shape)


def _row_block(R, pref=256):
    return _tile(R, pref) if R % 128 == 0 else R


def _slot():
    return 2 * lax.axis_index("x") + lax.axis_index("y")


def _cast_into_slot(w, layer, name):
    _, R, C = w.shape
    br = _row_block(R)

    def body(w_ref, out_ref):
        out_ref[...] = w_ref[...].astype(out_ref.dtype)

    return pl.pallas_call(
        body, grid=(R // br,),
        in_specs=[pl.BlockSpec((None, br, C), lambda i: (layer, i, 0))],
        out_specs=pl.BlockSpec((None, br, C), lambda i: (_slot(), i, 0)),
        out_shape=jax.ShapeDtypeStruct((N_CHIPS, R, C), CDT),
        name=name, compiler_params=_cparams(("parallel",)),
    )(w)


def _gather_copies(src_ref, dst_ref, send_sems, recv_sems, incoming):
    Rh = src_ref.shape[1] // 2
    x, y, c, chips = _place()
    slot = 2 * x + y

    def half(ref, s, hf):
        return ref.at[s, pl.ds(hf * Rh, Rh), :]

    copies = []
    for j, (cx, cy) in enumerate(chips):
        for e in range(2):
            copies.append(pltpu.make_async_remote_copy(
                src_ref=half(src_ref, slot, c), dst_ref=half(dst_ref, 2 * cx + cy, e) if incoming else half(dst_ref, slot, c),
                send_sem=send_sems.at[2 * j + e], recv_sem=recv_sems.at[2 * j + (e if incoming else c)],
                device_id=(cx, cy, e), device_id_type=MESH))
    return copies


def _gather_start(buf, after, name):
    n_after = len(after)

    def body(*refs):
        buf_ref = refs[0]
        send_sems, recv_sems, out_ref, token = refs[1 + n_after:]
        for cp in _gather_copies(buf_ref, out_ref, send_sems, recv_sems, incoming=False):
            cp.start()
        token[...] = jnp.zeros(token.shape, token.dtype)

    return pl.pallas_call(
        body, in_specs=[_HBM] + [_ANY] * n_after, out_specs=(_SEM, _SEM, _HBM, _VMEM),
        out_shape=(pltpu.SemaphoreType.DMA((6,)), pltpu.SemaphoreType.DMA((6,)), pltpu.HBM(buf.shape, buf.dtype), _TOKEN),
        input_output_aliases={0: 2}, name=name, compiler_params=_split_params(),
    )(pltpu.with_memory_space_constraint(buf, pltpu.HBM), *after)


def _gather_wait(send_sems, recv_sems, buf, after, name):
    def body(buf_ref, send_sems, recv_sems, after_ref, out_ref):
        for cp in _gather_copies(buf_ref, out_ref, send_sems, recv_sems, incoming=False):
            cp.wait_send()
        for cp in _gather_copies(buf_ref, out_ref, send_sems, recv_sems, incoming=True):
            cp.wait_recv()

    return pl.pallas_call(
        body, in_specs=[_HBM, _SEM, _SEM, _ANY], out_specs=_HBM, out_shape=pltpu.HBM(buf.shape, buf.dtype),
        input_output_aliases={0: 0}, name=name, compiler_params=_split_params(),
    )(buf, send_sems, recv_sems, after)


def _relay_copies(src_ref, dst_ref, send_sems, recv_sems, stage, incoming):
    Rh = src_ref.shape[1] // 2
    x, y, c, chips = _place()
    copies = []
    for j, (cx, cy) in enumerate(chips):
        if stage == 0:
            src_slot, src_half, peer = 2 * x + y, c, (cx, cy, c)
            dst_slot, dst_half = (2 * cx + cy, c) if incoming else (src_slot, c)
        else:
            src_slot, src_half, peer = 2 * cx + cy, c, (x, y, 1 - c)
            dst_slot, dst_half = src_slot, (1 - c if incoming else c)
        copies.append(pltpu.make_async_remote_copy(
            src_ref=src_ref.at[src_slot, pl.ds(src_half * Rh, Rh), :],
            dst_ref=dst_ref.at[dst_slot, pl.ds(dst_half * Rh, Rh), :],
            send_sem=send_sems.at[j], recv_sem=recv_sems.at[j], device_id=peer, device_id_type=MESH))
    return copies


def _relay_start(buf, after, name):
    n_after = len(after)

    def body(*refs):
        buf_ref = refs[0]
        send_sems, recv_sems, out_ref, token = refs[1 + n_after:]
        for cp in _relay_copies(buf_ref, out_ref, send_sems, recv_sems, 0, incoming=False):
            cp.start()
        token[...] = jnp.zeros(token.shape, token.dtype)

    return pl.pallas_call(
        body, in_specs=[_HBM] + [_ANY] * n_after, out_specs=(_SEM, _SEM, _HBM, _VMEM),
        out_shape=(pltpu.SemaphoreType.DMA((3,)), pltpu.SemaphoreType.DMA((3,)), pltpu.HBM(buf.shape, buf.dtype), _TOKEN),
        input_output_aliases={0: 2}, name=name, compiler_params=_split_params(),
    )(pltpu.with_memory_space_constraint(buf, pltpu.HBM), *after)


def _relay_pass(send_sems, recv_sems, buf, after, name):
    def body(buf_ref, send0, recv0, after_ref, send1, recv1, out_ref):
        for cp in _relay_copies(buf_ref, out_ref, send0, recv0, 0, incoming=False):
            cp.wait_send()
        for cp in _relay_copies(buf_ref, out_ref, send0, recv0, 0, incoming=True):
            cp.wait_recv()
        for cp in _relay_copies(out_ref, out_ref, send1, recv1, 1, incoming=False):
            cp.start()

    return pl.pallas_call(
        body, in_specs=[_HBM, _SEM, _SEM, _ANY], out_specs=(_SEM, _SEM, _HBM),
        out_shape=(pltpu.SemaphoreType.DMA((3,)), pltpu.SemaphoreType.DMA((3,)), pltpu.HBM(buf.shape, buf.dtype)),
        input_output_aliases={0: 2}, name=name, compiler_params=_split_params(),
    )(buf, send_sems, recv_sems, after)


def _relay_wait(send_sems, recv_sems, buf, name):
    def body(buf_ref, send1, recv1, out_ref):
        for cp in _relay_copies(buf_ref, out_ref, send1, recv1, 1, incoming=False):
            cp.wait_send()
        for cp in _relay_copies(buf_ref, out_ref, send1, recv1, 1, incoming=True):
            cp.wait_recv()

    return pl.pallas_call(
        body, in_specs=[_HBM, _SEM, _SEM], out_specs=_HBM, out_shape=pltpu.HBM(buf.shape, buf.dtype),
        input_output_aliases={0: 0}, name=name, compiler_params=_split_params(),
    )(buf, send_sems, recv_sems)


def _scatter_copies(g_ref, land_ref, send_sems, recv_sems, incoming):
    Rh = g_ref.shape[1] // 2
    x, y, c, _ = _place()
    me = 4 * x + 2 * y + c
    copies = []
    for k in range(1, N_DEV):
        px, py, pc = (x + (k >> 2)) % 2, (y + ((k >> 1) & 1)) % 2, (c + (k & 1)) % 2
        copies.append(pltpu.make_async_remote_copy(
            src_ref=g_ref.at[2 * px + py, pl.ds(pc * Rh, Rh), :],
            dst_ref=land_ref.at[4 * px + 2 * py + pc if incoming else me],
            send_sem=send_sems.at[k - 1], recv_sem=recv_sems.at[k - 1], device_id=(px, py, pc), device_id_type=MESH))
    return copies


def _scatter_start(g, name):
    ns, R, C = g.shape

    def body(g_ref, land_ref, send_sems, recv_sems, g_thru, land_thru, token):
        for cp in _scatter_copies(g_ref, land_thru, send_sems, recv_sems, incoming=False):
            cp.start()
        token[...] = jnp.zeros(token.shape, token.dtype)

    land = lax.empty((N_DEV, R // 2, C), g.dtype)
    n = N_DEV - 1
    return pl.pallas_call(
        body, in_specs=[_HBM, _HBM], out_specs=(_SEM, _SEM, _HBM, _HBM, _VMEM),
        out_shape=(pltpu.SemaphoreType.DMA((n,)), pltpu.SemaphoreType.DMA((n,)), pltpu.HBM(g.shape, g.dtype),
                   pltpu.HBM(land.shape, land.dtype), _TOKEN),
        input_output_aliases={0: 2, 1: 3}, name=name, compiler_params=_split_params(),
    )(pltpu.with_memory_space_constraint(g, pltpu.HBM), pltpu.with_memory_space_constraint(land, pltpu.HBM))


def _scatter_wait(send_sems, recv_sems, g, land, after, name):
    def body(g_ref, land_ref, send_sems, recv_sems, after_ref, g_out, land_out):
        for cp in _scatter_copies(g_ref, land_out, send_sems, recv_sems, incoming=False):
            cp.wait_send()
        for cp in _scatter_copies(g_ref, land_out, send_sems, recv_sems, incoming=True):
            cp.wait_recv()

    return pl.pallas_call(
        body, in_specs=[_HBM, _HBM, _SEM, _SEM, _ANY], out_specs=(_HBM, _HBM),
        out_shape=(pltpu.HBM(g.shape, g.dtype), pltpu.HBM(land.shape, land.dtype)),
        input_output_aliases={0: 0, 1: 1}, name=name, compiler_params=_split_params(),
    )(g, land, send_sems, recv_sems, after)


def _device_sum(land, g, layer, n_layers, prev, name):
    nd, Rh, C = land.shape
    br = _row_block(Rh)
    nb = Rh // br
    core = lambda: lax.axis_index("c")
    me = lambda: 2 * _slot() + core()

    def body(*refs):
        own = refs[nd][...]
        acc = None
        for d in range(nd):
            t = jnp.where(me() == d, own, refs[d][...]).astype(F32)
            acc = t if acc is None else acc + t
        refs[-1][...] = acc

    def piece(d):
        return pl.BlockSpec((None, br, C), lambda i: (jnp.where(me() == d, (d + 1) % nd, d), i, 0))

    ins = [land] * nd + [g] + ([prev] if prev is not None else [])
    return pl.pallas_call(
        body, grid=(nb,),
        in_specs=[piece(d) for d in range(nd)]
        + [pl.BlockSpec((None, br, C), lambda i: (_slot(), core() * nb + i, 0))]
        + ([_ANY] if prev is not None else []),
        out_specs=pl.BlockSpec((None, br, C), lambda i: (layer, core() * nb + i, 0)),
        out_shape=jax.ShapeDtypeStruct((n_layers, 2 * Rh, C), F32),
        input_output_aliases={nd + 1: 0} if prev is not None else {},
        name=name, compiler_params=_cparams(("parallel",)),
    )(*ins)


def _join_copy(src_ref, dst_ref, layer, send_sem, recv_sem, incoming):
    Rh = src_ref.shape[1] // 2
    x, y, c, _ = _place()
    mine, other = pl.ds(c * Rh, Rh), pl.ds((1 - c) * Rh, Rh)
    return pltpu.make_async_remote_copy(src_ref=src_ref.at[layer, mine, :],
                                        dst_ref=dst_ref.at[layer, other if incoming else mine, :],
                                        send_sem=send_sem, recv_sem=recv_sem, device_id=(x, y, 1 - c),
                                        device_id_type=MESH)


def _join_start(g, layer, name):
    def body(g_ref, send_sem, recv_sem, out_ref, token):
        _join_copy(g_ref, out_ref, layer, send_sem, recv_sem, incoming=False).start()
        token[...] = jnp.zeros(token.shape, token.dtype)

    return pl.pallas_call(
        body, in_specs=[_HBM], out_specs=(_SEM, _SEM, _HBM, _VMEM),
        out_shape=(pltpu.SemaphoreType.DMA(()), pltpu.SemaphoreType.DMA(()), pltpu.HBM(g.shape, g.dtype), _TOKEN),
        input_output_aliases={0: 2}, name=name, compiler_params=_split_params(),
    )(pltpu.with_memory_space_constraint(g, pltpu.HBM))


def _join_wait(send_sem, recv_sem, g, layer, after, name):
    def body(g_ref, send_sem, recv_sem, after_ref, out_ref):
        _join_copy(g_ref, out_ref, layer, send_sem, recv_sem, incoming=False).wait_send()
        _join_copy(g_ref, out_ref, layer, send_sem, recv_sem, incoming=True).wait_recv()

    return pl.pallas_call(
        body, in_specs=[_HBM, _SEM, _SEM, _ANY], out_specs=_HBM, out_shape=pltpu.HBM(g.shape, g.dtype),
        input_output_aliases={0: 0}, name=name, compiler_params=_split_params(),
    )(g, send_sem, recv_sem, after)


def _all_reduce_small(v, name):
    rows, cols = v.shape

    def body(v_ref, out_ref, buf, send_sems, recv_sems):
        x, y, c, _ = _place()
        me = 4 * x + 2 * y + c
        buf[me] = v_ref[...]
        peers = []
        for k in range(1, N_DEV):
            px, py, pc = (x + (k >> 2)) % 2, (y + ((k >> 1) & 1)) % 2, (c + (k & 1)) % 2
            peers.append((px, py, pc))
        sends = []
        for k, peer in enumerate(peers):
            cp = pltpu.make_async_remote_copy(src_ref=v_ref, dst_ref=buf.at[me], send_sem=send_sems.at[k],
                                              recv_sem=recv_sems.at[k], device_id=peer, device_id_type=MESH)
            cp.start()
            sends.append(cp)
        for k, (px, py, pc) in enumerate(peers):
            pltpu.make_async_remote_copy(src_ref=v_ref, dst_ref=buf.at[4 * px + 2 * py + pc], send_sem=send_sems.at[k],
                                         recv_sem=recv_sems.at[k], device_id=(px, py, pc),
                                         device_id_type=MESH).wait_recv()
        for cp in sends:
            cp.wait_send()
        acc = buf[0]
        for i in range(1, N_DEV):
            acc = acc + buf[i]
        out_ref[...] = acc

    vmem = pl.BlockSpec(memory_space=pltpu.VMEM)
    return pl.pallas_call(
        body, in_specs=[vmem], out_specs=vmem, out_shape=jax.ShapeDtypeStruct((rows, cols), F32),
        scratch_shapes=[pltpu.VMEM((N_DEV, rows, cols), F32), pltpu.SemaphoreType.DMA((N_DEV - 1,)),
                        pltpu.SemaphoreType.DMA((N_DEV - 1,))],
        name=name, compiler_params=pltpu.CompilerParams(),
    )(v)


def _reduce_scatter_sum(started, after, layer, n_layers, prev, tag):
    send_sems, recv_sems, g, land, _ = started
    g, land = _scatter_wait(send_sems, recv_sems, g, land, after, f"rs_wait_{tag}")
    f = _device_sum(land, g, layer, n_layers, prev, f"rs_sum_{tag}")
    return _join_start(f, layer, f"rs_join_start_{tag}")


def _split_w_in(wg, Hf, name):
    ns, D, cols = wg.shape
    a = 3 * Hf * HEAD_DIM
    n6 = ns * cols - Hf
    br = _row_block(D)

    def body(w_ref, w6_ref, wf_ref):
        nat = jnp.concatenate([w_ref[s] for s in range(ns)], axis=1)
        w6_ref[...] = jnp.concatenate([nat[:, :a], nat[:, a + Hf:]], axis=1)
        wf_ref[...] = nat[:, a:a + Hf]

    w6, wf = pl.pallas_call(
        body, grid=(D // br,), in_specs=[pl.BlockSpec((ns, br, cols), lambda i: (0, i, 0))],
        out_specs=[pl.BlockSpec((br, n6), lambda i: (i, 0)), pl.BlockSpec((br, Hf), lambda i: (i, 0))],
        out_shape=[jax.ShapeDtypeStruct((D, n6), wg.dtype), jax.ShapeDtypeStruct((D, Hf), wg.dtype)],
        name=name, compiler_params=_cparams(("parallel",)),
    )(wg)
    return w6, wf.T


def _join_dw_in(dw6, dwf_t, Hf, name):
    D, n6 = dw6.shape
    a = 3 * Hf * HEAD_DIM
    cols = (n6 + Hf) // N_CHIPS
    br = _row_block(D)

    def body(w6_ref, wf_ref, out_ref):
        w6 = w6_ref[...]
        nat = jnp.concatenate([w6[:, :a], wf_ref[...], w6[:, a:]], axis=1)
        for s in range(N_CHIPS):
            out_ref[s] = nat[:, s * cols:(s + 1) * cols]

    return pl.pallas_call(
        body, grid=(D // br,),
        in_specs=[pl.BlockSpec((br, n6), lambda i: (i, 0)), pl.BlockSpec((br, Hf), lambda i: (i, 0))],
        out_specs=pl.BlockSpec((N_CHIPS, br, cols), lambda i: (0, i, 0)),
        out_shape=jax.ShapeDtypeStruct((N_CHIPS, D, cols), dw6.dtype),
        name=name, compiler_params=_cparams(("parallel",)),
    )(dw6, dwf_t.T.astype(dw6.dtype))


def _tied(v, tokens, name):
    return _tie(v, tokens, name) if tokens else v


def _layer_fwd(x, p, weight, bias, tokens, tag):
    Hf, Hd = p["forget_b"].shape[0], bias.shape[1]
    h1 = _rms_fwd(x, _tied(p["norm1_g"], tokens, f"tie_norm1_{tag}"), f"norm1_{tag}")
    w6, wf_t = _split_w_in(weight("w_in", h1), Hf, f"split_w_in_{tag}")
    n_a = 3 * Hf * HEAD_DIM
    proj_a = _mm_nn(h1, w6, f"proj_a_{tag}", [CDT], epi=lambda acc: (acc,), b_cols=(0, n_a))[0]
    proj_b = _mm_nn(h1, w6, f"proj_b_{tag}", [F32], b_cols=(n_a, w6.shape[1] - n_a))[0]
    f_t = _mm_nt(wf_t, h1, f"fproj_{tag}", [F32])[0]
    qc, kc = _fox_bias_operands(_gates_fwd(f_t, p["forget_b"], f"gates_{tag}"), f"fox_operands_{tag}")
    y_a, lse_a = _fox_fwd(proj_a, qc, kc, Hf, f"fox_{tag}")
    y_b, lse_b = _dil_fwd(proj_b, bias, Hd, f"dil_{tag}")
    mixed = _pair_norm_fwd(y_a, y_b, p["outnorm_a_g"], p["outnorm_b_g"], f"norm_ab_{tag}")
    w_out = weight("w_out", mixed)
    w_out = w_out.reshape(-1, w_out.shape[2])
    x1 = _mm_nn(mixed, w_out, f"attn_out_{tag}", [F32], extras=[x])[0]
    h2 = _rms_fwd(x1, p["norm2_g"], f"norm2_{tag}")
    w_mi = weight("w_mlp_in", h2)
    u, act = _mm_nn(h2, w_mi, f"mlp_in_{tag}", [CDT, CDT], b_slots=True,
                    epi=lambda acc: (acc, jnp.square(jnp.maximum(acc, 0.0))))
    w_mo = weight("w_mlp_out", act)
    w_mo = w_mo.reshape(-1, w_mo.shape[2])
    x2 = _mm_nn(act, w_mo, f"mlp_out_{tag}", [F32], extras=[x1])[0]
    saved = dict(x=x, h1=h1, proj_a=proj_a, proj_b=proj_b, f_t=f_t, qc=qc, kc=kc, y_a=y_a, lse_a=lse_a, y_b=y_b,
                 lse_b=lse_b, mixed=mixed, x1=x1, h2=h2, u=u, act=act, w6=w6, wf_t=wf_t, w_out=w_out, w_mi=w_mi,
                 w_mo=w_mo)
    return x2, saved


def _layer_bwd(dx2, dx2c, p, send, bias, sv, defer_w_out, tag):
    Hf, Hd = p["forget_b"].shape[0], bias.shape[1]
    E = HEAD_DIM
    rows = lambda g: g.reshape(N_CHIPS, -1, g.shape[1])
    du = _mm_nt(dx2c, sv["w_mo"], f"d_act_{tag}", [CDT], extras=[sv["u"]],
                epi=lambda acc, u: (acc * (2.0 * jnp.maximum(u.astype(F32), 0.0)),))[0]
    tokens = send("w_mlp_out", rows(_mm_tn(sv["act"], dx2c, f"dw_mlp_out_{tag}", CDT)))
    dh2 = _mm_nt(du, sv["w_mi"], f"d_h2_{tag}", [F32], b_slots=True)[0]
    tokens = tokens + send("w_mlp_in", _mm_tn(sv["h2"], du, f"dw_mlp_in_{tag}", CDT, out_slots=N_CHIPS))
    dx1, dx1c, g_norm2 = _rms_bwd(sv["x1"], _tied(p["norm2_g"], tokens, f"tie_norm2_{tag}"), dh2, dx2,
                                  f"d_norm2_{tag}")
    dmixed = _mm_nt(dx1c, sv["w_out"], f"d_mixed_{tag}", [F32])[0]
    send_w_out = lambda: send("w_out", rows(_mm_tn(sv["mixed"], dx1c, f"dw_out_{tag}", CDT)))
    tokens = [] if defer_w_out else send_w_out()
    dy_a, dy_b, g_na, g_nb = _pair_norm_bwd(sv["y_a"], sv["y_b"], _tied(p["outnorm_a_g"], tokens, f"tie_norm_a_{tag}"),
                                            p["outnorm_b_g"], dmixed, f"d_norm_ab_{tag}")
    dq_a, dcq, dk_a, dv_a, dck = _fox_bwd(sv["proj_a"], sv["qc"], sv["kc"], sv["lse_a"], sv["y_a"], dy_a, Hf,
                                          f"fox_bwd_{tag}")
    df, dfc, g_fb = _gates_bwd(sv["f_t"], p["forget_b"], dcq[:, ::E].T, dck.reshape(Hf, -1), f"d_gates_{tag}")
    dq_b, dk_b, dv_b, dbias = _dil_bwd(sv["proj_b"], bias, sv["y_b"], dy_b, sv["lse_b"], Hd, f"dil_bwd_{tag}")
    dproj = jnp.concatenate([dq_a, dk_a, dv_a, dq_b, dk_b, dv_b], axis=1)
    g_w6 = _mm_tn(sv["h1"], dproj, f"dw_in_{tag}", CDT)
    g_wf_t = _mm_nn(dfc, sv["h1"], f"dw_f_{tag}", [F32])[0]
    tokens = send("w_in", _join_dw_in(g_w6, g_wf_t, Hf, f"join_dw_in_{tag}"))
    dh1_f = _mm_tn(dfc, _tied(sv["wf_t"], tokens, f"tie_wf_{tag}"), f"d_h1_f_{tag}", F32)
    dh1 = _mm_nt(dproj, sv["w6"], f"d_h1_{tag}", [F32], extras=[dh1_f])[0]
    dx, dxc, g_norm1 = _rms_bwd(sv["x"], p["norm1_g"], dh1, dx1, f"d_norm1_{tag}")
    grads = dict(norm1_g=g_norm1[0], norm2_g=g_norm2[0], outnorm_a_g=g_na[0], outnorm_b_g=g_nb[0],
                 forget_b=g_fb[:, 0], dbias=dbias)
    return dx, dxc, grads, (send_w_out if defer_w_out else None)


_LAYER_SMALL = ("norm1_g", "forget_b", "outnorm_a_g", "outnorm_b_g", "norm2_g")


def _local_step(x, target, small, weight, send, tokens):
    depth = small["norm1_g"].shape[0]
    buckets = _bucket_table()
    bias = _bias_table(small["rel_bias"], buckets, "bias_table")
    layers, saved = [], []
    for l in range(depth):
        p = {k: small[k][l] for k in _LAYER_SMALL}
        layers.append(p)
        x, sv = _layer_fwd(x, p, functools.partial(weight, l), bias, tokens if l == 0 else [], f"l{l}")
        saved.append(sv)
    dx, dxc, g_final, loss = _loss_bwd(x, small["final_norm_g"], target, "loss")
    layer_grads = [None] * depth
    for l in reversed(range(depth)):
        dx, dxc, layer_grads[l], last = _layer_bwd(dx, dxc, layers[l], functools.partial(send, l), bias, saved[l],
                                                   l == 0, f"l{l}")
    tokens = last()
    dbias = functools.reduce(jnp.add, [g["dbias"] for g in layer_grads])
    g_rel = _bias_table_bwd(dbias, buckets, "d_bias_table")[:, 0, :].T
    small_grads = dict(final_norm_g=g_final[0], rel_bias=g_rel,
                       **{k: jnp.stack([g[k] for g in layer_grads]) for k in _LAYER_SMALL})
    return loss[0, 0], dx, small_grads, tokens


_BIG = ("w_in", "w_out", "w_mlp_in", "w_mlp_out")
_SMALL = ("norm1_g", "forget_b", "rel_bias", "outnorm_a_g", "outnorm_b_g", "norm2_g", "final_norm_g")
_ORDER = ("norm1_g", "w_in", "forget_b", "rel_bias", "outnorm_a_g", "outnorm_b_g", "w_out", "norm2_g", "w_mlp_in",
          "w_mlp_out", "final_norm_g")


def _pack_small(d):
    flat = jnp.concatenate([d[k].reshape(-1) for k in _SMALL])
    rows = -(-flat.shape[0] // (8 * SMALL_COLS)) * 8
    return jnp.pad(flat, (0, rows * SMALL_COLS - flat.shape[0])).reshape(rows, SMALL_COLS)


def _unpack_small(packed, like):
    flat, out, at = packed.reshape(-1), {}, 0
    for k in _SMALL:
        n = like[k].size
        out[k] = flat[at:at + n].reshape(like[k].shape)
        at += n
    return out


def kernel(x, norm1_g, w_in, forget_b, rel_bias, outnorm_a_g, outnorm_b_g, w_out, norm2_g, w_mlp_in, w_mlp_out, final_norm_g, loss_target, m_norm1_g, m_w_in, m_forget_b, m_rel_bias, m_outnorm_a_g, m_outnorm_b_g, m_w_out, m_norm2_g, m_w_mlp_in, m_w_mlp_out, m_final_norm_g, v_norm1_g, v_w_in, v_forget_b, v_rel_bias, v_outnorm_a_g, v_outnorm_b_g, v_w_out, v_norm2_g, v_w_mlp_in, v_w_mlp_out, v_final_norm_g):
    w = dict(norm1_g=norm1_g, w_in=w_in, forget_b=forget_b, rel_bias=rel_bias, outnorm_a_g=outnorm_a_g,
             outnorm_b_g=outnorm_b_g, w_out=w_out, norm2_g=norm2_g, w_mlp_in=w_mlp_in, w_mlp_out=w_mlp_out,
             final_norm_g=final_norm_g)
    m = dict(norm1_g=m_norm1_g, w_in=m_w_in, forget_b=m_forget_b, rel_bias=m_rel_bias, outnorm_a_g=m_outnorm_a_g,
             outnorm_b_g=m_outnorm_b_g, w_out=m_w_out, norm2_g=m_norm2_g, w_mlp_in=m_w_mlp_in,
             w_mlp_out=m_w_mlp_out, final_norm_g=m_final_norm_g)
    v = dict(norm1_g=v_norm1_g, w_in=v_w_in, forget_b=v_forget_b, rel_bias=v_rel_bias, outnorm_a_g=v_outnorm_a_g,
             outnorm_b_g=v_outnorm_b_g, w_out=v_w_out, norm2_g=v_norm2_g, w_mlp_in=v_w_mlp_in,
             w_mlp_out=v_w_mlp_out, final_norm_g=v_final_norm_g)
    depth = w_in.shape[0]
    small = {k: w[k] for k in _SMALL}

    gathers, tokens = {}, []
    for l in range(depth):
        for k in _BIG:
            buf = _cast_into_slot(w[k], l, f"cast_{k}_l{l}")
            start = _gather_start if gathers else _relay_start
            send_sems, recv_sems, buf, token = start(buf, tokens, f"gather_start_{k}_l{l}")
            gathers[l, k], tokens = (send_sems, recv_sems, buf), [token]
    first = next(iter(gathers))

    def weight(l, k, after):
        if (l, k) == first:
            return _relay_wait(*_relay_pass(*gathers[l, k], after, f"gather_pass_{k}_l{l}"), f"gather_wait_{k}_l{l}")
        return _gather_wait(*gathers[l, k], after, f"gather_wait_{k}_l{l}")

    scatters = {}

    def send(l, k, g):
        scatters[l, k] = _scatter_start(g, f"rs_start_{k}_l{l}")
        return [scatters[l, k][4]]

    loss, grad_x, small_grads, tokens = _local_step(x[0], loss_target[0], small, weight, send, tokens)
    loss = lax.psum(loss, ("x", "y", "c"))

    grads, delta, new_m, new_v = {}, {}, {}, {}
    after, seen, joining = tokens[0], {k: 0 for k in _BIG}, None

    def joined(after):
        (l, k), (send_sem, recv_sem, g) = joining
        grads[k] = _join_wait(send_sem, recv_sem, g, l, after, f"rs_join_wait_{k}_l{l}")
        seen[k] += 1
        if seen[k] < depth:
            return after
        shape = w[k].shape
        flat = lambda t: t.reshape(-1, shape[-1])
        g_, d_, m_, v_ = _adamw(flat(w[k]), flat(grads[k]), flat(m[k]), flat(v[k]), f"adamw_{k}")
        grads[k], delta[k], new_m[k], new_v[k] = (t.reshape(shape) for t in (g_, d_, m_, v_))
        return d_

    for (l, k), started in scatters.items():
        assert joining is None or joining[0][1] != k
        send_sem, recv_sem, g, token = _reduce_scatter_sum(started, after, l, depth, grads.get(k), f"{k}_l{l}")
        if joining is not None:
            after = joined(token)
        joining = ((l, k), (send_sem, recv_sem, g))
    after = joined(after)
    small_sums = _all_reduce_small(_tied(_pack_small(small_grads), [after], "tie_small"), "small_all_reduce")
    grads.update(_unpack_small(small_sums, small))
    _, d_, m_, v_ = _adamw(_pack_small(small), _pack_small({k: grads[k] for k in _SMALL}),
                           _pack_small({k: m[k] for k in _SMALL}), _pack_small({k: v[k] for k in _SMALL}), "adamw_small")
    delta.update(_unpack_small(d_, small))
    new_m.update(_unpack_small(m_, small))
    new_v.update(_unpack_small(v_, small))

    return (loss, grad_x[None], *[grads[k] for k in _ORDER], *[delta[k] for k in _ORDER],
            *[new_m[k] for k in _ORDER], *[new_v[k] for k in _ORDER])
```

```python
import functools

import jax
import jax.numpy as jnp
from jax import lax
from jax.experimental import pallas as pl
from jax.experimental.pallas import tpu as pltpu

F32 = jnp.float32
CDT = jnp.bfloat16
HEAD_DIM = 128
NORM_EPS = 1e-6
NEG_INF = -1e30
LOG2E = 1.4426950408889634
REL_BUCKETS = 32
REL_MAX_DISTANCE = 2048
DIL_PATTERNS = ((128, 1), (512, 4), (2048, 16))
DIL_BLOCK = 128
ADAM_LR, ADAM_B1, ADAM_B2, ADAM_EPS, ADAM_WD, ADAM_STEP = 0.001, 0.9, 0.999, 1e-08, 0.01, 10
N_CHIPS = 4
N_DEV = 8
VMEM_LIMIT_BYTES = 56 * 1024 * 1024
SMALL_COLS = 1024
MESH = pl.DeviceIdType.MESH


def _cparams(sem=None):
    return pltpu.CompilerParams(dimension_semantics=sem, vmem_limit_bytes=VMEM_LIMIT_BYTES)


def _tile(dim, pref):
    t = min(pref, dim)
    t -= t % 128
    while t >= 128:
        if dim % t == 0:
            return t
        t -= 128
    return dim


def _rowwise(fn, ins, out_dtypes, name, bs=256, consts=()):
    R, C = ins[0].shape
    bs = min(bs, R)
    n_in, n_c = len(ins), len(consts)

    def body(*refs):
        vals = [r[...] for r in refs[:n_in + n_c]]
        res = fn(*vals)
        for o, r in zip(refs[n_in + n_c:], res):
            o[...] = r.astype(o.dtype)

    row = pl.BlockSpec((bs, C), lambda i: (i, 0))
    return pl.pallas_call(
        body, grid=(R // bs,),
        in_specs=[row] * n_in + [pl.BlockSpec((1, c.shape[-1]), lambda i: (0, 0)) for c in consts],
        out_specs=[row] * len(out_dtypes),
        out_shape=[jax.ShapeDtypeStruct((R, C), d) for d in out_dtypes],
        name=name, compiler_params=_cparams(("parallel",)),
    )(*ins, *[c.reshape(1, -1) for c in consts])


def _rms_fwd(x, g, name):
    def fn(xf, gg):
        r = lax.rsqrt(jnp.mean(xf * xf, axis=-1, keepdims=True) + NORM_EPS)
        return ((xf * r) * gg,)
    return _rowwise(fn, [x], [CDT], name, consts=[g])[0]


def _rms_bwd(x, g, dh, dres, name, bs=256):
    S, D = x.shape
    bs = min(bs, S)
    has_res = dres is not None

    def body(*refs):
        x_ref, g_ref, dh_ref = refs[:3]
        dx_ref, dxc_ref, dg_ref = refs[-3:]
        xf = x_ref[...]
        r = lax.rsqrt(jnp.mean(xf * xf, axis=-1, keepdims=True) + NORM_EPS)
        xhat = xf * r
        dh_ = dh_ref[...].astype(F32)
        dxhat = dh_ * g_ref[...]
        dx = r * (dxhat - xhat * jnp.mean(dxhat * xhat, axis=-1, keepdims=True))
        if has_res:
            dx = dx + refs[3][...]
        dx_ref[...] = dx
        dxc_ref[...] = dx.astype(dxc_ref.dtype)
        part = jnp.sum(dh_ * xhat, axis=0, keepdims=True)

        @pl.when(pl.program_id(0) == 0)
        def _():
            dg_ref[...] = part

        @pl.when(pl.program_id(0) > 0)
        def _():
            dg_ref[...] += part

    row = pl.BlockSpec((bs, D), lambda i: (i, 0))
    one = pl.BlockSpec((1, D), lambda i: (0, 0))
    ins = [x, g.reshape(1, D), dh] + ([dres] if has_res else [])
    return pl.pallas_call(
        body, grid=(S // bs,),
        in_specs=[row, one, row] + ([row] if has_res else []),
        out_specs=[row, row, one],
        out_shape=[jax.ShapeDtypeStruct((S, D), F32), jax.ShapeDtypeStruct((S, D), CDT),
                   jax.ShapeDtypeStruct((1, D), F32)],
        name=name, compiler_params=_cparams(("arbitrary",)),
    )(*ins)


def _pair_norm_fwd(y_a, y_b, g_a, g_b, name, bs=256):
    S, Da = y_a.shape
    Db = y_b.shape[1]
    bs = min(bs, S)

    def body(a_ref, b_ref, ga_ref, gb_ref, o_ref):
        def norm(x, g):
            r = lax.rsqrt(jnp.mean(x * x, axis=-1, keepdims=True) + NORM_EPS)
            return ((x * r) * g).astype(o_ref.dtype)
        o_ref[:, :Da] = norm(a_ref[...], ga_ref[...])
        o_ref[:, Da:] = norm(b_ref[...], gb_ref[...])

    row = lambda n: pl.BlockSpec((bs, n), lambda i: (i, 0))
    one = lambda n: pl.BlockSpec((1, n), lambda i: (0, 0))
    return pl.pallas_call(
        body, grid=(S // bs,), in_specs=[row(Da), row(Db), one(Da), one(Db)], out_specs=row(Da + Db),
        out_shape=jax.ShapeDtypeStruct((S, Da + Db), CDT), name=name, compiler_params=_cparams(("parallel",)),
    )(y_a, y_b, g_a.reshape(1, Da), g_b.reshape(1, Db))


def _pair_norm_bwd(y_a, y_b, g_a, g_b, dmixed, name, bs=256):
    S, Da = y_a.shape
    Db = y_b.shape[1]
    bs = min(bs, S)

    def body(a_ref, b_ref, ga_ref, gb_ref, dm_ref, da_ref, db_ref, dga_ref, dgb_ref):
        def one(x_ref, g_ref, dh, dx_ref, dg_ref):
            xf = x_ref[...]
            r = lax.rsqrt(jnp.mean(xf * xf, axis=-1, keepdims=True) + NORM_EPS)
            xhat = xf * r
            dxhat = dh * g_ref[...]
            dx_ref[...] = r * (dxhat - xhat * jnp.mean(dxhat * xhat, axis=-1, keepdims=True))
            part = jnp.sum(dh * xhat, axis=0, keepdims=True)

            @pl.when(pl.program_id(0) == 0)
            def _():
                dg_ref[...] = part

            @pl.when(pl.program_id(0) > 0)
            def _():
                dg_ref[...] += part

        dm = dm_ref[...]
        one(a_ref, ga_ref, dm[:, :Da], da_ref, dga_ref)
        one(b_ref, gb_ref, dm[:, Da:], db_ref, dgb_ref)

    row = lambda n: pl.BlockSpec((bs, n), lambda i: (i, 0))
    one_ = lambda n: pl.BlockSpec((1, n), lambda i: (0, 0))
    return pl.pallas_call(
        body, grid=(S // bs,), in_specs=[row(Da), row(Db), one_(Da), one_(Db), row(Da + Db)],
        out_specs=[row(Da), row(Db), one_(Da), one_(Db)],
        out_shape=[jax.ShapeDtypeStruct((S, Da), F32), jax.ShapeDtypeStruct((S, Db), F32),
                   jax.ShapeDtypeStruct((1, Da), F32), jax.ShapeDtypeStruct((1, Db), F32)],
        name=name, compiler_params=_cparams(("arbitrary",)),
    )(y_a, y_b, g_a.reshape(1, Da), g_b.reshape(1, Db), dmixed)


def _loss_bwd(x, g, target, name, bs=256):
    S, D = x.shape
    bs = min(bs, S)

    def body(x_ref, g_ref, t_ref, dx_ref, dxc_ref, dg_ref, loss_ref):
        xf = x_ref[...]
        r = lax.rsqrt(jnp.mean(xf * xf, axis=-1, keepdims=True) + NORM_EPS)
        xhat = xf * r
        err = xhat * g_ref[...] - t_ref[...]
        lpart = 0.5 * jnp.sum(jnp.mean(err * err, axis=-1, keepdims=True), axis=0, keepdims=True)
        dy = err / D
        dxhat = dy * g_ref[...]
        dx = r * (dxhat - xhat * jnp.mean(dxhat * xhat, axis=-1, keepdims=True))
        dx_ref[...] = dx
        dxc_ref[...] = dx.astype(dxc_ref.dtype)
        gpart = jnp.sum(dy * xhat, axis=0, keepdims=True)

        @pl.when(pl.program_id(0) == 0)
        def _():
            dg_ref[...] = gpart
            loss_ref[...] = lpart

        @pl.when(pl.program_id(0) > 0)
        def _():
            dg_ref[...] += gpart
            loss_ref[...] += lpart

    row = pl.BlockSpec((bs, D), lambda i: (i, 0))
    one = pl.BlockSpec((1, D), lambda i: (0, 0))
    return pl.pallas_call(
        body, grid=(S // bs,),
        in_specs=[row, one, row],
        out_specs=[row, row, one, pl.BlockSpec((1, 1), lambda i: (0, 0))],
        out_shape=[jax.ShapeDtypeStruct((S, D), F32), jax.ShapeDtypeStruct((S, D), CDT),
                   jax.ShapeDtypeStruct((1, D), F32), jax.ShapeDtypeStruct((1, 1), F32)],
        name=name, compiler_params=_cparams(("arbitrary",)),
    )(x, g.reshape(1, D), target)


_NN = (((1,), (0,)), ((), ()))
_NT = (((1,), (1,)), ((), ()))
_TN = (((0,), (0,)), ((), ()))


def _mm(a, b, *, M, N, K, a_spec, b_spec, o_spec, dims, tm, tn, tk, name, out_shapes, extras=(), epi=None):
    nk = K // tk
    n_ex, n_out = len(extras), len(out_shapes)
    in_place = epi is None
    if in_place:
        assert n_out == 1 and n_ex <= 1 and out_shapes[0].dtype == F32
        epi = lambda acc, *r: (acc + r[0] if r else acc,)

    def body(*refs):
        a_ref, b_ref = refs[0], refs[1]
        ex = refs[2:2 + n_ex]
        outs = refs[2 + n_ex:2 + n_ex + n_out]
        part = lax.dot_general(a_ref[...], b_ref[...], dims, preferred_element_type=F32)

        def finish(acc):
            for o, r in zip(outs, epi(acc, *[e[...] for e in ex])):
                o[...] = r.astype(o.dtype)

        if nk == 1:
            finish(part)
        elif in_place:
            k = pl.program_id(2)

            @pl.when(k == 0)
            def _():
                finish(part)

            @pl.when(k > 0)
            def _():
                outs[0][...] += part
        else:
            acc_ref = refs[-1]
            k = pl.program_id(2)

            @pl.when(k == 0)
            def _():
                acc_ref[...] = part

            @pl.when(k > 0)
            def _():
                acc_ref[...] += part

            @pl.when(k == nk - 1)
            def _():
                finish(acc_ref[...])

    ex_spec = pl.BlockSpec((tm, tn), lambda i, j, k: (i, j))
    return pl.pallas_call(
        body, grid=(M // tm, N // tn, nk),
        in_specs=[a_spec, b_spec] + [ex_spec] * n_ex,
        out_specs=[o_spec] * n_out,
        out_shape=out_shapes,
        scratch_shapes=[pltpu.VMEM((tm, tn), F32)] if nk > 1 and not in_place else [],
        name=name, compiler_params=_cparams(("parallel", "parallel", "arbitrary")),
    )(a, b, *extras)


def _mm_tiles(K):
    return (2048, 512, 2048) if K <= 2048 else (1024, 1024, 2048)


def _mm_nn(a, b, name, out_dtypes, extras=(), epi=None, b_slots=False, b_cols=None):
    M, K = a.shape
    tm, tn, tk = _mm_tiles(K)
    if b_slots:
        ns, _, Ns = b.shape
        N = ns * Ns
        tn = _tile(Ns, tn)
        npb = Ns // tn
        tk_ = _tile(K, tk)
        b_spec = pl.BlockSpec((None, tk_, tn), lambda i, j, k: (j // npb, k, j % npb))
    else:
        first, N = b_cols if b_cols is not None else (0, b.shape[1])
        tn = _tile(N, tn)
        assert first % tn == 0
        tk_ = _tile(K, tk)
        b_spec = pl.BlockSpec((tk_, tn), lambda i, j, k: (k, first // tn + j))
    tm = _tile(M, tm)
    return _mm(a, b, M=M, N=N, K=K, a_spec=pl.BlockSpec((tm, tk_), lambda i, j, k: (i, k)), b_spec=b_spec,
               o_spec=pl.BlockSpec((tm, tn), lambda i, j, k: (i, j)), dims=_NN, tm=tm, tn=tn, tk=tk_, name=name,
               out_shapes=[jax.ShapeDtypeStruct((M, N), d) for d in out_dtypes], extras=extras, epi=epi)


def _mm_nt(a, b, name, out_dtypes, extras=(), epi=None, b_slots=False):
    M, K = a.shape
    tm, tn, tk = _mm_tiles(K)
    tm = _tile(M, tm)
    if b_slots:
        ns, N, Ks = b.shape
        tk_ = _tile(Ks, tk)
        kpb = Ks // tk_
        tn = _tile(N, tn)
        b_spec = pl.BlockSpec((None, tn, tk_), lambda i, j, k: (k // kpb, j, k % kpb))
    else:
        N = b.shape[0]
        tk_ = _tile(K, tk)
        tn = _tile(N, tn)
        b_spec = pl.BlockSpec((tn, tk_), lambda i, j, k: (j, k))
    return _mm(a, b, M=M, N=N, K=K, a_spec=pl.BlockSpec((tm, tk_), lambda i, j, k: (i, k)), b_spec=b_spec,
               o_spec=pl.BlockSpec((tm, tn), lambda i, j, k: (i, j)), dims=_NT, tm=tm, tn=tn, tk=tk_, name=name,
               out_shapes=[jax.ShapeDtypeStruct((M, N), d) for d in out_dtypes], extras=extras, epi=epi)


def _mm_tn(a, b, name, out_dtype, out_slots=0, tm=2048, tn=1024, tk=2048):
    K, M = a.shape
    N = b.shape[1]
    tm, tk_ = _tile(M, tm), _tile(K, tk)
    if out_slots:
        Ns = N // out_slots
        tn = _tile(Ns, tn)
        npb = Ns // tn
        o_spec = pl.BlockSpec((None, tm, tn), lambda i, j, k: (j // npb, i, j % npb))
        out_shape = jax.ShapeDtypeStruct((out_slots, M, Ns), out_dtype)
    else:
        tn = _tile(N, tn)
        o_spec = pl.BlockSpec((tm, tn), lambda i, j, k: (i, j))
        out_shape = jax.ShapeDtypeStruct((M, N), out_dtype)
    return _mm(a, b, M=M, N=N, K=K, a_spec=pl.BlockSpec((tk_, tm), lambda i, j, k: (k, i)),
               b_spec=pl.BlockSpec((tk_, tn), lambda i, j, k: (k, j)), o_spec=o_spec, dims=_TN,
               tm=tm, tn=tn, tk=tk_, name=name, out_shapes=[out_shape],
               epi=None if out_dtype == F32 else (lambda acc: (acc,)))[0]


GATE_BLOCK = 512


def _split3(v):
    hi = v.astype(jnp.bfloat16)
    r1 = v - hi.astype(F32)
    mid = r1.astype(jnp.bfloat16)
    lo = (r1 - mid.astype(F32)).astype(jnp.bfloat16)
    return hi, mid, lo


def _exact_dot(v, tri):
    return functools.reduce(jnp.add, [jnp.dot(t, tri, preferred_element_type=F32) for t in _split3(v)])


def _gates_fwd(f_t, b, name):
    H, S = f_t.shape
    nb = _tile(S, GATE_BLOCK)
    inv_scale = HEAD_DIM ** 0.5

    def body(f_ref, b_ref, c_ref):
        upper = (lax.broadcasted_iota(jnp.int32, (nb, nb), 0)
                 <= lax.broadcasted_iota(jnp.int32, (nb, nb), 1)).astype(jnp.bfloat16)
        carry = jnp.zeros((H, 1), F32)
        for i in range(S // nb):
            z = f_ref[:, i * nb:(i + 1) * nb] + b_ref[...]
            logf = jnp.minimum(z, 0.0) - jnp.log1p(jnp.exp(-jnp.abs(z)))
            cs = _exact_dot(logf, upper) + carry
            for j, t in enumerate(_split3(cs * inv_scale)):
                c_ref[j, :, i * nb:(i + 1) * nb] = t
            carry = cs[:, nb - 1:nb]

    return pl.pallas_call(body, out_shape=jax.ShapeDtypeStruct((3, H, S), jnp.bfloat16), name=name,
                          compiler_params=_cparams())(f_t, b.reshape(H, 1))


def _gates_bwd(f_t, b, dcq, dck, name):
    H, S = f_t.shape
    nb = _tile(S, GATE_BLOCK)

    def body(f_ref, b_ref, dcq_ref, dck_ref, df_ref, dfc_ref, db_ref):
        lower = (lax.broadcasted_iota(jnp.int32, (nb, nb), 0)
                 >= lax.broadcasted_iota(jnp.int32, (nb, nb), 1)).astype(jnp.bfloat16)
        carry = jnp.zeros((H, 1), F32)
        db = jnp.zeros((H, 1), F32)
        for i in reversed(range(S // nb)):
            sl = slice(i * nb, (i + 1) * nb)
            dc = dcq_ref[:, sl] - dck_ref[:, sl]
            dlogf = _exact_dot(dc, lower) + carry
            carry = dlogf[:, 0:1]
            z = f_ref[:, sl] + b_ref[...]
            df = dlogf / (1.0 + jnp.exp(z))
            df_ref[:, sl] = df
            dfc_ref[:, sl] = df.astype(dfc_ref.dtype)
            db = db + jnp.sum(df, axis=1, keepdims=True)
        db_ref[...] = db

    return pl.pallas_call(
        body, out_shape=[jax.ShapeDtypeStruct((H, S), F32), jax.ShapeDtypeStruct((H, S), CDT),
                         jax.ShapeDtypeStruct((H, 1), F32)],
        name=name, compiler_params=_cparams())(f_t, b.reshape(H, 1), dcq, dck)


FOX_BLOCK = 1024


def _fox_bias_operands(csplit, name, bs=512):
    _, H, S = csplit.shape
    E = HEAD_DIM
    bs = _tile(S, bs)
    part = jnp.arange(3 * H)[:, None] // H
    head = jnp.arange(3 * H)[:, None] % H
    lane = jnp.arange(H * E)[None, :]
    place_q = (lane == head * E + part).astype(csplit.dtype)
    place_k = -(lane == head * E + 3 + part).astype(csplit.dtype)
    ones_q = ((lane % E >= 3) & (lane % E < 6)).astype(F32)
    ones_k = (lane % E < 3).astype(F32)

    def body(c_ref, pq_ref, pk_ref, oq_ref, ok_ref, qc_ref, kc_ref):
        c = c_ref[...]
        qc_ref[...] = (lax.dot_general(c, pq_ref[...], _TN, preferred_element_type=F32) + oq_ref[...]).astype(qc_ref.dtype)
        kc_ref[...] = (lax.dot_general(c, pk_ref[...], _TN, preferred_element_type=F32) + ok_ref[...]).astype(kc_ref.dtype)

    full = lambda a: pl.BlockSpec(a.shape, lambda i: (0, 0))
    out = pl.BlockSpec((bs, H * E), lambda i: (i, 0))
    return pl.pallas_call(
        body, grid=(S // bs,),
        in_specs=[pl.BlockSpec((3 * H, bs), lambda i: (0, i)), full(place_q), full(place_k), full(ones_q), full(ones_k)],
        out_specs=[out, out], out_shape=[jax.ShapeDtypeStruct((S, H * E), csplit.dtype)] * 2,
        name=name, compiler_params=_cparams(("parallel",)),
    )(csplit.reshape(3 * H, S), place_q, place_k, ones_q, ones_k)


def _fox_logits2(q_ref, qc_ref, k_ref, kc_ref, diag):
    q, k = q_ref[...], k_ref[...]
    qa = jnp.concatenate([q, qc_ref[...].astype(q.dtype)], axis=1)
    ka = jnp.concatenate([k, kc_ref[...].astype(k.dtype)], axis=1)
    s = lax.dot_general(qa, ka, _NT, preferred_element_type=F32) * (HEAD_DIM ** -0.5 * LOG2E)
    if diag:
        row = lax.broadcasted_iota(jnp.int32, s.shape, 0)
        col = lax.broadcasted_iota(jnp.int32, s.shape, 1)
        s = jnp.where(col <= row, s, NEG_INF)
    return s


def _fox_fwd(proj, qc, kc, H, name):
    S = proj.shape[0]
    E = HEAD_DIM
    blk = _tile(S, FOX_BLOCK)
    nq = S // blk

    def body(q_ref, qc_ref, k_ref, kc_ref, v_ref, o_ref, lse_ref, m_s, l_s, acc_s):
        qi, kj = pl.program_id(1), pl.program_id(2)

        @pl.when(kj == 0)
        def _():
            m_s[...] = jnp.full(m_s.shape, NEG_INF, F32)
            l_s[...] = jnp.zeros(l_s.shape, F32)
            acc_s[...] = jnp.zeros(acc_s.shape, F32)

        def step(diag):
            s = _fox_logits2(q_ref, qc_ref, k_ref, kc_ref, diag)
            m_prev = m_s[...]
            m_new = jnp.maximum(m_prev, jnp.max(s, axis=-1, keepdims=True))
            alpha = jnp.exp2(m_prev - m_new)
            p = jnp.exp2(s - m_new)
            l_s[...] = alpha * l_s[...] + jnp.sum(p, axis=-1, keepdims=True)
            acc_s[...] = alpha * acc_s[...] + jnp.dot(p.astype(CDT), v_ref[...], preferred_element_type=F32)
            m_s[...] = m_new

        pl.when(kj < qi)(lambda: step(False))
        pl.when(kj == qi)(lambda: step(True))

        @pl.when(kj == nq - 1)
        def _():
            o_ref[...] = acc_s[...] / l_s[...]
            lse_ref[...] = jnp.broadcast_to(m_s[...] + jnp.log2(l_s[...]), lse_ref.shape)

    qspec = lambda off: pl.BlockSpec((blk, E), lambda h, i, j: (i, off + h))
    kspec = lambda off: pl.BlockSpec((blk, E), lambda h, i, j: (jnp.minimum(j, i), off + h))
    return pl.pallas_call(
        body, grid=(H, nq, nq),
        in_specs=[qspec(0), qspec(0), kspec(H), kspec(0), kspec(2 * H)],
        out_specs=[qspec(0)] * 2,
        out_shape=[jax.ShapeDtypeStruct((S, H * E), F32)] * 2,
        scratch_shapes=[pltpu.VMEM((blk, 1), F32), pltpu.VMEM((blk, 1), F32), pltpu.VMEM((blk, E), F32)],
        name=name, compiler_params=_cparams(("parallel", "parallel", "arbitrary")),
    )(proj, qc, proj, kc, proj)


def _fox_bwd(proj, qc, kc, lse, o, do, H, name):
    S = proj.shape[0]
    E = HEAD_DIM
    blk = _tile(S, FOX_BLOCK)
    nq = S // blk
    scale = E ** -0.5

    def body(q_ref, qc_ref, k_ref, kc_ref, v_ref, lse_ref, o_ref, do_ref,
             dq_ref, dcq_ref, dk_ref, dv_ref, dck_ref, dq_s, dcq_s, dk_s, dv_s, dck_s):
        kj, qi = pl.program_id(1), pl.program_id(2)

        @pl.when(qi == 0)
        def _():
            dk_s[...] = jnp.zeros(dk_s.shape, F32)
            dv_s[...] = jnp.zeros(dv_s.shape, F32)
            dck_s[...] = jnp.zeros(dck_s.shape, F32)

        def step(diag):
            do = do_ref[...]
            doc = do.astype(CDT)
            delta = jnp.sum(do * o_ref[...], axis=-1, keepdims=True)
            p = jnp.exp2(_fox_logits2(q_ref, qc_ref, k_ref, kc_ref, diag) - lse_ref[:, 0:1])
            dp = lax.dot_general(doc, v_ref[...], _NT, preferred_element_type=F32)
            ds = p * (dp - delta)
            dss = ds * scale
            dck_s[...] += jnp.sum(ds, axis=0, keepdims=True)
            dv_s[...] += jnp.dot(p.T.astype(CDT), doc, preferred_element_type=F32)
            dk_s[...] += jnp.dot(dss.T.astype(CDT), q_ref[...], preferred_element_type=F32)
            dq_part = jnp.dot(dss.astype(CDT), k_ref[...], preferred_element_type=F32)
            dc_part = jnp.sum(ds, axis=-1, keepdims=True)
            rows = pl.ds(pl.multiple_of(qi * blk, blk), blk)

            @pl.when(kj == 0)
            def _():
                dq_s[rows, :] = dq_part
                dcq_s[rows, :] = dc_part

            @pl.when(kj > 0)
            def _():
                dq_s[rows, :] += dq_part
                dcq_s[rows, :] += dc_part

        pl.when(qi > kj)(lambda: step(False))
        pl.when(qi == kj)(lambda: step(True))

        @pl.when(qi == nq - 1)
        def _():
            dk_ref[...] = dk_s[...].astype(dk_ref.dtype)
            dv_ref[...] = dv_s[...].astype(dv_ref.dtype)
            dck_ref[...] = dck_s[...].reshape(dck_ref.shape)

        @pl.when((qi == nq - 1) & (kj == nq - 1))
        def _():
            dq_ref[...] = dq_s[...].astype(dq_ref.dtype)
            dcq_ref[...] = jnp.broadcast_to(dcq_s[...], dcq_ref.shape)

    qspec = lambda off: pl.BlockSpec((blk, E), lambda h, j, i: (jnp.maximum(i, j), off + h))
    kspec = lambda off: pl.BlockSpec((blk, E), lambda h, j, i: (j, off + h))
    head = pl.BlockSpec((S, E), lambda h, j, i: (0, h))
    return pl.pallas_call(
        body, grid=(H, nq, nq),
        in_specs=[qspec(0), qspec(0), kspec(H), kspec(0), kspec(2 * H), qspec(0), qspec(0), qspec(0)],
        out_specs=[head, head, kspec(0), kspec(0), pl.BlockSpec((1, 1, blk), lambda h, j, i: (h, 0, j))],
        out_shape=[jax.ShapeDtypeStruct((S, H * E), CDT), jax.ShapeDtypeStruct((S, H * E), F32),
                   jax.ShapeDtypeStruct((S, H * E), CDT), jax.ShapeDtypeStruct((S, H * E), CDT),
                   jax.ShapeDtypeStruct((H, 1, S), F32)],
        scratch_shapes=[pltpu.VMEM((S, E), F32), pltpu.VMEM((S, 1), F32), pltpu.VMEM((blk, E), F32),
                        pltpu.VMEM((blk, E), F32), pltpu.VMEM((1, blk), F32)],
        name=name, compiler_params=_cparams(("parallel", "arbitrary", "arbitrary")),
    )(proj, qc, proj, kc, proj, lse, o, do)


DIL_SLAB = 16 * DIL_BLOCK
DIL_UNROLL = 8


def _rel_bucket(dist):
    max_exact = REL_BUCKETS // 2
    d = jnp.maximum(dist.astype(F32), 1.0)
    large = max_exact + (jnp.log(d / max_exact) / jnp.log(jnp.float32(REL_MAX_DISTANCE / max_exact))
                         * (REL_BUCKETS - max_exact)).astype(jnp.int32)
    large = jnp.minimum(large, REL_BUCKETS - 1)
    return jnp.where(dist < max_exact, dist, large)


def _bucket_table():
    i = jnp.arange(DIL_BLOCK)[:, None]
    j = jnp.arange(2 * DIL_BLOCK)[None, :]
    rel = DIL_BLOCK + i - j
    tabs = [_rel_bucket(jnp.clip(rel, 0, w // d) * d) for w, d in DIL_PATTERNS]
    return jnp.stack(tabs).astype(jnp.int32)


def _bias_table(rel_bias, buckets, name):
    P = buckets.shape[0]
    H = rel_bias.shape[1]

    def body(rb_ref, bk_ref, out_ref):
        h = pl.program_id(1)
        bk = bk_ref[0]
        val = jnp.zeros(bk.shape, F32)
        for b in range(REL_BUCKETS):
            val = jnp.where(bk == b, rb_ref[b, h], val)
        out_ref[0, 0] = val

    return pl.pallas_call(
        body, grid=(P, H),
        in_specs=[pl.BlockSpec(memory_space=pltpu.SMEM),
                  pl.BlockSpec((1, DIL_BLOCK, 2 * DIL_BLOCK), lambda p, h: (p, 0, 0))],
        out_specs=pl.BlockSpec((1, 1, DIL_BLOCK, 2 * DIL_BLOCK), lambda p, h: (p, h, 0, 0)),
        out_shape=jax.ShapeDtypeStruct((P, H, DIL_BLOCK, 2 * DIL_BLOCK), F32),
        name=name, compiler_params=_cparams(("parallel", "parallel")),
    )(rel_bias, buckets)


def _bias_table_bwd(dbias, buckets, name):
    P, H = dbias.shape[:2]

    def body(db_ref, bk_ref, out_ref):
        lane = lax.broadcasted_iota(jnp.int32, (1, REL_BUCKETS), 1)
        acc = jnp.zeros((1, REL_BUCKETS), F32)
        bk = bk_ref[...]
        db = db_ref[:, 0]
        for b in range(REL_BUCKETS):
            tot = jnp.sum(jnp.where(bk == b, db, 0.0))
            acc = jnp.where(lane == b, tot, acc)
        out_ref[0] = acc

    return pl.pallas_call(
        body, grid=(H,),
        in_specs=[pl.BlockSpec((P, 1, DIL_BLOCK, 2 * DIL_BLOCK), lambda h: (0, h, 0, 0)),
                  pl.BlockSpec((P, DIL_BLOCK, 2 * DIL_BLOCK), lambda h: (0, 0, 0))],
        out_specs=pl.BlockSpec((1, 1, REL_BUCKETS), lambda h: (h, 0, 0)),
        out_shape=jax.ShapeDtypeStruct((H, 1, REL_BUCKETS), F32),
        name=name, compiler_params=_cparams(("parallel",)),
    )(dbias, buckets)


def _bdot(a, b, contract_b):
    return lax.dot_general(a, b, (((2,), (contract_b,)), ((0,), (0,))), preferred_element_type=F32)


def _dil_units(first, d):
    units = []
    for t in range(DIL_UNROLL):
        u = first + t
        sg = u // d
        units.append((sg, sg * (DIL_BLOCK * d) + u % d))
    return units


def _dil_rows(ref, starts, d, dtype=None):
    t = jnp.stack([ref[pl.ds(s, DIL_BLOCK, stride=d), :] for s in starts])
    return t if dtype is None else t.astype(dtype)


def _dil_keys(ref, units, d):
    B, SL = DIL_BLOCK, DIL_SLAB
    return jnp.stack([jnp.concatenate([ref[pl.ds(SL + b - B * d, B, stride=d), :], ref[pl.ds(SL + b, B, stride=d), :]],
                                      axis=0) for _, b in units]).astype(CDT)


def _dil_logits(q, keys, bias_pc, first, d, has_before):
    T, B = q.shape[0], DIL_BLOCK
    ii = lax.broadcasted_iota(jnp.int32, (T, B, 2 * B), 1)
    jj = lax.broadcasted_iota(jnp.int32, (T, B, 2 * B), 2)
    sg = (first + lax.broadcasted_iota(jnp.int32, (T, B, 2 * B), 0)) // d
    mask = (jj >= ii) & (jj <= ii + B) & ((jj >= B) | (sg > 0) | has_before)
    return jnp.where(mask, _bdot(q, keys, 2) * HEAD_DIM ** -0.5 + bias_pc[None], NEG_INF)


def _dil_specs(H, n_slabs):
    E, SL = DIL_BLOCK, DIL_SLAB
    cur = lambda off: pl.BlockSpec((SL, E), lambda h, g: (g, off + h))
    prev = lambda off: pl.BlockSpec((SL, E), lambda h, g: (jnp.maximum(g - 1, 0), off + h))
    nxt = lambda off: pl.BlockSpec((SL, E), lambda h, g: (jnp.minimum(g + 1, n_slabs - 1), off + h))
    bias = pl.BlockSpec((len(DIL_PATTERNS), 1, E, 2 * E), lambda h, g: (0, h, 0, 0))
    return cur, prev, nxt, bias


def _dil_fwd(proj, bias, H, name):
    S = proj.shape[0]
    E = B = DIL_BLOCK
    SL = DIL_SLAB
    P = len(DIL_PATTERNS)
    assert S % SL == 0
    n_slabs = S // SL

    def body(q_ref, kc_ref, kp_ref, vc_ref, vp_ref, b_ref, y_ref, lse_ref, kj, vj, o_s, l_s):
        g = pl.program_id(1)
        kj[0:SL, :] = kp_ref[...]
        kj[SL:2 * SL, :] = kc_ref[...]
        vj[0:SL, :] = vp_ref[...]
        vj[SL:2 * SL, :] = vc_ref[...]
        for p, (_, d) in enumerate(DIL_PATTERNS):
            def batch(it, carry, p=p, d=d):
                first = it * DIL_UNROLL
                units = _dil_units(first, d)
                q = _dil_rows(q_ref, [b for _, b in units], d, CDT)
                s = _dil_logits(q, _dil_keys(kj, units, d), b_ref[p, 0], first, d, g > 0)
                m = jnp.max(s, axis=-1, keepdims=True)
                e = jnp.exp(s - m)
                ssum = jnp.sum(e, axis=-1, keepdims=True)
                o = _bdot(e.astype(CDT), _dil_keys(vj, units, d), 1) / ssum
                lse = jnp.broadcast_to(m + jnp.log(ssum), o.shape)
                for t, (_, b) in enumerate(units):
                    o_s[p, pl.ds(b, B, stride=d), :] = o[t]
                    l_s[p, pl.ds(b, B, stride=d), :] = lse[t]
                return carry

            lax.fori_loop(0, SL // B // DIL_UNROLL, batch, 0)
        ls = [l_s[p] for p in range(P)]
        m = functools.reduce(jnp.maximum, ls)
        w = [jnp.exp(l - m) for l in ls]
        tot = functools.reduce(jnp.add, w)
        y_ref[...] = functools.reduce(jnp.add, [(w[p] / tot) * o_s[p] for p in range(P)])
        lse_ref[...] = m + jnp.log(tot)

    cur, prev, _, bspec = _dil_specs(H, n_slabs)
    return pl.pallas_call(
        body, grid=(H, n_slabs),
        in_specs=[cur(0), cur(H), prev(H), cur(2 * H), prev(2 * H), bspec],
        out_specs=[cur(0), cur(0)],
        out_shape=[jax.ShapeDtypeStruct((S, H * E), F32)] * 2,
        scratch_shapes=[pltpu.VMEM((2 * SL, E), F32), pltpu.VMEM((2 * SL, E), F32),
                        pltpu.VMEM((P, SL, E), F32), pltpu.VMEM((P, SL, E), F32)],
        name=name, compiler_params=_cparams(("parallel", "parallel")),
    )(proj, proj, proj, proj, proj, bias)


def _dil_bwd(proj, bias, y, dy, lse, H, name):
    S = proj.shape[0]
    E = B = DIL_BLOCK
    SL = DIL_SLAB
    P = len(DIL_PATTERNS)
    assert S % SL == 0
    n_slabs = S // SL
    scale = E ** -0.5

    def body(q_ref, kc_ref, kp_ref, vc_ref, vp_ref, b_ref, y_ref, dy_ref, lse_ref, qn_ref, yn_ref, dyn_ref, lsen_ref,
             dq_ref, dk_ref, dv_ref, db_ref, kj, vj, dq_s, dk_s, dv_s, dl_s, dln_s):
        g = pl.program_id(1)
        kj[0:SL, :] = kp_ref[...]
        kj[SL:2 * SL, :] = kc_ref[...]
        vj[0:SL, :] = vp_ref[...]
        vj[SL:2 * SL, :] = vc_ref[...]
        dq_s[...] = jnp.zeros(dq_s.shape, F32)
        dk_s[...] = jnp.zeros(dk_s.shape, F32)
        dv_s[...] = jnp.zeros(dv_s.shape, F32)
        dl_s[...] = jnp.broadcast_to(jnp.sum(dy_ref[...] * y_ref[...], axis=-1, keepdims=True), (SL, E))
        dln_s[...] = jnp.broadcast_to(jnp.sum(dyn_ref[...] * yn_ref[...], axis=-1, keepdims=True), (SL, E))

        @pl.when(g == 0)
        def _():
            db_ref[...] = jnp.zeros(db_ref.shape, F32)

        tr = lambda t: jnp.swapaxes(t, 1, 2).astype(CDT)
        for p, (_, d) in enumerate(DIL_PATTERNS):
            def batch(it, carry, p=p, d=d):
                first = it * DIL_UNROLL
                units = _dil_units(first, d)
                starts = [b for _, b in units]
                q = _dil_rows(q_ref, starts, d, CDT)
                dyc = _dil_rows(dy_ref, starts, d, CDT)
                keys, vals = _dil_keys(kj, units, d), _dil_keys(vj, units, d)
                s = _dil_logits(q, keys, b_ref[p, 0], first, d, g > 0)
                e = jnp.exp(s - _dil_rows(lse_ref, starts, d)[:, :, 0:1])
                ds = e * (_bdot(dyc, vals, 2) - _dil_rows(dl_s, starts, d)[:, :, 0:1])
                dss = ds * scale
                dq = _bdot(dss.astype(CDT), keys, 1)
                dk = _bdot(tr(dss), q, 1)
                dv = _bdot(tr(e), dyc, 1)
                for t, (sg, b) in enumerate(units):
                    rows = pl.ds(b, B, stride=d)
                    dq_s[rows, :] += dq[t]
                    dk_s[rows, :] += dk[t, B:]
                    dv_s[rows, :] += dv[t, B:]

                    @pl.when(sg > 0)
                    def _(t=t, b=b):
                        before = pl.ds(b - B * d, B, stride=d)
                        dk_s[before, :] += dk[t, :B]
                        dv_s[before, :] += dv[t, :B]

                db_ref[p, 0] += jnp.sum(ds, axis=0)
                return carry

            lax.fori_loop(0, SL // B // DIL_UNROLL, batch, 0)

            n_after = min(d, DIL_UNROLL)

            def after(it, carry, p=p, d=d, n_after=n_after):
                starts = [it * n_after + t for t in range(n_after)]
                kstarts = [SL - B * d + s for s in starts]
                q = _dil_rows(qn_ref, starts, d, CDT)
                dyc = _dil_rows(dyn_ref, starts, d, CDT)
                k, v = _dil_rows(kc_ref, kstarts, d, CDT), _dil_rows(vc_ref, kstarts, d, CDT)
                ii = lax.broadcasted_iota(jnp.int32, (n_after, B, B), 1)
                jj = lax.broadcasted_iota(jnp.int32, (n_after, B, B), 2)
                s = jnp.where((jj >= ii) & (g < n_slabs - 1), _bdot(q, k, 2) * scale + b_ref[p, 0][:, :B][None], NEG_INF)
                e = jnp.exp(s - _dil_rows(lsen_ref, starts, d)[:, :, 0:1])
                ds = e * (_bdot(dyc, v, 2) - _dil_rows(dln_s, starts, d)[:, :, 0:1])
                dk = _bdot(tr(ds * scale), q, 1)
                dv = _bdot(tr(e), dyc, 1)
                for t, ks in enumerate(kstarts):
                    dk_s[pl.ds(ks, B, stride=d), :] += dk[t]
                    dv_s[pl.ds(ks, B, stride=d), :] += dv[t]
                return carry

            lax.fori_loop(0, d // n_after, after, 0)

        dq_ref[...] = dq_s[...].astype(dq_ref.dtype)
        dk_ref[...] = dk_s[...].astype(dk_ref.dtype)
        dv_ref[...] = dv_s[...].astype(dv_ref.dtype)

    cur, prev, nxt, bspec = _dil_specs(H, n_slabs)
    slab = pltpu.VMEM((SL, E), F32)
    return pl.pallas_call(
        body, grid=(H, n_slabs),
        in_specs=[cur(0), cur(H), prev(H), cur(2 * H), prev(2 * H), bspec, cur(0), cur(0), cur(0),
                  nxt(0), nxt(0), nxt(0), nxt(0)],
        out_specs=[cur(0), cur(0), cur(0), bspec],
        out_shape=[jax.ShapeDtypeStruct((S, H * E), CDT)] * 3 + [jax.ShapeDtypeStruct((P, H, B, 2 * B), F32)],
        scratch_shapes=[pltpu.VMEM((2 * SL, E), F32), pltpu.VMEM((2 * SL, E), F32), slab, slab, slab, slab, slab],
        name=name, compiler_params=_cparams(("parallel", "arbitrary")),
    )(proj, proj, proj, proj, proj, bias, y, dy, lse, proj, y, dy, lse)


def _adamw(w, g, m, v, name, br=128):
    R, C = w.shape
    br = br if R % br == 0 else R

    def body(w_ref, g_ref, m_ref, v_ref, g_out, d_ref, nm_ref, nv_ref):
        g_ = g_ref[...]
        g_out[...] = g_
        m_ = ADAM_B1 * m_ref[...] + (1.0 - ADAM_B1) * g_
        v_ = ADAM_B2 * v_ref[...] + (1.0 - ADAM_B2) * jnp.square(g_)
        m_hat = m_ / (1.0 - ADAM_B1 ** ADAM_STEP)
        v_hat = v_ / (1.0 - ADAM_B2 ** ADAM_STEP)
        d_ref[...] = -ADAM_LR * (m_hat / (jnp.sqrt(v_hat) + ADAM_EPS) + ADAM_WD * w_ref[...])
        nm_ref[...] = m_
        nv_ref[...] = v_

    blk = pl.BlockSpec((br, C), lambda i: (i, 0))
    return pl.pallas_call(
        body, grid=(R // br,), in_specs=[blk] * 4, out_specs=[blk] * 4,
        out_shape=[jax.ShapeDtypeStruct((R, C), F32)] * 4,
        name=name, compiler_params=_cparams(("parallel",)),
    )(w, g, m, v)


_HBM = pl.BlockSpec(memory_space=pltpu.HBM)
_SEM = pl.BlockSpec(memory_space=pltpu.SEMAPHORE)
_ANY = pl.BlockSpec(memory_space=pl.ANY)
_VMEM = pl.BlockSpec(memory_space=pltpu.VMEM)
_TOKEN = jax.ShapeDtypeStruct((8, 128), F32)


def _split_params():
    return pltpu.CompilerParams(has_side_effects=pltpu.SideEffectType.DATAFLOW_SIDE_EFFECTING)


def _place():
    x, y, c = lax.axis_index("x"), lax.axis_index("y"), lax.axis_index("c")
    chips = [(1 - x, y), (x, 1 - y), (1 - x, 1 - y)]
    return x, y, c, chips


def _tie(v, tokens, name):
    flat = v.reshape(1, -1)

    def body(v_ref, *rest):
        rest[-1][...] = v_ref[...]

    return pl.pallas_call(body, in_specs=[_VMEM] + [_ANY] * len(tokens), out_specs=_VMEM,
                          out_shape=jax.ShapeDtypeStruct(flat.shape, flat.dtype), name=name,
                          compiler_params=_cparams())(flat, *tokens).reshape(v.shape)


def _row_block(R, pref=256):
    return _tile(R, pref) if R % 128 == 0 else R


def _slot():
    return 2 * lax.axis_index("x") + lax.axis_index("y")


def _cast_into_slot(w, layer, name):
    _, R, C = w.shape
    br = _row_block(R)

    def body(w_ref, out_ref):
        out_ref[...] = w_ref[...].astype(out_ref.dtype)

    return pl.pallas_call(
        body, grid=(R // br,),
        in_specs=[pl.BlockSpec((None, br, C), lambda i: (layer, i, 0))],
        out_specs=pl.BlockSpec((None, br, C), lambda i: (_slot(), i, 0)),
        out_shape=jax.ShapeDtypeStruct((N_CHIPS, R, C), CDT),
        name=name, compiler_params=_cparams(("parallel",)),
    )(w)


def _gather_copies(src_ref, dst_ref, send_sems, recv_sems, incoming):
    Rh = src_ref.shape[1] // 2
    x, y, c, chips = _place()
    slot = 2 * x + y

    def half(ref, s, hf):
        return ref.at[s, pl.ds(hf * Rh, Rh), :]

    copies = []
    for j, (cx, cy) in enumerate(chips):
        for e in range(2):
            copies.append(pltpu.make_async_remote_copy(
                src_ref=half(src_ref, slot, c), dst_ref=half(dst_ref, 2 * cx + cy, e) if incoming else half(dst_ref, slot, c),
                send_sem=send_sems.at[2 * j + e], recv_sem=recv_sems.at[2 * j + (e if incoming else c)],
                device_id=(cx, cy, e), device_id_type=MESH))
    return copies


def _gather_start(buf, after, name):
    n_after = len(after)

    def body(*refs):
        buf_ref = refs[0]
        send_sems, recv_sems, out_ref, token = refs[1 + n_after:]
        for cp in _gather_copies(buf_ref, out_ref, send_sems, recv_sems, incoming=False):
            cp.start()
        token[...] = jnp.zeros(token.shape, token.dtype)

    return pl.pallas_call(
        body, in_specs=[_HBM] + [_ANY] * n_after, out_specs=(_SEM, _SEM, _HBM, _VMEM),
        out_shape=(pltpu.SemaphoreType.DMA((6,)), pltpu.SemaphoreType.DMA((6,)), pltpu.HBM(buf.shape, buf.dtype), _TOKEN),
        input_output_aliases={0: 2}, name=name, compiler_params=_split_params(),
    )(pltpu.with_memory_space_constraint(buf, pltpu.HBM), *after)


def _gather_wait(send_sems, recv_sems, buf, after, name):
    def body(buf_ref, send_sems, recv_sems, after_ref, out_ref):
        for cp in _gather_copies(buf_ref, out_ref, send_sems, recv_sems, incoming=False):
            cp.wait_send()
        for cp in _gather_copies(buf_ref, out_ref, send_sems, recv_sems, incoming=True):
            cp.wait_recv()

    return pl.pallas_call(
        body, in_specs=[_HBM, _SEM, _SEM, _ANY], out_specs=_HBM, out_shape=pltpu.HBM(buf.shape, buf.dtype),
        input_output_aliases={0: 0}, name=name, compiler_params=_split_params(),
    )(buf, send_sems, recv_sems, after)


def _relay_copies(src_ref, dst_ref, send_sems, recv_sems, stage, incoming):
    Rh = src_ref.shape[1] // 2
    x, y, c, chips = _place()
    copies = []
    for j, (cx, cy) in enumerate(chips):
        if stage == 0:
            src_slot, src_half, peer = 2 * x + y, c, (cx, cy, c)
            dst_slot, dst_half = (2 * cx + cy, c) if incoming else (src_slot, c)
        else:
            src_slot, src_half, peer = 2 * cx + cy, c, (x, y, 1 - c)
            dst_slot, dst_half = src_slot, (1 - c if incoming else c)
        copies.append(pltpu.make_async_remote_copy(
            src_ref=src_ref.at[src_slot, pl.ds(src_half * Rh, Rh), :],
            dst_ref=dst_ref.at[dst_slot, pl.ds(dst_half * Rh, Rh), :],
            send_sem=send_sems.at[j], recv_sem=recv_sems.at[j], device_id=peer, device_id_type=MESH))
    return copies


def _relay_start(buf, after, name):
    n_after = len(after)

    def body(*refs):
        buf_ref = refs[0]
        send_sems, recv_sems, out_ref, token = refs[1 + n_after:]
        for cp in _relay_copies(buf_ref, out_ref, send_sems, recv_sems, 0, incoming=False):
            cp.start()
        token[...] = jnp.zeros(token.shape, token.dtype)

    return pl.pallas_call(
        body, in_specs=[_HBM] + [_ANY] * n_after, out_specs=(_SEM, _SEM, _HBM, _VMEM),
        out_shape=(pltpu.SemaphoreType.DMA((3,)), pltpu.SemaphoreType.DMA((3,)), pltpu.HBM(buf.shape, buf.dtype), _TOKEN),
        input_output_aliases={0: 2}, name=name, compiler_params=_split_params(),
    )(pltpu.with_memory_space_constraint(buf, pltpu.HBM), *after)


def _relay_pass(send_sems, recv_sems, buf, after, name):
    def body(buf_ref, send0, recv0, after_ref, send1, recv1, out_ref):
        for cp in _relay_copies(buf_ref, out_ref, send0, recv0, 0, incoming=False):
            cp.wait_send()
        for cp in _relay_copies(buf_ref, out_ref, send0, recv0, 0, incoming=True):
            cp.wait_recv()
        for cp in _relay_copies(out_ref, out_ref, send1, recv1, 1, incoming=False):
            cp.start()

    return pl.pallas_call(
        body, in_specs=[_HBM, _SEM, _SEM, _ANY], out_specs=(_SEM, _SEM, _HBM),
        out_shape=(pltpu.SemaphoreType.DMA((3,)), pltpu.SemaphoreType.DMA((3,)), pltpu.HBM(buf.shape, buf.dtype)),
        input_output_aliases={0: 2}, name=name, compiler_params=_split_params(),
    )(buf, send_sems, recv_sems, after)


def _relay_wait(send_sems, recv_sems, buf, name):
    def body(buf_ref, send1, recv1, out_ref):
        for cp in _relay_copies(buf_ref, out_ref, send1, recv1, 1, incoming=False):
            cp.wait_send()
        for cp in _relay_copies(buf_ref, out_ref, send1, recv1, 1, incoming=True):
            cp.wait_recv()

    return pl.pallas_call(
        body, in_specs=[_HBM, _SEM, _SEM], out_specs=_HBM, out_shape=pltpu.HBM(buf.shape, buf.dtype),
        input_output_aliases={0: 0}, name=name, compiler_params=_split_params(),
    )(buf, send_sems, recv_sems)


def _scatter_copies(g_ref, land_ref, send_sems, recv_sems, incoming):
    Rh = g_ref.shape[1] // 2
    x, y, c, _ = _place()
    me = 4 * x + 2 * y + c
    copies = []
    for k in range(1, N_DEV):
        px, py, pc = (x + (k >> 2)) % 2, (y + ((k >> 1) & 1)) % 2, (c + (k & 1)) % 2
        copies.append(pltpu.make_async_remote_copy(
            src_ref=g_ref.at[2 * px + py, pl.ds(pc * Rh, Rh), :],
            dst_ref=land_ref.at[4 * px + 2 * py + pc if incoming else me],
            send_sem=send_sems.at[k - 1], recv_sem=recv_sems.at[k - 1], device_id=(px, py, pc), device_id_type=MESH))
    return copies


def _scatter_start(g, name):
    ns, R, C = g.shape

    def body(g_ref, land_ref, send_sems, recv_sems, g_thru, land_thru, token):
        for cp in _scatter_copies(g_ref, land_thru, send_sems, recv_sems, incoming=False):
            cp.start()
        token[...] = jnp.zeros(token.shape, token.dtype)

    land = lax.empty((N_DEV, R // 2, C), g.dtype)
    n = N_DEV - 1
    return pl.pallas_call(
        body, in_specs=[_HBM, _HBM], out_specs=(_SEM, _SEM, _HBM, _HBM, _VMEM),
        out_shape=(pltpu.SemaphoreType.DMA((n,)), pltpu.SemaphoreType.DMA((n,)), pltpu.HBM(g.shape, g.dtype),
                   pltpu.HBM(land.shape, land.dtype), _TOKEN),
        input_output_aliases={0: 2, 1: 3}, name=name, compiler_params=_split_params(),
    )(pltpu.with_memory_space_constraint(g, pltpu.HBM), pltpu.with_memory_space_constraint(land, pltpu.HBM))


def _scatter_wait(send_sems, recv_sems, g, land, after, name):
    def body(g_ref, land_ref, send_sems, recv_sems, after_ref, g_out, land_out):
        for cp in _scatter_copies(g_ref, land_out, send_sems, recv_sems, incoming=False):
            cp.wait_send()
        for cp in _scatter_copies(g_ref, land_out, send_sems, recv_sems, incoming=True):
            cp.wait_recv()

    return pl.pallas_call(
        body, in_specs=[_HBM, _HBM, _SEM, _SEM, _ANY], out_specs=(_HBM, _HBM),
        out_shape=(pltpu.HBM(g.shape, g.dtype), pltpu.HBM(land.shape, land.dtype)),
        input_output_aliases={0: 0, 1: 1}, name=name, compiler_params=_split_params(),
    )(g, land, send_sems, recv_sems, after)


def _device_sum(land, g, layer, n_layers, prev, name):
    nd, Rh, C = land.shape
    br = _row_block(Rh)
    nb = Rh // br
    core = lambda: lax.axis_index("c")
    me = lambda: 2 * _slot() + core()

    def body(*refs):
        own = refs[nd][...]
        acc = None
        for d in range(nd):
            t = jnp.where(me() == d, own, refs[d][...]).astype(F32)
            acc = t if acc is None else acc + t
        refs[-1][...] = acc

    def piece(d):
        return pl.BlockSpec((None, br, C), lambda i: (jnp.where(me() == d, (d + 1) % nd, d), i, 0))

    ins = [land] * nd + [g] + ([prev] if prev is not None else [])
    return pl.pallas_call(
        body, grid=(nb,),
        in_specs=[piece(d) for d in range(nd)]
        + [pl.BlockSpec((None, br, C), lambda i: (_slot(), core() * nb + i, 0))]
        + ([_ANY] if prev is not None else []),
        out_specs=pl.BlockSpec((None, br, C), lambda i: (layer, core() * nb + i, 0)),
        out_shape=jax.ShapeDtypeStruct((n_layers, 2 * Rh, C), F32),
        input_output_aliases={nd + 1: 0} if prev is not None else {},
        name=name, compiler_params=_cparams(("parallel",)),
    )(*ins)


def _join_copy(src_ref, dst_ref, layer, send_sem, recv_sem, incoming):
    Rh = src_ref.shape[1] // 2
    x, y, c, _ = _place()
    mine, other = pl.ds(c * Rh, Rh), pl.ds((1 - c) * Rh, Rh)
    return pltpu.make_async_remote_copy(src_ref=src_ref.at[layer, mine, :],
                                        dst_ref=dst_ref.at[layer, other if incoming else mine, :],
                                        send_sem=send_sem, recv_sem=recv_sem, device_id=(x, y, 1 - c),
                                        device_id_type=MESH)


def _join_start(g, layer, name):
    def body(g_ref, send_sem, recv_sem, out_ref, token):
        _join_copy(g_ref, out_ref, layer, send_sem, recv_sem, incoming=False).start()
        token[...] = jnp.zeros(token.shape, token.dtype)

    return pl.pallas_call(
        body, in_specs=[_HBM], out_specs=(_SEM, _SEM, _HBM, _VMEM),
        out_shape=(pltpu.SemaphoreType.DMA(()), pltpu.SemaphoreType.DMA(()), pltpu.HBM(g.shape, g.dtype), _TOKEN),
        input_output_aliases={0: 2}, name=name, compiler_params=_split_params(),
    )(pltpu.with_memory_space_constraint(g, pltpu.HBM))


def _join_wait(send_sem, recv_sem, g, layer, after, name):
    def body(g_ref, send_sem, recv_sem, after_ref, out_ref):
        _join_copy(g_ref, out_ref, layer, send_sem, recv_sem, incoming=False).wait_send()
        _join_copy(g_ref, out_ref, layer, send_sem, recv_sem, incoming=True).wait_recv()

    return pl.pallas_call(
        body, in_specs=[_HBM, _SEM, _SEM, _ANY], out_specs=_HBM, out_shape=pltpu.HBM(g.shape, g.dtype),
        input_output_aliases={0: 0}, name=name, compiler_params=_split_params(),
    )(g, send_sem, recv_sem, after)


def _all_reduce_small(v, name):
    rows, cols = v.shape

    def body(v_ref, out_ref, buf, send_sems, recv_sems):
        x, y, c, _ = _place()
        me = 4 * x + 2 * y + c
        buf[me] = v_ref[...]
        peers = []
        for k in range(1, N_DEV):
            px, py, pc = (x + (k >> 2)) % 2, (y + ((k >> 1) & 1)) % 2, (c + (k & 1)) % 2
            peers.append((px, py, pc))
        sends = []
        for k, peer in enumerate(peers):
            cp = pltpu.make_async_remote_copy(src_ref=v_ref, dst_ref=buf.at[me], send_sem=send_sems.at[k],
                                              recv_sem=recv_sems.at[k], device_id=peer, device_id_type=MESH)
            cp.start()
            sends.append(cp)
        for k, (px, py, pc) in enumerate(peers):
            pltpu.make_async_remote_copy(src_ref=v_ref, dst_ref=buf.at[4 * px + 2 * py + pc], send_sem=send_sems.at[k],
                                         recv_sem=recv_sems.at[k], device_id=(px, py, pc),
                                         device_id_type=MESH).wait_recv()
        for cp in sends:
            cp.wait_send()
        acc = buf[0]
        for i in range(1, N_DEV):
            acc = acc + buf[i]
        out_ref[...] = acc

    vmem = pl.BlockSpec(memory_space=pltpu.VMEM)
    return pl.pallas_call(
        body, in_specs=[vmem], out_specs=vmem, out_shape=jax.ShapeDtypeStruct((rows, cols), F32),
        scratch_shapes=[pltpu.VMEM((N_DEV, rows, cols), F32), pltpu.SemaphoreType.DMA((N_DEV - 1,)),
                        pltpu.SemaphoreType.DMA((N_DEV - 1,))],
        name=name, compiler_params=pltpu.CompilerParams(),
    )(v)


def _reduce_scatter_sum(started, after, layer, n_layers, prev, tag):
    send_sems, recv_sems, g, land, _ = started
    g, land = _scatter_wait(send_sems, recv_sems, g, land, after, f"rs_wait_{tag}")
    f = _device_sum(land, g, layer, n_layers, prev, f"rs_sum_{tag}")
    return _join_start(f, layer, f"rs_join_start_{tag}")


def _split_w_in(wg, Hf, name):
    ns, D, cols = wg.shape
    a = 3 * Hf * HEAD_DIM
    n6 = ns * cols - Hf
    br = _row_block(D)

    def body(w_ref, w6_ref, wf_ref):
        nat = jnp.concatenate([w_ref[s] for s in range(ns)], axis=1)
        w6_ref[...] = jnp.concatenate([nat[:, :a], nat[:, a + Hf:]], axis=1)
        wf_ref[...] = nat[:, a:a + Hf]

    w6, wf = pl.pallas_call(
        body, grid=(D // br,), in_specs=[pl.BlockSpec((ns, br, cols), lambda i: (0, i, 0))],
        out_specs=[pl.BlockSpec((br, n6), lambda i: (i, 0)), pl.BlockSpec((br, Hf), lambda i: (i, 0))],
        out_shape=[jax.ShapeDtypeStruct((D, n6), wg.dtype), jax.ShapeDtypeStruct((D, Hf), wg.dtype)],
        name=name, compiler_params=_cparams(("parallel",)),
    )(wg)
    return w6, wf.T


def _join_dw_in(dw6, dwf_t, Hf, name):
    D, n6 = dw6.shape
    a = 3 * Hf * HEAD_DIM
    cols = (n6 + Hf) // N_CHIPS
    br = _row_block(D)

    def body(w6_ref, wf_ref, out_ref):
        w6 = w6_ref[...]
        nat = jnp.concatenate([w6[:, :a], wf_ref[...], w6[:, a:]], axis=1)
        for s in range(N_CHIPS):
            out_ref[s] = nat[:, s * cols:(s + 1) * cols]

    return pl.pallas_call(
        body, grid=(D // br,),
        in_specs=[pl.BlockSpec((br, n6), lambda i: (i, 0)), pl.BlockSpec((br, Hf), lambda i: (i, 0))],
        out_specs=pl.BlockSpec((N_CHIPS, br, cols), lambda i: (0, i, 0)),
        out_shape=jax.ShapeDtypeStruct((N_CHIPS, D, cols), dw6.dtype),
        name=name, compiler_params=_cparams(("parallel",)),
    )(dw6, dwf_t.T.astype(dw6.dtype))


def _tied(v, tokens, name):
    return _tie(v, tokens, name) if tokens else v


def _layer_fwd(x, p, weight, bias, tokens, tag):
    Hf, Hd = p["forget_b"].shape[0], bias.shape[1]
    h1 = _rms_fwd(x, _tied(p["norm1_g"], tokens, f"tie_norm1_{tag}"), f"norm1_{tag}")
    w6, wf_t = _split_w_in(weight("w_in", h1), Hf, f"split_w_in_{tag}")
    n_a = 3 * Hf * HEAD_DIM
    proj_a = _mm_nn(h1, w6, f"proj_a_{tag}", [CDT], epi=lambda acc: (acc,), b_cols=(0, n_a))[0]
    proj_b = _mm_nn(h1, w6, f"proj_b_{tag}", [F32], b_cols=(n_a, w6.shape[1] - n_a))[0]
    f_t = _mm_nt(wf_t, h1, f"fproj_{tag}", [F32])[0]
    qc, kc = _fox_bias_operands(_gates_fwd(f_t, p["forget_b"], f"gates_{tag}"), f"fox_operands_{tag}")
    y_a, lse_a = _fox_fwd(proj_a, qc, kc, Hf, f"fox_{tag}")
    y_b, lse_b = _dil_fwd(proj_b, bias, Hd, f"dil_{tag}")
    mixed = _pair_norm_fwd(y_a, y_b, p["outnorm_a_g"], p["outnorm_b_g"], f"norm_ab_{tag}")
    w_out = weight("w_out", mixed)
    w_out = w_out.reshape(-1, w_out.shape[2])
    x1 = _mm_nn(mixed, w_out, f"attn_out_{tag}", [F32], extras=[x])[0]
    h2 = _rms_fwd(x1, p["norm2_g"], f"norm2_{tag}")
    w_mi = weight("w_mlp_in", h2)
    u, act = _mm_nn(h2, w_mi, f"mlp_in_{tag}", [CDT, CDT], b_slots=True,
                    epi=lambda acc: (acc, jnp.square(jnp.maximum(acc, 0.0))))
    w_mo = weight("w_mlp_out", act)
    w_mo = w_mo.reshape(-1, w_mo.shape[2])
    x2 = _mm_nn(act, w_mo, f"mlp_out_{tag}", [F32], extras=[x1])[0]
    saved = dict(x=x, h1=h1, proj_a=proj_a, proj_b=proj_b, f_t=f_t, qc=qc, kc=kc, y_a=y_a, lse_a=lse_a, y_b=y_b,
                 lse_b=lse_b, mixed=mixed, x1=x1, h2=h2, u=u, act=act, w6=w6, wf_t=wf_t, w_out=w_out, w_mi=w_mi,
                 w_mo=w_mo)
    return x2, saved


def _layer_bwd(dx2, dx2c, p, send, bias, sv, defer_w_out, tag):
    Hf, Hd = p["forget_b"].shape[0], bias.shape[1]
    E = HEAD_DIM
    rows = lambda g: g.reshape(N_CHIPS, -1, g.shape[1])
    du = _mm_nt(dx2c, sv["w_mo"], f"d_act_{tag}", [CDT], extras=[sv["u"]],
                epi=lambda acc, u: (acc * (2.0 * jnp.maximum(u.astype(F32), 0.0)),))[0]
    tokens = send("w_mlp_out", rows(_mm_tn(sv["act"], dx2c, f"dw_mlp_out_{tag}", CDT)))
    dh2 = _mm_nt(du, sv["w_mi"], f"d_h2_{tag}", [F32], b_slots=True)[0]
    tokens = tokens + send("w_mlp_in", _mm_tn(sv["h2"], du, f"dw_mlp_in_{tag}", CDT, out_slots=N_CHIPS))
    dx1, dx1c, g_norm2 = _rms_bwd(sv["x1"], _tied(p["norm2_g"], tokens, f"tie_norm2_{tag}"), dh2, dx2,
                                  f"d_norm2_{tag}")
    dmixed = _mm_nt(dx1c, sv["w_out"], f"d_mixed_{tag}", [F32])[0]
    send_w_out = lambda: send("w_out", rows(_mm_tn(sv["mixed"], dx1c, f"dw_out_{tag}", CDT)))
    tokens = [] if defer_w_out else send_w_out()
    dy_a, dy_b, g_na, g_nb = _pair_norm_bwd(sv["y_a"], sv["y_b"], _tied(p["outnorm_a_g"], tokens, f"tie_norm_a_{tag}"),
                                            p["outnorm_b_g"], dmixed, f"d_norm_ab_{tag}")
    dq_a, dcq, dk_a, dv_a, dck = _fox_bwd(sv["proj_a"], sv["qc"], sv["kc"], sv["lse_a"], sv["y_a"], dy_a, Hf,
                                          f"fox_bwd_{tag}")
    df, dfc, g_fb = _gates_bwd(sv["f_t"], p["forget_b"], dcq[:, ::E].T, dck.reshape(Hf, -1), f"d_gates_{tag}")
    dq_b, dk_b, dv_b, dbias = _dil_bwd(sv["proj_b"], bias, sv["y_b"], dy_b, sv["lse_b"], Hd, f"dil_bwd_{tag}")
    dproj = jnp.concatenate([dq_a, dk_a, dv_a, dq_b, dk_b, dv_b], axis=1)
    g_w6 = _mm_tn(sv["h1"], dproj, f"dw_in_{tag}", CDT)
    g_wf_t = _mm_nn(dfc, sv["h1"], f"dw_f_{tag}", [F32])[0]
    tokens = send("w_in", _join_dw_in(g_w6, g_wf_t, Hf, f"join_dw_in_{tag}"))
    dh1_f = _mm_tn(dfc, _tied(sv["wf_t"], tokens, f"tie_wf_{tag}"), f"d_h1_f_{tag}", F32)
    dh1 = _mm_nt(dproj, sv["w6"], f"d_h1_{tag}", [F32], extras=[dh1_f])[0]
    dx, dxc, g_norm1 = _rms_bwd(sv["x"], p["norm1_g"], dh1, dx1, f"d_norm1_{tag}")
    grads = dict(norm1_g=g_norm1[0], norm2_g=g_norm2[0], outnorm_a_g=g_na[0], outnorm_b_g=g_nb[0],
                 forget_b=g_fb[:, 0], dbias=dbias)
    return dx, dxc, grads, (send_w_out if defer_w_out else None)


_LAYER_SMALL = ("norm1_g", "forget_b", "outnorm_a_g", "outnorm_b_g", "norm2_g")


def _local_step(x, target, small, weight, send, tokens):
    depth = small["norm1_g"].shape[0]
    buckets = _bucket_table()
    bias = _bias_table(small["rel_bias"], buckets, "bias_table")
    layers, saved = [], []
    for l in range(depth):
        p = {k: small[k][l] for k in _LAYER_SMALL}
        layers.append(p)
        x, sv = _layer_fwd(x, p, functools.partial(weight, l), bias, tokens if l == 0 else [], f"l{l}")
        saved.append(sv)
    dx, dxc, g_final, loss = _loss_bwd(x, small["final_norm_g"], target, "loss")
    layer_grads = [None] * depth
    for l in reversed(range(depth)):
        dx, dxc, layer_grads[l], last = _layer_bwd(dx, dxc, layers[l], functools.partial(send, l), bias, saved[l],
                                                   l == 0, f"l{l}")
    tokens = last()
    dbias = functools.reduce(jnp.add, [g["dbias"] for g in layer_grads])
    g_rel = _bias_table_bwd(dbias, buckets, "d_bias_table")[:, 0, :].T
    small_grads = dict(final_norm_g=g_final[0], rel_bias=g_rel,
                       **{k: jnp.stack([g[k] for g in layer_grads]) for k in _LAYER_SMALL})
    return loss[0, 0], dx, small_grads, tokens


_BIG = ("w_in", "w_out", "w_mlp_in", "w_mlp_out")
_SMALL = ("norm1_g", "forget_b", "rel_bias", "outnorm_a_g", "outnorm_b_g", "norm2_g", "final_norm_g")
_ORDER = ("norm1_g", "w_in", "forget_b", "rel_bias", "outnorm_a_g", "outnorm_b_g", "w_out", "norm2_g", "w_mlp_in",
          "w_mlp_out", "final_norm_g")


def _pack_small(d):
    flat = jnp.concatenate([d[k].reshape(-1) for k in _SMALL])
    rows = -(-flat.shape[0] // (8 * SMALL_COLS)) * 8
    return jnp.pad(flat, (0, rows * SMALL_COLS - flat.shape[0])).reshape(rows, SMALL_COLS)


def _unpack_small(packed, like):
    flat, out, at = packed.reshape(-1), {}, 0
    for k in _SMALL:
        n = like[k].size
        out[k] = flat[at:at + n].reshape(like[k].shape)
        at += n
    return out


def kernel(x, norm1_g, w_in, forget_b, rel_bias, outnorm_a_g, outnorm_b_g, w_out, norm2_g, w_mlp_in, w_mlp_out, final_norm_g, loss_target, m_norm1_g, m_w_in, m_forget_b, m_rel_bias, m_outnorm_a_g, m_outnorm_b_g, m_w_out, m_norm2_g, m_w_mlp_in, m_w_mlp_out, m_final_norm_g, v_norm1_g, v_w_in, v_forget_b, v_rel_bias, v_outnorm_a_g, v_outnorm_b_g, v_w_out, v_norm2_g, v_w_mlp_in, v_w_mlp_out, v_final_norm_g):
    w = dict(norm1_g=norm1_g, w_in=w_in, forget_b=forget_b, rel_bias=rel_bias, outnorm_a_g=outnorm_a_g,
             outnorm_b_g=outnorm_b_g, w_out=w_out, norm2_g=norm2_g, w_mlp_in=w_mlp_in, w_mlp_out=w_mlp_out,
             final_norm_g=final_norm_g)
    m = dict(norm1_g=m_norm1_g, w_in=m_w_in, forget_b=m_forget_b, rel_bias=m_rel_bias, outnorm_a_g=m_outnorm_a_g,
             outnorm_b_g=m_outnorm_b_g, w_out=m_w_out, norm2_g=m_norm2_g, w_mlp_in=m_w_mlp_in,
             w_mlp_out=m_w_mlp_out, final_norm_g=m_final_norm_g)
    v = dict(norm1_g=v_norm1_g, w_in=v_w_in, forget_b=v_forget_b, rel_bias=v_rel_bias, outnorm_a_g=v_outnorm_a_g,
             outnorm_b_g=v_outnorm_b_g, w_out=v_w_out, norm2_g=v_norm2_g, w_mlp_in=v_w_mlp_in,
             w_mlp_out=v_w_mlp_out, final_norm_g=v_final_norm_g)
    depth = w_in.shape[0]
    small = {k: w[k] for k in _SMALL}

    gathers, tokens = {}, []
    for l in range(depth):
        for k in _BIG:
            buf = _cast_into_slot(w[k], l, f"cast_{k}_l{l}")
            start = _gather_start if gathers else _relay_start
            send_sems, recv_sems, buf, token = start(buf, tokens, f"gather_start_{k}_l{l}")
            gathers[l, k], tokens = (send_sems, recv_sems, buf), [token]
    first = next(iter(gathers))

    def weight(l, k, after):
        if (l, k) == first:
            return _relay_wait(*_relay_pass(*gathers[l, k], after, f"gather_pass_{k}_l{l}"), f"gather_wait_{k}_l{l}")
        return _gather_wait(*gathers[l, k], after, f"gather_wait_{k}_l{l}")

    scatters = {}

    def send(l, k, g):
        scatters[l, k] = _scatter_start(g, f"rs_start_{k}_l{l}")
        return [scatters[l, k][4]]

    loss, grad_x, small_grads, tokens = _local_step(x[0], loss_target[0], small, weight, send, tokens)
    loss = lax.psum(loss, ("x", "y", "c"))

    grads, delta, new_m, new_v = {}, {}, {}, {}
    packed = _tied(_pack_small(small_grads), tokens, "tie_small")
    after, seen, joining = packed, {k: 0 for k in _BIG}, None

    def joined(after):
        (l, k), (send_sem, recv_sem, g) = joining
        grads[k] = _join_wait(send_sem, recv_sem, g, l, after, f"rs_join_wait_{k}_l{l}")
        seen[k] += 1
        if seen[k] < depth:
            return after
        shape = w[k].shape
        flat = lambda t: t.reshape(-1, shape[-1])
        g_, d_, m_, v_ = _adamw(flat(w[k]), flat(grads[k]), flat(m[k]), flat(v[k]), f"adamw_{k}")
        grads[k], delta[k], new_m[k], new_v[k] = (t.reshape(shape) for t in (g_, d_, m_, v_))
        return d_

    for (l, k), started in scatters.items():
        assert joining is None or joining[0][1] != k
        send_sem, recv_sem, g, token = _reduce_scatter_sum(started, after, l, depth, grads.get(k), f"{k}_l{l}")
        if joining is not None:
            after = joined(token)
        joining = ((l, k), (send_sem, recv_sem, g))
    after = joined(after)
    small_sums = _all_reduce_small(_tied(packed, [after], "tie_small_sums"), "small_all_reduce")
    grads.update(_unpack_small(small_sums, small))
    _, d_, m_, v_ = _adamw(_pack_small(small), _pack_small({k: grads[k] for k in _SMALL}),
                           _pack_small({k: m[k] for k in _SMALL}), _pack_small({k: v[k] for k in _SMALL}), "adamw_small")
    delta.update(_unpack_small(d_, small))
    new_m.update(_unpack_small(m_, small))
    new_v.update(_unpack_small(v_, small))

    return (loss, grad_x[None], *[grads[k] for k in _ORDER], *[delta[k] for k in _ORDER],
            *[new_m[k] for k in _ORDER], *[new_v[k] for k in _ORDER])
```

```python
import functools

import jax
import jax.numpy as jnp
from jax import lax
from jax.experimental import pallas as pl
from jax.experimental.pallas import tpu as pltpu

F32 = jnp.float32
CDT = jnp.bfloat16
HEAD_DIM = 128
NORM_EPS = 1e-6
NEG_INF = -1e30
LOG2E = 1.4426950408889634
REL_BUCKETS = 32
REL_MAX_DISTANCE = 2048
DIL_PATTERNS = ((128, 1), (512, 4), (2048, 16))
DIL_BLOCK = 128
ADAM_LR, ADAM_B1, ADAM_B2, ADAM_EPS, ADAM_WD, ADAM_STEP = 0.001, 0.9, 0.999, 1e-08, 0.01, 10
N_CHIPS = 4
N_DEV = 8
VMEM_LIMIT_BYTES = 56 * 1024 * 1024
SMALL_COLS = 1024
MESH = pl.DeviceIdType.MESH


def _cparams(sem=None):
    return pltpu.CompilerParams(dimension_semantics=sem, vmem_limit_bytes=VMEM_LIMIT_BYTES)


def _tile(dim, pref):
    t = min(pref, dim)
    t -= t % 128
    while t >= 128:
        if dim % t == 0:
            return t
        t -= 128
    return dim


def _rowwise(fn, ins, out_dtypes, name, bs=256, consts=()):
    R, C = ins[0].shape
    bs = min(bs, R)
    n_in, n_c = len(ins), len(consts)

    def body(*refs):
        vals = [r[...] for r in refs[:n_in + n_c]]
        res = fn(*vals)
        for o, r in zip(refs[n_in + n_c:], res):
            o[...] = r.astype(o.dtype)

    row = pl.BlockSpec((bs, C), lambda i: (i, 0))
    return pl.pallas_call(
        body, grid=(R // bs,),
        in_specs=[row] * n_in + [pl.BlockSpec((1, c.shape[-1]), lambda i: (0, 0)) for c in consts],
        out_specs=[row] * len(out_dtypes),
        out_shape=[jax.ShapeDtypeStruct((R, C), d) for d in out_dtypes],
        name=name, compiler_params=_cparams(("parallel",)),
    )(*ins, *[c.reshape(1, -1) for c in consts])


def _rms_fwd(x, g, name):
    def fn(xf, gg):
        r = lax.rsqrt(jnp.mean(xf * xf, axis=-1, keepdims=True) + NORM_EPS)
        return ((xf * r) * gg,)
    return _rowwise(fn, [x], [CDT], name, consts=[g])[0]


def _rms_bwd(x, g, dh, dres, name, bs=256):
    S, D = x.shape
    bs = min(bs, S)
    has_res = dres is not None

    def body(*refs):
        x_ref, g_ref, dh_ref = refs[:3]
        dx_ref, dxc_ref, dg_ref = refs[-3:]
        xf = x_ref[...]
        r = lax.rsqrt(jnp.mean(xf * xf, axis=-1, keepdims=True) + NORM_EPS)
        xhat = xf * r
        dh_ = dh_ref[...].astype(F32)
        dxhat = dh_ * g_ref[...]
        dx = r * (dxhat - xhat * jnp.mean(dxhat * xhat, axis=-1, keepdims=True))
        if has_res:
            dx = dx + refs[3][...]
        dx_ref[...] = dx
        dxc_ref[...] = dx.astype(dxc_ref.dtype)
        part = jnp.sum(dh_ * xhat, axis=0, keepdims=True)

        @pl.when(pl.program_id(0) == 0)
        def _():
            dg_ref[...] = part

        @pl.when(pl.program_id(0) > 0)
        def _():
            dg_ref[...] += part

    row = pl.BlockSpec((bs, D), lambda i: (i, 0))
    one = pl.BlockSpec((1, D), lambda i: (0, 0))
    ins = [x, g.reshape(1, D), dh] + ([dres] if has_res else [])
    return pl.pallas_call(
        body, grid=(S // bs,),
        in_specs=[row, one, row] + ([row] if has_res else []),
        out_specs=[row, row, one],
        out_shape=[jax.ShapeDtypeStruct((S, D), F32), jax.ShapeDtypeStruct((S, D), CDT),
                   jax.ShapeDtypeStruct((1, D), F32)],
        name=name, compiler_params=_cparams(("arbitrary",)),
    )(*ins)


def _pair_norm_fwd(y_a, y_b, g_a, g_b, name, bs=256):
    S, Da = y_a.shape
    Db = y_b.shape[1]
    bs = min(bs, S)

    def body(a_ref, b_ref, ga_ref, gb_ref, o_ref):
        def norm(x, g):
            r = lax.rsqrt(jnp.mean(x * x, axis=-1, keepdims=True) + NORM_EPS)
            return ((x * r) * g).astype(o_ref.dtype)
        o_ref[:, :Da] = norm(a_ref[...], ga_ref[...])
        o_ref[:, Da:] = norm(b_ref[...], gb_ref[...])

    row = lambda n: pl.BlockSpec((bs, n), lambda i: (i, 0))
    one = lambda n: pl.BlockSpec((1, n), lambda i: (0, 0))
    return pl.pallas_call(
        body, grid=(S // bs,), in_specs=[row(Da), row(Db), one(Da), one(Db)], out_specs=row(Da + Db),
        out_shape=jax.ShapeDtypeStruct((S, Da + Db), CDT), name=name, compiler_params=_cparams(("parallel",)),
    )(y_a, y_b, g_a.reshape(1, Da), g_b.reshape(1, Db))


def _pair_norm_bwd(y_a, y_b, g_a, g_b, dmixed, name, bs=256):
    S, Da = y_a.shape
    Db = y_b.shape[1]
    bs = min(bs, S)

    def body(a_ref, b_ref, ga_ref, gb_ref, dm_ref, da_ref, db_ref, dga_ref, dgb_ref):
        def one(x_ref, g_ref, dh, dx_ref, dg_ref):
            xf = x_ref[...]
            r = lax.rsqrt(jnp.mean(xf * xf, axis=-1, keepdims=True) + NORM_EPS)
            xhat = xf * r
            dxhat = dh * g_ref[...]
            dx_ref[...] = r * (dxhat - xhat * jnp.mean(dxhat * xhat, axis=-1, keepdims=True))
            part = jnp.sum(dh * xhat, axis=0, keepdims=True)

            @pl.when(pl.program_id(0) == 0)
            def _():
                dg_ref[...] = part

            @pl.when(pl.program_id(0) > 0)
            def _():
                dg_ref[...] += part

        dm = dm_ref[...]
        one(a_ref, ga_ref, dm[:, :Da], da_ref, dga_ref)
        one(b_ref, gb_ref, dm[:, Da:], db_ref, dgb_ref)

    row = lambda n: pl.BlockSpec((bs, n), lambda i: (i, 0))
    one_ = lambda n: pl.BlockSpec((1, n), lambda i: (0, 0))
    return pl.pallas_call(
        body, grid=(S // bs,), in_specs=[row(Da), row(Db), one_(Da), one_(Db), row(Da + Db)],
        out_specs=[row(Da), row(Db), one_(Da), one_(Db)],
        out_shape=[jax.ShapeDtypeStruct((S, Da), F32), jax.ShapeDtypeStruct((S, Db), F32),
                   jax.ShapeDtypeStruct((1, Da), F32), jax.ShapeDtypeStruct((1, Db), F32)],
        name=name, compiler_params=_cparams(("arbitrary",)),
    )(y_a, y_b, g_a.reshape(1, Da), g_b.reshape(1, Db), dmixed)


def _loss_bwd(x, g, target, name, bs=256):
    S, D = x.shape
    bs = min(bs, S)

    def body(x_ref, g_ref, t_ref, dx_ref, dxc_ref, dg_ref, loss_ref):
        xf = x_ref[...]
        r = lax.rsqrt(jnp.mean(xf * xf, axis=-1, keepdims=True) + NORM_EPS)
        xhat = xf * r
        err = xhat * g_ref[...] - t_ref[...]
        lpart = 0.5 * jnp.sum(jnp.mean(err * err, axis=-1, keepdims=True), axis=0, keepdims=True)
        dy = err / D
        dxhat = dy * g_ref[...]
        dx = r * (dxhat - xhat * jnp.mean(dxhat * xhat, axis=-1, keepdims=True))
        dx_ref[...] = dx
        dxc_ref[...] = dx.astype(dxc_ref.dtype)
        gpart = jnp.sum(dy * xhat, axis=0, keepdims=True)

        @pl.when(pl.program_id(0) == 0)
        def _():
            dg_ref[...] = gpart
            loss_ref[...] = lpart

        @pl.when(pl.program_id(0) > 0)
        def _():
            dg_ref[...] += gpart
            loss_ref[...] += lpart

    row = pl.BlockSpec((bs, D), lambda i: (i, 0))
    one = pl.BlockSpec((1, D), lambda i: (0, 0))
    return pl.pallas_call(
        body, grid=(S // bs,),
        in_specs=[row, one, row],
        out_specs=[row, row, one, pl.BlockSpec((1, 1), lambda i: (0, 0))],
        out_shape=[jax.ShapeDtypeStruct((S, D), F32), jax.ShapeDtypeStruct((S, D), CDT),
                   jax.ShapeDtypeStruct((1, D), F32), jax.ShapeDtypeStruct((1, 1), F32)],
        name=name, compiler_params=_cparams(("arbitrary",)),
    )(x, g.reshape(1, D), target)


_NN = (((1,), (0,)), ((), ()))
_NT = (((1,), (1,)), ((), ()))
_TN = (((0,), (0,)), ((), ()))


def _mm(a, b, *, M, N, K, a_spec, b_spec, o_spec, dims, tm, tn, tk, name, out_shapes, extras=(), epi=None):
    nk = K // tk
    n_ex, n_out = len(extras), len(out_shapes)
    in_place = epi is None
    if in_place:
        assert n_out == 1 and n_ex <= 1 and out_shapes[0].dtype == F32
        epi = lambda acc, *r: (acc + r[0] if r else acc,)

    def body(*refs):
        a_ref, b_ref = refs[0], refs[1]
        ex = refs[2:2 + n_ex]
        outs = refs[2 + n_ex:2 + n_ex + n_out]
        part = lax.dot_general(a_ref[...], b_ref[...], dims, preferred_element_type=F32)

        def finish(acc):
            for o, r in zip(outs, epi(acc, *[e[...] for e in ex])):
                o[...] = r.astype(o.dtype)

        if nk == 1:
            finish(part)
        elif in_place:
            k = pl.program_id(2)

            @pl.when(k == 0)
            def _():
                finish(part)

            @pl.when(k > 0)
            def _():
                outs[0][...] += part
        else:
            acc_ref = refs[-1]
            k = pl.program_id(2)

            @pl.when(k == 0)
            def _():
                acc_ref[...] = part

            @pl.when(k > 0)
            def _():
                acc_ref[...] += part

            @pl.when(k == nk - 1)
            def _():
                finish(acc_ref[...])

    ex_spec = pl.BlockSpec((tm, tn), lambda i, j, k: (i, j))
    return pl.pallas_call(
        body, grid=(M // tm, N // tn, nk),
        in_specs=[a_spec, b_spec] + [ex_spec] * n_ex,
        out_specs=[o_spec] * n_out,
        out_shape=out_shapes,
        scratch_shapes=[pltpu.VMEM((tm, tn), F32)] if nk > 1 and not in_place else [],
        name=name, compiler_params=_cparams(("parallel", "parallel", "arbitrary")),
    )(a, b, *extras)


def _mm_tiles(K):
    return (2048, 512, 2048) if K <= 2048 else (1024, 1024, 2048)


def _mm_nn(a, b, name, out_dtypes, extras=(), epi=None, b_slots=False, b_cols=None):
    M, K = a.shape
    tm, tn, tk = _mm_tiles(K)
    if b_slots:
        ns, _, Ns = b.shape
        N = ns * Ns
        tn = _tile(Ns, tn)
        npb = Ns // tn
        tk_ = _tile(K, tk)
        b_spec = pl.BlockSpec((None, tk_, tn), lambda i, j, k: (j // npb, k, j % npb))
    else:
        first, N = b_cols if b_cols is not None else (0, b.shape[1])
        tn = _tile(N, tn)
        assert first % tn == 0
        tk_ = _tile(K, tk)
        b_spec = pl.BlockSpec((tk_, tn), lambda i, j, k: (k, first // tn + j))
    tm = _tile(M, tm)
    return _mm(a, b, M=M, N=N, K=K, a_spec=pl.BlockSpec((tm, tk_), lambda i, j, k: (i, k)), b_spec=b_spec,
               o_spec=pl.BlockSpec((tm, tn), lambda i, j, k: (i, j)), dims=_NN, tm=tm, tn=tn, tk=tk_, name=name,
               out_shapes=[jax.ShapeDtypeStruct((M, N), d) for d in out_dtypes], extras=extras, epi=epi)


def _mm_nt(a, b, name, out_dtypes, extras=(), epi=None, b_slots=False):
    M, K = a.shape
    tm, tn, tk = _mm_tiles(K)
    tm = _tile(M, tm)
    if b_slots:
        ns, N, Ks = b.shape
        tk_ = _tile(Ks, tk)
        kpb = Ks // tk_
        tn = _tile(N, tn)
        b_spec = pl.BlockSpec((None, tn, tk_), lambda i, j, k: (k // kpb, j, k % kpb))
    else:
        N = b.shape[0]
        tk_ = _tile(K, tk)
        tn = _tile(N, tn)
        b_spec = pl.BlockSpec((tn, tk_), lambda i, j, k: (j, k))
    return _mm(a, b, M=M, N=N, K=K, a_spec=pl.BlockSpec((tm, tk_), lambda i, j, k: (i, k)), b_spec=b_spec,
               o_spec=pl.BlockSpec((tm, tn), lambda i, j, k: (i, j)), dims=_NT, tm=tm, tn=tn, tk=tk_, name=name,
               out_shapes=[jax.ShapeDtypeStruct((M, N), d) for d in out_dtypes], extras=extras, epi=epi)


def _mm_tn(a, b, name, out_dtype, out_slots=0, tm=2048, tn=1024, tk=2048):
    K, M = a.shape
    N = b.shape[1]
    tm, tk_ = _tile(M, tm), _tile(K, tk)
    if out_slots:
        Ns = N // out_slots
        tn = _tile(Ns, tn)
        npb = Ns // tn
        o_spec = pl.BlockSpec((None, tm, tn), lambda i, j, k: (j // npb, i, j % npb))
        out_shape = jax.ShapeDtypeStruct((out_slots, M, Ns), out_dtype)
    else:
        tn = _tile(N, tn)
        o_spec = pl.BlockSpec((tm, tn), lambda i, j, k: (i, j))
        out_shape = jax.ShapeDtypeStruct((M, N), out_dtype)
    return _mm(a, b, M=M, N=N, K=K, a_spec=pl.BlockSpec((tk_, tm), lambda i, j, k: (k, i)),
               b_spec=pl.BlockSpec((tk_, tn), lambda i, j, k: (k, j)), o_spec=o_spec, dims=_TN,
               tm=tm, tn=tn, tk=tk_, name=name, out_shapes=[out_shape],
               epi=None if out_dtype == F32 else (lambda acc: (acc,)))[0]


GATE_BLOCK = 512


def _split3(v):
    hi = v.astype(jnp.bfloat16)
    r1 = v - hi.astype(F32)
    mid = r1.astype(jnp.bfloat16)
    lo = (r1 - mid.astype(F32)).astype(jnp.bfloat16)
    return hi, mid, lo


def _exact_dot(v, tri):
    return functools.reduce(jnp.add, [jnp.dot(t, tri, preferred_element_type=F32) for t in _split3(v)])


def _gates_fwd(f_t, b, name):
    H, S = f_t.shape
    nb = _tile(S, GATE_BLOCK)
    inv_scale = HEAD_DIM ** 0.5

    def body(f_ref, b_ref, c_ref):
        upper = (lax.broadcasted_iota(jnp.int32, (nb, nb), 0)
                 <= lax.broadcasted_iota(jnp.int32, (nb, nb), 1)).astype(jnp.bfloat16)
        carry = jnp.zeros((H, 1), F32)
        for i in range(S // nb):
            z = f_ref[:, i * nb:(i + 1) * nb] + b_ref[...]
            logf = jnp.minimum(z, 0.0) - jnp.log1p(jnp.exp(-jnp.abs(z)))
            cs = _exact_dot(logf, upper) + carry
            for j, t in enumerate(_split3(cs * inv_scale)):
                c_ref[j, :, i * nb:(i + 1) * nb] = t
            carry = cs[:, nb - 1:nb]

    return pl.pallas_call(body, out_shape=jax.ShapeDtypeStruct((3, H, S), jnp.bfloat16), name=name,
                          compiler_params=_cparams())(f_t, b.reshape(H, 1))


def _gates_bwd(f_t, b, dcq, dck, name):
    H, S = f_t.shape
    nb = _tile(S, GATE_BLOCK)

    def body(f_ref, b_ref, dcq_ref, dck_ref, df_ref, dfc_ref, db_ref):
        lower = (lax.broadcasted_iota(jnp.int32, (nb, nb), 0)
                 >= lax.broadcasted_iota(jnp.int32, (nb, nb), 1)).astype(jnp.bfloat16)
        carry = jnp.zeros((H, 1), F32)
        db = jnp.zeros((H, 1), F32)
        for i in reversed(range(S // nb)):
            sl = slice(i * nb, (i + 1) * nb)
            dc = dcq_ref[:, sl] - dck_ref[:, sl]
            dlogf = _exact_dot(dc, lower) + carry
            carry = dlogf[:, 0:1]
            z = f_ref[:, sl] + b_ref[...]
            df = dlogf / (1.0 + jnp.exp(z))
            df_ref[:, sl] = df
            dfc_ref[:, sl] = df.astype(dfc_ref.dtype)
            db = db + jnp.sum(df, axis=1, keepdims=True)
        db_ref[...] = db

    return pl.pallas_call(
        body, out_shape=[jax.ShapeDtypeStruct((H, S), F32), jax.ShapeDtypeStruct((H, S), CDT),
                         jax.ShapeDtypeStruct((H, 1), F32)],
        name=name, compiler_params=_cparams())(f_t, b.reshape(H, 1), dcq, dck)


FOX_BLOCK = 1024


def _fox_bias_operands(csplit, name, bs=512):
    _, H, S = csplit.shape
    E = HEAD_DIM
    bs = _tile(S, bs)
    part = jnp.arange(3 * H)[:, None] // H
    head = jnp.arange(3 * H)[:, None] % H
    lane = jnp.arange(H * E)[None, :]
    place_q = (lane == head * E + part).astype(csplit.dtype)
    place_k = -(lane == head * E + 3 + part).astype(csplit.dtype)
    ones_q = ((lane % E >= 3) & (lane % E < 6)).astype(F32)
    ones_k = (lane % E < 3).astype(F32)

    def body(c_ref, pq_ref, pk_ref, oq_ref, ok_ref, qc_ref, kc_ref):
        c = c_ref[...]
        qc_ref[...] = (lax.dot_general(c, pq_ref[...], _TN, preferred_element_type=F32) + oq_ref[...]).astype(qc_ref.dtype)
        kc_ref[...] = (lax.dot_general(c, pk_ref[...], _TN, preferred_element_type=F32) + ok_ref[...]).astype(kc_ref.dtype)

    full = lambda a: pl.BlockSpec(a.shape, lambda i: (0, 0))
    out = pl.BlockSpec((bs, H * E), lambda i: (i, 0))
    return pl.pallas_call(
        body, grid=(S // bs,),
        in_specs=[pl.BlockSpec((3 * H, bs), lambda i: (0, i)), full(place_q), full(place_k), full(ones_q), full(ones_k)],
        out_specs=[out, out], out_shape=[jax.ShapeDtypeStruct((S, H * E), csplit.dtype)] * 2,
        name=name, compiler_params=_cparams(("parallel",)),
    )(csplit.reshape(3 * H, S), place_q, place_k, ones_q, ones_k)


def _fox_logits2(q_ref, qc_ref, k_ref, kc_ref, diag):
    q, k = q_ref[...], k_ref[...]
    qa = jnp.concatenate([q, qc_ref[...].astype(q.dtype)], axis=1)
    ka = jnp.concatenate([k, kc_ref[...].astype(k.dtype)], axis=1)
    s = lax.dot_general(qa, ka, _NT, preferred_element_type=F32) * (HEAD_DIM ** -0.5 * LOG2E)
    if diag:
        row = lax.broadcasted_iota(jnp.int32, s.shape, 0)
        col = lax.broadcasted_iota(jnp.int32, s.shape, 1)
        s = jnp.where(col <= row, s, NEG_INF)
    return s


def _fox_fwd(proj, qc, kc, H, name):
    S = proj.shape[0]
    E = HEAD_DIM
    blk = _tile(S, FOX_BLOCK)
    nq = S // blk

    def body(q_ref, qc_ref, k_ref, kc_ref, v_ref, o_ref, lse_ref, m_s, l_s, acc_s):
        qi, kj = pl.program_id(1), pl.program_id(2)

        @pl.when(kj == 0)
        def _():
            m_s[...] = jnp.full(m_s.shape, NEG_INF, F32)
            l_s[...] = jnp.zeros(l_s.shape, F32)
            acc_s[...] = jnp.zeros(acc_s.shape, F32)

        def step(diag):
            s = _fox_logits2(q_ref, qc_ref, k_ref, kc_ref, diag)
            m_prev = m_s[...]
            m_new = jnp.maximum(m_prev, jnp.max(s, axis=-1, keepdims=True))
            alpha = jnp.exp2(m_prev - m_new)
            p = jnp.exp2(s - m_new)
            l_s[...] = alpha * l_s[...] + jnp.sum(p, axis=-1, keepdims=True)
            acc_s[...] = alpha * acc_s[...] + jnp.dot(p.astype(CDT), v_ref[...], preferred_element_type=F32)
            m_s[...] = m_new

        pl.when(kj < qi)(lambda: step(False))
        pl.when(kj == qi)(lambda: step(True))

        @pl.when(kj == nq - 1)
        def _():
            o_ref[...] = acc_s[...] / l_s[...]
            lse_ref[...] = jnp.broadcast_to(m_s[...] + jnp.log2(l_s[...]), lse_ref.shape)

    qspec = lambda off: pl.BlockSpec((blk, E), lambda h, i, j: (i, off + h))
    kspec = lambda off: pl.BlockSpec((blk, E), lambda h, i, j: (jnp.minimum(j, i), off + h))
    return pl.pallas_call(
        body, grid=(H, nq, nq),
        in_specs=[qspec(0), qspec(0), kspec(H), kspec(0), kspec(2 * H)],
        out_specs=[qspec(0)] * 2,
        out_shape=[jax.ShapeDtypeStruct((S, H * E), F32)] * 2,
        scratch_shapes=[pltpu.VMEM((blk, 1), F32), pltpu.VMEM((blk, 1), F32), pltpu.VMEM((blk, E), F32)],
        name=name, compiler_params=_cparams(("parallel", "parallel", "arbitrary")),
    )(proj, qc, proj, kc, proj)


def _fox_bwd(proj, qc, kc, lse, o, do, H, name):
    S = proj.shape[0]
    E = HEAD_DIM
    blk = _tile(S, FOX_BLOCK)
    nq = S // blk
    scale = E ** -0.5

    def body(q_ref, qc_ref, k_ref, kc_ref, v_ref, lse_ref, o_ref, do_ref,
             dq_ref, dcq_ref, dk_ref, dv_ref, dck_ref, dq_s, dcq_s, dk_s, dv_s, dck_s):
        kj, qi = pl.program_id(1), pl.program_id(2)

        @pl.when(qi == 0)
        def _():
            dk_s[...] = jnp.zeros(dk_s.shape, F32)
            dv_s[...] = jnp.zeros(dv_s.shape, F32)
            dck_s[...] = jnp.zeros(dck_s.shape, F32)

        def step(diag):
            do = do_ref[...]
            doc = do.astype(CDT)
            delta = jnp.sum(do * o_ref[...], axis=-1, keepdims=True)
            p = jnp.exp2(_fox_logits2(q_ref, qc_ref, k_ref, kc_ref, diag) - lse_ref[:, 0:1])
            dp = lax.dot_general(doc, v_ref[...], _NT, preferred_element_type=F32)
            ds = p * (dp - delta)
            dss = ds * scale
            dck_s[...] += jnp.sum(ds, axis=0, keepdims=True)
            dv_s[...] += jnp.dot(p.T.astype(CDT), doc, preferred_element_type=F32)
            dk_s[...] += jnp.dot(dss.T.astype(CDT), q_ref[...], preferred_element_type=F32)
            dq_part = jnp.dot(dss.astype(CDT), k_ref[...], preferred_element_type=F32)
            dc_part = jnp.sum(ds, axis=-1, keepdims=True)
            rows = pl.ds(pl.multiple_of(qi * blk, blk), blk)

            @pl.when(kj == 0)
            def _():
                dq_s[rows, :] = dq_part
                dcq_s[rows, :] = dc_part

            @pl.when(kj > 0)
            def _():
                dq_s[rows, :] += dq_part
                dcq_s[rows, :] += dc_part

        pl.when(qi > kj)(lambda: step(False))
        pl.when(qi == kj)(lambda: step(True))

        @pl.when(qi == nq - 1)
        def _():
            dk_ref[...] = dk_s[...].astype(dk_ref.dtype)
            dv_ref[...] = dv_s[...].astype(dv_ref.dtype)
            dck_ref[...] = dck_s[...].reshape(dck_ref.shape)

        @pl.when((qi == nq - 1) & (kj == nq - 1))
        def _():
            dq_ref[...] = dq_s[...].astype(dq_ref.dtype)
            dcq_ref[...] = jnp.broadcast_to(dcq_s[...], dcq_ref.shape)

    qspec = lambda off: pl.BlockSpec((blk, E), lambda h, j, i: (jnp.maximum(i, j), off + h))
    kspec = lambda off: pl.BlockSpec((blk, E), lambda h, j, i: (j, off + h))
    head = pl.BlockSpec((S, E), lambda h, j, i: (0, h))
    return pl.pallas_call(
        body, grid=(H, nq, nq),
        in_specs=[qspec(0), qspec(0), kspec(H), kspec(0), kspec(2 * H), qspec(0), qspec(0), qspec(0)],
        out_specs=[head, head, kspec(0), kspec(0), pl.BlockSpec((1, 1, blk), lambda h, j, i: (h, 0, j))],
        out_shape=[jax.ShapeDtypeStruct((S, H * E), CDT), jax.ShapeDtypeStruct((S, H * E), F32),
                   jax.ShapeDtypeStruct((S, H * E), CDT), jax.ShapeDtypeStruct((S, H * E), CDT),
                   jax.ShapeDtypeStruct((H, 1, S), F32)],
        scratch_shapes=[pltpu.VMEM((S, E), F32), pltpu.VMEM((S, 1), F32), pltpu.VMEM((blk, E), F32),
                        pltpu.VMEM((blk, E), F32), pltpu.VMEM((1, blk), F32)],
        name=name, compiler_params=_cparams(("parallel", "arbitrary", "arbitrary")),
    )(proj, qc, proj, kc, proj, lse, o, do)


DIL_SLAB = 16 * DIL_BLOCK
DIL_UNROLL = 8


def _rel_bucket(dist):
    max_exact = REL_BUCKETS // 2
    d = jnp.maximum(dist.astype(F32), 1.0)
    large = max_exact + (jnp.log(d / max_exact) / jnp.log(jnp.float32(REL_MAX_DISTANCE / max_exact))
                         * (REL_BUCKETS - max_exact)).astype(jnp.int32)
    large = jnp.minimum(large, REL_BUCKETS - 1)
    return jnp.where(dist < max_exact, dist, large)


def _bucket_table():
    i = jnp.arange(DIL_BLOCK)[:, None]
    j = jnp.arange(2 * DIL_BLOCK)[None, :]
    rel = DIL_BLOCK + i - j
    tabs = [_rel_bucket(jnp.clip(rel, 0, w // d) * d) for w, d in DIL_PATTERNS]
    return jnp.stack(tabs).astype(jnp.int32)


def _bias_table(rel_bias, buckets, name):
    P = buckets.shape[0]
    H = rel_bias.shape[1]

    def body(rb_ref, bk_ref, out_ref):
        h = pl.program_id(1)
        bk = bk_ref[0]
        val = jnp.zeros(bk.shape, F32)
        for b in range(REL_BUCKETS):
            val = jnp.where(bk == b, rb_ref[b, h], val)
        out_ref[0, 0] = val

    return pl.pallas_call(
        body, grid=(P, H),
        in_specs=[pl.BlockSpec(memory_space=pltpu.SMEM),
                  pl.BlockSpec((1, DIL_BLOCK, 2 * DIL_BLOCK), lambda p, h: (p, 0, 0))],
        out_specs=pl.BlockSpec((1, 1, DIL_BLOCK, 2 * DIL_BLOCK), lambda p, h: (p, h, 0, 0)),
        out_shape=jax.ShapeDtypeStruct((P, H, DIL_BLOCK, 2 * DIL_BLOCK), F32),
        name=name, compiler_params=_cparams(("parallel", "parallel")),
    )(rel_bias, buckets)


def _bias_table_bwd(dbias, buckets, name):
    P, H = dbias.shape[:2]

    def body(db_ref, bk_ref, out_ref):
        lane = lax.broadcasted_iota(jnp.int32, (1, REL_BUCKETS), 1)
        acc = jnp.zeros((1, REL_BUCKETS), F32)
        bk = bk_ref[...]
        db = db_ref[:, 0]
        for b in range(REL_BUCKETS):
            tot = jnp.sum(jnp.where(bk == b, db, 0.0))
            acc = jnp.where(lane == b, tot, acc)
        out_ref[0] = acc

    return pl.pallas_call(
        body, grid=(H,),
        in_specs=[pl.BlockSpec((P, 1, DIL_BLOCK, 2 * DIL_BLOCK), lambda h: (0, h, 0, 0)),
                  pl.BlockSpec((P, DIL_BLOCK, 2 * DIL_BLOCK), lambda h: (0, 0, 0))],
        out_specs=pl.BlockSpec((1, 1, REL_BUCKETS), lambda h: (h, 0, 0)),
        out_shape=jax.ShapeDtypeStruct((H, 1, REL_BUCKETS), F32),
        name=name, compiler_params=_cparams(("parallel",)),
    )(dbias, buckets)


def _bdot(a, b, contract_b):
    return lax.dot_general(a, b, (((2,), (contract_b,)), ((0,), (0,))), preferred_element_type=F32)


def _dil_units(first, d):
    units = []
    for t in range(DIL_UNROLL):
        u = first + t
        sg = u // d
        units.append((sg, sg * (DIL_BLOCK * d) + u % d))
    return units


def _dil_rows(ref, starts, d, dtype=None):
    t = jnp.stack([ref[pl.ds(s, DIL_BLOCK, stride=d), :] for s in starts])
    return t if dtype is None else t.astype(dtype)


def _dil_keys(ref, units, d):
    B, SL = DIL_BLOCK, DIL_SLAB
    return jnp.stack([jnp.concatenate([ref[pl.ds(SL + b - B * d, B, stride=d), :], ref[pl.ds(SL + b, B, stride=d), :]],
                                      axis=0) for _, b in units]).astype(CDT)


def _dil_logits(q, keys, bias_pc, first, d, has_before):
    T, B = q.shape[0], DIL_BLOCK
    ii = lax.broadcasted_iota(jnp.int32, (T, B, 2 * B), 1)
    jj = lax.broadcasted_iota(jnp.int32, (T, B, 2 * B), 2)
    sg = (first + lax.broadcasted_iota(jnp.int32, (T, B, 2 * B), 0)) // d
    mask = (jj >= ii) & (jj <= ii + B) & ((jj >= B) | (sg > 0) | has_before)
    return jnp.where(mask, _bdot(q, keys, 2) * HEAD_DIM ** -0.5 + bias_pc[None], NEG_INF)


def _dil_specs(H):
    E, SL = DIL_BLOCK, DIL_SLAB
    cur = lambda off: pl.BlockSpec((SL, E), lambda h, g: (g, off + h))
    prev = lambda off: pl.BlockSpec((SL, E), lambda h, g: (jnp.maximum(g - 1, 0), off + h))
    bias = pl.BlockSpec((len(DIL_PATTERNS), 1, E, 2 * E), lambda h, g: (0, h, 0, 0))
    return cur, prev, bias


def _dil_fwd(proj, bias, H, name):
    S = proj.shape[0]
    E = B = DIL_BLOCK
    SL = DIL_SLAB
    P = len(DIL_PATTERNS)
    assert S % SL == 0
    n_slabs = S // SL

    def body(q_ref, kc_ref, kp_ref, vc_ref, vp_ref, b_ref, y_ref, lse_ref, kj, vj, o_s, l_s):
        g = pl.program_id(1)
        kj[0:SL, :] = kp_ref[...]
        kj[SL:2 * SL, :] = kc_ref[...]
        vj[0:SL, :] = vp_ref[...]
        vj[SL:2 * SL, :] = vc_ref[...]
        for p, (_, d) in enumerate(DIL_PATTERNS):
            def batch(it, carry, p=p, d=d):
                first = it * DIL_UNROLL
                units = _dil_units(first, d)
                q = _dil_rows(q_ref, [b for _, b in units], d, CDT)
                s = _dil_logits(q, _dil_keys(kj, units, d), b_ref[p, 0], first, d, g > 0)
                m = jnp.max(s, axis=-1, keepdims=True)
                e = jnp.exp(s - m)
                ssum = jnp.sum(e, axis=-1, keepdims=True)
                o = _bdot(e.astype(CDT), _dil_keys(vj, units, d), 1) / ssum
                lse = jnp.broadcast_to(m + jnp.log(ssum), o.shape)
                for t, (_, b) in enumerate(units):
                    o_s[p, pl.ds(b, B, stride=d), :] = o[t]
                    l_s[p, pl.ds(b, B, stride=d), :] = lse[t]
                return carry

            lax.fori_loop(0, SL // B // DIL_UNROLL, batch, 0)
        ls = [l_s[p] for p in range(P)]
        m = functools.reduce(jnp.maximum, ls)
        w = [jnp.exp(l - m) for l in ls]
        tot = functools.reduce(jnp.add, w)
        y_ref[...] = functools.reduce(jnp.add, [(w[p] / tot) * o_s[p] for p in range(P)])
        lse_ref[...] = m + jnp.log(tot)

    cur, prev, bspec = _dil_specs(H)
    return pl.pallas_call(
        body, grid=(H, n_slabs),
        in_specs=[cur(0), cur(H), prev(H), cur(2 * H), prev(2 * H), bspec],
        out_specs=[cur(0), cur(0)],
        out_shape=[jax.ShapeDtypeStruct((S, H * E), F32)] * 2,
        scratch_shapes=[pltpu.VMEM((2 * SL, E), F32), pltpu.VMEM((2 * SL, E), F32),
                        pltpu.VMEM((P, SL, E), F32), pltpu.VMEM((P, SL, E), F32)],
        name=name, compiler_params=_cparams(("parallel", "parallel")),
    )(proj, proj, proj, proj, proj, bias)


def _dil_bwd(proj, bias, y, dy, lse, H, name):
    S = proj.shape[0]
    E = B = DIL_BLOCK
    SL = DIL_SLAB
    P = len(DIL_PATTERNS)
    assert S % SL == 0
    n_slabs = S // SL
    scale = E ** -0.5

    def body(q_ref, kc_ref, kp_ref, vc_ref, vp_ref, b_ref, y_ref, dy_ref, lse_ref,
             dq_ref, dk_ref, dv_ref, db_ref, kj, vj, dq_s, dk_own, dv_own, dk_held, dv_held, dk_back, dv_back, dl_s):
        g = pl.program_id(1)

        @pl.when(g == 0)
        def _():
            db_ref[...] = jnp.zeros(db_ref.shape, F32)

        @pl.when(g > 0)
        def _():
            dk_held[...] = dk_own[...]
            dv_held[...] = dv_own[...]
            dk_back[...] = jnp.zeros(dk_back.shape, F32)
            dv_back[...] = jnp.zeros(dv_back.shape, F32)

        @pl.when(g < n_slabs)
        def _():
            kj[0:SL, :] = kp_ref[...]
            kj[SL:2 * SL, :] = kc_ref[...]
            vj[0:SL, :] = vp_ref[...]
            vj[SL:2 * SL, :] = vc_ref[...]
            dq_s[...] = jnp.zeros(dq_s.shape, F32)
            dk_own[...] = jnp.zeros(dk_own.shape, F32)
            dv_own[...] = jnp.zeros(dv_own.shape, F32)
            dl_s[...] = jnp.broadcast_to(jnp.sum(dy_ref[...] * y_ref[...], axis=-1, keepdims=True), (SL, E))
            tr = lambda t: jnp.swapaxes(t, 1, 2).astype(CDT)
            for p, (_, d) in enumerate(DIL_PATTERNS):
                def batch(it, carry, p=p, d=d):
                    first = it * DIL_UNROLL
                    units = _dil_units(first, d)
                    starts = [b for _, b in units]
                    q = _dil_rows(q_ref, starts, d, CDT)
                    dyc = _dil_rows(dy_ref, starts, d, CDT)
                    keys, vals = _dil_keys(kj, units, d), _dil_keys(vj, units, d)
                    s = _dil_logits(q, keys, b_ref[p, 0], first, d, g > 0)
                    e = jnp.exp(s - _dil_rows(lse_ref, starts, d)[:, :, 0:1])
                    ds = e * (_bdot(dyc, vals, 2) - _dil_rows(dl_s, starts, d)[:, :, 0:1])
                    dss = ds * scale
                    dq = _bdot(dss.astype(CDT), keys, 1)
                    dk = _bdot(tr(dss), q, 1)
                    dv = _bdot(tr(e), dyc, 1)
                    for t, (sg, b) in enumerate(units):
                        rows = pl.ds(b, B, stride=d)
                        dq_s[rows, :] += dq[t]
                        dk_own[rows, :] += dk[t, B:]
                        dv_own[rows, :] += dv[t, B:]

                        if B * d < SL:
                            @pl.when(sg > 0)
                            def _(t=t, b=b):
                                before = pl.ds(b - B * d, B, stride=d)
                                dk_own[before, :] += dk[t, :B]
                                dv_own[before, :] += dv[t, :B]

                        @pl.when((sg == 0) & (g > 0))
                        def _(t=t, b=b):
                            before = pl.ds(SL + b - B * d, B, stride=d)
                            dk_back[before, :] += dk[t, :B]
                            dv_back[before, :] += dv[t, :B]

                    db_ref[p, 0] += jnp.sum(ds, axis=0)
                    return carry

                lax.fori_loop(0, SL // B // DIL_UNROLL, batch, 0)
            dq_ref[...] = dq_s[...].astype(dq_ref.dtype)

        @pl.when(g > 0)
        def _():
            dk_ref[...] = (dk_held[...] + dk_back[...]).astype(dk_ref.dtype)
            dv_ref[...] = (dv_held[...] + dv_back[...]).astype(dv_ref.dtype)

    last = n_slabs - 1
    cur = lambda off: pl.BlockSpec((SL, E), lambda h, g: (jnp.minimum(g, last), off + h))
    prev = lambda off: pl.BlockSpec((SL, E), lambda h, g: (jnp.maximum(jnp.minimum(g, last) - 1, 0), off + h))
    late = pl.BlockSpec((SL, E), lambda h, g: (jnp.maximum(g - 1, 0), h))
    bspec = pl.BlockSpec((P, 1, B, 2 * B), lambda h, g: (0, h, 0, 0))
    slab = pltpu.VMEM((SL, E), F32)
    return pl.pallas_call(
        body, grid=(H, n_slabs + 1),
        in_specs=[cur(0), cur(H), prev(H), cur(2 * H), prev(2 * H), bspec, cur(0), cur(0), cur(0)],
        out_specs=[cur(0), late, late, bspec],
        out_shape=[jax.ShapeDtypeStruct((S, H * E), CDT)] * 3 + [jax.ShapeDtypeStruct((P, H, B, 2 * B), F32)],
        scratch_shapes=[pltpu.VMEM((2 * SL, E), F32), pltpu.VMEM((2 * SL, E), F32)] + [slab] * 8,
        name=name, compiler_params=_cparams(("parallel", "arbitrary")),
    )(proj, proj, proj, proj, proj, bias, y, dy, lse)


def _adamw(w, g, m, v, name, br=128):
    R, C = w.shape
    br = br if R % br == 0 else R

    def body(w_ref, g_ref, m_ref, v_ref, g_out, d_ref, nm_ref, nv_ref):
        g_ = g_ref[...]
        g_out[...] = g_
        m_ = ADAM_B1 * m_ref[...] + (1.0 - ADAM_B1) * g_
        v_ = ADAM_B2 * v_ref[...] + (1.0 - ADAM_B2) * jnp.square(g_)
        m_hat = m_ / (1.0 - ADAM_B1 ** ADAM_STEP)
        v_hat = v_ / (1.0 - ADAM_B2 ** ADAM_STEP)
        d_ref[...] = -ADAM_LR * (m_hat / (jnp.sqrt(v_hat) + ADAM_EPS) + ADAM_WD * w_ref[...])
        nm_ref[...] = m_
        nv_ref[...] = v_

    blk = pl.BlockSpec((br, C), lambda i: (i, 0))
    return pl.pallas_call(
        body, grid=(R // br,), in_specs=[blk] * 4, out_specs=[blk] * 4,
        out_shape=[jax.ShapeDtypeStruct((R, C), F32)] * 4,
        name=name, compiler_params=_cparams(("parallel",)),
    )(w, g, m, v)


_HBM = pl.BlockSpec(memory_space=pltpu.HBM)
_SEM = pl.BlockSpec(memory_space=pltpu.SEMAPHORE)
_ANY = pl.BlockSpec(memory_space=pl.ANY)
_VMEM = pl.BlockSpec(memory_space=pltpu.VMEM)
_TOKEN = jax.ShapeDtypeStruct((8, 128), F32)


def _split_params():
    return pltpu.CompilerParams(has_side_effects=pltpu.SideEffectType.DATAFLOW_SIDE_EFFECTING)


def _place():
    x, y, c = lax.axis_index("x"), lax.axis_index("y"), lax.axis_index("c")
    chips = [(1 - x, y), (x, 1 - y), (1 - x, 1 - y)]
    return x, y, c, chips


def _tie(v, tokens, name):
    flat = v.reshape(1, -1)

    def body(v_ref, *rest):
        rest[-1][...] = v_ref[...]

    return pl.pallas_call(body, in_specs=[_VMEM] + [_ANY] * len(tokens), out_specs=_VMEM,
                          out_shape=jax.ShapeDtypeStruct(flat.shape, flat.dtype), name=name,
                          compiler_params=_cparams())(flat, *tokens).reshape(v.shape)


def _row_block(R, pref=256):
    return _tile(R, pref) if R % 128 == 0 else R


def _slot():
    return 2 * lax.axis_index("x") + lax.axis_index("y")


def _cast_into_slot(w, layer, name):
    _, R, C = w.shape
    br = _row_block(R)

    def body(w_ref, out_ref):
        out_ref[...] = w_ref[...].astype(out_ref.dtype)

    return pl.pallas_call(
        body, grid=(R // br,),
        in_specs=[pl.BlockSpec((None, br, C), lambda i: (layer, i, 0))],
        out_specs=pl.BlockSpec((None, br, C), lambda i: (_slot(), i, 0)),
        out_shape=jax.ShapeDtypeStruct((N_CHIPS, R, C), CDT),
        name=name, compiler_params=_cparams(("parallel",)),
    )(w)


def _gather_copies(src_ref, dst_ref, send_sems, recv_sems, incoming):
    Rh = src_ref.shape[1] // 2
    x, y, c, chips = _place()
    slot = 2 * x + y

    def half(ref, s, hf):
        return ref.at[s, pl.ds(hf * Rh, Rh), :]

    copies = []
    for j, (cx, cy) in enumerate(chips):
        for e in range(2):
            copies.append(pltpu.make_async_remote_copy(
                src_ref=half(src_ref, slot, c), dst_ref=half(dst_ref, 2 * cx + cy, e) if incoming else half(dst_ref, slot, c),
                send_sem=send_sems.at[2 * j + e], recv_sem=recv_sems.at[2 * j + (e if incoming else c)],
                device_id=(cx, cy, e), device_id_type=MESH))
    return copies


def _gather_start(buf, after, name):
    n_after = len(after)

    def body(*refs):
        buf_ref = refs[0]
        send_sems, recv_sems, out_ref, token = refs[1 + n_after:]
        for cp in _gather_copies(buf_ref, out_ref, send_sems, recv_sems, incoming=False):
            cp.start()
        token[...] = jnp.zeros(token.shape, token.dtype)

    return pl.pallas_call(
        body, in_specs=[_HBM] + [_ANY] * n_after, out_specs=(_SEM, _SEM, _HBM, _VMEM),
        out_shape=(pltpu.SemaphoreType.DMA((6,)), pltpu.SemaphoreType.DMA((6,)), pltpu.HBM(buf.shape, buf.dtype), _TOKEN),
        input_output_aliases={0: 2}, name=name, compiler_params=_split_params(),
    )(pltpu.with_memory_space_constraint(buf, pltpu.HBM), *after)


def _gather_wait(send_sems, recv_sems, buf, after, name):
    def body(buf_ref, send_sems, recv_sems, after_ref, out_ref):
        for cp in _gather_copies(buf_ref, out_ref, send_sems, recv_sems, incoming=False):
            cp.wait_send()
        for cp in _gather_copies(buf_ref, out_ref, send_sems, recv_sems, incoming=True):
            cp.wait_recv()

    return pl.pallas_call(
        body, in_specs=[_HBM, _SEM, _SEM, _ANY], out_specs=_HBM, out_shape=pltpu.HBM(buf.shape, buf.dtype),
        input_output_aliases={0: 0}, name=name, compiler_params=_split_params(),
    )(buf, send_sems, recv_sems, after)


def _relay_copies(src_ref, dst_ref, send_sems, recv_sems, stage, incoming):
    Rh = src_ref.shape[1] // 2
    x, y, c, chips = _place()
    copies = []
    for j, (cx, cy) in enumerate(chips):
        if stage == 0:
            src_slot, src_half, peer = 2 * x + y, c, (cx, cy, c)
            dst_slot, dst_half = (2 * cx + cy, c) if incoming else (src_slot, c)
        else:
            src_slot, src_half, peer = 2 * cx + cy, c, (x, y, 1 - c)
            dst_slot, dst_half = src_slot, (1 - c if incoming else c)
        copies.append(pltpu.make_async_remote_copy(
            src_ref=src_ref.at[src_slot, pl.ds(src_half * Rh, Rh), :],
            dst_ref=dst_ref.at[dst_slot, pl.ds(dst_half * Rh, Rh), :],
            send_sem=send_sems.at[j], recv_sem=recv_sems.at[j], device_id=peer, device_id_type=MESH))
    return copies


def _relay_start(buf, after, name):
    n_after = len(after)

    def body(*refs):
        buf_ref = refs[0]
        send_sems, recv_sems, out_ref, token = refs[1 + n_after:]
        for cp in _relay_copies(buf_ref, out_ref, send_sems, recv_sems, 0, incoming=False):
            cp.start()
        token[...] = jnp.zeros(token.shape, token.dtype)

    return pl.pallas_call(
        body, in_specs=[_HBM] + [_ANY] * n_after, out_specs=(_SEM, _SEM, _HBM, _VMEM),
        out_shape=(pltpu.SemaphoreType.DMA((3,)), pltpu.SemaphoreType.DMA((3,)), pltpu.HBM(buf.shape, buf.dtype), _TOKEN),
        input_output_aliases={0: 2}, name=name, compiler_params=_split_params(),
    )(pltpu.with_memory_space_constraint(buf, pltpu.HBM), *after)


def _relay_pass(send_sems, recv_sems, buf, after, name):
    def body(buf_ref, send0, recv0, after_ref, send1, recv1, out_ref):
        for cp in _relay_copies(buf_ref, out_ref, send0, recv0, 0, incoming=False):
            cp.wait_send()
        for cp in _relay_copies(buf_ref, out_ref, send0, recv0, 0, incoming=True):
            cp.wait_recv()
        for cp in _relay_copies(out_ref, out_ref, send1, recv1, 1, incoming=False):
            cp.start()

    return pl.pallas_call(
        body, in_specs=[_HBM, _SEM, _SEM, _ANY], out_specs=(_SEM, _SEM, _HBM),
        out_shape=(pltpu.SemaphoreType.DMA((3,)), pltpu.SemaphoreType.DMA((3,)), pltpu.HBM(buf.shape, buf.dtype)),
        input_output_aliases={0: 2}, name=name, compiler_params=_split_params(),
    )(buf, send_sems, recv_sems, after)


def _relay_wait(send_sems, recv_sems, buf, name):
    def body(buf_ref, send1, recv1, out_ref):
        for cp in _relay_copies(buf_ref, out_ref, send1, recv1, 1, incoming=False):
            cp.wait_send()
        for cp in _relay_copies(buf_ref, out_ref, send1, recv1, 1, incoming=True):
            cp.wait_recv()

    return pl.pallas_call(
        body, in_specs=[_HBM, _SEM, _SEM], out_specs=_HBM, out_shape=pltpu.HBM(buf.shape, buf.dtype),
        input_output_aliases={0: 0}, name=name, compiler_params=_split_params(),
    )(buf, send_sems, recv_sems)


def _scatter_copies(g_ref, land_ref, send_sems, recv_sems, incoming):
    Rh = g_ref.shape[1] // 2
    x, y, c, _ = _place()
    me = 4 * x + 2 * y + c
    copies = []
    for k in range(1, N_DEV):
        px, py, pc = (x + (k >> 2)) % 2, (y + ((k >> 1) & 1)) % 2, (c + (k & 1)) % 2
        copies.append(pltpu.make_async_remote_copy(
            src_ref=g_ref.at[2 * px + py, pl.ds(pc * Rh, Rh), :],
            dst_ref=land_ref.at[4 * px + 2 * py + pc if incoming else me],
            send_sem=send_sems.at[k - 1], recv_sem=recv_sems.at[k - 1], device_id=(px, py, pc), device_id_type=MESH))
    return copies


def _scatter_start(g, name):
    ns, R, C = g.shape

    def body(g_ref, land_ref, send_sems, recv_sems, g_thru, land_thru, token):
        for cp in _scatter_copies(g_ref, land_thru, send_sems, recv_sems, incoming=False):
            cp.start()
        token[...] = jnp.zeros(token.shape, token.dtype)

    land = lax.empty((N_DEV, R // 2, C), g.dtype)
    n = N_DEV - 1
    return pl.pallas_call(
        body, in_specs=[_HBM, _HBM], out_specs=(_SEM, _SEM, _HBM, _HBM, _VMEM),
        out_shape=(pltpu.SemaphoreType.DMA((n,)), pltpu.SemaphoreType.DMA((n,)), pltpu.HBM(g.shape, g.dtype),
                   pltpu.HBM(land.shape, land.dtype), _TOKEN),
        input_output_aliases={0: 2, 1: 3}, name=name, compiler_params=_split_params(),
    )(pltpu.with_memory_space_constraint(g, pltpu.HBM), pltpu.with_memory_space_constraint(land, pltpu.HBM))


def _scatter_wait(send_sems, recv_sems, g, land, after, name):
    def body(g_ref, land_ref, send_sems, recv_sems, after_ref, g_out, land_out):
        for cp in _scatter_copies(g_ref, land_out, send_sems, recv_sems, incoming=False):
            cp.wait_send()
        for cp in _scatter_copies(g_ref, land_out, send_sems, recv_sems, incoming=True):
            cp.wait_recv()

    return pl.pallas_call(
        body, in_specs=[_HBM, _HBM, _SEM, _SEM, _ANY], out_specs=(_HBM, _HBM),
        out_shape=(pltpu.HBM(g.shape, g.dtype), pltpu.HBM(land.shape, land.dtype)),
        input_output_aliases={0: 0, 1: 1}, name=name, compiler_params=_split_params(),
    )(g, land, send_sems, recv_sems, after)


def _device_sum(land, g, layer, n_layers, prev, name):
    nd, Rh, C = land.shape
    br = _row_block(Rh)
    nb = Rh // br
    core = lambda: lax.axis_index("c")
    me = lambda: 2 * _slot() + core()

    def body(*refs):
        own = refs[nd][...]
        acc = None
        for d in range(nd):
            t = jnp.where(me() == d, own, refs[d][...]).astype(F32)
            acc = t if acc is None else acc + t
        refs[-1][...] = acc

    def piece(d):
        return pl.BlockSpec((None, br, C), lambda i: (jnp.where(me() == d, (d + 1) % nd, d), i, 0))

    ins = [land] * nd + [g] + ([prev] if prev is not None else [])
    return pl.pallas_call(
        body, grid=(nb,),
        in_specs=[piece(d) for d in range(nd)]
        + [pl.BlockSpec((None, br, C), lambda i: (_slot(), core() * nb + i, 0))]
        + ([_ANY] if prev is not None else []),
        out_specs=pl.BlockSpec((None, br, C), lambda i: (layer, core() * nb + i, 0)),
        out_shape=jax.ShapeDtypeStruct((n_layers, 2 * Rh, C), F32),
        input_output_aliases={nd + 1: 0} if prev is not None else {},
        name=name, compiler_params=_cparams(("parallel",)),
    )(*ins)


def _join_copy(src_ref, dst_ref, layer, send_sem, recv_sem, incoming):
    Rh = src_ref.shape[1] // 2
    x, y, c, _ = _place()
    mine, other = pl.ds(c * Rh, Rh), pl.ds((1 - c) * Rh, Rh)
    return pltpu.make_async_remote_copy(src_ref=src_ref.at[layer, mine, :],
                                        dst_ref=dst_ref.at[layer, other if incoming else mine, :],
                                        send_sem=send_sem, recv_sem=recv_sem, device_id=(x, y, 1 - c),
                                        device_id_type=MESH)


def _join_start(g, layer, name):
    def body(g_ref, send_sem, recv_sem, out_ref, token):
        _join_copy(g_ref, out_ref, layer, send_sem, recv_sem, incoming=False).start()
        token[...] = jnp.zeros(token.shape, token.dtype)

    return pl.pallas_call(
        body, in_specs=[_HBM], out_specs=(_SEM, _SEM, _HBM, _VMEM),
        out_shape=(pltpu.SemaphoreType.DMA(()), pltpu.SemaphoreType.DMA(()), pltpu.HBM(g.shape, g.dtype), _TOKEN),
        input_output_aliases={0: 2}, name=name, compiler_params=_split_params(),
    )(pltpu.with_memory_space_constraint(g, pltpu.HBM))


def _join_wait(send_sem, recv_sem, g, layer, after, name):
    def body(g_ref, send_sem, recv_sem, after_ref, out_ref):
        _join_copy(g_ref, out_ref, layer, send_sem, recv_sem, incoming=False).wait_send()
        _join_copy(g_ref, out_ref, layer, send_sem, recv_sem, incoming=True).wait_recv()

    return pl.pallas_call(
        body, in_specs=[_HBM, _SEM, _SEM, _ANY], out_specs=_HBM, out_shape=pltpu.HBM(g.shape, g.dtype),
        input_output_aliases={0: 0}, name=name, compiler_params=_split_params(),
    )(g, send_sem, recv_sem, after)


def _all_reduce_small(v, name):
    rows, cols = v.shape

    def body(v_ref, out_ref, buf, send_sems, recv_sems):
        x, y, c, _ = _place()
        me = 4 * x + 2 * y + c
        buf[me] = v_ref[...]
        peers = []
        for k in range(1, N_DEV):
            px, py, pc = (x + (k >> 2)) % 2, (y + ((k >> 1) & 1)) % 2, (c + (k & 1)) % 2
            peers.append((px, py, pc))
        sends = []
        for k, peer in enumerate(peers):
            cp = pltpu.make_async_remote_copy(src_ref=v_ref, dst_ref=buf.at[me], send_sem=send_sems.at[k],
                                              recv_sem=recv_sems.at[k], device_id=peer, device_id_type=MESH)
            cp.start()
            sends.append(cp)
        for k, (px, py, pc) in enumerate(peers):
            pltpu.make_async_remote_copy(src_ref=v_ref, dst_ref=buf.at[4 * px + 2 * py + pc], send_sem=send_sems.at[k],
                                         recv_sem=recv_sems.at[k], device_id=(px, py, pc),
                                         device_id_type=MESH).wait_recv()
        for cp in sends:
            cp.wait_send()
        acc = buf[0]
        for i in range(1, N_DEV):
            acc = acc + buf[i]
        out_ref[...] = acc

    vmem = pl.BlockSpec(memory_space=pltpu.VMEM)
    return pl.pallas_call(
        body, in_specs=[vmem], out_specs=vmem, out_shape=jax.ShapeDtypeStruct((rows, cols), F32),
        scratch_shapes=[pltpu.VMEM((N_DEV, rows, cols), F32), pltpu.SemaphoreType.DMA((N_DEV - 1,)),
                        pltpu.SemaphoreType.DMA((N_DEV - 1,))],
        name=name, compiler_params=pltpu.CompilerParams(),
    )(v)


def _reduce_scatter_sum(started, after, layer, n_layers, prev, tag):
    send_sems, recv_sems, g, land, _ = started
    g, land = _scatter_wait(send_sems, recv_sems, g, land, after, f"rs_wait_{tag}")
    f = _device_sum(land, g, layer, n_layers, prev, f"rs_sum_{tag}")
    return _join_start(f, layer, f"rs_join_start_{tag}")


def _split_w_in(wg, Hf, name):
    ns, D, cols = wg.shape
    a = 3 * Hf * HEAD_DIM
    n6 = ns * cols - Hf
    br = _row_block(D)

    def body(w_ref, w6_ref, wf_ref):
        nat = jnp.concatenate([w_ref[s] for s in range(ns)], axis=1)
        w6_ref[...] = jnp.concatenate([nat[:, :a], nat[:, a + Hf:]], axis=1)
        wf_ref[...] = nat[:, a:a + Hf]

    w6, wf = pl.pallas_call(
        body, grid=(D // br,), in_specs=[pl.BlockSpec((ns, br, cols), lambda i: (0, i, 0))],
        out_specs=[pl.BlockSpec((br, n6), lambda i: (i, 0)), pl.BlockSpec((br, Hf), lambda i: (i, 0))],
        out_shape=[jax.ShapeDtypeStruct((D, n6), wg.dtype), jax.ShapeDtypeStruct((D, Hf), wg.dtype)],
        name=name, compiler_params=_cparams(("parallel",)),
    )(wg)
    return w6, wf.T


def _join_dw_in(dw6, dwf_t, Hf, name):
    D, n6 = dw6.shape
    a = 3 * Hf * HEAD_DIM
    cols = (n6 + Hf) // N_CHIPS
    br = _row_block(D)

    def body(w6_ref, wf_ref, out_ref):
        w6 = w6_ref[...]
        nat = jnp.concatenate([w6[:, :a], wf_ref[...], w6[:, a:]], axis=1)
        for s in range(N_CHIPS):
            out_ref[s] = nat[:, s * cols:(s + 1) * cols]

    return pl.pallas_call(
        body, grid=(D // br,),
        in_specs=[pl.BlockSpec((br, n6), lambda i: (i, 0)), pl.BlockSpec((br, Hf), lambda i: (i, 0))],
        out_specs=pl.BlockSpec((N_CHIPS, br, cols), lambda i: (0, i, 0)),
        out_shape=jax.ShapeDtypeStruct((N_CHIPS, D, cols), dw6.dtype),
        name=name, compiler_params=_cparams(("parallel",)),
    )(dw6, dwf_t.T.astype(dw6.dtype))


def _tied(v, tokens, name):
    return _tie(v, tokens, name) if tokens else v


def _layer_fwd(x, p, weight, bias, tokens, tag):
    Hf, Hd = p["forget_b"].shape[0], bias.shape[1]
    h1 = _rms_fwd(x, _tied(p["norm1_g"], tokens, f"tie_norm1_{tag}"), f"norm1_{tag}")
    w6, wf_t = _split_w_in(weight("w_in", h1), Hf, f"split_w_in_{tag}")
    n_a = 3 * Hf * HEAD_DIM
    proj_a = _mm_nn(h1, w6, f"proj_a_{tag}", [CDT], epi=lambda acc: (acc,), b_cols=(0, n_a))[0]
    proj_b = _mm_nn(h1, w6, f"proj_b_{tag}", [F32], b_cols=(n_a, w6.shape[1] - n_a))[0]
    f_t = _mm_nt(wf_t, h1, f"fproj_{tag}", [F32])[0]
    qc, kc = _fox_bias_operands(_gates_fwd(f_t, p["forget_b"], f"gates_{tag}"), f"fox_operands_{tag}")
    y_a, lse_a = _fox_fwd(proj_a, qc, kc, Hf, f"fox_{tag}")
    y_b, lse_b = _dil_fwd(proj_b, bias, Hd, f"dil_{tag}")
    mixed = _pair_norm_fwd(y_a, y_b, p["outnorm_a_g"], p["outnorm_b_g"], f"norm_ab_{tag}")
    w_out = weight("w_out", mixed)
    w_out = w_out.reshape(-1, w_out.shape[2])
    x1 = _mm_nn(mixed, w_out, f"attn_out_{tag}", [F32], extras=[x])[0]
    h2 = _rms_fwd(x1, p["norm2_g"], f"norm2_{tag}")
    w_mi = weight("w_mlp_in", h2)
    act = _mm_nn(h2, w_mi, f"mlp_in_{tag}", [CDT], b_slots=True,
                 epi=lambda acc: (jnp.square(jnp.maximum(acc, 0.0)),))[0]
    w_mo = weight("w_mlp_out", act)
    w_mo = w_mo.reshape(-1, w_mo.shape[2])
    x2 = _mm_nn(act, w_mo, f"mlp_out_{tag}", [F32], extras=[x1])[0]
    saved = dict(x=x, h1=h1, proj_a=proj_a, proj_b=proj_b, f_t=f_t, qc=qc, kc=kc, y_a=y_a, lse_a=lse_a, y_b=y_b,
                 lse_b=lse_b, mixed=mixed, x1=x1, h2=h2, act=act, w6=w6, wf_t=wf_t, w_out=w_out, w_mi=w_mi,
                 w_mo=w_mo)
    return x2, saved


def _layer_bwd(dx2, dx2c, p, send, bias, sv, defer_w_out, tag):
    Hf, Hd = p["forget_b"].shape[0], bias.shape[1]
    E = HEAD_DIM
    rows = lambda g: g.reshape(N_CHIPS, -1, g.shape[1])
    du = _mm_nt(dx2c, sv["w_mo"], f"d_act_{tag}", [CDT], extras=[sv["act"]],
                epi=lambda acc, a: (acc * (2.0 * jnp.sqrt(a.astype(F32))),))[0]
    tokens = send("w_mlp_out", rows(_mm_tn(sv["act"], dx2c, f"dw_mlp_out_{tag}", CDT)))
    dh2 = _mm_nt(du, sv["w_mi"], f"d_h2_{tag}", [F32], b_slots=True)[0]
    tokens = tokens + send("w_mlp_in", _mm_tn(sv["h2"], du, f"dw_mlp_in_{tag}", CDT, out_slots=N_CHIPS))
    dx1, dx1c, g_norm2 = _rms_bwd(sv["x1"], _tied(p["norm2_g"], tokens, f"tie_norm2_{tag}"), dh2, dx2,
                                  f"d_norm2_{tag}")
    dmixed = _mm_nt(dx1c, sv["w_out"], f"d_mixed_{tag}", [F32])[0]
    send_w_out = lambda: send("w_out", rows(_mm_tn(sv["mixed"], dx1c, f"dw_out_{tag}", CDT)))
    tokens = [] if defer_w_out else send_w_out()
    dy_a, dy_b, g_na, g_nb = _pair_norm_bwd(sv["y_a"], sv["y_b"], _tied(p["outnorm_a_g"], tokens, f"tie_norm_a_{tag}"),
                                            p["outnorm_b_g"], dmixed, f"d_norm_ab_{tag}")
    dq_a, dcq, dk_a, dv_a, dck = _fox_bwd(sv["proj_a"], sv["qc"], sv["kc"], sv["lse_a"], sv["y_a"], dy_a, Hf,
                                          f"fox_bwd_{tag}")
    df, dfc, g_fb = _gates_bwd(sv["f_t"], p["forget_b"], dcq[:, ::E].T, dck.reshape(Hf, -1), f"d_gates_{tag}")
    dq_b, dk_b, dv_b, dbias = _dil_bwd(sv["proj_b"], bias, sv["y_b"], dy_b, sv["lse_b"], Hd, f"dil_bwd_{tag}")
    dproj = jnp.concatenate([dq_a, dk_a, dv_a, dq_b, dk_b, dv_b], axis=1)
    g_w6 = _mm_tn(sv["h1"], dproj, f"dw_in_{tag}", CDT)
    g_wf_t = _mm_nn(dfc, sv["h1"], f"dw_f_{tag}", [F32])[0]
    tokens = send("w_in", _join_dw_in(g_w6, g_wf_t, Hf, f"join_dw_in_{tag}"))
    dh1_f = _mm_tn(dfc, _tied(sv["wf_t"], tokens, f"tie_wf_{tag}"), f"d_h1_f_{tag}", F32)
    dh1 = _mm_nt(dproj, sv["w6"], f"d_h1_{tag}", [F32], extras=[dh1_f])[0]
    dx, dxc, g_norm1 = _rms_bwd(sv["x"], p["norm1_g"], dh1, dx1, f"d_norm1_{tag}")
    grads = dict(norm1_g=g_norm1[0], norm2_g=g_norm2[0], outnorm_a_g=g_na[0], outnorm_b_g=g_nb[0],
                 forget_b=g_fb[:, 0], dbias=dbias)
    return dx, dxc, grads, (send_w_out if defer_w_out else None)


_LAYER_SMALL = ("norm1_g", "forget_b", "outnorm_a_g", "outnorm_b_g", "norm2_g")


def _local_step(x, target, small, weight, send, tokens):
    depth = small["norm1_g"].shape[0]
    buckets = _bucket_table()
    bias = _bias_table(small["rel_bias"], buckets, "bias_table")
    layers, saved = [], []
    for l in range(depth):
        p = {k: small[k][l] for k in _LAYER_SMALL}
        layers.append(p)
        x, sv = _layer_fwd(x, p, functools.partial(weight, l), bias, tokens if l == 0 else [], f"l{l}")
        saved.append(sv)
    dx, dxc, g_final, loss = _loss_bwd(x, small["final_norm_g"], target, "loss")
    layer_grads = [None] * depth
    for l in reversed(range(depth)):
        dx, dxc, layer_grads[l], last = _layer_bwd(dx, dxc, layers[l], functools.partial(send, l), bias, saved[l],
                                                   l == 0, f"l{l}")
    tokens = last()
    dbias = functools.reduce(jnp.add, [g["dbias"] for g in layer_grads])
    g_rel = _bias_table_bwd(dbias, buckets, "d_bias_table")[:, 0, :].T
    small_grads = dict(final_norm_g=g_final[0], rel_bias=g_rel,
                       **{k: jnp.stack([g[k] for g in layer_grads]) for k in _LAYER_SMALL})
    return loss[0, 0], dx, small_grads, tokens


_BIG = ("w_in", "w_out", "w_mlp_in", "w_mlp_out")
_SMALL = ("norm1_g", "forget_b", "rel_bias", "outnorm_a_g", "outnorm_b_g", "norm2_g", "final_norm_g")
_ORDER = ("norm1_g", "w_in", "forget_b", "rel_bias", "outnorm_a_g", "outnorm_b_g", "w_out", "norm2_g", "w_mlp_in",
          "w_mlp_out", "final_norm_g")


def _pack_small(d):
    flat = jnp.concatenate([d[k].reshape(-1) for k in _SMALL])
    rows = -(-flat.shape[0] // (8 * SMALL_COLS)) * 8
    return jnp.pad(flat, (0, rows * SMALL_COLS - flat.shape[0])).reshape(rows, SMALL_COLS)


def _unpack_small(packed, like):
    flat, out, at = packed.reshape(-1), {}, 0
    for k in _SMALL:
        n = like[k].size
        out[k] = flat[at:at + n].reshape(like[k].shape)
        at += n
    return out


def kernel(x, norm1_g, w_in, forget_b, rel_bias, outnorm_a_g, outnorm_b_g, w_out, norm2_g, w_mlp_in, w_mlp_out, final_norm_g, loss_target, m_norm1_g, m_w_in, m_forget_b, m_rel_bias, m_outnorm_a_g, m_outnorm_b_g, m_w_out, m_norm2_g, m_w_mlp_in, m_w_mlp_out, m_final_norm_g, v_norm1_g, v_w_in, v_forget_b, v_rel_bias, v_outnorm_a_g, v_outnorm_b_g, v_w_out, v_norm2_g, v_w_mlp_in, v_w_mlp_out, v_final_norm_g):
    w = dict(norm1_g=norm1_g, w_in=w_in, forget_b=forget_b, rel_bias=rel_bias, outnorm_a_g=outnorm_a_g,
             outnorm_b_g=outnorm_b_g, w_out=w_out, norm2_g=norm2_g, w_mlp_in=w_mlp_in, w_mlp_out=w_mlp_out,
             final_norm_g=final_norm_g)
    m = dict(norm1_g=m_norm1_g, w_in=m_w_in, forget_b=m_forget_b, rel_bias=m_rel_bias, outnorm_a_g=m_outnorm_a_g,
             outnorm_b_g=m_outnorm_b_g, w_out=m_w_out, norm2_g=m_norm2_g, w_mlp_in=m_w_mlp_in,
             w_mlp_out=m_w_mlp_out, final_norm_g=m_final_norm_g)
    v = dict(norm1_g=v_norm1_g, w_in=v_w_in, forget_b=v_forget_b, rel_bias=v_rel_bias, outnorm_a_g=v_outnorm_a_g,
             outnorm_b_g=v_outnorm_b_g, w_out=v_w_out, norm2_g=v_norm2_g, w_mlp_in=v_w_mlp_in,
             w_mlp_out=v_w_mlp_out, final_norm_g=v_final_norm_g)
    depth = w_in.shape[0]
    small = {k: w[k] for k in _SMALL}

    gathers, tokens = {}, []
    for l in range(depth):
        for k in _BIG:
            buf = _cast_into_slot(w[k], l, f"cast_{k}_l{l}")
            start = _gather_start if gathers else _relay_start
            send_sems, recv_sems, buf, token = start(buf, tokens, f"gather_start_{k}_l{l}")
            gathers[l, k], tokens = (send_sems, recv_sems, buf), [token]
    first = next(iter(gathers))

    def weight(l, k, after):
        if (l, k) == first:
            return _relay_wait(*_relay_pass(*gathers[l, k], after, f"gather_pass_{k}_l{l}"), f"gather_wait_{k}_l{l}")
        return _gather_wait(*gathers[l, k], after, f"gather_wait_{k}_l{l}")

    scatters = {}

    def send(l, k, g):
        scatters[l, k] = _scatter_start(g, f"rs_start_{k}_l{l}")
        return [scatters[l, k][4]]

    loss, grad_x, small_grads, tokens = _local_step(x[0], loss_target[0], small, weight, send, tokens)
    loss = lax.psum(loss, ("x", "y", "c"))

    grads, delta, new_m, new_v = {}, {}, {}, {}
    packed = _tied(_pack_small(small_grads), tokens, "tie_small")
    after, seen, joining = packed, {k: 0 for k in _BIG}, None

    def joined(after):
        (l, k), (send_sem, recv_sem, g) = joining
        grads[k] = _join_wait(send_sem, recv_sem, g, l, after, f"rs_join_wait_{k}_l{l}")
        seen[k] += 1
        if seen[k] < depth:
            return after
        shape = w[k].shape
        flat = lambda t: t.reshape(-1, shape[-1])
        g_, d_, m_, v_ = _adamw(flat(w[k]), flat(grads[k]), flat(m[k]), flat(v[k]), f"adamw_{k}")
        grads[k], delta[k], new_m[k], new_v[k] = (t.reshape(shape) for t in (g_, d_, m_, v_))
        return d_

    for (l, k), started in scatters.items():
        assert joining is None or joining[0][1] != k
        send_sem, recv_sem, g, token = _reduce_scatter_sum(started, after, l, depth, grads.get(k), f"{k}_l{l}")
        if joining is not None:
            after = joined(token)
        joining = ((l, k), (send_sem, recv_sem, g))
    after = joined(after)
    small_sums = _all_reduce_small(_tied(packed, [after], "tie_small_sums"), "small_all_reduce")
    grads.update(_unpack_small(small_sums, small))
    _, d_, m_, v_ = _adamw(_pack_small(small), _pack_small({k: grads[k] for k in _SMALL}),
                           _pack_small({k: m[k] for k in _SMALL}), _pack_small({k: v[k] for k in _SMALL}), "adamw_small")
    delta.update(_unpack_small(d_, small))
    new_m.update(_unpack_small(m_, small))
    new_v.update(_unpack_small(v_, small))

    return (loss, grad_x[None], *[grads[k] for k in _ORDER], *[delta[k] for k in _ORDER],
            *[new_m[k] for k in _ORDER], *[new_v[k] for k in _ORDER])
```

```python
import functools

import jax
import jax.numpy as jnp
from jax import lax
from jax.experimental import pallas as pl
from jax.experimental.pallas import tpu as pltpu

F32 = jnp.float32
CDT = jnp.bfloat16
HEAD_DIM = 128
NORM_EPS = 1e-6
NEG_INF = -1e30
LOG2E = 1.4426950408889634
REL_BUCKETS = 32
REL_MAX_DISTANCE = 2048
DIL_PATTERNS = ((128, 1), (512, 4), (2048, 16))
DIL_BLOCK = 128
ADAM_LR, ADAM_B1, ADAM_B2, ADAM_EPS, ADAM_WD, ADAM_STEP = 0.001, 0.9, 0.999, 1e-08, 0.01, 10
N_CHIPS = 4
N_DEV = 8
VMEM_LIMIT_BYTES = 56 * 1024 * 1024
SMALL_COLS = 1024
MESH = pl.DeviceIdType.MESH


def _cparams(sem=None):
    return pltpu.CompilerParams(dimension_semantics=sem, vmem_limit_bytes=VMEM_LIMIT_BYTES)


def _tile(dim, pref):
    t = min(pref, dim)
    t -= t % 128
    while t >= 128:
        if dim % t == 0:
            return t
        t -= 128
    return dim


def _rowwise(fn, ins, out_dtypes, name, bs=256, consts=()):
    R, C = ins[0].shape
    bs = min(bs, R)
    n_in, n_c = len(ins), len(consts)

    def body(*refs):
        vals = [r[...] for r in refs[:n_in + n_c]]
        res = fn(*vals)
        for o, r in zip(refs[n_in + n_c:], res):
            o[...] = r.astype(o.dtype)

    row = pl.BlockSpec((bs, C), lambda i: (i, 0))
    return pl.pallas_call(
        body, grid=(R // bs,),
        in_specs=[row] * n_in + [pl.BlockSpec((1, c.shape[-1]), lambda i: (0, 0)) for c in consts],
        out_specs=[row] * len(out_dtypes),
        out_shape=[jax.ShapeDtypeStruct((R, C), d) for d in out_dtypes],
        name=name, compiler_params=_cparams(("parallel",)),
    )(*ins, *[c.reshape(1, -1) for c in consts])


def _rms_fwd(x, g, name):
    def fn(xf, gg):
        r = lax.rsqrt(jnp.mean(xf * xf, axis=-1, keepdims=True) + NORM_EPS)
        return ((xf * r) * gg,)
    return _rowwise(fn, [x], [CDT], name, consts=[g])[0]


def _rms_bwd(x, g, dh, dres, name, bs=256):
    S, D = x.shape
    bs = min(bs, S)
    has_res = dres is not None

    def body(*refs):
        x_ref, g_ref, dh_ref = refs[:3]
        dx_ref, dxc_ref, dg_ref = refs[-3:]
        xf = x_ref[...]
        r = lax.rsqrt(jnp.mean(xf * xf, axis=-1, keepdims=True) + NORM_EPS)
        xhat = xf * r
        dh_ = dh_ref[...].astype(F32)
        dxhat = dh_ * g_ref[...]
        dx = r * (dxhat - xhat * jnp.mean(dxhat * xhat, axis=-1, keepdims=True))
        if has_res:
            dx = dx + refs[3][...]
        dx_ref[...] = dx
        dxc_ref[...] = dx.astype(dxc_ref.dtype)
        part = jnp.sum(dh_ * xhat, axis=0, keepdims=True)

        @pl.when(pl.program_id(0) == 0)
        def _():
            dg_ref[...] = part

        @pl.when(pl.program_id(0) > 0)
        def _():
            dg_ref[...] += part

    row = pl.BlockSpec((bs, D), lambda i: (i, 0))
    one = pl.BlockSpec((1, D), lambda i: (0, 0))
    ins = [x, g.reshape(1, D), dh] + ([dres] if has_res else [])
    return pl.pallas_call(
        body, grid=(S // bs,),
        in_specs=[row, one, row] + ([row] if has_res else []),
        out_specs=[row, row, one],
        out_shape=[jax.ShapeDtypeStruct((S, D), F32), jax.ShapeDtypeStruct((S, D), CDT),
                   jax.ShapeDtypeStruct((1, D), F32)],
        name=name, compiler_params=_cparams(("arbitrary",)),
    )(*ins)


def _pair_norm_fwd(y_a, y_b, g_a, g_b, name, bs=256):
    S, Da = y_a.shape
    Db = y_b.shape[1]
    bs = min(bs, S)

    def body(a_ref, b_ref, ga_ref, gb_ref, o_ref):
        def norm(x, g):
            r = lax.rsqrt(jnp.mean(x * x, axis=-1, keepdims=True) + NORM_EPS)
            return ((x * r) * g).astype(o_ref.dtype)
        o_ref[:, :Da] = norm(a_ref[...], ga_ref[...])
        o_ref[:, Da:] = norm(b_ref[...], gb_ref[...])

    row = lambda n: pl.BlockSpec((bs, n), lambda i: (i, 0))
    one = lambda n: pl.BlockSpec((1, n), lambda i: (0, 0))
    return pl.pallas_call(
        body, grid=(S // bs,), in_specs=[row(Da), row(Db), one(Da), one(Db)], out_specs=row(Da + Db),
        out_shape=jax.ShapeDtypeStruct((S, Da + Db), CDT), name=name, compiler_params=_cparams(("parallel",)),
    )(y_a, y_b, g_a.reshape(1, Da), g_b.reshape(1, Db))


def _pair_norm_bwd(y_a, y_b, g_a, g_b, dmixed, name, bs=256):
    S, Da = y_a.shape
    Db = y_b.shape[1]
    bs = min(bs, S)

    def body(a_ref, b_ref, ga_ref, gb_ref, dm_ref, da_ref, db_ref, dga_ref, dgb_ref):
        def one(x_ref, g_ref, dh, dx_ref, dg_ref):
            xf = x_ref[...]
            r = lax.rsqrt(jnp.mean(xf * xf, axis=-1, keepdims=True) + NORM_EPS)
            xhat = xf * r
            dxhat = dh * g_ref[...]
            dx_ref[...] = r * (dxhat - xhat * jnp.mean(dxhat * xhat, axis=-1, keepdims=True))
            part = jnp.sum(dh * xhat, axis=0, keepdims=True)

            @pl.when(pl.program_id(0) == 0)
            def _():
                dg_ref[...] = part

            @pl.when(pl.program_id(0) > 0)
            def _():
                dg_ref[...] += part

        dm = dm_ref[...]
        one(a_ref, ga_ref, dm[:, :Da], da_ref, dga_ref)
        one(b_ref, gb_ref, dm[:, Da:], db_ref, dgb_ref)

    row = lambda n: pl.BlockSpec((bs, n), lambda i: (i, 0))
    one_ = lambda n: pl.BlockSpec((1, n), lambda i: (0, 0))
    return pl.pallas_call(
        body, grid=(S // bs,), in_specs=[row(Da), row(Db), one_(Da), one_(Db), row(Da + Db)],
        out_specs=[row(Da), row(Db), one_(Da), one_(Db)],
        out_shape=[jax.ShapeDtypeStruct((S, Da), F32), jax.ShapeDtypeStruct((S, Db), F32),
                   jax.ShapeDtypeStruct((1, Da), F32), jax.ShapeDtypeStruct((1, Db), F32)],
        name=name, compiler_params=_cparams(("arbitrary",)),
    )(y_a, y_b, g_a.reshape(1, Da), g_b.reshape(1, Db), dmixed)


def _loss_bwd(x, g, target, name, bs=256):
    S, D = x.shape
    bs = min(bs, S)

    def body(x_ref, g_ref, t_ref, dx_ref, dxc_ref, dg_ref, loss_ref):
        xf = x_ref[...]
        r = lax.rsqrt(jnp.mean(xf * xf, axis=-1, keepdims=True) + NORM_EPS)
        xhat = xf * r
        err = xhat * g_ref[...] - t_ref[...]
        lpart = 0.5 * jnp.sum(jnp.mean(err * err, axis=-1, keepdims=True), axis=0, keepdims=True)
        dy = err / D
        dxhat = dy * g_ref[...]
        dx = r * (dxhat - xhat * jnp.mean(dxhat * xhat, axis=-1, keepdims=True))
        dx_ref[...] = dx
        dxc_ref[...] = dx.astype(dxc_ref.dtype)
        gpart = jnp.sum(dy * xhat, axis=0, keepdims=True)

        @pl.when(pl.program_id(0) == 0)
        def _():
            dg_ref[...] = gpart
            loss_ref[...] = lpart

        @pl.when(pl.program_id(0) > 0)
        def _():
            dg_ref[...] += gpart
            loss_ref[...] += lpart

    row = pl.BlockSpec((bs, D), lambda i: (i, 0))
    one = pl.BlockSpec((1, D), lambda i: (0, 0))
    return pl.pallas_call(
        body, grid=(S // bs,),
        in_specs=[row, one, row],
        out_specs=[row, row, one, pl.BlockSpec((1, 1), lambda i: (0, 0))],
        out_shape=[jax.ShapeDtypeStruct((S, D), F32), jax.ShapeDtypeStruct((S, D), CDT),
                   jax.ShapeDtypeStruct((1, D), F32), jax.ShapeDtypeStruct((1, 1), F32)],
        name=name, compiler_params=_cparams(("arbitrary",)),
    )(x, g.reshape(1, D), target)


_NN = (((1,), (0,)), ((), ()))
_NT = (((1,), (1,)), ((), ()))
_TN = (((0,), (0,)), ((), ()))


def _mm(a, b, *, M, N, K, a_spec, b_spec, o_spec, dims, tm, tn, tk, name, out_shapes, extras=(), epi=None):
    nk = K // tk
    n_ex, n_out = len(extras), len(out_shapes)
    in_place = epi is None
    if in_place:
        assert n_out == 1 and n_ex <= 1 and out_shapes[0].dtype == F32
        epi = lambda acc, *r: (acc + r[0] if r else acc,)

    def body(*refs):
        a_ref, b_ref = refs[0], refs[1]
        ex = refs[2:2 + n_ex]
        outs = refs[2 + n_ex:2 + n_ex + n_out]
        part = lax.dot_general(a_ref[...], b_ref[...], dims, preferred_element_type=F32)

        def finish(acc):
            for o, r in zip(outs, epi(acc, *[e[...] for e in ex])):
                o[...] = r.astype(o.dtype)

        if nk == 1:
            finish(part)
        elif in_place:
            k = pl.program_id(2)

            @pl.when(k == 0)
            def _():
                finish(part)

            @pl.when(k > 0)
            def _():
                outs[0][...] += part
        else:
            acc_ref = refs[-1]
            k = pl.program_id(2)

            @pl.when(k == 0)
            def _():
                acc_ref[...] = part

            @pl.when(k > 0)
            def _():
                acc_ref[...] += part

            @pl.when(k == nk - 1)
            def _():
                finish(acc_ref[...])

    ex_spec = pl.BlockSpec((tm, tn), lambda i, j, k: (i, j))
    return pl.pallas_call(
        body, grid=(M // tm, N // tn, nk),
        in_specs=[a_spec, b_spec] + [ex_spec] * n_ex,
        out_specs=[o_spec] * n_out,
        out_shape=out_shapes,
        scratch_shapes=[pltpu.VMEM((tm, tn), F32)] if nk > 1 and not in_place else [],
        name=name, compiler_params=_cparams(("parallel", "parallel", "arbitrary")),
    )(a, b, *extras)


def _mm_tiles(K):
    return (2048, 512, 2048) if K <= 2048 else (1024, 1024, 2048)


def _mm_nn(a, b, name, out_dtypes, extras=(), epi=None, b_slots=False, b_cols=None):
    M, K = a.shape
    tm, tn, tk = _mm_tiles(K)
    if b_slots:
        ns, _, Ns = b.shape
        N = ns * Ns
        tn = _tile(Ns, tn)
        npb = Ns // tn
        tk_ = _tile(K, tk)
        b_spec = pl.BlockSpec((None, tk_, tn), lambda i, j, k: (j // npb, k, j % npb))
    else:
        first, N = b_cols if b_cols is not None else (0, b.shape[1])
        tn = _tile(N, tn)
        assert first % tn == 0
        tk_ = _tile(K, tk)
        b_spec = pl.BlockSpec((tk_, tn), lambda i, j, k: (k, first // tn + j))
    tm = _tile(M, tm)
    return _mm(a, b, M=M, N=N, K=K, a_spec=pl.BlockSpec((tm, tk_), lambda i, j, k: (i, k)), b_spec=b_spec,
               o_spec=pl.BlockSpec((tm, tn), lambda i, j, k: (i, j)), dims=_NN, tm=tm, tn=tn, tk=tk_, name=name,
               out_shapes=[jax.ShapeDtypeStruct((M, N), d) for d in out_dtypes], extras=extras, epi=epi)


def _mm_nt(a, b, name, out_dtypes, extras=(), epi=None, b_slots=False):
    M, K = a.shape
    tm, tn, tk = _mm_tiles(K)
    tm = _tile(M, tm)
    if b_slots:
        ns, N, Ks = b.shape
        tk_ = _tile(Ks, tk)
        kpb = Ks // tk_
        tn = _tile(N, tn)
        b_spec = pl.BlockSpec((None, tn, tk_), lambda i, j, k: (k // kpb, j, k % kpb))
    else:
        N = b.shape[0]
        tk_ = _tile(K, tk)
        tn = _tile(N, tn)
        b_spec = pl.BlockSpec((tn, tk_), lambda i, j, k: (j, k))
    return _mm(a, b, M=M, N=N, K=K, a_spec=pl.BlockSpec((tm, tk_), lambda i, j, k: (i, k)), b_spec=b_spec,
               o_spec=pl.BlockSpec((tm, tn), lambda i, j, k: (i, j)), dims=_NT, tm=tm, tn=tn, tk=tk_, name=name,
               out_shapes=[jax.ShapeDtypeStruct((M, N), d) for d in out_dtypes], extras=extras, epi=epi)


def _mm_tn(a, b, name, out_dtype, out_slots=0, tm=2048, tn=1024, tk=2048):
    K, M = a.shape
    N = b.shape[1]
    tm, tk_ = _tile(M, tm), _tile(K, tk)
    if out_slots:
        Ns = N // out_slots
        tn = _tile(Ns, tn)
        npb = Ns // tn
        o_spec = pl.BlockSpec((None, tm, tn), lambda i, j, k: (j // npb, i, j % npb))
        out_shape = jax.ShapeDtypeStruct((out_slots, M, Ns), out_dtype)
    else:
        tn = _tile(N, tn)
        o_spec = pl.BlockSpec((tm, tn), lambda i, j, k: (i, j))
        out_shape = jax.ShapeDtypeStruct((M, N), out_dtype)
    return _mm(a, b, M=M, N=N, K=K, a_spec=pl.BlockSpec((tk_, tm), lambda i, j, k: (k, i)),
               b_spec=pl.BlockSpec((tk_, tn), lambda i, j, k: (k, j)), o_spec=o_spec, dims=_TN,
               tm=tm, tn=tn, tk=tk_, name=name, out_shapes=[out_shape],
               epi=None if out_dtype == F32 else (lambda acc: (acc,)))[0]


GATE_BLOCK = 512


def _split3(v):
    hi = v.astype(jnp.bfloat16)
    r1 = v - hi.astype(F32)
    mid = r1.astype(jnp.bfloat16)
    lo = (r1 - mid.astype(F32)).astype(jnp.bfloat16)
    return hi, mid, lo


def _exact_dot(v, tri):
    return functools.reduce(jnp.add, [jnp.dot(t, tri, preferred_element_type=F32) for t in _split3(v)])


def _gates_fwd(f_t, b, name):
    H, S = f_t.shape
    nb = _tile(S, GATE_BLOCK)
    inv_scale = HEAD_DIM ** 0.5

    def body(f_ref, b_ref, c_ref):
        upper = (lax.broadcasted_iota(jnp.int32, (nb, nb), 0)
                 <= lax.broadcasted_iota(jnp.int32, (nb, nb), 1)).astype(jnp.bfloat16)
        carry = jnp.zeros((H, 1), F32)
        for i in range(S // nb):
            z = f_ref[:, i * nb:(i + 1) * nb] + b_ref[...]
            logf = jnp.minimum(z, 0.0) - jnp.log1p(jnp.exp(-jnp.abs(z)))
            cs = _exact_dot(logf, upper) + carry
            for j, t in enumerate(_split3(cs * inv_scale)):
                c_ref[j, :, i * nb:(i + 1) * nb] = t
            carry = cs[:, nb - 1:nb]

    return pl.pallas_call(body, out_shape=jax.ShapeDtypeStruct((3, H, S), jnp.bfloat16), name=name,
                          compiler_params=_cparams())(f_t, b.reshape(H, 1))


def _gates_bwd(f_t, b, dcq, dck, name):
    H, S = f_t.shape
    nb = _tile(S, GATE_BLOCK)

    def body(f_ref, b_ref, dcq_ref, dck_ref, df_ref, dfc_ref, db_ref):
        lower = (lax.broadcasted_iota(jnp.int32, (nb, nb), 0)
                 >= lax.broadcasted_iota(jnp.int32, (nb, nb), 1)).astype(jnp.bfloat16)
        carry = jnp.zeros((H, 1), F32)
        db = jnp.zeros((H, 1), F32)
        for i in reversed(range(S // nb)):
            sl = slice(i * nb, (i + 1) * nb)
            dc = dcq_ref[:, sl] - dck_ref[:, sl]
            dlogf = _exact_dot(dc, lower) + carry
            carry = dlogf[:, 0:1]
            z = f_ref[:, sl] + b_ref[...]
            df = dlogf / (1.0 + jnp.exp(z))
            df_ref[:, sl] = df
            dfc_ref[:, sl] = df.astype(dfc_ref.dtype)
            db = db + jnp.sum(df, axis=1, keepdims=True)
        db_ref[...] = db

    return pl.pallas_call(
        body, out_shape=[jax.ShapeDtypeStruct((H, S), F32), jax.ShapeDtypeStruct((H, S), CDT),
                         jax.ShapeDtypeStruct((H, 1), F32)],
        name=name, compiler_params=_cparams())(f_t, b.reshape(H, 1), dcq, dck)


FOX_BLOCK = 1024


def _fox_bias_operands(csplit, name, bs=512):
    _, H, S = csplit.shape
    E = HEAD_DIM
    bs = _tile(S, bs)
    part = jnp.arange(3 * H)[:, None] // H
    head = jnp.arange(3 * H)[:, None] % H
    lane = jnp.arange(H * E)[None, :]
    place_q = (lane == head * E + part).astype(csplit.dtype)
    place_k = -(lane == head * E + 3 + part).astype(csplit.dtype)
    ones_q = ((lane % E >= 3) & (lane % E < 6)).astype(F32)
    ones_k = (lane % E < 3).astype(F32)

    def body(c_ref, pq_ref, pk_ref, oq_ref, ok_ref, qc_ref, kc_ref):
        c = c_ref[...]
        qc_ref[...] = (lax.dot_general(c, pq_ref[...], _TN, preferred_element_type=F32) + oq_ref[...]).astype(qc_ref.dtype)
        kc_ref[...] = (lax.dot_general(c, pk_ref[...], _TN, preferred_element_type=F32) + ok_ref[...]).astype(kc_ref.dtype)

    full = lambda a: pl.BlockSpec(a.shape, lambda i: (0, 0))
    out = pl.BlockSpec((bs, H * E), lambda i: (i, 0))
    return pl.pallas_call(
        body, grid=(S // bs,),
        in_specs=[pl.BlockSpec((3 * H, bs), lambda i: (0, i)), full(place_q), full(place_k), full(ones_q), full(ones_k)],
        out_specs=[out, out], out_shape=[jax.ShapeDtypeStruct((S, H * E), csplit.dtype)] * 2,
        name=name, compiler_params=_cparams(("parallel",)),
    )(csplit.reshape(3 * H, S), place_q, place_k, ones_q, ones_k)


def _fox_logits2(q_ref, qc_ref, k_ref, kc_ref, diag):
    q, k = q_ref[...], k_ref[...]
    qa = jnp.concatenate([q, qc_ref[...].astype(q.dtype)], axis=1)
    ka = jnp.concatenate([k, kc_ref[...].astype(k.dtype)], axis=1)
    s = lax.dot_general(qa, ka, _NT, preferred_element_type=F32) * (HEAD_DIM ** -0.5 * LOG2E)
    if diag:
        row = lax.broadcasted_iota(jnp.int32, s.shape, 0)
        col = lax.broadcasted_iota(jnp.int32, s.shape, 1)
        s = jnp.where(col <= row, s, NEG_INF)
    return s


def _fox_fwd(proj, qc, kc, H, name):
    S = proj.shape[0]
    E = HEAD_DIM
    blk = _tile(S, FOX_BLOCK)
    nq = S // blk

    def pair(t):
        qi = sum((t >= i * (i + 1) // 2).astype(jnp.int32) for i in range(1, nq)) if nq > 1 else 0 * t
        return qi, t - qi * (qi + 1) // 2

    def body(q_ref, qc_ref, k_ref, kc_ref, v_ref, o_ref, lse_ref, m_s, l_s, acc_s):
        qi, kj = pair(pl.program_id(1))

        @pl.when(kj == 0)
        def _():
            m_s[...] = jnp.full(m_s.shape, NEG_INF, F32)
            l_s[...] = jnp.zeros(l_s.shape, F32)
            acc_s[...] = jnp.zeros(acc_s.shape, F32)

        def step(diag):
            s = _fox_logits2(q_ref, qc_ref, k_ref, kc_ref, diag)
            m_prev = m_s[...]
            m_new = jnp.maximum(m_prev, jnp.max(s, axis=-1, keepdims=True))
            alpha = jnp.exp2(m_prev - m_new)
            p = jnp.exp2(s - m_new)
            l_s[...] = alpha * l_s[...] + jnp.sum(p, axis=-1, keepdims=True)
            acc_s[...] = alpha * acc_s[...] + jnp.dot(p.astype(CDT), v_ref[...], preferred_element_type=F32)
            m_s[...] = m_new

        pl.when(kj < qi)(lambda: step(False))
        pl.when(kj == qi)(lambda: step(True))

        @pl.when(kj == qi)
        def _():
            o_ref[...] = acc_s[...] / l_s[...]
            lse_ref[...] = jnp.broadcast_to(m_s[...] + jnp.log2(l_s[...]), lse_ref.shape)

    qspec = lambda off: pl.BlockSpec((blk, E), lambda h, t: (pair(t)[0], off + h))
    kspec = lambda off: pl.BlockSpec((blk, E), lambda h, t: (pair(t)[1], off + h))
    return pl.pallas_call(
        body, grid=(H, nq * (nq + 1) // 2),
        in_specs=[qspec(0), qspec(0), kspec(H), kspec(0), kspec(2 * H)],
        out_specs=[qspec(0)] * 2,
        out_shape=[jax.ShapeDtypeStruct((S, H * E), F32)] * 2,
        scratch_shapes=[pltpu.VMEM((blk, 1), F32), pltpu.VMEM((blk, 1), F32), pltpu.VMEM((blk, E), F32)],
        name=name, compiler_params=_cparams(("parallel", "arbitrary")),
    )(proj, qc, proj, kc, proj)


def _fox_bwd(proj, qc, kc, lse, o, do, H, name):
    S = proj.shape[0]
    E = HEAD_DIM
    blk = _tile(S, FOX_BLOCK)
    nq = S // blk
    scale = E ** -0.5

    def pair(t):
        first = lambda j: j * nq - j * (j - 1) // 2
        kj = sum((t >= first(j)).astype(jnp.int32) for j in range(1, nq)) if nq > 1 else 0 * t
        return kj, kj + t - first(kj)

    def body(q_ref, qc_ref, k_ref, kc_ref, v_ref, lse_ref, o_ref, do_ref,
             dq_ref, dcq_ref, dk_ref, dv_ref, dck_ref, dq_s, dcq_s, dk_s, dv_s, dck_s):
        kj, qi = pair(pl.program_id(1))

        @pl.when(qi == kj)
        def _():
            dk_s[...] = jnp.zeros(dk_s.shape, F32)
            dv_s[...] = jnp.zeros(dv_s.shape, F32)
            dck_s[...] = jnp.zeros(dck_s.shape, F32)

        def step(diag):
            do = do_ref[...]
            doc = do.astype(CDT)
            delta = jnp.sum(do * o_ref[...], axis=-1, keepdims=True)
            p = jnp.exp2(_fox_logits2(q_ref, qc_ref, k_ref, kc_ref, diag) - lse_ref[:, 0:1])
            dp = lax.dot_general(doc, v_ref[...], _NT, preferred_element_type=F32)
            ds = p * (dp - delta)
            dss = ds * scale
            dck_s[...] += jnp.sum(ds, axis=0, keepdims=True)
            dv_s[...] += jnp.dot(p.T.astype(CDT), doc, preferred_element_type=F32)
            dk_s[...] += jnp.dot(dss.T.astype(CDT), q_ref[...], preferred_element_type=F32)
            dq_part = jnp.dot(dss.astype(CDT), k_ref[...], preferred_element_type=F32)
            dc_part = jnp.sum(ds, axis=-1, keepdims=True)
            rows = pl.ds(pl.multiple_of(qi * blk, blk), blk)

            @pl.when(kj == 0)
            def _():
                dq_s[rows, :] = dq_part
                dcq_s[rows, :] = dc_part

            @pl.when(kj > 0)
            def _():
                dq_s[rows, :] += dq_part
                dcq_s[rows, :] += dc_part

        pl.when(qi > kj)(lambda: step(False))
        pl.when(qi == kj)(lambda: step(True))

        @pl.when(qi == nq - 1)
        def _():
            dk_ref[...] = dk_s[...].astype(dk_ref.dtype)
            dv_ref[...] = dv_s[...].astype(dv_ref.dtype)
            dck_ref[...] = dck_s[...].reshape(dck_ref.shape)

        @pl.when((qi == nq - 1) & (kj == nq - 1))
        def _():
            dq_ref[...] = dq_s[...].astype(dq_ref.dtype)
            dcq_ref[...] = jnp.broadcast_to(dcq_s[...], dcq_ref.shape)

    qspec = lambda off: pl.BlockSpec((blk, E), lambda h, t: (pair(t)[1], off + h))
    kspec = lambda off: pl.BlockSpec((blk, E), lambda h, t: (pair(t)[0], off + h))
    head = pl.BlockSpec((S, E), lambda h, t: (0, h))
    return pl.pallas_call(
        body, grid=(H, nq * (nq + 1) // 2),
        in_specs=[qspec(0), qspec(0), kspec(H), kspec(0), kspec(2 * H), qspec(0), qspec(0), qspec(0)],
        out_specs=[head, head, kspec(0), kspec(0), pl.BlockSpec((1, 1, blk), lambda h, t: (h, 0, pair(t)[0]))],
        out_shape=[jax.ShapeDtypeStruct((S, H * E), CDT), jax.ShapeDtypeStruct((S, H * E), F32),
                   jax.ShapeDtypeStruct((S, H * E), CDT), jax.ShapeDtypeStruct((S, H * E), CDT),
                   jax.ShapeDtypeStruct((H, 1, S), F32)],
        scratch_shapes=[pltpu.VMEM((S, E), F32), pltpu.VMEM((S, 1), F32), pltpu.VMEM((blk, E), F32),
                        pltpu.VMEM((blk, E), F32), pltpu.VMEM((1, blk), F32)],
        name=name, compiler_params=_cparams(("parallel", "arbitrary")),
    )(proj, qc, proj, kc, proj, lse, o, do)


DIL_SLAB = 16 * DIL_BLOCK
DIL_UNROLL = 8


def _rel_bucket(dist):
    max_exact = REL_BUCKETS // 2
    d = jnp.maximum(dist.astype(F32), 1.0)
    large = max_exact + (jnp.log(d / max_exact) / jnp.log(jnp.float32(REL_MAX_DISTANCE / max_exact))
                         * (REL_BUCKETS - max_exact)).astype(jnp.int32)
    large = jnp.minimum(large, REL_BUCKETS - 1)
    return jnp.where(dist < max_exact, dist, large)


def _bucket_table():
    i = jnp.arange(DIL_BLOCK)[:, None]
    j = jnp.arange(2 * DIL_BLOCK)[None, :]
    rel = DIL_BLOCK + i - j
    tabs = [_rel_bucket(jnp.clip(rel, 0, w // d) * d) for w, d in DIL_PATTERNS]
    return jnp.stack(tabs).astype(jnp.int32)


def _bias_table(rel_bias, buckets, name):
    P = buckets.shape[0]
    H = rel_bias.shape[1]

    def body(rb_ref, bk_ref, out_ref):
        h = pl.program_id(1)
        bk = bk_ref[0]
        val = jnp.zeros(bk.shape, F32)
        for b in range(REL_BUCKETS):
            val = jnp.where(bk == b, rb_ref[b, h], val)
        out_ref[0, 0] = val

    return pl.pallas_call(
        body, grid=(P, H),
        in_specs=[pl.BlockSpec(memory_space=pltpu.SMEM),
                  pl.BlockSpec((1, DIL_BLOCK, 2 * DIL_BLOCK), lambda p, h: (p, 0, 0))],
        out_specs=pl.BlockSpec((1, 1, DIL_BLOCK, 2 * DIL_BLOCK), lambda p, h: (p, h, 0, 0)),
        out_shape=jax.ShapeDtypeStruct((P, H, DIL_BLOCK, 2 * DIL_BLOCK), F32),
        name=name, compiler_params=_cparams(("parallel", "parallel")),
    )(rel_bias, buckets)


def _bias_table_bwd(dbias, buckets, name):
    P, H = dbias.shape[:2]

    def body(db_ref, bk_ref, out_ref):
        lane = lax.broadcasted_iota(jnp.int32, (1, REL_BUCKETS), 1)
        acc = jnp.zeros((1, REL_BUCKETS), F32)
        bk = bk_ref[...]
        db = db_ref[:, 0]
        for b in range(REL_BUCKETS):
            tot = jnp.sum(jnp.where(bk == b, db, 0.0))
            acc = jnp.where(lane == b, tot, acc)
        out_ref[0] = acc

    return pl.pallas_call(
        body, grid=(H,),
        in_specs=[pl.BlockSpec((P, 1, DIL_BLOCK, 2 * DIL_BLOCK), lambda h: (0, h, 0, 0)),
                  pl.BlockSpec((P, DIL_BLOCK, 2 * DIL_BLOCK), lambda h: (0, 0, 0))],
        out_specs=pl.BlockSpec((1, 1, REL_BUCKETS), lambda h: (h, 0, 0)),
        out_shape=jax.ShapeDtypeStruct((H, 1, REL_BUCKETS), F32),
        name=name, compiler_params=_cparams(("parallel",)),
    )(dbias, buckets)


def _bdot(a, b, contract_b):
    return lax.dot_general(a, b, (((2,), (contract_b,)), ((0,), (0,))), preferred_element_type=F32)


def _dil_units(first, d):
    units = []
    for t in range(DIL_UNROLL):
        u = first + t
        sg = u // d
        units.append((sg, sg * (DIL_BLOCK * d) + u % d))
    return units


def _dil_rows(ref, starts, d, dtype=None):
    t = jnp.stack([ref[pl.ds(s, DIL_BLOCK, stride=d), :] for s in starts])
    return t if dtype is None else t.astype(dtype)


def _dil_keys(ref, units, d):
    B, SL = DIL_BLOCK, DIL_SLAB
    return jnp.stack([jnp.concatenate([ref[pl.ds(SL + b - B * d, B, stride=d), :], ref[pl.ds(SL + b, B, stride=d), :]],
                                      axis=0) for _, b in units]).astype(CDT)


def _dil_logits(q, keys, bias_pc, first, d, has_before):
    T, B = q.shape[0], DIL_BLOCK
    ii = lax.broadcasted_iota(jnp.int32, (T, B, 2 * B), 1)
    jj = lax.broadcasted_iota(jnp.int32, (T, B, 2 * B), 2)
    sg = (first + lax.broadcasted_iota(jnp.int32, (T, B, 2 * B), 0)) // d
    mask = (jj >= ii) & (jj <= ii + B) & ((jj >= B) | (sg > 0) | has_before)
    return jnp.where(mask, _bdot(q, keys, 2) * HEAD_DIM ** -0.5 + bias_pc[None], NEG_INF)


def _dil_specs(H):
    E, SL = DIL_BLOCK, DIL_SLAB
    cur = lambda off: pl.BlockSpec((SL, E), lambda h, g: (g, off + h))
    prev = lambda off: pl.BlockSpec((SL, E), lambda h, g: (jnp.maximum(g - 1, 0), off + h))
    bias = pl.BlockSpec((len(DIL_PATTERNS), 1, E, 2 * E), lambda h, g: (0, h, 0, 0))
    return cur, prev, bias


def _dil_fwd(proj, bias, H, name):
    S = proj.shape[0]
    E = B = DIL_BLOCK
    SL = DIL_SLAB
    P = len(DIL_PATTERNS)
    assert S % SL == 0
    n_slabs = S // SL

    def body(q_ref, kc_ref, kp_ref, vc_ref, vp_ref, b_ref, y_ref, lse_ref, kj, vj, o_s, l_s):
        g = pl.program_id(1)
        kj[0:SL, :] = kp_ref[...]
        kj[SL:2 * SL, :] = kc_ref[...]
        vj[0:SL, :] = vp_ref[...]
        vj[SL:2 * SL, :] = vc_ref[...]
        for p, (_, d) in enumerate(DIL_PATTERNS):
            def batch(it, carry, p=p, d=d):
                first = it * DIL_UNROLL
                units = _dil_units(first, d)
                q = _dil_rows(q_ref, [b for _, b in units], d, CDT)
                s = _dil_logits(q, _dil_keys(kj, units, d), b_ref[p, 0], first, d, g > 0)
                m = jnp.max(s, axis=-1, keepdims=True)
                e = jnp.exp(s - m)
                ssum = jnp.sum(e, axis=-1, keepdims=True)
                o = _bdot(e.astype(CDT), _dil_keys(vj, units, d), 1) / ssum
                lse = jnp.broadcast_to(m + jnp.log(ssum), o.shape)
                for t, (_, b) in enumerate(units):
                    o_s[p, pl.ds(b, B, stride=d), :] = o[t]
                    l_s[p, pl.ds(b, B, stride=d), :] = lse[t]
                return carry

            lax.fori_loop(0, SL // B // DIL_UNROLL, batch, 0)
        ls = [l_s[p] for p in range(P)]
        m = functools.reduce(jnp.maximum, ls)
        w = [jnp.exp(l - m) for l in ls]
        tot = functools.reduce(jnp.add, w)
        y_ref[...] = functools.reduce(jnp.add, [(w[p] / tot) * o_s[p] for p in range(P)])
        lse_ref[...] = m + jnp.log(tot)

    cur, prev, bspec = _dil_specs(H)
    return pl.pallas_call(
        body, grid=(H, n_slabs),
        in_specs=[cur(0), cur(H), prev(H), cur(2 * H), prev(2 * H), bspec],
        out_specs=[cur(0), cur(0)],
        out_shape=[jax.ShapeDtypeStruct((S, H * E), F32)] * 2,
        scratch_shapes=[pltpu.VMEM((2 * SL, E), F32), pltpu.VMEM((2 * SL, E), F32),
                        pltpu.VMEM((P, SL, E), F32), pltpu.VMEM((P, SL, E), F32)],
        name=name, compiler_params=_cparams(("parallel", "parallel")),
    )(proj, proj, proj, proj, proj, bias)


def _dil_bwd(proj, bias, y, dy, lse, H, name):
    S = proj.shape[0]
    E = B = DIL_BLOCK
    SL = DIL_SLAB
    P = len(DIL_PATTERNS)
    assert S % SL == 0
    n_slabs = S // SL
    scale = E ** -0.5

    def body(q_ref, kc_ref, kp_ref, vc_ref, vp_ref, b_ref, y_ref, dy_ref, lse_ref,
             dq_ref, dk_ref, dv_ref, db_ref, kj, vj, dq_s, dk_own, dv_own, dk_held, dv_held, dk_back, dv_back, dl_s):
        g = pl.program_id(1)

        @pl.when(g == 0)
        def _():
            db_ref[...] = jnp.zeros(db_ref.shape, F32)

        @pl.when(g > 0)
        def _():
            dk_held[...] = dk_own[...]
            dv_held[...] = dv_own[...]
            dk_back[...] = jnp.zeros(dk_back.shape, F32)
            dv_back[...] = jnp.zeros(dv_back.shape, F32)

        @pl.when(g < n_slabs)
        def _():
            kj[0:SL, :] = kp_ref[...]
            kj[SL:2 * SL, :] = kc_ref[...]
            vj[0:SL, :] = vp_ref[...]
            vj[SL:2 * SL, :] = vc_ref[...]
            dq_s[...] = jnp.zeros(dq_s.shape, F32)
            dk_own[...] = jnp.zeros(dk_own.shape, F32)
            dv_own[...] = jnp.zeros(dv_own.shape, F32)
            dl_s[...] = jnp.broadcast_to(jnp.sum(dy_ref[...] * y_ref[...], axis=-1, keepdims=True), (SL, E))
            tr = lambda t: jnp.swapaxes(t, 1, 2).astype(CDT)
            for p, (_, d) in enumerate(DIL_PATTERNS):
                def batch(it, carry, p=p, d=d):
                    first = it * DIL_UNROLL
                    units = _dil_units(first, d)
                    starts = [b for _, b in units]
                    q = _dil_rows(q_ref, starts, d, CDT)
                    dyc = _dil_rows(dy_ref, starts, d, CDT)
                    keys, vals = _dil_keys(kj, units, d), _dil_keys(vj, units, d)
                    s = _dil_logits(q, keys, b_ref[p, 0], first, d, g > 0)
                    e = jnp.exp(s - _dil_rows(lse_ref, starts, d)[:, :, 0:1])
                    ds = e * (_bdot(dyc, vals, 2) - _dil_rows(dl_s, starts, d)[:, :, 0:1])
                    dss = ds * scale
                    dq = _bdot(dss.astype(CDT), keys, 1)
                    dk = _bdot(tr(dss), q, 1)
                    dv = _bdot(tr(e), dyc, 1)
                    for t, (sg, b) in enumerate(units):
                        rows = pl.ds(b, B, stride=d)
                        dq_s[rows, :] += dq[t]
                        dk_own[rows, :] += dk[t, B:]
                        dv_own[rows, :] += dv[t, B:]

                        if B * d < SL:
                            @pl.when(sg > 0)
                            def _(t=t, b=b):
                                before = pl.ds(b - B * d, B, stride=d)
                                dk_own[before, :] += dk[t, :B]
                                dv_own[before, :] += dv[t, :B]

                        @pl.when((sg == 0) & (g > 0))
                        def _(t=t, b=b):
                            before = pl.ds(SL + b - B * d, B, stride=d)
                            dk_back[before, :] += dk[t, :B]
                            dv_back[before, :] += dv[t, :B]

                    db_ref[p, 0] += jnp.sum(ds, axis=0)
                    return carry

                lax.fori_loop(0, SL // B // DIL_UNROLL, batch, 0)
            dq_ref[...] = dq_s[...].astype(dq_ref.dtype)

        @pl.when(g > 0)
        def _():
            dk_ref[...] = (dk_held[...] + dk_back[...]).astype(dk_ref.dtype)
            dv_ref[...] = (dv_held[...] + dv_back[...]).astype(dv_ref.dtype)

    last = n_slabs - 1
    cur = lambda off: pl.BlockSpec((SL, E), lambda h, g: (jnp.minimum(g, last), off + h))
    prev = lambda off: pl.BlockSpec((SL, E), lambda h, g: (jnp.maximum(jnp.minimum(g, last) - 1, 0), off + h))
    late = pl.BlockSpec((SL, E), lambda h, g: (jnp.maximum(g - 1, 0), h))
    bspec = pl.BlockSpec((P, 1, B, 2 * B), lambda h, g: (0, h, 0, 0))
    slab = pltpu.VMEM((SL, E), F32)
    return pl.pallas_call(
        body, grid=(H, n_slabs + 1),
        in_specs=[cur(0), cur(H), prev(H), cur(2 * H), prev(2 * H), bspec, cur(0), cur(0), cur(0)],
        out_specs=[cur(0), late, late, bspec],
        out_shape=[jax.ShapeDtypeStruct((S, H * E), CDT)] * 3 + [jax.ShapeDtypeStruct((P, H, B, 2 * B), F32)],
        scratch_shapes=[pltpu.VMEM((2 * SL, E), F32), pltpu.VMEM((2 * SL, E), F32)] + [slab] * 8,
        name=name, compiler_params=_cparams(("parallel", "arbitrary")),
    )(proj, proj, proj, proj, proj, bias, y, dy, lse)


def _adamw(w, g, m, v, name, br=128):
    R, C = w.shape
    br = br if R % br == 0 else R

    def body(w_ref, g_ref, m_ref, v_ref, g_out, d_ref, nm_ref, nv_ref):
        g_ = g_ref[...]
        g_out[...] = g_
        m_ = ADAM_B1 * m_ref[...] + (1.0 - ADAM_B1) * g_
        v_ = ADAM_B2 * v_ref[...] + (1.0 - ADAM_B2) * jnp.square(g_)
        m_hat = m_ / (1.0 - ADAM_B1 ** ADAM_STEP)
        v_hat = v_ / (1.0 - ADAM_B2 ** ADAM_STEP)
        d_ref[...] = -ADAM_LR * (m_hat / (jnp.sqrt(v_hat) + ADAM_EPS) + ADAM_WD * w_ref[...])
        nm_ref[...] = m_
        nv_ref[...] = v_

    blk = pl.BlockSpec((br, C), lambda i: (i, 0))
    return pl.pallas_call(
        body, grid=(R // br,), in_specs=[blk] * 4, out_specs=[blk] * 4,
        out_shape=[jax.ShapeDtypeStruct((R, C), F32)] * 4,
        name=name, compiler_params=_cparams(("parallel",)),
    )(w, g, m, v)


_HBM = pl.BlockSpec(memory_space=pltpu.HBM)
_SEM = pl.BlockSpec(memory_space=pltpu.SEMAPHORE)
_ANY = pl.BlockSpec(memory_space=pl.ANY)
_VMEM = pl.BlockSpec(memory_space=pltpu.VMEM)
_TOKEN = jax.ShapeDtypeStruct((8, 128), F32)


def _split_params():
    return pltpu.CompilerParams(has_side_effects=pltpu.SideEffectType.DATAFLOW_SIDE_EFFECTING)


def _place():
    x, y, c = lax.axis_index("x"), lax.axis_index("y"), lax.axis_index("c")
    chips = [(1 - x, y), (x, 1 - y), (1 - x, 1 - y)]
    return x, y, c, chips


def _tie(v, tokens, name):
    flat = v.reshape(1, -1)

    def body(v_ref, *rest):
        rest[-1][...] = v_ref[...]

    return pl.pallas_call(body, in_specs=[_VMEM] + [_ANY] * len(tokens), out_specs=_VMEM,
                          out_shape=jax.ShapeDtypeStruct(flat.shape, flat.dtype), name=name,
                          compiler_params=_cparams())(flat, *tokens).reshape(v.shape)


def _row_block(R, pref=256):
    return _tile(R, pref) if R % 128 == 0 else R


def _slot():
    return 2 * lax.axis_index("x") + lax.axis_index("y")


def _cast_into_slot(w, layer, name):
    _, R, C = w.shape
    br = _row_block(R)

    def body(w_ref, out_ref):
        out_ref[...] = w_ref[...].astype(out_ref.dtype)

    return pl.pallas_call(
        body, grid=(R // br,),
        in_specs=[pl.BlockSpec((None, br, C), lambda i: (layer, i, 0))],
        out_specs=pl.BlockSpec((None, br, C), lambda i: (_slot(), i, 0)),
        out_shape=jax.ShapeDtypeStruct((N_CHIPS, R, C), CDT),
        name=name, compiler_params=_cparams(("parallel",)),
    )(w)


def _gather_copies(src_ref, dst_ref, send_sems, recv_sems, incoming):
    Rh = src_ref.shape[1] // 2
    x, y, c, chips = _place()
    slot = 2 * x + y

    def half(ref, s, hf):
        return ref.at[s, pl.ds(hf * Rh, Rh), :]

    copies = []
    for j, (cx, cy) in enumerate(chips):
        for e in range(2):
            copies.append(pltpu.make_async_remote_copy(
                src_ref=half(src_ref, slot, c), dst_ref=half(dst_ref, 2 * cx + cy, e) if incoming else half(dst_ref, slot, c),
                send_sem=send_sems.at[2 * j + e], recv_sem=recv_sems.at[2 * j + (e if incoming else c)],
                device_id=(cx, cy, e), device_id_type=MESH))
    return copies


def _gather_start(buf, after, name):
    n_after = len(after)

    def body(*refs):
        buf_ref = refs[0]
        send_sems, recv_sems, out_ref, token = refs[1 + n_after:]
        for cp in _gather_copies(buf_ref, out_ref, send_sems, recv_sems, incoming=False):
            cp.start()
        token[...] = jnp.zeros(token.shape, token.dtype)

    return pl.pallas_call(
        body, in_specs=[_HBM] + [_ANY] * n_after, out_specs=(_SEM, _SEM, _HBM, _VMEM),
        out_shape=(pltpu.SemaphoreType.DMA((6,)), pltpu.SemaphoreType.DMA((6,)), pltpu.HBM(buf.shape, buf.dtype), _TOKEN),
        input_output_aliases={0: 2}, name=name, compiler_params=_split_params(),
    )(pltpu.with_memory_space_constraint(buf, pltpu.HBM), *after)


def _gather_wait(send_sems, recv_sems, buf, after, name):
    def body(buf_ref, send_sems, recv_sems, after_ref, out_ref):
        for cp in _gather_copies(buf_ref, out_ref, send_sems, recv_sems, incoming=False):
            cp.wait_send()
        for cp in _gather_copies(buf_ref, out_ref, send_sems, recv_sems, incoming=True):
            cp.wait_recv()

    return pl.pallas_call(
        body, in_specs=[_HBM, _SEM, _SEM, _ANY], out_specs=_HBM, out_shape=pltpu.HBM(buf.shape, buf.dtype),
        input_output_aliases={0: 0}, name=name, compiler_params=_split_params(),
    )(buf, send_sems, recv_sems, after)


def _relay_copies(src_ref, dst_ref, send_sems, recv_sems, stage, incoming):
    Rh = src_ref.shape[1] // 2
    x, y, c, chips = _place()
    copies = []
    for j, (cx, cy) in enumerate(chips):
        if stage == 0:
            src_slot, src_half, peer = 2 * x + y, c, (cx, cy, c)
            dst_slot, dst_half = (2 * cx + cy, c) if incoming else (src_slot, c)
        else:
            src_slot, src_half, peer = 2 * cx + cy, c, (x, y, 1 - c)
            dst_slot, dst_half = src_slot, (1 - c if incoming else c)
        copies.append(pltpu.make_async_remote_copy(
            src_ref=src_ref.at[src_slot, pl.ds(src_half * Rh, Rh), :],
            dst_ref=dst_ref.at[dst_slot, pl.ds(dst_half * Rh, Rh), :],
            send_sem=send_sems.at[j], recv_sem=recv_sems.at[j], device_id=peer, device_id_type=MESH))
    return copies


def _relay_start(buf, after, name):
    n_after = len(after)

    def body(*refs):
        buf_ref = refs[0]
        send_sems, recv_sems, out_ref, token = refs[1 + n_after:]
        for cp in _relay_copies(buf_ref, out_ref, send_sems, recv_sems, 0, incoming=False):
            cp.start()
        token[...] = jnp.zeros(token.shape, token.dtype)

    return pl.pallas_call(
        body, in_specs=[_HBM] + [_ANY] * n_after, out_specs=(_SEM, _SEM, _HBM, _VMEM),
        out_shape=(pltpu.SemaphoreType.DMA((3,)), pltpu.SemaphoreType.DMA((3,)), pltpu.HBM(buf.shape, buf.dtype), _TOKEN),
        input_output_aliases={0: 2}, name=name, compiler_params=_split_params(),
    )(pltpu.with_memory_space_constraint(buf, pltpu.HBM), *after)


def _relay_pass(send_sems, recv_sems, buf, after, name):
    def body(buf_ref, send0, recv0, after_ref, send1, recv1, out_ref):
        for cp in _relay_copies(buf_ref, out_ref, send0, recv0, 0, incoming=False):
            cp.wait_send()
        for cp in _relay_copies(buf_ref, out_ref, send0, recv0, 0, incoming=True):
            cp.wait_recv()
        for cp in _relay_copies(out_ref, out_ref, send1, recv1, 1, incoming=False):
            cp.start()

    return pl.pallas_call(
        body, in_specs=[_HBM, _SEM, _SEM, _ANY], out_specs=(_SEM, _SEM, _HBM),
        out_shape=(pltpu.SemaphoreType.DMA((3,)), pltpu.SemaphoreType.DMA((3,)), pltpu.HBM(buf.shape, buf.dtype)),
        input_output_aliases={0: 2}, name=name, compiler_params=_split_params(),
    )(buf, send_sems, recv_sems, after)


def _relay_wait(send_sems, recv_sems, buf, after, name):
    def body(buf_ref, send1, recv1, after_ref, out_ref):
        for cp in _relay_copies(buf_ref, out_ref, send1, recv1, 1, incoming=False):
            cp.wait_send()
        for cp in _relay_copies(buf_ref, out_ref, send1, recv1, 1, incoming=True):
            cp.wait_recv()

    return pl.pallas_call(
        body, in_specs=[_HBM, _SEM, _SEM, _ANY], out_specs=_HBM, out_shape=pltpu.HBM(buf.shape, buf.dtype),
        input_output_aliases={0: 0}, name=name, compiler_params=_split_params(),
    )(buf, send_sems, recv_sems, after)


def _scatter_copies(g_ref, land_ref, send_sems, recv_sems, incoming):
    Rh = g_ref.shape[1] // 2
    x, y, c, _ = _place()
    me = 4 * x + 2 * y + c
    copies = []
    for k in range(1, N_DEV):
        px, py, pc = (x + (k >> 2)) % 2, (y + ((k >> 1) & 1)) % 2, (c + (k & 1)) % 2
        copies.append(pltpu.make_async_remote_copy(
            src_ref=g_ref.at[2 * px + py, pl.ds(pc * Rh, Rh), :],
            dst_ref=land_ref.at[4 * px + 2 * py + pc if incoming else me],
            send_sem=send_sems.at[k - 1], recv_sem=recv_sems.at[k - 1], device_id=(px, py, pc), device_id_type=MESH))
    return copies


def _scatter_start(g, name):
    ns, R, C = g.shape

    def body(g_ref, land_ref, send_sems, recv_sems, g_thru, land_thru, token):
        for cp in _scatter_copies(g_ref, land_thru, send_sems, recv_sems, incoming=False):
            cp.start()
        token[...] = jnp.zeros(token.shape, token.dtype)

    land = lax.empty((N_DEV, R // 2, C), g.dtype)
    n = N_DEV - 1
    return pl.pallas_call(
        body, in_specs=[_HBM, _HBM], out_specs=(_SEM, _SEM, _HBM, _HBM, _VMEM),
        out_shape=(pltpu.SemaphoreType.DMA((n,)), pltpu.SemaphoreType.DMA((n,)), pltpu.HBM(g.shape, g.dtype),
                   pltpu.HBM(land.shape, land.dtype), _TOKEN),
        input_output_aliases={0: 2, 1: 3}, name=name, compiler_params=_split_params(),
    )(pltpu.with_memory_space_constraint(g, pltpu.HBM), pltpu.with_memory_space_constraint(land, pltpu.HBM))


def _scatter_wait(send_sems, recv_sems, g, land, after, name):
    def body(g_ref, land_ref, send_sems, recv_sems, after_ref, g_out, land_out):
        for cp in _scatter_copies(g_ref, land_out, send_sems, recv_sems, incoming=False):
            cp.wait_send()
        for cp in _scatter_copies(g_ref, land_out, send_sems, recv_sems, incoming=True):
            cp.wait_recv()

    return pl.pallas_call(
        body, in_specs=[_HBM, _HBM, _SEM, _SEM, _ANY], out_specs=(_HBM, _HBM),
        out_shape=(pltpu.HBM(g.shape, g.dtype), pltpu.HBM(land.shape, land.dtype)),
        input_output_aliases={0: 0, 1: 1}, name=name, compiler_params=_split_params(),
    )(g, land, send_sems, recv_sems, after)


def _device_sum(land, g, layer, n_layers, prev, name):
    nd, Rh, C = land.shape
    br = _row_block(Rh)
    nb = Rh // br
    core = lambda: lax.axis_index("c")
    me = lambda: 2 * _slot() + core()

    def body(*refs):
        own = refs[nd][...]
        acc = None
        for d in range(nd):
            t = jnp.where(me() == d, own, refs[d][...]).astype(F32)
            acc = t if acc is None else acc + t
        refs[-1][...] = acc

    def piece(d):
        return pl.BlockSpec((None, br, C), lambda i: (jnp.where(me() == d, (d + 1) % nd, d), i, 0))

    ins = [land] * nd + [g] + ([prev] if prev is not None else [])
    return pl.pallas_call(
        body, grid=(nb,),
        in_specs=[piece(d) for d in range(nd)]
        + [pl.BlockSpec((None, br, C), lambda i: (_slot(), core() * nb + i, 0))]
        + ([_ANY] if prev is not None else []),
        out_specs=pl.BlockSpec((None, br, C), lambda i: (layer, core() * nb + i, 0)),
        out_shape=jax.ShapeDtypeStruct((n_layers, 2 * Rh, C), F32),
        input_output_aliases={nd + 1: 0} if prev is not None else {},
        name=name, compiler_params=_cparams(("parallel",)),
    )(*ins)


def _join_copy(src_ref, dst_ref, layer, send_sem, recv_sem, incoming):
    Rh = src_ref.shape[1] // 2
    x, y, c, _ = _place()
    mine, other = pl.ds(c * Rh, Rh), pl.ds((1 - c) * Rh, Rh)
    return pltpu.make_async_remote_copy(src_ref=src_ref.at[layer, mine, :],
                                        dst_ref=dst_ref.at[layer, other if incoming else mine, :],
                                        send_sem=send_sem, recv_sem=recv_sem, device_id=(x, y, 1 - c),
                                        device_id_type=MESH)


def _join_start(g, layer, name):
    def body(g_ref, send_sem, recv_sem, out_ref, token):
        _join_copy(g_ref, out_ref, layer, send_sem, recv_sem, incoming=False).start()
        token[...] = jnp.zeros(token.shape, token.dtype)

    return pl.pallas_call(
        body, in_specs=[_HBM], out_specs=(_SEM, _SEM, _HBM, _VMEM),
        out_shape=(pltpu.SemaphoreType.DMA(()), pltpu.SemaphoreType.DMA(()), pltpu.HBM(g.shape, g.dtype), _TOKEN),
        input_output_aliases={0: 2}, name=name, compiler_params=_split_params(),
    )(pltpu.with_memory_space_constraint(g, pltpu.HBM))


def _join_wait(send_sem, recv_sem, g, layer, after, name):
    def body(g_ref, send_sem, recv_sem, after_ref, out_ref):
        _join_copy(g_ref, out_ref, layer, send_sem, recv_sem, incoming=False).wait_send()
        _join_copy(g_ref, out_ref, layer, send_sem, recv_sem, incoming=True).wait_recv()

    return pl.pallas_call(
        body, in_specs=[_HBM, _SEM, _SEM, _ANY], out_specs=_HBM, out_shape=pltpu.HBM(g.shape, g.dtype),
        input_output_aliases={0: 0}, name=name, compiler_params=_split_params(),
    )(g, send_sem, recv_sem, after)


def _all_reduce_small(v, name):
    rows, cols = v.shape

    def body(v_ref, out_ref, buf, send_sems, recv_sems):
        x, y, c, _ = _place()
        me = 4 * x + 2 * y + c
        buf[me] = v_ref[...]
        peers = []
        for k in range(1, N_DEV):
            px, py, pc = (x + (k >> 2)) % 2, (y + ((k >> 1) & 1)) % 2, (c + (k & 1)) % 2
            peers.append((px, py, pc))
        sends = []
        for k, peer in enumerate(peers):
            cp = pltpu.make_async_remote_copy(src_ref=v_ref, dst_ref=buf.at[me], send_sem=send_sems.at[k],
                                              recv_sem=recv_sems.at[k], device_id=peer, device_id_type=MESH)
            cp.start()
            sends.append(cp)
        for k, (px, py, pc) in enumerate(peers):
            pltpu.make_async_remote_copy(src_ref=v_ref, dst_ref=buf.at[4 * px + 2 * py + pc], send_sem=send_sems.at[k],
                                         recv_sem=recv_sems.at[k], device_id=(px, py, pc),
                                         device_id_type=MESH).wait_recv()
        for cp in sends:
            cp.wait_send()
        acc = buf[0]
        for i in range(1, N_DEV):
            acc = acc + buf[i]
        out_ref[...] = acc

    vmem = pl.BlockSpec(memory_space=pltpu.VMEM)
    return pl.pallas_call(
        body, in_specs=[vmem], out_specs=vmem, out_shape=jax.ShapeDtypeStruct((rows, cols), F32),
        scratch_shapes=[pltpu.VMEM((N_DEV, rows, cols), F32), pltpu.SemaphoreType.DMA((N_DEV - 1,)),
                        pltpu.SemaphoreType.DMA((N_DEV - 1,))],
        name=name, compiler_params=pltpu.CompilerParams(),
    )(v)


def _reduce_scatter_sum(started, after, layer, n_layers, prev, tag):
    send_sems, recv_sems, g, land, _ = started
    g, land = _scatter_wait(send_sems, recv_sems, g, land, after, f"rs_wait_{tag}")
    f = _device_sum(land, g, layer, n_layers, prev, f"rs_sum_{tag}")
    return _join_start(f, layer, f"rs_join_start_{tag}")


def _split_w_in(wg, Hf, name):
    ns, D, cols = wg.shape
    a = 3 * Hf * HEAD_DIM
    n6 = ns * cols - Hf
    br = _row_block(D)

    def body(w_ref, w6_ref, wf_ref):
        nat = jnp.concatenate([w_ref[s] for s in range(ns)], axis=1)
        w6_ref[...] = jnp.concatenate([nat[:, :a], nat[:, a + Hf:]], axis=1)
        wf_ref[...] = nat[:, a:a + Hf]

    w6, wf = pl.pallas_call(
        body, grid=(D // br,), in_specs=[pl.BlockSpec((ns, br, cols), lambda i: (0, i, 0))],
        out_specs=[pl.BlockSpec((br, n6), lambda i: (i, 0)), pl.BlockSpec((br, Hf), lambda i: (i, 0))],
        out_shape=[jax.ShapeDtypeStruct((D, n6), wg.dtype), jax.ShapeDtypeStruct((D, Hf), wg.dtype)],
        name=name, compiler_params=_cparams(("parallel",)),
    )(wg)
    return w6, wf.T


def _join_dw_in(dw6, dwf_t, Hf, name):
    D, n6 = dw6.shape
    a = 3 * Hf * HEAD_DIM
    cols = (n6 + Hf) // N_CHIPS
    br = _row_block(D)

    def body(w6_ref, wf_ref, out_ref):
        w6 = w6_ref[...]
        nat = jnp.concatenate([w6[:, :a], wf_ref[...], w6[:, a:]], axis=1)
        for s in range(N_CHIPS):
            out_ref[s] = nat[:, s * cols:(s + 1) * cols]

    return pl.pallas_call(
        body, grid=(D // br,),
        in_specs=[pl.BlockSpec((br, n6), lambda i: (i, 0)), pl.BlockSpec((br, Hf), lambda i: (i, 0))],
        out_specs=pl.BlockSpec((N_CHIPS, br, cols), lambda i: (0, i, 0)),
        out_shape=jax.ShapeDtypeStruct((N_CHIPS, D, cols), dw6.dtype),
        name=name, compiler_params=_cparams(("parallel",)),
    )(dw6, dwf_t.T.astype(dw6.dtype))


def _tied(v, tokens, name):
    return _tie(v, tokens, name) if tokens else v


def _layer_fwd(x, p, weight, bias, tokens, tag):
    Hf, Hd = p["forget_b"].shape[0], bias.shape[1]
    h1 = _rms_fwd(x, _tied(p["norm1_g"], tokens, f"tie_norm1_{tag}"), f"norm1_{tag}")
    w6, wf_t = _split_w_in(weight("w_in", h1), Hf, f"split_w_in_{tag}")
    n_a = 3 * Hf * HEAD_DIM
    proj_a = _mm_nn(h1, w6, f"proj_a_{tag}", [CDT], epi=lambda acc: (acc,), b_cols=(0, n_a))[0]
    proj_b = _mm_nn(h1, w6, f"proj_b_{tag}", [F32], b_cols=(n_a, w6.shape[1] - n_a))[0]
    f_t = _mm_nt(wf_t, h1, f"fproj_{tag}", [F32])[0]
    qc, kc = _fox_bias_operands(_gates_fwd(f_t, p["forget_b"], f"gates_{tag}"), f"fox_operands_{tag}")
    y_a, lse_a = _fox_fwd(proj_a, qc, kc, Hf, f"fox_{tag}")
    y_b, lse_b = _dil_fwd(proj_b, bias, Hd, f"dil_{tag}")
    mixed = _pair_norm_fwd(y_a, y_b, p["outnorm_a_g"], p["outnorm_b_g"], f"norm_ab_{tag}")
    w_out = weight("w_out", mixed)
    w_out = w_out.reshape(-1, w_out.shape[2])
    x1 = _mm_nn(mixed, w_out, f"attn_out_{tag}", [F32], extras=[x])[0]
    h2 = _rms_fwd(x1, p["norm2_g"], f"norm2_{tag}")
    w_mi = weight("w_mlp_in", h2)
    u, act = _mm_nn(h2, w_mi, f"mlp_in_{tag}", [CDT, CDT], b_slots=True,
                    epi=lambda acc: (acc, jnp.square(jnp.maximum(acc, 0.0))))
    w_mo = weight("w_mlp_out", act)
    w_mo = w_mo.reshape(-1, w_mo.shape[2])
    x2 = _mm_nn(act, w_mo, f"mlp_out_{tag}", [F32], extras=[x1])[0]
    saved = dict(x=x, h1=h1, proj_a=proj_a, proj_b=proj_b, f_t=f_t, qc=qc, kc=kc, y_a=y_a, lse_a=lse_a, y_b=y_b,
                 lse_b=lse_b, mixed=mixed, x1=x1, h2=h2, u=u, act=act, w6=w6, wf_t=wf_t, w_out=w_out, w_mi=w_mi,
                 w_mo=w_mo)
    return x2, saved


def _layer_bwd(dx2, dx2c, p, send, bias, sv, defer_w_out, tag):
    Hf, Hd = p["forget_b"].shape[0], bias.shape[1]
    E = HEAD_DIM
    rows = lambda g: g.reshape(N_CHIPS, -1, g.shape[1])
    du = _mm_nt(dx2c, sv["w_mo"], f"d_act_{tag}", [CDT], extras=[sv["u"]],
                epi=lambda acc, u: (acc * (2.0 * jnp.maximum(u.astype(F32), 0.0)),))[0]
    tokens = send("w_mlp_out", rows(_mm_tn(sv["act"], dx2c, f"dw_mlp_out_{tag}", CDT)))
    dh2 = _mm_nt(du, sv["w_mi"], f"d_h2_{tag}", [F32], b_slots=True)[0]
    tokens = tokens + send("w_mlp_in", _mm_tn(sv["h2"], du, f"dw_mlp_in_{tag}", CDT, out_slots=N_CHIPS))
    dx1, dx1c, g_norm2 = _rms_bwd(sv["x1"], _tied(p["norm2_g"], tokens, f"tie_norm2_{tag}"), dh2, dx2,
                                  f"d_norm2_{tag}")
    dmixed = _mm_nt(dx1c, sv["w_out"], f"d_mixed_{tag}", [F32])[0]
    send_w_out = lambda: send("w_out", rows(_mm_tn(sv["mixed"], dx1c, f"dw_out_{tag}", CDT)))
    tokens = [] if defer_w_out else send_w_out()
    dy_a, dy_b, g_na, g_nb = _pair_norm_bwd(sv["y_a"], sv["y_b"], _tied(p["outnorm_a_g"], tokens, f"tie_norm_a_{tag}"),
                                            p["outnorm_b_g"], dmixed, f"d_norm_ab_{tag}")
    dq_a, dcq, dk_a, dv_a, dck = _fox_bwd(sv["proj_a"], sv["qc"], sv["kc"], sv["lse_a"], sv["y_a"], dy_a, Hf,
                                          f"fox_bwd_{tag}")
    df, dfc, g_fb = _gates_bwd(sv["f_t"], p["forget_b"], dcq[:, ::E].T, dck.reshape(Hf, -1), f"d_gates_{tag}")
    dq_b, dk_b, dv_b, dbias = _dil_bwd(sv["proj_b"], bias, sv["y_b"], dy_b, sv["lse_b"], Hd, f"dil_bwd_{tag}")
    dproj = jnp.concatenate([dq_a, dk_a, dv_a, dq_b, dk_b, dv_b], axis=1)
    g_w6 = _mm_tn(sv["h1"], dproj, f"dw_in_{tag}", CDT)
    g_wf_t = _mm_nn(dfc, sv["h1"], f"dw_f_{tag}", [F32])[0]
    tokens = send("w_in", _join_dw_in(g_w6, g_wf_t, Hf, f"join_dw_in_{tag}"))
    dh1_f = _mm_tn(dfc, _tied(sv["wf_t"], tokens, f"tie_wf_{tag}"), f"d_h1_f_{tag}", F32)
    dh1 = _mm_nt(dproj, sv["w6"], f"d_h1_{tag}", [F32], extras=[dh1_f])[0]
    dx, dxc, g_norm1 = _rms_bwd(sv["x"], p["norm1_g"], dh1, dx1, f"d_norm1_{tag}")
    grads = dict(norm1_g=g_norm1[0], norm2_g=g_norm2[0], outnorm_a_g=g_na[0], outnorm_b_g=g_nb[0],
                 forget_b=g_fb[:, 0], dbias=dbias)
    return dx, dxc, grads, (send_w_out if defer_w_out else None)


_LAYER_SMALL = ("norm1_g", "forget_b", "outnorm_a_g", "outnorm_b_g", "norm2_g")


def _local_step(x, target, small, weight, send, tokens):
    depth = small["norm1_g"].shape[0]
    buckets = _bucket_table()
    bias = _bias_table(small["rel_bias"], buckets, "bias_table")
    layers, saved = [], []
    for l in range(depth):
        p = {k: small[k][l] for k in _LAYER_SMALL}
        layers.append(p)
        x, sv = _layer_fwd(x, p, functools.partial(weight, l), bias, tokens if l == 0 else [], f"l{l}")
        saved.append(sv)
    dx, dxc, g_final, loss = _loss_bwd(x, small["final_norm_g"], target, "loss")
    layer_grads = [None] * depth
    for l in reversed(range(depth)):
        dx, dxc, layer_grads[l], last = _layer_bwd(dx, dxc, layers[l], functools.partial(send, l), bias, saved[l],
                                                   l == 0, f"l{l}")
    tokens = last()
    dbias = functools.reduce(jnp.add, [g["dbias"] for g in layer_grads])
    g_rel = _bias_table_bwd(dbias, buckets, "d_bias_table")[:, 0, :].T
    small_grads = dict(final_norm_g=g_final[0], rel_bias=g_rel,
                       **{k: jnp.stack([g[k] for g in layer_grads]) for k in _LAYER_SMALL})
    return loss[0, 0], dx, small_grads, tokens


_BIG = ("w_in", "w_out", "w_mlp_in", "w_mlp_out")
_SMALL = ("norm1_g", "forget_b", "rel_bias", "outnorm_a_g", "outnorm_b_g", "norm2_g", "final_norm_g")
_ORDER = ("norm1_g", "w_in", "forget_b", "rel_bias", "outnorm_a_g", "outnorm_b_g", "w_out", "norm2_g", "w_mlp_in",
          "w_mlp_out", "final_norm_g")


def _pack_small(d):
    flat = jnp.concatenate([d[k].reshape(-1) for k in _SMALL])
    rows = -(-flat.shape[0] // (8 * SMALL_COLS)) * 8
    return jnp.pad(flat, (0, rows * SMALL_COLS - flat.shape[0])).reshape(rows, SMALL_COLS)


def _unpack_small(packed, like):
    flat, out, at = packed.reshape(-1), {}, 0
    for k in _SMALL:
        n = like[k].size
        out[k] = flat[at:at + n].reshape(like[k].shape)
        at += n
    return out


def kernel(x, norm1_g, w_in, forget_b, rel_bias, outnorm_a_g, outnorm_b_g, w_out, norm2_g, w_mlp_in, w_mlp_out, final_norm_g, loss_target, m_norm1_g, m_w_in, m_forget_b, m_rel_bias, m_outnorm_a_g, m_outnorm_b_g, m_w_out, m_norm2_g, m_w_mlp_in, m_w_mlp_out, m_final_norm_g, v_norm1_g, v_w_in, v_forget_b, v_rel_bias, v_outnorm_a_g, v_outnorm_b_g, v_w_out, v_norm2_g, v_w_mlp_in, v_w_mlp_out, v_final_norm_g):
    w = dict(norm1_g=norm1_g, w_in=w_in, forget_b=forget_b, rel_bias=rel_bias, outnorm_a_g=outnorm_a_g,
             outnorm_b_g=outnorm_b_g, w_out=w_out, norm2_g=norm2_g, w_mlp_in=w_mlp_in, w_mlp_out=w_mlp_out,
             final_norm_g=final_norm_g)
    m = dict(norm1_g=m_norm1_g, w_in=m_w_in, forget_b=m_forget_b, rel_bias=m_rel_bias, outnorm_a_g=m_outnorm_a_g,
             outnorm_b_g=m_outnorm_b_g, w_out=m_w_out, norm2_g=m_norm2_g, w_mlp_in=m_w_mlp_in,
             w_mlp_out=m_w_mlp_out, final_norm_g=m_final_norm_g)
    v = dict(norm1_g=v_norm1_g, w_in=v_w_in, forget_b=v_forget_b, rel_bias=v_rel_bias, outnorm_a_g=v_outnorm_a_g,
             outnorm_b_g=v_outnorm_b_g, w_out=v_w_out, norm2_g=v_norm2_g, w_mlp_in=v_w_mlp_in,
             w_mlp_out=v_w_mlp_out, final_norm_g=v_final_norm_g)
    depth = w_in.shape[0]
    small = {k: w[k] for k in _SMALL}

    gathers, passed, tokens = {}, {}, []
    for l in range(depth):
        for k in _BIG:
            buf = _cast_into_slot(w[k], l, f"cast_{k}_l{l}")
            start = _relay_start if k == _BIG[0] else _gather_start
            send_sems, recv_sems, buf, token = start(buf, tokens, f"gather_start_{k}_l{l}")
            gathers[l, k], tokens = (send_sems, recv_sems, buf), [token]

    def weight(l, k, after):
        if k == _BIG[-1] and l + 1 < depth:
            passed[l + 1] = _relay_pass(*gathers[l + 1, _BIG[0]], after, f"gather_pass_{_BIG[0]}_l{l + 1}")
        if k != _BIG[0]:
            return _gather_wait(*gathers[l, k], after, f"gather_wait_{k}_l{l}")
        if l not in passed:
            passed[l] = _relay_pass(*gathers[l, k], after, f"gather_pass_{k}_l{l}")
        return _relay_wait(*passed[l], after, f"gather_wait_{k}_l{l}")

    scatters = {}

    def send(l, k, g):
        scatters[l, k] = _scatter_start(g, f"rs_start_{k}_l{l}")
        return [scatters[l, k][4]]

    loss, grad_x, small_grads, tokens = _local_step(x[0], loss_target[0], small, weight, send, tokens)
    loss = lax.psum(loss, ("x", "y", "c"))

    grads, delta, new_m, new_v = {}, {}, {}, {}
    packed = _tied(_pack_small(small_grads), tokens, "tie_small")
    after, seen, joining = packed, {k: 0 for k in _BIG}, None

    def joined(after):
        (l, k), (send_sem, recv_sem, g) = joining
        grads[k] = _join_wait(send_sem, recv_sem, g, l, after, f"rs_join_wait_{k}_l{l}")
        seen[k] += 1
        if seen[k] < depth:
            return after
        shape = w[k].shape
        flat = lambda t: t.reshape(-1, shape[-1])
        g_, d_, m_, v_ = _adamw(flat(w[k]), flat(grads[k]), flat(m[k]), flat(v[k]), f"adamw_{k}")
        grads[k], delta[k], new_m[k], new_v[k] = (t.reshape(shape) for t in (g_, d_, m_, v_))
        return d_

    for (l, k), started in scatters.items():
        assert joining is None or joining[0][1] != k
        send_sem, recv_sem, g, token = _reduce_scatter_sum(started, after, l, depth, grads.get(k), f"{k}_l{l}")
        if joining is not None:
            after = joined(token)
        joining = ((l, k), (send_sem, recv_sem, g))
    after = joined(after)
    small_sums = _all_reduce_small(_tied(packed, [after], "tie_small_sums"), "small_all_reduce")
    grads.update(_unpack_small(small_sums, small))
    _, d_, m_, v_ = _adamw(_pack_small(small), _pack_small({k: grads[k] for k in _SMALL}),
                           _pack_small({k: m[k] for k in _SMALL}), _pack_small({k: v[k] for k in _SMALL}), "adamw_small")
    delta.update(_unpack_small(d_, small))
    new_m.update(_unpack_small(m_, small))
    new_v.update(_unpack_small(v_, small))

    return (loss, grad_x[None], *[grads[k] for k in _ORDER], *[delta[k] for k in _ORDER],
            *[new_m[k] for k in _ORDER], *[new_v[k] for k in _ORDER])
```

```python
import functools

import jax
import jax.numpy as jnp
from jax import lax
from jax.experimental import pallas as pl
from jax.experimental.pallas import tpu as pltpu

F32 = jnp.float32
CDT = jnp.bfloat16
HEAD_DIM = 128
NORM_EPS = 1e-6
NEG_INF = -1e30
LOG2E = 1.4426950408889634
REL_BUCKETS = 32
REL_MAX_DISTANCE = 2048
DIL_PATTERNS = ((128, 1), (512, 4), (2048, 16))
DIL_BLOCK = 128
ADAM_LR, ADAM_B1, ADAM_B2, ADAM_EPS, ADAM_WD, ADAM_STEP = 0.001, 0.9, 0.999, 1e-08, 0.01, 10
N_CHIPS = 4
N_DEV = 8
VMEM_LIMIT_BYTES = 56 * 1024 * 1024
SMALL_COLS = 1024
MESH = pl.DeviceIdType.MESH


def _cparams(sem=None):
    return pltpu.CompilerParams(dimension_semantics=sem, vmem_limit_bytes=VMEM_LIMIT_BYTES)


def _tile(dim, pref):
    t = min(pref, dim)
    t -= t % 128
    while t >= 128:
        if dim % t == 0:
            return t
        t -= 128
    return dim


def _rowwise(fn, ins, out_dtypes, name, bs=256, consts=()):
    R, C = ins[0].shape
    bs = min(bs, R)
    n_in, n_c = len(ins), len(consts)

    def body(*refs):
        vals = [r[...] for r in refs[:n_in + n_c]]
        res = fn(*vals)
        for o, r in zip(refs[n_in + n_c:], res):
            o[...] = r.astype(o.dtype)

    row = pl.BlockSpec((bs, C), lambda i: (i, 0))
    return pl.pallas_call(
        body, grid=(R // bs,),
        in_specs=[row] * n_in + [pl.BlockSpec((1, c.shape[-1]), lambda i: (0, 0)) for c in consts],
        out_specs=[row] * len(out_dtypes),
        out_shape=[jax.ShapeDtypeStruct((R, C), d) for d in out_dtypes],
        name=name, compiler_params=_cparams(("parallel",)),
    )(*ins, *[c.reshape(1, -1) for c in consts])


def _rms_fwd(x, g, name):
    def fn(xf, gg):
        r = lax.rsqrt(jnp.mean(xf * xf, axis=-1, keepdims=True) + NORM_EPS)
        return ((xf * r) * gg,)
    return _rowwise(fn, [x], [CDT], name, consts=[g])[0]


def _rms_bwd(x, g, dh, dres, name, bs=256):
    S, D = x.shape
    bs = min(bs, S)
    has_res = dres is not None

    def body(*refs):
        x_ref, g_ref, dh_ref = refs[:3]
        dx_ref, dxc_ref, dg_ref = refs[-3:]
        xf = x_ref[...]
        r = lax.rsqrt(jnp.mean(xf * xf, axis=-1, keepdims=True) + NORM_EPS)
        xhat = xf * r
        dh_ = dh_ref[...].astype(F32)
        dxhat = dh_ * g_ref[...]
        dx = r * (dxhat - xhat * jnp.mean(dxhat * xhat, axis=-1, keepdims=True))
        if has_res:
            dx = dx + refs[3][...]
        dx_ref[...] = dx
        dxc_ref[...] = dx.astype(dxc_ref.dtype)
        part = jnp.sum(dh_ * xhat, axis=0, keepdims=True)

        @pl.when(pl.program_id(0) == 0)
        def _():
            dg_ref[...] = part

        @pl.when(pl.program_id(0) > 0)
        def _():
            dg_ref[...] += part

    row = pl.BlockSpec((bs, D), lambda i: (i, 0))
    one = pl.BlockSpec((1, D), lambda i: (0, 0))
    ins = [x, g.reshape(1, D), dh] + ([dres] if has_res else [])
    return pl.pallas_call(
        body, grid=(S // bs,),
        in_specs=[row, one, row] + ([row] if has_res else []),
        out_specs=[row, row, one],
        out_shape=[jax.ShapeDtypeStruct((S, D), F32), jax.ShapeDtypeStruct((S, D), CDT),
                   jax.ShapeDtypeStruct((1, D), F32)],
        name=name, compiler_params=_cparams(("arbitrary",)),
    )(*ins)


def _pair_norm_fwd(y_a, y_b, g_a, g_b, name, bs=256):
    S, Da = y_a.shape
    Db = y_b.shape[1]
    bs = min(bs, S)

    def body(a_ref, b_ref, ga_ref, gb_ref, o_ref):
        def norm(x, g):
            r = lax.rsqrt(jnp.mean(x * x, axis=-1, keepdims=True) + NORM_EPS)
            return ((x * r) * g).astype(o_ref.dtype)
        o_ref[:, :Da] = norm(a_ref[...], ga_ref[...])
        o_ref[:, Da:] = norm(b_ref[...], gb_ref[...])

    row = lambda n: pl.BlockSpec((bs, n), lambda i: (i, 0))
    one = lambda n: pl.BlockSpec((1, n), lambda i: (0, 0))
    return pl.pallas_call(
        body, grid=(S // bs,), in_specs=[row(Da), row(Db), one(Da), one(Db)], out_specs=row(Da + Db),
        out_shape=jax.ShapeDtypeStruct((S, Da + Db), CDT), name=name, compiler_params=_cparams(("parallel",)),
    )(y_a, y_b, g_a.reshape(1, Da), g_b.reshape(1, Db))


def _pair_norm_bwd(y_a, y_b, g_a, g_b, dmixed, name, bs=256):
    S, Da = y_a.shape
    Db = y_b.shape[1]
    bs = min(bs, S)

    def body(a_ref, b_ref, ga_ref, gb_ref, dm_ref, da_ref, db_ref, dga_ref, dgb_ref):
        def one(x_ref, g_ref, dh, dx_ref, dg_ref):
            xf = x_ref[...]
            r = lax.rsqrt(jnp.mean(xf * xf, axis=-1, keepdims=True) + NORM_EPS)
            xhat = xf * r
            dxhat = dh * g_ref[...]
            dx_ref[...] = r * (dxhat - xhat * jnp.mean(dxhat * xhat, axis=-1, keepdims=True))
            part = jnp.sum(dh * xhat, axis=0, keepdims=True)

            @pl.when(pl.program_id(0) == 0)
            def _():
                dg_ref[...] = part

            @pl.when(pl.program_id(0) > 0)
            def _():
                dg_ref[...] += part

        dm = dm_ref[...]
        one(a_ref, ga_ref, dm[:, :Da], da_ref, dga_ref)
        one(b_ref, gb_ref, dm[:, Da:], db_ref, dgb_ref)

    row = lambda n: pl.BlockSpec((bs, n), lambda i: (i, 0))
    one_ = lambda n: pl.BlockSpec((1, n), lambda i: (0, 0))
    return pl.pallas_call(
        body, grid=(S // bs,), in_specs=[row(Da), row(Db), one_(Da), one_(Db), row(Da + Db)],
        out_specs=[row(Da), row(Db), one_(Da), one_(Db)],
        out_shape=[jax.ShapeDtypeStruct((S, Da), F32), jax.ShapeDtypeStruct((S, Db), F32),
                   jax.ShapeDtypeStruct((1, Da), F32), jax.ShapeDtypeStruct((1, Db), F32)],
        name=name, compiler_params=_cparams(("arbitrary",)),
    )(y_a, y_b, g_a.reshape(1, Da), g_b.reshape(1, Db), dmixed)


def _loss_bwd(x, g, target, name, bs=256):
    S, D = x.shape
    bs = min(bs, S)

    def body(x_ref, g_ref, t_ref, dx_ref, dxc_ref, dg_ref, loss_ref):
        xf = x_ref[...]
        r = lax.rsqrt(jnp.mean(xf * xf, axis=-1, keepdims=True) + NORM_EPS)
        xhat = xf * r
        err = xhat * g_ref[...] - t_ref[...]
        lpart = 0.5 * jnp.sum(jnp.mean(err * err, axis=-1, keepdims=True), axis=0, keepdims=True)
        dy = err / D
        dxhat = dy * g_ref[...]
        dx = r * (dxhat - xhat * jnp.mean(dxhat * xhat, axis=-1, keepdims=True))
        dx_ref[...] = dx
        dxc_ref[...] = dx.astype(dxc_ref.dtype)
        gpart = jnp.sum(dy * xhat, axis=0, keepdims=True)

        @pl.when(pl.program_id(0) == 0)
        def _():
            dg_ref[...] = gpart
            loss_ref[...] = lpart

        @pl.when(pl.program_id(0) > 0)
        def _():
            dg_ref[...] += gpart
            loss_ref[...] += lpart

    row = pl.BlockSpec((bs, D), lambda i: (i, 0))
    one = pl.BlockSpec((1, D), lambda i: (0, 0))
    return pl.pallas_call(
        body, grid=(S // bs,),
        in_specs=[row, one, row],
        out_specs=[row, row, one, pl.BlockSpec((1, 1), lambda i: (0, 0))],
        out_shape=[jax.ShapeDtypeStruct((S, D), F32), jax.ShapeDtypeStruct((S, D), CDT),
                   jax.ShapeDtypeStruct((1, D), F32), jax.ShapeDtypeStruct((1, 1), F32)],
        name=name, compiler_params=_cparams(("arbitrary",)),
    )(x, g.reshape(1, D), target)


_NN = (((1,), (0,)), ((), ()))
_NT = (((1,), (1,)), ((), ()))
_TN = (((0,), (0,)), ((), ()))


def _mm(a, b, *, M, N, K, a_spec, b_spec, o_spec, dims, tm, tn, tk, name, out_shapes, extras=(), epi=None):
    nk = K // tk
    n_ex, n_out = len(extras), len(out_shapes)
    in_place = epi is None
    if in_place:
        assert n_out == 1 and n_ex <= 1 and out_shapes[0].dtype == F32
        epi = lambda acc, *r: (acc + r[0] if r else acc,)

    def body(*refs):
        a_ref, b_ref = refs[0], refs[1]
        ex = refs[2:2 + n_ex]
        outs = refs[2 + n_ex:2 + n_ex + n_out]
        part = lax.dot_general(a_ref[...], b_ref[...], dims, preferred_element_type=F32)

        def finish(acc):
            for o, r in zip(outs, epi(acc, *[e[...] for e in ex])):
                o[...] = r.astype(o.dtype)

        if nk == 1:
            finish(part)
        elif in_place:
            k = pl.program_id(2)

            @pl.when(k == 0)
            def _():
                finish(part)

            @pl.when(k > 0)
            def _():
                outs[0][...] += part
        else:
            acc_ref = refs[-1]
            k = pl.program_id(2)

            @pl.when(k == 0)
            def _():
                acc_ref[...] = part

            @pl.when(k > 0)
            def _():
                acc_ref[...] += part

            @pl.when(k == nk - 1)
            def _():
                finish(acc_ref[...])

    ex_spec = pl.BlockSpec((tm, tn), lambda i, j, k: (i, j))
    return pl.pallas_call(
        body, grid=(M // tm, N // tn, nk),
        in_specs=[a_spec, b_spec] + [ex_spec] * n_ex,
        out_specs=[o_spec] * n_out,
        out_shape=out_shapes,
        scratch_shapes=[pltpu.VMEM((tm, tn), F32)] if nk > 1 and not in_place else [],
        name=name, compiler_params=_cparams(("parallel", "parallel", "arbitrary")),
    )(a, b, *extras)


def _mm_tiles(K):
    return (2048, 512, 2048) if K <= 2048 else (1024, 1024, 2048)


def _mm_nn(a, b, name, out_dtypes, extras=(), epi=None, b_slots=False, b_cols=None):
    M, K = a.shape
    tm, tn, tk = _mm_tiles(K)
    if b_slots:
        ns, _, Ns = b.shape
        N = ns * Ns
        tn = _tile(Ns, tn)
        npb = Ns // tn
        tk_ = _tile(K, tk)
        b_spec = pl.BlockSpec((None, tk_, tn), lambda i, j, k: (j // npb, k, j % npb))
    else:
        first, N = b_cols if b_cols is not None else (0, b.shape[1])
        tn = _tile(N, tn)
        assert first % tn == 0
        tk_ = _tile(K, tk)
        b_spec = pl.BlockSpec((tk_, tn), lambda i, j, k: (k, first // tn + j))
    tm = _tile(M, tm)
    return _mm(a, b, M=M, N=N, K=K, a_spec=pl.BlockSpec((tm, tk_), lambda i, j, k: (i, k)), b_spec=b_spec,
               o_spec=pl.BlockSpec((tm, tn), lambda i, j, k: (i, j)), dims=_NN, tm=tm, tn=tn, tk=tk_, name=name,
               out_shapes=[jax.ShapeDtypeStruct((M, N), d) for d in out_dtypes], extras=extras, epi=epi)


def _mm_nt(a, b, name, out_dtypes, extras=(), epi=None, b_slots=False):
    M, K = a.shape
    tm, tn, tk = _mm_tiles(K)
    tm = _tile(M, tm)
    if b_slots:
        ns, N, Ks = b.shape
        tk_ = _tile(Ks, tk)
        kpb = Ks // tk_
        tn = _tile(N, tn)
        b_spec = pl.BlockSpec((None, tn, tk_), lambda i, j, k: (k // kpb, j, k % kpb))
    else:
        N = b.shape[0]
        tk_ = _tile(K, tk)
        tn = _tile(N, tn)
        b_spec = pl.BlockSpec((tn, tk_), lambda i, j, k: (j, k))
    return _mm(a, b, M=M, N=N, K=K, a_spec=pl.BlockSpec((tm, tk_), lambda i, j, k: (i, k)), b_spec=b_spec,
               o_spec=pl.BlockSpec((tm, tn), lambda i, j, k: (i, j)), dims=_NT, tm=tm, tn=tn, tk=tk_, name=name,
               out_shapes=[jax.ShapeDtypeStruct((M, N), d) for d in out_dtypes], extras=extras, epi=epi)


def _mm_tn(a, b, name, out_dtype, out_slots=0, tm=2048, tn=1024, tk=2048):
    K, M = a.shape
    N = b.shape[1]
    tm, tk_ = _tile(M, tm), _tile(K, tk)
    if out_slots:
        Ns = N // out_slots
        tn = _tile(Ns, tn)
        npb = Ns // tn
        o_spec = pl.BlockSpec((None, tm, tn), lambda i, j, k: (j // npb, i, j % npb))
        out_shape = jax.ShapeDtypeStruct((out_slots, M, Ns), out_dtype)
    else:
        tn = _tile(N, tn)
        o_spec = pl.BlockSpec((tm, tn), lambda i, j, k: (i, j))
        out_shape = jax.ShapeDtypeStruct((M, N), out_dtype)
    return _mm(a, b, M=M, N=N, K=K, a_spec=pl.BlockSpec((tk_, tm), lambda i, j, k: (k, i)),
               b_spec=pl.BlockSpec((tk_, tn), lambda i, j, k: (k, j)), o_spec=o_spec, dims=_TN,
               tm=tm, tn=tn, tk=tk_, name=name, out_shapes=[out_shape],
               epi=None if out_dtype == F32 else (lambda acc: (acc,)))[0]


GATE_BLOCK = 512


def _split3(v):
    hi = v.astype(jnp.bfloat16)
    r1 = v - hi.astype(F32)
    mid = r1.astype(jnp.bfloat16)
    lo = (r1 - mid.astype(F32)).astype(jnp.bfloat16)
    return hi, mid, lo


def _exact_dot(v, tri):
    return functools.reduce(jnp.add, [jnp.dot(t, tri, preferred_element_type=F32) for t in _split3(v)])


def _gates_fwd(f_t, b, name):
    H, S = f_t.shape
    nb = _tile(S, GATE_BLOCK)
    inv_scale = HEAD_DIM ** 0.5

    def body(f_ref, b_ref, c_ref):
        upper = (lax.broadcasted_iota(jnp.int32, (nb, nb), 0)
                 <= lax.broadcasted_iota(jnp.int32, (nb, nb), 1)).astype(jnp.bfloat16)
        carry = jnp.zeros((H, 1), F32)
        for i in range(S // nb):
            z = f_ref[:, i * nb:(i + 1) * nb] + b_ref[...]
            logf = jnp.minimum(z, 0.0) - jnp.log1p(jnp.exp(-jnp.abs(z)))
            cs = _exact_dot(logf, upper) + carry
            for j, t in enumerate(_split3(cs * inv_scale)):
                c_ref[j, :, i * nb:(i + 1) * nb] = t
            carry = cs[:, nb - 1:nb]

    return pl.pallas_call(body, out_shape=jax.ShapeDtypeStruct((3, H, S), jnp.bfloat16), name=name,
                          compiler_params=_cparams())(f_t, b.reshape(H, 1))


def _gates_bwd(f_t, b, dcq, dck, name):
    H, S = f_t.shape
    nb = _tile(S, GATE_BLOCK)

    def body(f_ref, b_ref, dcq_ref, dck_ref, df_ref, dfc_ref, db_ref):
        lower = (lax.broadcasted_iota(jnp.int32, (nb, nb), 0)
                 >= lax.broadcasted_iota(jnp.int32, (nb, nb), 1)).astype(jnp.bfloat16)
        carry = jnp.zeros((H, 1), F32)
        db = jnp.zeros((H, 1), F32)
        for i in reversed(range(S // nb)):
            sl = slice(i * nb, (i + 1) * nb)
            dc = dcq_ref[:, sl] - dck_ref[:, sl]
            dlogf = _exact_dot(dc, lower) + carry
            carry = dlogf[:, 0:1]
            z = f_ref[:, sl] + b_ref[...]
            df = dlogf / (1.0 + jnp.exp(z))
            df_ref[:, sl] = df
            dfc_ref[:, sl] = df.astype(dfc_ref.dtype)
            db = db + jnp.sum(df, axis=1, keepdims=True)
        db_ref[...] = db

    return pl.pallas_call(
        body, out_shape=[jax.ShapeDtypeStruct((H, S), F32), jax.ShapeDtypeStruct((H, S), CDT),
                         jax.ShapeDtypeStruct((H, 1), F32)],
        name=name, compiler_params=_cparams())(f_t, b.reshape(H, 1), dcq, dck)


FOX_BLOCK = 1024


def _fox_bias_operands(csplit, name, bs=512):
    _, H, S = csplit.shape
    E = HEAD_DIM
    bs = _tile(S, bs)
    part = jnp.arange(3 * H)[:, None] // H
    head = jnp.arange(3 * H)[:, None] % H
    lane = jnp.arange(H * E)[None, :]
    place_q = (lane == head * E + part).astype(csplit.dtype)
    place_k = -(lane == head * E + 3 + part).astype(csplit.dtype)
    ones_q = ((lane % E >= 3) & (lane % E < 6)).astype(F32)
    ones_k = (lane % E < 3).astype(F32)

    def body(c_ref, pq_ref, pk_ref, oq_ref, ok_ref, qc_ref, kc_ref):
        c = c_ref[...]
        qc_ref[...] = (lax.dot_general(c, pq_ref[...], _TN, preferred_element_type=F32) + oq_ref[...]).astype(qc_ref.dtype)
        kc_ref[...] = (lax.dot_general(c, pk_ref[...], _TN, preferred_element_type=F32) + ok_ref[...]).astype(kc_ref.dtype)

    full = lambda a: pl.BlockSpec(a.shape, lambda i: (0, 0))
    out = pl.BlockSpec((bs, H * E), lambda i: (i, 0))
    return pl.pallas_call(
        body, grid=(S // bs,),
        in_specs=[pl.BlockSpec((3 * H, bs), lambda i: (0, i)), full(place_q), full(place_k), full(ones_q), full(ones_k)],
        out_specs=[out, out], out_shape=[jax.ShapeDtypeStruct((S, H * E), csplit.dtype)] * 2,
        name=name, compiler_params=_cparams(("parallel",)),
    )(csplit.reshape(3 * H, S), place_q, place_k, ones_q, ones_k)


def _fox_logits2(q_ref, qc_ref, k_ref, kc_ref, diag):
    q, k = q_ref[...], k_ref[...]
    qa = jnp.concatenate([q, qc_ref[...].astype(q.dtype)], axis=1)
    ka = jnp.concatenate([k, kc_ref[...].astype(k.dtype)], axis=1)
    s = lax.dot_general(qa, ka, _NT, preferred_element_type=F32) * (HEAD_DIM ** -0.5 * LOG2E)
    if diag:
        row = lax.broadcasted_iota(jnp.int32, s.shape, 0)
        col = lax.broadcasted_iota(jnp.int32, s.shape, 1)
        s = jnp.where(col <= row, s, NEG_INF)
    return s


def _fox_fwd(proj, qc, kc, H, name):
    S = proj.shape[0]
    E = HEAD_DIM
    blk = _tile(S, FOX_BLOCK)
    nq = S // blk

    def pair(t):
        qi = sum((t >= i * (i + 1) // 2).astype(jnp.int32) for i in range(1, nq)) if nq > 1 else 0 * t
        return qi, t - qi * (qi + 1) // 2

    def body(q_ref, qc_ref, k_ref, kc_ref, v_ref, o_ref, lse_ref, m_s, l_s, acc_s):
        qi, kj = pair(pl.program_id(1))

        @pl.when(kj == 0)
        def _():
            m_s[...] = jnp.full(m_s.shape, NEG_INF, F32)
            l_s[...] = jnp.zeros(l_s.shape, F32)
            acc_s[...] = jnp.zeros(acc_s.shape, F32)

        def step(diag):
            s = _fox_logits2(q_ref, qc_ref, k_ref, kc_ref, diag)
            m_prev = m_s[...]
            m_new = jnp.maximum(m_prev, jnp.max(s, axis=-1, keepdims=True))
            alpha = jnp.exp2(m_prev - m_new)
            p = jnp.exp2(s - m_new)
            l_s[...] = alpha * l_s[...] + jnp.sum(p, axis=-1, keepdims=True)
            acc_s[...] = alpha * acc_s[...] + jnp.dot(p.astype(CDT), v_ref[...], preferred_element_type=F32)
            m_s[...] = m_new

        pl.when(kj < qi)(lambda: step(False))
        pl.when(kj == qi)(lambda: step(True))

        @pl.when(kj == qi)
        def _():
            o_ref[...] = acc_s[...] / l_s[...]
            lse_ref[...] = jnp.broadcast_to(m_s[...] + jnp.log2(l_s[...]), lse_ref.shape)

    qspec = lambda off: pl.BlockSpec((blk, E), lambda h, t: (pair(t)[0], off + h))
    kspec = lambda off: pl.BlockSpec((blk, E), lambda h, t: (pair(t)[1], off + h))
    return pl.pallas_call(
        body, grid=(H, nq * (nq + 1) // 2),
        in_specs=[qspec(0), qspec(0), kspec(H), kspec(0), kspec(2 * H)],
        out_specs=[qspec(0)] * 2,
        out_shape=[jax.ShapeDtypeStruct((S, H * E), F32)] * 2,
        scratch_shapes=[pltpu.VMEM((blk, 1), F32), pltpu.VMEM((blk, 1), F32), pltpu.VMEM((blk, E), F32)],
        name=name, compiler_params=_cparams(("parallel", "arbitrary")),
    )(proj, qc, proj, kc, proj)


def _fox_bwd(proj, qc, kc, lse, o, do, H, name):
    S = proj.shape[0]
    E = HEAD_DIM
    blk = _tile(S, FOX_BLOCK)
    nq = S // blk
    scale = E ** -0.5

    def pair(t):
        first = lambda j: j * nq - j * (j - 1) // 2
        kj = sum((t >= first(j)).astype(jnp.int32) for j in range(1, nq)) if nq > 1 else 0 * t
        return kj, kj + t - first(kj)

    def body(q_ref, qc_ref, k_ref, kc_ref, v_ref, lse_ref, o_ref, do_ref,
             dq_ref, dcq_ref, dk_ref, dv_ref, dck_ref, dq_s, dcq_s, dk_s, dv_s, dck_s):
        kj, qi = pair(pl.program_id(1))

        @pl.when(qi == kj)
        def _():
            dk_s[...] = jnp.zeros(dk_s.shape, F32)
            dv_s[...] = jnp.zeros(dv_s.shape, F32)
            dck_s[...] = jnp.zeros(dck_s.shape, F32)

        def step(diag):
            do = do_ref[...]
            doc = do.astype(CDT)
            delta = jnp.sum(do * o_ref[...], axis=-1, keepdims=True)
            p = jnp.exp2(_fox_logits2(q_ref, qc_ref, k_ref, kc_ref, diag) - lse_ref[:, 0:1])
            dp = lax.dot_general(doc, v_ref[...], _NT, preferred_element_type=F32)
            ds = p * (dp - delta)
            dss = ds * scale
            dck_s[...] += jnp.sum(ds, axis=0, keepdims=True)
            dv_s[...] += jnp.dot(p.T.astype(CDT), doc, preferred_element_type=F32)
            dk_s[...] += jnp.dot(dss.T.astype(CDT), q_ref[...], preferred_element_type=F32)
            dq_part = jnp.dot(dss.astype(CDT), k_ref[...], preferred_element_type=F32)
            dc_part = jnp.sum(ds, axis=-1, keepdims=True)
            rows = pl.ds(pl.multiple_of(qi * blk, blk), blk)

            @pl.when(kj == 0)
            def _():
                dq_s[rows, :] = dq_part
                dcq_s[rows, :] = dc_part

            @pl.when(kj > 0)
            def _():
                dq_s[rows, :] += dq_part
                dcq_s[rows, :] += dc_part

        pl.when(qi > kj)(lambda: step(False))
        pl.when(qi == kj)(lambda: step(True))

        @pl.when(qi == nq - 1)
        def _():
            dk_ref[...] = dk_s[...].astype(dk_ref.dtype)
            dv_ref[...] = dv_s[...].astype(dv_ref.dtype)
            dck_ref[...] = dck_s[...].reshape(dck_ref.shape)

        @pl.when((qi == nq - 1) & (kj == nq - 1))
        def _():
            dq_ref[...] = dq_s[...].astype(dq_ref.dtype)
            dcq_ref[...] = jnp.broadcast_to(dcq_s[...], dcq_ref.shape)

    qspec = lambda off: pl.BlockSpec((blk, E), lambda h, t: (pair(t)[1], off + h))
    kspec = lambda off: pl.BlockSpec((blk, E), lambda h, t: (pair(t)[0], off + h))
    head = pl.BlockSpec((S, E), lambda h, t: (0, h))
    return pl.pallas_call(
        body, grid=(H, nq * (nq + 1) // 2),
        in_specs=[qspec(0), qspec(0), kspec(H), kspec(0), kspec(2 * H), qspec(0), qspec(0), qspec(0)],
        out_specs=[head, head, kspec(0), kspec(0), pl.BlockSpec((1, 1, blk), lambda h, t: (h, 0, pair(t)[0]))],
        out_shape=[jax.ShapeDtypeStruct((S, H * E), CDT), jax.ShapeDtypeStruct((S, H * E), F32),
                   jax.ShapeDtypeStruct((S, H * E), CDT), jax.ShapeDtypeStruct((S, H * E), CDT),
                   jax.ShapeDtypeStruct((H, 1, S), F32)],
        scratch_shapes=[pltpu.VMEM((S, E), F32), pltpu.VMEM((S, 1), F32), pltpu.VMEM((blk, E), F32),
                        pltpu.VMEM((blk, E), F32), pltpu.VMEM((1, blk), F32)],
        name=name, compiler_params=_cparams(("parallel", "arbitrary")),
    )(proj, qc, proj, kc, proj, lse, o, do)


DIL_SLAB = 16 * DIL_BLOCK
DIL_UNROLL = 16


def _rel_bucket(dist):
    max_exact = REL_BUCKETS // 2
    d = jnp.maximum(dist.astype(F32), 1.0)
    large = max_exact + (jnp.log(d / max_exact) / jnp.log(jnp.float32(REL_MAX_DISTANCE / max_exact))
                         * (REL_BUCKETS - max_exact)).astype(jnp.int32)
    large = jnp.minimum(large, REL_BUCKETS - 1)
    return jnp.where(dist < max_exact, dist, large)


def _bucket_table():
    i = jnp.arange(DIL_BLOCK)[:, None]
    j = jnp.arange(2 * DIL_BLOCK)[None, :]
    rel = DIL_BLOCK + i - j
    tabs = [_rel_bucket(jnp.clip(rel, 0, w // d) * d) for w, d in DIL_PATTERNS]
    return jnp.stack(tabs).astype(jnp.int32)


def _bias_table(rel_bias, buckets, name):
    P = buckets.shape[0]
    H = rel_bias.shape[1]

    def body(rb_ref, bk_ref, out_ref):
        h = pl.program_id(1)
        bk = bk_ref[0]
        val = jnp.zeros(bk.shape, F32)
        for b in range(REL_BUCKETS):
            val = jnp.where(bk == b, rb_ref[b, h], val)
        out_ref[0, 0] = val

    return pl.pallas_call(
        body, grid=(P, H),
        in_specs=[pl.BlockSpec(memory_space=pltpu.SMEM),
                  pl.BlockSpec((1, DIL_BLOCK, 2 * DIL_BLOCK), lambda p, h: (p, 0, 0))],
        out_specs=pl.BlockSpec((1, 1, DIL_BLOCK, 2 * DIL_BLOCK), lambda p, h: (p, h, 0, 0)),
        out_shape=jax.ShapeDtypeStruct((P, H, DIL_BLOCK, 2 * DIL_BLOCK), F32),
        name=name, compiler_params=_cparams(("parallel", "parallel")),
    )(rel_bias, buckets)


def _bias_table_bwd(dbias, buckets, name):
    P, H = dbias.shape[:2]

    def body(db_ref, bk_ref, out_ref):
        lane = lax.broadcasted_iota(jnp.int32, (1, REL_BUCKETS), 1)
        acc = jnp.zeros((1, REL_BUCKETS), F32)
        bk = bk_ref[...]
        db = db_ref[:, 0]
        for b in range(REL_BUCKETS):
            tot = jnp.sum(jnp.where(bk == b, db, 0.0))
            acc = jnp.where(lane == b, tot, acc)
        out_ref[0] = acc

    return pl.pallas_call(
        body, grid=(H,),
        in_specs=[pl.BlockSpec((P, 1, DIL_BLOCK, 2 * DIL_BLOCK), lambda h: (0, h, 0, 0)),
                  pl.BlockSpec((P, DIL_BLOCK, 2 * DIL_BLOCK), lambda h: (0, 0, 0))],
        out_specs=pl.BlockSpec((1, 1, REL_BUCKETS), lambda h: (h, 0, 0)),
        out_shape=jax.ShapeDtypeStruct((H, 1, REL_BUCKETS), F32),
        name=name, compiler_params=_cparams(("parallel",)),
    )(dbias, buckets)


def _bdot(a, b, contract_b):
    return lax.dot_general(a, b, (((2,), (contract_b,)), ((0,), (0,))), preferred_element_type=F32)


def _dil_units(first, d):
    units = []
    for t in range(DIL_UNROLL):
        u = first + t
        sg = u // d
        units.append((sg, sg * (DIL_BLOCK * d) + u % d))
    return units


def _dil_rows(ref, starts, d, dtype=None):
    t = jnp.stack([ref[pl.ds(s, DIL_BLOCK, stride=d), :] for s in starts])
    return t if dtype is None else t.astype(dtype)


def _dil_keys(ref, units, d):
    B, SL = DIL_BLOCK, DIL_SLAB
    return jnp.stack([jnp.concatenate([ref[pl.ds(SL + b - B * d, B, stride=d), :], ref[pl.ds(SL + b, B, stride=d), :]],
                                      axis=0) for _, b in units]).astype(CDT)


def _dil_logits(q, keys, bias_pc, first, d, has_before):
    T, B = q.shape[0], DIL_BLOCK
    ii = lax.broadcasted_iota(jnp.int32, (T, B, 2 * B), 1)
    jj = lax.broadcasted_iota(jnp.int32, (T, B, 2 * B), 2)
    sg = (first + lax.broadcasted_iota(jnp.int32, (T, B, 2 * B), 0)) // d
    mask = (jj >= ii) & (jj <= ii + B) & ((jj >= B) | (sg > 0) | has_before)
    return jnp.where(mask, _bdot(q, keys, 2) * HEAD_DIM ** -0.5 + bias_pc[None], NEG_INF)


def _dil_specs(H):
    E, SL = DIL_BLOCK, DIL_SLAB
    cur = lambda off: pl.BlockSpec((SL, E), lambda h, g: (g, off + h))
    prev = lambda off: pl.BlockSpec((SL, E), lambda h, g: (jnp.maximum(g - 1, 0), off + h))
    bias = pl.BlockSpec((len(DIL_PATTERNS), 1, E, 2 * E), lambda h, g: (0, h, 0, 0))
    return cur, prev, bias


def _dil_fwd(proj, bias, H, name):
    S = proj.shape[0]
    E = B = DIL_BLOCK
    SL = DIL_SLAB
    P = len(DIL_PATTERNS)
    assert S % SL == 0
    n_slabs = S // SL

    def body(q_ref, kc_ref, kp_ref, vc_ref, vp_ref, b_ref, y_ref, lse_ref, kj, vj, o_s, l_s):
        g = pl.program_id(1)
        kj[0:SL, :] = kp_ref[...]
        kj[SL:2 * SL, :] = kc_ref[...]
        vj[0:SL, :] = vp_ref[...]
        vj[SL:2 * SL, :] = vc_ref[...]
        for p, (_, d) in enumerate(DIL_PATTERNS):
            def batch(it, carry, p=p, d=d):
                first = it * DIL_UNROLL
                units = _dil_units(first, d)
                q = _dil_rows(q_ref, [b for _, b in units], d, CDT)
                s = _dil_logits(q, _dil_keys(kj, units, d), b_ref[p, 0], first, d, g > 0)
                m = jnp.max(s, axis=-1, keepdims=True)
                e = jnp.exp(s - m)
                ssum = jnp.sum(e, axis=-1, keepdims=True)
                o = _bdot(e.astype(CDT), _dil_keys(vj, units, d), 1) / ssum
                lse = jnp.broadcast_to(m + jnp.log(ssum), o.shape)
                for t, (_, b) in enumerate(units):
                    o_s[p, pl.ds(b, B, stride=d), :] = o[t]
                    l_s[p, pl.ds(b, B, stride=d), :] = lse[t]
                return carry

            lax.fori_loop(0, SL // B // DIL_UNROLL, batch, 0)
        ls = [l_s[p] for p in range(P)]
        m = functools.reduce(jnp.maximum, ls)
        w = [jnp.exp(l - m) for l in ls]
        tot = functools.reduce(jnp.add, w)
        y_ref[...] = functools.reduce(jnp.add, [(w[p] / tot) * o_s[p] for p in range(P)])
        lse_ref[...] = m + jnp.log(tot)

    cur, prev, bspec = _dil_specs(H)
    return pl.pallas_call(
        body, grid=(H, n_slabs),
        in_specs=[cur(0), cur(H), prev(H), cur(2 * H), prev(2 * H), bspec],
        out_specs=[cur(0), cur(0)],
        out_shape=[jax.ShapeDtypeStruct((S, H * E), F32)] * 2,
        scratch_shapes=[pltpu.VMEM((2 * SL, E), F32), pltpu.VMEM((2 * SL, E), F32),
                        pltpu.VMEM((P, SL, E), F32), pltpu.VMEM((P, SL, E), F32)],
        name=name, compiler_params=_cparams(("parallel", "parallel")),
    )(proj, proj, proj, proj, proj, bias)


def _dil_bwd(proj, bias, y, dy, lse, H, name):
    S = proj.shape[0]
    E = B = DIL_BLOCK
    SL = DIL_SLAB
    P = len(DIL_PATTERNS)
    assert S % SL == 0
    n_slabs = S // SL
    scale = E ** -0.5

    def body(q_ref, kc_ref, kp_ref, vc_ref, vp_ref, b_ref, y_ref, dy_ref, lse_ref,
             dq_ref, dk_ref, dv_ref, db_ref, kj, vj, dq_s, dk_own, dv_own, dk_held, dv_held, dk_back, dv_back, dl_s):
        g = pl.program_id(1)

        @pl.when(g == 0)
        def _():
            db_ref[...] = jnp.zeros(db_ref.shape, F32)

        @pl.when(g > 0)
        def _():
            dk_held[...] = dk_own[...]
            dv_held[...] = dv_own[...]
            dk_back[...] = jnp.zeros(dk_back.shape, F32)
            dv_back[...] = jnp.zeros(dv_back.shape, F32)

        @pl.when(g < n_slabs)
        def _():
            kj[0:SL, :] = kp_ref[...]
            kj[SL:2 * SL, :] = kc_ref[...]
            vj[0:SL, :] = vp_ref[...]
            vj[SL:2 * SL, :] = vc_ref[...]
            dq_s[...] = jnp.zeros(dq_s.shape, F32)
            dk_own[...] = jnp.zeros(dk_own.shape, F32)
            dv_own[...] = jnp.zeros(dv_own.shape, F32)
            dl_s[...] = jnp.broadcast_to(jnp.sum(dy_ref[...] * y_ref[...], axis=-1, keepdims=True), (SL, E))
            tr = lambda t: jnp.swapaxes(t, 1, 2).astype(CDT)
            for p, (_, d) in enumerate(DIL_PATTERNS):
                def batch(it, carry, p=p, d=d):
                    first = it * DIL_UNROLL
                    units = _dil_units(first, d)
                    starts = [b for _, b in units]
                    q = _dil_rows(q_ref, starts, d, CDT)
                    dyc = _dil_rows(dy_ref, starts, d, CDT)
                    keys, vals = _dil_keys(kj, units, d), _dil_keys(vj, units, d)
                    s = _dil_logits(q, keys, b_ref[p, 0], first, d, g > 0)
                    e = jnp.exp(s - _dil_rows(lse_ref, starts, d)[:, :, 0:1])
                    ds = e * (_bdot(dyc, vals, 2) - _dil_rows(dl_s, starts, d)[:, :, 0:1])
                    dss = ds * scale
                    dq = _bdot(dss.astype(CDT), keys, 1)
                    dk = _bdot(tr(dss), q, 1)
                    dv = _bdot(tr(e), dyc, 1)
                    for t, (sg, b) in enumerate(units):
                        rows = pl.ds(b, B, stride=d)
                        dq_s[rows, :] += dq[t]
                        dk_own[rows, :] += dk[t, B:]
                        dv_own[rows, :] += dv[t, B:]

                        if B * d < SL:
                            @pl.when(sg > 0)
                            def _(t=t, b=b):
                                before = pl.ds(b - B * d, B, stride=d)
                                dk_own[before, :] += dk[t, :B]
                                dv_own[before, :] += dv[t, :B]

                        @pl.when((sg == 0) & (g > 0))
                        def _(t=t, b=b):
                            before = pl.ds(SL + b - B * d, B, stride=d)
                            dk_back[before, :] += dk[t, :B]
                            dv_back[before, :] += dv[t, :B]

                    db_ref[p, 0] += jnp.sum(ds, axis=0)
                    return carry

                lax.fori_loop(0, SL // B // DIL_UNROLL, batch, 0)
            dq_ref[...] = dq_s[...].astype(dq_ref.dtype)

        @pl.when(g > 0)
        def _():
            dk_ref[...] = (dk_held[...] + dk_back[...]).astype(dk_ref.dtype)
            dv_ref[...] = (dv_held[...] + dv_back[...]).astype(dv_ref.dtype)

    last = n_slabs - 1
    cur = lambda off: pl.BlockSpec((SL, E), lambda h, g: (jnp.minimum(g, last), off + h))
    prev = lambda off: pl.BlockSpec((SL, E), lambda h, g: (jnp.maximum(jnp.minimum(g, last) - 1, 0), off + h))
    late = pl.BlockSpec((SL, E), lambda h, g: (jnp.maximum(g - 1, 0), h))
    bspec = pl.BlockSpec((P, 1, B, 2 * B), lambda h, g: (0, h, 0, 0))
    slab = pltpu.VMEM((SL, E), F32)
    return pl.pallas_call(
        body, grid=(H, n_slabs + 1),
        in_specs=[cur(0), cur(H), prev(H), cur(2 * H), prev(2 * H), bspec, cur(0), cur(0), cur(0)],
        out_specs=[cur(0), late, late, bspec],
        out_shape=[jax.ShapeDtypeStruct((S, H * E), CDT)] * 3 + [jax.ShapeDtypeStruct((P, H, B, 2 * B), F32)],
        scratch_shapes=[pltpu.VMEM((2 * SL, E), F32), pltpu.VMEM((2 * SL, E), F32)] + [slab] * 8,
        name=name, compiler_params=_cparams(("parallel", "arbitrary")),
    )(proj, proj, proj, proj, proj, bias, y, dy, lse)


def _adamw_tile(g_, w_ref, m_ref, v_ref, g_out, d_ref, nm_ref, nv_ref):
    g_out[...] = g_
    m_ = ADAM_B1 * m_ref[...] + (1.0 - ADAM_B1) * g_
    v_ = ADAM_B2 * v_ref[...] + (1.0 - ADAM_B2) * jnp.square(g_)
    m_hat = m_ / (1.0 - ADAM_B1 ** ADAM_STEP)
    v_hat = v_ / (1.0 - ADAM_B2 ** ADAM_STEP)
    d_ref[...] = -ADAM_LR * (m_hat / (jnp.sqrt(v_hat) + ADAM_EPS) + ADAM_WD * w_ref[...])
    nm_ref[...] = m_
    nv_ref[...] = v_


def _adamw(w, g, m, v, name, br=128):
    R, C = w.shape
    br = br if R % br == 0 else R

    def body(w_ref, g_ref, m_ref, v_ref, *outs):
        _adamw_tile(g_ref[...], w_ref, m_ref, v_ref, *outs)

    blk = pl.BlockSpec((br, C), lambda i: (i, 0))
    return pl.pallas_call(
        body, grid=(R // br,), in_specs=[blk] * 4, out_specs=[blk] * 4,
        out_shape=[jax.ShapeDtypeStruct((R, C), F32)] * 4,
        name=name, compiler_params=_cparams(("parallel",)),
    )(w, g, m, v)


_HBM = pl.BlockSpec(memory_space=pltpu.HBM)
_SEM = pl.BlockSpec(memory_space=pltpu.SEMAPHORE)
_ANY = pl.BlockSpec(memory_space=pl.ANY)
_VMEM = pl.BlockSpec(memory_space=pltpu.VMEM)
_TOKEN = jax.ShapeDtypeStruct((8, 128), F32)


def _split_params():
    return pltpu.CompilerParams(has_side_effects=pltpu.SideEffectType.DATAFLOW_SIDE_EFFECTING)


def _place():
    x, y, c = lax.axis_index("x"), lax.axis_index("y"), lax.axis_index("c")
    chips = [(1 - x, y), (x, 1 - y), (1 - x, 1 - y)]
    return x, y, c, chips


def _tie(v, tokens, name):
    flat = v.reshape(1, -1)

    def body(v_ref, *rest):
        rest[-1][...] = v_ref[...]

    return pl.pallas_call(body, in_specs=[_VMEM] + [_ANY] * len(tokens), out_specs=_VMEM,
                          out_shape=jax.ShapeDtypeStruct(flat.shape, flat.dtype), name=name,
                          compiler_params=_cparams())(flat, *tokens).reshape(v.shape)


def _row_block(R, pref=256):
    return _tile(R, pref) if R % 128 == 0 else R


def _slot():
    return 2 * lax.axis_index("x") + lax.axis_index("y")


def _cast_into_slot(w, layer, name):
    _, R, C = w.shape
    br = _row_block(R)

    def body(w_ref, out_ref):
        out_ref[...] = w_ref[...].astype(out_ref.dtype)

    return pl.pallas_call(
        body, grid=(R // br,),
        in_specs=[pl.BlockSpec((None, br, C), lambda i: (layer, i, 0))],
        out_specs=pl.BlockSpec((None, br, C), lambda i: (_slot(), i, 0)),
        out_shape=jax.ShapeDtypeStruct((N_CHIPS, R, C), CDT),
        name=name, compiler_params=_cparams(("parallel",)),
    )(w)


def _gather_copies(src_ref, dst_ref, send_sems, recv_sems, incoming):
    Rh = src_ref.shape[1] // 2
    x, y, c, chips = _place()
    slot = 2 * x + y

    def half(ref, s, hf):
        return ref.at[s, pl.ds(hf * Rh, Rh), :]

    copies = []
    for j, (cx, cy) in enumerate(chips):
        for e in range(2):
            copies.append(pltpu.make_async_remote_copy(
                src_ref=half(src_ref, slot, c), dst_ref=half(dst_ref, 2 * cx + cy, e) if incoming else half(dst_ref, slot, c),
                send_sem=send_sems.at[2 * j + e], recv_sem=recv_sems.at[2 * j + (e if incoming else c)],
                device_id=(cx, cy, e), device_id_type=MESH))
    return copies


def _gather_start(buf, after, name):
    n_after = len(after)

    def body(*refs):
        buf_ref = refs[0]
        send_sems, recv_sems, out_ref, token = refs[1 + n_after:]
        for cp in _gather_copies(buf_ref, out_ref, send_sems, recv_sems, incoming=False):
            cp.start()
        token[...] = jnp.zeros(token.shape, token.dtype)

    return pl.pallas_call(
        body, in_specs=[_HBM] + [_ANY] * n_after, out_specs=(_SEM, _SEM, _HBM, _VMEM),
        out_shape=(pltpu.SemaphoreType.DMA((6,)), pltpu.SemaphoreType.DMA((6,)), pltpu.HBM(buf.shape, buf.dtype), _TOKEN),
        input_output_aliases={0: 2}, name=name, compiler_params=_split_params(),
    )(pltpu.with_memory_space_constraint(buf, pltpu.HBM), *after)


def _gather_wait(send_sems, recv_sems, buf, after, name):
    def body(buf_ref, send_sems, recv_sems, after_ref, out_ref):
        for cp in _gather_copies(buf_ref, out_ref, send_sems, recv_sems, incoming=False):
            cp.wait_send()
        for cp in _gather_copies(buf_ref, out_ref, send_sems, recv_sems, incoming=True):
            cp.wait_recv()

    return pl.pallas_call(
        body, in_specs=[_HBM, _SEM, _SEM, _ANY], out_specs=_HBM, out_shape=pltpu.HBM(buf.shape, buf.dtype),
        input_output_aliases={0: 0}, name=name, compiler_params=_split_params(),
    )(buf, send_sems, recv_sems, after)


def _relay_copies(src_ref, dst_ref, send_sems, recv_sems, stage, incoming):
    Rh = src_ref.shape[1] // 2
    x, y, c, chips = _place()
    copies = []
    for j, (cx, cy) in enumerate(chips):
        if stage == 0:
            src_slot, src_half, peer = 2 * x + y, c, (cx, cy, c)
            dst_slot, dst_half = (2 * cx + cy, c) if incoming else (src_slot, c)
        else:
            src_slot, src_half, peer = 2 * cx + cy, c, (x, y, 1 - c)
            dst_slot, dst_half = src_slot, (1 - c if incoming else c)
        copies.append(pltpu.make_async_remote_copy(
            src_ref=src_ref.at[src_slot, pl.ds(src_half * Rh, Rh), :],
            dst_ref=dst_ref.at[dst_slot, pl.ds(dst_half * Rh, Rh), :],
            send_sem=send_sems.at[j], recv_sem=recv_sems.at[j], device_id=peer, device_id_type=MESH))
    return copies


def _relay_start(buf, after, name):
    n_after = len(after)

    def body(*refs):
        buf_ref = refs[0]
        send_sems, recv_sems, out_ref, token = refs[1 + n_after:]
        for cp in _relay_copies(buf_ref, out_ref, send_sems, recv_sems, 0, incoming=False):
            cp.start()
        token[...] = jnp.zeros(token.shape, token.dtype)

    return pl.pallas_call(
        body, in_specs=[_HBM] + [_ANY] * n_after, out_specs=(_SEM, _SEM, _HBM, _VMEM),
        out_shape=(pltpu.SemaphoreType.DMA((3,)), pltpu.SemaphoreType.DMA((3,)), pltpu.HBM(buf.shape, buf.dtype), _TOKEN),
        input_output_aliases={0: 2}, name=name, compiler_params=_split_params(),
    )(pltpu.with_memory_space_constraint(buf, pltpu.HBM), *after)


def _relay_pass(send_sems, recv_sems, buf, after, name):
    def body(buf_ref, send0, recv0, after_ref, send1, recv1, out_ref):
        for cp in _relay_copies(buf_ref, out_ref, send0, recv0, 0, incoming=False):
            cp.wait_send()
        for cp in _relay_copies(buf_ref, out_ref, send0, recv0, 0, incoming=True):
            cp.wait_recv()
        for cp in _relay_copies(out_ref, out_ref, send1, recv1, 1, incoming=False):
            cp.start()

    return pl.pallas_call(
        body, in_specs=[_HBM, _SEM, _SEM, _ANY], out_specs=(_SEM, _SEM, _HBM),
        out_shape=(pltpu.SemaphoreType.DMA((3,)), pltpu.SemaphoreType.DMA((3,)), pltpu.HBM(buf.shape, buf.dtype)),
        input_output_aliases={0: 2}, name=name, compiler_params=_split_params(),
    )(buf, send_sems, recv_sems, after)


def _relay_wait(send_sems, recv_sems, buf, after, name):
    def body(buf_ref, send1, recv1, after_ref, out_ref):
        for cp in _relay_copies(buf_ref, out_ref, send1, recv1, 1, incoming=False):
            cp.wait_send()
        for cp in _relay_copies(buf_ref, out_ref, send1, recv1, 1, incoming=True):
            cp.wait_recv()

    return pl.pallas_call(
        body, in_specs=[_HBM, _SEM, _SEM, _ANY], out_specs=_HBM, out_shape=pltpu.HBM(buf.shape, buf.dtype),
        input_output_aliases={0: 0}, name=name, compiler_params=_split_params(),
    )(buf, send_sems, recv_sems, after)


def _scatter_copies(g_ref, land_ref, send_sems, recv_sems, incoming):
    Rh = g_ref.shape[1] // 2
    x, y, c, _ = _place()
    me = 4 * x + 2 * y + c
    copies = []
    for k in range(1, N_DEV):
        px, py, pc = (x + (k >> 2)) % 2, (y + ((k >> 1) & 1)) % 2, (c + (k & 1)) % 2
        copies.append(pltpu.make_async_remote_copy(
            src_ref=g_ref.at[2 * px + py, pl.ds(pc * Rh, Rh), :],
            dst_ref=land_ref.at[4 * px + 2 * py + pc if incoming else me],
            send_sem=send_sems.at[k - 1], recv_sem=recv_sems.at[k - 1], device_id=(px, py, pc), device_id_type=MESH))
    return copies


def _scatter_start(g, name):
    ns, R, C = g.shape

    def body(g_ref, land_ref, send_sems, recv_sems, g_thru, land_thru, token):
        for cp in _scatter_copies(g_ref, land_thru, send_sems, recv_sems, incoming=False):
            cp.start()
        token[...] = jnp.zeros(token.shape, token.dtype)

    land = lax.empty((N_DEV, R // 2, C), g.dtype)
    n = N_DEV - 1
    return pl.pallas_call(
        body, in_specs=[_HBM, _HBM], out_specs=(_SEM, _SEM, _HBM, _HBM, _VMEM),
        out_shape=(pltpu.SemaphoreType.DMA((n,)), pltpu.SemaphoreType.DMA((n,)), pltpu.HBM(g.shape, g.dtype),
                   pltpu.HBM(land.shape, land.dtype), _TOKEN),
        input_output_aliases={0: 2, 1: 3}, name=name, compiler_params=_split_params(),
    )(pltpu.with_memory_space_constraint(g, pltpu.HBM), pltpu.with_memory_space_constraint(land, pltpu.HBM))


def _scatter_wait(send_sems, recv_sems, g, land, after, name):
    def body(g_ref, land_ref, send_sems, recv_sems, after_ref, g_out, land_out):
        for cp in _scatter_copies(g_ref, land_out, send_sems, recv_sems, incoming=False):
            cp.wait_send()
        for cp in _scatter_copies(g_ref, land_out, send_sems, recv_sems, incoming=True):
            cp.wait_recv()

    return pl.pallas_call(
        body, in_specs=[_HBM, _HBM, _SEM, _SEM, _ANY], out_specs=(_HBM, _HBM),
        out_shape=(pltpu.HBM(g.shape, g.dtype), pltpu.HBM(land.shape, land.dtype)),
        input_output_aliases={0: 0, 1: 1}, name=name, compiler_params=_split_params(),
    )(g, land, send_sems, recv_sems, after)


def _device_sum(land, g, layer, n_layers, prev, name):
    nd, Rh, C = land.shape
    br = _row_block(Rh)
    nb = Rh // br
    core = lambda: lax.axis_index("c")
    me = lambda: 2 * _slot() + core()

    def body(*refs):
        own = refs[nd][...]
        acc = None
        for d in range(nd):
            t = jnp.where(me() == d, own, refs[d][...]).astype(F32)
            acc = t if acc is None else acc + t
        refs[-1][...] = acc

    def piece(d):
        return pl.BlockSpec((None, br, C), lambda i: (jnp.where(me() == d, (d + 1) % nd, d), i, 0))

    ins = [land] * nd + [g] + ([prev] if prev is not None else [])
    return pl.pallas_call(
        body, grid=(nb,),
        in_specs=[piece(d) for d in range(nd)]
        + [pl.BlockSpec((None, br, C), lambda i: (_slot(), core() * nb + i, 0))]
        + ([_ANY] if prev is not None else []),
        out_specs=pl.BlockSpec((None, br, C), lambda i: (layer, core() * nb + i, 0)),
        out_shape=jax.ShapeDtypeStruct((n_layers, 2 * Rh, C), F32),
        input_output_aliases={nd + 1: 0} if prev is not None else {},
        name=name, compiler_params=_cparams(("parallel",)),
    )(*ins)


def _join_copy(src_ref, dst_ref, layer, send_sem, recv_sem, incoming):
    Rh = src_ref.shape[1] // 2
    x, y, c, _ = _place()
    mine, other = pl.ds(c * Rh, Rh), pl.ds((1 - c) * Rh, Rh)
    return pltpu.make_async_remote_copy(src_ref=src_ref.at[layer, mine, :],
                                        dst_ref=dst_ref.at[layer, other if incoming else mine, :],
                                        send_sem=send_sem, recv_sem=recv_sem, device_id=(x, y, 1 - c),
                                        device_id_type=MESH)


def _join_start(g, layer, name):
    def body(g_ref, send_sem, recv_sem, out_ref, token):
        _join_copy(g_ref, out_ref, layer, send_sem, recv_sem, incoming=False).start()
        token[...] = jnp.zeros(token.shape, token.dtype)

    return pl.pallas_call(
        body, in_specs=[_HBM], out_specs=(_SEM, _SEM, _HBM, _VMEM),
        out_shape=(pltpu.SemaphoreType.DMA(()), pltpu.SemaphoreType.DMA(()), pltpu.HBM(g.shape, g.dtype), _TOKEN),
        input_output_aliases={0: 2}, name=name, compiler_params=_split_params(),
    )(pltpu.with_memory_space_constraint(g, pltpu.HBM))


def _join_wait(send_sem, recv_sem, g, layer, after, name):
    def body(g_ref, send_sem, recv_sem, after_ref, out_ref):
        _join_copy(g_ref, out_ref, layer, send_sem, recv_sem, incoming=False).wait_send()
        _join_copy(g_ref, out_ref, layer, send_sem, recv_sem, incoming=True).wait_recv()

    return pl.pallas_call(
        body, in_specs=[_HBM, _SEM, _SEM, _ANY], out_specs=_HBM, out_shape=pltpu.HBM(g.shape, g.dtype),
        input_output_aliases={0: 0}, name=name, compiler_params=_split_params(),
    )(g, send_sem, recv_sem, after)


def _all_reduce_small(v, name):
    rows, cols = v.shape

    def body(v_ref, out_ref, buf, send_sems, recv_sems):
        x, y, c, _ = _place()
        me = 4 * x + 2 * y + c
        buf[me] = v_ref[...]
        peers = []
        for k in range(1, N_DEV):
            px, py, pc = (x + (k >> 2)) % 2, (y + ((k >> 1) & 1)) % 2, (c + (k & 1)) % 2
            peers.append((px, py, pc))
        sends = []
        for k, peer in enumerate(peers):
            cp = pltpu.make_async_remote_copy(src_ref=v_ref, dst_ref=buf.at[me], send_sem=send_sems.at[k],
                                              recv_sem=recv_sems.at[k], device_id=peer, device_id_type=MESH)
            cp.start()
            sends.append(cp)
        for k, (px, py, pc) in enumerate(peers):
            pltpu.make_async_remote_copy(src_ref=v_ref, dst_ref=buf.at[4 * px + 2 * py + pc], send_sem=send_sems.at[k],
                                         recv_sem=recv_sems.at[k], device_id=(px, py, pc),
                                         device_id_type=MESH).wait_recv()
        for cp in sends:
            cp.wait_send()
        acc = buf[0]
        for i in range(1, N_DEV):
            acc = acc + buf[i]
        out_ref[...] = acc

    vmem = pl.BlockSpec(memory_space=pltpu.VMEM)
    return pl.pallas_call(
        body, in_specs=[vmem], out_specs=vmem, out_shape=jax.ShapeDtypeStruct((rows, cols), F32),
        scratch_shapes=[pltpu.VMEM((N_DEV, rows, cols), F32), pltpu.SemaphoreType.DMA((N_DEV - 1,)),
                        pltpu.SemaphoreType.DMA((N_DEV - 1,))],
        name=name, compiler_params=pltpu.CompilerParams(),
    )(v)


def _reduce_scatter_sum(started, after, layer, n_layers, prev, tag):
    send_sems, recv_sems, g, land, _ = started
    g, land = _scatter_wait(send_sems, recv_sems, g, land, after, f"rs_wait_{tag}")
    f = _device_sum(land, g, layer, n_layers, prev, f"rs_sum_{tag}")
    return _join_start(f, layer, f"rs_join_start_{tag}")


def _split_w_in(wg, Hf, name):
    ns, D, cols = wg.shape
    a = 3 * Hf * HEAD_DIM
    n6 = ns * cols - Hf
    br = _row_block(D)

    def body(w_ref, w6_ref, wf_ref):
        nat = jnp.concatenate([w_ref[s] for s in range(ns)], axis=1)
        w6_ref[...] = jnp.concatenate([nat[:, :a], nat[:, a + Hf:]], axis=1)
        wf_ref[...] = nat[:, a:a + Hf]

    w6, wf = pl.pallas_call(
        body, grid=(D // br,), in_specs=[pl.BlockSpec((ns, br, cols), lambda i: (0, i, 0))],
        out_specs=[pl.BlockSpec((br, n6), lambda i: (i, 0)), pl.BlockSpec((br, Hf), lambda i: (i, 0))],
        out_shape=[jax.ShapeDtypeStruct((D, n6), wg.dtype), jax.ShapeDtypeStruct((D, Hf), wg.dtype)],
        name=name, compiler_params=_cparams(("parallel",)),
    )(wg)
    return w6, wf.T


def _join_dw_in(dw6, dwf_t, Hf, name):
    D, n6 = dw6.shape
    a = 3 * Hf * HEAD_DIM
    cols = (n6 + Hf) // N_CHIPS
    br = _row_block(D)

    def body(w6_ref, wf_ref, out_ref):
        w6 = w6_ref[...]
        nat = jnp.concatenate([w6[:, :a], wf_ref[...], w6[:, a:]], axis=1)
        for s in range(N_CHIPS):
            out_ref[s] = nat[:, s * cols:(s + 1) * cols]

    return pl.pallas_call(
        body, grid=(D // br,),
        in_specs=[pl.BlockSpec((br, n6), lambda i: (i, 0)), pl.BlockSpec((br, Hf), lambda i: (i, 0))],
        out_specs=pl.BlockSpec((N_CHIPS, br, cols), lambda i: (0, i, 0)),
        out_shape=jax.ShapeDtypeStruct((N_CHIPS, D, cols), dw6.dtype),
        name=name, compiler_params=_cparams(("parallel",)),
    )(dw6, dwf_t.T.astype(dw6.dtype))


def _tied(v, tokens, name):
    return _tie(v, tokens, name) if tokens else v


def _layer_fwd(x, p, weight, bias, tokens, tag):
    Hf, Hd = p["forget_b"].shape[0], bias.shape[1]
    h1 = _rms_fwd(x, _tied(p["norm1_g"], tokens, f"tie_norm1_{tag}"), f"norm1_{tag}")
    w6, wf_t = _split_w_in(weight("w_in", h1), Hf, f"split_w_in_{tag}")
    n_a = 3 * Hf * HEAD_DIM
    proj_a = _mm_nn(h1, w6, f"proj_a_{tag}", [CDT], epi=lambda acc: (acc,), b_cols=(0, n_a))[0]
    proj_b = _mm_nn(h1, w6, f"proj_b_{tag}", [F32], b_cols=(n_a, w6.shape[1] - n_a))[0]
    f_t = _mm_nt(wf_t, h1, f"fproj_{tag}", [F32])[0]
    qc, kc = _fox_bias_operands(_gates_fwd(f_t, p["forget_b"], f"gates_{tag}"), f"fox_operands_{tag}")
    y_a, lse_a = _fox_fwd(proj_a, qc, kc, Hf, f"fox_{tag}")
    y_b, lse_b = _dil_fwd(proj_b, bias, Hd, f"dil_{tag}")
    mixed = _pair_norm_fwd(y_a, y_b, p["outnorm_a_g"], p["outnorm_b_g"], f"norm_ab_{tag}")
    w_out = weight("w_out", mixed)
    w_out = w_out.reshape(-1, w_out.shape[2])
    x1 = _mm_nn(mixed, w_out, f"attn_out_{tag}", [F32], extras=[x])[0]
    h2 = _rms_fwd(x1, p["norm2_g"], f"norm2_{tag}")
    w_mi = weight("w_mlp_in", h2)
    u, act = _mm_nn(h2, w_mi, f"mlp_in_{tag}", [CDT, CDT], b_slots=True,
                    epi=lambda acc: (acc, jnp.square(jnp.maximum(acc, 0.0))))
    w_mo = weight("w_mlp_out", act)
    w_mo = w_mo.reshape(-1, w_mo.shape[2])
    x2 = _mm_nn(act, w_mo, f"mlp_out_{tag}", [F32], extras=[x1])[0]
    saved = dict(x=x, h1=h1, proj_a=proj_a, proj_b=proj_b, f_t=f_t, qc=qc, kc=kc, y_a=y_a, lse_a=lse_a, y_b=y_b,
                 lse_b=lse_b, mixed=mixed, x1=x1, h2=h2, u=u, act=act, w6=w6, wf_t=wf_t, w_out=w_out, w_mi=w_mi,
                 w_mo=w_mo)
    return x2, saved


def _layer_bwd(dx2, dx2c, p, send, bias, sv, defer_w_out, tag):
    Hf, Hd = p["forget_b"].shape[0], bias.shape[1]
    E = HEAD_DIM
    rows = lambda g: g.reshape(N_CHIPS, -1, g.shape[1])
    du = _mm_nt(dx2c, sv["w_mo"], f"d_act_{tag}", [CDT], extras=[sv["u"]],
                epi=lambda acc, u: (acc * (2.0 * jnp.maximum(u.astype(F32), 0.0)),))[0]
    tokens = send("w_mlp_out", rows(_mm_tn(sv["act"], dx2c, f"dw_mlp_out_{tag}", CDT)))
    dh2 = _mm_nt(du, sv["w_mi"], f"d_h2_{tag}", [F32], b_slots=True)[0]
    tokens = tokens + send("w_mlp_in", _mm_tn(sv["h2"], du, f"dw_mlp_in_{tag}", CDT, out_slots=N_CHIPS))
    dx1, dx1c, g_norm2 = _rms_bwd(sv["x1"], _tied(p["norm2_g"], tokens, f"tie_norm2_{tag}"), dh2, dx2,
                                  f"d_norm2_{tag}")
    dmixed = _mm_nt(dx1c, sv["w_out"], f"d_mixed_{tag}", [F32])[0]
    send_w_out = lambda: send("w_out", rows(_mm_tn(sv["mixed"], dx1c, f"dw_out_{tag}", CDT)))
    tokens = [] if defer_w_out else send_w_out()
    dy_a, dy_b, g_na, g_nb = _pair_norm_bwd(sv["y_a"], sv["y_b"], _tied(p["outnorm_a_g"], tokens, f"tie_norm_a_{tag}"),
                                            p["outnorm_b_g"], dmixed, f"d_norm_ab_{tag}")
    dq_a, dcq, dk_a, dv_a, dck = _fox_bwd(sv["proj_a"], sv["qc"], sv["kc"], sv["lse_a"], sv["y_a"], dy_a, Hf,
                                          f"fox_bwd_{tag}")
    df, dfc, g_fb = _gates_bwd(sv["f_t"], p["forget_b"], dcq[:, ::E].T, dck.reshape(Hf, -1), f"d_gates_{tag}")
    dq_b, dk_b, dv_b, dbias = _dil_bwd(sv["proj_b"], bias, sv["y_b"], dy_b, sv["lse_b"], Hd, f"dil_bwd_{tag}")
    dproj = jnp.concatenate([dq_a, dk_a, dv_a, dq_b, dk_b, dv_b], axis=1)
    g_w6 = _mm_tn(sv["h1"], dproj, f"dw_in_{tag}", CDT)
    g_wf_t = _mm_nn(dfc, sv["h1"], f"dw_f_{tag}", [F32])[0]
    tokens = send("w_in", _join_dw_in(g_w6, g_wf_t, Hf, f"join_dw_in_{tag}"))
    dh1_f = _mm_tn(dfc, _tied(sv["wf_t"], tokens, f"tie_wf_{tag}"), f"d_h1_f_{tag}", F32)
    dh1 = _mm_nt(dproj, sv["w6"], f"d_h1_{tag}", [F32], extras=[dh1_f])[0]
    dx, dxc, g_norm1 = _rms_bwd(sv["x"], p["norm1_g"], dh1, dx1, f"d_norm1_{tag}")
    grads = dict(norm1_g=g_norm1[0], norm2_g=g_norm2[0], outnorm_a_g=g_na[0], outnorm_b_g=g_nb[0],
                 forget_b=g_fb[:, 0], dbias=dbias)
    return dx, dxc, grads, (send_w_out if defer_w_out else None)


_LAYER_SMALL = ("norm1_g", "forget_b", "outnorm_a_g", "outnorm_b_g", "norm2_g")


def _local_step(x, target, small, weight, send, tokens):
    depth = small["norm1_g"].shape[0]
    buckets = _bucket_table()
    bias = _bias_table(small["rel_bias"], buckets, "bias_table")
    layers, saved = [], []
    for l in range(depth):
        p = {k: small[k][l] for k in _LAYER_SMALL}
        layers.append(p)
        x, sv = _layer_fwd(x, p, functools.partial(weight, l), bias, tokens if l == 0 else [], f"l{l}")
        saved.append(sv)
    dx, dxc, g_final, loss = _loss_bwd(x, small["final_norm_g"], target, "loss")
    layer_grads = [None] * depth
    for l in reversed(range(depth)):
        dx, dxc, layer_grads[l], last = _layer_bwd(dx, dxc, layers[l], functools.partial(send, l), bias, saved[l],
                                                   l == 0, f"l{l}")
    tokens = last()
    dbias = functools.reduce(jnp.add, [g["dbias"] for g in layer_grads])
    g_rel = _bias_table_bwd(dbias, buckets, "d_bias_table")[:, 0, :].T
    small_grads = dict(final_norm_g=g_final[0], rel_bias=g_rel,
                       **{k: jnp.stack([g[k] for g in layer_grads]) for k in _LAYER_SMALL})
    return loss[0, 0], dx, small_grads, tokens


_BIG = ("w_in", "w_out", "w_mlp_in", "w_mlp_out")
_SMALL = ("norm1_g", "forget_b", "rel_bias", "outnorm_a_g", "outnorm_b_g", "norm2_g", "final_norm_g")
_ORDER = ("norm1_g", "w_in", "forget_b", "rel_bias", "outnorm_a_g", "outnorm_b_g", "w_out", "norm2_g", "w_mlp_in",
          "w_mlp_out", "final_norm_g")


def _pack_small(d):
    flat = jnp.concatenate([d[k].reshape(-1) for k in _SMALL])
    rows = -(-flat.shape[0] // (8 * SMALL_COLS)) * 8
    return jnp.pad(flat, (0, rows * SMALL_COLS - flat.shape[0])).reshape(rows, SMALL_COLS)


def _unpack_small(packed, like):
    flat, out, at = packed.reshape(-1), {}, 0
    for k in _SMALL:
        n = like[k].size
        out[k] = flat[at:at + n].reshape(like[k].shape)
        at += n
    return out


def kernel(x, norm1_g, w_in, forget_b, rel_bias, outnorm_a_g, outnorm_b_g, w_out, norm2_g, w_mlp_in, w_mlp_out, final_norm_g, loss_target, m_norm1_g, m_w_in, m_forget_b, m_rel_bias, m_outnorm_a_g, m_outnorm_b_g, m_w_out, m_norm2_g, m_w_mlp_in, m_w_mlp_out, m_final_norm_g, v_norm1_g, v_w_in, v_forget_b, v_rel_bias, v_outnorm_a_g, v_outnorm_b_g, v_w_out, v_norm2_g, v_w_mlp_in, v_w_mlp_out, v_final_norm_g):
    w = dict(norm1_g=norm1_g, w_in=w_in, forget_b=forget_b, rel_bias=rel_bias, outnorm_a_g=outnorm_a_g,
             outnorm_b_g=outnorm_b_g, w_out=w_out, norm2_g=norm2_g, w_mlp_in=w_mlp_in, w_mlp_out=w_mlp_out,
             final_norm_g=final_norm_g)
    m = dict(norm1_g=m_norm1_g, w_in=m_w_in, forget_b=m_forget_b, rel_bias=m_rel_bias, outnorm_a_g=m_outnorm_a_g,
             outnorm_b_g=m_outnorm_b_g, w_out=m_w_out, norm2_g=m_norm2_g, w_mlp_in=m_w_mlp_in,
             w_mlp_out=m_w_mlp_out, final_norm_g=m_final_norm_g)
    v = dict(norm1_g=v_norm1_g, w_in=v_w_in, forget_b=v_forget_b, rel_bias=v_rel_bias, outnorm_a_g=v_outnorm_a_g,
             outnorm_b_g=v_outnorm_b_g, w_out=v_w_out, norm2_g=v_norm2_g, w_mlp_in=v_w_mlp_in,
             w_mlp_out=v_w_mlp_out, final_norm_g=v_final_norm_g)
    depth = w_in.shape[0]
    small = {k: w[k] for k in _SMALL}

    gathers, passed, tokens = {}, {}, []
    for l in range(depth):
        for k in _BIG:
            buf = _cast_into_slot(w[k], l, f"cast_{k}_l{l}")
            start = _relay_start if k == _BIG[0] else _gather_start
            send_sems, recv_sems, buf, token = start(buf, tokens, f"gather_start_{k}_l{l}")
            gathers[l, k], tokens = (send_sems, recv_sems, buf), [token]

    def weight(l, k, after):
        if k == _BIG[-1] and l + 1 < depth:
            passed[l + 1] = _relay_pass(*gathers[l + 1, _BIG[0]], after, f"gather_pass_{_BIG[0]}_l{l + 1}")
        if k != _BIG[0]:
            return _gather_wait(*gathers[l, k], after, f"gather_wait_{k}_l{l}")
        if l not in passed:
            passed[l] = _relay_pass(*gathers[l, k], after, f"gather_pass_{k}_l{l}")
        return _relay_wait(*passed[l], after, f"gather_wait_{k}_l{l}")

    scatters = {}

    def send(l, k, g):
        scatters[l, k] = _scatter_start(g, f"rs_start_{k}_l{l}")
        return [scatters[l, k][4]]

    loss, grad_x, small_grads, tokens = _local_step(x[0], loss_target[0], small, weight, send, tokens)
    loss = lax.psum(loss, ("x", "y", "c"))

    grads, delta, new_m, new_v = {}, {}, {}, {}
    packed = _tied(_pack_small(small_grads), tokens, "tie_small")
    after, seen, joining = packed, {k: 0 for k in _BIG}, None

    def joined(after):
        (l, k), (send_sem, recv_sem, g) = joining
        grads[k] = _join_wait(send_sem, recv_sem, g, l, after, f"rs_join_wait_{k}_l{l}")
        seen[k] += 1
        if seen[k] < depth:
            return after
        shape = w[k].shape
        flat = lambda t: t.reshape(-1, shape[-1])
        outs = _adamw(flat(w[k]), flat(grads[k]), flat(m[k]), flat(v[k]), f"adamw_{k}")
        grads[k], delta[k], new_m[k], new_v[k] = (t.reshape(shape) for t in outs)
        return outs[1]

    for (l, k), started in scatters.items():
        assert joining is None or joining[0][1] != k
        send_sem, recv_sem, g, token = _reduce_scatter_sum(started, after, l, depth, grads.get(k), f"{k}_l{l}")
        if joining is not None:
            after = joined(token)
        joining = ((l, k), (send_sem, recv_sem, g))
    after = joined(after)
    small_sums = _all_reduce_small(_tied(packed, [after], "tie_small_sums"), "small_all_reduce")
    grads.update(_unpack_small(small_sums, small))
    _, d_, m_, v_ = _adamw(_pack_small(small), _pack_small({k: grads[k] for k in _SMALL}),
                           _pack_small({k: m[k] for k in _SMALL}), _pack_small({k: v[k] for k in _SMALL}), "adamw_small")
    delta.update(_unpack_small(d_, small))
    new_m.update(_unpack_small(m_, small))
    new_v.update(_unpack_small(v_, small))

    return (loss, grad_x[None], *[grads[k] for k in _ORDER], *[delta[k] for k in _ORDER],
            *[new_m[k] for k in _ORDER], *[new_v[k] for k in _ORDER])
```

```python
import functools

import jax
import jax.numpy as jnp
from jax import lax
from jax.experimental import pallas as pl
from jax.experimental.pallas import tpu as pltpu

F32 = jnp.float32
CDT = jnp.bfloat16
HEAD_DIM = 128
NORM_EPS = 1e-6
NEG_INF = -1e30
LOG2E = 1.4426950408889634
REL_BUCKETS = 32
REL_MAX_DISTANCE = 2048
DIL_PATTERNS = ((128, 1), (512, 4), (2048, 16))
DIL_BLOCK = 128
ADAM_LR, ADAM_B1, ADAM_B2, ADAM_EPS, ADAM_WD, ADAM_STEP = 0.001, 0.9, 0.999, 1e-08, 0.01, 10
N_CHIPS = 4
N_DEV = 8
VMEM_LIMIT_BYTES = 56 * 1024 * 1024
SMALL_COLS = 1024
MESH = pl.DeviceIdType.MESH


def _cparams(sem=None):
    return pltpu.CompilerParams(dimension_semantics=sem, vmem_limit_bytes=VMEM_LIMIT_BYTES)


def _tile(dim, pref):
    t = min(pref, dim)
    t -= t % 128
    while t >= 128:
        if dim % t == 0:
            return t
        t -= 128
    return dim


def _rowwise(fn, ins, out_dtypes, name, bs=256, consts=()):
    R, C = ins[0].shape
    bs = min(bs, R)
    n_in, n_c = len(ins), len(consts)

    def body(*refs):
        vals = [r[...] for r in refs[:n_in + n_c]]
        res = fn(*vals)
        for o, r in zip(refs[n_in + n_c:], res):
            o[...] = r.astype(o.dtype)

    row = pl.BlockSpec((bs, C), lambda i: (i, 0))
    return pl.pallas_call(
        body, grid=(R // bs,),
        in_specs=[row] * n_in + [pl.BlockSpec((1, c.shape[-1]), lambda i: (0, 0)) for c in consts],
        out_specs=[row] * len(out_dtypes),
        out_shape=[jax.ShapeDtypeStruct((R, C), d) for d in out_dtypes],
        name=name, compiler_params=_cparams(("parallel",)),
    )(*ins, *[c.reshape(1, -1) for c in consts])


def _rms_fwd(x, g, name):
    def fn(xf, gg):
        r = lax.rsqrt(jnp.mean(xf * xf, axis=-1, keepdims=True) + NORM_EPS)
        return ((xf * r) * gg,)
    return _rowwise(fn, [x], [CDT], name, consts=[g])[0]


def _rms_bwd(x, g, dh, dres, name, bs=256):
    S, D = x.shape
    bs = min(bs, S)
    has_res = dres is not None

    def body(*refs):
        x_ref, g_ref, dh_ref = refs[:3]
        dx_ref, dxc_ref, dg_ref = refs[-3:]
        xf = x_ref[...]
        r = lax.rsqrt(jnp.mean(xf * xf, axis=-1, keepdims=True) + NORM_EPS)
        xhat = xf * r
        dh_ = dh_ref[...].astype(F32)
        dxhat = dh_ * g_ref[...]
        dx = r * (dxhat - xhat * jnp.mean(dxhat * xhat, axis=-1, keepdims=True))
        if has_res:
            dx = dx + refs[3][...]
        dx_ref[...] = dx
        dxc_ref[...] = dx.astype(dxc_ref.dtype)
        part = jnp.sum(dh_ * xhat, axis=0, keepdims=True)

        @pl.when(pl.program_id(0) == 0)
        def _():
            dg_ref[...] = part

        @pl.when(pl.program_id(0) > 0)
        def _():
            dg_ref[...] += part

    row = pl.BlockSpec((bs, D), lambda i: (i, 0))
    one = pl.BlockSpec((1, D), lambda i: (0, 0))
    ins = [x, g.reshape(1, D), dh] + ([dres] if has_res else [])
    return pl.pallas_call(
        body, grid=(S // bs,),
        in_specs=[row, one, row] + ([row] if has_res else []),
        out_specs=[row, row, one],
        out_shape=[jax.ShapeDtypeStruct((S, D), F32), jax.ShapeDtypeStruct((S, D), CDT),
                   jax.ShapeDtypeStruct((1, D), F32)],
        name=name, compiler_params=_cparams(("arbitrary",)),
    )(*ins)


def _pair_norm_fwd(y_a, y_b, g_a, g_b, name, bs=256):
    S, Da = y_a.shape
    Db = y_b.shape[1]
    bs = min(bs, S)

    def body(a_ref, b_ref, ga_ref, gb_ref, o_ref):
        def norm(x, g):
            r = lax.rsqrt(jnp.mean(x * x, axis=-1, keepdims=True) + NORM_EPS)
            return ((x * r) * g).astype(o_ref.dtype)
        o_ref[:, :Da] = norm(a_ref[...], ga_ref[...])
        o_ref[:, Da:] = norm(b_ref[...], gb_ref[...])

    row = lambda n: pl.BlockSpec((bs, n), lambda i: (i, 0))
    one = lambda n: pl.BlockSpec((1, n), lambda i: (0, 0))
    return pl.pallas_call(
        body, grid=(S // bs,), in_specs=[row(Da), row(Db), one(Da), one(Db)], out_specs=row(Da + Db),
        out_shape=jax.ShapeDtypeStruct((S, Da + Db), CDT), name=name, compiler_params=_cparams(("parallel",)),
    )(y_a, y_b, g_a.reshape(1, Da), g_b.reshape(1, Db))


def _pair_norm_bwd(y_a, y_b, g_a, g_b, dmixed, name, bs=256):
    S, Da = y_a.shape
    Db = y_b.shape[1]
    bs = min(bs, S)

    def body(a_ref, b_ref, ga_ref, gb_ref, dm_ref, da_ref, db_ref, dga_ref, dgb_ref):
        def one(x_ref, g_ref, dh, dx_ref, dg_ref):
            xf = x_ref[...]
            r = lax.rsqrt(jnp.mean(xf * xf, axis=-1, keepdims=True) + NORM_EPS)
            xhat = xf * r
            dxhat = dh * g_ref[...]
            dx_ref[...] = r * (dxhat - xhat * jnp.mean(dxhat * xhat, axis=-1, keepdims=True))
            part = jnp.sum(dh * xhat, axis=0, keepdims=True)

            @pl.when(pl.program_id(0) == 0)
            def _():
                dg_ref[...] = part

            @pl.when(pl.program_id(0) > 0)
            def _():
                dg_ref[...] += part

        dm = dm_ref[...]
        one(a_ref, ga_ref, dm[:, :Da], da_ref, dga_ref)
        one(b_ref, gb_ref, dm[:, Da:], db_ref, dgb_ref)

    row = lambda n: pl.BlockSpec((bs, n), lambda i: (i, 0))
    one_ = lambda n: pl.BlockSpec((1, n), lambda i: (0, 0))
    return pl.pallas_call(
        body, grid=(S // bs,), in_specs=[row(Da), row(Db), one_(Da), one_(Db), row(Da + Db)],
        out_specs=[row(Da), row(Db), one_(Da), one_(Db)],
        out_shape=[jax.ShapeDtypeStruct((S, Da), F32), jax.ShapeDtypeStruct((S, Db), F32),
                   jax.ShapeDtypeStruct((1, Da), F32), jax.ShapeDtypeStruct((1, Db), F32)],
        name=name, compiler_params=_cparams(("arbitrary",)),
    )(y_a, y_b, g_a.reshape(1, Da), g_b.reshape(1, Db), dmixed)


def _loss_bwd(x, g, target, name, bs=256):
    S, D = x.shape
    bs = min(bs, S)

    def body(x_ref, g_ref, t_ref, dx_ref, dxc_ref, dg_ref, loss_ref):
        xf = x_ref[...]
        r = lax.rsqrt(jnp.mean(xf * xf, axis=-1, keepdims=True) + NORM_EPS)
        xhat = xf * r
        err = xhat * g_ref[...] - t_ref[...]
        lpart = 0.5 * jnp.sum(jnp.mean(err * err, axis=-1, keepdims=True), axis=0, keepdims=True)
        dy = err / D
        dxhat = dy * g_ref[...]
        dx = r * (dxhat - xhat * jnp.mean(dxhat * xhat, axis=-1, keepdims=True))
        dx_ref[...] = dx
        dxc_ref[...] = dx.astype(dxc_ref.dtype)
        gpart = jnp.sum(dy * xhat, axis=0, keepdims=True)

        @pl.when(pl.program_id(0) == 0)
        def _():
            dg_ref[...] = gpart
            loss_ref[...] = lpart

        @pl.when(pl.program_id(0) > 0)
        def _():
            dg_ref[...] += gpart
            loss_ref[...] += lpart

    row = pl.BlockSpec((bs, D), lambda i: (i, 0))
    one = pl.BlockSpec((1, D), lambda i: (0, 0))
    return pl.pallas_call(
        body, grid=(S // bs,),
        in_specs=[row, one, row],
        out_specs=[row, row, one, pl.BlockSpec((1, 1), lambda i: (0, 0))],
        out_shape=[jax.ShapeDtypeStruct((S, D), F32), jax.ShapeDtypeStruct((S, D), CDT),
                   jax.ShapeDtypeStruct((1, D), F32), jax.ShapeDtypeStruct((1, 1), F32)],
        name=name, compiler_params=_cparams(("arbitrary",)),
    )(x, g.reshape(1, D), target)


_NN = (((1,), (0,)), ((), ()))
_NT = (((1,), (1,)), ((), ()))
_TN = (((0,), (0,)), ((), ()))


def _mm(a, b, *, M, N, K, a_spec, b_spec, o_spec, dims, tm, tn, tk, name, out_shapes, extras=(), epi=None):
    nk = K // tk
    n_ex, n_out = len(extras), len(out_shapes)
    in_place = epi is None
    if in_place:
        assert n_out == 1 and n_ex <= 1 and out_shapes[0].dtype == F32
        epi = lambda acc, *r: (acc + r[0] if r else acc,)

    def body(*refs):
        a_ref, b_ref = refs[0], refs[1]
        ex = refs[2:2 + n_ex]
        outs = refs[2 + n_ex:2 + n_ex + n_out]
        part = lax.dot_general(a_ref[...], b_ref[...], dims, preferred_element_type=F32)

        def finish(acc):
            for o, r in zip(outs, epi(acc, *[e[...] for e in ex])):
                o[...] = r.astype(o.dtype)

        if nk == 1:
            finish(part)
        elif in_place:
            k = pl.program_id(2)

            @pl.when(k == 0)
            def _():
                finish(part)

            @pl.when(k > 0)
            def _():
                outs[0][...] += part
        else:
            acc_ref = refs[-1]
            k = pl.program_id(2)

            @pl.when(k == 0)
            def _():
                acc_ref[...] = part

            @pl.when(k > 0)
            def _():
                acc_ref[...] += part

            @pl.when(k == nk - 1)
            def _():
                finish(acc_ref[...])

    ex_spec = pl.BlockSpec((tm, tn), lambda i, j, k: (i, j))
    return pl.pallas_call(
        body, grid=(M // tm, N // tn, nk),
        in_specs=[a_spec, b_spec] + [ex_spec] * n_ex,
        out_specs=[o_spec] * n_out,
        out_shape=out_shapes,
        scratch_shapes=[pltpu.VMEM((tm, tn), F32)] if nk > 1 and not in_place else [],
        name=name, compiler_params=_cparams(("parallel", "parallel", "arbitrary")),
    )(a, b, *extras)


def _mm_tiles(K):
    return (2048, 512, 2048) if K <= 2048 else (1024, 1024, 2048)


def _mm_nn(a, b, name, out_dtypes, extras=(), epi=None, b_slots=False, b_cols=None):
    M, K = a.shape
    tm, tn, tk = _mm_tiles(K)
    if b_slots:
        ns, _, Ns = b.shape
        N = ns * Ns
        tn = _tile(Ns, tn)
        npb = Ns // tn
        tk_ = _tile(K, tk)
        b_spec = pl.BlockSpec((None, tk_, tn), lambda i, j, k: (j // npb, k, j % npb))
    else:
        first, N = b_cols if b_cols is not None else (0, b.shape[1])
        tn = _tile(N, tn)
        assert first % tn == 0
        tk_ = _tile(K, tk)
        b_spec = pl.BlockSpec((tk_, tn), lambda i, j, k: (k, first // tn + j))
    tm = _tile(M, tm)
    return _mm(a, b, M=M, N=N, K=K, a_spec=pl.BlockSpec((tm, tk_), lambda i, j, k: (i, k)), b_spec=b_spec,
               o_spec=pl.BlockSpec((tm, tn), lambda i, j, k: (i, j)), dims=_NN, tm=tm, tn=tn, tk=tk_, name=name,
               out_shapes=[jax.ShapeDtypeStruct((M, N), d) for d in out_dtypes], extras=extras, epi=epi)


def _mm_nt(a, b, name, out_dtypes, extras=(), epi=None, b_slots=False):
    M, K = a.shape
    tm, tn, tk = _mm_tiles(K)
    tm = _tile(M, tm)
    if b_slots:
        ns, N, Ks = b.shape
        tk_ = _tile(Ks, tk)
        kpb = Ks // tk_
        tn = _tile(N, tn)
        b_spec = pl.BlockSpec((None, tn, tk_), lambda i, j, k: (k // kpb, j, k % kpb))
    else:
        N = b.shape[0]
        tk_ = _tile(K, tk)
        tn = _tile(N, tn)
        b_spec = pl.BlockSpec((tn, tk_), lambda i, j, k: (j, k))
    return _mm(a, b, M=M, N=N, K=K, a_spec=pl.BlockSpec((tm, tk_), lambda i, j, k: (i, k)), b_spec=b_spec,
               o_spec=pl.BlockSpec((tm, tn), lambda i, j, k: (i, j)), dims=_NT, tm=tm, tn=tn, tk=tk_, name=name,
               out_shapes=[jax.ShapeDtypeStruct((M, N), d) for d in out_dtypes], extras=extras, epi=epi)


def _mm_tn(a, b, name, out_dtype, out_slots=0, tm=2048, tn=1024, tk=2048):
    K, M = a.shape
    N = b.shape[1]
    tm, tk_ = _tile(M, tm), _tile(K, tk)
    if out_slots:
        Ns = N // out_slots
        tn = _tile(Ns, tn)
        npb = Ns // tn
        o_spec = pl.BlockSpec((None, tm, tn), lambda i, j, k: (j // npb, i, j % npb))
        out_shape = jax.ShapeDtypeStruct((out_slots, M, Ns), out_dtype)
    else:
        tn = _tile(N, tn)
        o_spec = pl.BlockSpec((tm, tn), lambda i, j, k: (i, j))
        out_shape = jax.ShapeDtypeStruct((M, N), out_dtype)
    return _mm(a, b, M=M, N=N, K=K, a_spec=pl.BlockSpec((tk_, tm), lambda i, j, k: (k, i)),
               b_spec=pl.BlockSpec((tk_, tn), lambda i, j, k: (k, j)), o_spec=o_spec, dims=_TN,
               tm=tm, tn=tn, tk=tk_, name=name, out_shapes=[out_shape],
               epi=None if out_dtype == F32 else (lambda acc: (acc,)))[0]


GATE_BLOCK = 512


def _split3(v):
    hi = v.astype(jnp.bfloat16)
    r1 = v - hi.astype(F32)
    mid = r1.astype(jnp.bfloat16)
    lo = (r1 - mid.astype(F32)).astype(jnp.bfloat16)
    return hi, mid, lo


def _exact_dot(v, tri):
    return functools.reduce(jnp.add, [jnp.dot(t, tri, preferred_element_type=F32) for t in _split3(v)])


def _gates_fwd(f_t, b, name):
    H, S = f_t.shape
    nb = _tile(S, GATE_BLOCK)
    inv_scale = HEAD_DIM ** 0.5

    def body(f_ref, b_ref, c_ref):
        upper = (lax.broadcasted_iota(jnp.int32, (nb, nb), 0)
                 <= lax.broadcasted_iota(jnp.int32, (nb, nb), 1)).astype(jnp.bfloat16)
        carry = jnp.zeros((H, 1), F32)
        for i in range(S // nb):
            z = f_ref[:, i * nb:(i + 1) * nb] + b_ref[...]
            logf = jnp.minimum(z, 0.0) - jnp.log1p(jnp.exp(-jnp.abs(z)))
            cs = _exact_dot(logf, upper) + carry
            for j, t in enumerate(_split3(cs * inv_scale)):
                c_ref[j, :, i * nb:(i + 1) * nb] = t
            carry = cs[:, nb - 1:nb]

    return pl.pallas_call(body, out_shape=jax.ShapeDtypeStruct((3, H, S), jnp.bfloat16), name=name,
                          compiler_params=_cparams())(f_t, b.reshape(H, 1))


def _gates_bwd(f_t, b, dcq, dck, name):
    H, S = f_t.shape
    nb = _tile(S, GATE_BLOCK)

    def body(f_ref, b_ref, dcq_ref, dck_ref, df_ref, dfc_ref, db_ref):
        lower = (lax.broadcasted_iota(jnp.int32, (nb, nb), 0)
                 >= lax.broadcasted_iota(jnp.int32, (nb, nb), 1)).astype(jnp.bfloat16)
        carry = jnp.zeros((H, 1), F32)
        db = jnp.zeros((H, 1), F32)
        for i in reversed(range(S // nb)):
            sl = slice(i * nb, (i + 1) * nb)
            dc = dcq_ref[:, sl] - dck_ref[:, sl]
            dlogf = _exact_dot(dc, lower) + carry
            carry = dlogf[:, 0:1]
            z = f_ref[:, sl] + b_ref[...]
            df = dlogf / (1.0 + jnp.exp(z))
            df_ref[:, sl] = df
            dfc_ref[:, sl] = df.astype(dfc_ref.dtype)
            db = db + jnp.sum(df, axis=1, keepdims=True)
        db_ref[...] = db

    return pl.pallas_call(
        body, out_shape=[jax.ShapeDtypeStruct((H, S), F32), jax.ShapeDtypeStruct((H, S), CDT),
                         jax.ShapeDtypeStruct((H, 1), F32)],
        name=name, compiler_params=_cparams())(f_t, b.reshape(H, 1), dcq, dck)


FOX_BLOCK = 1024


def _fox_bias_operands(csplit, name, bs=512):
    _, H, S = csplit.shape
    E = HEAD_DIM
    bs = _tile(S, bs)
    part = jnp.arange(3 * H)[:, None] // H
    head = jnp.arange(3 * H)[:, None] % H
    lane = jnp.arange(H * E)[None, :]
    place_q = (lane == head * E + part).astype(csplit.dtype)
    place_k = -(lane == head * E + 3 + part).astype(csplit.dtype)
    ones_q = ((lane % E >= 3) & (lane % E < 6)).astype(F32)
    ones_k = (lane % E < 3).astype(F32)

    def body(c_ref, pq_ref, pk_ref, oq_ref, ok_ref, qc_ref, kc_ref):
        c = c_ref[...]
        qc_ref[...] = (lax.dot_general(c, pq_ref[...], _TN, preferred_element_type=F32) + oq_ref[...]).astype(qc_ref.dtype)
        kc_ref[...] = (lax.dot_general(c, pk_ref[...], _TN, preferred_element_type=F32) + ok_ref[...]).astype(kc_ref.dtype)

    full = lambda a: pl.BlockSpec(a.shape, lambda i: (0, 0))
    out = pl.BlockSpec((bs, H * E), lambda i: (i, 0))
    return pl.pallas_call(
        body, grid=(S // bs,),
        in_specs=[pl.BlockSpec((3 * H, bs), lambda i: (0, i)), full(place_q), full(place_k), full(ones_q), full(ones_k)],
        out_specs=[out, out], out_shape=[jax.ShapeDtypeStruct((S, H * E), csplit.dtype)] * 2,
        name=name, compiler_params=_cparams(("parallel",)),
    )(csplit.reshape(3 * H, S), place_q, place_k, ones_q, ones_k)


def _fox_logits2(q_ref, qc_ref, k_ref, kc_ref, diag):
    q, k = q_ref[...], k_ref[...]
    qa = jnp.concatenate([q, qc_ref[...].astype(q.dtype)], axis=1)
    ka = jnp.concatenate([k, kc_ref[...].astype(k.dtype)], axis=1)
    s = lax.dot_general(qa, ka, _NT, preferred_element_type=F32) * (HEAD_DIM ** -0.5 * LOG2E)
    if diag:
        row = lax.broadcasted_iota(jnp.int32, s.shape, 0)
        col = lax.broadcasted_iota(jnp.int32, s.shape, 1)
        s = jnp.where(col <= row, s, NEG_INF)
    return s


def _fox_fwd(proj, qc, kc, H, name):
    S = proj.shape[0]
    E = HEAD_DIM
    blk = _tile(S, FOX_BLOCK)
    nq = S // blk

    def pair(t):
        qi = sum((t >= i * (i + 1) // 2).astype(jnp.int32) for i in range(1, nq)) if nq > 1 else 0 * t
        return qi, t - qi * (qi + 1) // 2

    def body(q_ref, qc_ref, k_ref, kc_ref, v_ref, o_ref, lse_ref, m_s, l_s, acc_s):
        qi, kj = pair(pl.program_id(1))

        @pl.when(kj == 0)
        def _():
            m_s[...] = jnp.full(m_s.shape, NEG_INF, F32)
            l_s[...] = jnp.zeros(l_s.shape, F32)
            acc_s[...] = jnp.zeros(acc_s.shape, F32)

        def step(diag):
            s = _fox_logits2(q_ref, qc_ref, k_ref, kc_ref, diag)
            m_prev = m_s[...]
            m_new = jnp.maximum(m_prev, jnp.max(s, axis=-1, keepdims=True))
            alpha = jnp.exp2(m_prev - m_new)
            p = jnp.exp2(s - m_new)
            l_s[...] = alpha * l_s[...] + jnp.sum(p, axis=-1, keepdims=True)
            acc_s[...] = alpha * acc_s[...] + jnp.dot(p.astype(CDT), v_ref[...], preferred_element_type=F32)
            m_s[...] = m_new

        pl.when(kj < qi)(lambda: step(False))
        pl.when(kj == qi)(lambda: step(True))

        @pl.when(kj == qi)
        def _():
            o_ref[...] = acc_s[...] / l_s[...]
            lse_ref[...] = jnp.broadcast_to(m_s[...] + jnp.log2(l_s[...]), lse_ref.shape)

    qspec = lambda off: pl.BlockSpec((blk, E), lambda h, t: (pair(t)[0], off + h))
    kspec = lambda off: pl.BlockSpec((blk, E), lambda h, t: (pair(t)[1], off + h))
    return pl.pallas_call(
        body, grid=(H, nq * (nq + 1) // 2),
        in_specs=[qspec(0), qspec(0), kspec(H), kspec(0), kspec(2 * H)],
        out_specs=[qspec(0)] * 2,
        out_shape=[jax.ShapeDtypeStruct((S, H * E), F32)] * 2,
        scratch_shapes=[pltpu.VMEM((blk, 1), F32), pltpu.VMEM((blk, 1), F32), pltpu.VMEM((blk, E), F32)],
        name=name, compiler_params=_cparams(("parallel", "arbitrary")),
    )(proj, qc, proj, kc, proj)


def _fox_bwd(proj, qc, kc, lse, o, do, H, name):
    S = proj.shape[0]
    E = HEAD_DIM
    blk = _tile(S, FOX_BLOCK)
    nq = S // blk
    scale = E ** -0.5

    def pair(t):
        first = lambda j: j * nq - j * (j - 1) // 2
        kj = sum((t >= first(j)).astype(jnp.int32) for j in range(1, nq)) if nq > 1 else 0 * t
        return kj, kj + t - first(kj)

    def body(q_ref, qc_ref, k_ref, kc_ref, v_ref, lse_ref, o_ref, do_ref,
             dq_ref, dcq_ref, dk_ref, dv_ref, dck_ref, dq_s, dcq_s, dk_s, dv_s, dck_s):
        kj, qi = pair(pl.program_id(1))

        @pl.when(qi == kj)
        def _():
            dk_s[...] = jnp.zeros(dk_s.shape, F32)
            dv_s[...] = jnp.zeros(dv_s.shape, F32)
            dck_s[...] = jnp.zeros(dck_s.shape, F32)

        def step(diag):
            do = do_ref[...]
            doc = do.astype(CDT)
            delta = jnp.sum(do * o_ref[...], axis=-1, keepdims=True)
            p = jnp.exp2(_fox_logits2(q_ref, qc_ref, k_ref, kc_ref, diag) - lse_ref[:, 0:1])
            dp = lax.dot_general(doc, v_ref[...], _NT, preferred_element_type=F32)
            ds = p * (dp - delta)
            dss = ds * scale
            dck_s[...] += jnp.sum(ds, axis=0, keepdims=True)
            dv_s[...] += jnp.dot(p.T.astype(CDT), doc, preferred_element_type=F32)
            dk_s[...] += jnp.dot(dss.T.astype(CDT), q_ref[...], preferred_element_type=F32)
            dq_part = jnp.dot(dss.astype(CDT), k_ref[...], preferred_element_type=F32)
            dc_part = jnp.sum(ds, axis=-1, keepdims=True)
            rows = pl.ds(pl.multiple_of(qi * blk, blk), blk)

            @pl.when(kj == 0)
            def _():
                dq_s[rows, :] = dq_part
                dcq_s[rows, :] = dc_part

            @pl.when(kj > 0)
            def _():
                dq_s[rows, :] += dq_part
                dcq_s[rows, :] += dc_part

        pl.when(qi > kj)(lambda: step(False))
        pl.when(qi == kj)(lambda: step(True))

        @pl.when(qi == nq - 1)
        def _():
            dk_ref[...] = dk_s[...].astype(dk_ref.dtype)
            dv_ref[...] = dv_s[...].astype(dv_ref.dtype)
            dck_ref[...] = dck_s[...].reshape(dck_ref.shape)

        @pl.when((qi == nq - 1) & (kj == nq - 1))
        def _():
            dq_ref[...] = dq_s[...].astype(dq_ref.dtype)
            dcq_ref[...] = jnp.broadcast_to(dcq_s[...], dcq_ref.shape)

    qspec = lambda off: pl.BlockSpec((blk, E), lambda h, t: (pair(t)[1], off + h))
    kspec = lambda off: pl.BlockSpec((blk, E), lambda h, t: (pair(t)[0], off + h))
    head = pl.BlockSpec((S, E), lambda h, t: (0, h))
    return pl.pallas_call(
        body, grid=(H, nq * (nq + 1) // 2),
        in_specs=[qspec(0), qspec(0), kspec(H), kspec(0), kspec(2 * H), qspec(0), qspec(0), qspec(0)],
        out_specs=[head, head, kspec(0), kspec(0), pl.BlockSpec((1, 1, blk), lambda h, t: (h, 0, pair(t)[0]))],
        out_shape=[jax.ShapeDtypeStruct((S, H * E), CDT), jax.ShapeDtypeStruct((S, H * E), F32),
                   jax.ShapeDtypeStruct((S, H * E), CDT), jax.ShapeDtypeStruct((S, H * E), CDT),
                   jax.ShapeDtypeStruct((H, 1, S), F32)],
        scratch_shapes=[pltpu.VMEM((S, E), F32), pltpu.VMEM((S, 1), F32), pltpu.VMEM((blk, E), F32),
                        pltpu.VMEM((blk, E), F32), pltpu.VMEM((1, blk), F32)],
        name=name, compiler_params=_cparams(("parallel", "arbitrary")),
    )(proj, qc, proj, kc, proj, lse, o, do)


DIL_SLAB = 16 * DIL_BLOCK


def _rel_bucket(dist):
    max_exact = REL_BUCKETS // 2
    d = jnp.maximum(dist.astype(F32), 1.0)
    large = max_exact + (jnp.log(d / max_exact) / jnp.log(jnp.float32(REL_MAX_DISTANCE / max_exact))
                         * (REL_BUCKETS - max_exact)).astype(jnp.int32)
    large = jnp.minimum(large, REL_BUCKETS - 1)
    return jnp.where(dist < max_exact, dist, large)


def _bucket_table():
    i = jnp.arange(DIL_BLOCK)[:, None]
    j = jnp.arange(2 * DIL_BLOCK)[None, :]
    rel = DIL_BLOCK + i - j
    tabs = [_rel_bucket(jnp.clip(rel, 0, w // d) * d) for w, d in DIL_PATTERNS]
    return jnp.stack(tabs).astype(jnp.int32)


def _bias_table(rel_bias, buckets, name):
    P = buckets.shape[0]
    H = rel_bias.shape[1]

    def body(rb_ref, bk_ref, out_ref):
        h = pl.program_id(1)
        bk = bk_ref[0]
        val = jnp.zeros(bk.shape, F32)
        for b in range(REL_BUCKETS):
            val = jnp.where(bk == b, rb_ref[b, h], val)
        out_ref[0, 0] = val

    return pl.pallas_call(
        body, grid=(P, H),
        in_specs=[pl.BlockSpec(memory_space=pltpu.SMEM),
                  pl.BlockSpec((1, DIL_BLOCK, 2 * DIL_BLOCK), lambda p, h: (p, 0, 0))],
        out_specs=pl.BlockSpec((1, 1, DIL_BLOCK, 2 * DIL_BLOCK), lambda p, h: (p, h, 0, 0)),
        out_shape=jax.ShapeDtypeStruct((P, H, DIL_BLOCK, 2 * DIL_BLOCK), F32),
        name=name, compiler_params=_cparams(("parallel", "parallel")),
    )(rel_bias, buckets)


def _bias_table_bwd(dbias, buckets, name):
    P, H = dbias.shape[:2]

    def body(db_ref, bk_ref, out_ref):
        lane = lax.broadcasted_iota(jnp.int32, (1, REL_BUCKETS), 1)
        acc = jnp.zeros((1, REL_BUCKETS), F32)
        bk = bk_ref[...]
        db = db_ref[:, 0]
        for b in range(REL_BUCKETS):
            tot = jnp.sum(jnp.where(bk == b, db, 0.0))
            acc = jnp.where(lane == b, tot, acc)
        out_ref[0] = acc

    return pl.pallas_call(
        body, grid=(H,),
        in_specs=[pl.BlockSpec((P, 1, DIL_BLOCK, 2 * DIL_BLOCK), lambda h: (0, h, 0, 0)),
                  pl.BlockSpec((P, DIL_BLOCK, 2 * DIL_BLOCK), lambda h: (0, 0, 0))],
        out_specs=pl.BlockSpec((1, 1, REL_BUCKETS), lambda h: (h, 0, 0)),
        out_shape=jax.ShapeDtypeStruct((H, 1, REL_BUCKETS), F32),
        name=name, compiler_params=_cparams(("parallel",)),
    )(dbias, buckets)


def _bdot(a, b, contract_b):
    return lax.dot_general(a, b, (((2,), (contract_b,)), ((0,), (0,))), preferred_element_type=F32)


def _dil_units(d):
    return [(u // d, (u // d) * (DIL_BLOCK * d) + u % d) for u in range(DIL_SLAB // DIL_BLOCK)]


def _dil_rows(ref, starts, d, dtype=None):
    t = jnp.stack([ref[pl.ds(s, DIL_BLOCK, stride=d), :] for s in starts])
    return t if dtype is None else t.astype(dtype)


def _dil_keys(before_ref, ref, units, d):
    B, SL = DIL_BLOCK, DIL_SLAB

    def one(sg, b):
        before = ref[pl.ds(b - B * d, B, stride=d), :] if sg else before_ref[pl.ds(SL + b - B * d, B, stride=d), :]
        return jnp.concatenate([before, ref[pl.ds(b, B, stride=d), :]], axis=0)

    return jnp.stack([one(sg, b) for sg, b in units]).astype(CDT)


def _dil_logits(q, keys, bias_pc, d, has_before):
    T, B = q.shape[0], DIL_BLOCK
    ii = lax.broadcasted_iota(jnp.int32, (T, B, 2 * B), 1)
    jj = lax.broadcasted_iota(jnp.int32, (T, B, 2 * B), 2)
    sg = lax.broadcasted_iota(jnp.int32, (T, B, 2 * B), 0) // d
    mask = (jj >= ii) & (jj <= ii + B) & ((jj >= B) | (sg > 0) | has_before)
    return jnp.where(mask, _bdot(q, keys, 2) * HEAD_DIM ** -0.5 + bias_pc[None], NEG_INF)


def _dil_specs(H):
    E, SL = DIL_BLOCK, DIL_SLAB
    cur = lambda off: pl.BlockSpec((SL, E), lambda h, g: (g, off + h))
    prev = lambda off: pl.BlockSpec((SL, E), lambda h, g: (jnp.maximum(g - 1, 0), off + h))
    bias = pl.BlockSpec((len(DIL_PATTERNS), 1, E, 2 * E), lambda h, g: (0, h, 0, 0))
    return cur, prev, bias


def _dil_fwd(proj, bias, H, name):
    S = proj.shape[0]
    E = B = DIL_BLOCK
    SL = DIL_SLAB
    P = len(DIL_PATTERNS)
    assert S % SL == 0
    n_slabs = S // SL

    def body(q_ref, kc_ref, kp_ref, vc_ref, vp_ref, b_ref, y_ref, lse_ref, o_s, l_s):
        g = pl.program_id(1)
        for p, (_, d) in enumerate(DIL_PATTERNS):
            units = _dil_units(d)
            q = _dil_rows(q_ref, [b for _, b in units], d, CDT)
            s = _dil_logits(q, _dil_keys(kp_ref, kc_ref, units, d), b_ref[p, 0], d, g > 0)
            m = jnp.max(s, axis=-1, keepdims=True)
            e = jnp.exp(s - m)
            ssum = jnp.sum(e, axis=-1, keepdims=True)
            o = _bdot(e.astype(CDT), _dil_keys(vp_ref, vc_ref, units, d), 1) / ssum
            lse = jnp.broadcast_to(m + jnp.log(ssum), o.shape)
            for t, (_, b) in enumerate(units):
                o_s[p, pl.ds(b, B, stride=d), :] = o[t]
                l_s[p, pl.ds(b, B, stride=d), :] = lse[t]
        ls = [l_s[p] for p in range(P)]
        m = functools.reduce(jnp.maximum, ls)
        w = [jnp.exp(l - m) for l in ls]
        tot = functools.reduce(jnp.add, w)
        y_ref[...] = functools.reduce(jnp.add, [(w[p] / tot) * o_s[p] for p in range(P)])
        lse_ref[...] = m + jnp.log(tot)

    cur, prev, bspec = _dil_specs(H)
    return pl.pallas_call(
        body, grid=(H, n_slabs),
        in_specs=[cur(0), cur(H), prev(H), cur(2 * H), prev(2 * H), bspec],
        out_specs=[cur(0), cur(0)],
        out_shape=[jax.ShapeDtypeStruct((S, H * E), F32)] * 2,
        scratch_shapes=[pltpu.VMEM((P, SL, E), F32), pltpu.VMEM((P, SL, E), F32)],
        name=name, compiler_params=_cparams(("parallel", "parallel")),
    )(proj, proj, proj, proj, proj, bias)


def _dil_bwd(proj, bias, y, dy, lse, H, name):
    S = proj.shape[0]
    E = B = DIL_BLOCK
    SL = DIL_SLAB
    P = len(DIL_PATTERNS)
    assert S % SL == 0
    n_slabs = S // SL
    scale = E ** -0.5

    def body(q_ref, kc_ref, kp_ref, vc_ref, vp_ref, b_ref, y_ref, dy_ref, lse_ref,
             dq_ref, dk_ref, dv_ref, db_ref, dq_s, dk_own, dv_own, dk_held, dv_held, dk_back, dv_back, dl_s):
        g = pl.program_id(1)

        @pl.when(g == 0)
        def _():
            db_ref[...] = jnp.zeros(db_ref.shape, F32)

        @pl.when(g > 0)
        def _():
            dk_held[...] = dk_own[...]
            dv_held[...] = dv_own[...]
            dk_back[...] = jnp.zeros(dk_back.shape, F32)
            dv_back[...] = jnp.zeros(dv_back.shape, F32)

        @pl.when(g < n_slabs)
        def _():
            dl_s[...] = jnp.broadcast_to(jnp.sum(dy_ref[...] * y_ref[...], axis=-1, keepdims=True), (SL, E))
            tr = lambda t: jnp.swapaxes(t, 1, 2).astype(CDT)
            for p, (_, d) in enumerate(DIL_PATTERNS):
                assert p > 0 or d == 1
                units = _dil_units(d)
                starts = [b for _, b in units]
                q = _dil_rows(q_ref, starts, d, CDT)
                dyc = _dil_rows(dy_ref, starts, d, CDT)
                keys, vals = _dil_keys(kp_ref, kc_ref, units, d), _dil_keys(vp_ref, vc_ref, units, d)
                s = _dil_logits(q, keys, b_ref[p, 0], d, g > 0)
                e = jnp.exp(s - _dil_rows(lse_ref, starts, d)[:, :, 0:1])
                ds = e * (_bdot(dyc, vals, 2) - _dil_rows(dl_s, starts, d)[:, :, 0:1])
                dss = ds * scale
                dq = _bdot(dss.astype(CDT), keys, 1)
                dk = _bdot(tr(dss), q, 1)
                dv = _bdot(tr(e), dyc, 1)
                for t, (sg, b) in enumerate(units):
                    rows = pl.ds(b, B, stride=d)
                    if p == 0:
                        dq_s[rows, :] = dq[t]
                        dk_own[rows, :] = dk[t, B:]
                        dv_own[rows, :] = dv[t, B:]
                    else:
                        dq_s[rows, :] += dq[t]
                        dk_own[rows, :] += dk[t, B:]
                        dv_own[rows, :] += dv[t, B:]
                    if sg > 0:
                        before = pl.ds(b - B * d, B, stride=d)
                        dk_own[before, :] += dk[t, :B]
                        dv_own[before, :] += dv[t, :B]
                    else:
                        @pl.when(g > 0)
                        def _(t=t, b=b, d=d, dk=dk, dv=dv):
                            before = pl.ds(SL + b - B * d, B, stride=d)
                            dk_back[before, :] += dk[t, :B]
                            dv_back[before, :] += dv[t, :B]

                db_ref[p, 0] += jnp.sum(ds, axis=0)
            dq_ref[...] = dq_s[...].astype(dq_ref.dtype)

        @pl.when(g > 0)
        def _():
            dk_ref[...] = (dk_held[...] + dk_back[...]).astype(dk_ref.dtype)
            dv_ref[...] = (dv_held[...] + dv_back[...]).astype(dv_ref.dtype)

    last = n_slabs - 1
    cur = lambda off: pl.BlockSpec((SL, E), lambda h, g: (jnp.minimum(g, last), off + h))
    prev = lambda off: pl.BlockSpec((SL, E), lambda h, g: (jnp.maximum(jnp.minimum(g, last) - 1, 0), off + h))
    late = pl.BlockSpec((SL, E), lambda h, g: (jnp.maximum(g - 1, 0), h))
    bspec = pl.BlockSpec((P, 1, B, 2 * B), lambda h, g: (0, h, 0, 0))
    slab = pltpu.VMEM((SL, E), F32)
    return pl.pallas_call(
        body, grid=(H, n_slabs + 1),
        in_specs=[cur(0), cur(H), prev(H), cur(2 * H), prev(2 * H), bspec, cur(0), cur(0), cur(0)],
        out_specs=[cur(0), late, late, bspec],
        out_shape=[jax.ShapeDtypeStruct((S, H * E), CDT)] * 3 + [jax.ShapeDtypeStruct((P, H, B, 2 * B), F32)],
        scratch_shapes=[slab] * 8,
        name=name, compiler_params=_cparams(("parallel", "arbitrary")),
    )(proj, proj, proj, proj, proj, bias, y, dy, lse)


def _adamw_tile(g_, w_ref, m_ref, v_ref, g_out, d_ref, nm_ref, nv_ref):
    g_out[...] = g_
    m_ = ADAM_B1 * m_ref[...] + (1.0 - ADAM_B1) * g_
    v_ = ADAM_B2 * v_ref[...] + (1.0 - ADAM_B2) * jnp.square(g_)
    m_hat = m_ / (1.0 - ADAM_B1 ** ADAM_STEP)
    v_hat = v_ / (1.0 - ADAM_B2 ** ADAM_STEP)
    d_ref[...] = -ADAM_LR * (m_hat / (jnp.sqrt(v_hat) + ADAM_EPS) + ADAM_WD * w_ref[...])
    nm_ref[...] = m_
    nv_ref[...] = v_


def _adamw(w, g, m, v, name, br=128):
    R, C = w.shape
    br = br if R % br == 0 else R

    def body(w_ref, g_ref, m_ref, v_ref, *outs):
        _adamw_tile(g_ref[...], w_ref, m_ref, v_ref, *outs)

    blk = pl.BlockSpec((br, C), lambda i: (i, 0))
    return pl.pallas_call(
        body, grid=(R // br,), in_specs=[blk] * 4, out_specs=[blk] * 4,
        out_shape=[jax.ShapeDtypeStruct((R, C), F32)] * 4,
        name=name, compiler_params=_cparams(("parallel",)),
    )(w, g, m, v)


_HBM = pl.BlockSpec(memory_space=pltpu.HBM)
_SEM = pl.BlockSpec(memory_space=pltpu.SEMAPHORE)
_ANY = pl.BlockSpec(memory_space=pl.ANY)
_VMEM = pl.BlockSpec(memory_space=pltpu.VMEM)
_TOKEN = jax.ShapeDtypeStruct((8, 128), F32)


def _split_params():
    return pltpu.CompilerParams(has_side_effects=pltpu.SideEffectType.DATAFLOW_SIDE_EFFECTING)


def _place():
    x, y, c = lax.axis_index("x"), lax.axis_index("y"), lax.axis_index("c")
    chips = [(1 - x, y), (x, 1 - y), (1 - x, 1 - y)]
    return x, y, c, chips


def _tie(v, tokens, name):
    flat = v.reshape(1, -1)

    def body(v_ref, *rest):
        rest[-1][...] = v_ref[...]

    return pl.pallas_call(body, in_specs=[_VMEM] + [_ANY] * len(tokens), out_specs=_VMEM,
                          out_shape=jax.ShapeDtypeStruct(flat.shape, flat.dtype), name=name,
                          compiler_params=_cparams())(flat, *tokens).reshape(v.shape)


def _row_block(R, pref=256):
    return _tile(R, pref) if R % 128 == 0 else R


def _slot():
    return 2 * lax.axis_index("x") + lax.axis_index("y")


def _cast_into_slot(w, layer, name):
    _, R, C = w.shape
    br = _row_block(R)

    def body(w_ref, out_ref):
        out_ref[...] = w_ref[...].astype(out_ref.dtype)

    return pl.pallas_call(
        body, grid=(R // br,),
        in_specs=[pl.BlockSpec((None, br, C), lambda i: (layer, i, 0))],
        out_specs=pl.BlockSpec((None, br, C), lambda i: (_slot(), i, 0)),
        out_shape=jax.ShapeDtypeStruct((N_CHIPS, R, C), CDT),
        name=name, compiler_params=_cparams(("parallel",)),
    )(w)


def _gather_copies(src_ref, dst_ref, send_sems, recv_sems, incoming):
    Rh = src_ref.shape[1] // 2
    x, y, c, chips = _place()
    slot = 2 * x + y

    def half(ref, s, hf):
        return ref.at[s, pl.ds(hf * Rh, Rh), :]

    copies = []
    for j, (cx, cy) in enumerate(chips):
        for e in range(2):
            copies.append(pltpu.make_async_remote_copy(
                src_ref=half(src_ref, slot, c), dst_ref=half(dst_ref, 2 * cx + cy, e) if incoming else half(dst_ref, slot, c),
                send_sem=send_sems.at[2 * j + e], recv_sem=recv_sems.at[2 * j + (e if incoming else c)],
                device_id=(cx, cy, e), device_id_type=MESH))
    return copies


def _gather_start(buf, after, name):
    n_after = len(after)

    def body(*refs):
        buf_ref = refs[0]
        send_sems, recv_sems, out_ref, token = refs[1 + n_after:]
        for cp in _gather_copies(buf_ref, out_ref, send_sems, recv_sems, incoming=False):
            cp.start()
        token[...] = jnp.zeros(token.shape, token.dtype)

    return pl.pallas_call(
        body, in_specs=[_HBM] + [_ANY] * n_after, out_specs=(_SEM, _SEM, _HBM, _VMEM),
        out_shape=(pltpu.SemaphoreType.DMA((6,)), pltpu.SemaphoreType.DMA((6,)), pltpu.HBM(buf.shape, buf.dtype), _TOKEN),
        input_output_aliases={0: 2}, name=name, compiler_params=_split_params(),
    )(pltpu.with_memory_space_constraint(buf, pltpu.HBM), *after)


def _gather_wait(send_sems, recv_sems, buf, after, name):
    def body(buf_ref, send_sems, recv_sems, after_ref, out_ref):
        for cp in _gather_copies(buf_ref, out_ref, send_sems, recv_sems, incoming=False):
            cp.wait_send()
        for cp in _gather_copies(buf_ref, out_ref, send_sems, recv_sems, incoming=True):
            cp.wait_recv()

    return pl.pallas_call(
        body, in_specs=[_HBM, _SEM, _SEM, _ANY], out_specs=_HBM, out_shape=pltpu.HBM(buf.shape, buf.dtype),
        input_output_aliases={0: 0}, name=name, compiler_params=_split_params(),
    )(buf, send_sems, recv_sems, after)


def _relay_copies(src_ref, dst_ref, send_sems, recv_sems, stage, incoming):
    Rh = src_ref.shape[1] // 2
    x, y, c, chips = _place()
    copies = []
    for j, (cx, cy) in enumerate(chips):
        if stage == 0:
            src_slot, src_half, peer = 2 * x + y, c, (cx, cy, c)
            dst_slot, dst_half = (2 * cx + cy, c) if incoming else (src_slot, c)
        else:
            src_slot, src_half, peer = 2 * cx + cy, c, (x, y, 1 - c)
            dst_slot, dst_half = src_slot, (1 - c if incoming else c)
        copies.append(pltpu.make_async_remote_copy(
            src_ref=src_ref.at[src_slot, pl.ds(src_half * Rh, Rh), :],
            dst_ref=dst_ref.at[dst_slot, pl.ds(dst_half * Rh, Rh), :],
            send_sem=send_sems.at[j], recv_sem=recv_sems.at[j], device_id=peer, device_id_type=MESH))
    return copies


def _relay_start(buf, after, name):
    n_after = len(after)

    def body(*refs):
        buf_ref = refs[0]
        send_sems, recv_sems, out_ref, token = refs[1 + n_after:]
        for cp in _relay_copies(buf_ref, out_ref, send_sems, recv_sems, 0, incoming=False):
            cp.start()
        token[...] = jnp.zeros(token.shape, token.dtype)

    return pl.pallas_call(
        body, in_specs=[_HBM] + [_ANY] * n_after, out_specs=(_SEM, _SEM, _HBM, _VMEM),
        out_shape=(pltpu.SemaphoreType.DMA((3,)), pltpu.SemaphoreType.DMA((3,)), pltpu.HBM(buf.shape, buf.dtype), _TOKEN),
        input_output_aliases={0: 2}, name=name, compiler_params=_split_params(),
    )(pltpu.with_memory_space_constraint(buf, pltpu.HBM), *after)


def _relay_pass(send_sems, recv_sems, buf, after, name):
    def body(buf_ref, send0, recv0, after_ref, send1, recv1, out_ref):
        for cp in _relay_copies(buf_ref, out_ref, send0, recv0, 0, incoming=False):
            cp.wait_send()
        for cp in _relay_copies(buf_ref, out_ref, send0, recv0, 0, incoming=True):
            cp.wait_recv()
        for cp in _relay_copies(out_ref, out_ref, send1, recv1, 1, incoming=False):
            cp.start()

    return pl.pallas_call(
        body, in_specs=[_HBM, _SEM, _SEM, _ANY], out_specs=(_SEM, _SEM, _HBM),
        out_shape=(pltpu.SemaphoreType.DMA((3,)), pltpu.SemaphoreType.DMA((3,)), pltpu.HBM(buf.shape, buf.dtype)),
        input_output_aliases={0: 2}, name=name, compiler_params=_split_params(),
    )(buf, send_sems, recv_sems, after)


def _relay_wait(send_sems, recv_sems, buf, after, name):
    def body(buf_ref, send1, recv1, after_ref, out_ref):
        for cp in _relay_copies(buf_ref, out_ref, send1, recv1, 1, incoming=False):
            cp.wait_send()
        for cp in _relay_copies(buf_ref, out_ref, send1, recv1, 1, incoming=True):
            cp.wait_recv()

    return pl.pallas_call(
        body, in_specs=[_HBM, _SEM, _SEM, _ANY], out_specs=_HBM, out_shape=pltpu.HBM(buf.shape, buf.dtype),
        input_output_aliases={0: 0}, name=name, compiler_params=_split_params(),
    )(buf, send_sems, recv_sems, after)


def _scatter_copies(g_ref, land_ref, send_sems, recv_sems, incoming):
    Rh = g_ref.shape[1] // 2
    x, y, c, _ = _place()
    me = 4 * x + 2 * y + c
    copies = []
    for k in range(1, N_DEV):
        px, py, pc = (x + (k >> 2)) % 2, (y + ((k >> 1) & 1)) % 2, (c + (k & 1)) % 2
        copies.append(pltpu.make_async_remote_copy(
            src_ref=g_ref.at[2 * px + py, pl.ds(pc * Rh, Rh), :],
            dst_ref=land_ref.at[4 * px + 2 * py + pc if incoming else me],
            send_sem=send_sems.at[k - 1], recv_sem=recv_sems.at[k - 1], device_id=(px, py, pc), device_id_type=MESH))
    return copies


def _scatter_start(g, name):
    ns, R, C = g.shape

    def body(g_ref, land_ref, send_sems, recv_sems, g_thru, land_thru, token):
        for cp in _scatter_copies(g_ref, land_thru, send_sems, recv_sems, incoming=False):
            cp.start()
        token[...] = jnp.zeros(token.shape, token.dtype)

    land = lax.empty((N_DEV, R // 2, C), g.dtype)
    n = N_DEV - 1
    return pl.pallas_call(
        body, in_specs=[_HBM, _HBM], out_specs=(_SEM, _SEM, _HBM, _HBM, _VMEM),
        out_shape=(pltpu.SemaphoreType.DMA((n,)), pltpu.SemaphoreType.DMA((n,)), pltpu.HBM(g.shape, g.dtype),
                   pltpu.HBM(land.shape, land.dtype), _TOKEN),
        input_output_aliases={0: 2, 1: 3}, name=name, compiler_params=_split_params(),
    )(pltpu.with_memory_space_constraint(g, pltpu.HBM), pltpu.with_memory_space_constraint(land, pltpu.HBM))


def _scatter_wait(send_sems, recv_sems, g, land, after, name):
    def body(g_ref, land_ref, send_sems, recv_sems, after_ref, g_out, land_out):
        for cp in _scatter_copies(g_ref, land_out, send_sems, recv_sems, incoming=False):
            cp.wait_send()
        for cp in _scatter_copies(g_ref, land_out, send_sems, recv_sems, incoming=True):
            cp.wait_recv()

    return pl.pallas_call(
        body, in_specs=[_HBM, _HBM, _SEM, _SEM, _ANY], out_specs=(_HBM, _HBM),
        out_shape=(pltpu.HBM(g.shape, g.dtype), pltpu.HBM(land.shape, land.dtype)),
        input_output_aliases={0: 0, 1: 1}, name=name, compiler_params=_split_params(),
    )(g, land, send_sems, recv_sems, after)


def _device_sum(land, g, layer, n_layers, prev, name):
    nd, Rh, C = land.shape
    br = _row_block(Rh)
    nb = Rh // br
    core = lambda: lax.axis_index("c")
    me = lambda: 2 * _slot() + core()

    def body(*refs):
        own = refs[nd][...]
        acc = None
        for d in range(nd):
            t = jnp.where(me() == d, own, refs[d][...]).astype(F32)
            acc = t if acc is None else acc + t
        refs[-1][...] = acc

    def piece(d):
        return pl.BlockSpec((None, br, C), lambda i: (jnp.where(me() == d, (d + 1) % nd, d), i, 0))

    ins = [land] * nd + [g] + ([prev] if prev is not None else [])
    return pl.pallas_call(
        body, grid=(nb,),
        in_specs=[piece(d) for d in range(nd)]
        + [pl.BlockSpec((None, br, C), lambda i: (_slot(), core() * nb + i, 0))]
        + ([_ANY] if prev is not None else []),
        out_specs=pl.BlockSpec((None, br, C), lambda i: (layer, core() * nb + i, 0)),
        out_shape=jax.ShapeDtypeStruct((n_layers, 2 * Rh, C), F32),
        input_output_aliases={nd + 1: 0} if prev is not None else {},
        name=name, compiler_params=_cparams(("parallel",)),
    )(*ins)


def _join_copy(src_ref, dst_ref, layer, send_sem, recv_sem, incoming):
    Rh = src_ref.shape[1] // 2
    x, y, c, _ = _place()
    mine, other = pl.ds(c * Rh, Rh), pl.ds((1 - c) * Rh, Rh)
    return pltpu.make_async_remote_copy(src_ref=src_ref.at[layer, mine, :],
                                        dst_ref=dst_ref.at[layer, other if incoming else mine, :],
                                        send_sem=send_sem, recv_sem=recv_sem, device_id=(x, y, 1 - c),
                                        device_id_type=MESH)


def _join_start(g, layer, name):
    def body(g_ref, send_sem, recv_sem, out_ref, token):
        _join_copy(g_ref, out_ref, layer, send_sem, recv_sem, incoming=False).start()
        token[...] = jnp.zeros(token.shape, token.dtype)

    return pl.pallas_call(
        body, in_specs=[_HBM], out_specs=(_SEM, _SEM, _HBM, _VMEM),
        out_shape=(pltpu.SemaphoreType.DMA(()), pltpu.SemaphoreType.DMA(()), pltpu.HBM(g.shape, g.dtype), _TOKEN),
        input_output_aliases={0: 2}, name=name, compiler_params=_split_params(),
    )(pltpu.with_memory_space_constraint(g, pltpu.HBM))


def _join_wait(send_sem, recv_sem, g, layer, after, name):
    def body(g_ref, send_sem, recv_sem, after_ref, out_ref):
        _join_copy(g_ref, out_ref, layer, send_sem, recv_sem, incoming=False).wait_send()
        _join_copy(g_ref, out_ref, layer, send_sem, recv_sem, incoming=True).wait_recv()

    return pl.pallas_call(
        body, in_specs=[_HBM, _SEM, _SEM, _ANY], out_specs=_HBM, out_shape=pltpu.HBM(g.shape, g.dtype),
        input_output_aliases={0: 0}, name=name, compiler_params=_split_params(),
    )(g, send_sem, recv_sem, after)


def _all_reduce_small(v, name):
    rows, cols = v.shape

    def body(v_ref, out_ref, buf, send_sems, recv_sems):
        x, y, c, _ = _place()
        me = 4 * x + 2 * y + c
        buf[me] = v_ref[...]
        peers = []
        for k in range(1, N_DEV):
            px, py, pc = (x + (k >> 2)) % 2, (y + ((k >> 1) & 1)) % 2, (c + (k & 1)) % 2
            peers.append((px, py, pc))
        sends = []
        for k, peer in enumerate(peers):
            cp = pltpu.make_async_remote_copy(src_ref=v_ref, dst_ref=buf.at[me], send_sem=send_sems.at[k],
                                              recv_sem=recv_sems.at[k], device_id=peer, device_id_type=MESH)
            cp.start()
            sends.append(cp)
        for k, (px, py, pc) in enumerate(peers):
            pltpu.make_async_remote_copy(src_ref=v_ref, dst_ref=buf.at[4 * px + 2 * py + pc], send_sem=send_sems.at[k],
                                         recv_sem=recv_sems.at[k], device_id=(px, py, pc),
                                         device_id_type=MESH).wait_recv()
        for cp in sends:
            cp.wait_send()
        acc = buf[0]
        for i in range(1, N_DEV):
            acc = acc + buf[i]
        out_ref[...] = acc

    vmem = pl.BlockSpec(memory_space=pltpu.VMEM)
    return pl.pallas_call(
        body, in_specs=[vmem], out_specs=vmem, out_shape=jax.ShapeDtypeStruct((rows, cols), F32),
        scratch_shapes=[pltpu.VMEM((N_DEV, rows, cols), F32), pltpu.SemaphoreType.DMA((N_DEV - 1,)),
                        pltpu.SemaphoreType.DMA((N_DEV - 1,))],
        name=name, compiler_params=pltpu.CompilerParams(),
    )(v)


def _reduce_scatter_sum(started, after, layer, n_layers, prev, tag):
    send_sems, recv_sems, g, land, _ = started
    g, land = _scatter_wait(send_sems, recv_sems, g, land, after, f"rs_wait_{tag}")
    f = _device_sum(land, g, layer, n_layers, prev, f"rs_sum_{tag}")
    return _join_start(f, layer, f"rs_join_start_{tag}")


def _split_w_in(wg, Hf, name):
    ns, D, cols = wg.shape
    a = 3 * Hf * HEAD_DIM
    n6 = ns * cols - Hf
    br = _row_block(D)

    def body(w_ref, w6_ref, wf_ref):
        nat = jnp.concatenate([w_ref[s] for s in range(ns)], axis=1)
        w6_ref[...] = jnp.concatenate([nat[:, :a], nat[:, a + Hf:]], axis=1)
        wf_ref[...] = nat[:, a:a + Hf]

    w6, wf = pl.pallas_call(
        body, grid=(D // br,), in_specs=[pl.BlockSpec((ns, br, cols), lambda i: (0, i, 0))],
        out_specs=[pl.BlockSpec((br, n6), lambda i: (i, 0)), pl.BlockSpec((br, Hf), lambda i: (i, 0))],
        out_shape=[jax.ShapeDtypeStruct((D, n6), wg.dtype), jax.ShapeDtypeStruct((D, Hf), wg.dtype)],
        name=name, compiler_params=_cparams(("parallel",)),
    )(wg)
    return w6, wf.T


def _join_dw_in(dw6, dwf_t, Hf, name):
    D, n6 = dw6.shape
    a = 3 * Hf * HEAD_DIM
    cols = (n6 + Hf) // N_CHIPS
    br = _row_block(D)

    def body(w6_ref, wf_ref, out_ref):
        w6 = w6_ref[...]
        nat = jnp.concatenate([w6[:, :a], wf_ref[...], w6[:, a:]], axis=1)
        for s in range(N_CHIPS):
            out_ref[s] = nat[:, s * cols:(s + 1) * cols]

    return pl.pallas_call(
        body, grid=(D // br,),
        in_specs=[pl.BlockSpec((br, n6), lambda i: (i, 0)), pl.BlockSpec((br, Hf), lambda i: (i, 0))],
        out_specs=pl.BlockSpec((N_CHIPS, br, cols), lambda i: (0, i, 0)),
        out_shape=jax.ShapeDtypeStruct((N_CHIPS, D, cols), dw6.dtype),
        name=name, compiler_params=_cparams(("parallel",)),
    )(dw6, dwf_t.T.astype(dw6.dtype))


def _tied(v, tokens, name):
    return _tie(v, tokens, name) if tokens else v


def _layer_fwd(x, p, weight, bias, tokens, tag):
    Hf, Hd = p["forget_b"].shape[0], bias.shape[1]
    h1 = _rms_fwd(x, _tied(p["norm1_g"], tokens, f"tie_norm1_{tag}"), f"norm1_{tag}")
    w6, wf_t = _split_w_in(weight("w_in", h1), Hf, f"split_w_in_{tag}")
    n_a = 3 * Hf * HEAD_DIM
    proj_a = _mm_nn(h1, w6, f"proj_a_{tag}", [CDT], epi=lambda acc: (acc,), b_cols=(0, n_a))[0]
    proj_b = _mm_nn(h1, w6, f"proj_b_{tag}", [F32], b_cols=(n_a, w6.shape[1] - n_a))[0]
    f_t = _mm_nt(wf_t, h1, f"fproj_{tag}", [F32])[0]
    qc, kc = _fox_bias_operands(_gates_fwd(f_t, p["forget_b"], f"gates_{tag}"), f"fox_operands_{tag}")
    y_a, lse_a = _fox_fwd(proj_a, qc, kc, Hf, f"fox_{tag}")
    y_b, lse_b = _dil_fwd(proj_b, bias, Hd, f"dil_{tag}")
    mixed = _pair_norm_fwd(y_a, y_b, p["outnorm_a_g"], p["outnorm_b_g"], f"norm_ab_{tag}")
    w_out = weight("w_out", mixed)
    w_out = w_out.reshape(-1, w_out.shape[2])
    x1 = _mm_nn(mixed, w_out, f"attn_out_{tag}", [F32], extras=[x])[0]
    h2 = _rms_fwd(x1, p["norm2_g"], f"norm2_{tag}")
    w_mi = weight("w_mlp_in", h2)
    u, act = _mm_nn(h2, w_mi, f"mlp_in_{tag}", [CDT, CDT], b_slots=True,
                    epi=lambda acc: (acc, jnp.square(jnp.maximum(acc, 0.0))))
    w_mo = weight("w_mlp_out", act)
    w_mo = w_mo.reshape(-1, w_mo.shape[2])
    x2 = _mm_nn(act, w_mo, f"mlp_out_{tag}", [F32], extras=[x1])[0]
    saved = dict(x=x, h1=h1, proj_a=proj_a, proj_b=proj_b, f_t=f_t, qc=qc, kc=kc, y_a=y_a, lse_a=lse_a, y_b=y_b,
                 lse_b=lse_b, mixed=mixed, x1=x1, h2=h2, u=u, act=act, w6=w6, wf_t=wf_t, w_out=w_out, w_mi=w_mi,
                 w_mo=w_mo)
    return x2, saved


def _layer_bwd(dx2, dx2c, p, send, bias, sv, defer_w_out, tag):
    Hf, Hd = p["forget_b"].shape[0], bias.shape[1]
    E = HEAD_DIM
    rows = lambda g: g.reshape(N_CHIPS, -1, g.shape[1])
    du = _mm_nt(dx2c, sv["w_mo"], f"d_act_{tag}", [CDT], extras=[sv["u"]],
                epi=lambda acc, u: (acc * (2.0 * jnp.maximum(u.astype(F32), 0.0)),))[0]
    tokens = send("w_mlp_out", rows(_mm_tn(sv["act"], dx2c, f"dw_mlp_out_{tag}", CDT)))
    dh2 = _mm_nt(du, sv["w_mi"], f"d_h2_{tag}", [F32], b_slots=True)[0]
    tokens = tokens + send("w_mlp_in", _mm_tn(sv["h2"], du, f"dw_mlp_in_{tag}", CDT, out_slots=N_CHIPS))
    dx1, dx1c, g_norm2 = _rms_bwd(sv["x1"], _tied(p["norm2_g"], tokens, f"tie_norm2_{tag}"), dh2, dx2,
                                  f"d_norm2_{tag}")
    dmixed = _mm_nt(dx1c, sv["w_out"], f"d_mixed_{tag}", [F32])[0]
    send_w_out = lambda: send("w_out", rows(_mm_tn(sv["mixed"], dx1c, f"dw_out_{tag}", CDT)))
    tokens = [] if defer_w_out else send_w_out()
    dy_a, dy_b, g_na, g_nb = _pair_norm_bwd(sv["y_a"], sv["y_b"], _tied(p["outnorm_a_g"], tokens, f"tie_norm_a_{tag}"),
                                            p["outnorm_b_g"], dmixed, f"d_norm_ab_{tag}")
    dq_a, dcq, dk_a, dv_a, dck = _fox_bwd(sv["proj_a"], sv["qc"], sv["kc"], sv["lse_a"], sv["y_a"], dy_a, Hf,
                                          f"fox_bwd_{tag}")
    df, dfc, g_fb = _gates_bwd(sv["f_t"], p["forget_b"], dcq[:, ::E].T, dck.reshape(Hf, -1), f"d_gates_{tag}")
    dq_b, dk_b, dv_b, dbias = _dil_bwd(sv["proj_b"], bias, sv["y_b"], dy_b, sv["lse_b"], Hd, f"dil_bwd_{tag}")
    dproj = jnp.concatenate([dq_a, dk_a, dv_a, dq_b, dk_b, dv_b], axis=1)
    g_w6 = _mm_tn(sv["h1"], dproj, f"dw_in_{tag}", CDT)
    g_wf_t = _mm_nn(dfc, sv["h1"], f"dw_f_{tag}", [F32])[0]
    tokens = send("w_in", _join_dw_in(g_w6, g_wf_t, Hf, f"join_dw_in_{tag}"))
    dh1_f = _mm_tn(dfc, _tied(sv["wf_t"], tokens, f"tie_wf_{tag}"), f"d_h1_f_{tag}", F32)
    dh1 = _mm_nt(dproj, sv["w6"], f"d_h1_{tag}", [F32], extras=[dh1_f])[0]
    dx, dxc, g_norm1 = _rms_bwd(sv["x"], p["norm1_g"], dh1, dx1, f"d_norm1_{tag}")
    grads = dict(norm1_g=g_norm1[0], norm2_g=g_norm2[0], outnorm_a_g=g_na[0], outnorm_b_g=g_nb[0],
                 forget_b=g_fb[:, 0], dbias=dbias)
    return dx, dxc, grads, (send_w_out if defer_w_out else None)


_LAYER_SMALL = ("norm1_g", "forget_b", "outnorm_a_g", "outnorm_b_g", "norm2_g")


def _local_step(x, target, small, weight, send, tokens):
    depth = small["norm1_g"].shape[0]
    buckets = _bucket_table()
    bias = _bias_table(small["rel_bias"], buckets, "bias_table")
    layers, saved = [], []
    for l in range(depth):
        p = {k: small[k][l] for k in _LAYER_SMALL}
        layers.append(p)
        x, sv = _layer_fwd(x, p, functools.partial(weight, l), bias, tokens if l == 0 else [], f"l{l}")
        saved.append(sv)
    dx, dxc, g_final, loss = _loss_bwd(x, small["final_norm_g"], target, "loss")
    layer_grads = [None] * depth
    for l in reversed(range(depth)):
        dx, dxc, layer_grads[l], last = _layer_bwd(dx, dxc, layers[l], functools.partial(send, l), bias, saved[l],
                                                   l == 0, f"l{l}")
    tokens = last()
    dbias = functools.reduce(jnp.add, [g["dbias"] for g in layer_grads])
    g_rel = _bias_table_bwd(dbias, buckets, "d_bias_table")[:, 0, :].T
    small_grads = dict(final_norm_g=g_final[0], rel_bias=g_rel,
                       **{k: jnp.stack([g[k] for g in layer_grads]) for k in _LAYER_SMALL})
    return loss[0, 0], dx, small_grads, tokens


_BIG = ("w_in", "w_out", "w_mlp_in", "w_mlp_out")
_SMALL = ("norm1_g", "forget_b", "rel_bias", "outnorm_a_g", "outnorm_b_g", "norm2_g", "final_norm_g")
_ORDER = ("norm1_g", "w_in", "forget_b", "rel_bias", "outnorm_a_g", "outnorm_b_g", "w_out", "norm2_g", "w_mlp_in",
          "w_mlp_out", "final_norm_g")


def _pack_small(d):
    flat = jnp.concatenate([d[k].reshape(-1) for k in _SMALL])
    rows = -(-flat.shape[0] // (8 * SMALL_COLS)) * 8
    return jnp.pad(flat, (0, rows * SMALL_COLS - flat.shape[0])).reshape(rows, SMALL_COLS)


def _unpack_small(packed, like):
    flat, out, at = packed.reshape(-1), {}, 0
    for k in _SMALL:
        n = like[k].size
        out[k] = flat[at:at + n].reshape(like[k].shape)
        at += n
    return out


def kernel(x, norm1_g, w_in, forget_b, rel_bias, outnorm_a_g, outnorm_b_g, w_out, norm2_g, w_mlp_in, w_mlp_out, final_norm_g, loss_target, m_norm1_g, m_w_in, m_forget_b, m_rel_bias, m_outnorm_a_g, m_outnorm_b_g, m_w_out, m_norm2_g, m_w_mlp_in, m_w_mlp_out, m_final_norm_g, v_norm1_g, v_w_in, v_forget_b, v_rel_bias, v_outnorm_a_g, v_outnorm_b_g, v_w_out, v_norm2_g, v_w_mlp_in, v_w_mlp_out, v_final_norm_g):
    w = dict(norm1_g=norm1_g, w_in=w_in, forget_b=forget_b, rel_bias=rel_bias, outnorm_a_g=outnorm_a_g,
             outnorm_b_g=outnorm_b_g, w_out=w_out, norm2_g=norm2_g, w_mlp_in=w_mlp_in, w_mlp_out=w_mlp_out,
             final_norm_g=final_norm_g)
    m = dict(norm1_g=m_norm1_g, w_in=m_w_in, forget_b=m_forget_b, rel_bias=m_rel_bias, outnorm_a_g=m_outnorm_a_g,
             outnorm_b_g=m_outnorm_b_g, w_out=m_w_out, norm2_g=m_norm2_g, w_mlp_in=m_w_mlp_in,
             w_mlp_out=m_w_mlp_out, final_norm_g=m_final_norm_g)
    v = dict(norm1_g=v_norm1_g, w_in=v_w_in, forget_b=v_forget_b, rel_bias=v_rel_bias, outnorm_a_g=v_outnorm_a_g,
             outnorm_b_g=v_outnorm_b_g, w_out=v_w_out, norm2_g=v_norm2_g, w_mlp_in=v_w_mlp_in,
             w_mlp_out=v_w_mlp_out, final_norm_g=v_final_norm_g)
    depth = w_in.shape[0]
    small = {k: w[k] for k in _SMALL}

    gathers, passed, tokens = {}, {}, []
    for l in range(depth):
        for k in _BIG:
            buf = _cast_into_slot(w[k], l, f"cast_{k}_l{l}")
            start = _relay_start if k == _BIG[0] else _gather_start
            send_sems, recv_sems, buf, token = start(buf, tokens, f"gather_start_{k}_l{l}")
            gathers[l, k], tokens = (send_sems, recv_sems, buf), [token]

    def weight(l, k, after):
        if k == _BIG[-1] and l + 1 < depth:
            passed[l + 1] = _relay_pass(*gathers[l + 1, _BIG[0]], after, f"gather_pass_{_BIG[0]}_l{l + 1}")
        if k != _BIG[0]:
            return _gather_wait(*gathers[l, k], after, f"gather_wait_{k}_l{l}")
        if l not in passed:
            passed[l] = _relay_pass(*gathers[l, k], after, f"gather_pass_{k}_l{l}")
        return _relay_wait(*passed[l], after, f"gather_wait_{k}_l{l}")

    scatters = {}

    def send(l, k, g):
        scatters[l, k] = _scatter_start(g, f"rs_start_{k}_l{l}")
        return [scatters[l, k][4]]

    loss, grad_x, small_grads, tokens = _local_step(x[0], loss_target[0], small, weight, send, tokens)
    loss = lax.psum(loss, ("x", "y", "c"))

    grads, delta, new_m, new_v = {}, {}, {}, {}
    packed = _tied(_pack_small(small_grads), tokens, "tie_small")
    after, seen, joining = packed, {k: 0 for k in _BIG}, None

    def joined(after):
        (l, k), (send_sem, recv_sem, g) = joining
        grads[k] = _join_wait(send_sem, recv_sem, g, l, after, f"rs_join_wait_{k}_l{l}")
        seen[k] += 1
        if seen[k] < depth:
            return after
        shape = w[k].shape
        flat = lambda t: t.reshape(-1, shape[-1])
        outs = _adamw(flat(w[k]), flat(grads[k]), flat(m[k]), flat(v[k]), f"adamw_{k}")
        grads[k], delta[k], new_m[k], new_v[k] = (t.reshape(shape) for t in outs)
        return outs[1]

    for (l, k), started in scatters.items():
        assert joining is None or joining[0][1] != k
        send_sem, recv_sem, g, token = _reduce_scatter_sum(started, after, l, depth, grads.get(k), f"{k}_l{l}")
        if joining is not None:
            after = joined(token)
        joining = ((l, k), (send_sem, recv_sem, g))
    after = joined(after)
    small_sums = _all_reduce_small(_tied(packed, [after], "tie_small_sums"), "small_all_reduce")
    grads.update(_unpack_small(small_sums, small))
    _, d_, m_, v_ = _adamw(_pack_small(small), _pack_small({k: grads[k] for k in _SMALL}),
                           _pack_small({k: m[k] for k in _SMALL}), _pack_small({k: v[k] for k in _SMALL}), "adamw_small")
    delta.update(_unpack_small(d_, small))
    new_m.update(_unpack_small(m_, small))
    new_v.update(_unpack_small(v_, small))

    return (loss, grad_x[None], *[grads[k] for k in _ORDER], *[delta[k] for k in _ORDER],
            *[new_m[k] for k in _ORDER], *[new_v[k] for k in _ORDER])
```

```python
import functools

import jax
import jax.numpy as jnp
from jax import lax
from jax.experimental import pallas as pl
from jax.experimental.pallas import tpu as pltpu

F32 = jnp.float32
CDT = jnp.bfloat16
HEAD_DIM = 128
NORM_EPS = 1e-6
NEG_INF = -1e30
LOG2E = 1.4426950408889634
REL_BUCKETS = 32
REL_MAX_DISTANCE = 2048
DIL_PATTERNS = ((128, 1), (512, 4), (2048, 16))
DIL_BLOCK = 128
ADAM_LR, ADAM_B1, ADAM_B2, ADAM_EPS, ADAM_WD, ADAM_STEP = 0.001, 0.9, 0.999, 1e-08, 0.01, 10
N_CHIPS = 4
N_DEV = 8
VMEM_LIMIT_BYTES = 56 * 1024 * 1024
SMALL_COLS = 1024
MESH = pl.DeviceIdType.MESH


def _cparams(sem=None):
    return pltpu.CompilerParams(dimension_semantics=sem, vmem_limit_bytes=VMEM_LIMIT_BYTES)


def _tile(dim, pref):
    t = min(pref, dim)
    t -= t % 128
    while t >= 128:
        if dim % t == 0:
            return t
        t -= 128
    return dim


def _rowwise(fn, ins, out_dtypes, name, bs=256, consts=()):
    R, C = ins[0].shape
    bs = min(bs, R)
    n_in, n_c = len(ins), len(consts)

    def body(*refs):
        vals = [r[...] for r in refs[:n_in + n_c]]
        res = fn(*vals)
        for o, r in zip(refs[n_in + n_c:], res):
            o[...] = r.astype(o.dtype)

    row = pl.BlockSpec((bs, C), lambda i: (i, 0))
    return pl.pallas_call(
        body, grid=(R // bs,),
        in_specs=[row] * n_in + [pl.BlockSpec((1, c.shape[-1]), lambda i: (0, 0)) for c in consts],
        out_specs=[row] * len(out_dtypes),
        out_shape=[jax.ShapeDtypeStruct((R, C), d) for d in out_dtypes],
        name=name, compiler_params=_cparams(("parallel",)),
    )(*ins, *[c.reshape(1, -1) for c in consts])


def _rms_fwd(x, g, name):
    def fn(xf, gg):
        r = lax.rsqrt(jnp.mean(xf * xf, axis=-1, keepdims=True) + NORM_EPS)
        return ((xf * r) * gg,)
    return _rowwise(fn, [x], [CDT], name, consts=[g])[0]


def _rms_bwd(x, g, dh, dres, name, bs=256):
    S, D = x.shape
    bs = min(bs, S)
    has_res = dres is not None

    def body(*refs):
        x_ref, g_ref, dh_ref = refs[:3]
        dx_ref, dxc_ref, dg_ref = refs[-3:]
        xf = x_ref[...]
        r = lax.rsqrt(jnp.mean(xf * xf, axis=-1, keepdims=True) + NORM_EPS)
        xhat = xf * r
        dh_ = dh_ref[...].astype(F32)
        dxhat = dh_ * g_ref[...]
        dx = r * (dxhat - xhat * jnp.mean(dxhat * xhat, axis=-1, keepdims=True))
        if has_res:
            dx = dx + refs[3][...]
        dx_ref[...] = dx
        dxc_ref[...] = dx.astype(dxc_ref.dtype)
        part = jnp.sum(dh_ * xhat, axis=0, keepdims=True)

        @pl.when(pl.program_id(0) == 0)
        def _():
            dg_ref[...] = part

        @pl.when(pl.program_id(0) > 0)
        def _():
            dg_ref[...] += part

    row = pl.BlockSpec((bs, D), lambda i: (i, 0))
    one = pl.BlockSpec((1, D), lambda i: (0, 0))
    ins = [x, g.reshape(1, D), dh] + ([dres] if has_res else [])
    return pl.pallas_call(
        body, grid=(S // bs,),
        in_specs=[row, one, row] + ([row] if has_res else []),
        out_specs=[row, row, one],
        out_shape=[jax.ShapeDtypeStruct((S, D), F32), jax.ShapeDtypeStruct((S, D), CDT),
                   jax.ShapeDtypeStruct((1, D), F32)],
        name=name, compiler_params=_cparams(("arbitrary",)),
    )(*ins)


def _pair_norm_fwd(y_a, y_b, g_a, g_b, name, bs=256):
    S, Da = y_a.shape
    Db = y_b.shape[1]
    bs = min(bs, S)

    def body(a_ref, b_ref, ga_ref, gb_ref, o_ref):
        def norm(x, g):
            r = lax.rsqrt(jnp.mean(x * x, axis=-1, keepdims=True) + NORM_EPS)
            return ((x * r) * g).astype(o_ref.dtype)
        o_ref[:, :Da] = norm(a_ref[...], ga_ref[...])
        o_ref[:, Da:] = norm(b_ref[...], gb_ref[...])

    row = lambda n: pl.BlockSpec((bs, n), lambda i: (i, 0))
    one = lambda n: pl.BlockSpec((1, n), lambda i: (0, 0))
    return pl.pallas_call(
        body, grid=(S // bs,), in_specs=[row(Da), row(Db), one(Da), one(Db)], out_specs=row(Da + Db),
        out_shape=jax.ShapeDtypeStruct((S, Da + Db), CDT), name=name, compiler_params=_cparams(("parallel",)),
    )(y_a, y_b, g_a.reshape(1, Da), g_b.reshape(1, Db))


def _pair_norm_bwd(y_a, y_b, g_a, g_b, dmixed, name, bs=256):
    S, Da = y_a.shape
    Db = y_b.shape[1]
    bs = min(bs, S)

    def body(a_ref, b_ref, ga_ref, gb_ref, dm_ref, da_ref, db_ref, dga_ref, dgb_ref):
        def one(x_ref, g_ref, dh, dx_ref, dg_ref):
            xf = x_ref[...]
            r = lax.rsqrt(jnp.mean(xf * xf, axis=-1, keepdims=True) + NORM_EPS)
            xhat = xf * r
            dxhat = dh * g_ref[...]
            dx_ref[...] = r * (dxhat - xhat * jnp.mean(dxhat * xhat, axis=-1, keepdims=True))
            part = jnp.sum(dh * xhat, axis=0, keepdims=True)

            @pl.when(pl.program_id(0) == 0)
            def _():
                dg_ref[...] = part

            @pl.when(pl.program_id(0) > 0)
            def _():
                dg_ref[...] += part

        dm = dm_ref[...]
        one(a_ref, ga_ref, dm[:, :Da], da_ref, dga_ref)
        one(b_ref, gb_ref, dm[:, Da:], db_ref, dgb_ref)

    row = lambda n: pl.BlockSpec((bs, n), lambda i: (i, 0))
    one_ = lambda n: pl.BlockSpec((1, n), lambda i: (0, 0))
    return pl.pallas_call(
        body, grid=(S // bs,), in_specs=[row(Da), row(Db), one_(Da), one_(Db), row(Da + Db)],
        out_specs=[row(Da), row(Db), one_(Da), one_(Db)],
        out_shape=[jax.ShapeDtypeStruct((S, Da), F32), jax.ShapeDtypeStruct((S, Db), F32),
                   jax.ShapeDtypeStruct((1, Da), F32), jax.ShapeDtypeStruct((1, Db), F32)],
        name=name, compiler_params=_cparams(("arbitrary",)),
    )(y_a, y_b, g_a.reshape(1, Da), g_b.reshape(1, Db), dmixed)


def _loss_bwd(x, g, target, name, bs=256):
    S, D = x.shape
    bs = min(bs, S)

    def body(x_ref, g_ref, t_ref, dx_ref, dxc_ref, dg_ref, loss_ref):
        xf = x_ref[...]
        r = lax.rsqrt(jnp.mean(xf * xf, axis=-1, keepdims=True) + NORM_EPS)
        xhat = xf * r
        err = xhat * g_ref[...] - t_ref[...]
        lpart = 0.5 * jnp.sum(jnp.mean(err * err, axis=-1, keepdims=True), axis=0, keepdims=True)
        dy = err / D
        dxhat = dy * g_ref[...]
        dx = r * (dxhat - xhat * jnp.mean(dxhat * xhat, axis=-1, keepdims=True))
        dx_ref[...] = dx
        dxc_ref[...] = dx.astype(dxc_ref.dtype)
        gpart = jnp.sum(dy * xhat, axis=0, keepdims=True)

        @pl.when(pl.program_id(0) == 0)
        def _():
            dg_ref[...] = gpart
            loss_ref[...] = lpart

        @pl.when(pl.program_id(0) > 0)
        def _():
            dg_ref[...] += gpart
            loss_ref[...] += lpart

    row = pl.BlockSpec((bs, D), lambda i: (i, 0))
    one = pl.BlockSpec((1, D), lambda i: (0, 0))
    return pl.pallas_call(
        body, grid=(S // bs,),
        in_specs=[row, one, row],
        out_specs=[row, row, one, pl.BlockSpec((1, 1), lambda i: (0, 0))],
        out_shape=[jax.ShapeDtypeStruct((S, D), F32), jax.ShapeDtypeStruct((S, D), CDT),
                   jax.ShapeDtypeStruct((1, D), F32), jax.ShapeDtypeStruct((1, 1), F32)],
        name=name, compiler_params=_cparams(("arbitrary",)),
    )(x, g.reshape(1, D), target)


_NN = (((1,), (0,)), ((), ()))
_NT = (((1,), (1,)), ((), ()))
_TN = (((0,), (0,)), ((), ()))


def _mm(a, b, *, M, N, K, a_spec, b_spec, o_spec, dims, tm, tn, tk, name, out_shapes, extras=(), epi=None):
    nk = K // tk
    n_ex, n_out = len(extras), len(out_shapes)
    in_place = epi is None
    if in_place:
        assert n_out == 1 and n_ex <= 1 and out_shapes[0].dtype == F32
        epi = lambda acc, *r: (acc + r[0] if r else acc,)

    def body(*refs):
        a_ref, b_ref = refs[0], refs[1]
        ex = refs[2:2 + n_ex]
        outs = refs[2 + n_ex:2 + n_ex + n_out]
        part = lax.dot_general(a_ref[...], b_ref[...], dims, preferred_element_type=F32)

        def finish(acc):
            for o, r in zip(outs, epi(acc, *[e[...] for e in ex])):
                o[...] = r.astype(o.dtype)

        if nk == 1:
            finish(part)
        elif in_place:
            k = pl.program_id(2)

            @pl.when(k == 0)
            def _():
                finish(part)

            @pl.when(k > 0)
            def _():
                outs[0][...] += part
        else:
            acc_ref = refs[-1]
            k = pl.program_id(2)

            @pl.when(k == 0)
            def _():
                acc_ref[...] = part

            @pl.when(k > 0)
            def _():
                acc_ref[...] += part

            @pl.when(k == nk - 1)
            def _():
                finish(acc_ref[...])

    ex_spec = pl.BlockSpec((tm, tn), lambda i, j, k: (i, j))
    return pl.pallas_call(
        body, grid=(M // tm, N // tn, nk),
        in_specs=[a_spec, b_spec] + [ex_spec] * n_ex,
        out_specs=[o_spec] * n_out,
        out_shape=out_shapes,
        scratch_shapes=[pltpu.VMEM((tm, tn), F32)] if nk > 1 and not in_place else [],
        name=name, compiler_params=_cparams(("parallel", "parallel", "arbitrary")),
    )(a, b, *extras)


def _mm_tiles(K):
    return (2048, 512, 2048) if K <= 2048 else (1024, 1024, 2048)


def _mm_nn(a, b, name, out_dtypes, extras=(), epi=None, b_slots=False, b_cols=None):
    M, K = a.shape
    tm, tn, tk = _mm_tiles(K)
    if b_slots:
        ns, _, Ns = b.shape
        N = ns * Ns
        tn = _tile(Ns, tn)
        npb = Ns // tn
        tk_ = _tile(K, tk)
        b_spec = pl.BlockSpec((None, tk_, tn), lambda i, j, k: (j // npb, k, j % npb))
    else:
        first, N = b_cols if b_cols is not None else (0, b.shape[1])
        tn = _tile(N, tn)
        assert first % tn == 0
        tk_ = _tile(K, tk)
        b_spec = pl.BlockSpec((tk_, tn), lambda i, j, k: (k, first // tn + j))
    tm = _tile(M, tm)
    return _mm(a, b, M=M, N=N, K=K, a_spec=pl.BlockSpec((tm, tk_), lambda i, j, k: (i, k)), b_spec=b_spec,
               o_spec=pl.BlockSpec((tm, tn), lambda i, j, k: (i, j)), dims=_NN, tm=tm, tn=tn, tk=tk_, name=name,
               out_shapes=[jax.ShapeDtypeStruct((M, N), d) for d in out_dtypes], extras=extras, epi=epi)


def _mm_nt(a, b, name, out_dtypes, extras=(), epi=None, b_slots=False):
    M, K = a.shape
    tm, tn, tk = _mm_tiles(K)
    tm = _tile(M, tm)
    if b_slots:
        ns, N, Ks = b.shape
        tk_ = _tile(Ks, tk)
        kpb = Ks // tk_
        tn = _tile(N, tn)
        b_spec = pl.BlockSpec((None, tn, tk_), lambda i, j, k: (k // kpb, j, k % kpb))
    else:
        N = b.shape[0]
        tk_ = _tile(K, tk)
        tn = _tile(N, tn)
        b_spec = pl.BlockSpec((tn, tk_), lambda i, j, k: (j, k))
    return _mm(a, b, M=M, N=N, K=K, a_spec=pl.BlockSpec((tm, tk_), lambda i, j, k: (i, k)), b_spec=b_spec,
               o_spec=pl.BlockSpec((tm, tn), lambda i, j, k: (i, j)), dims=_NT, tm=tm, tn=tn, tk=tk_, name=name,
               out_shapes=[jax.ShapeDtypeStruct((M, N), d) for d in out_dtypes], extras=extras, epi=epi)


def _mm_tn(a, b, name, out_dtype, out_slots=0, tm=2048, tn=1024, tk=2048):
    K, M = a.shape
    N = b.shape[1]
    tm, tk_ = _tile(M, tm), _tile(K, tk)
    if out_slots:
        Ns = N // out_slots
        tn = _tile(Ns, tn)
        npb = Ns // tn
        o_spec = pl.BlockSpec((None, tm, tn), lambda i, j, k: (j // npb, i, j % npb))
        out_shape = jax.ShapeDtypeStruct((out_slots, M, Ns), out_dtype)
    else:
        tn = _tile(N, tn)
        o_spec = pl.BlockSpec((tm, tn), lambda i, j, k: (i, j))
        out_shape = jax.ShapeDtypeStruct((M, N), out_dtype)
    return _mm(a, b, M=M, N=N, K=K, a_spec=pl.BlockSpec((tk_, tm), lambda i, j, k: (k, i)),
               b_spec=pl.BlockSpec((tk_, tn), lambda i, j, k: (k, j)), o_spec=o_spec, dims=_TN,
               tm=tm, tn=tn, tk=tk_, name=name, out_shapes=[out_shape],
               epi=None if out_dtype == F32 else (lambda acc: (acc,)))[0]


def _mm_nt_parts(parts, b, residual, name, tm=1024, tn=1024):
    P = len(parts)
    M, Kp = parts[0].shape
    N = b.shape[0]
    tm, tn = _tile(M, tm), _tile(N, tn)

    def body(*refs):
        b_ref, r_ref, o_ref = refs[P:]
        k = pl.program_id(2)
        for p in range(P):
            @pl.when(k == p)
            def _(p=p):
                part = lax.dot_general(refs[p][...], b_ref[...], _NT, preferred_element_type=F32)
                if p == 0:
                    o_ref[...] = r_ref[...] + part
                else:
                    o_ref[...] += part

    tile = pl.BlockSpec((tm, tn), lambda i, j, k: (i, j))
    return pl.pallas_call(
        body, grid=(M // tm, N // tn, P),
        in_specs=[pl.BlockSpec((tm, Kp), lambda i, j, k: (i, 0))] * P
        + [pl.BlockSpec((tn, Kp), lambda i, j, k: (j, k)), tile],
        out_specs=tile, out_shape=jax.ShapeDtypeStruct((M, N), F32),
        name=name, compiler_params=_cparams(("parallel", "parallel", "arbitrary")),
    )(*parts, b, residual)


def _mm_tn_parts(a, parts, name, out_dtype, tk=512):
    P = len(parts)
    K, M = a.shape
    Np = parts[0].shape[1]
    tk = _tile(K, tk)
    nk = K // tk

    def body(a_ref, *refs):
        o_ref, acc_ref = refs[P:]
        j, k = pl.program_id(0), pl.program_id(1)
        for p in range(P):
            @pl.when(j == p)
            def _(p=p):
                part = lax.dot_general(a_ref[...], refs[p][...], _TN, preferred_element_type=F32)

                @pl.when(k == 0)
                def _():
                    acc_ref[...] = part

                @pl.when(k > 0)
                def _():
                    acc_ref[...] += part

        @pl.when(k == nk - 1)
        def _():
            o_ref[...] = acc_ref[...].astype(o_ref.dtype)

    def part_spec(p):
        return pl.BlockSpec((tk, Np), lambda j, k: (jnp.where(j == p, k, 0), 0))

    return pl.pallas_call(
        body, grid=(P, nk),
        in_specs=[pl.BlockSpec((tk, M), lambda j, k: (k, 0))] + [part_spec(p) for p in range(P)],
        out_specs=pl.BlockSpec((M, Np), lambda j, k: (0, j)),
        out_shape=jax.ShapeDtypeStruct((M, P * Np), out_dtype),
        scratch_shapes=[pltpu.VMEM((M, Np), F32)],
        name=name, compiler_params=_cparams(("parallel", "arbitrary")),
    )(a, *parts)


GATE_BLOCK = 512


def _split3(v):
    hi = v.astype(jnp.bfloat16)
    r1 = v - hi.astype(F32)
    mid = r1.astype(jnp.bfloat16)
    lo = (r1 - mid.astype(F32)).astype(jnp.bfloat16)
    return hi, mid, lo


def _exact_dot(v, tri):
    return functools.reduce(jnp.add, [jnp.dot(t, tri, preferred_element_type=F32) for t in _split3(v)])


def _gates_fwd(f_t, b, name):
    H, S = f_t.shape
    nb = _tile(S, GATE_BLOCK)
    inv_scale = HEAD_DIM ** 0.5

    def body(f_ref, b_ref, c_ref):
        upper = (lax.broadcasted_iota(jnp.int32, (nb, nb), 0)
                 <= lax.broadcasted_iota(jnp.int32, (nb, nb), 1)).astype(jnp.bfloat16)
        carry = jnp.zeros((H, 1), F32)
        for i in range(S // nb):
            z = f_ref[:, i * nb:(i + 1) * nb] + b_ref[...]
            logf = jnp.minimum(z, 0.0) - jnp.log1p(jnp.exp(-jnp.abs(z)))
            cs = _exact_dot(logf, upper) + carry
            for j, t in enumerate(_split3(cs * inv_scale)):
                c_ref[j, :, i * nb:(i + 1) * nb] = t
            carry = cs[:, nb - 1:nb]

    return pl.pallas_call(body, out_shape=jax.ShapeDtypeStruct((3, H, S), jnp.bfloat16), name=name,
                          compiler_params=_cparams())(f_t, b.reshape(H, 1))


def _gates_bwd(f_t, b, dcq, dck, name):
    H, S = f_t.shape
    nb = _tile(S, GATE_BLOCK)

    def body(f_ref, b_ref, dcq_ref, dck_ref, df_ref, dfc_ref, db_ref):
        lower = (lax.broadcasted_iota(jnp.int32, (nb, nb), 0)
                 >= lax.broadcasted_iota(jnp.int32, (nb, nb), 1)).astype(jnp.bfloat16)
        carry = jnp.zeros((H, 1), F32)
        db = jnp.zeros((H, 1), F32)
        for i in reversed(range(S // nb)):
            sl = slice(i * nb, (i + 1) * nb)
            dc = dcq_ref[:, sl] - dck_ref[:, sl]
            dlogf = _exact_dot(dc, lower) + carry
            carry = dlogf[:, 0:1]
            z = f_ref[:, sl] + b_ref[...]
            df = dlogf / (1.0 + jnp.exp(z))
            df_ref[:, sl] = df
            dfc_ref[:, sl] = df.astype(dfc_ref.dtype)
            db = db + jnp.sum(df, axis=1, keepdims=True)
        db_ref[...] = db

    return pl.pallas_call(
        body, out_shape=[jax.ShapeDtypeStruct((H, S), F32), jax.ShapeDtypeStruct((H, S), CDT),
                         jax.ShapeDtypeStruct((H, 1), F32)],
        name=name, compiler_params=_cparams())(f_t, b.reshape(H, 1), dcq, dck)


FOX_BLOCK = 1024


def _fox_bias_operands(csplit, name, bs=512):
    _, H, S = csplit.shape
    E = HEAD_DIM
    bs = _tile(S, bs)
    part = jnp.arange(3 * H)[:, None] // H
    head = jnp.arange(3 * H)[:, None] % H
    lane = jnp.arange(H * E)[None, :]
    place_q = (lane == head * E + part).astype(csplit.dtype)
    place_k = -(lane == head * E + 3 + part).astype(csplit.dtype)
    ones_q = ((lane % E >= 3) & (lane % E < 6)).astype(F32)
    ones_k = (lane % E < 3).astype(F32)

    def body(c_ref, pq_ref, pk_ref, oq_ref, ok_ref, qc_ref, kc_ref):
        c = c_ref[...]
        qc_ref[...] = (lax.dot_general(c, pq_ref[...], _TN, preferred_element_type=F32) + oq_ref[...]).astype(qc_ref.dtype)
        kc_ref[...] = (lax.dot_general(c, pk_ref[...], _TN, preferred_element_type=F32) + ok_ref[...]).astype(kc_ref.dtype)

    full = lambda a: pl.BlockSpec(a.shape, lambda i: (0, 0))
    out = pl.BlockSpec((bs, H * E), lambda i: (i, 0))
    return pl.pallas_call(
        body, grid=(S // bs,),
        in_specs=[pl.BlockSpec((3 * H, bs), lambda i: (0, i)), full(place_q), full(place_k), full(ones_q), full(ones_k)],
        out_specs=[out, out], out_shape=[jax.ShapeDtypeStruct((S, H * E), csplit.dtype)] * 2,
        name=name, compiler_params=_cparams(("parallel",)),
    )(csplit.reshape(3 * H, S), place_q, place_k, ones_q, ones_k)


def _fox_logits2(q_ref, qc_ref, k_ref, kc_ref, diag):
    q, k = q_ref[...], k_ref[...]
    qa = jnp.concatenate([q, qc_ref[...].astype(q.dtype)], axis=1)
    ka = jnp.concatenate([k, kc_ref[...].astype(k.dtype)], axis=1)
    s = lax.dot_general(qa, ka, _NT, preferred_element_type=F32) * (HEAD_DIM ** -0.5 * LOG2E)
    if diag:
        row = lax.broadcasted_iota(jnp.int32, s.shape, 0)
        col = lax.broadcasted_iota(jnp.int32, s.shape, 1)
        s = jnp.where(col <= row, s, NEG_INF)
    return s


def _fox_fwd(proj, qc, kc, H, name):
    S = proj.shape[0]
    E = HEAD_DIM
    blk = _tile(S, FOX_BLOCK)
    nq = S // blk

    def pair(t):
        qi = sum((t >= i * (i + 1) // 2).astype(jnp.int32) for i in range(1, nq)) if nq > 1 else 0 * t
        return qi, t - qi * (qi + 1) // 2

    def body(q_ref, qc_ref, k_ref, kc_ref, v_ref, o_ref, lse_ref, m_s, l_s, acc_s):
        qi, kj = pair(pl.program_id(1))

        @pl.when(kj == 0)
        def _():
            m_s[...] = jnp.full(m_s.shape, NEG_INF, F32)
            l_s[...] = jnp.zeros(l_s.shape, F32)
            acc_s[...] = jnp.zeros(acc_s.shape, F32)

        def step(diag):
            s = _fox_logits2(q_ref, qc_ref, k_ref, kc_ref, diag)
            m_prev = m_s[...]
            m_new = jnp.maximum(m_prev, jnp.max(s, axis=-1, keepdims=True))
            alpha = jnp.exp2(m_prev - m_new)
            p = jnp.exp2(s - m_new)
            l_s[...] = alpha * l_s[...] + jnp.sum(p, axis=-1, keepdims=True)
            acc_s[...] = alpha * acc_s[...] + jnp.dot(p.astype(CDT), v_ref[...], preferred_element_type=F32)
            m_s[...] = m_new

        pl.when(kj < qi)(lambda: step(False))
        pl.when(kj == qi)(lambda: step(True))

        @pl.when(kj == qi)
        def _():
            o_ref[...] = acc_s[...] / l_s[...]
            lse_ref[...] = jnp.broadcast_to(m_s[...] + jnp.log2(l_s[...]), lse_ref.shape)

    qspec = lambda off: pl.BlockSpec((blk, E), lambda h, t: (pair(t)[0], off + h))
    kspec = lambda off: pl.BlockSpec((blk, E), lambda h, t: (pair(t)[1], off + h))
    return pl.pallas_call(
        body, grid=(H, nq * (nq + 1) // 2),
        in_specs=[qspec(0), qspec(0), kspec(H), kspec(0), kspec(2 * H)],
        out_specs=[qspec(0)] * 2,
        out_shape=[jax.ShapeDtypeStruct((S, H * E), F32)] * 2,
        scratch_shapes=[pltpu.VMEM((blk, 1), F32), pltpu.VMEM((blk, 1), F32), pltpu.VMEM((blk, E), F32)],
        name=name, compiler_params=_cparams(("parallel", "arbitrary")),
    )(proj, qc, proj, kc, proj)


def _fox_bwd(proj, qc, kc, lse, o, do, H, name):
    S = proj.shape[0]
    E = HEAD_DIM
    blk = _tile(S, FOX_BLOCK)
    nq = S // blk
    scale = E ** -0.5

    def pair(t):
        first = lambda j: j * nq - j * (j - 1) // 2
        kj = sum((t >= first(j)).astype(jnp.int32) for j in range(1, nq)) if nq > 1 else 0 * t
        return kj, kj + t - first(kj)

    def body(q_ref, qc_ref, k_ref, kc_ref, v_ref, lse_ref, o_ref, do_ref,
             dq_ref, dcq_ref, dk_ref, dv_ref, dck_ref, dq_s, dcq_s, dk_s, dv_s, dck_s):
        kj, qi = pair(pl.program_id(1))

        @pl.when(qi == kj)
        def _():
            dk_s[...] = jnp.zeros(dk_s.shape, F32)
            dv_s[...] = jnp.zeros(dv_s.shape, F32)
            dck_s[...] = jnp.zeros(dck_s.shape, F32)

        def step(diag):
            do = do_ref[...]
            doc = do.astype(CDT)
            delta = jnp.sum(do * o_ref[...], axis=-1, keepdims=True)
            p = jnp.exp2(_fox_logits2(q_ref, qc_ref, k_ref, kc_ref, diag) - lse_ref[:, 0:1])
            dp = lax.dot_general(doc, v_ref[...], _NT, preferred_element_type=F32)
            ds = p * (dp - delta)
            dss = ds * scale
            dck_s[...] += jnp.sum(ds, axis=0, keepdims=True)
            dv_s[...] += jnp.dot(p.T.astype(CDT), doc, preferred_element_type=F32)
            dk_s[...] += jnp.dot(dss.T.astype(CDT), q_ref[...], preferred_element_type=F32)
            dq_part = jnp.dot(dss.astype(CDT), k_ref[...], preferred_element_type=F32)
            dc_part = jnp.sum(ds, axis=-1, keepdims=True)
            rows = pl.ds(pl.multiple_of(qi * blk, blk), blk)

            @pl.when(kj == 0)
            def _():
                dq_s[rows, :] = dq_part
                dcq_s[rows, :] = dc_part

            @pl.when(kj > 0)
            def _():
                dq_s[rows, :] += dq_part
                dcq_s[rows, :] += dc_part

        pl.when(qi > kj)(lambda: step(False))
        pl.when(qi == kj)(lambda: step(True))

        @pl.when(qi == nq - 1)
        def _():
            dk_ref[...] = dk_s[...].astype(dk_ref.dtype)
            dv_ref[...] = dv_s[...].astype(dv_ref.dtype)
            dck_ref[...] = dck_s[...].reshape(dck_ref.shape)

        @pl.when((qi == nq - 1) & (kj == nq - 1))
        def _():
            dq_ref[...] = dq_s[...].astype(dq_ref.dtype)
            dcq_ref[...] = jnp.broadcast_to(dcq_s[...], dcq_ref.shape)

    qspec = lambda off: pl.BlockSpec((blk, E), lambda h, t: (pair(t)[1], off + h))
    kspec = lambda off: pl.BlockSpec((blk, E), lambda h, t: (pair(t)[0], off + h))
    head = pl.BlockSpec((S, E), lambda h, t: (0, h))
    return pl.pallas_call(
        body, grid=(H, nq * (nq + 1) // 2),
        in_specs=[qspec(0), qspec(0), kspec(H), kspec(0), kspec(2 * H), qspec(0), qspec(0), qspec(0)],
        out_specs=[head, head, kspec(0), kspec(0), pl.BlockSpec((1, 1, blk), lambda h, t: (h, 0, pair(t)[0]))],
        out_shape=[jax.ShapeDtypeStruct((S, H * E), CDT), jax.ShapeDtypeStruct((S, H * E), F32),
                   jax.ShapeDtypeStruct((S, H * E), CDT), jax.ShapeDtypeStruct((S, H * E), CDT),
                   jax.ShapeDtypeStruct((H, 1, S), F32)],
        scratch_shapes=[pltpu.VMEM((S, E), F32), pltpu.VMEM((S, 1), F32), pltpu.VMEM((blk, E), F32),
                        pltpu.VMEM((blk, E), F32), pltpu.VMEM((1, blk), F32)],
        name=name, compiler_params=_cparams(("parallel", "arbitrary")),
    )(proj, qc, proj, kc, proj, lse, o, do)


DIL_SLAB = 16 * DIL_BLOCK


def _rel_bucket(dist):
    max_exact = REL_BUCKETS // 2
    d = jnp.maximum(dist.astype(F32), 1.0)
    large = max_exact + (jnp.log(d / max_exact) / jnp.log(jnp.float32(REL_MAX_DISTANCE / max_exact))
                         * (REL_BUCKETS - max_exact)).astype(jnp.int32)
    large = jnp.minimum(large, REL_BUCKETS - 1)
    return jnp.where(dist < max_exact, dist, large)


def _bucket_table():
    i = jnp.arange(DIL_BLOCK)[:, None]
    j = jnp.arange(2 * DIL_BLOCK)[None, :]
    rel = DIL_BLOCK + i - j
    tabs = [_rel_bucket(jnp.clip(rel, 0, w // d) * d) for w, d in DIL_PATTERNS]
    return jnp.stack(tabs).astype(jnp.int32)


def _bias_table(rel_bias, buckets, name):
    P = buckets.shape[0]
    H = rel_bias.shape[1]

    def body(rb_ref, bk_ref, out_ref):
        h = pl.program_id(1)
        bk = bk_ref[0]
        val = jnp.zeros(bk.shape, F32)
        for b in range(REL_BUCKETS):
            val = jnp.where(bk == b, rb_ref[b, h], val)
        out_ref[0, 0] = val

    return pl.pallas_call(
        body, grid=(P, H),
        in_specs=[pl.BlockSpec(memory_space=pltpu.SMEM),
                  pl.BlockSpec((1, DIL_BLOCK, 2 * DIL_BLOCK), lambda p, h: (p, 0, 0))],
        out_specs=pl.BlockSpec((1, 1, DIL_BLOCK, 2 * DIL_BLOCK), lambda p, h: (p, h, 0, 0)),
        out_shape=jax.ShapeDtypeStruct((P, H, DIL_BLOCK, 2 * DIL_BLOCK), F32),
        name=name, compiler_params=_cparams(("parallel", "parallel")),
    )(rel_bias, buckets)


def _bias_table_bwd(dbias, buckets, name):
    P, H = dbias.shape[:2]

    def body(db_ref, bk_ref, out_ref):
        lane = lax.broadcasted_iota(jnp.int32, (1, REL_BUCKETS), 1)
        acc = jnp.zeros((1, REL_BUCKETS), F32)
        bk = bk_ref[...]
        db = db_ref[:, 0]
        for b in range(REL_BUCKETS):
            tot = jnp.sum(jnp.where(bk == b, db, 0.0))
            acc = jnp.where(lane == b, tot, acc)
        out_ref[0] = acc

    return pl.pallas_call(
        body, grid=(H,),
        in_specs=[pl.BlockSpec((P, 1, DIL_BLOCK, 2 * DIL_BLOCK), lambda h: (0, h, 0, 0)),
                  pl.BlockSpec((P, DIL_BLOCK, 2 * DIL_BLOCK), lambda h: (0, 0, 0))],
        out_specs=pl.BlockSpec((1, 1, REL_BUCKETS), lambda h: (h, 0, 0)),
        out_shape=jax.ShapeDtypeStruct((H, 1, REL_BUCKETS), F32),
        name=name, compiler_params=_cparams(("parallel",)),
    )(dbias, buckets)


def _bdot(a, b, contract_b):
    return lax.dot_general(a, b, (((2,), (contract_b,)), ((0,), (0,))), preferred_element_type=F32)


def _dil_units(d):
    return [(u // d, (u // d) * (DIL_BLOCK * d) + u % d) for u in range(DIL_SLAB // DIL_BLOCK)]


def _dil_rows(ref, starts, d, dtype=None):
    t = jnp.stack([ref[pl.ds(s, DIL_BLOCK, stride=d), :] for s in starts])
    return t if dtype is None else t.astype(dtype)


def _dil_keys(before_ref, ref, units, d):
    B, SL = DIL_BLOCK, DIL_SLAB

    def one(sg, b):
        before = ref[pl.ds(b - B * d, B, stride=d), :] if sg else before_ref[pl.ds(SL + b - B * d, B, stride=d), :]
        return jnp.concatenate([before, ref[pl.ds(b, B, stride=d), :]], axis=0)

    return jnp.stack([one(sg, b) for sg, b in units]).astype(CDT)


def _dil_logits(q, keys, bias_pc, d, has_before):
    T, B = q.shape[0], DIL_BLOCK
    ii = lax.broadcasted_iota(jnp.int32, (T, B, 2 * B), 1)
    jj = lax.broadcasted_iota(jnp.int32, (T, B, 2 * B), 2)
    sg = lax.broadcasted_iota(jnp.int32, (T, B, 2 * B), 0) // d
    mask = (jj >= ii) & (jj <= ii + B) & ((jj >= B) | (sg > 0) | has_before)
    return jnp.where(mask, _bdot(q, keys, 2) * HEAD_DIM ** -0.5 + bias_pc[None], NEG_INF)


def _dil_specs(H):
    E, SL = DIL_BLOCK, DIL_SLAB
    cur = lambda off: pl.BlockSpec((SL, E), lambda h, g: (g, off + h))
    prev = lambda off: pl.BlockSpec((SL, E), lambda h, g: (jnp.maximum(g - 1, 0), off + h))
    bias = pl.BlockSpec((len(DIL_PATTERNS), 1, E, 2 * E), lambda h, g: (0, h, 0, 0))
    return cur, prev, bias


def _dil_fwd(proj, bias, H, name):
    S = proj.shape[0]
    E = B = DIL_BLOCK
    SL = DIL_SLAB
    P = len(DIL_PATTERNS)
    assert S % SL == 0
    n_slabs = S // SL

    def body(q_ref, kc_ref, kp_ref, vc_ref, vp_ref, b_ref, y_ref, lse_ref, o_s, l_s):
        g = pl.program_id(1)
        for p, (_, d) in enumerate(DIL_PATTERNS):
            units = _dil_units(d)
            q = _dil_rows(q_ref, [b for _, b in units], d, CDT)
            s = _dil_logits(q, _dil_keys(kp_ref, kc_ref, units, d), b_ref[p, 0], d, g > 0)
            m = jnp.max(s, axis=-1, keepdims=True)
            e = jnp.exp(s - m)
            ssum = jnp.sum(e, axis=-1, keepdims=True)
            o = _bdot(e.astype(CDT), _dil_keys(vp_ref, vc_ref, units, d), 1) / ssum
            lse = jnp.broadcast_to(m + jnp.log(ssum), o.shape)
            for t, (_, b) in enumerate(units):
                o_s[p, pl.ds(b, B, stride=d), :] = o[t]
                l_s[p, pl.ds(b, B, stride=d), :] = lse[t]
        ls = [l_s[p] for p in range(P)]
        m = functools.reduce(jnp.maximum, ls)
        w = [jnp.exp(l - m) for l in ls]
        tot = functools.reduce(jnp.add, w)
        y_ref[...] = functools.reduce(jnp.add, [(w[p] / tot) * o_s[p] for p in range(P)])
        lse_ref[...] = m + jnp.log(tot)

    cur, prev, bspec = _dil_specs(H)
    return pl.pallas_call(
        body, grid=(H, n_slabs),
        in_specs=[cur(0), cur(H), prev(H), cur(2 * H), prev(2 * H), bspec],
        out_specs=[cur(0), cur(0)],
        out_shape=[jax.ShapeDtypeStruct((S, H * E), F32)] * 2,
        scratch_shapes=[pltpu.VMEM((P, SL, E), F32), pltpu.VMEM((P, SL, E), F32)],
        name=name, compiler_params=_cparams(("parallel", "parallel")),
    )(proj, proj, proj, proj, proj, bias)


def _dil_bwd(proj, bias, y, dy, lse, H, name):
    S = proj.shape[0]
    E = B = DIL_BLOCK
    SL = DIL_SLAB
    P = len(DIL_PATTERNS)
    assert S % SL == 0
    n_slabs = S // SL
    scale = E ** -0.5

    def body(q_ref, kc_ref, kp_ref, vc_ref, vp_ref, b_ref, y_ref, dy_ref, lse_ref,
             dq_ref, dk_ref, dv_ref, db_ref, dq_s, dk_own, dv_own, dk_held, dv_held, dk_back, dv_back, dl_s):
        g = pl.program_id(1)

        @pl.when(g == 0)
        def _():
            db_ref[...] = jnp.zeros(db_ref.shape, F32)

        @pl.when(g > 0)
        def _():
            dk_held[...] = dk_own[...]
            dv_held[...] = dv_own[...]
            dk_back[...] = jnp.zeros(dk_back.shape, F32)
            dv_back[...] = jnp.zeros(dv_back.shape, F32)

        @pl.when(g < n_slabs)
        def _():
            dl_s[...] = jnp.broadcast_to(jnp.sum(dy_ref[...] * y_ref[...], axis=-1, keepdims=True), (SL, E))
            tr = lambda t: jnp.swapaxes(t, 1, 2).astype(CDT)
            for p, (_, d) in enumerate(DIL_PATTERNS):
                assert p > 0 or d == 1
                units = _dil_units(d)
                starts = [b for _, b in units]
                q = _dil_rows(q_ref, starts, d, CDT)
                dyc = _dil_rows(dy_ref, starts, d, CDT)
                keys, vals = _dil_keys(kp_ref, kc_ref, units, d), _dil_keys(vp_ref, vc_ref, units, d)
                s = _dil_logits(q, keys, b_ref[p, 0], d, g > 0)
                e = jnp.exp(s - _dil_rows(lse_ref, starts, d)[:, :, 0:1])
                ds = e * (_bdot(dyc, vals, 2) - _dil_rows(dl_s, starts, d)[:, :, 0:1])
                dss = ds * scale
                dq = _bdot(dss.astype(CDT), keys, 1)
                dk = _bdot(tr(dss), q, 1)
                dv = _bdot(tr(e), dyc, 1)
                for t, (sg, b) in enumerate(units):
                    rows = pl.ds(b, B, stride=d)
                    if p == 0:
                        dq_s[rows, :] = dq[t]
                        dk_own[rows, :] = dk[t, B:]
                        dv_own[rows, :] = dv[t, B:]
                    else:
                        dq_s[rows, :] += dq[t]
                        dk_own[rows, :] += dk[t, B:]
                        dv_own[rows, :] += dv[t, B:]
                    if sg > 0:
                        before = pl.ds(b - B * d, B, stride=d)
                        dk_own[before, :] += dk[t, :B]
                        dv_own[before, :] += dv[t, :B]
                    else:
                        @pl.when(g > 0)
                        def _(t=t, b=b, d=d, dk=dk, dv=dv):
                            before = pl.ds(SL + b - B * d, B, stride=d)
                            dk_back[before, :] += dk[t, :B]
                            dv_back[before, :] += dv[t, :B]

                db_ref[p, 0] += jnp.sum(ds, axis=0)
            dq_ref[...] = dq_s[...].astype(dq_ref.dtype)

        @pl.when(g > 0)
        def _():
            dk_ref[...] = (dk_held[...] + dk_back[...]).astype(dk_ref.dtype)
            dv_ref[...] = (dv_held[...] + dv_back[...]).astype(dv_ref.dtype)

    last = n_slabs - 1
    cur = lambda off: pl.BlockSpec((SL, E), lambda h, g: (jnp.minimum(g, last), off + h))
    prev = lambda off: pl.BlockSpec((SL, E), lambda h, g: (jnp.maximum(jnp.minimum(g, last) - 1, 0), off + h))
    late = pl.BlockSpec((SL, E), lambda h, g: (jnp.maximum(g - 1, 0), h))
    bspec = pl.BlockSpec((P, 1, B, 2 * B), lambda h, g: (0, h, 0, 0))
    slab = pltpu.VMEM((SL, E), F32)
    return pl.pallas_call(
        body, grid=(H, n_slabs + 1),
        in_specs=[cur(0), cur(H), prev(H), cur(2 * H), prev(2 * H), bspec, cur(0), cur(0), cur(0)],
        out_specs=[cur(0), late, late, bspec],
        out_shape=[jax.ShapeDtypeStruct((S, H * E), CDT)] * 3 + [jax.ShapeDtypeStruct((P, H, B, 2 * B), F32)],
        scratch_shapes=[slab] * 8,
        name=name, compiler_params=_cparams(("parallel", "arbitrary")),
    )(proj, proj, proj, proj, proj, bias, y, dy, lse)


def _adamw_tile(g_, w_ref, m_ref, v_ref, g_out, d_ref, nm_ref, nv_ref):
    g_out[...] = g_
    m_ = ADAM_B1 * m_ref[...] + (1.0 - ADAM_B1) * g_
    v_ = ADAM_B2 * v_ref[...] + (1.0 - ADAM_B2) * jnp.square(g_)
    m_hat = m_ / (1.0 - ADAM_B1 ** ADAM_STEP)
    v_hat = v_ / (1.0 - ADAM_B2 ** ADAM_STEP)
    d_ref[...] = -ADAM_LR * (m_hat / (jnp.sqrt(v_hat) + ADAM_EPS) + ADAM_WD * w_ref[...])
    nm_ref[...] = m_
    nv_ref[...] = v_


def _adamw(w, g, m, v, name, br=128):
    R, C = w.shape
    br = br if R % br == 0 else R

    def body(w_ref, g_ref, m_ref, v_ref, *outs):
        _adamw_tile(g_ref[...], w_ref, m_ref, v_ref, *outs)

    blk = pl.BlockSpec((br, C), lambda i: (i, 0))
    return pl.pallas_call(
        body, grid=(R // br,), in_specs=[blk] * 4, out_specs=[blk] * 4,
        out_shape=[jax.ShapeDtypeStruct((R, C), F32)] * 4,
        name=name, compiler_params=_cparams(("parallel",)),
    )(w, g, m, v)


_HBM = pl.BlockSpec(memory_space=pltpu.HBM)
_SEM = pl.BlockSpec(memory_space=pltpu.SEMAPHORE)
_ANY = pl.BlockSpec(memory_space=pl.ANY)
_VMEM = pl.BlockSpec(memory_space=pltpu.VMEM)
_TOKEN = jax.ShapeDtypeStruct((8, 128), F32)


def _split_params():
    return pltpu.CompilerParams(has_side_effects=pltpu.SideEffectType.DATAFLOW_SIDE_EFFECTING)


def _place():
    x, y, c = lax.axis_index("x"), lax.axis_index("y"), lax.axis_index("c")
    chips = [(1 - x, y), (x, 1 - y), (1 - x, 1 - y)]
    return x, y, c, chips


def _tie(v, tokens, name):
    flat = v.reshape(1, -1)

    def body(v_ref, *rest):
        rest[-1][...] = v_ref[...]

    return pl.pallas_call(body, in_specs=[_VMEM] + [_ANY] * len(tokens), out_specs=_VMEM,
                          out_shape=jax.ShapeDtypeStruct(flat.shape, flat.dtype), name=name,
                          compiler_params=_cparams())(flat, *tokens).reshape(v.shape)


def _row_block(R, pref=256):
    return _tile(R, pref) if R % 128 == 0 else R


def _slot():
    return 2 * lax.axis_index("x") + lax.axis_index("y")


def _cast_into_slot(w, layer, name):
    _, R, C = w.shape
    br = _row_block(R)

    def body(w_ref, out_ref):
        out_ref[...] = w_ref[...].astype(out_ref.dtype)

    return pl.pallas_call(
        body, grid=(R // br,),
        in_specs=[pl.BlockSpec((None, br, C), lambda i: (layer, i, 0))],
        out_specs=pl.BlockSpec((None, br, C), lambda i: (_slot(), i, 0)),
        out_shape=jax.ShapeDtypeStruct((N_CHIPS, R, C), CDT),
        name=name, compiler_params=_cparams(("parallel",)),
    )(w)


def _gather_copies(src_ref, dst_ref, send_sems, recv_sems, incoming):
    Rh = src_ref.shape[1] // 2
    x, y, c, chips = _place()
    slot = 2 * x + y

    def half(ref, s, hf):
        return ref.at[s, pl.ds(hf * Rh, Rh), :]

    copies = []
    for j, (cx, cy) in enumerate(chips):
        for e in range(2):
            copies.append(pltpu.make_async_remote_copy(
                src_ref=half(src_ref, slot, c), dst_ref=half(dst_ref, 2 * cx + cy, e) if incoming else half(dst_ref, slot, c),
                send_sem=send_sems.at[2 * j + e], recv_sem=recv_sems.at[2 * j + (e if incoming else c)],
                device_id=(cx, cy, e), device_id_type=MESH))
    return copies


def _gather_start(buf, after, name):
    n_after = len(after)

    def body(*refs):
        buf_ref = refs[0]
        send_sems, recv_sems, out_ref, token = refs[1 + n_after:]
        for cp in _gather_copies(buf_ref, out_ref, send_sems, recv_sems, incoming=False):
            cp.start()
        token[...] = jnp.zeros(token.shape, token.dtype)

    return pl.pallas_call(
        body, in_specs=[_HBM] + [_ANY] * n_after, out_specs=(_SEM, _SEM, _HBM, _VMEM),
        out_shape=(pltpu.SemaphoreType.DMA((6,)), pltpu.SemaphoreType.DMA((6,)), pltpu.HBM(buf.shape, buf.dtype), _TOKEN),
        input_output_aliases={0: 2}, name=name, compiler_params=_split_params(),
    )(pltpu.with_memory_space_constraint(buf, pltpu.HBM), *after)


def _gather_wait(send_sems, recv_sems, buf, after, name):
    def body(buf_ref, send_sems, recv_sems, after_ref, out_ref):
        for cp in _gather_copies(buf_ref, out_ref, send_sems, recv_sems, incoming=False):
            cp.wait_send()
        for cp in _gather_copies(buf_ref, out_ref, send_sems, recv_sems, incoming=True):
            cp.wait_recv()

    return pl.pallas_call(
        body, in_specs=[_HBM, _SEM, _SEM, _ANY], out_specs=_HBM, out_shape=pltpu.HBM(buf.shape, buf.dtype),
        input_output_aliases={0: 0}, name=name, compiler_params=_split_params(),
    )(buf, send_sems, recv_sems, after)


def _relay_copies(src_ref, dst_ref, send_sems, recv_sems, stage, incoming):
    Rh = src_ref.shape[1] // 2
    x, y, c, chips = _place()
    copies = []
    for j, (cx, cy) in enumerate(chips):
        if stage == 0:
            src_slot, src_half, peer = 2 * x + y, c, (cx, cy, c)
            dst_slot, dst_half = (2 * cx + cy, c) if incoming else (src_slot, c)
        else:
            src_slot, src_half, peer = 2 * cx + cy, c, (x, y, 1 - c)
            dst_slot, dst_half = src_slot, (1 - c if incoming else c)
        copies.append(pltpu.make_async_remote_copy(
            src_ref=src_ref.at[src_slot, pl.ds(src_half * Rh, Rh), :],
            dst_ref=dst_ref.at[dst_slot, pl.ds(dst_half * Rh, Rh), :],
            send_sem=send_sems.at[j], recv_sem=recv_sems.at[j], device_id=peer, device_id_type=MESH))
    return copies


def _relay_start(buf, after, name):
    n_after = len(after)

    def body(*refs):
        buf_ref = refs[0]
        send_sems, recv_sems, out_ref, token = refs[1 + n_after:]
        for cp in _relay_copies(buf_ref, out_ref, send_sems, recv_sems, 0, incoming=False):
            cp.start()
        token[...] = jnp.zeros(token.shape, token.dtype)

    return pl.pallas_call(
        body, in_specs=[_HBM] + [_ANY] * n_after, out_specs=(_SEM, _SEM, _HBM, _VMEM),
        out_shape=(pltpu.SemaphoreType.DMA((3,)), pltpu.SemaphoreType.DMA((3,)), pltpu.HBM(buf.shape, buf.dtype), _TOKEN),
        input_output_aliases={0: 2}, name=name, compiler_params=_split_params(),
    )(pltpu.with_memory_space_constraint(buf, pltpu.HBM), *after)


def _relay_pass(send_sems, recv_sems, buf, after, name):
    def body(buf_ref, send0, recv0, after_ref, send1, recv1, out_ref):
        for cp in _relay_copies(buf_ref, out_ref, send0, recv0, 0, incoming=False):
            cp.wait_send()
        for cp in _relay_copies(buf_ref, out_ref, send0, recv0, 0, incoming=True):
            cp.wait_recv()
        for cp in _relay_copies(out_ref, out_ref, send1, recv1, 1, incoming=False):
            cp.start()

    return pl.pallas_call(
        body, in_specs=[_HBM, _SEM, _SEM, _ANY], out_specs=(_SEM, _SEM, _HBM),
        out_shape=(pltpu.SemaphoreType.DMA((3,)), pltpu.SemaphoreType.DMA((3,)), pltpu.HBM(buf.shape, buf.dtype)),
        input_output_aliases={0: 2}, name=name, compiler_params=_split_params(),
    )(buf, send_sems, recv_sems, after)


def _relay_wait(send_sems, recv_sems, buf, after, name):
    def body(buf_ref, send1, recv1, after_ref, out_ref):
        for cp in _relay_copies(buf_ref, out_ref, send1, recv1, 1, incoming=False):
            cp.wait_send()
        for cp in _relay_copies(buf_ref, out_ref, send1, recv1, 1, incoming=True):
            cp.wait_recv()

    return pl.pallas_call(
        body, in_specs=[_HBM, _SEM, _SEM, _ANY], out_specs=_HBM, out_shape=pltpu.HBM(buf.shape, buf.dtype),
        input_output_aliases={0: 0}, name=name, compiler_params=_split_params(),
    )(buf, send_sems, recv_sems, after)


def _scatter_copies(g_ref, land_ref, send_sems, recv_sems, incoming):
    Rh = g_ref.shape[1] // 2
    x, y, c, _ = _place()
    me = 4 * x + 2 * y + c
    copies = []
    for k in range(1, N_DEV):
        px, py, pc = (x + (k >> 2)) % 2, (y + ((k >> 1) & 1)) % 2, (c + (k & 1)) % 2
        copies.append(pltpu.make_async_remote_copy(
            src_ref=g_ref.at[2 * px + py, pl.ds(pc * Rh, Rh), :],
            dst_ref=land_ref.at[4 * px + 2 * py + pc if incoming else me],
            send_sem=send_sems.at[k - 1], recv_sem=recv_sems.at[k - 1], device_id=(px, py, pc), device_id_type=MESH))
    return copies


def _scatter_start(g, name):
    ns, R, C = g.shape

    def body(g_ref, land_ref, send_sems, recv_sems, g_thru, land_thru, token):
        for cp in _scatter_copies(g_ref, land_thru, send_sems, recv_sems, incoming=False):
            cp.start()
        token[...] = jnp.zeros(token.shape, token.dtype)

    land = lax.empty((N_DEV, R // 2, C), g.dtype)
    n = N_DEV - 1
    return pl.pallas_call(
        body, in_specs=[_HBM, _HBM], out_specs=(_SEM, _SEM, _HBM, _HBM, _VMEM),
        out_shape=(pltpu.SemaphoreType.DMA((n,)), pltpu.SemaphoreType.DMA((n,)), pltpu.HBM(g.shape, g.dtype),
                   pltpu.HBM(land.shape, land.dtype), _TOKEN),
        input_output_aliases={0: 2, 1: 3}, name=name, compiler_params=_split_params(),
    )(pltpu.with_memory_space_constraint(g, pltpu.HBM), pltpu.with_memory_space_constraint(land, pltpu.HBM))


def _scatter_wait(send_sems, recv_sems, g, land, after, name):
    def body(g_ref, land_ref, send_sems, recv_sems, after_ref, g_out, land_out):
        for cp in _scatter_copies(g_ref, land_out, send_sems, recv_sems, incoming=False):
            cp.wait_send()
        for cp in _scatter_copies(g_ref, land_out, send_sems, recv_sems, incoming=True):
            cp.wait_recv()

    return pl.pallas_call(
        body, in_specs=[_HBM, _HBM, _SEM, _SEM, _ANY], out_specs=(_HBM, _HBM),
        out_shape=(pltpu.HBM(g.shape, g.dtype), pltpu.HBM(land.shape, land.dtype)),
        input_output_aliases={0: 0, 1: 1}, name=name, compiler_params=_split_params(),
    )(g, land, send_sems, recv_sems, after)


def _device_sum(land, g, layer, n_layers, prev, name):
    nd, Rh, C = land.shape
    br = _row_block(Rh)
    nb = Rh // br
    core = lambda: lax.axis_index("c")
    me = lambda: 2 * _slot() + core()

    def body(*refs):
        own = refs[nd][...]
        acc = None
        for d in range(nd):
            t = jnp.where(me() == d, own, refs[d][...]).astype(F32)
            acc = t if acc is None else acc + t
        refs[-1][...] = acc

    def piece(d):
        return pl.BlockSpec((None, br, C), lambda i: (jnp.where(me() == d, (d + 1) % nd, d), i, 0))

    ins = [land] * nd + [g] + ([prev] if prev is not None else [])
    return pl.pallas_call(
        body, grid=(nb,),
        in_specs=[piece(d) for d in range(nd)]
        + [pl.BlockSpec((None, br, C), lambda i: (_slot(), core() * nb + i, 0))]
        + ([_ANY] if prev is not None else []),
        out_specs=pl.BlockSpec((None, br, C), lambda i: (layer, core() * nb + i, 0)),
        out_shape=jax.ShapeDtypeStruct((n_layers, 2 * Rh, C), F32),
        input_output_aliases={nd + 1: 0} if prev is not None else {},
        name=name, compiler_params=_cparams(("parallel",)),
    )(*ins)


def _join_copy(src_ref, dst_ref, layer, send_sem, recv_sem, incoming):
    Rh = src_ref.shape[1] // 2
    x, y, c, _ = _place()
    mine, other = pl.ds(c * Rh, Rh), pl.ds((1 - c) * Rh, Rh)
    return pltpu.make_async_remote_copy(src_ref=src_ref.at[layer, mine, :],
                                        dst_ref=dst_ref.at[layer, other if incoming else mine, :],
                                        send_sem=send_sem, recv_sem=recv_sem, device_id=(x, y, 1 - c),
                                        device_id_type=MESH)


def _join_start(g, layer, name):
    def body(g_ref, send_sem, recv_sem, out_ref, token):
        _join_copy(g_ref, out_ref, layer, send_sem, recv_sem, incoming=False).start()
        token[...] = jnp.zeros(token.shape, token.dtype)

    return pl.pallas_call(
        body, in_specs=[_HBM], out_specs=(_SEM, _SEM, _HBM, _VMEM),
        out_shape=(pltpu.SemaphoreType.DMA(()), pltpu.SemaphoreType.DMA(()), pltpu.HBM(g.shape, g.dtype), _TOKEN),
        input_output_aliases={0: 2}, name=name, compiler_params=_split_params(),
    )(pltpu.with_memory_space_constraint(g, pltpu.HBM))


def _join_wait(send_sem, recv_sem, g, layer, after, name):
    def body(g_ref, send_sem, recv_sem, after_ref, out_ref):
        _join_copy(g_ref, out_ref, layer, send_sem, recv_sem, incoming=False).wait_send()
        _join_copy(g_ref, out_ref, layer, send_sem, recv_sem, incoming=True).wait_recv()

    return pl.pallas_call(
        body, in_specs=[_HBM, _SEM, _SEM, _ANY], out_specs=_HBM, out_shape=pltpu.HBM(g.shape, g.dtype),
        input_output_aliases={0: 0}, name=name, compiler_params=_split_params(),
    )(g, send_sem, recv_sem, after)


def _all_reduce_small(v, name):
    rows, cols = v.shape

    def body(v_ref, out_ref, buf, send_sems, recv_sems):
        x, y, c, _ = _place()
        me = 4 * x + 2 * y + c
        buf[me] = v_ref[...]
        peers = []
        for k in range(1, N_DEV):
            px, py, pc = (x + (k >> 2)) % 2, (y + ((k >> 1) & 1)) % 2, (c + (k & 1)) % 2
            peers.append((px, py, pc))
        sends = []
        for k, peer in enumerate(peers):
            cp = pltpu.make_async_remote_copy(src_ref=v_ref, dst_ref=buf.at[me], send_sem=send_sems.at[k],
                                              recv_sem=recv_sems.at[k], device_id=peer, device_id_type=MESH)
            cp.start()
            sends.append(cp)
        for k, (px, py, pc) in enumerate(peers):
            pltpu.make_async_remote_copy(src_ref=v_ref, dst_ref=buf.at[4 * px + 2 * py + pc], send_sem=send_sems.at[k],
                                         recv_sem=recv_sems.at[k], device_id=(px, py, pc),
                                         device_id_type=MESH).wait_recv()
        for cp in sends:
            cp.wait_send()
        acc = buf[0]
        for i in range(1, N_DEV):
            acc = acc + buf[i]
        out_ref[...] = acc

    vmem = pl.BlockSpec(memory_space=pltpu.VMEM)
    return pl.pallas_call(
        body, in_specs=[vmem], out_specs=vmem, out_shape=jax.ShapeDtypeStruct((rows, cols), F32),
        scratch_shapes=[pltpu.VMEM((N_DEV, rows, cols), F32), pltpu.SemaphoreType.DMA((N_DEV - 1,)),
                        pltpu.SemaphoreType.DMA((N_DEV - 1,))],
        name=name, compiler_params=pltpu.CompilerParams(),
    )(v)


def _reduce_scatter_sum(started, after, layer, n_layers, prev, tag):
    send_sems, recv_sems, g, land, _ = started
    g, land = _scatter_wait(send_sems, recv_sems, g, land, after, f"rs_wait_{tag}")
    f = _device_sum(land, g, layer, n_layers, prev, f"rs_sum_{tag}")
    return _join_start(f, layer, f"rs_join_start_{tag}")


def _split_w_in(wg, Hf, name):
    ns, D, cols = wg.shape
    a = 3 * Hf * HEAD_DIM
    n6 = ns * cols - Hf
    br = _row_block(D)

    def body(w_ref, w6_ref, wf_ref):
        nat = jnp.concatenate([w_ref[s] for s in range(ns)], axis=1)
        w6_ref[...] = jnp.concatenate([nat[:, :a], nat[:, a + Hf:]], axis=1)
        wf_ref[...] = nat[:, a:a + Hf]

    w6, wf = pl.pallas_call(
        body, grid=(D // br,), in_specs=[pl.BlockSpec((ns, br, cols), lambda i: (0, i, 0))],
        out_specs=[pl.BlockSpec((br, n6), lambda i: (i, 0)), pl.BlockSpec((br, Hf), lambda i: (i, 0))],
        out_shape=[jax.ShapeDtypeStruct((D, n6), wg.dtype), jax.ShapeDtypeStruct((D, Hf), wg.dtype)],
        name=name, compiler_params=_cparams(("parallel",)),
    )(wg)
    return w6, wf.T


def _join_dw_in(dw6, dwf_t, Hf, name):
    D, n6 = dw6.shape
    a = 3 * Hf * HEAD_DIM
    cols = (n6 + Hf) // N_CHIPS
    br = _row_block(D)

    def body(w6_ref, wf_ref, out_ref):
        w6 = w6_ref[...]
        nat = jnp.concatenate([w6[:, :a], wf_ref[...], w6[:, a:]], axis=1)
        for s in range(N_CHIPS):
            out_ref[s] = nat[:, s * cols:(s + 1) * cols]

    return pl.pallas_call(
        body, grid=(D // br,),
        in_specs=[pl.BlockSpec((br, n6), lambda i: (i, 0)), pl.BlockSpec((br, Hf), lambda i: (i, 0))],
        out_specs=pl.BlockSpec((N_CHIPS, br, cols), lambda i: (0, i, 0)),
        out_shape=jax.ShapeDtypeStruct((N_CHIPS, D, cols), dw6.dtype),
        name=name, compiler_params=_cparams(("parallel",)),
    )(dw6, dwf_t.T.astype(dw6.dtype))


def _tied(v, tokens, name):
    return _tie(v, tokens, name) if tokens else v


def _layer_fwd(x, p, weight, bias, tokens, tag):
    Hf, Hd = p["forget_b"].shape[0], bias.shape[1]
    h1 = _rms_fwd(x, _tied(p["norm1_g"], tokens, f"tie_norm1_{tag}"), f"norm1_{tag}")
    w6, wf_t = _split_w_in(weight("w_in", h1), Hf, f"split_w_in_{tag}")
    n_a = 3 * Hf * HEAD_DIM
    proj_a = _mm_nn(h1, w6, f"proj_a_{tag}", [CDT], epi=lambda acc: (acc,), b_cols=(0, n_a))[0]
    proj_b = _mm_nn(h1, w6, f"proj_b_{tag}", [F32], b_cols=(n_a, w6.shape[1] - n_a))[0]
    f_t = _mm_nt(wf_t, h1, f"fproj_{tag}", [F32])[0]
    qc, kc = _fox_bias_operands(_gates_fwd(f_t, p["forget_b"], f"gates_{tag}"), f"fox_operands_{tag}")
    y_a, lse_a = _fox_fwd(proj_a, qc, kc, Hf, f"fox_{tag}")
    y_b, lse_b = _dil_fwd(proj_b, bias, Hd, f"dil_{tag}")
    mixed = _pair_norm_fwd(y_a, y_b, p["outnorm_a_g"], p["outnorm_b_g"], f"norm_ab_{tag}")
    w_out = weight("w_out", mixed)
    w_out = w_out.reshape(-1, w_out.shape[2])
    x1 = _mm_nn(mixed, w_out, f"attn_out_{tag}", [F32], extras=[x])[0]
    h2 = _rms_fwd(x1, p["norm2_g"], f"norm2_{tag}")
    w_mi = weight("w_mlp_in", h2)
    u, act = _mm_nn(h2, w_mi, f"mlp_in_{tag}", [CDT, CDT], b_slots=True,
                    epi=lambda acc: (acc, jnp.square(jnp.maximum(acc, 0.0))))
    w_mo = weight("w_mlp_out", act)
    w_mo = w_mo.reshape(-1, w_mo.shape[2])
    x2 = _mm_nn(act, w_mo, f"mlp_out_{tag}", [F32], extras=[x1])[0]
    saved = dict(x=x, h1=h1, proj_a=proj_a, proj_b=proj_b, f_t=f_t, qc=qc, kc=kc, y_a=y_a, lse_a=lse_a, y_b=y_b,
                 lse_b=lse_b, mixed=mixed, x1=x1, h2=h2, u=u, act=act, w6=w6, wf_t=wf_t, w_out=w_out, w_mi=w_mi,
                 w_mo=w_mo)
    return x2, saved


def _layer_bwd(dx2, dx2c, p, send, bias, sv, defer_w_out, tag):
    Hf, Hd = p["forget_b"].shape[0], bias.shape[1]
    E = HEAD_DIM
    rows = lambda g: g.reshape(N_CHIPS, -1, g.shape[1])
    du = _mm_nt(dx2c, sv["w_mo"], f"d_act_{tag}", [CDT], extras=[sv["u"]],
                epi=lambda acc, u: (acc * (2.0 * jnp.maximum(u.astype(F32), 0.0)),))[0]
    tokens = send("w_mlp_out", rows(_mm_tn(sv["act"], dx2c, f"dw_mlp_out_{tag}", CDT)))
    dh2 = _mm_nt(du, sv["w_mi"], f"d_h2_{tag}", [F32], b_slots=True)[0]
    tokens = tokens + send("w_mlp_in", _mm_tn(sv["h2"], du, f"dw_mlp_in_{tag}", CDT, out_slots=N_CHIPS))
    dx1, dx1c, g_norm2 = _rms_bwd(sv["x1"], _tied(p["norm2_g"], tokens, f"tie_norm2_{tag}"), dh2, dx2,
                                  f"d_norm2_{tag}")
    dmixed = _mm_nt(dx1c, sv["w_out"], f"d_mixed_{tag}", [F32])[0]
    send_w_out = lambda: send("w_out", rows(_mm_tn(sv["mixed"], dx1c, f"dw_out_{tag}", CDT)))
    tokens = [] if defer_w_out else send_w_out()
    dy_a, dy_b, g_na, g_nb = _pair_norm_bwd(sv["y_a"], sv["y_b"], _tied(p["outnorm_a_g"], tokens, f"tie_norm_a_{tag}"),
                                            p["outnorm_b_g"], dmixed, f"d_norm_ab_{tag}")
    dq_a, dcq, dk_a, dv_a, dck = _fox_bwd(sv["proj_a"], sv["qc"], sv["kc"], sv["lse_a"], sv["y_a"], dy_a, Hf,
                                          f"fox_bwd_{tag}")
    df, dfc, g_fb = _gates_bwd(sv["f_t"], p["forget_b"], dcq[:, ::E].T, dck.reshape(Hf, -1), f"d_gates_{tag}")
    dq_b, dk_b, dv_b, dbias = _dil_bwd(sv["proj_b"], bias, sv["y_b"], dy_b, sv["lse_b"], Hd, f"dil_bwd_{tag}")
    dproj = [dq_a, dk_a, dv_a, dq_b, dk_b, dv_b]
    g_w6 = _mm_tn_parts(sv["h1"], dproj, f"dw_in_{tag}", CDT)
    g_wf_t = _mm_nn(dfc, sv["h1"], f"dw_f_{tag}", [F32])[0]
    tokens = send("w_in", _join_dw_in(g_w6, g_wf_t, Hf, f"join_dw_in_{tag}"))
    dh1_f = _mm_tn(dfc, _tied(sv["wf_t"], tokens, f"tie_wf_{tag}"), f"d_h1_f_{tag}", F32)
    dh1 = _mm_nt_parts(dproj, sv["w6"], dh1_f, f"d_h1_{tag}")
    dx, dxc, g_norm1 = _rms_bwd(sv["x"], p["norm1_g"], dh1, dx1, f"d_norm1_{tag}")
    grads = dict(norm1_g=g_norm1[0], norm2_g=g_norm2[0], outnorm_a_g=g_na[0], outnorm_b_g=g_nb[0],
                 forget_b=g_fb[:, 0], dbias=dbias)
    return dx, dxc, grads, (send_w_out if defer_w_out else None)


_LAYER_SMALL = ("norm1_g", "forget_b", "outnorm_a_g", "outnorm_b_g", "norm2_g")


def _local_step(x, target, small, weight, send, tokens):
    depth = small["norm1_g"].shape[0]
    buckets = _bucket_table()
    bias = _bias_table(small["rel_bias"], buckets, "bias_table")
    layers, saved = [], []
    for l in range(depth):
        p = {k: small[k][l] for k in _LAYER_SMALL}
        layers.append(p)
        x, sv = _layer_fwd(x, p, functools.partial(weight, l), bias, tokens if l == 0 else [], f"l{l}")
        saved.append(sv)
    dx, dxc, g_final, loss = _loss_bwd(x, small["final_norm_g"], target, "loss")
    layer_grads = [None] * depth
    for l in reversed(range(depth)):
        dx, dxc, layer_grads[l], last = _layer_bwd(dx, dxc, layers[l], functools.partial(send, l), bias, saved[l],
                                                   l == 0, f"l{l}")
    tokens = last()
    dbias = functools.reduce(jnp.add, [g["dbias"] for g in layer_grads])
    g_rel = _bias_table_bwd(dbias, buckets, "d_bias_table")[:, 0, :].T
    small_grads = dict(final_norm_g=g_final[0], rel_bias=g_rel,
                       **{k: jnp.stack([g[k] for g in layer_grads]) for k in _LAYER_SMALL})
    return loss[0, 0], dx, small_grads, tokens


_BIG = ("w_in", "w_out", "w_mlp_in", "w_mlp_out")
_SMALL = ("norm1_g", "forget_b", "rel_bias", "outnorm_a_g", "outnorm_b_g", "norm2_g", "final_norm_g")
_ORDER = ("norm1_g", "w_in", "forget_b", "rel_bias", "outnorm_a_g", "outnorm_b_g", "w_out", "norm2_g", "w_mlp_in",
          "w_mlp_out", "final_norm_g")


def _pack_small(d):
    flat = jnp.concatenate([d[k].reshape(-1) for k in _SMALL])
    rows = -(-flat.shape[0] // (8 * SMALL_COLS)) * 8
    return jnp.pad(flat, (0, rows * SMALL_COLS - flat.shape[0])).reshape(rows, SMALL_COLS)


def _unpack_small(packed, like):
    flat, out, at = packed.reshape(-1), {}, 0
    for k in _SMALL:
        n = like[k].size
        out[k] = flat[at:at + n].reshape(like[k].shape)
        at += n
    return out


def kernel(x, norm1_g, w_in, forget_b, rel_bias, outnorm_a_g, outnorm_b_g, w_out, norm2_g, w_mlp_in, w_mlp_out, final_norm_g, loss_target, m_norm1_g, m_w_in, m_forget_b, m_rel_bias, m_outnorm_a_g, m_outnorm_b_g, m_w_out, m_norm2_g, m_w_mlp_in, m_w_mlp_out, m_final_norm_g, v_norm1_g, v_w_in, v_forget_b, v_rel_bias, v_outnorm_a_g, v_outnorm_b_g, v_w_out, v_norm2_g, v_w_mlp_in, v_w_mlp_out, v_final_norm_g):
    w = dict(norm1_g=norm1_g, w_in=w_in, forget_b=forget_b, rel_bias=rel_bias, outnorm_a_g=outnorm_a_g,
             outnorm_b_g=outnorm_b_g, w_out=w_out, norm2_g=norm2_g, w_mlp_in=w_mlp_in, w_mlp_out=w_mlp_out,
             final_norm_g=final_norm_g)
    m = dict(norm1_g=m_norm1_g, w_in=m_w_in, forget_b=m_forget_b, rel_bias=m_rel_bias, outnorm_a_g=m_outnorm_a_g,
             outnorm_b_g=m_outnorm_b_g, w_out=m_w_out, norm2_g=m_norm2_g, w_mlp_in=m_w_mlp_in,
             w_mlp_out=m_w_mlp_out, final_norm_g=m_final_norm_g)
    v = dict(norm1_g=v_norm1_g, w_in=v_w_in, forget_b=v_forget_b, rel_bias=v_rel_bias, outnorm_a_g=v_outnorm_a_g,
             outnorm_b_g=v_outnorm_b_g, w_out=v_w_out, norm2_g=v_norm2_g, w_mlp_in=v_w_mlp_in,
             w_mlp_out=v_w_mlp_out, final_norm_g=v_final_norm_g)
    depth = w_in.shape[0]
    small = {k: w[k] for k in _SMALL}

    gathers, passed, tokens = {}, {}, []
    for l in range(depth):
        for k in _BIG:
            buf = _cast_into_slot(w[k], l, f"cast_{k}_l{l}")
            start = _relay_start if k == _BIG[0] else _gather_start
            send_sems, recv_sems, buf, token = start(buf, tokens, f"gather_start_{k}_l{l}")
            gathers[l, k], tokens = (send_sems, recv_sems, buf), [token]

    def weight(l, k, after):
        if k == _BIG[-1] and l + 1 < depth:
            passed[l + 1] = _relay_pass(*gathers[l + 1, _BIG[0]], after, f"gather_pass_{_BIG[0]}_l{l + 1}")
        if k != _BIG[0]:
            return _gather_wait(*gathers[l, k], after, f"gather_wait_{k}_l{l}")
        if l not in passed:
            passed[l] = _relay_pass(*gathers[l, k], after, f"gather_pass_{k}_l{l}")
        return _relay_wait(*passed[l], after, f"gather_wait_{k}_l{l}")

    scatters = {}

    def send(l, k, g):
        scatters[l, k] = _scatter_start(g, f"rs_start_{k}_l{l}")
        return [scatters[l, k][4]]

    loss, grad_x, small_grads, tokens = _local_step(x[0], loss_target[0], small, weight, send, tokens)
    loss = lax.psum(loss, ("x", "y", "c"))

    grads, delta, new_m, new_v = {}, {}, {}, {}
    packed = _tied(_pack_small(small_grads), tokens, "tie_small")
    after, seen, joining = packed, {k: 0 for k in _BIG}, None

    def joined(after):
        (l, k), (send_sem, recv_sem, g) = joining
        grads[k] = _join_wait(send_sem, recv_sem, g, l, after, f"rs_join_wait_{k}_l{l}")
        seen[k] += 1
        if seen[k] < depth:
            return after
        shape = w[k].shape
        flat = lambda t: t.reshape(-1, shape[-1])
        outs = _adamw(flat(w[k]), flat(grads[k]), flat(m[k]), flat(v[k]), f"adamw_{k}")
        grads[k], delta[k], new_m[k], new_v[k] = (t.reshape(shape) for t in outs)
        return outs[1]

    for (l, k), started in scatters.items():
        assert joining is None or joining[0][1] != k
        send_sem, recv_sem, g, token = _reduce_scatter_sum(started, after, l, depth, grads.get(k), f"{k}_l{l}")
        if joining is not None:
            after = joined(token)
        joining = ((l, k), (send_sem, recv_sem, g))
    after = joined(after)
    small_sums = _all_reduce_small(_tied(packed, [after], "tie_small_sums"), "small_all_reduce")
    grads.update(_unpack_small(small_sums, small))
    _, d_, m_, v_ = _adamw(_pack_small(small), _pack_small({k: grads[k] for k in _SMALL}),
                           _pack_small({k: m[k] for k in _SMALL}), _pack_small({k: v[k] for k in _SMALL}), "adamw_small")
    delta.update(_unpack_small(d_, small))
    new_m.update(_unpack_small(m_, small))
    new_v.update(_unpack_small(v_, small))

    return (loss, grad_x[None], *[grads[k] for k in _ORDER], *[delta[k] for k in _ORDER],
            *[new_m[k] for k in _ORDER], *[new_v[k] for k in _ORDER])
```

```python
import functools

import jax
import jax.numpy as jnp
from jax import lax
from jax.experimental import pallas as pl
from jax.experimental.pallas import tpu as pltpu

F32 = jnp.float32
CDT = jnp.bfloat16
HEAD_DIM = 128
NORM_EPS = 1e-6
NEG_INF = -1e30
LOG2E = 1.4426950408889634
REL_BUCKETS = 32
REL_MAX_DISTANCE = 2048
DIL_PATTERNS = ((128, 1), (512, 4), (2048, 16))
DIL_BLOCK = 128
ADAM_LR, ADAM_B1, ADAM_B2, ADAM_EPS, ADAM_WD, ADAM_STEP = 0.001, 0.9, 0.999, 1e-08, 0.01, 10
N_CHIPS = 4
N_DEV = 8
VMEM_LIMIT_BYTES = 56 * 1024 * 1024
SMALL_COLS = 1024
MESH = pl.DeviceIdType.MESH


def _cparams(sem=None):
    return pltpu.CompilerParams(dimension_semantics=sem, vmem_limit_bytes=VMEM_LIMIT_BYTES)


def _tile(dim, pref):
    t = min(pref, dim)
    t -= t % 128
    while t >= 128:
        if dim % t == 0:
            return t
        t -= 128
    return dim


def _rowwise(fn, ins, out_dtypes, name, bs=512, consts=()):
    R, C = ins[0].shape
    bs = min(bs, R)
    n_in, n_c = len(ins), len(consts)

    def body(*refs):
        vals = [r[...] for r in refs[:n_in + n_c]]
        res = fn(*vals)
        for o, r in zip(refs[n_in + n_c:], res):
            o[...] = r.astype(o.dtype)

    row = pl.BlockSpec((bs, C), lambda i: (i, 0))
    return pl.pallas_call(
        body, grid=(R // bs,),
        in_specs=[row] * n_in + [pl.BlockSpec((1, c.shape[-1]), lambda i: (0, 0)) for c in consts],
        out_specs=[row] * len(out_dtypes),
        out_shape=[jax.ShapeDtypeStruct((R, C), d) for d in out_dtypes],
        name=name, compiler_params=_cparams(("parallel",)),
    )(*ins, *[c.reshape(1, -1) for c in consts])


def _rms_fwd(x, g, name):
    def fn(xf, gg):
        r = lax.rsqrt(jnp.mean(xf * xf, axis=-1, keepdims=True) + NORM_EPS)
        return ((xf * r) * gg,)
    return _rowwise(fn, [x], [CDT], name, consts=[g])[0]


def _rms_bwd(x, g, dh, dres, name, bs=512):
    S, D = x.shape
    bs = min(bs, S)
    has_res = dres is not None

    def body(*refs):
        x_ref, g_ref, dh_ref = refs[:3]
        dx_ref, dxc_ref, dg_ref = refs[-3:]
        xf = x_ref[...]
        r = lax.rsqrt(jnp.mean(xf * xf, axis=-1, keepdims=True) + NORM_EPS)
        xhat = xf * r
        dh_ = dh_ref[...].astype(F32)
        dxhat = dh_ * g_ref[...]
        dx = r * (dxhat - xhat * jnp.mean(dxhat * xhat, axis=-1, keepdims=True))
        if has_res:
            dx = dx + refs[3][...]
        dx_ref[...] = dx
        dxc_ref[...] = dx.astype(dxc_ref.dtype)
        part = jnp.sum(dh_ * xhat, axis=0, keepdims=True)

        @pl.when(pl.program_id(0) == 0)
        def _():
            dg_ref[...] = part

        @pl.when(pl.program_id(0) > 0)
        def _():
            dg_ref[...] += part

    row = pl.BlockSpec((bs, D), lambda i: (i, 0))
    one = pl.BlockSpec((1, D), lambda i: (0, 0))
    ins = [x, g.reshape(1, D), dh] + ([dres] if has_res else [])
    return pl.pallas_call(
        body, grid=(S // bs,),
        in_specs=[row, one, row] + ([row] if has_res else []),
        out_specs=[row, row, one],
        out_shape=[jax.ShapeDtypeStruct((S, D), F32), jax.ShapeDtypeStruct((S, D), CDT),
                   jax.ShapeDtypeStruct((1, D), F32)],
        name=name, compiler_params=_cparams(("arbitrary",)),
    )(*ins)


def _pair_norm_fwd(y_a, y_b, g_a, g_b, name, bs=512):
    S, Da = y_a.shape
    Db = y_b.shape[1]
    bs = min(bs, S)

    def body(a_ref, b_ref, ga_ref, gb_ref, o_ref):
        def norm(x, g):
            r = lax.rsqrt(jnp.mean(x * x, axis=-1, keepdims=True) + NORM_EPS)
            return ((x * r) * g).astype(o_ref.dtype)
        o_ref[:, :Da] = norm(a_ref[...], ga_ref[...])
        o_ref[:, Da:] = norm(b_ref[...], gb_ref[...])

    row = lambda n: pl.BlockSpec((bs, n), lambda i: (i, 0))
    one = lambda n: pl.BlockSpec((1, n), lambda i: (0, 0))
    return pl.pallas_call(
        body, grid=(S // bs,), in_specs=[row(Da), row(Db), one(Da), one(Db)], out_specs=row(Da + Db),
        out_shape=jax.ShapeDtypeStruct((S, Da + Db), CDT), name=name, compiler_params=_cparams(("parallel",)),
    )(y_a, y_b, g_a.reshape(1, Da), g_b.reshape(1, Db))


def _pair_norm_bwd(y_a, y_b, g_a, g_b, dmixed, name, bs=512):
    S, Da = y_a.shape
    Db = y_b.shape[1]
    bs = min(bs, S)

    def body(a_ref, b_ref, ga_ref, gb_ref, dm_ref, da_ref, db_ref, dga_ref, dgb_ref):
        def one(x_ref, g_ref, dh, dx_ref, dg_ref):
            xf = x_ref[...]
            r = lax.rsqrt(jnp.mean(xf * xf, axis=-1, keepdims=True) + NORM_EPS)
            xhat = xf * r
            dxhat = dh * g_ref[...]
            dx_ref[...] = r * (dxhat - xhat * jnp.mean(dxhat * xhat, axis=-1, keepdims=True))
            part = jnp.sum(dh * xhat, axis=0, keepdims=True)

            @pl.when(pl.program_id(0) == 0)
            def _():
                dg_ref[...] = part

            @pl.when(pl.program_id(0) > 0)
            def _():
                dg_ref[...] += part

        dm = dm_ref[...]
        one(a_ref, ga_ref, dm[:, :Da], da_ref, dga_ref)
        one(b_ref, gb_ref, dm[:, Da:], db_ref, dgb_ref)

    row = lambda n: pl.BlockSpec((bs, n), lambda i: (i, 0))
    one_ = lambda n: pl.BlockSpec((1, n), lambda i: (0, 0))
    return pl.pallas_call(
        body, grid=(S // bs,), in_specs=[row(Da), row(Db), one_(Da), one_(Db), row(Da + Db)],
        out_specs=[row(Da), row(Db), one_(Da), one_(Db)],
        out_shape=[jax.ShapeDtypeStruct((S, Da), F32), jax.ShapeDtypeStruct((S, Db), F32),
                   jax.ShapeDtypeStruct((1, Da), F32), jax.ShapeDtypeStruct((1, Db), F32)],
        name=name, compiler_params=_cparams(("arbitrary",)),
    )(y_a, y_b, g_a.reshape(1, Da), g_b.reshape(1, Db), dmixed)


def _loss_bwd(x, g, target, name, bs=512):
    S, D = x.shape
    bs = min(bs, S)

    def body(x_ref, g_ref, t_ref, dx_ref, dxc_ref, dg_ref, loss_ref):
        xf = x_ref[...]
        r = lax.rsqrt(jnp.mean(xf * xf, axis=-1, keepdims=True) + NORM_EPS)
        xhat = xf * r
        err = xhat * g_ref[...] - t_ref[...]
        lpart = 0.5 * jnp.sum(jnp.mean(err * err, axis=-1, keepdims=True), axis=0, keepdims=True)
        dy = err / D
        dxhat = dy * g_ref[...]
        dx = r * (dxhat - xhat * jnp.mean(dxhat * xhat, axis=-1, keepdims=True))
        dx_ref[...] = dx
        dxc_ref[...] = dx.astype(dxc_ref.dtype)
        gpart = jnp.sum(dy * xhat, axis=0, keepdims=True)

        @pl.when(pl.program_id(0) == 0)
        def _():
            dg_ref[...] = gpart
            loss_ref[...] = lpart

        @pl.when(pl.program_id(0) > 0)
        def _():
            dg_ref[...] += gpart
            loss_ref[...] += lpart

    row = pl.BlockSpec((bs, D), lambda i: (i, 0))
    one = pl.BlockSpec((1, D), lambda i: (0, 0))
    return pl.pallas_call(
        body, grid=(S // bs,),
        in_specs=[row, one, row],
        out_specs=[row, row, one, pl.BlockSpec((1, 1), lambda i: (0, 0))],
        out_shape=[jax.ShapeDtypeStruct((S, D), F32), jax.ShapeDtypeStruct((S, D), CDT),
                   jax.ShapeDtypeStruct((1, D), F32), jax.ShapeDtypeStruct((1, 1), F32)],
        name=name, compiler_params=_cparams(("arbitrary",)),
    )(x, g.reshape(1, D), target)


_NN = (((1,), (0,)), ((), ()))
_NT = (((1,), (1,)), ((), ()))
_TN = (((0,), (0,)), ((), ()))


def _mm(a, b, *, M, N, K, a_spec, b_spec, o_spec, dims, tm, tn, tk, name, out_shapes, extras=(), epi=None):
    nk = K // tk
    n_ex, n_out = len(extras), len(out_shapes)
    in_place = epi is None
    if in_place:
        assert n_out == 1 and n_ex <= 1 and out_shapes[0].dtype == F32
        epi = lambda acc, *r: (acc + r[0] if r else acc,)

    def body(*refs):
        a_ref, b_ref = refs[0], refs[1]
        ex = refs[2:2 + n_ex]
        outs = refs[2 + n_ex:2 + n_ex + n_out]
        part = lax.dot_general(a_ref[...], b_ref[...], dims, preferred_element_type=F32)

        def finish(acc):
            for o, r in zip(outs, epi(acc, *[e[...] for e in ex])):
                o[...] = r.astype(o.dtype)

        if nk == 1:
            finish(part)
        elif in_place:
            k = pl.program_id(2)

            @pl.when(k == 0)
            def _():
                finish(part)

            @pl.when(k > 0)
            def _():
                outs[0][...] += part
        else:
            acc_ref = refs[-1]
            k = pl.program_id(2)

            @pl.when(k == 0)
            def _():
                acc_ref[...] = part

            @pl.when(k > 0)
            def _():
                acc_ref[...] += part

            @pl.when(k == nk - 1)
            def _():
                finish(acc_ref[...])

    ex_spec = pl.BlockSpec((tm, tn), lambda i, j, k: (i, j))
    return pl.pallas_call(
        body, grid=(M // tm, N // tn, nk),
        in_specs=[a_spec, b_spec] + [ex_spec] * n_ex,
        out_specs=[o_spec] * n_out,
        out_shape=out_shapes,
        scratch_shapes=[pltpu.VMEM((tm, tn), F32)] if nk > 1 and not in_place else [],
        name=name, compiler_params=_cparams(("parallel", "parallel", "arbitrary")),
    )(a, b, *extras)


def _mm_tiles(K):
    return (2048, 512, 2048) if K <= 2048 else (1024, 1024, 2048)


def _mm_nn(a, b, name, out_dtypes, extras=(), epi=None, b_slots=False, b_cols=None):
    M, K = a.shape
    tm, tn, tk = _mm_tiles(K)
    if b_slots:
        ns, _, Ns = b.shape
        N = ns * Ns
        tn = _tile(Ns, tn)
        npb = Ns // tn
        tk_ = _tile(K, tk)
        b_spec = pl.BlockSpec((None, tk_, tn), lambda i, j, k: (j // npb, k, j % npb))
    else:
        first, N = b_cols if b_cols is not None else (0, b.shape[1])
        tn = _tile(N, tn)
        assert first % tn == 0
        tk_ = _tile(K, tk)
        b_spec = pl.BlockSpec((tk_, tn), lambda i, j, k: (k, first // tn + j))
    tm = _tile(M, tm)
    return _mm(a, b, M=M, N=N, K=K, a_spec=pl.BlockSpec((tm, tk_), lambda i, j, k: (i, k)), b_spec=b_spec,
               o_spec=pl.BlockSpec((tm, tn), lambda i, j, k: (i, j)), dims=_NN, tm=tm, tn=tn, tk=tk_, name=name,
               out_shapes=[jax.ShapeDtypeStruct((M, N), d) for d in out_dtypes], extras=extras, epi=epi)


def _mm_nt(a, b, name, out_dtypes, extras=(), epi=None, b_slots=False):
    M, K = a.shape
    tm, tn, tk = _mm_tiles(K)
    tm = _tile(M, tm)
    if b_slots:
        ns, N, Ks = b.shape
        tk_ = _tile(Ks, tk)
        kpb = Ks // tk_
        tn = _tile(N, tn)
        b_spec = pl.BlockSpec((None, tn, tk_), lambda i, j, k: (k // kpb, j, k % kpb))
    else:
        N = b.shape[0]
        tk_ = _tile(K, tk)
        tn = _tile(N, tn)
        b_spec = pl.BlockSpec((tn, tk_), lambda i, j, k: (j, k))
    return _mm(a, b, M=M, N=N, K=K, a_spec=pl.BlockSpec((tm, tk_), lambda i, j, k: (i, k)), b_spec=b_spec,
               o_spec=pl.BlockSpec((tm, tn), lambda i, j, k: (i, j)), dims=_NT, tm=tm, tn=tn, tk=tk_, name=name,
               out_shapes=[jax.ShapeDtypeStruct((M, N), d) for d in out_dtypes], extras=extras, epi=epi)


def _mm_tn(a, b, name, out_dtype, out_slots=0, tm=2048, tn=1024, tk=2048):
    K, M = a.shape
    N = b.shape[1]
    tm, tk_ = _tile(M, tm), _tile(K, tk)
    if out_slots:
        Ns = N // out_slots
        tn = _tile(Ns, tn)
        npb = Ns // tn
        o_spec = pl.BlockSpec((None, tm, tn), lambda i, j, k: (j // npb, i, j % npb))
        out_shape = jax.ShapeDtypeStruct((out_slots, M, Ns), out_dtype)
    else:
        tn = _tile(N, tn)
        o_spec = pl.BlockSpec((tm, tn), lambda i, j, k: (i, j))
        out_shape = jax.ShapeDtypeStruct((M, N), out_dtype)
    return _mm(a, b, M=M, N=N, K=K, a_spec=pl.BlockSpec((tk_, tm), lambda i, j, k: (k, i)),
               b_spec=pl.BlockSpec((tk_, tn), lambda i, j, k: (k, j)), o_spec=o_spec, dims=_TN,
               tm=tm, tn=tn, tk=tk_, name=name, out_shapes=[out_shape],
               epi=None if out_dtype == F32 else (lambda acc: (acc,)))[0]


GATE_BLOCK = 512


def _split3(v):
    hi = v.astype(jnp.bfloat16)
    r1 = v - hi.astype(F32)
    mid = r1.astype(jnp.bfloat16)
    lo = (r1 - mid.astype(F32)).astype(jnp.bfloat16)
    return hi, mid, lo


def _exact_dot(v, tri):
    return functools.reduce(jnp.add, [jnp.dot(t, tri, preferred_element_type=F32) for t in _split3(v)])


def _gates_fwd(f_t, b, name):
    H, S = f_t.shape
    nb = _tile(S, GATE_BLOCK)
    inv_scale = HEAD_DIM ** 0.5

    def body(f_ref, b_ref, c_ref):
        upper = (lax.broadcasted_iota(jnp.int32, (nb, nb), 0)
                 <= lax.broadcasted_iota(jnp.int32, (nb, nb), 1)).astype(jnp.bfloat16)
        carry = jnp.zeros((H, 1), F32)
        for i in range(S // nb):
            z = f_ref[:, i * nb:(i + 1) * nb] + b_ref[...]
            logf = jnp.minimum(z, 0.0) - jnp.log1p(jnp.exp(-jnp.abs(z)))
            cs = _exact_dot(logf, upper) + carry
            for j, t in enumerate(_split3(cs * inv_scale)):
                c_ref[j, :, i * nb:(i + 1) * nb] = t
            carry = cs[:, nb - 1:nb]

    return pl.pallas_call(body, out_shape=jax.ShapeDtypeStruct((3, H, S), jnp.bfloat16), name=name,
                          compiler_params=_cparams())(f_t, b.reshape(H, 1))


def _gates_bwd(f_t, b, dcq, dck, name):
    H, S = f_t.shape
    nb = _tile(S, GATE_BLOCK)

    def body(f_ref, b_ref, dcq_ref, dck_ref, df_ref, dfc_ref, db_ref):
        lower = (lax.broadcasted_iota(jnp.int32, (nb, nb), 0)
                 >= lax.broadcasted_iota(jnp.int32, (nb, nb), 1)).astype(jnp.bfloat16)
        carry = jnp.zeros((H, 1), F32)
        db = jnp.zeros((H, 1), F32)
        for i in reversed(range(S // nb)):
            sl = slice(i * nb, (i + 1) * nb)
            dc = dcq_ref[:, sl] - dck_ref[:, sl]
            dlogf = _exact_dot(dc, lower) + carry
            carry = dlogf[:, 0:1]
            z = f_ref[:, sl] + b_ref[...]
            df = dlogf / (1.0 + jnp.exp(z))
            df_ref[:, sl] = df
            dfc_ref[:, sl] = df.astype(dfc_ref.dtype)
            db = db + jnp.sum(df, axis=1, keepdims=True)
        db_ref[...] = db

    return pl.pallas_call(
        body, out_shape=[jax.ShapeDtypeStruct((H, S), F32), jax.ShapeDtypeStruct((H, S), CDT),
                         jax.ShapeDtypeStruct((H, 1), F32)],
        name=name, compiler_params=_cparams())(f_t, b.reshape(H, 1), dcq, dck)


FOX_BLOCK = 1024


def _fox_bias_operands(csplit, name, bs=512):
    _, H, S = csplit.shape
    E = HEAD_DIM
    bs = _tile(S, bs)
    part = jnp.arange(3 * H)[:, None] // H
    head = jnp.arange(3 * H)[:, None] % H
    lane = jnp.arange(H * E)[None, :]
    place_q = (lane == head * E + part).astype(csplit.dtype)
    place_k = -(lane == head * E + 3 + part).astype(csplit.dtype)
    ones_q = ((lane % E >= 3) & (lane % E < 6)).astype(F32)
    ones_k = (lane % E < 3).astype(F32)

    def body(c_ref, pq_ref, pk_ref, oq_ref, ok_ref, qc_ref, kc_ref):
        c = c_ref[...]
        qc_ref[...] = (lax.dot_general(c, pq_ref[...], _TN, preferred_element_type=F32) + oq_ref[...]).astype(qc_ref.dtype)
        kc_ref[...] = (lax.dot_general(c, pk_ref[...], _TN, preferred_element_type=F32) + ok_ref[...]).astype(kc_ref.dtype)

    full = lambda a: pl.BlockSpec(a.shape, lambda i: (0, 0))
    out = pl.BlockSpec((bs, H * E), lambda i: (i, 0))
    return pl.pallas_call(
        body, grid=(S // bs,),
        in_specs=[pl.BlockSpec((3 * H, bs), lambda i: (0, i)), full(place_q), full(place_k), full(ones_q), full(ones_k)],
        out_specs=[out, out], out_shape=[jax.ShapeDtypeStruct((S, H * E), csplit.dtype)] * 2,
        name=name, compiler_params=_cparams(("parallel",)),
    )(csplit.reshape(3 * H, S), place_q, place_k, ones_q, ones_k)


def _fox_logits2(q_ref, qc_ref, k_ref, kc_ref, diag):
    q, k = q_ref[...], k_ref[...]
    qa = jnp.concatenate([q, qc_ref[...].astype(q.dtype)], axis=1)
    ka = jnp.concatenate([k, kc_ref[...].astype(k.dtype)], axis=1)
    s = lax.dot_general(qa, ka, _NT, preferred_element_type=F32) * (HEAD_DIM ** -0.5 * LOG2E)
    if diag:
        row = lax.broadcasted_iota(jnp.int32, s.shape, 0)
        col = lax.broadcasted_iota(jnp.int32, s.shape, 1)
        s = jnp.where(col <= row, s, NEG_INF)
    return s


def _fox_fwd(proj, qc, kc, H, name):
    S = proj.shape[0]
    E = HEAD_DIM
    blk = _tile(S, FOX_BLOCK)
    nq = S // blk

    def pair(t):
        qi = sum((t >= i * (i + 1) // 2).astype(jnp.int32) for i in range(1, nq)) if nq > 1 else 0 * t
        return qi, t - qi * (qi + 1) // 2

    def body(q_ref, qc_ref, k_ref, kc_ref, v_ref, o_ref, lse_ref, m_s, l_s, acc_s):
        qi, kj = pair(pl.program_id(1))

        @pl.when(kj == 0)
        def _():
            m_s[...] = jnp.full(m_s.shape, NEG_INF, F32)
            l_s[...] = jnp.zeros(l_s.shape, F32)
            acc_s[...] = jnp.zeros(acc_s.shape, F32)

        def step(diag):
            s = _fox_logits2(q_ref, qc_ref, k_ref, kc_ref, diag)
            m_prev = m_s[...]
            m_new = jnp.maximum(m_prev, jnp.max(s, axis=-1, keepdims=True))
            alpha = jnp.exp2(m_prev - m_new)
            p = jnp.exp2(s - m_new)
            l_s[...] = alpha * l_s[...] + jnp.sum(p, axis=-1, keepdims=True)
            acc_s[...] = alpha * acc_s[...] + jnp.dot(p.astype(CDT), v_ref[...], preferred_element_type=F32)
            m_s[...] = m_new

        pl.when(kj < qi)(lambda: step(False))
        pl.when(kj == qi)(lambda: step(True))

        @pl.when(kj == qi)
        def _():
            o_ref[...] = acc_s[...] / l_s[...]
            lse_ref[...] = jnp.broadcast_to(m_s[...] + jnp.log2(l_s[...]), lse_ref.shape)

    qspec = lambda off: pl.BlockSpec((blk, E), lambda h, t: (pair(t)[0], off + h))
    kspec = lambda off: pl.BlockSpec((blk, E), lambda h, t: (pair(t)[1], off + h))
    return pl.pallas_call(
        body, grid=(H, nq * (nq + 1) // 2),
        in_specs=[qspec(0), qspec(0), kspec(H), kspec(0), kspec(2 * H)],
        out_specs=[qspec(0)] * 2,
        out_shape=[jax.ShapeDtypeStruct((S, H * E), F32)] * 2,
        scratch_shapes=[pltpu.VMEM((blk, 1), F32), pltpu.VMEM((blk, 1), F32), pltpu.VMEM((blk, E), F32)],
        name=name, compiler_params=_cparams(("parallel", "arbitrary")),
    )(proj, qc, proj, kc, proj)


def _fox_bwd(proj, qc, kc, lse, o, do, H, name):
    S = proj.shape[0]
    E = HEAD_DIM
    blk = _tile(S, FOX_BLOCK)
    nq = S // blk
    scale = E ** -0.5

    def pair(t):
        first = lambda j: j * nq - j * (j - 1) // 2
        kj = sum((t >= first(j)).astype(jnp.int32) for j in range(1, nq)) if nq > 1 else 0 * t
        return kj, kj + t - first(kj)

    def body(q_ref, qc_ref, k_ref, kc_ref, v_ref, lse_ref, o_ref, do_ref,
             dq_ref, dcq_ref, dk_ref, dv_ref, dck_ref, dq_s, dcq_s, dk_s, dv_s, dck_s):
        kj, qi = pair(pl.program_id(1))

        @pl.when(qi == kj)
        def _():
            dk_s[...] = jnp.zeros(dk_s.shape, F32)
            dv_s[...] = jnp.zeros(dv_s.shape, F32)
            dck_s[...] = jnp.zeros(dck_s.shape, F32)

        def step(diag):
            do = do_ref[...]
            doc = do.astype(CDT)
            delta = jnp.sum(do * o_ref[...], axis=-1, keepdims=True)
            p = jnp.exp2(_fox_logits2(q_ref, qc_ref, k_ref, kc_ref, diag) - lse_ref[:, 0:1])
            dp = lax.dot_general(doc, v_ref[...], _NT, preferred_element_type=F32)
            ds = p * (dp - delta)
            dss = ds * scale
            dck_s[...] += jnp.sum(ds, axis=0, keepdims=True)
            dv_s[...] += jnp.dot(p.T.astype(CDT), doc, preferred_element_type=F32)
            dk_s[...] += jnp.dot(dss.T.astype(CDT), q_ref[...], preferred_element_type=F32)
            dq_part = jnp.dot(dss.astype(CDT), k_ref[...], preferred_element_type=F32)
            dc_part = jnp.sum(ds, axis=-1, keepdims=True)
            rows = pl.ds(pl.multiple_of(qi * blk, blk), blk)

            @pl.when(kj == 0)
            def _():
                dq_s[rows, :] = dq_part
                dcq_s[rows, :] = dc_part

            @pl.when(kj > 0)
            def _():
                dq_s[rows, :] += dq_part
                dcq_s[rows, :] += dc_part

        pl.when(qi > kj)(lambda: step(False))
        pl.when(qi == kj)(lambda: step(True))

        @pl.when(qi == nq - 1)
        def _():
            dk_ref[...] = dk_s[...].astype(dk_ref.dtype)
            dv_ref[...] = dv_s[...].astype(dv_ref.dtype)
            dck_ref[...] = dck_s[...].reshape(dck_ref.shape)

        @pl.when((qi == nq - 1) & (kj == nq - 1))
        def _():
            dq_ref[...] = dq_s[...].astype(dq_ref.dtype)
            dcq_ref[...] = jnp.broadcast_to(dcq_s[...], dcq_ref.shape)

    qspec = lambda off: pl.BlockSpec((blk, E), lambda h, t: (pair(t)[1], off + h))
    kspec = lambda off: pl.BlockSpec((blk, E), lambda h, t: (pair(t)[0], off + h))
    head = pl.BlockSpec((S, E), lambda h, t: (0, h))
    return pl.pallas_call(
        body, grid=(H, nq * (nq + 1) // 2),
        in_specs=[qspec(0), qspec(0), kspec(H), kspec(0), kspec(2 * H), qspec(0), qspec(0), qspec(0)],
        out_specs=[head, head, kspec(0), kspec(0), pl.BlockSpec((1, 1, blk), lambda h, t: (h, 0, pair(t)[0]))],
        out_shape=[jax.ShapeDtypeStruct((S, H * E), CDT), jax.ShapeDtypeStruct((S, H * E), F32),
                   jax.ShapeDtypeStruct((S, H * E), CDT), jax.ShapeDtypeStruct((S, H * E), CDT),
                   jax.ShapeDtypeStruct((H, 1, S), F32)],
        scratch_shapes=[pltpu.VMEM((S, E), F32), pltpu.VMEM((S, 1), F32), pltpu.VMEM((blk, E), F32),
                        pltpu.VMEM((blk, E), F32), pltpu.VMEM((1, blk), F32)],
        name=name, compiler_params=_cparams(("parallel", "arbitrary")),
    )(proj, qc, proj, kc, proj, lse, o, do)


DIL_SLAB = 16 * DIL_BLOCK
DIL_UNROLL = 16


def _rel_bucket(dist):
    max_exact = REL_BUCKETS // 2
    d = jnp.maximum(dist.astype(F32), 1.0)
    large = max_exact + (jnp.log(d / max_exact) / jnp.log(jnp.float32(REL_MAX_DISTANCE / max_exact))
                         * (REL_BUCKETS - max_exact)).astype(jnp.int32)
    large = jnp.minimum(large, REL_BUCKETS - 1)
    return jnp.where(dist < max_exact, dist, large)


def _bucket_table():
    i = jnp.arange(DIL_BLOCK)[:, None]
    j = jnp.arange(2 * DIL_BLOCK)[None, :]
    rel = DIL_BLOCK + i - j
    tabs = [_rel_bucket(jnp.clip(rel, 0, w // d) * d) for w, d in DIL_PATTERNS]
    return jnp.stack(tabs).astype(jnp.int32)


def _bias_table(rel_bias, buckets, name):
    P = buckets.shape[0]
    H = rel_bias.shape[1]

    def body(rb_ref, bk_ref, out_ref):
        h = pl.program_id(1)
        bk = bk_ref[0]
        val = jnp.zeros(bk.shape, F32)
        for b in range(REL_BUCKETS):
            val = jnp.where(bk == b, rb_ref[b, h], val)
        out_ref[0, 0] = val

    return pl.pallas_call(
        body, grid=(P, H),
        in_specs=[pl.BlockSpec(memory_space=pltpu.SMEM),
                  pl.BlockSpec((1, DIL_BLOCK, 2 * DIL_BLOCK), lambda p, h: (p, 0, 0))],
        out_specs=pl.BlockSpec((1, 1, DIL_BLOCK, 2 * DIL_BLOCK), lambda p, h: (p, h, 0, 0)),
        out_shape=jax.ShapeDtypeStruct((P, H, DIL_BLOCK, 2 * DIL_BLOCK), F32),
        name=name, compiler_params=_cparams(("parallel", "parallel")),
    )(rel_bias, buckets)


def _bias_table_bwd(dbias, buckets, name):
    P, H = dbias.shape[:2]

    def body(db_ref, bk_ref, out_ref):
        lane = lax.broadcasted_iota(jnp.int32, (1, REL_BUCKETS), 1)
        acc = jnp.zeros((1, REL_BUCKETS), F32)
        bk = bk_ref[...]
        db = db_ref[:, 0]
        for b in range(REL_BUCKETS):
            tot = jnp.sum(jnp.where(bk == b, db, 0.0))
            acc = jnp.where(lane == b, tot, acc)
        out_ref[0] = acc

    return pl.pallas_call(
        body, grid=(H,),
        in_specs=[pl.BlockSpec((P, 1, DIL_BLOCK, 2 * DIL_BLOCK), lambda h: (0, h, 0, 0)),
                  pl.BlockSpec((P, DIL_BLOCK, 2 * DIL_BLOCK), lambda h: (0, 0, 0))],
        out_specs=pl.BlockSpec((1, 1, REL_BUCKETS), lambda h: (h, 0, 0)),
        out_shape=jax.ShapeDtypeStruct((H, 1, REL_BUCKETS), F32),
        name=name, compiler_params=_cparams(("parallel",)),
    )(dbias, buckets)


def _bdot(a, b, contract_b):
    return lax.dot_general(a, b, (((2,), (contract_b,)), ((0,), (0,))), preferred_element_type=F32)


def _dil_units(first, d):
    units = []
    for t in range(DIL_UNROLL):
        u = first + t
        sg = u // d
        units.append((sg, sg * (DIL_BLOCK * d) + u % d))
    return units


def _dil_rows(ref, starts, d, dtype=None):
    t = jnp.stack([ref[pl.ds(s, DIL_BLOCK, stride=d), :] for s in starts])
    return t if dtype is None else t.astype(dtype)


def _dil_keys(ref, units, d):
    B, SL = DIL_BLOCK, DIL_SLAB
    return jnp.stack([jnp.concatenate([ref[pl.ds(SL + b - B * d, B, stride=d), :], ref[pl.ds(SL + b, B, stride=d), :]],
                                      axis=0) for _, b in units]).astype(CDT)


def _dil_logits(q, keys, bias_pc, first, d, has_before):
    T, B = q.shape[0], DIL_BLOCK
    ii = lax.broadcasted_iota(jnp.int32, (T, B, 2 * B), 1)
    jj = lax.broadcasted_iota(jnp.int32, (T, B, 2 * B), 2)
    sg = (first + lax.broadcasted_iota(jnp.int32, (T, B, 2 * B), 0)) // d
    mask = (jj >= ii) & (jj <= ii + B) & ((jj >= B) | (sg > 0) | has_before)
    return jnp.where(mask, _bdot(q, keys, 2) * HEAD_DIM ** -0.5 + bias_pc[None], NEG_INF)


def _dil_specs(H):
    E, SL = DIL_BLOCK, DIL_SLAB
    cur = lambda off: pl.BlockSpec((SL, E), lambda h, g: (g, off + h))
    prev = lambda off: pl.BlockSpec((SL, E), lambda h, g: (jnp.maximum(g - 1, 0), off + h))
    bias = pl.BlockSpec((len(DIL_PATTERNS), 1, E, 2 * E), lambda h, g: (0, h, 0, 0))
    return cur, prev, bias


def _dil_fwd(proj, bias, H, name):
    S = proj.shape[0]
    E = B = DIL_BLOCK
    SL = DIL_SLAB
    P = len(DIL_PATTERNS)
    assert S % SL == 0
    n_slabs = S // SL

    def body(q_ref, kc_ref, kp_ref, vc_ref, vp_ref, b_ref, y_ref, lse_ref, kj, vj, o_s, l_s):
        g = pl.program_id(1)
        kj[0:SL, :] = kp_ref[...]
        kj[SL:2 * SL, :] = kc_ref[...]
        vj[0:SL, :] = vp_ref[...]
        vj[SL:2 * SL, :] = vc_ref[...]
        for p, (_, d) in enumerate(DIL_PATTERNS):
            def batch(it, carry, p=p, d=d):
                first = it * DIL_UNROLL
                units = _dil_units(first, d)
                q = _dil_rows(q_ref, [b for _, b in units], d, CDT)
                s = _dil_logits(q, _dil_keys(kj, units, d), b_ref[p, 0], first, d, g > 0)
                m = jnp.max(s, axis=-1, keepdims=True)
                e = jnp.exp(s - m)
                ssum = jnp.sum(e, axis=-1, keepdims=True)
                o = _bdot(e.astype(CDT), _dil_keys(vj, units, d), 1) / ssum
                lse = jnp.broadcast_to(m + jnp.log(ssum), o.shape)
                for t, (_, b) in enumerate(units):
                    o_s[p, pl.ds(b, B, stride=d), :] = o[t]
                    l_s[p, pl.ds(b, B, stride=d), :] = lse[t]
                return carry

            lax.fori_loop(0, SL // B // DIL_UNROLL, batch, 0)
        ls = [l_s[p] for p in range(P)]
        m = functools.reduce(jnp.maximum, ls)
        w = [jnp.exp(l - m) for l in ls]
        tot = functools.reduce(jnp.add, w)
        y_ref[...] = functools.reduce(jnp.add, [(w[p] / tot) * o_s[p] for p in range(P)])
        lse_ref[...] = m + jnp.log(tot)

    cur, prev, bspec = _dil_specs(H)
    return pl.pallas_call(
        body, grid=(H, n_slabs),
        in_specs=[cur(0), cur(H), prev(H), cur(2 * H), prev(2 * H), bspec],
        out_specs=[cur(0), cur(0)],
        out_shape=[jax.ShapeDtypeStruct((S, H * E), F32)] * 2,
        scratch_shapes=[pltpu.VMEM((2 * SL, E), F32), pltpu.VMEM((2 * SL, E), F32),
                        pltpu.VMEM((P, SL, E), F32), pltpu.VMEM((P, SL, E), F32)],
        name=name, compiler_params=_cparams(("parallel", "parallel")),
    )(proj, proj, proj, proj, proj, bias)


def _dil_bwd(proj, bias, y, dy, lse, H, name):
    S = proj.shape[0]
    E = B = DIL_BLOCK
    SL = DIL_SLAB
    P = len(DIL_PATTERNS)
    assert S % SL == 0
    n_slabs = S // SL
    scale = E ** -0.5

    def body(q_ref, kc_ref, kp_ref, vc_ref, vp_ref, b_ref, y_ref, dy_ref, lse_ref,
             dq_ref, dk_ref, dv_ref, db_ref, kj, vj, dq_s, dk_own, dv_own, dk_held, dv_held, dk_back, dv_back, dl_s):
        g = pl.program_id(1)

        @pl.when(g == 0)
        def _():
            db_ref[...] = jnp.zeros(db_ref.shape, F32)

        @pl.when(g > 0)
        def _():
            dk_held[...] = dk_own[...]
            dv_held[...] = dv_own[...]
            dk_back[...] = jnp.zeros(dk_back.shape, F32)
            dv_back[...] = jnp.zeros(dv_back.shape, F32)

        @pl.when(g < n_slabs)
        def _():
            kj[0:SL, :] = kp_ref[...]
            kj[SL:2 * SL, :] = kc_ref[...]
            vj[0:SL, :] = vp_ref[...]
            vj[SL:2 * SL, :] = vc_ref[...]
            dq_s[...] = jnp.zeros(dq_s.shape, F32)
            dk_own[...] = jnp.zeros(dk_own.shape, F32)
            dv_own[...] = jnp.zeros(dv_own.shape, F32)
            dl_s[...] = jnp.broadcast_to(jnp.sum(dy_ref[...] * y_ref[...], axis=-1, keepdims=True), (SL, E))
            tr = lambda t: jnp.swapaxes(t, 1, 2).astype(CDT)
            for p, (_, d) in enumerate(DIL_PATTERNS):
                def batch(it, carry, p=p, d=d):
                    first = it * DIL_UNROLL
                    units = _dil_units(first, d)
                    starts = [b for _, b in units]
                    q = _dil_rows(q_ref, starts, d, CDT)
                    dyc = _dil_rows(dy_ref, starts, d, CDT)
                    keys, vals = _dil_keys(kj, units, d), _dil_keys(vj, units, d)
                    s = _dil_logits(q, keys, b_ref[p, 0], first, d, g > 0)
                    e = jnp.exp(s - _dil_rows(lse_ref, starts, d)[:, :, 0:1])
                    ds = e * (_bdot(dyc, vals, 2) - _dil_rows(dl_s, starts, d)[:, :, 0:1])
                    dss = ds * scale
                    dq = _bdot(dss.astype(CDT), keys, 1)
                    dk = _bdot(tr(dss), q, 1)
                    dv = _bdot(tr(e), dyc, 1)
                    for t, (sg, b) in enumerate(units):
                        rows = pl.ds(b, B, stride=d)
                        dq_s[rows, :] += dq[t]
                        dk_own[rows, :] += dk[t, B:]
                        dv_own[rows, :] += dv[t, B:]

                        if B * d < SL:
                            @pl.when(sg > 0)
                            def _(t=t, b=b):
                                before = pl.ds(b - B * d, B, stride=d)
                                dk_own[before, :] += dk[t, :B]
                                dv_own[before, :] += dv[t, :B]

                        @pl.when((sg == 0) & (g > 0))
                        def _(t=t, b=b):
                            before = pl.ds(SL + b - B * d, B, stride=d)
                            dk_back[before, :] += dk[t, :B]
                            dv_back[before, :] += dv[t, :B]

                    db_ref[p, 0] += jnp.sum(ds, axis=0)
                    return carry

                lax.fori_loop(0, SL // B // DIL_UNROLL, batch, 0)
            dq_ref[...] = dq_s[...].astype(dq_ref.dtype)

        @pl.when(g > 0)
        def _():
            dk_ref[...] = (dk_held[...] + dk_back[...]).astype(dk_ref.dtype)
            dv_ref[...] = (dv_held[...] + dv_back[...]).astype(dv_ref.dtype)

    last = n_slabs - 1
    cur = lambda off: pl.BlockSpec((SL, E), lambda h, g: (jnp.minimum(g, last), off + h))
    prev = lambda off: pl.BlockSpec((SL, E), lambda h, g: (jnp.maximum(jnp.minimum(g, last) - 1, 0), off + h))
    late = pl.BlockSpec((SL, E), lambda h, g: (jnp.maximum(g - 1, 0), h))
    bspec = pl.BlockSpec((P, 1, B, 2 * B), lambda h, g: (0, h, 0, 0))
    slab = pltpu.VMEM((SL, E), F32)
    return pl.pallas_call(
        body, grid=(H, n_slabs + 1),
        in_specs=[cur(0), cur(H), prev(H), cur(2 * H), prev(2 * H), bspec, cur(0), cur(0), cur(0)],
        out_specs=[cur(0), late, late, bspec],
        out_shape=[jax.ShapeDtypeStruct((S, H * E), CDT)] * 3 + [jax.ShapeDtypeStruct((P, H, B, 2 * B), F32)],
        scratch_shapes=[pltpu.VMEM((2 * SL, E), F32), pltpu.VMEM((2 * SL, E), F32)] + [slab] * 8,
        name=name, compiler_params=_cparams(("parallel", "arbitrary")),
    )(proj, proj, proj, proj, proj, bias, y, dy, lse)


def _adamw_tile(g_, w_ref, m_ref, v_ref, g_out, d_ref, nm_ref, nv_ref):
    g_out[...] = g_
    m_ = ADAM_B1 * m_ref[...] + (1.0 - ADAM_B1) * g_
    v_ = ADAM_B2 * v_ref[...] + (1.0 - ADAM_B2) * jnp.square(g_)
    m_hat = m_ / (1.0 - ADAM_B1 ** ADAM_STEP)
    v_hat = v_ / (1.0 - ADAM_B2 ** ADAM_STEP)
    d_ref[...] = -ADAM_LR * (m_hat / (jnp.sqrt(v_hat) + ADAM_EPS) + ADAM_WD * w_ref[...])
    nm_ref[...] = m_
    nv_ref[...] = v_


def _adamw(w, g, m, v, name, br=256):
    R, C = w.shape
    br = br if R % br == 0 else R

    def body(w_ref, g_ref, m_ref, v_ref, *outs):
        _adamw_tile(g_ref[...], w_ref, m_ref, v_ref, *outs)

    blk = pl.BlockSpec((br, C), lambda i: (i, 0))
    return pl.pallas_call(
        body, grid=(R // br,), in_specs=[blk] * 4, out_specs=[blk] * 4,
        out_shape=[jax.ShapeDtypeStruct((R, C), F32)] * 4,
        name=name, compiler_params=_cparams(("parallel",)),
    )(w, g, m, v)


_HBM = pl.BlockSpec(memory_space=pltpu.HBM)
_SEM = pl.BlockSpec(memory_space=pltpu.SEMAPHORE)
_ANY = pl.BlockSpec(memory_space=pl.ANY)
_VMEM = pl.BlockSpec(memory_space=pltpu.VMEM)
_TOKEN = jax.ShapeDtypeStruct((8, 128), F32)


def _split_params():
    return pltpu.CompilerParams(has_side_effects=pltpu.SideEffectType.DATAFLOW_SIDE_EFFECTING)


def _place():
    x, y, c = lax.axis_index("x"), lax.axis_index("y"), lax.axis_index("c")
    chips = [(1 - x, y), (x, 1 - y), (1 - x, 1 - y)]
    return x, y, c, chips


def _tie(v, tokens, name):
    flat = v.reshape(1, -1)

    def body(v_ref, *rest):
        rest[-1][...] = v_ref[...]

    return pl.pallas_call(body, in_specs=[_VMEM] + [_ANY] * len(tokens), out_specs=_VMEM,
                          out_shape=jax.ShapeDtypeStruct(flat.shape, flat.dtype), name=name,
                          compiler_params=_cparams())(flat, *tokens).reshape(v.shape)


def _row_block(R, pref=256):
    return _tile(R, pref) if R % 128 == 0 else R


def _slot():
    return 2 * lax.axis_index("x") + lax.axis_index("y")


def _cast_into_slot(w, layer, name):
    _, R, C = w.shape
    br = _row_block(R)

    def body(w_ref, out_ref):
        out_ref[...] = w_ref[...].astype(out_ref.dtype)

    return pl.pallas_call(
        body, grid=(R // br,),
        in_specs=[pl.BlockSpec((None, br, C), lambda i: (layer, i, 0))],
        out_specs=pl.BlockSpec((None, br, C), lambda i: (_slot(), i, 0)),
        out_shape=jax.ShapeDtypeStruct((N_CHIPS, R, C), CDT),
        name=name, compiler_params=_cparams(("parallel",)),
    )(w)


def _gather_copies(src_ref, dst_ref, send_sems, recv_sems, incoming):
    Rh = src_ref.shape[1] // 2
    x, y, c, chips = _place()
    slot = 2 * x + y

    def half(ref, s, hf):
        return ref.at[s, pl.ds(hf * Rh, Rh), :]

    copies = []
    for j, (cx, cy) in enumerate(chips):
        for e in range(2):
            copies.append(pltpu.make_async_remote_copy(
                src_ref=half(src_ref, slot, c), dst_ref=half(dst_ref, 2 * cx + cy, e) if incoming else half(dst_ref, slot, c),
                send_sem=send_sems.at[2 * j + e], recv_sem=recv_sems.at[2 * j + (e if incoming else c)],
                device_id=(cx, cy, e), device_id_type=MESH))
    return copies


def _gather_start(buf, after, name):
    n_after = len(after)

    def body(*refs):
        buf_ref = refs[0]
        send_sems, recv_sems, out_ref, token = refs[1 + n_after:]
        for cp in _gather_copies(buf_ref, out_ref, send_sems, recv_sems, incoming=False):
            cp.start()
        token[...] = jnp.zeros(token.shape, token.dtype)

    return pl.pallas_call(
        body, in_specs=[_HBM] + [_ANY] * n_after, out_specs=(_SEM, _SEM, _HBM, _VMEM),
        out_shape=(pltpu.SemaphoreType.DMA((6,)), pltpu.SemaphoreType.DMA((6,)), pltpu.HBM(buf.shape, buf.dtype), _TOKEN),
        input_output_aliases={0: 2}, name=name, compiler_params=_split_params(),
    )(pltpu.with_memory_space_constraint(buf, pltpu.HBM), *after)


def _gather_wait(send_sems, recv_sems, buf, after, name):
    def body(buf_ref, send_sems, recv_sems, after_ref, out_ref):
        for cp in _gather_copies(buf_ref, out_ref, send_sems, recv_sems, incoming=False):
            cp.wait_send()
        for cp in _gather_copies(buf_ref, out_ref, send_sems, recv_sems, incoming=True):
            cp.wait_recv()

    return pl.pallas_call(
        body, in_specs=[_HBM, _SEM, _SEM, _ANY], out_specs=_HBM, out_shape=pltpu.HBM(buf.shape, buf.dtype),
        input_output_aliases={0: 0}, name=name, compiler_params=_split_params(),
    )(buf, send_sems, recv_sems, after)


def _relay_copies(src_ref, dst_ref, send_sems, recv_sems, stage, incoming):
    Rh = src_ref.shape[1] // 2
    x, y, c, chips = _place()
    copies = []
    for j, (cx, cy) in enumerate(chips):
        if stage == 0:
            src_slot, src_half, peer = 2 * x + y, c, (cx, cy, c)
            dst_slot, dst_half = (2 * cx + cy, c) if incoming else (src_slot, c)
        else:
            src_slot, src_half, peer = 2 * cx + cy, c, (x, y, 1 - c)
            dst_slot, dst_half = src_slot, (1 - c if incoming else c)
        copies.append(pltpu.make_async_remote_copy(
            src_ref=src_ref.at[src_slot, pl.ds(src_half * Rh, Rh), :],
            dst_ref=dst_ref.at[dst_slot, pl.ds(dst_half * Rh, Rh), :],
            send_sem=send_sems.at[j], recv_sem=recv_sems.at[j], device_id=peer, device_id_type=MESH))
    return copies


def _relay_start(buf, after, name):
    n_after = len(after)

    def body(*refs):
        buf_ref = refs[0]
        send_sems, recv_sems, out_ref, token = refs[1 + n_after:]
        for cp in _relay_copies(buf_ref, out_ref, send_sems, recv_sems, 0, incoming=False):
            cp.start()
        token[...] = jnp.zeros(token.shape, token.dtype)

    return pl.pallas_call(
        body, in_specs=[_HBM] + [_ANY] * n_after, out_specs=(_SEM, _SEM, _HBM, _VMEM),
        out_shape=(pltpu.SemaphoreType.DMA((3,)), pltpu.SemaphoreType.DMA((3,)), pltpu.HBM(buf.shape, buf.dtype), _TOKEN),
        input_output_aliases={0: 2}, name=name, compiler_params=_split_params(),
    )(pltpu.with_memory_space_constraint(buf, pltpu.HBM), *after)


def _relay_pass(send_sems, recv_sems, buf, after, name):
    def body(buf_ref, send0, recv0, after_ref, send1, recv1, out_ref):
        for cp in _relay_copies(buf_ref, out_ref, send0, recv0, 0, incoming=False):
            cp.wait_send()
        for cp in _relay_copies(buf_ref, out_ref, send0, recv0, 0, incoming=True):
            cp.wait_recv()
        for cp in _relay_copies(out_ref, out_ref, send1, recv1, 1, incoming=False):
            cp.start()

    return pl.pallas_call(
        body, in_specs=[_HBM, _SEM, _SEM, _ANY], out_specs=(_SEM, _SEM, _HBM),
        out_shape=(pltpu.SemaphoreType.DMA((3,)), pltpu.SemaphoreType.DMA((3,)), pltpu.HBM(buf.shape, buf.dtype)),
        input_output_aliases={0: 2}, name=name, compiler_params=_split_params(),
    )(buf, send_sems, recv_sems, after)


def _relay_wait(send_sems, recv_sems, buf, after, name):
    def body(buf_ref, send1, recv1, after_ref, out_ref):
        for cp in _relay_copies(buf_ref, out_ref, send1, recv1, 1, incoming=False):
            cp.wait_send()
        for cp in _relay_copies(buf_ref, out_ref, send1, recv1, 1, incoming=True):
            cp.wait_recv()

    return pl.pallas_call(
        body, in_specs=[_HBM, _SEM, _SEM, _ANY], out_specs=_HBM, out_shape=pltpu.HBM(buf.shape, buf.dtype),
        input_output_aliases={0: 0}, name=name, compiler_params=_split_params(),
    )(buf, send_sems, recv_sems, after)


def _scatter_copies(g_ref, land_ref, send_sems, recv_sems, incoming):
    Rh = g_ref.shape[1] // 2
    x, y, c, _ = _place()
    me = 4 * x + 2 * y + c
    copies = []
    for k in range(1, N_DEV):
        px, py, pc = (x + (k >> 2)) % 2, (y + ((k >> 1) & 1)) % 2, (c + (k & 1)) % 2
        copies.append(pltpu.make_async_remote_copy(
            src_ref=g_ref.at[2 * px + py, pl.ds(pc * Rh, Rh), :],
            dst_ref=land_ref.at[4 * px + 2 * py + pc if incoming else me],
            send_sem=send_sems.at[k - 1], recv_sem=recv_sems.at[k - 1], device_id=(px, py, pc), device_id_type=MESH))
    return copies


def _scatter_start(g, name):
    ns, R, C = g.shape

    def body(g_ref, land_ref, send_sems, recv_sems, g_thru, land_thru, token):
        for cp in _scatter_copies(g_ref, land_thru, send_sems, recv_sems, incoming=False):
            cp.start()
        token[...] = jnp.zeros(token.shape, token.dtype)

    land = lax.empty((N_DEV, R // 2, C), g.dtype)
    n = N_DEV - 1
    return pl.pallas_call(
        body, in_specs=[_HBM, _HBM], out_specs=(_SEM, _SEM, _HBM, _HBM, _VMEM),
        out_shape=(pltpu.SemaphoreType.DMA((n,)), pltpu.SemaphoreType.DMA((n,)), pltpu.HBM(g.shape, g.dtype),
                   pltpu.HBM(land.shape, land.dtype), _TOKEN),
        input_output_aliases={0: 2, 1: 3}, name=name, compiler_params=_split_params(),
    )(pltpu.with_memory_space_constraint(g, pltpu.HBM), pltpu.with_memory_space_constraint(land, pltpu.HBM))


def _scatter_wait(send_sems, recv_sems, g, land, after, name):
    def body(g_ref, land_ref, send_sems, recv_sems, after_ref, g_out, land_out):
        for cp in _scatter_copies(g_ref, land_out, send_sems, recv_sems, incoming=False):
            cp.wait_send()
        for cp in _scatter_copies(g_ref, land_out, send_sems, recv_sems, incoming=True):
            cp.wait_recv()

    return pl.pallas_call(
        body, in_specs=[_HBM, _HBM, _SEM, _SEM, _ANY], out_specs=(_HBM, _HBM),
        out_shape=(pltpu.HBM(g.shape, g.dtype), pltpu.HBM(land.shape, land.dtype)),
        input_output_aliases={0: 0, 1: 1}, name=name, compiler_params=_split_params(),
    )(g, land, send_sems, recv_sems, after)


def _device_sum(land, g, layer, n_layers, prev, name):
    nd, Rh, C = land.shape
    br = _row_block(Rh)
    nb = Rh // br
    core = lambda: lax.axis_index("c")
    me = lambda: 2 * _slot() + core()

    def body(*refs):
        own = refs[nd][...]
        acc = None
        for d in range(nd):
            t = jnp.where(me() == d, own, refs[d][...]).astype(F32)
            acc = t if acc is None else acc + t
        refs[-1][...] = acc

    def piece(d):
        return pl.BlockSpec((None, br, C), lambda i: (jnp.where(me() == d, (d + 1) % nd, d), i, 0))

    ins = [land] * nd + [g] + ([prev] if prev is not None else [])
    return pl.pallas_call(
        body, grid=(nb,),
        in_specs=[piece(d) for d in range(nd)]
        + [pl.BlockSpec((None, br, C), lambda i: (_slot(), core() * nb + i, 0))]
        + ([_ANY] if prev is not None else []),
        out_specs=pl.BlockSpec((None, br, C), lambda i: (layer, core() * nb + i, 0)),
        out_shape=jax.ShapeDtypeStruct((n_layers, 2 * Rh, C), F32),
        input_output_aliases={nd + 1: 0} if prev is not None else {},
        name=name, compiler_params=_cparams(("parallel",)),
    )(*ins)


def _join_copy(src_ref, dst_ref, layer, send_sem, recv_sem, incoming):
    Rh = src_ref.shape[1] // 2
    x, y, c, _ = _place()
    mine, other = pl.ds(c * Rh, Rh), pl.ds((1 - c) * Rh, Rh)
    return pltpu.make_async_remote_copy(src_ref=src_ref.at[layer, mine, :],
                                        dst_ref=dst_ref.at[layer, other if incoming else mine, :],
                                        send_sem=send_sem, recv_sem=recv_sem, device_id=(x, y, 1 - c),
                                        device_id_type=MESH)


def _join_start(g, layer, name):
    def body(g_ref, send_sem, recv_sem, out_ref, token):
        _join_copy(g_ref, out_ref, layer, send_sem, recv_sem, incoming=False).start()
        token[...] = jnp.zeros(token.shape, token.dtype)

    return pl.pallas_call(
        body, in_specs=[_HBM], out_specs=(_SEM, _SEM, _HBM, _VMEM),
        out_shape=(pltpu.SemaphoreType.DMA(()), pltpu.SemaphoreType.DMA(()), pltpu.HBM(g.shape, g.dtype), _TOKEN),
        input_output_aliases={0: 2}, name=name, compiler_params=_split_params(),
    )(pltpu.with_memory_space_constraint(g, pltpu.HBM))


def _join_wait(send_sem, recv_sem, g, layer, after, name):
    def body(g_ref, send_sem, recv_sem, after_ref, out_ref):
        _join_copy(g_ref, out_ref, layer, send_sem, recv_sem, incoming=False).wait_send()
        _join_copy(g_ref, out_ref, layer, send_sem, recv_sem, incoming=True).wait_recv()

    return pl.pallas_call(
        body, in_specs=[_HBM, _SEM, _SEM, _ANY], out_specs=_HBM, out_shape=pltpu.HBM(g.shape, g.dtype),
        input_output_aliases={0: 0}, name=name, compiler_params=_split_params(),
    )(g, send_sem, recv_sem, after)


def _all_reduce_small(v, name):
    rows, cols = v.shape

    def body(v_ref, out_ref, buf, send_sems, recv_sems):
        x, y, c, _ = _place()
        me = 4 * x + 2 * y + c
        buf[me] = v_ref[...]
        peers = []
        for k in range(1, N_DEV):
            px, py, pc = (x + (k >> 2)) % 2, (y + ((k >> 1) & 1)) % 2, (c + (k & 1)) % 2
            peers.append((px, py, pc))
        sends = []
        for k, peer in enumerate(peers):
            cp = pltpu.make_async_remote_copy(src_ref=v_ref, dst_ref=buf.at[me], send_sem=send_sems.at[k],
                                              recv_sem=recv_sems.at[k], device_id=peer, device_id_type=MESH)
            cp.start()
            sends.append(cp)
        for k, (px, py, pc) in enumerate(peers):
            pltpu.make_async_remote_copy(src_ref=v_ref, dst_ref=buf.at[4 * px + 2 * py + pc], send_sem=send_sems.at[k],
                                         recv_sem=recv_sems.at[k], device_id=(px, py, pc),
                                         device_id_type=MESH).wait_recv()
        for cp in sends:
            cp.wait_send()
        acc = buf[0]
        for i in range(1, N_DEV):
            acc = acc + buf[i]
        out_ref[...] = acc

    vmem = pl.BlockSpec(memory_space=pltpu.VMEM)
    return pl.pallas_call(
        body, in_specs=[vmem], out_specs=vmem, out_shape=jax.ShapeDtypeStruct((rows, cols), F32),
        scratch_shapes=[pltpu.VMEM((N_DEV, rows, cols), F32), pltpu.SemaphoreType.DMA((N_DEV - 1,)),
                        pltpu.SemaphoreType.DMA((N_DEV - 1,))],
        name=name, compiler_params=pltpu.CompilerParams(),
    )(v)


def _reduce_scatter_sum(started, after, layer, n_layers, prev, tag):
    send_sems, recv_sems, g, land, _ = started
    g, land = _scatter_wait(send_sems, recv_sems, g, land, after, f"rs_wait_{tag}")
    f = _device_sum(land, g, layer, n_layers, prev, f"rs_sum_{tag}")
    return _join_start(f, layer, f"rs_join_start_{tag}")


def _split_w_in(wg, Hf, name):
    ns, D, cols = wg.shape
    a = 3 * Hf * HEAD_DIM
    n6 = ns * cols - Hf
    br = _row_block(D)

    def body(w_ref, w6_ref, wf_ref):
        nat = jnp.concatenate([w_ref[s] for s in range(ns)], axis=1)
        w6_ref[...] = jnp.concatenate([nat[:, :a], nat[:, a + Hf:]], axis=1)
        wf_ref[...] = nat[:, a:a + Hf]

    w6, wf = pl.pallas_call(
        body, grid=(D // br,), in_specs=[pl.BlockSpec((ns, br, cols), lambda i: (0, i, 0))],
        out_specs=[pl.BlockSpec((br, n6), lambda i: (i, 0)), pl.BlockSpec((br, Hf), lambda i: (i, 0))],
        out_shape=[jax.ShapeDtypeStruct((D, n6), wg.dtype), jax.ShapeDtypeStruct((D, Hf), wg.dtype)],
        name=name, compiler_params=_cparams(("parallel",)),
    )(wg)
    return w6, wf.T


def _join_dw_in(dw6, dwf_t, Hf, name):
    D, n6 = dw6.shape
    a = 3 * Hf * HEAD_DIM
    cols = (n6 + Hf) // N_CHIPS
    br = _row_block(D)

    def body(w6_ref, wf_ref, out_ref):
        w6 = w6_ref[...]
        nat = jnp.concatenate([w6[:, :a], wf_ref[...], w6[:, a:]], axis=1)
        for s in range(N_CHIPS):
            out_ref[s] = nat[:, s * cols:(s + 1) * cols]

    return pl.pallas_call(
        body, grid=(D // br,),
        in_specs=[pl.BlockSpec((br, n6), lambda i: (i, 0)), pl.BlockSpec((br, Hf), lambda i: (i, 0))],
        out_specs=pl.BlockSpec((N_CHIPS, br, cols), lambda i: (0, i, 0)),
        out_shape=jax.ShapeDtypeStruct((N_CHIPS, D, cols), dw6.dtype),
        name=name, compiler_params=_cparams(("parallel",)),
    )(dw6, dwf_t.T.astype(dw6.dtype))


def _tied(v, tokens, name):
    return _tie(v, tokens, name) if tokens else v


def _layer_fwd(x, p, weight, bias, tokens, tag):
    Hf, Hd = p["forget_b"].shape[0], bias.shape[1]
    h1 = _rms_fwd(x, _tied(p["norm1_g"], tokens, f"tie_norm1_{tag}"), f"norm1_{tag}")
    w6, wf_t = _split_w_in(weight("w_in", h1), Hf, f"split_w_in_{tag}")
    n_a = 3 * Hf * HEAD_DIM
    proj_a = _mm_nn(h1, w6, f"proj_a_{tag}", [CDT], epi=lambda acc: (acc,), b_cols=(0, n_a))[0]
    proj_b = _mm_nn(h1, w6, f"proj_b_{tag}", [F32], b_cols=(n_a, w6.shape[1] - n_a))[0]
    f_t = _mm_nt(wf_t, h1, f"fproj_{tag}", [F32])[0]
    qc, kc = _fox_bias_operands(_gates_fwd(f_t, p["forget_b"], f"gates_{tag}"), f"fox_operands_{tag}")
    y_a, lse_a = _fox_fwd(proj_a, qc, kc, Hf, f"fox_{tag}")
    y_b, lse_b = _dil_fwd(proj_b, bias, Hd, f"dil_{tag}")
    mixed = _pair_norm_fwd(y_a, y_b, p["outnorm_a_g"], p["outnorm_b_g"], f"norm_ab_{tag}")
    w_out = weight("w_out", mixed)
    w_out = w_out.reshape(-1, w_out.shape[2])
    x1 = _mm_nn(mixed, w_out, f"attn_out_{tag}", [F32], extras=[x])[0]
    h2 = _rms_fwd(x1, p["norm2_g"], f"norm2_{tag}")
    w_mi = weight("w_mlp_in", h2)
    u, act = _mm_nn(h2, w_mi, f"mlp_in_{tag}", [CDT, CDT], b_slots=True,
                    epi=lambda acc: (acc, jnp.square(jnp.maximum(acc, 0.0))))
    w_mo = weight("w_mlp_out", act)
    w_mo = w_mo.reshape(-1, w_mo.shape[2])
    x2 = _mm_nn(act, w_mo, f"mlp_out_{tag}", [F32], extras=[x1])[0]
    saved = dict(x=x, h1=h1, proj_a=proj_a, proj_b=proj_b, f_t=f_t, qc=qc, kc=kc, y_a=y_a, lse_a=lse_a, y_b=y_b,
                 lse_b=lse_b, mixed=mixed, x1=x1, h2=h2, u=u, act=act, w6=w6, wf_t=wf_t, w_out=w_out, w_mi=w_mi,
                 w_mo=w_mo)
    return x2, saved


def _layer_bwd(dx2, dx2c, p, send, bias, sv, defer_w_out, tag):
    Hf, Hd = p["forget_b"].shape[0], bias.shape[1]
    E = HEAD_DIM
    rows = lambda g: g.reshape(N_CHIPS, -1, g.shape[1])
    du = _mm_nt(dx2c, sv["w_mo"], f"d_act_{tag}", [CDT], extras=[sv["u"]],
                epi=lambda acc, u: (acc * (2.0 * jnp.maximum(u.astype(F32), 0.0)),))[0]
    tokens = send("w_mlp_out", rows(_mm_tn(sv["act"], dx2c, f"dw_mlp_out_{tag}", CDT)))
    dh2 = _mm_nt(du, sv["w_mi"], f"d_h2_{tag}", [F32], b_slots=True)[0]
    tokens = tokens + send("w_mlp_in", _mm_tn(sv["h2"], du, f"dw_mlp_in_{tag}", CDT, out_slots=N_CHIPS))
    dx1, dx1c, g_norm2 = _rms_bwd(sv["x1"], _tied(p["norm2_g"], tokens, f"tie_norm2_{tag}"), dh2, dx2,
                                  f"d_norm2_{tag}")
    dmixed = _mm_nt(dx1c, sv["w_out"], f"d_mixed_{tag}", [F32])[0]
    send_w_out = lambda: send("w_out", rows(_mm_tn(sv["mixed"], dx1c, f"dw_out_{tag}", CDT)))
    tokens = [] if defer_w_out else send_w_out()
    dy_a, dy_b, g_na, g_nb = _pair_norm_bwd(sv["y_a"], sv["y_b"], _tied(p["outnorm_a_g"], tokens, f"tie_norm_a_{tag}"),
                                            p["outnorm_b_g"], dmixed, f"d_norm_ab_{tag}")
    dq_a, dcq, dk_a, dv_a, dck = _fox_bwd(sv["proj_a"], sv["qc"], sv["kc"], sv["lse_a"], sv["y_a"], dy_a, Hf,
                                          f"fox_bwd_{tag}")
    df, dfc, g_fb = _gates_bwd(sv["f_t"], p["forget_b"], dcq[:, ::E].T, dck.reshape(Hf, -1), f"d_gates_{tag}")
    dq_b, dk_b, dv_b, dbias = _dil_bwd(sv["proj_b"], bias, sv["y_b"], dy_b, sv["lse_b"], Hd, f"dil_bwd_{tag}")
    dproj = jnp.concatenate([dq_a, dk_a, dv_a, dq_b, dk_b, dv_b], axis=1)
    g_w6 = _mm_tn(sv["h1"], dproj, f"dw_in_{tag}", CDT)
    g_wf_t = _mm_nn(dfc, sv["h1"], f"dw_f_{tag}", [F32])[0]
    tokens = send("w_in", _join_dw_in(g_w6, g_wf_t, Hf, f"join_dw_in_{tag}"))
    dh1_f = _mm_tn(dfc, _tied(sv["wf_t"], tokens, f"tie_wf_{tag}"), f"d_h1_f_{tag}", F32)
    dh1 = _mm_nt(dproj, sv["w6"], f"d_h1_{tag}", [F32], extras=[dh1_f])[0]
    dx, dxc, g_norm1 = _rms_bwd(sv["x"], p["norm1_g"], dh1, dx1, f"d_norm1_{tag}")
    grads = dict(norm1_g=g_norm1[0], norm2_g=g_norm2[0], outnorm_a_g=g_na[0], outnorm_b_g=g_nb[0],
                 forget_b=g_fb[:, 0], dbias=dbias)
    return dx, dxc, grads, (send_w_out if defer_w_out else None)


_LAYER_SMALL = ("norm1_g", "forget_b", "outnorm_a_g", "outnorm_b_g", "norm2_g")


def _local_step(x, target, small, weight, send, tokens):
    depth = small["norm1_g"].shape[0]
    buckets = _bucket_table()
    bias = _bias_table(small["rel_bias"], buckets, "bias_table")
    layers, saved = [], []
    for l in range(depth):
        p = {k: small[k][l] for k in _LAYER_SMALL}
        layers.append(p)
        x, sv = _layer_fwd(x, p, functools.partial(weight, l), bias, tokens if l == 0 else [], f"l{l}")
        saved.append(sv)
    dx, dxc, g_final, loss = _loss_bwd(x, small["final_norm_g"], target, "loss")
    layer_grads = [None] * depth
    for l in reversed(range(depth)):
        dx, dxc, layer_grads[l], last = _layer_bwd(dx, dxc, layers[l], functools.partial(send, l), bias, saved[l],
                                                   l == 0, f"l{l}")
    tokens = last()
    dbias = functools.reduce(jnp.add, [g["dbias"] for g in layer_grads])
    g_rel = _bias_table_bwd(dbias, buckets, "d_bias_table")[:, 0, :].T
    small_grads = dict(final_norm_g=g_final[0], rel_bias=g_rel,
                       **{k: jnp.stack([g[k] for g in layer_grads]) for k in _LAYER_SMALL})
    return loss[0, 0], dx, small_grads, tokens


_BIG = ("w_in", "w_out", "w_mlp_in", "w_mlp_out")
_SMALL = ("norm1_g", "forget_b", "rel_bias", "outnorm_a_g", "outnorm_b_g", "norm2_g", "final_norm_g")
_ORDER = ("norm1_g", "w_in", "forget_b", "rel_bias", "outnorm_a_g", "outnorm_b_g", "w_out", "norm2_g", "w_mlp_in",
          "w_mlp_out", "final_norm_g")


def _pack_small(d):
    flat = jnp.concatenate([d[k].reshape(-1) for k in _SMALL])
    rows = -(-flat.shape[0] // (8 * SMALL_COLS)) * 8
    return jnp.pad(flat, (0, rows * SMALL_COLS - flat.shape[0])).reshape(rows, SMALL_COLS)


def _unpack_small(packed, like):
    flat, out, at = packed.reshape(-1), {}, 0
    for k in _SMALL:
        n = like[k].size
        out[k] = flat[at:at + n].reshape(like[k].shape)
        at += n
    return out


def kernel(x, norm1_g, w_in, forget_b, rel_bias, outnorm_a_g, outnorm_b_g, w_out, norm2_g, w_mlp_in, w_mlp_out, final_norm_g, loss_target, m_norm1_g, m_w_in, m_forget_b, m_rel_bias, m_outnorm_a_g, m_outnorm_b_g, m_w_out, m_norm2_g, m_w_mlp_in, m_w_mlp_out, m_final_norm_g, v_norm1_g, v_w_in, v_forget_b, v_rel_bias, v_outnorm_a_g, v_outnorm_b_g, v_w_out, v_norm2_g, v_w_mlp_in, v_w_mlp_out, v_final_norm_g):
    w = dict(norm1_g=norm1_g, w_in=w_in, forget_b=forget_b, rel_bias=rel_bias, outnorm_a_g=outnorm_a_g,
             outnorm_b_g=outnorm_b_g, w_out=w_out, norm2_g=norm2_g, w_mlp_in=w_mlp_in, w_mlp_out=w_mlp_out,
             final_norm_g=final_norm_g)
    m = dict(norm1_g=m_norm1_g, w_in=m_w_in, forget_b=m_forget_b, rel_bias=m_rel_bias, outnorm_a_g=m_outnorm_a_g,
             outnorm_b_g=m_outnorm_b_g, w_out=m_w_out, norm2_g=m_norm2_g, w_mlp_in=m_w_mlp_in,
             w_mlp_out=m_w_mlp_out, final_norm_g=m_final_norm_g)
    v = dict(norm1_g=v_norm1_g, w_in=v_w_in, forget_b=v_forget_b, rel_bias=v_rel_bias, outnorm_a_g=v_outnorm_a_g,
             outnorm_b_g=v_outnorm_b_g, w_out=v_w_out, norm2_g=v_norm2_g, w_mlp_in=v_w_mlp_in,
             w_mlp_out=v_w_mlp_out, final_norm_g=v_final_norm_g)
    depth = w_in.shape[0]
    small = {k: w[k] for k in _SMALL}

    gathers, passed, tokens = {}, {}, []
    for l in range(depth):
        for k in _BIG:
            buf = _cast_into_slot(w[k], l, f"cast_{k}_l{l}")
            start = _relay_start if k == _BIG[0] else _gather_start
            send_sems, recv_sems, buf, token = start(buf, tokens, f"gather_start_{k}_l{l}")
            gathers[l, k], tokens = (send_sems, recv_sems, buf), [token]

    def weight(l, k, after):
        if k == _BIG[-1] and l + 1 < depth:
            passed[l + 1] = _relay_pass(*gathers[l + 1, _BIG[0]], after, f"gather_pass_{_BIG[0]}_l{l + 1}")
        if k != _BIG[0]:
            return _gather_wait(*gathers[l, k], after, f"gather_wait_{k}_l{l}")
        if l not in passed:
            passed[l] = _relay_pass(*gathers[l, k], after, f"gather_pass_{k}_l{l}")
        return _relay_wait(*passed[l], after, f"gather_wait_{k}_l{l}")

    scatters = {}

    def send(l, k, g):
        scatters[l, k] = _scatter_start(g, f"rs_start_{k}_l{l}")
        return [scatters[l, k][4]]

    loss, grad_x, small_grads, tokens = _local_step(x[0], loss_target[0], small, weight, send, tokens)
    loss = lax.psum(loss, ("x", "y", "c"))

    grads, delta, new_m, new_v = {}, {}, {}, {}
    packed = _tied(_pack_small(small_grads), tokens, "tie_small")
    after, seen, joining = packed, {k: 0 for k in _BIG}, None

    def joined(after):
        (l, k), (send_sem, recv_sem, g) = joining
        grads[k] = _join_wait(send_sem, recv_sem, g, l, after, f"rs_join_wait_{k}_l{l}")
        seen[k] += 1
        if seen[k] < depth:
            return after
        shape = w[k].shape
        flat = lambda t: t.reshape(-1, shape[-1])
        outs = _adamw(flat(w[k]), flat(grads[k]), flat(m[k]), flat(v[k]), f"adamw_{k}")
        grads[k], delta[k], new_m[k], new_v[k] = (t.reshape(shape) for t in outs)
        return outs[1]

    for (l, k), started in scatters.items():
        assert joining is None or joining[0][1] != k
        send_sem, recv_sem, g, token = _reduce_scatter_sum(started, after, l, depth, grads.get(k), f"{k}_l{l}")
        if joining is not None:
            after = joined(token)
        joining = ((l, k), (send_sem, recv_sem, g))
    after = joined(after)
    small_sums = _all_reduce_small(_tied(packed, [after], "tie_small_sums"), "small_all_reduce")
    grads.update(_unpack_small(small_sums, small))
    _, d_, m_, v_ = _adamw(_pack_small(small), _pack_small({k: grads[k] for k in _SMALL}),
                           _pack_small({k: m[k] for k in _SMALL}), _pack_small({k: v[k] for k in _SMALL}), "adamw_small")
    delta.update(_unpack_small(d_, small))
    new_m.update(_unpack_small(m_, small))
    new_v.update(_unpack_small(v_, small))

    return (loss, grad_x[None], *[grads[k] for k in _ORDER], *[delta[k] for k in _ORDER],
            *[new_m[k] for k in _ORDER], *[new_v[k] for k in _ORDER])
```

```python
import functools

import jax
import jax.numpy as jnp
from jax import lax
from jax.experimental import pallas as pl
from jax.experimental.pallas import tpu as pltpu

F32 = jnp.float32
CDT = jnp.bfloat16
HEAD_DIM = 128
NORM_EPS = 1e-6
NEG_INF = -1e30
LOG2E = 1.4426950408889634
REL_BUCKETS = 32
REL_MAX_DISTANCE = 2048
DIL_PATTERNS = ((128, 1), (512, 4), (2048, 16))
DIL_BLOCK = 128
ADAM_LR, ADAM_B1, ADAM_B2, ADAM_EPS, ADAM_WD, ADAM_STEP = 0.001, 0.9, 0.999, 1e-08, 0.01, 10
N_CHIPS = 4
N_DEV = 8
VMEM_LIMIT_BYTES = 56 * 1024 * 1024
SMALL_COLS = 1024
MESH = pl.DeviceIdType.MESH


def _cparams(sem=None):
    return pltpu.CompilerParams(dimension_semantics=sem, vmem_limit_bytes=VMEM_LIMIT_BYTES)


def _tile(dim, pref):
    t = min(pref, dim)
    t -= t % 128
    while t >= 128:
        if dim % t == 0:
            return t
        t -= 128
    return dim


def _rowwise(fn, ins, out_dtypes, name, bs=512, consts=()):
    R, C = ins[0].shape
    bs = min(bs, R)
    n_in, n_c = len(ins), len(consts)

    def body(*refs):
        vals = [r[...] for r in refs[:n_in + n_c]]
        res = fn(*vals)
        for o, r in zip(refs[n_in + n_c:], res):
            o[...] = r.astype(o.dtype)

    row = pl.BlockSpec((bs, C), lambda i: (i, 0))
    return pl.pallas_call(
        body, grid=(R // bs,),
        in_specs=[row] * n_in + [pl.BlockSpec((1, c.shape[-1]), lambda i: (0, 0)) for c in consts],
        out_specs=[row] * len(out_dtypes),
        out_shape=[jax.ShapeDtypeStruct((R, C), d) for d in out_dtypes],
        name=name, compiler_params=_cparams(("parallel",)),
    )(*ins, *[c.reshape(1, -1) for c in consts])


def _rms_fwd(x, g, name):
    def fn(xf, gg):
        r = lax.rsqrt(jnp.mean(xf * xf, axis=-1, keepdims=True) + NORM_EPS)
        return ((xf * r) * gg,)
    return _rowwise(fn, [x], [CDT], name, consts=[g])[0]


def _rms_bwd(x, g, dh, dres, name, bs=512):
    S, D = x.shape
    bs = min(bs, S)
    has_res = dres is not None

    def body(*refs):
        x_ref, g_ref, dh_ref = refs[:3]
        dx_ref, dxc_ref, dg_ref = refs[-3:]
        xf = x_ref[...]
        r = lax.rsqrt(jnp.mean(xf * xf, axis=-1, keepdims=True) + NORM_EPS)
        xhat = xf * r
        dh_ = dh_ref[...].astype(F32)
        dxhat = dh_ * g_ref[...]
        dx = r * (dxhat - xhat * jnp.mean(dxhat * xhat, axis=-1, keepdims=True))
        if has_res:
            dx = dx + refs[3][...]
        dx_ref[...] = dx
        dxc_ref[...] = dx.astype(dxc_ref.dtype)
        part = jnp.sum(dh_ * xhat, axis=0, keepdims=True)

        @pl.when(pl.program_id(0) == 0)
        def _():
            dg_ref[...] = part

        @pl.when(pl.program_id(0) > 0)
        def _():
            dg_ref[...] += part

    row = pl.BlockSpec((bs, D), lambda i: (i, 0))
    one = pl.BlockSpec((1, D), lambda i: (0, 0))
    ins = [x, g.reshape(1, D), dh] + ([dres] if has_res else [])
    return pl.pallas_call(
        body, grid=(S // bs,),
        in_specs=[row, one, row] + ([row] if has_res else []),
        out_specs=[row, row, one],
        out_shape=[jax.ShapeDtypeStruct((S, D), F32), jax.ShapeDtypeStruct((S, D), CDT),
                   jax.ShapeDtypeStruct((1, D), F32)],
        name=name, compiler_params=_cparams(("arbitrary",)),
    )(*ins)


def _pair_norm_fwd(y_a, y_b, g_a, g_b, name, bs=512):
    S, Da = y_a.shape
    Db = y_b.shape[1]
    bs = min(bs, S)

    def body(a_ref, b_ref, ga_ref, gb_ref, o_ref):
        def norm(x, g):
            r = lax.rsqrt(jnp.mean(x * x, axis=-1, keepdims=True) + NORM_EPS)
            return ((x * r) * g).astype(o_ref.dtype)
        o_ref[:, :Da] = norm(a_ref[...], ga_ref[...])
        o_ref[:, Da:] = norm(b_ref[...], gb_ref[...])

    row = lambda n: pl.BlockSpec((bs, n), lambda i: (i, 0))
    one = lambda n: pl.BlockSpec((1, n), lambda i: (0, 0))
    return pl.pallas_call(
        body, grid=(S // bs,), in_specs=[row(Da), row(Db), one(Da), one(Db)], out_specs=row(Da + Db),
        out_shape=jax.ShapeDtypeStruct((S, Da + Db), CDT), name=name, compiler_params=_cparams(("parallel",)),
    )(y_a, y_b, g_a.reshape(1, Da), g_b.reshape(1, Db))


def _pair_norm_bwd(y_a, y_b, g_a, g_b, dmixed, name, bs=512):
    S, Da = y_a.shape
    Db = y_b.shape[1]
    bs = min(bs, S)

    def body(a_ref, b_ref, ga_ref, gb_ref, dm_ref, da_ref, db_ref, dga_ref, dgb_ref):
        def one(x_ref, g_ref, dh, dx_ref, dg_ref):
            xf = x_ref[...]
            r = lax.rsqrt(jnp.mean(xf * xf, axis=-1, keepdims=True) + NORM_EPS)
            xhat = xf * r
            dxhat = dh * g_ref[...]
            dx_ref[...] = r * (dxhat - xhat * jnp.mean(dxhat * xhat, axis=-1, keepdims=True))
            part = jnp.sum(dh * xhat, axis=0, keepdims=True)

            @pl.when(pl.program_id(0) == 0)
            def _():
                dg_ref[...] = part

            @pl.when(pl.program_id(0) > 0)
            def _():
                dg_ref[...] += part

        dm = dm_ref[...]
        one(a_ref, ga_ref, dm[:, :Da], da_ref, dga_ref)
        one(b_ref, gb_ref, dm[:, Da:], db_ref, dgb_ref)

    row = lambda n: pl.BlockSpec((bs, n), lambda i: (i, 0))
    one_ = lambda n: pl.BlockSpec((1, n), lambda i: (0, 0))
    return pl.pallas_call(
        body, grid=(S // bs,), in_specs=[row(Da), row(Db), one_(Da), one_(Db), row(Da + Db)],
        out_specs=[row(Da), row(Db), one_(Da), one_(Db)],
        out_shape=[jax.ShapeDtypeStruct((S, Da), F32), jax.ShapeDtypeStruct((S, Db), F32),
                   jax.ShapeDtypeStruct((1, Da), F32), jax.ShapeDtypeStruct((1, Db), F32)],
        name=name, compiler_params=_cparams(("arbitrary",)),
    )(y_a, y_b, g_a.reshape(1, Da), g_b.reshape(1, Db), dmixed)


def _loss_bwd(x, g, target, name, bs=512):
    S, D = x.shape
    bs = min(bs, S)

    def body(x_ref, g_ref, t_ref, dx_ref, dxc_ref, dg_ref, loss_ref):
        xf = x_ref[...]
        r = lax.rsqrt(jnp.mean(xf * xf, axis=-1, keepdims=True) + NORM_EPS)
        xhat = xf * r
        err = xhat * g_ref[...] - t_ref[...]
        lpart = 0.5 * jnp.sum(jnp.mean(err * err, axis=-1, keepdims=True), axis=0, keepdims=True)
        dy = err / D
        dxhat = dy * g_ref[...]
        dx = r * (dxhat - xhat * jnp.mean(dxhat * xhat, axis=-1, keepdims=True))
        dx_ref[...] = dx
        dxc_ref[...] = dx.astype(dxc_ref.dtype)
        gpart = jnp.sum(dy * xhat, axis=0, keepdims=True)

        @pl.when(pl.program_id(0) == 0)
        def _():
            dg_ref[...] = gpart
            loss_ref[...] = lpart

        @pl.when(pl.program_id(0) > 0)
        def _():
            dg_ref[...] += gpart
            loss_ref[...] += lpart

    row = pl.BlockSpec((bs, D), lambda i: (i, 0))
    one = pl.BlockSpec((1, D), lambda i: (0, 0))
    return pl.pallas_call(
        body, grid=(S // bs,),
        in_specs=[row, one, row],
        out_specs=[row, row, one, pl.BlockSpec((1, 1), lambda i: (0, 0))],
        out_shape=[jax.ShapeDtypeStruct((S, D), F32), jax.ShapeDtypeStruct((S, D), CDT),
                   jax.ShapeDtypeStruct((1, D), F32), jax.ShapeDtypeStruct((1, 1), F32)],
        name=name, compiler_params=_cparams(("arbitrary",)),
    )(x, g.reshape(1, D), target)


_NN = (((1,), (0,)), ((), ()))
_NT = (((1,), (1,)), ((), ()))
_TN = (((0,), (0,)), ((), ()))


def _mm(a, b, *, M, N, K, a_spec, b_spec, o_spec, dims, tm, tn, tk, name, out_shapes, extras=(), epi=None):
    nk = K // tk
    n_ex, n_out = len(extras), len(out_shapes)
    in_place = epi is None
    if in_place:
        assert n_out == 1 and n_ex <= 1 and out_shapes[0].dtype == F32
        epi = lambda acc, *r: (acc + r[0] if r else acc,)

    def body(*refs):
        a_ref, b_ref = refs[0], refs[1]
        ex = refs[2:2 + n_ex]
        outs = refs[2 + n_ex:2 + n_ex + n_out]
        part = lax.dot_general(a_ref[...], b_ref[...], dims, preferred_element_type=F32)

        def finish(acc):
            for o, r in zip(outs, epi(acc, *[e[...] for e in ex])):
                o[...] = r.astype(o.dtype)

        if nk == 1:
            finish(part)
        elif in_place:
            k = pl.program_id(2)

            @pl.when(k == 0)
            def _():
                finish(part)

            @pl.when(k > 0)
            def _():
                outs[0][...] += part
        else:
            acc_ref = refs[-1]
            k = pl.program_id(2)

            @pl.when(k == 0)
            def _():
                acc_ref[...] = part

            @pl.when(k > 0)
            def _():
                acc_ref[...] += part

            @pl.when(k == nk - 1)
            def _():
                finish(acc_ref[...])

    ex_spec = pl.BlockSpec((tm, tn), lambda i, j, k: (i, j))
    return pl.pallas_call(
        body, grid=(M // tm, N // tn, nk),
        in_specs=[a_spec, b_spec] + [ex_spec] * n_ex,
        out_specs=[o_spec] * n_out,
        out_shape=out_shapes,
        scratch_shapes=[pltpu.VMEM((tm, tn), F32)] if nk > 1 and not in_place else [],
        name=name, compiler_params=_cparams(("parallel", "parallel", "arbitrary")),
    )(a, b, *extras)


def _mm_tiles(K):
    return (2048, 512, 2048) if K <= 2048 else (1024, 1024, 2048)


def _mm_nn(a, b, name, out_dtypes, extras=(), epi=None, b_slots=False, b_cols=None):
    M, K = a.shape
    tm, tn, tk = _mm_tiles(K)
    if b_slots:
        ns, _, Ns = b.shape
        N = ns * Ns
        tn = _tile(Ns, tn)
        npb = Ns // tn
        tk_ = _tile(K, tk)
        b_spec = pl.BlockSpec((None, tk_, tn), lambda i, j, k: (j // npb, k, j % npb))
    else:
        first, N = b_cols if b_cols is not None else (0, b.shape[1])
        tn = _tile(N, tn)
        assert first % tn == 0
        tk_ = _tile(K, tk)
        b_spec = pl.BlockSpec((tk_, tn), lambda i, j, k: (k, first // tn + j))
    tm = _tile(M, tm)
    return _mm(a, b, M=M, N=N, K=K, a_spec=pl.BlockSpec((tm, tk_), lambda i, j, k: (i, k)), b_spec=b_spec,
               o_spec=pl.BlockSpec((tm, tn), lambda i, j, k: (i, j)), dims=_NN, tm=tm, tn=tn, tk=tk_, name=name,
               out_shapes=[jax.ShapeDtypeStruct((M, N), d) for d in out_dtypes], extras=extras, epi=epi)


def _mm_nt(a, b, name, out_dtypes, extras=(), epi=None, b_slots=False):
    M, K = a.shape
    tm, tn, tk = _mm_tiles(K)
    tm = _tile(M, tm)
    if b_slots:
        ns, N, Ks = b.shape
        tk_ = _tile(Ks, tk)
        kpb = Ks // tk_
        tn = _tile(N, tn)
        b_spec = pl.BlockSpec((None, tn, tk_), lambda i, j, k: (k // kpb, j, k % kpb))
    else:
        N = b.shape[0]
        tk_ = _tile(K, tk)
        tn = _tile(N, tn)
        b_spec = pl.BlockSpec((tn, tk_), lambda i, j, k: (j, k))
    return _mm(a, b, M=M, N=N, K=K, a_spec=pl.BlockSpec((tm, tk_), lambda i, j, k: (i, k)), b_spec=b_spec,
               o_spec=pl.BlockSpec((tm, tn), lambda i, j, k: (i, j)), dims=_NT, tm=tm, tn=tn, tk=tk_, name=name,
               out_shapes=[jax.ShapeDtypeStruct((M, N), d) for d in out_dtypes], extras=extras, epi=epi)


def _mm_tn(a, b, name, out_dtype, out_slots=0, tm=2048, tn=1024, tk=2048):
    K, M = a.shape
    N = b.shape[1]
    tm, tk_ = _tile(M, tm), _tile(K, tk)
    if out_slots:
        Ns = N // out_slots
        tn = _tile(Ns, tn)
        npb = Ns // tn
        o_spec = pl.BlockSpec((None, tm, tn), lambda i, j, k: (j // npb, i, j % npb))
        out_shape = jax.ShapeDtypeStruct((out_slots, M, Ns), out_dtype)
    else:
        tn = _tile(N, tn)
        o_spec = pl.BlockSpec((tm, tn), lambda i, j, k: (i, j))
        out_shape = jax.ShapeDtypeStruct((M, N), out_dtype)
    return _mm(a, b, M=M, N=N, K=K, a_spec=pl.BlockSpec((tk_, tm), lambda i, j, k: (k, i)),
               b_spec=pl.BlockSpec((tk_, tn), lambda i, j, k: (k, j)), o_spec=o_spec, dims=_TN,
               tm=tm, tn=tn, tk=tk_, name=name, out_shapes=[out_shape],
               epi=None if out_dtype == F32 else (lambda acc: (acc,)))[0]


GATE_BLOCK = 512


def _split3(v):
    hi = v.astype(jnp.bfloat16)
    r1 = v - hi.astype(F32)
    mid = r1.astype(jnp.bfloat16)
    lo = (r1 - mid.astype(F32)).astype(jnp.bfloat16)
    return hi, mid, lo


def _exact_dot(v, tri):
    return functools.reduce(jnp.add, [jnp.dot(t, tri, preferred_element_type=F32) for t in _split3(v)])


def _gates_fwd(f_t, b, name):
    H, S = f_t.shape
    nb = _tile(S, GATE_BLOCK)
    inv_scale = HEAD_DIM ** 0.5

    def body(f_ref, b_ref, c_ref):
        upper = (lax.broadcasted_iota(jnp.int32, (nb, nb), 0)
                 <= lax.broadcasted_iota(jnp.int32, (nb, nb), 1)).astype(jnp.bfloat16)
        carry = jnp.zeros((H, 1), F32)
        for i in range(S // nb):
            z = f_ref[:, i * nb:(i + 1) * nb] + b_ref[...]
            logf = jnp.minimum(z, 0.0) - jnp.log1p(jnp.exp(-jnp.abs(z)))
            cs = _exact_dot(logf, upper) + carry
            for j, t in enumerate(_split3(cs * inv_scale)):
                c_ref[j, :, i * nb:(i + 1) * nb] = t
            carry = cs[:, nb - 1:nb]

    return pl.pallas_call(body, out_shape=jax.ShapeDtypeStruct((3, H, S), jnp.bfloat16), name=name,
                          compiler_params=_cparams())(f_t, b.reshape(H, 1))


def _gates_bwd(f_t, b, dcq, dck, name):
    H, S = f_t.shape
    nb = _tile(S, GATE_BLOCK)

    def body(f_ref, b_ref, dcq_ref, dck_ref, df_ref, dfc_ref, db_ref):
        lower = (lax.broadcasted_iota(jnp.int32, (nb, nb), 0)
                 >= lax.broadcasted_iota(jnp.int32, (nb, nb), 1)).astype(jnp.bfloat16)
        carry = jnp.zeros((H, 1), F32)
        db = jnp.zeros((H, 1), F32)
        for i in reversed(range(S // nb)):
            sl = slice(i * nb, (i + 1) * nb)
            dc = dcq_ref[:, sl] - dck_ref[:, sl]
            dlogf = _exact_dot(dc, lower) + carry
            carry = dlogf[:, 0:1]
            z = f_ref[:, sl] + b_ref[...]
            df = dlogf / (1.0 + jnp.exp(z))
            df_ref[:, sl] = df
            dfc_ref[:, sl] = df.astype(dfc_ref.dtype)
            db = db + jnp.sum(df, axis=1, keepdims=True)
        db_ref[...] = db

    return pl.pallas_call(
        body, out_shape=[jax.ShapeDtypeStruct((H, S), F32), jax.ShapeDtypeStruct((H, S), CDT),
                         jax.ShapeDtypeStruct((H, 1), F32)],
        name=name, compiler_params=_cparams())(f_t, b.reshape(H, 1), dcq, dck)


FOX_BLOCK = 1024


def _fox_bias_operands(csplit, name, bs=1024):
    _, H, S = csplit.shape
    E = HEAD_DIM
    bs = _tile(S, bs)
    part = jnp.arange(3 * H)[:, None] // H
    head = jnp.arange(3 * H)[:, None] % H
    lane = jnp.arange(H * E)[None, :]
    place_q = (lane == head * E + part).astype(csplit.dtype)
    place_k = -(lane == head * E + 3 + part).astype(csplit.dtype)
    ones_q = ((lane % E >= 3) & (lane % E < 6)).astype(F32)
    ones_k = (lane % E < 3).astype(F32)

    def body(c_ref, pq_ref, pk_ref, oq_ref, ok_ref, qc_ref, kc_ref):
        c = c_ref[...]
        qc_ref[...] = (lax.dot_general(c, pq_ref[...], _TN, preferred_element_type=F32) + oq_ref[...]).astype(qc_ref.dtype)
        kc_ref[...] = (lax.dot_general(c, pk_ref[...], _TN, preferred_element_type=F32) + ok_ref[...]).astype(kc_ref.dtype)

    full = lambda a: pl.BlockSpec(a.shape, lambda i: (0, 0))
    out = pl.BlockSpec((bs, H * E), lambda i: (i, 0))
    return pl.pallas_call(
        body, grid=(S // bs,),
        in_specs=[pl.BlockSpec((3 * H, bs), lambda i: (0, i)), full(place_q), full(place_k), full(ones_q), full(ones_k)],
        out_specs=[out, out], out_shape=[jax.ShapeDtypeStruct((S, H * E), csplit.dtype)] * 2,
        name=name, compiler_params=_cparams(("parallel",)),
    )(csplit.reshape(3 * H, S), place_q, place_k, ones_q, ones_k)


def _fox_logits2(q_ref, qc_ref, k_ref, kc_ref, diag):
    q, k = q_ref[...], k_ref[...]
    qa = jnp.concatenate([q, qc_ref[...].astype(q.dtype)], axis=1)
    ka = jnp.concatenate([k, kc_ref[...].astype(k.dtype)], axis=1)
    s = lax.dot_general(qa, ka, _NT, preferred_element_type=F32) * (HEAD_DIM ** -0.5 * LOG2E)
    if diag:
        row = lax.broadcasted_iota(jnp.int32, s.shape, 0)
        col = lax.broadcasted_iota(jnp.int32, s.shape, 1)
        s = jnp.where(col <= row, s, NEG_INF)
    return s


def _fox_fwd(proj, qc, kc, H, name):
    S = proj.shape[0]
    E = HEAD_DIM
    blk = _tile(S, FOX_BLOCK)
    nq = S // blk

    def pair(t):
        qi = sum((t >= i * (i + 1) // 2).astype(jnp.int32) for i in range(1, nq)) if nq > 1 else 0 * t
        return qi, t - qi * (qi + 1) // 2

    def body(q_ref, qc_ref, k_ref, kc_ref, v_ref, o_ref, lse_ref, m_s, l_s, acc_s):
        qi, kj = pair(pl.program_id(1))

        @pl.when(kj == 0)
        def _():
            m_s[...] = jnp.full(m_s.shape, NEG_INF, F32)
            l_s[...] = jnp.zeros(l_s.shape, F32)
            acc_s[...] = jnp.zeros(acc_s.shape, F32)

        def step(diag):
            s = _fox_logits2(q_ref, qc_ref, k_ref, kc_ref, diag)
            m_prev = m_s[...]
            m_new = jnp.maximum(m_prev, jnp.max(s, axis=-1, keepdims=True))
            alpha = jnp.exp2(m_prev - m_new)
            p = jnp.exp2(s - m_new)
            l_s[...] = alpha * l_s[...] + jnp.sum(p, axis=-1, keepdims=True)
            acc_s[...] = alpha * acc_s[...] + jnp.dot(p.astype(CDT), v_ref[...], preferred_element_type=F32)
            m_s[...] = m_new

        pl.when(kj < qi)(lambda: step(False))
        pl.when(kj == qi)(lambda: step(True))

        @pl.when(kj == qi)
        def _():
            o_ref[...] = acc_s[...] / l_s[...]
            lse_ref[...] = jnp.broadcast_to(m_s[...] + jnp.log2(l_s[...]), lse_ref.shape)

    qspec = lambda off: pl.BlockSpec((blk, E), lambda h, t: (pair(t)[0], off + h))
    kspec = lambda off: pl.BlockSpec((blk, E), lambda h, t: (pair(t)[1], off + h))
    return pl.pallas_call(
        body, grid=(H, nq * (nq + 1) // 2),
        in_specs=[qspec(0), qspec(0), kspec(H), kspec(0), kspec(2 * H)],
        out_specs=[qspec(0)] * 2,
        out_shape=[jax.ShapeDtypeStruct((S, H * E), F32)] * 2,
        scratch_shapes=[pltpu.VMEM((blk, 1), F32), pltpu.VMEM((blk, 1), F32), pltpu.VMEM((blk, E), F32)],
        name=name, compiler_params=_cparams(("parallel", "arbitrary")),
    )(proj, qc, proj, kc, proj)


def _fox_bwd(proj, qc, kc, lse, o, do, H, name):
    S = proj.shape[0]
    E = HEAD_DIM
    blk = _tile(S, FOX_BLOCK)
    nq = S // blk
    scale = E ** -0.5

    def pair(t):
        first = lambda j: j * nq - j * (j - 1) // 2
        kj = sum((t >= first(j)).astype(jnp.int32) for j in range(1, nq)) if nq > 1 else 0 * t
        return kj, kj + t - first(kj)

    def body(q_ref, qc_ref, k_ref, kc_ref, v_ref, lse_ref, o_ref, do_ref,
             dq_ref, dcq_ref, dk_ref, dv_ref, dck_ref, dq_s, dcq_s, dk_s, dv_s, dck_s):
        kj, qi = pair(pl.program_id(1))

        @pl.when(qi == kj)
        def _():
            dk_s[...] = jnp.zeros(dk_s.shape, F32)
            dv_s[...] = jnp.zeros(dv_s.shape, F32)
            dck_s[...] = jnp.zeros(dck_s.shape, F32)

        def step(diag):
            do = do_ref[...]
            doc = do.astype(CDT)
            delta = jnp.sum(do * o_ref[...], axis=-1, keepdims=True)
            p = jnp.exp2(_fox_logits2(q_ref, qc_ref, k_ref, kc_ref, diag) - lse_ref[:, 0:1])
            dp = lax.dot_general(doc, v_ref[...], _NT, preferred_element_type=F32)
            ds = p * (dp - delta)
            dss = ds * scale
            dck_s[...] += jnp.sum(ds, axis=0, keepdims=True)
            dv_s[...] += jnp.dot(p.T.astype(CDT), doc, preferred_element_type=F32)
            dk_s[...] += jnp.dot(dss.T.astype(CDT), q_ref[...], preferred_element_type=F32)
            dq_part = jnp.dot(dss.astype(CDT), k_ref[...], preferred_element_type=F32)
            dc_part = jnp.sum(ds, axis=-1, keepdims=True)
            rows = pl.ds(pl.multiple_of(qi * blk, blk), blk)

            @pl.when(kj == 0)
            def _():
                dq_s[rows, :] = dq_part
                dcq_s[rows, :] = dc_part

            @pl.when(kj > 0)
            def _():
                dq_s[rows, :] += dq_part
                dcq_s[rows, :] += dc_part

        pl.when(qi > kj)(lambda: step(False))
        pl.when(qi == kj)(lambda: step(True))

        @pl.when(qi == nq - 1)
        def _():
            dk_ref[...] = dk_s[...].astype(dk_ref.dtype)
            dv_ref[...] = dv_s[...].astype(dv_ref.dtype)
            dck_ref[...] = dck_s[...].reshape(dck_ref.shape)

        @pl.when((qi == nq - 1) & (kj == nq - 1))
        def _():
            dq_ref[...] = dq_s[...].astype(dq_ref.dtype)
            dcq_ref[...] = jnp.broadcast_to(dcq_s[...], dcq_ref.shape)

    qspec = lambda off: pl.BlockSpec((blk, E), lambda h, t: (pair(t)[1], off + h))
    kspec = lambda off: pl.BlockSpec((blk, E), lambda h, t: (pair(t)[0], off + h))
    head = pl.BlockSpec((S, E), lambda h, t: (0, h))
    return pl.pallas_call(
        body, grid=(H, nq * (nq + 1) // 2),
        in_specs=[qspec(0), qspec(0), kspec(H), kspec(0), kspec(2 * H), qspec(0), qspec(0), qspec(0)],
        out_specs=[head, head, kspec(0), kspec(0), pl.BlockSpec((1, 1, blk), lambda h, t: (h, 0, pair(t)[0]))],
        out_shape=[jax.ShapeDtypeStruct((S, H * E), CDT), jax.ShapeDtypeStruct((S, H * E), F32),
                   jax.ShapeDtypeStruct((S, H * E), CDT), jax.ShapeDtypeStruct((S, H * E), CDT),
                   jax.ShapeDtypeStruct((H, 1, S), F32)],
        scratch_shapes=[pltpu.VMEM((S, E), F32), pltpu.VMEM((S, 1), F32), pltpu.VMEM((blk, E), F32),
                        pltpu.VMEM((blk, E), F32), pltpu.VMEM((1, blk), F32)],
        name=name, compiler_params=_cparams(("parallel", "arbitrary")),
    )(proj, qc, proj, kc, proj, lse, o, do)


DIL_SLAB = 16 * DIL_BLOCK
DIL_UNROLL = 16


def _rel_bucket(dist):
    max_exact = REL_BUCKETS // 2
    d = jnp.maximum(dist.astype(F32), 1.0)
    large = max_exact + (jnp.log(d / max_exact) / jnp.log(jnp.float32(REL_MAX_DISTANCE / max_exact))
                         * (REL_BUCKETS - max_exact)).astype(jnp.int32)
    large = jnp.minimum(large, REL_BUCKETS - 1)
    return jnp.where(dist < max_exact, dist, large)


def _bucket_table():
    i = jnp.arange(DIL_BLOCK)[:, None]
    j = jnp.arange(2 * DIL_BLOCK)[None, :]
    rel = DIL_BLOCK + i - j
    tabs = [_rel_bucket(jnp.clip(rel, 0, w // d) * d) for w, d in DIL_PATTERNS]
    return jnp.stack(tabs).astype(jnp.int32)


def _bias_table(rel_bias, buckets, name):
    P = buckets.shape[0]
    H = rel_bias.shape[1]

    def body(rb_ref, bk_ref, out_ref):
        h = pl.program_id(1)
        bk = bk_ref[0]
        val = jnp.zeros(bk.shape, F32)
        for b in range(REL_BUCKETS):
            val = jnp.where(bk == b, rb_ref[b, h], val)
        out_ref[0, 0] = val

    return pl.pallas_call(
        body, grid=(P, H),
        in_specs=[pl.BlockSpec(memory_space=pltpu.SMEM),
                  pl.BlockSpec((1, DIL_BLOCK, 2 * DIL_BLOCK), lambda p, h: (p, 0, 0))],
        out_specs=pl.BlockSpec((1, 1, DIL_BLOCK, 2 * DIL_BLOCK), lambda p, h: (p, h, 0, 0)),
        out_shape=jax.ShapeDtypeStruct((P, H, DIL_BLOCK, 2 * DIL_BLOCK), F32),
        name=name, compiler_params=_cparams(("parallel", "parallel")),
    )(rel_bias, buckets)


def _bias_table_bwd(dbias, buckets, name):
    P, H = dbias.shape[:2]

    def body(db_ref, bk_ref, out_ref):
        lane = lax.broadcasted_iota(jnp.int32, (1, REL_BUCKETS), 1)
        acc = jnp.zeros((1, REL_BUCKETS), F32)
        bk = bk_ref[...]
        db = db_ref[:, 0]
        for b in range(REL_BUCKETS):
            tot = jnp.sum(jnp.where(bk == b, db, 0.0))
            acc = jnp.where(lane == b, tot, acc)
        out_ref[0] = acc

    return pl.pallas_call(
        body, grid=(H,),
        in_specs=[pl.BlockSpec((P, 1, DIL_BLOCK, 2 * DIL_BLOCK), lambda h: (0, h, 0, 0)),
                  pl.BlockSpec((P, DIL_BLOCK, 2 * DIL_BLOCK), lambda h: (0, 0, 0))],
        out_specs=pl.BlockSpec((1, 1, REL_BUCKETS), lambda h: (h, 0, 0)),
        out_shape=jax.ShapeDtypeStruct((H, 1, REL_BUCKETS), F32),
        name=name, compiler_params=_cparams(("parallel",)),
    )(dbias, buckets)


def _bdot(a, b, contract_b):
    return lax.dot_general(a, b, (((2,), (contract_b,)), ((0,), (0,))), preferred_element_type=F32)


def _dil_units(first, d):
    units = []
    for t in range(DIL_UNROLL):
        u = first + t
        sg = u // d
        units.append((sg, sg * (DIL_BLOCK * d) + u % d))
    return units


def _dil_rows(ref, starts, d, dtype=None):
    t = jnp.stack([ref[pl.ds(s, DIL_BLOCK, stride=d), :] for s in starts])
    return t if dtype is None else t.astype(dtype)


def _dil_keys(ref, units, d):
    B, SL = DIL_BLOCK, DIL_SLAB
    return jnp.stack([jnp.concatenate([ref[pl.ds(SL + b - B * d, B, stride=d), :], ref[pl.ds(SL + b, B, stride=d), :]],
                                      axis=0) for _, b in units]).astype(CDT)


def _dil_logits(q, keys, bias_pc, first, d, has_before):
    T, B = q.shape[0], DIL_BLOCK
    ii = lax.broadcasted_iota(jnp.int32, (T, B, 2 * B), 1)
    jj = lax.broadcasted_iota(jnp.int32, (T, B, 2 * B), 2)
    sg = (first + lax.broadcasted_iota(jnp.int32, (T, B, 2 * B), 0)) // d
    mask = (jj >= ii) & (jj <= ii + B) & ((jj >= B) | (sg > 0) | has_before)
    return jnp.where(mask, _bdot(q, keys, 2) * HEAD_DIM ** -0.5 + bias_pc[None], NEG_INF)


def _dil_specs(H):
    E, SL = DIL_BLOCK, DIL_SLAB
    cur = lambda off: pl.BlockSpec((SL, E), lambda h, g: (g, off + h))
    prev = lambda off: pl.BlockSpec((SL, E), lambda h, g: (jnp.maximum(g - 1, 0), off + h))
    bias = pl.BlockSpec((len(DIL_PATTERNS), 1, E, 2 * E), lambda h, g: (0, h, 0, 0))
    return cur, prev, bias


def _dil_fwd(proj, bias, H, name):
    S = proj.shape[0]
    E = B = DIL_BLOCK
    SL = DIL_SLAB
    P = len(DIL_PATTERNS)
    assert S % SL == 0
    n_slabs = S // SL

    def body(q_ref, kc_ref, kp_ref, vc_ref, vp_ref, b_ref, y_ref, lse_ref, kj, vj, o_s, l_s):
        g = pl.program_id(1)
        kj[0:SL, :] = kp_ref[...]
        kj[SL:2 * SL, :] = kc_ref[...]
        vj[0:SL, :] = vp_ref[...]
        vj[SL:2 * SL, :] = vc_ref[...]
        for p, (_, d) in enumerate(DIL_PATTERNS):
            def batch(it, carry, p=p, d=d):
                first = it * DIL_UNROLL
                units = _dil_units(first, d)
                q = _dil_rows(q_ref, [b for _, b in units], d, CDT)
                s = _dil_logits(q, _dil_keys(kj, units, d), b_ref[p, 0], first, d, g > 0)
                m = jnp.max(s, axis=-1, keepdims=True)
                e = jnp.exp(s - m)
                ssum = jnp.sum(e, axis=-1, keepdims=True)
                o = _bdot(e.astype(CDT), _dil_keys(vj, units, d), 1) / ssum
                lse = jnp.broadcast_to(m + jnp.log(ssum), o.shape)
                for t, (_, b) in enumerate(units):
                    o_s[p, pl.ds(b, B, stride=d), :] = o[t]
                    l_s[p, pl.ds(b, B, stride=d), :] = lse[t]
                return carry

            lax.fori_loop(0, SL // B // DIL_UNROLL, batch, 0)
        ls = [l_s[p] for p in range(P)]
        m = functools.reduce(jnp.maximum, ls)
        w = [jnp.exp(l - m) for l in ls]
        tot = functools.reduce(jnp.add, w)
        y_ref[...] = functools.reduce(jnp.add, [(w[p] / tot) * o_s[p] for p in range(P)])
        lse_ref[...] = m + jnp.log(tot)

    cur, prev, bspec = _dil_specs(H)
    return pl.pallas_call(
        body, grid=(H, n_slabs),
        in_specs=[cur(0), cur(H), prev(H), cur(2 * H), prev(2 * H), bspec],
        out_specs=[cur(0), cur(0)],
        out_shape=[jax.ShapeDtypeStruct((S, H * E), F32)] * 2,
        scratch_shapes=[pltpu.VMEM((2 * SL, E), F32), pltpu.VMEM((2 * SL, E), F32),
                        pltpu.VMEM((P, SL, E), F32), pltpu.VMEM((P, SL, E), F32)],
        name=name, compiler_params=_cparams(("parallel", "parallel")),
    )(proj, proj, proj, proj, proj, bias)


def _dil_bwd(proj, bias, y, dy, lse, H, name):
    S = proj.shape[0]
    E = B = DIL_BLOCK
    SL = DIL_SLAB
    P = len(DIL_PATTERNS)
    assert S % SL == 0
    n_slabs = S // SL
    scale = E ** -0.5

    def body(q_ref, kc_ref, kp_ref, vc_ref, vp_ref, b_ref, y_ref, dy_ref, lse_ref,
             dq_ref, dk_ref, dv_ref, db_ref, kj, vj, dq_s, dk_own, dv_own, dk_held, dv_held, dk_back, dv_back, dl_s):
        g = pl.program_id(1)

        @pl.when(g == 0)
        def _():
            db_ref[...] = jnp.zeros(db_ref.shape, F32)

        @pl.when(g > 0)
        def _():
            dk_held[...] = dk_own[...]
            dv_held[...] = dv_own[...]
            dk_back[...] = jnp.zeros(dk_back.shape, F32)
            dv_back[...] = jnp.zeros(dv_back.shape, F32)

        @pl.when(g < n_slabs)
        def _():
            kj[0:SL, :] = kp_ref[...]
            kj[SL:2 * SL, :] = kc_ref[...]
            vj[0:SL, :] = vp_ref[...]
            vj[SL:2 * SL, :] = vc_ref[...]
            dq_s[...] = jnp.zeros(dq_s.shape, F32)
            dk_own[...] = jnp.zeros(dk_own.shape, F32)
            dv_own[...] = jnp.zeros(dv_own.shape, F32)
            dl_s[...] = jnp.broadcast_to(jnp.sum(dy_ref[...] * y_ref[...], axis=-1, keepdims=True), (SL, E))
            tr = lambda t: jnp.swapaxes(t, 1, 2).astype(CDT)
            for p, (_, d) in enumerate(DIL_PATTERNS):
                def batch(it, carry, p=p, d=d):
                    first = it * DIL_UNROLL
                    units = _dil_units(first, d)
                    starts = [b for _, b in units]
                    q = _dil_rows(q_ref, starts, d, CDT)
                    dyc = _dil_rows(dy_ref, starts, d, CDT)
                    keys, vals = _dil_keys(kj, units, d), _dil_keys(vj, units, d)
                    s = _dil_logits(q, keys, b_ref[p, 0], first, d, g > 0)
                    e = jnp.exp(s - _dil_rows(lse_ref, starts, d)[:, :, 0:1])
                    ds = e * (_bdot(dyc, vals, 2) - _dil_rows(dl_s, starts, d)[:, :, 0:1])
                    dss = ds * scale
                    dq = _bdot(dss.astype(CDT), keys, 1)
                    dk = _bdot(tr(dss), q, 1)
                    dv = _bdot(tr(e), dyc, 1)
                    for t, (sg, b) in enumerate(units):
                        rows = pl.ds(b, B, stride=d)
                        dq_s[rows, :] += dq[t]
                        dk_own[rows, :] += dk[t, B:]
                        dv_own[rows, :] += dv[t, B:]

                        if B * d < SL:
                            @pl.when(sg > 0)
                            def _(t=t, b=b):
                                before = pl.ds(b - B * d, B, stride=d)
                                dk_own[before, :] += dk[t, :B]
                                dv_own[before, :] += dv[t, :B]

                        @pl.when((sg == 0) & (g > 0))
                        def _(t=t, b=b):
                            before = pl.ds(SL + b - B * d, B, stride=d)
                            dk_back[before, :] += dk[t, :B]
                            dv_back[before, :] += dv[t, :B]

                    db_ref[p, 0] += jnp.sum(ds, axis=0)
                    return carry

                lax.fori_loop(0, SL // B // DIL_UNROLL, batch, 0)
            dq_ref[...] = dq_s[...].astype(dq_ref.dtype)

        @pl.when(g > 0)
        def _():
            dk_ref[...] = (dk_held[...] + dk_back[...]).astype(dk_ref.dtype)
            dv_ref[...] = (dv_held[...] + dv_back[...]).astype(dv_ref.dtype)

    last = n_slabs - 1
    cur = lambda off: pl.BlockSpec((SL, E), lambda h, g: (jnp.minimum(g, last), off + h))
    prev = lambda off: pl.BlockSpec((SL, E), lambda h, g: (jnp.maximum(jnp.minimum(g, last) - 1, 0), off + h))
    late = pl.BlockSpec((SL, E), lambda h, g: (jnp.maximum(g - 1, 0), h))
    bspec = pl.BlockSpec((P, 1, B, 2 * B), lambda h, g: (0, h, 0, 0))
    slab = pltpu.VMEM((SL, E), F32)
    return pl.pallas_call(
        body, grid=(H, n_slabs + 1),
        in_specs=[cur(0), cur(H), prev(H), cur(2 * H), prev(2 * H), bspec, cur(0), cur(0), cur(0)],
        out_specs=[cur(0), late, late, bspec],
        out_shape=[jax.ShapeDtypeStruct((S, H * E), CDT)] * 3 + [jax.ShapeDtypeStruct((P, H, B, 2 * B), F32)],
        scratch_shapes=[pltpu.VMEM((2 * SL, E), F32), pltpu.VMEM((2 * SL, E), F32)] + [slab] * 8,
        name=name, compiler_params=_cparams(("parallel", "arbitrary")),
    )(proj, proj, proj, proj, proj, bias, y, dy, lse)


def _adamw_tile(g_, w_ref, m_ref, v_ref, g_out, d_ref, nm_ref, nv_ref):
    g_out[...] = g_
    m_ = ADAM_B1 * m_ref[...] + (1.0 - ADAM_B1) * g_
    v_ = ADAM_B2 * v_ref[...] + (1.0 - ADAM_B2) * jnp.square(g_)
    m_hat = m_ / (1.0 - ADAM_B1 ** ADAM_STEP)
    v_hat = v_ / (1.0 - ADAM_B2 ** ADAM_STEP)
    d_ref[...] = -ADAM_LR * (m_hat / (jnp.sqrt(v_hat) + ADAM_EPS) + ADAM_WD * w_ref[...])
    nm_ref[...] = m_
    nv_ref[...] = v_


def _adamw(w, g, m, v, name, br=256):
    R, C = w.shape
    br = br if R % br == 0 else R

    def body(w_ref, g_ref, m_ref, v_ref, *outs):
        _adamw_tile(g_ref[...], w_ref, m_ref, v_ref, *outs)

    blk = pl.BlockSpec((br, C), lambda i: (i, 0))
    return pl.pallas_call(
        body, grid=(R // br,), in_specs=[blk] * 4, out_specs=[blk] * 4,
        out_shape=[jax.ShapeDtypeStruct((R, C), F32)] * 4,
        name=name, compiler_params=_cparams(("parallel",)),
    )(w, g, m, v)


_HBM = pl.BlockSpec(memory_space=pltpu.HBM)
_SEM = pl.BlockSpec(memory_space=pltpu.SEMAPHORE)
_ANY = pl.BlockSpec(memory_space=pl.ANY)
_VMEM = pl.BlockSpec(memory_space=pltpu.VMEM)
_TOKEN = jax.ShapeDtypeStruct((8, 128), F32)


def _split_params():
    return pltpu.CompilerParams(has_side_effects=pltpu.SideEffectType.DATAFLOW_SIDE_EFFECTING)


def _place():
    x, y, c = lax.axis_index("x"), lax.axis_index("y"), lax.axis_index("c")
    chips = [(1 - x, y), (x, 1 - y), (1 - x, 1 - y)]
    return x, y, c, chips


def _tie(v, tokens, name):
    flat = v.reshape(1, -1)

    def body(v_ref, *rest):
        rest[-1][...] = v_ref[...]

    return pl.pallas_call(body, in_specs=[_VMEM] + [_ANY] * len(tokens), out_specs=_VMEM,
                          out_shape=jax.ShapeDtypeStruct(flat.shape, flat.dtype), name=name,
                          compiler_params=_cparams())(flat, *tokens).reshape(v.shape)


def _row_block(R, pref=512):
    return _tile(R, pref) if R % 128 == 0 else R


def _slot():
    return 2 * lax.axis_index("x") + lax.axis_index("y")


def _cast_into_slot(w, layer, name):
    _, R, C = w.shape
    br = _row_block(R)

    def body(w_ref, out_ref):
        out_ref[...] = w_ref[...].astype(out_ref.dtype)

    return pl.pallas_call(
        body, grid=(R // br,),
        in_specs=[pl.BlockSpec((None, br, C), lambda i: (layer, i, 0))],
        out_specs=pl.BlockSpec((None, br, C), lambda i: (_slot(), i, 0)),
        out_shape=jax.ShapeDtypeStruct((N_CHIPS, R, C), CDT),
        name=name, compiler_params=_cparams(("parallel",)),
    )(w)


def _gather_copies(src_ref, dst_ref, send_sems, recv_sems, incoming):
    Rh = src_ref.shape[1] // 2
    x, y, c, chips = _place()
    slot = 2 * x + y

    def half(ref, s, hf):
        return ref.at[s, pl.ds(hf * Rh, Rh), :]

    copies = []
    for j, (cx, cy) in enumerate(chips):
        for e in range(2):
            copies.append(pltpu.make_async_remote_copy(
                src_ref=half(src_ref, slot, c), dst_ref=half(dst_ref, 2 * cx + cy, e) if incoming else half(dst_ref, slot, c),
                send_sem=send_sems.at[2 * j + e], recv_sem=recv_sems.at[2 * j + (e if incoming else c)],
                device_id=(cx, cy, e), device_id_type=MESH))
    return copies


def _gather_start(buf, after, name):
    n_after = len(after)

    def body(*refs):
        buf_ref = refs[0]
        send_sems, recv_sems, out_ref, token = refs[1 + n_after:]
        for cp in _gather_copies(buf_ref, out_ref, send_sems, recv_sems, incoming=False):
            cp.start()
        token[...] = jnp.zeros(token.shape, token.dtype)

    return pl.pallas_call(
        body, in_specs=[_HBM] + [_ANY] * n_after, out_specs=(_SEM, _SEM, _HBM, _VMEM),
        out_shape=(pltpu.SemaphoreType.DMA((6,)), pltpu.SemaphoreType.DMA((6,)), pltpu.HBM(buf.shape, buf.dtype), _TOKEN),
        input_output_aliases={0: 2}, name=name, compiler_params=_split_params(),
    )(pltpu.with_memory_space_constraint(buf, pltpu.HBM), *after)


def _gather_wait(send_sems, recv_sems, buf, after, name):
    def body(buf_ref, send_sems, recv_sems, after_ref, out_ref):
        for cp in _gather_copies(buf_ref, out_ref, send_sems, recv_sems, incoming=False):
            cp.wait_send()
        for cp in _gather_copies(buf_ref, out_ref, send_sems, recv_sems, incoming=True):
            cp.wait_recv()

    return pl.pallas_call(
        body, in_specs=[_HBM, _SEM, _SEM, _ANY], out_specs=_HBM, out_shape=pltpu.HBM(buf.shape, buf.dtype),
        input_output_aliases={0: 0}, name=name, compiler_params=_split_params(),
    )(buf, send_sems, recv_sems, after)


def _relay_copies(src_ref, dst_ref, send_sems, recv_sems, stage, incoming):
    Rh = src_ref.shape[1] // 2
    x, y, c, chips = _place()
    copies = []
    for j, (cx, cy) in enumerate(chips):
        if stage == 0:
            src_slot, src_half, peer = 2 * x + y, c, (cx, cy, c)
            dst_slot, dst_half = (2 * cx + cy, c) if incoming else (src_slot, c)
        else:
            src_slot, src_half, peer = 2 * cx + cy, c, (x, y, 1 - c)
            dst_slot, dst_half = src_slot, (1 - c if incoming else c)
        copies.append(pltpu.make_async_remote_copy(
            src_ref=src_ref.at[src_slot, pl.ds(src_half * Rh, Rh), :],
            dst_ref=dst_ref.at[dst_slot, pl.ds(dst_half * Rh, Rh), :],
            send_sem=send_sems.at[j], recv_sem=recv_sems.at[j], device_id=peer, device_id_type=MESH))
    return copies


def _relay_start(buf, after, name):
    n_after = len(after)

    def body(*refs):
        buf_ref = refs[0]
        send_sems, recv_sems, out_ref, token = refs[1 + n_after:]
        for cp in _relay_copies(buf_ref, out_ref, send_sems, recv_sems, 0, incoming=False):
            cp.start()
        token[...] = jnp.zeros(token.shape, token.dtype)

    return pl.pallas_call(
        body, in_specs=[_HBM] + [_ANY] * n_after, out_specs=(_SEM, _SEM, _HBM, _VMEM),
        out_shape=(pltpu.SemaphoreType.DMA((3,)), pltpu.SemaphoreType.DMA((3,)), pltpu.HBM(buf.shape, buf.dtype), _TOKEN),
        input_output_aliases={0: 2}, name=name, compiler_params=_split_params(),
    )(pltpu.with_memory_space_constraint(buf, pltpu.HBM), *after)


def _relay_pass(send_sems, recv_sems, buf, after, name):
    def body(buf_ref, send0, recv0, after_ref, send1, recv1, out_ref):
        for cp in _relay_copies(buf_ref, out_ref, send0, recv0, 0, incoming=False):
            cp.wait_send()
        for cp in _relay_copies(buf_ref, out_ref, send0, recv0, 0, incoming=True):
            cp.wait_recv()
        for cp in _relay_copies(out_ref, out_ref, send1, recv1, 1, incoming=False):
            cp.start()

    return pl.pallas_call(
        body, in_specs=[_HBM, _SEM, _SEM, _ANY], out_specs=(_SEM, _SEM, _HBM),
        out_shape=(pltpu.SemaphoreType.DMA((3,)), pltpu.SemaphoreType.DMA((3,)), pltpu.HBM(buf.shape, buf.dtype)),
        input_output_aliases={0: 2}, name=name, compiler_params=_split_params(),
    )(buf, send_sems, recv_sems, after)


def _relay_wait(send_sems, recv_sems, buf, after, name):
    def body(buf_ref, send1, recv1, after_ref, out_ref):
        for cp in _relay_copies(buf_ref, out_ref, send1, recv1, 1, incoming=False):
            cp.wait_send()
        for cp in _relay_copies(buf_ref, out_ref, send1, recv1, 1, incoming=True):
            cp.wait_recv()

    return pl.pallas_call(
        body, in_specs=[_HBM, _SEM, _SEM, _ANY], out_specs=_HBM, out_shape=pltpu.HBM(buf.shape, buf.dtype),
        input_output_aliases={0: 0}, name=name, compiler_params=_split_params(),
    )(buf, send_sems, recv_sems, after)


def _scatter_copies(g_ref, land_ref, send_sems, recv_sems, incoming):
    Rh = g_ref.shape[1] // 2
    x, y, c, _ = _place()
    me = 4 * x + 2 * y + c
    copies = []
    for k in range(1, N_DEV):
        px, py, pc = (x + (k >> 2)) % 2, (y + ((k >> 1) & 1)) % 2, (c + (k & 1)) % 2
        copies.append(pltpu.make_async_remote_copy(
            src_ref=g_ref.at[2 * px + py, pl.ds(pc * Rh, Rh), :],
            dst_ref=land_ref.at[4 * px + 2 * py + pc if incoming else me],
            send_sem=send_sems.at[k - 1], recv_sem=recv_sems.at[k - 1], device_id=(px, py, pc), device_id_type=MESH))
    return copies


def _scatter_start(g, name):
    ns, R, C = g.shape

    def body(g_ref, land_ref, send_sems, recv_sems, g_thru, land_thru, token):
        for cp in _scatter_copies(g_ref, land_thru, send_sems, recv_sems, incoming=False):
            cp.start()
        token[...] = jnp.zeros(token.shape, token.dtype)

    land = lax.empty((N_DEV, R // 2, C), g.dtype)
    n = N_DEV - 1
    return pl.pallas_call(
        body, in_specs=[_HBM, _HBM], out_specs=(_SEM, _SEM, _HBM, _HBM, _VMEM),
        out_shape=(pltpu.SemaphoreType.DMA((n,)), pltpu.SemaphoreType.DMA((n,)), pltpu.HBM(g.shape, g.dtype),
                   pltpu.HBM(land.shape, land.dtype), _TOKEN),
        input_output_aliases={0: 2, 1: 3}, name=name, compiler_params=_split_params(),
    )(pltpu.with_memory_space_constraint(g, pltpu.HBM), pltpu.with_memory_space_constraint(land, pltpu.HBM))


def _scatter_wait(send_sems, recv_sems, g, land, after, name):
    def body(g_ref, land_ref, send_sems, recv_sems, after_ref, g_out, land_out):
        for cp in _scatter_copies(g_ref, land_out, send_sems, recv_sems, incoming=False):
            cp.wait_send()
        for cp in _scatter_copies(g_ref, land_out, send_sems, recv_sems, incoming=True):
            cp.wait_recv()

    return pl.pallas_call(
        body, in_specs=[_HBM, _HBM, _SEM, _SEM, _ANY], out_specs=(_HBM, _HBM),
        out_shape=(pltpu.HBM(g.shape, g.dtype), pltpu.HBM(land.shape, land.dtype)),
        input_output_aliases={0: 0, 1: 1}, name=name, compiler_params=_split_params(),
    )(g, land, send_sems, recv_sems, after)


def _device_sum(land, g, layer, n_layers, prev, name):
    nd, Rh, C = land.shape
    br = _row_block(Rh)
    nb = Rh // br
    core = lambda: lax.axis_index("c")
    me = lambda: 2 * _slot() + core()

    def body(*refs):
        own = refs[nd][...]
        acc = None
        for d in range(nd):
            t = jnp.where(me() == d, own, refs[d][...]).astype(F32)
            acc = t if acc is None else acc + t
        refs[-1][...] = acc

    def piece(d):
        return pl.BlockSpec((None, br, C), lambda i: (jnp.where(me() == d, (d + 1) % nd, d), i, 0))

    ins = [land] * nd + [g] + ([prev] if prev is not None else [])
    return pl.pallas_call(
        body, grid=(nb,),
        in_specs=[piece(d) for d in range(nd)]
        + [pl.BlockSpec((None, br, C), lambda i: (_slot(), core() * nb + i, 0))]
        + ([_ANY] if prev is not None else []),
        out_specs=pl.BlockSpec((None, br, C), lambda i: (layer, core() * nb + i, 0)),
        out_shape=jax.ShapeDtypeStruct((n_layers, 2 * Rh, C), F32),
        input_output_aliases={nd + 1: 0} if prev is not None else {},
        name=name, compiler_params=_cparams(("parallel",)),
    )(*ins)


def _join_copy(src_ref, dst_ref, layer, send_sem, recv_sem, incoming):
    Rh = src_ref.shape[1] // 2
    x, y, c, _ = _place()
    mine, other = pl.ds(c * Rh, Rh), pl.ds((1 - c) * Rh, Rh)
    return pltpu.make_async_remote_copy(src_ref=src_ref.at[layer, mine, :],
                                        dst_ref=dst_ref.at[layer, other if incoming else mine, :],
                                        send_sem=send_sem, recv_sem=recv_sem, device_id=(x, y, 1 - c),
                                        device_id_type=MESH)


def _join_start(g, layer, name):
    def body(g_ref, send_sem, recv_sem, out_ref, token):
        _join_copy(g_ref, out_ref, layer, send_sem, recv_sem, incoming=False).start()
        token[...] = jnp.zeros(token.shape, token.dtype)

    return pl.pallas_call(
        body, in_specs=[_HBM], out_specs=(_SEM, _SEM, _HBM, _VMEM),
        out_shape=(pltpu.SemaphoreType.DMA(()), pltpu.SemaphoreType.DMA(()), pltpu.HBM(g.shape, g.dtype), _TOKEN),
        input_output_aliases={0: 2}, name=name, compiler_params=_split_params(),
    )(pltpu.with_memory_space_constraint(g, pltpu.HBM))


def _join_wait(send_sem, recv_sem, g, layer, after, name):
    def body(g_ref, send_sem, recv_sem, after_ref, out_ref):
        _join_copy(g_ref, out_ref, layer, send_sem, recv_sem, incoming=False).wait_send()
        _join_copy(g_ref, out_ref, layer, send_sem, recv_sem, incoming=True).wait_recv()

    return pl.pallas_call(
        body, in_specs=[_HBM, _SEM, _SEM, _ANY], out_specs=_HBM, out_shape=pltpu.HBM(g.shape, g.dtype),
        input_output_aliases={0: 0}, name=name, compiler_params=_split_params(),
    )(g, send_sem, recv_sem, after)


def _all_reduce_small(v, name):
    rows, cols = v.shape

    def body(v_ref, out_ref, buf, send_sems, recv_sems):
        x, y, c, _ = _place()
        me = 4 * x + 2 * y + c
        buf[me] = v_ref[...]
        peers = []
        for k in range(1, N_DEV):
            px, py, pc = (x + (k >> 2)) % 2, (y + ((k >> 1) & 1)) % 2, (c + (k & 1)) % 2
            peers.append((px, py, pc))
        sends = []
        for k, peer in enumerate(peers):
            cp = pltpu.make_async_remote_copy(src_ref=v_ref, dst_ref=buf.at[me], send_sem=send_sems.at[k],
                                              recv_sem=recv_sems.at[k], device_id=peer, device_id_type=MESH)
            cp.start()
            sends.append(cp)
        for k, (px, py, pc) in enumerate(peers):
            pltpu.make_async_remote_copy(src_ref=v_ref, dst_ref=buf.at[4 * px + 2 * py + pc], send_sem=send_sems.at[k],
                                         recv_sem=recv_sems.at[k], device_id=(px, py, pc),
                                         device_id_type=MESH).wait_recv()
        for cp in sends:
            cp.wait_send()
        acc = buf[0]
        for i in range(1, N_DEV):
            acc = acc + buf[i]
        out_ref[...] = acc

    vmem = pl.BlockSpec(memory_space=pltpu.VMEM)
    return pl.pallas_call(
        body, in_specs=[vmem], out_specs=vmem, out_shape=jax.ShapeDtypeStruct((rows, cols), F32),
        scratch_shapes=[pltpu.VMEM((N_DEV, rows, cols), F32), pltpu.SemaphoreType.DMA((N_DEV - 1,)),
                        pltpu.SemaphoreType.DMA((N_DEV - 1,))],
        name=name, compiler_params=pltpu.CompilerParams(),
    )(v)


def _reduce_scatter_sum(started, after, layer, n_layers, prev, tag):
    send_sems, recv_sems, g, land, _ = started
    g, land = _scatter_wait(send_sems, recv_sems, g, land, after, f"rs_wait_{tag}")
    f = _device_sum(land, g, layer, n_layers, prev, f"rs_sum_{tag}")
    return _join_start(f, layer, f"rs_join_start_{tag}")


def _split_w_in(wg, Hf, name):
    ns, D, cols = wg.shape
    a = 3 * Hf * HEAD_DIM
    n6 = ns * cols - Hf
    br = _row_block(D)

    def body(w_ref, w6_ref, wf_ref):
        nat = jnp.concatenate([w_ref[s] for s in range(ns)], axis=1)
        w6_ref[...] = jnp.concatenate([nat[:, :a], nat[:, a + Hf:]], axis=1)
        wf_ref[...] = nat[:, a:a + Hf]

    w6, wf = pl.pallas_call(
        body, grid=(D // br,), in_specs=[pl.BlockSpec((ns, br, cols), lambda i: (0, i, 0))],
        out_specs=[pl.BlockSpec((br, n6), lambda i: (i, 0)), pl.BlockSpec((br, Hf), lambda i: (i, 0))],
        out_shape=[jax.ShapeDtypeStruct((D, n6), wg.dtype), jax.ShapeDtypeStruct((D, Hf), wg.dtype)],
        name=name, compiler_params=_cparams(("parallel",)),
    )(wg)
    return w6, wf.T


def _join_dw_in(dw6, dwf_t, Hf, name):
    D, n6 = dw6.shape
    a = 3 * Hf * HEAD_DIM
    cols = (n6 + Hf) // N_CHIPS
    br = _row_block(D)

    def body(w6_ref, wf_ref, out_ref):
        w6 = w6_ref[...]
        nat = jnp.concatenate([w6[:, :a], wf_ref[...], w6[:, a:]], axis=1)
        for s in range(N_CHIPS):
            out_ref[s] = nat[:, s * cols:(s + 1) * cols]

    return pl.pallas_call(
        body, grid=(D // br,),
        in_specs=[pl.BlockSpec((br, n6), lambda i: (i, 0)), pl.BlockSpec((br, Hf), lambda i: (i, 0))],
        out_specs=pl.BlockSpec((N_CHIPS, br, cols), lambda i: (0, i, 0)),
        out_shape=jax.ShapeDtypeStruct((N_CHIPS, D, cols), dw6.dtype),
        name=name, compiler_params=_cparams(("parallel",)),
    )(dw6, dwf_t.T.astype(dw6.dtype))


def _tied(v, tokens, name):
    return _tie(v, tokens, name) if tokens else v


def _layer_fwd(x, p, weight, bias, tokens, tag):
    Hf, Hd = p["forget_b"].shape[0], bias.shape[1]
    h1 = _rms_fwd(x, _tied(p["norm1_g"], tokens, f"tie_norm1_{tag}"), f"norm1_{tag}")
    w6, wf_t = _split_w_in(weight("w_in", h1), Hf, f"split_w_in_{tag}")
    n_a = 3 * Hf * HEAD_DIM
    proj_a = _mm_nn(h1, w6, f"proj_a_{tag}", [CDT], epi=lambda acc: (acc,), b_cols=(0, n_a))[0]
    proj_b = _mm_nn(h1, w6, f"proj_b_{tag}", [F32], b_cols=(n_a, w6.shape[1] - n_a))[0]
    f_t = _mm_nt(wf_t, h1, f"fproj_{tag}", [F32])[0]
    qc, kc = _fox_bias_operands(_gates_fwd(f_t, p["forget_b"], f"gates_{tag}"), f"fox_operands_{tag}")
    y_a, lse_a = _fox_fwd(proj_a, qc, kc, Hf, f"fox_{tag}")
    y_b, lse_b = _dil_fwd(proj_b, bias, Hd, f"dil_{tag}")
    mixed = _pair_norm_fwd(y_a, y_b, p["outnorm_a_g"], p["outnorm_b_g"], f"norm_ab_{tag}")
    w_out = weight("w_out", mixed)
    w_out = w_out.reshape(-1, w_out.shape[2])
    x1 = _mm_nn(mixed, w_out, f"attn_out_{tag}", [F32], extras=[x])[0]
    h2 = _rms_fwd(x1, p["norm2_g"], f"norm2_{tag}")
    w_mi = weight("w_mlp_in", h2)
    u, act = _mm_nn(h2, w_mi, f"mlp_in_{tag}", [CDT, CDT], b_slots=True,
                    epi=lambda acc: (acc, jnp.square(jnp.maximum(acc, 0.0))))
    w_mo = weight("w_mlp_out", act)
    w_mo = w_mo.reshape(-1, w_mo.shape[2])
    x2 = _mm_nn(act, w_mo, f"mlp_out_{tag}", [F32], extras=[x1])[0]
    saved = dict(x=x, h1=h1, proj_a=proj_a, proj_b=proj_b, f_t=f_t, qc=qc, kc=kc, y_a=y_a, lse_a=lse_a, y_b=y_b,
                 lse_b=lse_b, mixed=mixed, x1=x1, h2=h2, u=u, act=act, w6=w6, wf_t=wf_t, w_out=w_out, w_mi=w_mi,
                 w_mo=w_mo)
    return x2, saved


def _layer_bwd(dx2, dx2c, p, send, bias, sv, defer_w_out, tag):
    Hf, Hd = p["forget_b"].shape[0], bias.shape[1]
    E = HEAD_DIM
    rows = lambda g: g.reshape(N_CHIPS, -1, g.shape[1])
    du = _mm_nt(dx2c, sv["w_mo"], f"d_act_{tag}", [CDT], extras=[sv["u"]],
                epi=lambda acc, u: (acc * (2.0 * jnp.maximum(u.astype(F32), 0.0)),))[0]
    tokens = send("w_mlp_out", rows(_mm_tn(sv["act"], dx2c, f"dw_mlp_out_{tag}", CDT)))
    dh2 = _mm_nt(du, sv["w_mi"], f"d_h2_{tag}", [F32], b_slots=True)[0]
    tokens = tokens + send("w_mlp_in", _mm_tn(sv["h2"], du, f"dw_mlp_in_{tag}", CDT, out_slots=N_CHIPS))
    dx1, dx1c, g_norm2 = _rms_bwd(sv["x1"], _tied(p["norm2_g"], tokens, f"tie_norm2_{tag}"), dh2, dx2,
                                  f"d_norm2_{tag}")
    dmixed = _mm_nt(dx1c, sv["w_out"], f"d_mixed_{tag}", [F32])[0]
    send_w_out = lambda: send("w_out", rows(_mm_tn(sv["mixed"], dx1c, f"dw_out_{tag}", CDT)))
    tokens = [] if defer_w_out else send_w_out()
    dy_a, dy_b, g_na, g_nb = _pair_norm_bwd(sv["y_a"], sv["y_b"], _tied(p["outnorm_a_g"], tokens, f"tie_norm_a_{tag}"),
                                            p["outnorm_b_g"], dmixed, f"d_norm_ab_{tag}")
    dq_a, dcq, dk_a, dv_a, dck = _fox_bwd(sv["proj_a"], sv["qc"], sv["kc"], sv["lse_a"], sv["y_a"], dy_a, Hf,
                                          f"fox_bwd_{tag}")
    df, dfc, g_fb = _gates_bwd(sv["f_t"], p["forget_b"], dcq[:, ::E].T, dck.reshape(Hf, -1), f"d_gates_{tag}")
    dq_b, dk_b, dv_b, dbias = _dil_bwd(sv["proj_b"], bias, sv["y_b"], dy_b, sv["lse_b"], Hd, f"dil_bwd_{tag}")
    dproj = jnp.concatenate([dq_a, dk_a, dv_a, dq_b, dk_b, dv_b], axis=1)
    g_w6 = _mm_tn(sv["h1"], dproj, f"dw_in_{tag}", CDT)
    g_wf_t = _mm_nn(dfc, sv["h1"], f"dw_f_{tag}", [F32])[0]
    tokens = send("w_in", _join_dw_in(g_w6, g_wf_t, Hf, f"join_dw_in_{tag}"))
    dh1_f = _mm_tn(dfc, _tied(sv["wf_t"], tokens, f"tie_wf_{tag}"), f"d_h1_f_{tag}", F32)
    dh1 = _mm_nt(dproj, sv["w6"], f"d_h1_{tag}", [F32], extras=[dh1_f])[0]
    dx, dxc, g_norm1 = _rms_bwd(sv["x"], p["norm1_g"], dh1, dx1, f"d_norm1_{tag}")
    grads = dict(norm1_g=g_norm1[0], norm2_g=g_norm2[0], outnorm_a_g=g_na[0], outnorm_b_g=g_nb[0],
                 forget_b=g_fb[:, 0], dbias=dbias)
    return dx, dxc, grads, (send_w_out if defer_w_out else None)


_LAYER_SMALL = ("norm1_g", "forget_b", "outnorm_a_g", "outnorm_b_g", "norm2_g")


def _local_step(x, target, small, weight, send, tokens):
    depth = small["norm1_g"].shape[0]
    buckets = _bucket_table()
    bias = _bias_table(small["rel_bias"], buckets, "bias_table")
    layers, saved = [], []
    for l in range(depth):
        p = {k: small[k][l] for k in _LAYER_SMALL}
        layers.append(p)
        x, sv = _layer_fwd(x, p, functools.partial(weight, l), bias, tokens if l == 0 else [], f"l{l}")
        saved.append(sv)
    dx, dxc, g_final, loss = _loss_bwd(x, small["final_norm_g"], target, "loss")
    layer_grads = [None] * depth
    for l in reversed(range(depth)):
        dx, dxc, layer_grads[l], last = _layer_bwd(dx, dxc, layers[l], functools.partial(send, l), bias, saved[l],
                                                   l == 0, f"l{l}")
    tokens = last()
    dbias = functools.reduce(jnp.add, [g["dbias"] for g in layer_grads])
    g_rel = _bias_table_bwd(dbias, buckets, "d_bias_table")[:, 0, :].T
    small_grads = dict(final_norm_g=g_final[0], rel_bias=g_rel,
                       **{k: jnp.stack([g[k] for g in layer_grads]) for k in _LAYER_SMALL})
    return loss[0, 0], dx, small_grads, tokens


_BIG = ("w_in", "w_out", "w_mlp_in", "w_mlp_out")
_SMALL = ("norm1_g", "forget_b", "rel_bias", "outnorm_a_g", "outnorm_b_g", "norm2_g", "final_norm_g")
_ORDER = ("norm1_g", "w_in", "forget_b", "rel_bias", "outnorm_a_g", "outnorm_b_g", "w_out", "norm2_g", "w_mlp_in",
          "w_mlp_out", "final_norm_g")


def _pack_small(d):
    flat = jnp.concatenate([d[k].reshape(-1) for k in _SMALL])
    rows = -(-flat.shape[0] // (8 * SMALL_COLS)) * 8
    return jnp.pad(flat, (0, rows * SMALL_COLS - flat.shape[0])).reshape(rows, SMALL_COLS)


def _unpack_small(packed, like):
    flat, out, at = packed.reshape(-1), {}, 0
    for k in _SMALL:
        n = like[k].size
        out[k] = flat[at:at + n].reshape(like[k].shape)
        at += n
    return out


def kernel(x, norm1_g, w_in, forget_b, rel_bias, outnorm_a_g, outnorm_b_g, w_out, norm2_g, w_mlp_in, w_mlp_out, final_norm_g, loss_target, m_norm1_g, m_w_in, m_forget_b, m_rel_bias, m_outnorm_a_g, m_outnorm_b_g, m_w_out, m_norm2_g, m_w_mlp_in, m_w_mlp_out, m_final_norm_g, v_norm1_g, v_w_in, v_forget_b, v_rel_bias, v_outnorm_a_g, v_outnorm_b_g, v_w_out, v_norm2_g, v_w_mlp_in, v_w_mlp_out, v_final_norm_g):
    w = dict(norm1_g=norm1_g, w_in=w_in, forget_b=forget_b, rel_bias=rel_bias, outnorm_a_g=outnorm_a_g,
             outnorm_b_g=outnorm_b_g, w_out=w_out, norm2_g=norm2_g, w_mlp_in=w_mlp_in, w_mlp_out=w_mlp_out,
             final_norm_g=final_norm_g)
    m = dict(norm1_g=m_norm1_g, w_in=m_w_in, forget_b=m_forget_b, rel_bias=m_rel_bias, outnorm_a_g=m_outnorm_a_g,
             outnorm_b_g=m_outnorm_b_g, w_out=m_w_out, norm2_g=m_norm2_g, w_mlp_in=m_w_mlp_in,
             w_mlp_out=m_w_mlp_out, final_norm_g=m_final_norm_g)
    v = dict(norm1_g=v_norm1_g, w_in=v_w_in, forget_b=v_forget_b, rel_bias=v_rel_bias, outnorm_a_g=v_outnorm_a_g,
             outnorm_b_g=v_outnorm_b_g, w_out=v_w_out, norm2_g=v_norm2_g, w_mlp_in=v_w_mlp_in,
             w_mlp_out=v_w_mlp_out, final_norm_g=v_final_norm_g)
    depth = w_in.shape[0]
    small = {k: w[k] for k in _SMALL}

    gathers, passed, tokens = {}, {}, []
    for l in range(depth):
        for k in _BIG:
            buf = _cast_into_slot(w[k], l, f"cast_{k}_l{l}")
            start = _relay_start if k == _BIG[0] else _gather_start
            send_sems, recv_sems, buf, token = start(buf, tokens, f"gather_start_{k}_l{l}")
            gathers[l, k], tokens = (send_sems, recv_sems, buf), [token]

    def weight(l, k, after):
        if k == _BIG[-1] and l + 1 < depth:
            passed[l + 1] = _relay_pass(*gathers[l + 1, _BIG[0]], after, f"gather_pass_{_BIG[0]}_l{l + 1}")
        if k != _BIG[0]:
            return _gather_wait(*gathers[l, k], after, f"gather_wait_{k}_l{l}")
        if l not in passed:
            passed[l] = _relay_pass(*gathers[l, k], after, f"gather_pass_{k}_l{l}")
        return _relay_wait(*passed[l], after, f"gather_wait_{k}_l{l}")

    scatters = {}

    def send(l, k, g):
        scatters[l, k] = _scatter_start(g, f"rs_start_{k}_l{l}")
        return [scatters[l, k][4]]

    loss, grad_x, small_grads, tokens = _local_step(x[0], loss_target[0], small, weight, send, tokens)
    loss = lax.psum(loss, ("x", "y", "c"))

    grads, delta, new_m, new_v = {}, {}, {}, {}
    packed = _tied(_pack_small(small_grads), tokens, "tie_small")
    after, seen, joining = packed, {k: 0 for k in _BIG}, None

    def joined(after):
        (l, k), (send_sem, recv_sem, g) = joining
        grads[k] = _join_wait(send_sem, recv_sem, g, l, after, f"rs_join_wait_{k}_l{l}")
        seen[k] += 1
        if seen[k] < depth:
            return after
        shape = w[k].shape
        flat = lambda t: t.reshape(-1, shape[-1])
        outs = _adamw(flat(w[k]), flat(grads[k]), flat(m[k]), flat(v[k]), f"adamw_{k}")
        grads[k], delta[k], new_m[k], new_v[k] = (t.reshape(shape) for t in outs)
        return outs[1]

    for (l, k), started in scatters.items():
        assert joining is None or joining[0][1] != k
        send_sem, recv_sem, g, token = _reduce_scatter_sum(started, after, l, depth, grads.get(k), f"{k}_l{l}")
        if joining is not None:
            after = joined(token)
        joining = ((l, k), (send_sem, recv_sem, g))
    after = joined(after)
    small_sums = _all_reduce_small(_tied(packed, [after], "tie_small_sums"), "small_all_reduce")
    grads.update(_unpack_small(small_sums, small))
    _, d_, m_, v_ = _adamw(_pack_small(small), _pack_small({k: grads[k] for k in _SMALL}),
                           _pack_small({k: m[k] for k in _SMALL}), _pack_small({k: v[k] for k in _SMALL}), "adamw_small")
    delta.update(_unpack_small(d_, small))
    new_m.update(_unpack_small(m_, small))
    new_v.update(_unpack_small(v_, small))

    return (loss, grad_x[None], *[grads[k] for k in _ORDER], *[delta[k] for k in _ORDER],
            *[new_m[k] for k in _ORDER], *[new_v[k] for k in _ORDER])
```

```python
import functools

import jax
import jax.numpy as jnp
from jax import lax
from jax.experimental import pallas as pl
from jax.experimental.pallas import tpu as pltpu

F32 = jnp.float32
CDT = jnp.bfloat16
HEAD_DIM = 128
NORM_EPS = 1e-6
NEG_INF = -1e30
LOG2E = 1.4426950408889634
REL_BUCKETS = 32
REL_MAX_DISTANCE = 2048
DIL_PATTERNS = ((128, 1), (512, 4), (2048, 16))
DIL_BLOCK = 128
ADAM_LR, ADAM_B1, ADAM_B2, ADAM_EPS, ADAM_WD, ADAM_STEP = 0.001, 0.9, 0.999, 1e-08, 0.01, 10
N_CHIPS = 4
N_DEV = 8
VMEM_LIMIT_BYTES = 56 * 1024 * 1024
SMALL_COLS = 1024
MESH = pl.DeviceIdType.MESH


def _cparams(sem=None):
    return pltpu.CompilerParams(dimension_semantics=sem, vmem_limit_bytes=VMEM_LIMIT_BYTES)


def _tile(dim, pref):
    t = min(pref, dim)
    t -= t % 128
    while t >= 128:
        if dim % t == 0:
            return t
        t -= 128
    return dim


def _rowwise(fn, ins, out_dtypes, name, bs=512, consts=()):
    R, C = ins[0].shape
    bs = min(bs, R)
    n_in, n_c = len(ins), len(consts)

    def body(*refs):
        vals = [r[...] for r in refs[:n_in + n_c]]
        res = fn(*vals)
        for o, r in zip(refs[n_in + n_c:], res):
            o[...] = r.astype(o.dtype)

    row = pl.BlockSpec((bs, C), lambda i: (i, 0))
    return pl.pallas_call(
        body, grid=(R // bs,),
        in_specs=[row] * n_in + [pl.BlockSpec((1, c.shape[-1]), lambda i: (0, 0)) for c in consts],
        out_specs=[row] * len(out_dtypes),
        out_shape=[jax.ShapeDtypeStruct((R, C), d) for d in out_dtypes],
        name=name, compiler_params=_cparams(("parallel",)),
    )(*ins, *[c.reshape(1, -1) for c in consts])


def _rms_fwd(x, g, name):
    def fn(xf, gg):
        r = lax.rsqrt(jnp.mean(xf * xf, axis=-1, keepdims=True) + NORM_EPS)
        return ((xf * r) * gg,)
    return _rowwise(fn, [x], [CDT], name, consts=[g])[0]


def _rms_bwd(x, g, dh, dres, name, bs=512):
    S, D = x.shape
    bs = min(bs, S)
    has_res = dres is not None

    def body(*refs):
        x_ref, g_ref, dh_ref = refs[:3]
        dx_ref, dxc_ref, dg_ref = refs[-3:]
        xf = x_ref[...]
        r = lax.rsqrt(jnp.mean(xf * xf, axis=-1, keepdims=True) + NORM_EPS)
        xhat = xf * r
        dh_ = dh_ref[...].astype(F32)
        dxhat = dh_ * g_ref[...]
        dx = r * (dxhat - xhat * jnp.mean(dxhat * xhat, axis=-1, keepdims=True))
        if has_res:
            dx = dx + refs[3][...]
        dx_ref[...] = dx
        dxc_ref[...] = dx.astype(dxc_ref.dtype)
        part = jnp.sum(dh_ * xhat, axis=0, keepdims=True)

        @pl.when(pl.program_id(0) == 0)
        def _():
            dg_ref[...] = part

        @pl.when(pl.program_id(0) > 0)
        def _():
            dg_ref[...] += part

    row = pl.BlockSpec((bs, D), lambda i: (i, 0))
    one = pl.BlockSpec((1, D), lambda i: (0, 0))
    ins = [x, g.reshape(1, D), dh] + ([dres] if has_res else [])
    return pl.pallas_call(
        body, grid=(S // bs,),
        in_specs=[row, one, row] + ([row] if has_res else []),
        out_specs=[row, row, one],
        out_shape=[jax.ShapeDtypeStruct((S, D), F32), jax.ShapeDtypeStruct((S, D), CDT),
                   jax.ShapeDtypeStruct((1, D), F32)],
        name=name, compiler_params=_cparams(("arbitrary",)),
    )(*ins)


def _pair_norm_fwd(y_a, y_b, g_a, g_b, name, bs=512):
    S, Da = y_a.shape
    Db = y_b.shape[1]
    bs = min(bs, S)

    def body(a_ref, b_ref, ga_ref, gb_ref, o_ref):
        def norm(x, g):
            r = lax.rsqrt(jnp.mean(x * x, axis=-1, keepdims=True) + NORM_EPS)
            return ((x * r) * g).astype(o_ref.dtype)
        o_ref[:, :Da] = norm(a_ref[...], ga_ref[...])
        o_ref[:, Da:] = norm(b_ref[...], gb_ref[...])

    row = lambda n: pl.BlockSpec((bs, n), lambda i: (i, 0))
    one = lambda n: pl.BlockSpec((1, n), lambda i: (0, 0))
    return pl.pallas_call(
        body, grid=(S // bs,), in_specs=[row(Da), row(Db), one(Da), one(Db)], out_specs=row(Da + Db),
        out_shape=jax.ShapeDtypeStruct((S, Da + Db), CDT), name=name, compiler_params=_cparams(("parallel",)),
    )(y_a, y_b, g_a.reshape(1, Da), g_b.reshape(1, Db))


def _pair_norm_bwd(y_a, y_b, g_a, g_b, dmixed, name, bs=512):
    S, Da = y_a.shape
    Db = y_b.shape[1]
    bs = min(bs, S)

    def body(a_ref, b_ref, ga_ref, gb_ref, dm_ref, da_ref, db_ref, dga_ref, dgb_ref):
        def one(x_ref, g_ref, dh, dx_ref, dg_ref):
            xf = x_ref[...]
            r = lax.rsqrt(jnp.mean(xf * xf, axis=-1, keepdims=True) + NORM_EPS)
            xhat = xf * r
            dxhat = dh * g_ref[...]
            dx_ref[...] = r * (dxhat - xhat * jnp.mean(dxhat * xhat, axis=-1, keepdims=True))
            part = jnp.sum(dh * xhat, axis=0, keepdims=True)

            @pl.when(pl.program_id(0) == 0)
            def _():
                dg_ref[...] = part

            @pl.when(pl.program_id(0) > 0)
            def _():
                dg_ref[...] += part

        dm = dm_ref[...]
        one(a_ref, ga_ref, dm[:, :Da], da_ref, dga_ref)
        one(b_ref, gb_ref, dm[:, Da:], db_ref, dgb_ref)

    row = lambda n: pl.BlockSpec((bs, n), lambda i: (i, 0))
    one_ = lambda n: pl.BlockSpec((1, n), lambda i: (0, 0))
    return pl.pallas_call(
        body, grid=(S // bs,), in_specs=[row(Da), row(Db), one_(Da), one_(Db), row(Da + Db)],
        out_specs=[row(Da), row(Db), one_(Da), one_(Db)],
        out_shape=[jax.ShapeDtypeStruct((S, Da), F32), jax.ShapeDtypeStruct((S, Db), F32),
                   jax.ShapeDtypeStruct((1, Da), F32), jax.ShapeDtypeStruct((1, Db), F32)],
        name=name, compiler_params=_cparams(("arbitrary",)),
    )(y_a, y_b, g_a.reshape(1, Da), g_b.reshape(1, Db), dmixed)


def _loss_bwd(x, g, target, name, bs=512):
    S, D = x.shape
    bs = min(bs, S)

    def body(x_ref, g_ref, t_ref, dx_ref, dxc_ref, dg_ref, loss_ref):
        xf = x_ref[...]
        r = lax.rsqrt(jnp.mean(xf * xf, axis=-1, keepdims=True) + NORM_EPS)
        xhat = xf * r
        err = xhat * g_ref[...] - t_ref[...]
        lpart = 0.5 * jnp.sum(jnp.mean(err * err, axis=-1, keepdims=True), axis=0, keepdims=True)
        dy = err / D
        dxhat = dy * g_ref[...]
        dx = r * (dxhat - xhat * jnp.mean(dxhat * xhat, axis=-1, keepdims=True))
        dx_ref[...] = dx
        dxc_ref[...] = dx.astype(dxc_ref.dtype)
        gpart = jnp.sum(dy * xhat, axis=0, keepdims=True)

        @pl.when(pl.program_id(0) == 0)
        def _():
            dg_ref[...] = gpart
            loss_ref[...] = lpart

        @pl.when(pl.program_id(0) > 0)
        def _():
            dg_ref[...] += gpart
            loss_ref[...] += lpart

    row = pl.BlockSpec((bs, D), lambda i: (i, 0))
    one = pl.BlockSpec((1, D), lambda i: (0, 0))
    return pl.pallas_call(
        body, grid=(S // bs,),
        in_specs=[row, one, row],
        out_specs=[row, row, one, pl.BlockSpec((1, 1), lambda i: (0, 0))],
        out_shape=[jax.ShapeDtypeStruct((S, D), F32), jax.ShapeDtypeStruct((S, D), CDT),
                   jax.ShapeDtypeStruct((1, D), F32), jax.ShapeDtypeStruct((1, 1), F32)],
        name=name, compiler_params=_cparams(("arbitrary",)),
    )(x, g.reshape(1, D), target)


_NN = (((1,), (0,)), ((), ()))
_NT = (((1,), (1,)), ((), ()))
_TN = (((0,), (0,)), ((), ()))


def _mm(a, b, *, M, N, K, a_spec, b_spec, o_spec, dims, tm, tn, tk, name, out_shapes, extras=(), epi=None):
    nk = K // tk
    n_ex, n_out = len(extras), len(out_shapes)
    in_place = epi is None
    if in_place:
        assert n_out == 1 and n_ex <= 1 and out_shapes[0].dtype == F32
        epi = lambda acc, *r: (acc + r[0] if r else acc,)

    def body(*refs):
        a_ref, b_ref = refs[0], refs[1]
        ex = refs[2:2 + n_ex]
        outs = refs[2 + n_ex:2 + n_ex + n_out]
        part = lax.dot_general(a_ref[...], b_ref[...], dims, preferred_element_type=F32)

        def finish(acc):
            for o, r in zip(outs, epi(acc, *[e[...] for e in ex])):
                o[...] = r.astype(o.dtype)

        if nk == 1:
            finish(part)
        elif in_place:
            k = pl.program_id(2)

            @pl.when(k == 0)
            def _():
                finish(part)

            @pl.when(k > 0)
            def _():
                outs[0][...] += part
        else:
            acc_ref = refs[-1]
            k = pl.program_id(2)

            @pl.when(k == 0)
            def _():
                acc_ref[...] = part

            @pl.when(k > 0)
            def _():
                acc_ref[...] += part

            @pl.when(k == nk - 1)
            def _():
                finish(acc_ref[...])

    ex_spec = pl.BlockSpec((tm, tn), lambda i, j, k: (i, j))
    return pl.pallas_call(
        body, grid=(M // tm, N // tn, nk),
        in_specs=[a_spec, b_spec] + [ex_spec] * n_ex,
        out_specs=[o_spec] * n_out,
        out_shape=out_shapes,
        scratch_shapes=[pltpu.VMEM((tm, tn), F32)] if nk > 1 and not in_place else [],
        name=name, compiler_params=_cparams(("parallel", "parallel", "arbitrary")),
    )(a, b, *extras)


def _mm_tiles(K):
    return (2048, 512, 2048) if K <= 2048 else (1024, 1024, 2048)


def _mm_nn(a, b, name, out_dtypes, extras=(), epi=None, b_slots=False, b_cols=None):
    M, K = a.shape
    tm, tn, tk = _mm_tiles(K)
    if b_slots:
        ns, _, Ns = b.shape
        N = ns * Ns
        tn = _tile(Ns, tn)
        npb = Ns // tn
        tk_ = _tile(K, tk)
        b_spec = pl.BlockSpec((None, tk_, tn), lambda i, j, k: (j // npb, k, j % npb))
    else:
        first, N = b_cols if b_cols is not None else (0, b.shape[1])
        tn = _tile(N, tn)
        assert first % tn == 0
        tk_ = _tile(K, tk)
        b_spec = pl.BlockSpec((tk_, tn), lambda i, j, k: (k, first // tn + j))
    tm = _tile(M, tm)
    return _mm(a, b, M=M, N=N, K=K, a_spec=pl.BlockSpec((tm, tk_), lambda i, j, k: (i, k)), b_spec=b_spec,
               o_spec=pl.BlockSpec((tm, tn), lambda i, j, k: (i, j)), dims=_NN, tm=tm, tn=tn, tk=tk_, name=name,
               out_shapes=[jax.ShapeDtypeStruct((M, N), d) for d in out_dtypes], extras=extras, epi=epi)


def _mm_nt(a, b, name, out_dtypes, extras=(), epi=None, b_slots=False):
    M, K = a.shape
    tm, tn, tk = _mm_tiles(K)
    tm = _tile(M, tm)
    if b_slots:
        ns, N, Ks = b.shape
        tk_ = _tile(Ks, tk)
        kpb = Ks // tk_
        tn = _tile(N, tn)
        b_spec = pl.BlockSpec((None, tn, tk_), lambda i, j, k: (k // kpb, j, k % kpb))
    else:
        N = b.shape[0]
        tk_ = _tile(K, tk)
        tn = _tile(N, tn)
        b_spec = pl.BlockSpec((tn, tk_), lambda i, j, k: (j, k))
    return _mm(a, b, M=M, N=N, K=K, a_spec=pl.BlockSpec((tm, tk_), lambda i, j, k: (i, k)), b_spec=b_spec,
               o_spec=pl.BlockSpec((tm, tn), lambda i, j, k: (i, j)), dims=_NT, tm=tm, tn=tn, tk=tk_, name=name,
               out_shapes=[jax.ShapeDtypeStruct((M, N), d) for d in out_dtypes], extras=extras, epi=epi)


def _mm_tn(a, b, name, out_dtype, out_slots=0, tm=1024, tn=1024, tk=4096):
    K, M = a.shape
    N = b.shape[1]
    tm, tk_ = _tile(M, tm), _tile(K, tk)
    if out_slots:
        Ns = N // out_slots
        tn = _tile(Ns, tn)
        npb = Ns // tn
        o_spec = pl.BlockSpec((None, tm, tn), lambda i, j, k: (j // npb, i, j % npb))
        out_shape = jax.ShapeDtypeStruct((out_slots, M, Ns), out_dtype)
    else:
        tn = _tile(N, tn)
        o_spec = pl.BlockSpec((tm, tn), lambda i, j, k: (i, j))
        out_shape = jax.ShapeDtypeStruct((M, N), out_dtype)
    return _mm(a, b, M=M, N=N, K=K, a_spec=pl.BlockSpec((tk_, tm), lambda i, j, k: (k, i)),
               b_spec=pl.BlockSpec((tk_, tn), lambda i, j, k: (k, j)), o_spec=o_spec, dims=_TN,
               tm=tm, tn=tn, tk=tk_, name=name, out_shapes=[out_shape],
               epi=None if out_dtype == F32 else (lambda acc: (acc,)))[0]


GATE_BLOCK = 512


def _split3(v):
    hi = v.astype(jnp.bfloat16)
    r1 = v - hi.astype(F32)
    mid = r1.astype(jnp.bfloat16)
    lo = (r1 - mid.astype(F32)).astype(jnp.bfloat16)
    return hi, mid, lo


def _exact_dot(v, tri):
    return functools.reduce(jnp.add, [jnp.dot(t, tri, preferred_element_type=F32) for t in _split3(v)])


def _gates_fwd(f_t, b, name):
    H, S = f_t.shape
    nb = _tile(S, GATE_BLOCK)
    inv_scale = HEAD_DIM ** 0.5

    def body(f_ref, b_ref, c_ref):
        upper = (lax.broadcasted_iota(jnp.int32, (nb, nb), 0)
                 <= lax.broadcasted_iota(jnp.int32, (nb, nb), 1)).astype(jnp.bfloat16)
        carry = jnp.zeros((H, 1), F32)
        for i in range(S // nb):
            z = f_ref[:, i * nb:(i + 1) * nb] + b_ref[...]
            logf = jnp.minimum(z, 0.0) - jnp.log1p(jnp.exp(-jnp.abs(z)))
            cs = _exact_dot(logf, upper) + carry
            for j, t in enumerate(_split3(cs * inv_scale)):
                c_ref[j, :, i * nb:(i + 1) * nb] = t
            carry = cs[:, nb - 1:nb]

    return pl.pallas_call(body, out_shape=jax.ShapeDtypeStruct((3, H, S), jnp.bfloat16), name=name,
                          compiler_params=_cparams())(f_t, b.reshape(H, 1))


def _gates_bwd(f_t, b, dcq, dck, name):
    H, S = f_t.shape
    nb = _tile(S, GATE_BLOCK)

    def body(f_ref, b_ref, dcq_ref, dck_ref, df_ref, dfc_ref, db_ref):
        lower = (lax.broadcasted_iota(jnp.int32, (nb, nb), 0)
                 >= lax.broadcasted_iota(jnp.int32, (nb, nb), 1)).astype(jnp.bfloat16)
        carry = jnp.zeros((H, 1), F32)
        db = jnp.zeros((H, 1), F32)
        for i in reversed(range(S // nb)):
            sl = slice(i * nb, (i + 1) * nb)
            dc = dcq_ref[:, sl] - dck_ref[:, sl]
            dlogf = _exact_dot(dc, lower) + carry
            carry = dlogf[:, 0:1]
            z = f_ref[:, sl] + b_ref[...]
            df = dlogf / (1.0 + jnp.exp(z))
            df_ref[:, sl] = df
            dfc_ref[:, sl] = df.astype(dfc_ref.dtype)
            db = db + jnp.sum(df, axis=1, keepdims=True)
        db_ref[...] = db

    return pl.pallas_call(
        body, out_shape=[jax.ShapeDtypeStruct((H, S), F32), jax.ShapeDtypeStruct((H, S), CDT),
                         jax.ShapeDtypeStruct((H, 1), F32)],
        name=name, compiler_params=_cparams())(f_t, b.reshape(H, 1), dcq, dck)


FOX_BLOCK = 1024


def _fox_bias_operands(csplit, name, bs=512):
    _, H, S = csplit.shape
    E = HEAD_DIM
    bs = _tile(S, bs)
    part = jnp.arange(3 * H)[:, None] // H
    head = jnp.arange(3 * H)[:, None] % H
    lane = jnp.arange(H * E)[None, :]
    place_q = (lane == head * E + part).astype(csplit.dtype)
    place_k = -(lane == head * E + 3 + part).astype(csplit.dtype)
    ones_q = ((lane % E >= 3) & (lane % E < 6)).astype(F32)
    ones_k = (lane % E < 3).astype(F32)

    def body(c_ref, pq_ref, pk_ref, oq_ref, ok_ref, qc_ref, kc_ref):
        c = c_ref[...]
        qc_ref[...] = (lax.dot_general(c, pq_ref[...], _TN, preferred_element_type=F32) + oq_ref[...]).astype(qc_ref.dtype)
        kc_ref[...] = (lax.dot_general(c, pk_ref[...], _TN, preferred_element_type=F32) + ok_ref[...]).astype(kc_ref.dtype)

    full = lambda a: pl.BlockSpec(a.shape, lambda i: (0, 0))
    out = pl.BlockSpec((bs, H * E), lambda i: (i, 0))
    return pl.pallas_call(
        body, grid=(S // bs,),
        in_specs=[pl.BlockSpec((3 * H, bs), lambda i: (0, i)), full(place_q), full(place_k), full(ones_q), full(ones_k)],
        out_specs=[out, out], out_shape=[jax.ShapeDtypeStruct((S, H * E), csplit.dtype)] * 2,
        name=name, compiler_params=_cparams(("parallel",)),
    )(csplit.reshape(3 * H, S), place_q, place_k, ones_q, ones_k)


def _fox_logits2(q_ref, qc_ref, k_ref, kc_ref, diag):
    q, k = q_ref[...], k_ref[...]
    qa = jnp.concatenate([q, qc_ref[...].astype(q.dtype)], axis=1)
    ka = jnp.concatenate([k, kc_ref[...].astype(k.dtype)], axis=1)
    s = lax.dot_general(qa, ka, _NT, preferred_element_type=F32) * (HEAD_DIM ** -0.5 * LOG2E)
    if diag:
        row = lax.broadcasted_iota(jnp.int32, s.shape, 0)
        col = lax.broadcasted_iota(jnp.int32, s.shape, 1)
        s = jnp.where(col <= row, s, NEG_INF)
    return s


def _fox_fwd(proj, qc, kc, H, name):
    S = proj.shape[0]
    E = HEAD_DIM
    blk = _tile(S, FOX_BLOCK)
    nq = S // blk

    def pair(t):
        qi = sum((t >= i * (i + 1) // 2).astype(jnp.int32) for i in range(1, nq)) if nq > 1 else 0 * t
        return qi, t - qi * (qi + 1) // 2

    def body(q_ref, qc_ref, k_ref, kc_ref, v_ref, o_ref, lse_ref, m_s, l_s, acc_s):
        qi, kj = pair(pl.program_id(1))

        @pl.when(kj == 0)
        def _():
            m_s[...] = jnp.full(m_s.shape, NEG_INF, F32)
            l_s[...] = jnp.zeros(l_s.shape, F32)
            acc_s[...] = jnp.zeros(acc_s.shape, F32)

        def step(diag):
            s = _fox_logits2(q_ref, qc_ref, k_ref, kc_ref, diag)
            m_prev = m_s[...]
            m_new = jnp.maximum(m_prev, jnp.max(s, axis=-1, keepdims=True))
            alpha = jnp.exp2(m_prev - m_new)
            p = jnp.exp2(s - m_new)
            l_s[...] = alpha * l_s[...] + jnp.sum(p, axis=-1, keepdims=True)
            acc_s[...] = alpha * acc_s[...] + jnp.dot(p.astype(CDT), v_ref[...], preferred_element_type=F32)
            m_s[...] = m_new

        pl.when(kj < qi)(lambda: step(False))
        pl.when(kj == qi)(lambda: step(True))

        @pl.when(kj == qi)
        def _():
            o_ref[...] = acc_s[...] / l_s[...]
            lse_ref[...] = jnp.broadcast_to(m_s[...] + jnp.log2(l_s[...]), lse_ref.shape)

    qspec = lambda off: pl.BlockSpec((blk, E), lambda h, t: (pair(t)[0], off + h))
    kspec = lambda off: pl.BlockSpec((blk, E), lambda h, t: (pair(t)[1], off + h))
    return pl.pallas_call(
        body, grid=(H, nq * (nq + 1) // 2),
        in_specs=[qspec(0), qspec(0), kspec(H), kspec(0), kspec(2 * H)],
        out_specs=[qspec(0)] * 2,
        out_shape=[jax.ShapeDtypeStruct((S, H * E), F32)] * 2,
        scratch_shapes=[pltpu.VMEM((blk, 1), F32), pltpu.VMEM((blk, 1), F32), pltpu.VMEM((blk, E), F32)],
        name=name, compiler_params=_cparams(("parallel", "arbitrary")),
    )(proj, qc, proj, kc, proj)


def _fox_bwd(proj, qc, kc, lse, o, do, H, name):
    S = proj.shape[0]
    E = HEAD_DIM
    blk = _tile(S, FOX_BLOCK)
    nq = S // blk
    scale = E ** -0.5

    def pair(t):
        first = lambda j: j * nq - j * (j - 1) // 2
        kj = sum((t >= first(j)).astype(jnp.int32) for j in range(1, nq)) if nq > 1 else 0 * t
        return kj, kj + t - first(kj)

    def body(q_ref, qc_ref, k_ref, kc_ref, v_ref, lse_ref, o_ref, do_ref,
             dq_ref, dcq_ref, dk_ref, dv_ref, dck_ref, dq_s, dcq_s, dk_s, dv_s, dck_s):
        kj, qi = pair(pl.program_id(1))

        @pl.when(qi == kj)
        def _():
            dk_s[...] = jnp.zeros(dk_s.shape, F32)
            dv_s[...] = jnp.zeros(dv_s.shape, F32)
            dck_s[...] = jnp.zeros(dck_s.shape, F32)

        def step(diag):
            do = do_ref[...]
            doc = do.astype(CDT)
            delta = jnp.sum(do * o_ref[...], axis=-1, keepdims=True)
            p = jnp.exp2(_fox_logits2(q_ref, qc_ref, k_ref, kc_ref, diag) - lse_ref[:, 0:1])
            dp = lax.dot_general(doc, v_ref[...], _NT, preferred_element_type=F32)
            ds = p * (dp - delta)
            dss = ds * scale
            dck_s[...] += jnp.sum(ds, axis=0, keepdims=True)
            dv_s[...] += jnp.dot(p.T.astype(CDT), doc, preferred_element_type=F32)
            dk_s[...] += jnp.dot(dss.T.astype(CDT), q_ref[...], preferred_element_type=F32)
            dq_part = jnp.dot(dss.astype(CDT), k_ref[...], preferred_element_type=F32)
            dc_part = jnp.sum(ds, axis=-1, keepdims=True)
            rows = pl.ds(pl.multiple_of(qi * blk, blk), blk)

            @pl.when(kj == 0)
            def _():
                dq_s[rows, :] = dq_part
                dcq_s[rows, :] = dc_part

            @pl.when(kj > 0)
            def _():
                dq_s[rows, :] += dq_part
                dcq_s[rows, :] += dc_part

        pl.when(qi > kj)(lambda: step(False))
        pl.when(qi == kj)(lambda: step(True))

        @pl.when(qi == nq - 1)
        def _():
            dk_ref[...] = dk_s[...].astype(dk_ref.dtype)
            dv_ref[...] = dv_s[...].astype(dv_ref.dtype)
            dck_ref[...] = dck_s[...].reshape(dck_ref.shape)

        @pl.when((qi == nq - 1) & (kj == nq - 1))
        def _():
            dq_ref[...] = dq_s[...].astype(dq_ref.dtype)
            dcq_ref[...] = jnp.broadcast_to(dcq_s[...], dcq_ref.shape)

    qspec = lambda off: pl.BlockSpec((blk, E), lambda h, t: (pair(t)[1], off + h))
    kspec = lambda off: pl.BlockSpec((blk, E), lambda h, t: (pair(t)[0], off + h))
    head = pl.BlockSpec((S, E), lambda h, t: (0, h))
    return pl.pallas_call(
        body, grid=(H, nq * (nq + 1) // 2),
        in_specs=[qspec(0), qspec(0), kspec(H), kspec(0), kspec(2 * H), qspec(0), qspec(0), qspec(0)],
        out_specs=[head, head, kspec(0), kspec(0), pl.BlockSpec((1, 1, blk), lambda h, t: (h, 0, pair(t)[0]))],
        out_shape=[jax.ShapeDtypeStruct((S, H * E), CDT), jax.ShapeDtypeStruct((S, H * E), F32),
                   jax.ShapeDtypeStruct((S, H * E), CDT), jax.ShapeDtypeStruct((S, H * E), CDT),
                   jax.ShapeDtypeStruct((H, 1, S), F32)],
        scratch_shapes=[pltpu.VMEM((S, E), F32), pltpu.VMEM((S, 1), F32), pltpu.VMEM((blk, E), F32),
                        pltpu.VMEM((blk, E), F32), pltpu.VMEM((1, blk), F32)],
        name=name, compiler_params=_cparams(("parallel", "arbitrary")),
    )(proj, qc, proj, kc, proj, lse, o, do)


DIL_SLAB = 16 * DIL_BLOCK
DIL_UNROLL = 16


def _rel_bucket(dist):
    max_exact = REL_BUCKETS // 2
    d = jnp.maximum(dist.astype(F32), 1.0)
    large = max_exact + (jnp.log(d / max_exact) / jnp.log(jnp.float32(REL_MAX_DISTANCE / max_exact))
                         * (REL_BUCKETS - max_exact)).astype(jnp.int32)
    large = jnp.minimum(large, REL_BUCKETS - 1)
    return jnp.where(dist < max_exact, dist, large)


def _bucket_table():
    i = jnp.arange(DIL_BLOCK)[:, None]
    j = jnp.arange(2 * DIL_BLOCK)[None, :]
    rel = DIL_BLOCK + i - j
    tabs = [_rel_bucket(jnp.clip(rel, 0, w // d) * d) for w, d in DIL_PATTERNS]
    return jnp.stack(tabs).astype(jnp.int32)


def _bias_table(rel_bias, buckets, name):
    P = buckets.shape[0]
    H = rel_bias.shape[1]

    def body(rb_ref, bk_ref, out_ref):
        h = pl.program_id(1)
        bk = bk_ref[0]
        val = jnp.zeros(bk.shape, F32)
        for b in range(REL_BUCKETS):
            val = jnp.where(bk == b, rb_ref[b, h], val)
        out_ref[0, 0] = val

    return pl.pallas_call(
        body, grid=(P, H),
        in_specs=[pl.BlockSpec(memory_space=pltpu.SMEM),
                  pl.BlockSpec((1, DIL_BLOCK, 2 * DIL_BLOCK), lambda p, h: (p, 0, 0))],
        out_specs=pl.BlockSpec((1, 1, DIL_BLOCK, 2 * DIL_BLOCK), lambda p, h: (p, h, 0, 0)),
        out_shape=jax.ShapeDtypeStruct((P, H, DIL_BLOCK, 2 * DIL_BLOCK), F32),
        name=name, compiler_params=_cparams(("parallel", "parallel")),
    )(rel_bias, buckets)


def _bias_table_bwd(dbias, buckets, name):
    P, H = dbias.shape[:2]

    def body(db_ref, bk_ref, out_ref):
        lane = lax.broadcasted_iota(jnp.int32, (1, REL_BUCKETS), 1)
        acc = jnp.zeros((1, REL_BUCKETS), F32)
        bk = bk_ref[...]
        db = db_ref[:, 0]
        for b in range(REL_BUCKETS):
            tot = jnp.sum(jnp.where(bk == b, db, 0.0))
            acc = jnp.where(lane == b, tot, acc)
        out_ref[0] = acc

    return pl.pallas_call(
        body, grid=(H,),
        in_specs=[pl.BlockSpec((P, 1, DIL_BLOCK, 2 * DIL_BLOCK), lambda h: (0, h, 0, 0)),
                  pl.BlockSpec((P, DIL_BLOCK, 2 * DIL_BLOCK), lambda h: (0, 0, 0))],
        out_specs=pl.BlockSpec((1, 1, REL_BUCKETS), lambda h: (h, 0, 0)),
        out_shape=jax.ShapeDtypeStruct((H, 1, REL_BUCKETS), F32),
        name=name, compiler_params=_cparams(("parallel",)),
    )(dbias, buckets)


def _bdot(a, b, contract_b):
    return lax.dot_general(a, b, (((2,), (contract_b,)), ((0,), (0,))), preferred_element_type=F32)


def _dil_units(first, d):
    units = []
    for t in range(DIL_UNROLL):
        u = first + t
        sg = u // d
        units.append((sg, sg * (DIL_BLOCK * d) + u % d))
    return units


def _dil_rows(ref, starts, d, dtype=None):
    t = jnp.stack([ref[pl.ds(s, DIL_BLOCK, stride=d), :] for s in starts])
    return t if dtype is None else t.astype(dtype)


def _dil_keys(ref, units, d):
    B, SL = DIL_BLOCK, DIL_SLAB
    return jnp.stack([jnp.concatenate([ref[pl.ds(SL + b - B * d, B, stride=d), :], ref[pl.ds(SL + b, B, stride=d), :]],
                                      axis=0) for _, b in units]).astype(CDT)


def _dil_logits(q, keys, bias_pc, first, d, has_before):
    T, B = q.shape[0], DIL_BLOCK
    ii = lax.broadcasted_iota(jnp.int32, (T, B, 2 * B), 1)
    jj = lax.broadcasted_iota(jnp.int32, (T, B, 2 * B), 2)
    sg = (first + lax.broadcasted_iota(jnp.int32, (T, B, 2 * B), 0)) // d
    mask = (jj >= ii) & (jj <= ii + B) & ((jj >= B) | (sg > 0) | has_before)
    return jnp.where(mask, _bdot(q, keys, 2) * HEAD_DIM ** -0.5 + bias_pc[None], NEG_INF)


def _dil_specs(H):
    E, SL = DIL_BLOCK, DIL_SLAB
    cur = lambda off: pl.BlockSpec((SL, E), lambda h, g: (g, off + h))
    prev = lambda off: pl.BlockSpec((SL, E), lambda h, g: (jnp.maximum(g - 1, 0), off + h))
    bias = pl.BlockSpec((len(DIL_PATTERNS), 1, E, 2 * E), lambda h, g: (0, h, 0, 0))
    return cur, prev, bias


def _dil_fwd(proj, bias, H, name):
    S = proj.shape[0]
    E = B = DIL_BLOCK
    SL = DIL_SLAB
    P = len(DIL_PATTERNS)
    assert S % SL == 0
    n_slabs = S // SL

    def body(q_ref, kc_ref, kp_ref, vc_ref, vp_ref, b_ref, y_ref, lse_ref, kj, vj, o_s, l_s):
        g = pl.program_id(1)
        kj[0:SL, :] = kp_ref[...]
        kj[SL:2 * SL, :] = kc_ref[...]
        vj[0:SL, :] = vp_ref[...]
        vj[SL:2 * SL, :] = vc_ref[...]
        for p, (_, d) in enumerate(DIL_PATTERNS):
            def batch(it, carry, p=p, d=d):
                first = it * DIL_UNROLL
                units = _dil_units(first, d)
                q = _dil_rows(q_ref, [b for _, b in units], d, CDT)
                s = _dil_logits(q, _dil_keys(kj, units, d), b_ref[p, 0], first, d, g > 0)
                m = jnp.max(s, axis=-1, keepdims=True)
                e = jnp.exp(s - m)
                ssum = jnp.sum(e, axis=-1, keepdims=True)
                o = _bdot(e.astype(CDT), _dil_keys(vj, units, d), 1) / ssum
                lse = jnp.broadcast_to(m + jnp.log(ssum), o.shape)
                for t, (_, b) in enumerate(units):
                    o_s[p, pl.ds(b, B, stride=d), :] = o[t]
                    l_s[p, pl.ds(b, B, stride=d), :] = lse[t]
                return carry

            lax.fori_loop(0, SL // B // DIL_UNROLL, batch, 0)
        ls = [l_s[p] for p in range(P)]
        m = functools.reduce(jnp.maximum, ls)
        w = [jnp.exp(l - m) for l in ls]
        tot = functools.reduce(jnp.add, w)
        y_ref[...] = functools.reduce(jnp.add, [(w[p] / tot) * o_s[p] for p in range(P)])
        lse_ref[...] = m + jnp.log(tot)

    cur, prev, bspec = _dil_specs(H)
    return pl.pallas_call(
        body, grid=(H, n_slabs),
        in_specs=[cur(0), cur(H), prev(H), cur(2 * H), prev(2 * H), bspec],
        out_specs=[cur(0), cur(0)],
        out_shape=[jax.ShapeDtypeStruct((S, H * E), F32)] * 2,
        scratch_shapes=[pltpu.VMEM((2 * SL, E), F32), pltpu.VMEM((2 * SL, E), F32),
                        pltpu.VMEM((P, SL, E), F32), pltpu.VMEM((P, SL, E), F32)],
        name=name, compiler_params=_cparams(("parallel", "parallel")),
    )(proj, proj, proj, proj, proj, bias)


def _dil_bwd(proj, bias, y, dy, lse, H, name):
    S = proj.shape[0]
    E = B = DIL_BLOCK
    SL = DIL_SLAB
    P = len(DIL_PATTERNS)
    assert S % SL == 0
    n_slabs = S // SL
    scale = E ** -0.5

    def body(q_ref, kc_ref, kp_ref, vc_ref, vp_ref, b_ref, y_ref, dy_ref, lse_ref,
             dq_ref, dk_ref, dv_ref, db_ref, kj, vj, dq_s, dk_own, dv_own, dk_held, dv_held, dk_back, dv_back, dl_s):
        g = pl.program_id(1)

        @pl.when(g == 0)
        def _():
            db_ref[...] = jnp.zeros(db_ref.shape, F32)

        @pl.when(g > 0)
        def _():
            dk_held[...] = dk_own[...]
            dv_held[...] = dv_own[...]
            dk_back[...] = jnp.zeros(dk_back.shape, F32)
            dv_back[...] = jnp.zeros(dv_back.shape, F32)

        @pl.when(g < n_slabs)
        def _():
            kj[0:SL, :] = kp_ref[...]
            kj[SL:2 * SL, :] = kc_ref[...]
            vj[0:SL, :] = vp_ref[...]
            vj[SL:2 * SL, :] = vc_ref[...]
            dq_s[...] = jnp.zeros(dq_s.shape, F32)
            dk_own[...] = jnp.zeros(dk_own.shape, F32)
            dv_own[...] = jnp.zeros(dv_own.shape, F32)
            dl_s[...] = jnp.broadcast_to(jnp.sum(dy_ref[...] * y_ref[...], axis=-1, keepdims=True), (SL, E))
            tr = lambda t: jnp.swapaxes(t, 1, 2).astype(CDT)
            for p, (_, d) in enumerate(DIL_PATTERNS):
                def batch(it, carry, p=p, d=d):
                    first = it * DIL_UNROLL
                    units = _dil_units(first, d)
                    starts = [b for _, b in units]
                    q = _dil_rows(q_ref, starts, d, CDT)
                    dyc = _dil_rows(dy_ref, starts, d, CDT)
                    keys, vals = _dil_keys(kj, units, d), _dil_keys(vj, units, d)
                    s = _dil_logits(q, keys, b_ref[p, 0], first, d, g > 0)
                    e = jnp.exp(s - _dil_rows(lse_ref, starts, d)[:, :, 0:1])
                    ds = e * (_bdot(dyc, vals, 2) - _dil_rows(dl_s, starts, d)[:, :, 0:1])
                    dss = ds * scale
                    dq = _bdot(dss.astype(CDT), keys, 1)
                    dk = _bdot(tr(dss), q, 1)
                    dv = _bdot(tr(e), dyc, 1)
                    for t, (sg, b) in enumerate(units):
                        rows = pl.ds(b, B, stride=d)
                        dq_s[rows, :] += dq[t]
                        dk_own[rows, :] += dk[t, B:]
                        dv_own[rows, :] += dv[t, B:]

                        if B * d < SL:
                            @pl.when(sg > 0)
                            def _(t=t, b=b):
                                before = pl.ds(b - B * d, B, stride=d)
                                dk_own[before, :] += dk[t, :B]
                                dv_own[before, :] += dv[t, :B]

                        @pl.when((sg == 0) & (g > 0))
                        def _(t=t, b=b):
                            before = pl.ds(SL + b - B * d, B, stride=d)
                            dk_back[before, :] += dk[t, :B]
                            dv_back[before, :] += dv[t, :B]

                    db_ref[p, 0] += jnp.sum(ds, axis=0)
                    return carry

                lax.fori_loop(0, SL // B // DIL_UNROLL, batch, 0)
            dq_ref[...] = dq_s[...].astype(dq_ref.dtype)

        @pl.when(g > 0)
        def _():
            dk_ref[...] = (dk_held[...] + dk_back[...]).astype(dk_ref.dtype)
            dv_ref[...] = (dv_held[...] + dv_back[...]).astype(dv_ref.dtype)

    last = n_slabs - 1
    cur = lambda off: pl.BlockSpec((SL, E), lambda h, g: (jnp.minimum(g, last), off + h))
    prev = lambda off: pl.BlockSpec((SL, E), lambda h, g: (jnp.maximum(jnp.minimum(g, last) - 1, 0), off + h))
    late = pl.BlockSpec((SL, E), lambda h, g: (jnp.maximum(g - 1, 0), h))
    bspec = pl.BlockSpec((P, 1, B, 2 * B), lambda h, g: (0, h, 0, 0))
    slab = pltpu.VMEM((SL, E), F32)
    return pl.pallas_call(
        body, grid=(H, n_slabs + 1),
        in_specs=[cur(0), cur(H), prev(H), cur(2 * H), prev(2 * H), bspec, cur(0), cur(0), cur(0)],
        out_specs=[cur(0), late, late, bspec],
        out_shape=[jax.ShapeDtypeStruct((S, H * E), CDT)] * 3 + [jax.ShapeDtypeStruct((P, H, B, 2 * B), F32)],
        scratch_shapes=[pltpu.VMEM((2 * SL, E), F32), pltpu.VMEM((2 * SL, E), F32)] + [slab] * 8,
        name=name, compiler_params=_cparams(("parallel", "arbitrary")),
    )(proj, proj, proj, proj, proj, bias, y, dy, lse)


def _adamw_tile(g_, w_ref, m_ref, v_ref, g_out, d_ref, nm_ref, nv_ref):
    g_out[...] = g_
    m_ = ADAM_B1 * m_ref[...] + (1.0 - ADAM_B1) * g_
    v_ = ADAM_B2 * v_ref[...] + (1.0 - ADAM_B2) * jnp.square(g_)
    m_hat = m_ / (1.0 - ADAM_B1 ** ADAM_STEP)
    v_hat = v_ / (1.0 - ADAM_B2 ** ADAM_STEP)
    d_ref[...] = -ADAM_LR * (m_hat / (jnp.sqrt(v_hat) + ADAM_EPS) + ADAM_WD * w_ref[...])
    nm_ref[...] = m_
    nv_ref[...] = v_


def _adamw(w, g, m, v, name, br=256):
    R, C = w.shape
    br = br if R % br == 0 else R

    def body(w_ref, g_ref, m_ref, v_ref, *outs):
        _adamw_tile(g_ref[...], w_ref, m_ref, v_ref, *outs)

    blk = pl.BlockSpec((br, C), lambda i: (i, 0))
    return pl.pallas_call(
        body, grid=(R // br,), in_specs=[blk] * 4, out_specs=[blk] * 4,
        out_shape=[jax.ShapeDtypeStruct((R, C), F32)] * 4,
        name=name, compiler_params=_cparams(("parallel",)),
    )(w, g, m, v)


_HBM = pl.BlockSpec(memory_space=pltpu.HBM)
_SEM = pl.BlockSpec(memory_space=pltpu.SEMAPHORE)
_ANY = pl.BlockSpec(memory_space=pl.ANY)
_VMEM = pl.BlockSpec(memory_space=pltpu.VMEM)
_TOKEN = jax.ShapeDtypeStruct((8, 128), F32)


def _split_params():
    return pltpu.CompilerParams(has_side_effects=pltpu.SideEffectType.DATAFLOW_SIDE_EFFECTING)


def _place():
    x, y, c = lax.axis_index("x"), lax.axis_index("y"), lax.axis_index("c")
    chips = [(1 - x, y), (x, 1 - y), (1 - x, 1 - y)]
    return x, y, c, chips


def _tie(v, tokens, name):
    flat = v.reshape(1, -1)

    def body(v_ref, *rest):
        rest[-1][...] = v_ref[...]

    return pl.pallas_call(body, in_specs=[_VMEM] + [_ANY] * len(tokens), out_specs=_VMEM,
                          out_shape=jax.ShapeDtypeStruct(flat.shape, flat.dtype), name=name,
                          compiler_params=_cparams())(flat, *tokens).reshape(v.shape)


def _row_block(R, pref=256):
    return _tile(R, pref) if R % 128 == 0 else R


def _slot():
    return 2 * lax.axis_index("x") + lax.axis_index("y")


def _cast_into_slot(w, layer, name):
    _, R, C = w.shape
    br = _row_block(R)

    def body(w_ref, out_ref):
        out_ref[...] = w_ref[...].astype(out_ref.dtype)

    return pl.pallas_call(
        body, grid=(R // br,),
        in_specs=[pl.BlockSpec((None, br, C), lambda i: (layer, i, 0))],
        out_specs=pl.BlockSpec((None, br, C), lambda i: (_slot(), i, 0)),
        out_shape=jax.ShapeDtypeStruct((N_CHIPS, R, C), CDT),
        name=name, compiler_params=_cparams(("parallel",)),
    )(w)


def _gather_copies(src_ref, dst_ref, send_sems, recv_sems, incoming):
    Rh = src_ref.shape[1] // 2
    x, y, c, chips = _place()
    slot = 2 * x + y

    def half(ref, s, hf):
        return ref.at[s, pl.ds(hf * Rh, Rh), :]

    copies = []
    for j, (cx, cy) in enumerate(chips):
        for e in range(2):
            copies.append(pltpu.make_async_remote_copy(
                src_ref=half(src_ref, slot, c), dst_ref=half(dst_ref, 2 * cx + cy, e) if incoming else half(dst_ref, slot, c),
                send_sem=send_sems.at[2 * j + e], recv_sem=recv_sems.at[2 * j + (e if incoming else c)],
                device_id=(cx, cy, e), device_id_type=MESH))
    return copies


def _gather_start(buf, after, name):
    n_after = len(after)

    def body(*refs):
        buf_ref = refs[0]
        send_sems, recv_sems, out_ref, token = refs[1 + n_after:]
        for cp in _gather_copies(buf_ref, out_ref, send_sems, recv_sems, incoming=False):
            cp.start()
        token[...] = jnp.zeros(token.shape, token.dtype)

    return pl.pallas_call(
        body, in_specs=[_HBM] + [_ANY] * n_after, out_specs=(_SEM, _SEM, _HBM, _VMEM),
        out_shape=(pltpu.SemaphoreType.DMA((6,)), pltpu.SemaphoreType.DMA((6,)), pltpu.HBM(buf.shape, buf.dtype), _TOKEN),
        input_output_aliases={0: 2}, name=name, compiler_params=_split_params(),
    )(pltpu.with_memory_space_constraint(buf, pltpu.HBM), *after)


def _gather_wait(send_sems, recv_sems, buf, after, name):
    def body(buf_ref, send_sems, recv_sems, after_ref, out_ref):
        for cp in _gather_copies(buf_ref, out_ref, send_sems, recv_sems, incoming=False):
            cp.wait_send()
        for cp in _gather_copies(buf_ref, out_ref, send_sems, recv_sems, incoming=True):
            cp.wait_recv()

    return pl.pallas_call(
        body, in_specs=[_HBM, _SEM, _SEM, _ANY], out_specs=_HBM, out_shape=pltpu.HBM(buf.shape, buf.dtype),
        input_output_aliases={0: 0}, name=name, compiler_params=_split_params(),
    )(buf, send_sems, recv_sems, after)


def _relay_copies(src_ref, dst_ref, send_sems, recv_sems, stage, incoming):
    Rh = src_ref.shape[1] // 2
    x, y, c, chips = _place()
    copies = []
    for j, (cx, cy) in enumerate(chips):
        if stage == 0:
            src_slot, src_half, peer = 2 * x + y, c, (cx, cy, c)
            dst_slot, dst_half = (2 * cx + cy, c) if incoming else (src_slot, c)
        else:
            src_slot, src_half, peer = 2 * cx + cy, c, (x, y, 1 - c)
            dst_slot, dst_half = src_slot, (1 - c if incoming else c)
        copies.append(pltpu.make_async_remote_copy(
            src_ref=src_ref.at[src_slot, pl.ds(src_half * Rh, Rh), :],
            dst_ref=dst_ref.at[dst_slot, pl.ds(dst_half * Rh, Rh), :],
            send_sem=send_sems.at[j], recv_sem=recv_sems.at[j], device_id=peer, device_id_type=MESH))
    return copies


def _relay_start(buf, after, name):
    n_after = len(after)

    def body(*refs):
        buf_ref = refs[0]
        send_sems, recv_sems, out_ref, token = refs[1 + n_after:]
        for cp in _relay_copies(buf_ref, out_ref, send_sems, recv_sems, 0, incoming=False):
            cp.start()
        token[...] = jnp.zeros(token.shape, token.dtype)

    return pl.pallas_call(
        body, in_specs=[_HBM] + [_ANY] * n_after, out_specs=(_SEM, _SEM, _HBM, _VMEM),
        out_shape=(pltpu.SemaphoreType.DMA((3,)), pltpu.SemaphoreType.DMA((3,)), pltpu.HBM(buf.shape, buf.dtype), _TOKEN),
        input_output_aliases={0: 2}, name=name, compiler_params=_split_params(),
    )(pltpu.with_memory_space_constraint(buf, pltpu.HBM), *after)


def _relay_pass(send_sems, recv_sems, buf, after, name):
    def body(buf_ref, send0, recv0, after_ref, send1, recv1, out_ref):
        for cp in _relay_copies(buf_ref, out_ref, send0, recv0, 0, incoming=False):
            cp.wait_send()
        for cp in _relay_copies(buf_ref, out_ref, send0, recv0, 0, incoming=True):
            cp.wait_recv()
        for cp in _relay_copies(out_ref, out_ref, send1, recv1, 1, incoming=False):
            cp.start()

    return pl.pallas_call(
        body, in_specs=[_HBM, _SEM, _SEM, _ANY], out_specs=(_SEM, _SEM, _HBM),
        out_shape=(pltpu.SemaphoreType.DMA((3,)), pltpu.SemaphoreType.DMA((3,)), pltpu.HBM(buf.shape, buf.dtype)),
        input_output_aliases={0: 2}, name=name, compiler_params=_split_params(),
    )(buf, send_sems, recv_sems, after)


def _relay_wait(send_sems, recv_sems, buf, after, name):
    def body(buf_ref, send1, recv1, after_ref, out_ref):
        for cp in _relay_copies(buf_ref, out_ref, send1, recv1, 1, incoming=False):
            cp.wait_send()
        for cp in _relay_copies(buf_ref, out_ref, send1, recv1, 1, incoming=True):
            cp.wait_recv()

    return pl.pallas_call(
        body, in_specs=[_HBM, _SEM, _SEM, _ANY], out_specs=_HBM, out_shape=pltpu.HBM(buf.shape, buf.dtype),
        input_output_aliases={0: 0}, name=name, compiler_params=_split_params(),
    )(buf, send_sems, recv_sems, after)


def _scatter_copies(g_ref, land_ref, send_sems, recv_sems, incoming):
    Rh = g_ref.shape[1] // 2
    x, y, c, _ = _place()
    me = 4 * x + 2 * y + c
    copies = []
    for k in range(1, N_DEV):
        px, py, pc = (x + (k >> 2)) % 2, (y + ((k >> 1) & 1)) % 2, (c + (k & 1)) % 2
        copies.append(pltpu.make_async_remote_copy(
            src_ref=g_ref.at[2 * px + py, pl.ds(pc * Rh, Rh), :],
            dst_ref=land_ref.at[4 * px + 2 * py + pc if incoming else me],
            send_sem=send_sems.at[k - 1], recv_sem=recv_sems.at[k - 1], device_id=(px, py, pc), device_id_type=MESH))
    return copies


def _scatter_start(g, name):
    ns, R, C = g.shape

    def body(g_ref, land_ref, send_sems, recv_sems, g_thru, land_thru, token):
        for cp in _scatter_copies(g_ref, land_thru, send_sems, recv_sems, incoming=False):
            cp.start()
        token[...] = jnp.zeros(token.shape, token.dtype)

    land = lax.empty((N_DEV, R // 2, C), g.dtype)
    n = N_DEV - 1
    return pl.pallas_call(
        body, in_specs=[_HBM, _HBM], out_specs=(_SEM, _SEM, _HBM, _HBM, _VMEM),
        out_shape=(pltpu.SemaphoreType.DMA((n,)), pltpu.SemaphoreType.DMA((n,)), pltpu.HBM(g.shape, g.dtype),
                   pltpu.HBM(land.shape, land.dtype), _TOKEN),
        input_output_aliases={0: 2, 1: 3}, name=name, compiler_params=_split_params(),
    )(pltpu.with_memory_space_constraint(g, pltpu.HBM), pltpu.with_memory_space_constraint(land, pltpu.HBM))


def _scatter_wait(send_sems, recv_sems, g, land, after, name):
    def body(g_ref, land_ref, send_sems, recv_sems, after_ref, g_out, land_out):
        for cp in _scatter_copies(g_ref, land_out, send_sems, recv_sems, incoming=False):
            cp.wait_send()
        for cp in _scatter_copies(g_ref, land_out, send_sems, recv_sems, incoming=True):
            cp.wait_recv()

    return pl.pallas_call(
        body, in_specs=[_HBM, _HBM, _SEM, _SEM, _ANY], out_specs=(_HBM, _HBM),
        out_shape=(pltpu.HBM(g.shape, g.dtype), pltpu.HBM(land.shape, land.dtype)),
        input_output_aliases={0: 0, 1: 1}, name=name, compiler_params=_split_params(),
    )(g, land, send_sems, recv_sems, after)


def _device_sum(land, g, layer, n_layers, prev, name):
    nd, Rh, C = land.shape
    br = _row_block(Rh)
    nb = Rh // br
    core = lambda: lax.axis_index("c")
    me = lambda: 2 * _slot() + core()

    def body(*refs):
        own = refs[nd][...]
        acc = None
        for d in range(nd):
            t = jnp.where(me() == d, own, refs[d][...]).astype(F32)
            acc = t if acc is None else acc + t
        refs[-1][...] = acc

    def piece(d):
        return pl.BlockSpec((None, br, C), lambda i: (jnp.where(me() == d, (d + 1) % nd, d), i, 0))

    ins = [land] * nd + [g] + ([prev] if prev is not None else [])
    return pl.pallas_call(
        body, grid=(nb,),
        in_specs=[piece(d) for d in range(nd)]
        + [pl.BlockSpec((None, br, C), lambda i: (_slot(), core() * nb + i, 0))]
        + ([_ANY] if prev is not None else []),
        out_specs=pl.BlockSpec((None, br, C), lambda i: (layer, core() * nb + i, 0)),
        out_shape=jax.ShapeDtypeStruct((n_layers, 2 * Rh, C), F32),
        input_output_aliases={nd + 1: 0} if prev is not None else {},
        name=name, compiler_params=_cparams(("parallel",)),
    )(*ins)


def _join_copy(src_ref, dst_ref, layer, send_sem, recv_sem, incoming):
    Rh = src_ref.shape[1] // 2
    x, y, c, _ = _place()
    mine, other = pl.ds(c * Rh, Rh), pl.ds((1 - c) * Rh, Rh)
    return pltpu.make_async_remote_copy(src_ref=src_ref.at[layer, mine, :],
                                        dst_ref=dst_ref.at[layer, other if incoming else mine, :],
                                        send_sem=send_sem, recv_sem=recv_sem, device_id=(x, y, 1 - c),
                                        device_id_type=MESH)


def _join_start(g, layer, name):
    def body(g_ref, send_sem, recv_sem, out_ref, token):
        _join_copy(g_ref, out_ref, layer, send_sem, recv_sem, incoming=False).start()
        token[...] = jnp.zeros(token.shape, token.dtype)

    return pl.pallas_call(
        body, in_specs=[_HBM], out_specs=(_SEM, _SEM, _HBM, _VMEM),
        out_shape=(pltpu.SemaphoreType.DMA(()), pltpu.SemaphoreType.DMA(()), pltpu.HBM(g.shape, g.dtype), _TOKEN),
        input_output_aliases={0: 2}, name=name, compiler_params=_split_params(),
    )(pltpu.with_memory_space_constraint(g, pltpu.HBM))


def _join_wait(send_sem, recv_sem, g, layer, after, name):
    def body(g_ref, send_sem, recv_sem, after_ref, out_ref):
        _join_copy(g_ref, out_ref, layer, send_sem, recv_sem, incoming=False).wait_send()
        _join_copy(g_ref, out_ref, layer, send_sem, recv_sem, incoming=True).wait_recv()

    return pl.pallas_call(
        body, in_specs=[_HBM, _SEM, _SEM, _ANY], out_specs=_HBM, out_shape=pltpu.HBM(g.shape, g.dtype),
        input_output_aliases={0: 0}, name=name, compiler_params=_split_params(),
    )(g, send_sem, recv_sem, after)


def _all_reduce_small(v, name):
    rows, cols = v.shape

    def body(v_ref, out_ref, buf, send_sems, recv_sems):
        x, y, c, _ = _place()
        me = 4 * x + 2 * y + c
        buf[me] = v_ref[...]
        peers = []
        for k in range(1, N_DEV):
            px, py, pc = (x + (k >> 2)) % 2, (y + ((k >> 1) & 1)) % 2, (c + (k & 1)) % 2
            peers.append((px, py, pc))
        sends = []
        for k, peer in enumerate(peers):
            cp = pltpu.make_async_remote_copy(src_ref=v_ref, dst_ref=buf.at[me], send_sem=send_sems.at[k],
                                              recv_sem=recv_sems.at[k], device_id=peer, device_id_type=MESH)
            cp.start()
            sends.append(cp)
        for k, (px, py, pc) in enumerate(peers):
            pltpu.make_async_remote_copy(src_ref=v_ref, dst_ref=buf.at[4 * px + 2 * py + pc], send_sem=send_sems.at[k],
                                         recv_sem=recv_sems.at[k], device_id=(px, py, pc),
                                         device_id_type=MESH).wait_recv()
        for cp in sends:
            cp.wait_send()
        acc = buf[0]
        for i in range(1, N_DEV):
            acc = acc + buf[i]
        out_ref[...] = acc

    vmem = pl.BlockSpec(memory_space=pltpu.VMEM)
    return pl.pallas_call(
        body, in_specs=[vmem], out_specs=vmem, out_shape=jax.ShapeDtypeStruct((rows, cols), F32),
        scratch_shapes=[pltpu.VMEM((N_DEV, rows, cols), F32), pltpu.SemaphoreType.DMA((N_DEV - 1,)),
                        pltpu.SemaphoreType.DMA((N_DEV - 1,))],
        name=name, compiler_params=pltpu.CompilerParams(),
    )(v)


def _reduce_scatter_sum(started, after, layer, n_layers, prev, tag):
    send_sems, recv_sems, g, land, _ = started
    g, land = _scatter_wait(send_sems, recv_sems, g, land, after, f"rs_wait_{tag}")
    f = _device_sum(land, g, layer, n_layers, prev, f"rs_sum_{tag}")
    return _join_start(f, layer, f"rs_join_start_{tag}")


def _split_w_in(wg, Hf, name):
    ns, D, cols = wg.shape
    a = 3 * Hf * HEAD_DIM
    n6 = ns * cols - Hf
    br = _row_block(D)

    def body(w_ref, w6_ref, wf_ref):
        nat = jnp.concatenate([w_ref[s] for s in range(ns)], axis=1)
        w6_ref[...] = jnp.concatenate([nat[:, :a], nat[:, a + Hf:]], axis=1)
        wf_ref[...] = nat[:, a:a + Hf]

    w6, wf = pl.pallas_call(
        body, grid=(D // br,), in_specs=[pl.BlockSpec((ns, br, cols), lambda i: (0, i, 0))],
        out_specs=[pl.BlockSpec((br, n6), lambda i: (i, 0)), pl.BlockSpec((br, Hf), lambda i: (i, 0))],
        out_shape=[jax.ShapeDtypeStruct((D, n6), wg.dtype), jax.ShapeDtypeStruct((D, Hf), wg.dtype)],
        name=name, compiler_params=_cparams(("parallel",)),
    )(wg)
    return w6, wf.T


def _join_dw_in(dw6, dwf_t, Hf, name):
    D, n6 = dw6.shape
    a = 3 * Hf * HEAD_DIM
    cols = (n6 + Hf) // N_CHIPS
    br = _row_block(D)

    def body(w6_ref, wf_ref, out_ref):
        w6 = w6_ref[...]
        nat = jnp.concatenate([w6[:, :a], wf_ref[...], w6[:, a:]], axis=1)
        for s in range(N_CHIPS):
            out_ref[s] = nat[:, s * cols:(s + 1) * cols]

    return pl.pallas_call(
        body, grid=(D // br,),
        in_specs=[pl.BlockSpec((br, n6), lambda i: (i, 0)), pl.BlockSpec((br, Hf), lambda i: (i, 0))],
        out_specs=pl.BlockSpec((N_CHIPS, br, cols), lambda i: (0, i, 0)),
        out_shape=jax.ShapeDtypeStruct((N_CHIPS, D, cols), dw6.dtype),
        name=name, compiler_params=_cparams(("parallel",)),
    )(dw6, dwf_t.T.astype(dw6.dtype))


def _tied(v, tokens, name):
    return _tie(v, tokens, name) if tokens else v


def _layer_fwd(x, p, weight, bias, tokens, tag):
    Hf, Hd = p["forget_b"].shape[0], bias.shape[1]
    h1 = _rms_fwd(x, _tied(p["norm1_g"], tokens, f"tie_norm1_{tag}"), f"norm1_{tag}")
    w6, wf_t = _split_w_in(weight("w_in", h1), Hf, f"split_w_in_{tag}")
    n_a = 3 * Hf * HEAD_DIM
    proj_a = _mm_nn(h1, w6, f"proj_a_{tag}", [CDT], epi=lambda acc: (acc,), b_cols=(0, n_a))[0]
    proj_b = _mm_nn(h1, w6, f"proj_b_{tag}", [F32], b_cols=(n_a, w6.shape[1] - n_a))[0]
    f_t = _mm_nt(wf_t, h1, f"fproj_{tag}", [F32])[0]
    qc, kc = _fox_bias_operands(_gates_fwd(f_t, p["forget_b"], f"gates_{tag}"), f"fox_operands_{tag}")
    y_a, lse_a = _fox_fwd(proj_a, qc, kc, Hf, f"fox_{tag}")
    y_b, lse_b = _dil_fwd(proj_b, bias, Hd, f"dil_{tag}")
    mixed = _pair_norm_fwd(y_a, y_b, p["outnorm_a_g"], p["outnorm_b_g"], f"norm_ab_{tag}")
    w_out = weight("w_out", mixed)
    w_out = w_out.reshape(-1, w_out.shape[2])
    x1 = _mm_nn(mixed, w_out, f"attn_out_{tag}", [F32], extras=[x])[0]
    h2 = _rms_fwd(x1, p["norm2_g"], f"norm2_{tag}")
    w_mi = weight("w_mlp_in", h2)
    u, act = _mm_nn(h2, w_mi, f"mlp_in_{tag}", [CDT, CDT], b_slots=True,
                    epi=lambda acc: (acc, jnp.square(jnp.maximum(acc, 0.0))))
    w_mo = weight("w_mlp_out", act)
    w_mo = w_mo.reshape(-1, w_mo.shape[2])
    x2 = _mm_nn(act, w_mo, f"mlp_out_{tag}", [F32], extras=[x1])[0]
    saved = dict(x=x, h1=h1, proj_a=proj_a, proj_b=proj_b, f_t=f_t, qc=qc, kc=kc, y_a=y_a, lse_a=lse_a, y_b=y_b,
                 lse_b=lse_b, mixed=mixed, x1=x1, h2=h2, u=u, act=act, w6=w6, wf_t=wf_t, w_out=w_out, w_mi=w_mi,
                 w_mo=w_mo)
    return x2, saved


def _layer_bwd(dx2, dx2c, p, send, bias, sv, defer_w_out, tag):
    Hf, Hd = p["forget_b"].shape[0], bias.shape[1]
    E = HEAD_DIM
    rows = lambda g: g.reshape(N_CHIPS, -1, g.shape[1])
    du = _mm_nt(dx2c, sv["w_mo"], f"d_act_{tag}", [CDT], extras=[sv["u"]],
                epi=lambda acc, u: (acc * (2.0 * jnp.maximum(u.astype(F32), 0.0)),))[0]
    tokens = send("w_mlp_out", rows(_mm_tn(sv["act"], dx2c, f"dw_mlp_out_{tag}", CDT)))
    dh2 = _mm_nt(du, sv["w_mi"], f"d_h2_{tag}", [F32], b_slots=True)[0]
    tokens = tokens + send("w_mlp_in", _mm_tn(sv["h2"], du, f"dw_mlp_in_{tag}", CDT, out_slots=N_CHIPS))
    dx1, dx1c, g_norm2 = _rms_bwd(sv["x1"], _tied(p["norm2_g"], tokens, f"tie_norm2_{tag}"), dh2, dx2,
                                  f"d_norm2_{tag}")
    dmixed = _mm_nt(dx1c, sv["w_out"], f"d_mixed_{tag}", [F32])[0]
    send_w_out = lambda: send("w_out", rows(_mm_tn(sv["mixed"], dx1c, f"dw_out_{tag}", CDT)))
    tokens = [] if defer_w_out else send_w_out()
    dy_a, dy_b, g_na, g_nb = _pair_norm_bwd(sv["y_a"], sv["y_b"], _tied(p["outnorm_a_g"], tokens, f"tie_norm_a_{tag}"),
                                            p["outnorm_b_g"], dmixed, f"d_norm_ab_{tag}")
    dq_a, dcq, dk_a, dv_a, dck = _fox_bwd(sv["proj_a"], sv["qc"], sv["kc"], sv["lse_a"], sv["y_a"], dy_a, Hf,
                                          f"fox_bwd_{tag}")
    df, dfc, g_fb = _gates_bwd(sv["f_t"], p["forget_b"], dcq[:, ::E].T, dck.reshape(Hf, -1), f"d_gates_{tag}")
    dq_b, dk_b, dv_b, dbias = _dil_bwd(sv["proj_b"], bias, sv["y_b"], dy_b, sv["lse_b"], Hd, f"dil_bwd_{tag}")
    dproj = jnp.concatenate([dq_a, dk_a, dv_a, dq_b, dk_b, dv_b], axis=1)
    g_w6 = _mm_tn(sv["h1"], dproj, f"dw_in_{tag}", CDT)
    g_wf_t = _mm_nn(dfc, sv["h1"], f"dw_f_{tag}", [F32])[0]
    tokens = send("w_in", _join_dw_in(g_w6, g_wf_t, Hf, f"join_dw_in_{tag}"))
    dh1_f = _mm_tn(dfc, _tied(sv["wf_t"], tokens, f"tie_wf_{tag}"), f"d_h1_f_{tag}", F32)
    dh1 = _mm_nt(dproj, sv["w6"], f"d_h1_{tag}", [F32], extras=[dh1_f])[0]
    dx, dxc, g_norm1 = _rms_bwd(sv["x"], p["norm1_g"], dh1, dx1, f"d_norm1_{tag}")
    grads = dict(norm1_g=g_norm1[0], norm2_g=g_norm2[0], outnorm_a_g=g_na[0], outnorm_b_g=g_nb[0],
                 forget_b=g_fb[:, 0], dbias=dbias)
    return dx, dxc, grads, (send_w_out if defer_w_out else None)


_LAYER_SMALL = ("norm1_g", "forget_b", "outnorm_a_g", "outnorm_b_g", "norm2_g")


def _local_step(x, target, small, weight, send, tokens):
    depth = small["norm1_g"].shape[0]
    buckets = _bucket_table()
    bias = _bias_table(small["rel_bias"], buckets, "bias_table")
    layers, saved = [], []
    for l in range(depth):
        p = {k: small[k][l] for k in _LAYER_SMALL}
        layers.append(p)
        x, sv = _layer_fwd(x, p, functools.partial(weight, l), bias, tokens if l == 0 else [], f"l{l}")
        saved.append(sv)
    dx, dxc, g_final, loss = _loss_bwd(x, small["final_norm_g"], target, "loss")
    layer_grads = [None] * depth
    for l in reversed(range(depth)):
        dx, dxc, layer_grads[l], last = _layer_bwd(dx, dxc, layers[l], functools.partial(send, l), bias, saved[l],
                                                   l == 0, f"l{l}")
    tokens = last()
    dbias = functools.reduce(jnp.add, [g["dbias"] for g in layer_grads])
    g_rel = _bias_table_bwd(dbias, buckets, "d_bias_table")[:, 0, :].T
    small_grads = dict(final_norm_g=g_final[0], rel_bias=g_rel,
                       **{k: jnp.stack([g[k] for g in layer_grads]) for k in _LAYER_SMALL})
    return loss[0, 0], dx, small_grads, tokens


_BIG = ("w_in", "w_out", "w_mlp_in", "w_mlp_out")
_SMALL = ("norm1_g", "forget_b", "rel_bias", "outnorm_a_g", "outnorm_b_g", "norm2_g", "final_norm_g")
_ORDER = ("norm1_g", "w_in", "forget_b", "rel_bias", "outnorm_a_g", "outnorm_b_g", "w_out", "norm2_g", "w_mlp_in",
          "w_mlp_out", "final_norm_g")


def _pack_small(d):
    flat = jnp.concatenate([d[k].reshape(-1) for k in _SMALL])
    rows = -(-flat.shape[0] // (8 * SMALL_COLS)) * 8
    return jnp.pad(flat, (0, rows * SMALL_COLS - flat.shape[0])).reshape(rows, SMALL_COLS)


def _unpack_small(packed, like):
    flat, out, at = packed.reshape(-1), {}, 0
    for k in _SMALL:
        n = like[k].size
        out[k] = flat[at:at + n].reshape(like[k].shape)
        at += n
    return out


def kernel(x, norm1_g, w_in, forget_b, rel_bias, outnorm_a_g, outnorm_b_g, w_out, norm2_g, w_mlp_in, w_mlp_out, final_norm_g, loss_target, m_norm1_g, m_w_in, m_forget_b, m_rel_bias, m_outnorm_a_g, m_outnorm_b_g, m_w_out, m_norm2_g, m_w_mlp_in, m_w_mlp_out, m_final_norm_g, v_norm1_g, v_w_in, v_forget_b, v_rel_bias, v_outnorm_a_g, v_outnorm_b_g, v_w_out, v_norm2_g, v_w_mlp_in, v_w_mlp_out, v_final_norm_g):
    w = dict(norm1_g=norm1_g, w_in=w_in, forget_b=forget_b, rel_bias=rel_bias, outnorm_a_g=outnorm_a_g,
             outnorm_b_g=outnorm_b_g, w_out=w_out, norm2_g=norm2_g, w_mlp_in=w_mlp_in, w_mlp_out=w_mlp_out,
             final_norm_g=final_norm_g)
    m = dict(norm1_g=m_norm1_g, w_in=m_w_in, forget_b=m_forget_b, rel_bias=m_rel_bias, outnorm_a_g=m_outnorm_a_g,
             outnorm_b_g=m_outnorm_b_g, w_out=m_w_out, norm2_g=m_norm2_g, w_mlp_in=m_w_mlp_in,
             w_mlp_out=m_w_mlp_out, final_norm_g=m_final_norm_g)
    v = dict(norm1_g=v_norm1_g, w_in=v_w_in, forget_b=v_forget_b, rel_bias=v_rel_bias, outnorm_a_g=v_outnorm_a_g,
             outnorm_b_g=v_outnorm_b_g, w_out=v_w_out, norm2_g=v_norm2_g, w_mlp_in=v_w_mlp_in,
             w_mlp_out=v_w_mlp_out, final_norm_g=v_final_norm_g)
    depth = w_in.shape[0]
    small = {k: w[k] for k in _SMALL}

    gathers, passed, tokens = {}, {}, []
    for l in range(depth):
        for k in _BIG:
            buf = _cast_into_slot(w[k], l, f"cast_{k}_l{l}")
            start = _relay_start if k == _BIG[0] else _gather_start
            send_sems, recv_sems, buf, token = start(buf, tokens, f"gather_start_{k}_l{l}")
            gathers[l, k], tokens = (send_sems, recv_sems, buf), [token]

    def weight(l, k, after):
        if k == _BIG[-1] and l + 1 < depth:
            passed[l + 1] = _relay_pass(*gathers[l + 1, _BIG[0]], after, f"gather_pass_{_BIG[0]}_l{l + 1}")
        if k != _BIG[0]:
            return _gather_wait(*gathers[l, k], after, f"gather_wait_{k}_l{l}")
        if l not in passed:
            passed[l] = _relay_pass(*gathers[l, k], after, f"gather_pass_{k}_l{l}")
        return _relay_wait(*passed[l], after, f"gather_wait_{k}_l{l}")

    scatters = {}

    def send(l, k, g):
        scatters[l, k] = _scatter_start(g, f"rs_start_{k}_l{l}")
        return [scatters[l, k][4]]

    loss, grad_x, small_grads, tokens = _local_step(x[0], loss_target[0], small, weight, send, tokens)
    loss = lax.psum(loss, ("x", "y", "c"))

    grads, delta, new_m, new_v = {}, {}, {}, {}
    packed = _tied(_pack_small(small_grads), tokens, "tie_small")
    after, seen, joining = packed, {k: 0 for k in _BIG}, None

    def joined(after):
        (l, k), (send_sem, recv_sem, g) = joining
        grads[k] = _join_wait(send_sem, recv_sem, g, l, after, f"rs_join_wait_{k}_l{l}")
        seen[k] += 1
        if seen[k] < depth:
            return after
        shape = w[k].shape
        flat = lambda t: t.reshape(-1, shape[-1])
        outs = _adamw(flat(w[k]), flat(grads[k]), flat(m[k]), flat(v[k]), f"adamw_{k}")
        grads[k], delta[k], new_m[k], new_v[k] = (t.reshape(shape) for t in outs)
        return outs[1]

    for (l, k), started in scatters.items():
        assert joining is None or joining[0][1] != k
        send_sem, recv_sem, g, token = _reduce_scatter_sum(started, after, l, depth, grads.get(k), f"{k}_l{l}")
        if joining is not None:
            after = joined(token)
        joining = ((l, k), (send_sem, recv_sem, g))
    after = joined(after)
    small_sums = _all_reduce_small(_tied(packed, [after], "tie_small_sums"), "small_all_reduce")
    grads.update(_unpack_small(small_sums, small))
    _, d_, m_, v_ = _adamw(_pack_small(small), _pack_small({k: grads[k] for k in _SMALL}),
                           _pack_small({k: m[k] for k in _SMALL}), _pack_small({k: v[k] for k in _SMALL}), "adamw_small")
    delta.update(_unpack_small(d_, small))
    new_m.update(_unpack_small(m_, small))
    new_v.update(_unpack_small(v_, small))

    return (loss, grad_x[None], *[grads[k] for k in _ORDER], *[delta[k] for k in _ORDER],
            *[new_m[k] for k in _ORDER], *[new_v[k] for k in _ORDER])
```

```python
import functools

import jax
import jax.numpy as jnp
from jax import lax
from jax.experimental import pallas as pl
from jax.experimental.pallas import tpu as pltpu

F32 = jnp.float32
CDT = jnp.bfloat16
HEAD_DIM = 128
NORM_EPS = 1e-6
NEG_INF = -1e30
LOG2E = 1.4426950408889634
REL_BUCKETS = 32
REL_MAX_DISTANCE = 2048
DIL_PATTERNS = ((128, 1), (512, 4), (2048, 16))
DIL_BLOCK = 128
ADAM_LR, ADAM_B1, ADAM_B2, ADAM_EPS, ADAM_WD, ADAM_STEP = 0.001, 0.9, 0.999, 1e-08, 0.01, 10
N_CHIPS = 4
N_DEV = 8
VMEM_LIMIT_BYTES = 56 * 1024 * 1024
SMALL_COLS = 1024
MESH = pl.DeviceIdType.MESH


def _cparams(sem=None):
    return pltpu.CompilerParams(dimension_semantics=sem, vmem_limit_bytes=VMEM_LIMIT_BYTES)


def _tile(dim, pref):
    t = min(pref, dim)
    t -= t % 128
    while t >= 128:
        if dim % t == 0:
            return t
        t -= 128
    return dim


def _rowwise(fn, ins, out_dtypes, name, bs=512, consts=()):
    R, C = ins[0].shape
    bs = min(bs, R)
    n_in, n_c = len(ins), len(consts)

    def body(*refs):
        vals = [r[...] for r in refs[:n_in + n_c]]
        res = fn(*vals)
        for o, r in zip(refs[n_in + n_c:], res):
            o[...] = r.astype(o.dtype)

    row = pl.BlockSpec((bs, C), lambda i: (i, 0))
    return pl.pallas_call(
        body, grid=(R // bs,),
        in_specs=[row] * n_in + [pl.BlockSpec((1, c.shape[-1]), lambda i: (0, 0)) for c in consts],
        out_specs=[row] * len(out_dtypes),
        out_shape=[jax.ShapeDtypeStruct((R, C), d) for d in out_dtypes],
        name=name, compiler_params=_cparams(("parallel",)),
    )(*ins, *[c.reshape(1, -1) for c in consts])


def _rms_fwd(x, g, name):
    def fn(xf, gg):
        r = lax.rsqrt(jnp.mean(xf * xf, axis=-1, keepdims=True) + NORM_EPS)
        return ((xf * r) * gg,)
    return _rowwise(fn, [x], [CDT], name, consts=[g])[0]


def _rms_bwd(x, g, dh, dres, name, bs=512):
    S, D = x.shape
    bs = min(bs, S)
    has_res = dres is not None

    def body(*refs):
        x_ref, g_ref, dh_ref = refs[:3]
        dx_ref, dxc_ref, dg_ref = refs[-3:]
        xf = x_ref[...]
        r = lax.rsqrt(jnp.mean(xf * xf, axis=-1, keepdims=True) + NORM_EPS)
        xhat = xf * r
        dh_ = dh_ref[...].astype(F32)
        dxhat = dh_ * g_ref[...]
        dx = r * (dxhat - xhat * jnp.mean(dxhat * xhat, axis=-1, keepdims=True))
        if has_res:
            dx = dx + refs[3][...]
        dx_ref[...] = dx
        dxc_ref[...] = dx.astype(dxc_ref.dtype)
        part = jnp.sum(dh_ * xhat, axis=0, keepdims=True)

        @pl.when(pl.program_id(0) == 0)
        def _():
            dg_ref[...] = part

        @pl.when(pl.program_id(0) > 0)
        def _():
            dg_ref[...] += part

    row = pl.BlockSpec((bs, D), lambda i: (i, 0))
    one = pl.BlockSpec((1, D), lambda i: (0, 0))
    ins = [x, g.reshape(1, D), dh] + ([dres] if has_res else [])
    return pl.pallas_call(
        body, grid=(S // bs,),
        in_specs=[row, one, row] + ([row] if has_res else []),
        out_specs=[row, row, one],
        out_shape=[jax.ShapeDtypeStruct((S, D), F32), jax.ShapeDtypeStruct((S, D), CDT),
                   jax.ShapeDtypeStruct((1, D), F32)],
        name=name, compiler_params=_cparams(("arbitrary",)),
    )(*ins)


def _pair_norm_fwd(y_a, y_b, g_a, g_b, name, bs=512):
    S, Da = y_a.shape
    Db = y_b.shape[1]
    bs = min(bs, S)

    def body(a_ref, b_ref, ga_ref, gb_ref, o_ref):
        def norm(x, g):
            r = lax.rsqrt(jnp.mean(x * x, axis=-1, keepdims=True) + NORM_EPS)
            return ((x * r) * g).astype(o_ref.dtype)
        o_ref[:, :Da] = norm(a_ref[...], ga_ref[...])
        o_ref[:, Da:] = norm(b_ref[...], gb_ref[...])

    row = lambda n: pl.BlockSpec((bs, n), lambda i: (i, 0))
    one = lambda n: pl.BlockSpec((1, n), lambda i: (0, 0))
    return pl.pallas_call(
        body, grid=(S // bs,), in_specs=[row(Da), row(Db), one(Da), one(Db)], out_specs=row(Da + Db),
        out_shape=jax.ShapeDtypeStruct((S, Da + Db), CDT), name=name, compiler_params=_cparams(("parallel",)),
    )(y_a, y_b, g_a.reshape(1, Da), g_b.reshape(1, Db))


def _pair_norm_bwd(y_a, y_b, g_a, g_b, dmixed, name, bs=512):
    S, Da = y_a.shape
    Db = y_b.shape[1]
    bs = min(bs, S)

    def body(a_ref, b_ref, ga_ref, gb_ref, dm_ref, da_ref, db_ref, dga_ref, dgb_ref):
        def one(x_ref, g_ref, dh, dx_ref, dg_ref):
            xf = x_ref[...]
            r = lax.rsqrt(jnp.mean(xf * xf, axis=-1, keepdims=True) + NORM_EPS)
            xhat = xf * r
            dxhat = dh * g_ref[...]
            dx_ref[...] = r * (dxhat - xhat * jnp.mean(dxhat * xhat, axis=-1, keepdims=True))
            part = jnp.sum(dh * xhat, axis=0, keepdims=True)

            @pl.when(pl.program_id(0) == 0)
            def _():
                dg_ref[...] = part

            @pl.when(pl.program_id(0) > 0)
            def _():
                dg_ref[...] += part

        dm = dm_ref[...]
        one(a_ref, ga_ref, dm[:, :Da], da_ref, dga_ref)
        one(b_ref, gb_ref, dm[:, Da:], db_ref, dgb_ref)

    row = lambda n: pl.BlockSpec((bs, n), lambda i: (i, 0))
    one_ = lambda n: pl.BlockSpec((1, n), lambda i: (0, 0))
    return pl.pallas_call(
        body, grid=(S // bs,), in_specs=[row(Da), row(Db), one_(Da), one_(Db), row(Da + Db)],
        out_specs=[row(Da), row(Db), one_(Da), one_(Db)],
        out_shape=[jax.ShapeDtypeStruct((S, Da), F32), jax.ShapeDtypeStruct((S, Db), F32),
                   jax.ShapeDtypeStruct((1, Da), F32), jax.ShapeDtypeStruct((1, Db), F32)],
        name=name, compiler_params=_cparams(("arbitrary",)),
    )(y_a, y_b, g_a.reshape(1, Da), g_b.reshape(1, Db), dmixed)


def _loss_bwd(x, g, target, name, bs=512):
    S, D = x.shape
    bs = min(bs, S)

    def body(x_ref, g_ref, t_ref, dx_ref, dxc_ref, dg_ref, loss_ref):
        xf = x_ref[...]
        r = lax.rsqrt(jnp.mean(xf * xf, axis=-1, keepdims=True) + NORM_EPS)
        xhat = xf * r
        err = xhat * g_ref[...] - t_ref[...]
        lpart = 0.5 * jnp.sum(jnp.mean(err * err, axis=-1, keepdims=True), axis=0, keepdims=True)
        dy = err / D
        dxhat = dy * g_ref[...]
        dx = r * (dxhat - xhat * jnp.mean(dxhat * xhat, axis=-1, keepdims=True))
        dx_ref[...] = dx
        dxc_ref[...] = dx.astype(dxc_ref.dtype)
        gpart = jnp.sum(dy * xhat, axis=0, keepdims=True)

        @pl.when(pl.program_id(0) == 0)
        def _():
            dg_ref[...] = gpart
            loss_ref[...] = lpart

        @pl.when(pl.program_id(0) > 0)
        def _():
            dg_ref[...] += gpart
            loss_ref[...] += lpart

    row = pl.BlockSpec((bs, D), lambda i: (i, 0))
    one = pl.BlockSpec((1, D), lambda i: (0, 0))
    return pl.pallas_call(
        body, grid=(S // bs,),
        in_specs=[row, one, row],
        out_specs=[row, row, one, pl.BlockSpec((1, 1), lambda i: (0, 0))],
        out_shape=[jax.ShapeDtypeStruct((S, D), F32), jax.ShapeDtypeStruct((S, D), CDT),
                   jax.ShapeDtypeStruct((1, D), F32), jax.ShapeDtypeStruct((1, 1), F32)],
        name=name, compiler_params=_cparams(("arbitrary",)),
    )(x, g.reshape(1, D), target)


_NN = (((1,), (0,)), ((), ()))
_NT = (((1,), (1,)), ((), ()))
_TN = (((0,), (0,)), ((), ()))


def _mm(a, b, *, M, N, K, a_spec, b_spec, o_spec, dims, tm, tn, tk, name, out_shapes, extras=(), epi=None):
    nk = K // tk
    n_ex, n_out = len(extras), len(out_shapes)
    in_place = epi is None
    if in_place:
        assert n_out == 1 and n_ex <= 1 and out_shapes[0].dtype == F32
        epi = lambda acc, *r: (acc + r[0] if r else acc,)

    def body(*refs):
        a_ref, b_ref = refs[0], refs[1]
        ex = refs[2:2 + n_ex]
        outs = refs[2 + n_ex:2 + n_ex + n_out]
        part = lax.dot_general(a_ref[...], b_ref[...], dims, preferred_element_type=F32)

        def finish(acc):
            for o, r in zip(outs, epi(acc, *[e[...] for e in ex])):
                o[...] = r.astype(o.dtype)

        if nk == 1:
            finish(part)
        elif in_place:
            k = pl.program_id(2)

            @pl.when(k == 0)
            def _():
                finish(part)

            @pl.when(k > 0)
            def _():
                outs[0][...] += part
        else:
            acc_ref = refs[-1]
            k = pl.program_id(2)

            @pl.when(k == 0)
            def _():
                acc_ref[...] = part

            @pl.when(k > 0)
            def _():
                acc_ref[...] += part

            @pl.when(k == nk - 1)
            def _():
                finish(acc_ref[...])

    ex_spec = pl.BlockSpec((tm, tn), lambda i, j, k: (i, j))
    return pl.pallas_call(
        body, grid=(M // tm, N // tn, nk),
        in_specs=[a_spec, b_spec] + [ex_spec] * n_ex,
        out_specs=[o_spec] * n_out,
        out_shape=out_shapes,
        scratch_shapes=[pltpu.VMEM((tm, tn), F32)] if nk > 1 and not in_place else [],
        name=name, compiler_params=_cparams(("parallel", "parallel", "arbitrary")),
    )(a, b, *extras)


def _mm_tiles(K):
    return (2048, 512, 2048) if K <= 2048 else (1024, 1024, 2048)


def _mm_nn(a, b, name, out_dtypes, extras=(), epi=None, b_slots=False, b_cols=None):
    M, K = a.shape
    tm, tn, tk = _mm_tiles(K)
    if b_slots:
        ns, _, Ns = b.shape
        N = ns * Ns
        tn = _tile(Ns, tn)
        npb = Ns // tn
        tk_ = _tile(K, tk)
        b_spec = pl.BlockSpec((None, tk_, tn), lambda i, j, k: (j // npb, k, j % npb))
    else:
        first, N = b_cols if b_cols is not None else (0, b.shape[1])
        tn = _tile(N, tn)
        assert first % tn == 0
        tk_ = _tile(K, tk)
        b_spec = pl.BlockSpec((tk_, tn), lambda i, j, k: (k, first // tn + j))
    tm = _tile(M, tm)
    return _mm(a, b, M=M, N=N, K=K, a_spec=pl.BlockSpec((tm, tk_), lambda i, j, k: (i, k)), b_spec=b_spec,
               o_spec=pl.BlockSpec((tm, tn), lambda i, j, k: (i, j)), dims=_NN, tm=tm, tn=tn, tk=tk_, name=name,
               out_shapes=[jax.ShapeDtypeStruct((M, N), d) for d in out_dtypes], extras=extras, epi=epi)


def _mm_nt(a, b, name, out_dtypes, extras=(), epi=None, b_slots=False):
    M, K = a.shape
    tm, tn, tk = _mm_tiles(K)
    tm = _tile(M, tm)
    if b_slots:
        ns, N, Ks = b.shape
        tk_ = _tile(Ks, tk)
        kpb = Ks // tk_
        tn = _tile(N, tn)
        b_spec = pl.BlockSpec((None, tn, tk_), lambda i, j, k: (k // kpb, j, k % kpb))
    else:
        N = b.shape[0]
        tk_ = _tile(K, tk)
        tn = _tile(N, tn)
        b_spec = pl.BlockSpec((tn, tk_), lambda i, j, k: (j, k))
    return _mm(a, b, M=M, N=N, K=K, a_spec=pl.BlockSpec((tm, tk_), lambda i, j, k: (i, k)), b_spec=b_spec,
               o_spec=pl.BlockSpec((tm, tn), lambda i, j, k: (i, j)), dims=_NT, tm=tm, tn=tn, tk=tk_, name=name,
               out_shapes=[jax.ShapeDtypeStruct((M, N), d) for d in out_dtypes], extras=extras, epi=epi)


def _mm_tn(a, b, name, out_dtype, out_slots=0, tm=1024, tn=1024, tk=4096):
    K, M = a.shape
    N = b.shape[1]
    tm, tk_ = _tile(M, tm), _tile(K, tk)
    if out_slots:
        Ns = N // out_slots
        tn = _tile(Ns, tn)
        npb = Ns // tn
        o_spec = pl.BlockSpec((None, tm, tn), lambda i, j, k: (j // npb, i, j % npb))
        out_shape = jax.ShapeDtypeStruct((out_slots, M, Ns), out_dtype)
    else:
        tn = _tile(N, tn)
        o_spec = pl.BlockSpec((tm, tn), lambda i, j, k: (i, j))
        out_shape = jax.ShapeDtypeStruct((M, N), out_dtype)
    return _mm(a, b, M=M, N=N, K=K, a_spec=pl.BlockSpec((tk_, tm), lambda i, j, k: (k, i)),
               b_spec=pl.BlockSpec((tk_, tn), lambda i, j, k: (k, j)), o_spec=o_spec, dims=_TN,
               tm=tm, tn=tn, tk=tk_, name=name, out_shapes=[out_shape],
               epi=None if out_dtype == F32 else (lambda acc: (acc,)))[0]


GATE_BLOCK = 512


def _split3(v):
    hi = v.astype(jnp.bfloat16)
    r1 = v - hi.astype(F32)
    mid = r1.astype(jnp.bfloat16)
    lo = (r1 - mid.astype(F32)).astype(jnp.bfloat16)
    return hi, mid, lo


def _exact_dot(v, tri):
    return functools.reduce(jnp.add, [jnp.dot(t, tri, preferred_element_type=F32) for t in _split3(v)])


def _gates_fwd(f_t, b, name):
    H, S = f_t.shape
    nb = _tile(S, GATE_BLOCK)
    inv_scale = HEAD_DIM ** 0.5

    def body(f_ref, b_ref, c_ref):
        upper = (lax.broadcasted_iota(jnp.int32, (nb, nb), 0)
                 <= lax.broadcasted_iota(jnp.int32, (nb, nb), 1)).astype(jnp.bfloat16)
        carry = jnp.zeros((H, 1), F32)
        for i in range(S // nb):
            z = f_ref[:, i * nb:(i + 1) * nb] + b_ref[...]
            logf = jnp.minimum(z, 0.0) - jnp.log1p(jnp.exp(-jnp.abs(z)))
            cs = _exact_dot(logf, upper) + carry
            for j, t in enumerate(_split3(cs * inv_scale)):
                c_ref[j, :, i * nb:(i + 1) * nb] = t
            carry = cs[:, nb - 1:nb]

    return pl.pallas_call(body, out_shape=jax.ShapeDtypeStruct((3, H, S), jnp.bfloat16), name=name,
                          compiler_params=_cparams())(f_t, b.reshape(H, 1))


def _gates_bwd(f_t, b, dcq, dck, name):
    H, S = f_t.shape
    nb = _tile(S, GATE_BLOCK)

    def body(f_ref, b_ref, dcq_ref, dck_ref, df_ref, dfc_ref, db_ref):
        lower = (lax.broadcasted_iota(jnp.int32, (nb, nb), 0)
                 >= lax.broadcasted_iota(jnp.int32, (nb, nb), 1)).astype(jnp.bfloat16)
        carry = jnp.zeros((H, 1), F32)
        db = jnp.zeros((H, 1), F32)
        for i in reversed(range(S // nb)):
            sl = slice(i * nb, (i + 1) * nb)
            dc = dcq_ref[:, sl] - dck_ref[:, sl]
            dlogf = _exact_dot(dc, lower) + carry
            carry = dlogf[:, 0:1]
            z = f_ref[:, sl] + b_ref[...]
            df = dlogf / (1.0 + jnp.exp(z))
            df_ref[:, sl] = df
            dfc_ref[:, sl] = df.astype(dfc_ref.dtype)
            db = db + jnp.sum(df, axis=1, keepdims=True)
        db_ref[...] = db

    return pl.pallas_call(
        body, out_shape=[jax.ShapeDtypeStruct((H, S), F32), jax.ShapeDtypeStruct((H, S), CDT),
                         jax.ShapeDtypeStruct((H, 1), F32)],
        name=name, compiler_params=_cparams())(f_t, b.reshape(H, 1), dcq, dck)


FOX_BLOCK = 1024
FOX_ROW_PARTS = 2


def _fox_bias_operands(csplit, name, bs=512):
    _, H, S = csplit.shape
    E = HEAD_DIM
    bs = _tile(S, bs)
    part = jnp.arange(3 * H)[:, None] // H
    head = jnp.arange(3 * H)[:, None] % H
    lane = jnp.arange(H * E)[None, :]
    place_q = (lane == head * E + part).astype(csplit.dtype)
    place_k = -(lane == head * E + 3 + part).astype(csplit.dtype)
    ones_q = ((lane % E >= 3) & (lane % E < 6)).astype(F32)
    ones_k = (lane % E < 3).astype(F32)

    def body(c_ref, pq_ref, pk_ref, oq_ref, ok_ref, qc_ref, kc_ref):
        c = c_ref[...]
        qc_ref[...] = (lax.dot_general(c, pq_ref[...], _TN, preferred_element_type=F32) + oq_ref[...]).astype(qc_ref.dtype)
        kc_ref[...] = (lax.dot_general(c, pk_ref[...], _TN, preferred_element_type=F32) + ok_ref[...]).astype(kc_ref.dtype)

    full = lambda a: pl.BlockSpec(a.shape, lambda i: (0, 0))
    out = pl.BlockSpec((bs, H * E), lambda i: (i, 0))
    return pl.pallas_call(
        body, grid=(S // bs,),
        in_specs=[pl.BlockSpec((3 * H, bs), lambda i: (0, i)), full(place_q), full(place_k), full(ones_q), full(ones_k)],
        out_specs=[out, out], out_shape=[jax.ShapeDtypeStruct((S, H * E), csplit.dtype)] * 2,
        name=name, compiler_params=_cparams(("parallel",)),
    )(csplit.reshape(3 * H, S), place_q, place_k, ones_q, ones_k)


def _fox_logits2(q_ref, qc_ref, k_ref, kc_ref, diag):
    q, k = q_ref[...], k_ref[...]
    qa = jnp.concatenate([q, qc_ref[...].astype(q.dtype)], axis=1)
    ka = jnp.concatenate([k, kc_ref[...].astype(k.dtype)], axis=1)
    s = lax.dot_general(qa, ka, _NT, preferred_element_type=F32) * (HEAD_DIM ** -0.5 * LOG2E)
    if diag:
        row = lax.broadcasted_iota(jnp.int32, s.shape, 0)
        col = lax.broadcasted_iota(jnp.int32, s.shape, 1)
        s = jnp.where(col <= row, s, NEG_INF)
    return s


def _fox_fwd(proj, qc, kc, H, name):
    S = proj.shape[0]
    E = HEAD_DIM
    blk = _tile(S, FOX_BLOCK)
    nq = S // blk

    def pair(t):
        qi = sum((t >= i * (i + 1) // 2).astype(jnp.int32) for i in range(1, nq)) if nq > 1 else 0 * t
        return qi, t - qi * (qi + 1) // 2

    def body(q_ref, qc_ref, k_ref, kc_ref, v_ref, o_ref, lse_ref, m_s, l_s, acc_s):
        qi, kj = pair(pl.program_id(1))

        @pl.when(kj == 0)
        def _():
            m_s[...] = jnp.full(m_s.shape, NEG_INF, F32)
            l_s[...] = jnp.zeros(l_s.shape, F32)
            acc_s[...] = jnp.zeros(acc_s.shape, F32)

        def step(diag):
            ka = jnp.concatenate([k_ref[...], kc_ref[...].astype(k_ref.dtype)], axis=1)
            for part in range(FOX_ROW_PARTS):
                rows = slice(part * blk // FOX_ROW_PARTS, (part + 1) * blk // FOX_ROW_PARTS)
                qa = jnp.concatenate([q_ref[rows, :], qc_ref[rows, :].astype(q_ref.dtype)], axis=1)
                s = lax.dot_general(qa, ka, _NT, preferred_element_type=F32) * (HEAD_DIM ** -0.5 * LOG2E)
                if diag:
                    row = rows.start + lax.broadcasted_iota(jnp.int32, s.shape, 0)
                    col = lax.broadcasted_iota(jnp.int32, s.shape, 1)
                    s = jnp.where(col <= row, s, NEG_INF)
                m_prev = m_s[rows, :]
                m_new = jnp.maximum(m_prev, jnp.max(s, axis=-1, keepdims=True))
                alpha = jnp.exp2(m_prev - m_new)
                p = jnp.exp2(s - m_new)
                l_s[rows, :] = alpha * l_s[rows, :] + jnp.sum(p, axis=-1, keepdims=True)
                acc_s[rows, :] = alpha * acc_s[rows, :] + jnp.dot(p.astype(CDT), v_ref[...],
                                                                  preferred_element_type=F32)
                m_s[rows, :] = m_new

        pl.when(kj < qi)(lambda: step(False))
        pl.when(kj == qi)(lambda: step(True))

        @pl.when(kj == qi)
        def _():
            o_ref[...] = acc_s[...] / l_s[...]
            lse_ref[...] = jnp.broadcast_to(m_s[...] + jnp.log2(l_s[...]), lse_ref.shape)

    qspec = lambda off: pl.BlockSpec((blk, E), lambda h, t: (pair(t)[0], off + h))
    kspec = lambda off: pl.BlockSpec((blk, E), lambda h, t: (pair(t)[1], off + h))
    return pl.pallas_call(
        body, grid=(H, nq * (nq + 1) // 2),
        in_specs=[qspec(0), qspec(0), kspec(H), kspec(0), kspec(2 * H)],
        out_specs=[qspec(0)] * 2,
        out_shape=[jax.ShapeDtypeStruct((S, H * E), F32)] * 2,
        scratch_shapes=[pltpu.VMEM((blk, 1), F32), pltpu.VMEM((blk, 1), F32), pltpu.VMEM((blk, E), F32)],
        name=name, compiler_params=_cparams(("parallel", "arbitrary")),
    )(proj, qc, proj, kc, proj)


def _fox_bwd(proj, qc, kc, lse, o, do, H, name):
    S = proj.shape[0]
    E = HEAD_DIM
    blk = _tile(S, FOX_BLOCK)
    nq = S // blk
    scale = E ** -0.5

    def pair(t):
        first = lambda j: j * nq - j * (j - 1) // 2
        kj = sum((t >= first(j)).astype(jnp.int32) for j in range(1, nq)) if nq > 1 else 0 * t
        return kj, kj + t - first(kj)

    def body(q_ref, qc_ref, k_ref, kc_ref, v_ref, lse_ref, o_ref, do_ref,
             dq_ref, dcq_ref, dk_ref, dv_ref, dck_ref, dq_s, dcq_s, dk_s, dv_s, dck_s):
        kj, qi = pair(pl.program_id(1))

        @pl.when(qi == kj)
        def _():
            dk_s[...] = jnp.zeros(dk_s.shape, F32)
            dv_s[...] = jnp.zeros(dv_s.shape, F32)
            dck_s[...] = jnp.zeros(dck_s.shape, F32)

        def step(diag):
            do = do_ref[...]
            doc = do.astype(CDT)
            delta = jnp.sum(do * o_ref[...], axis=-1, keepdims=True)
            p = jnp.exp2(_fox_logits2(q_ref, qc_ref, k_ref, kc_ref, diag) - lse_ref[:, 0:1])
            dp = lax.dot_general(doc, v_ref[...], _NT, preferred_element_type=F32)
            ds = p * (dp - delta)
            dss = ds * scale
            dck_s[...] += jnp.sum(ds, axis=0, keepdims=True)
            dv_s[...] += jnp.dot(p.T.astype(CDT), doc, preferred_element_type=F32)
            dk_s[...] += jnp.dot(dss.T.astype(CDT), q_ref[...], preferred_element_type=F32)
            dq_part = jnp.dot(dss.astype(CDT), k_ref[...], preferred_element_type=F32)
            dc_part = jnp.sum(ds, axis=-1, keepdims=True)
            rows = pl.ds(pl.multiple_of(qi * blk, blk), blk)

            @pl.when(kj == 0)
            def _():
                dq_s[rows, :] = dq_part
                dcq_s[rows, :] = dc_part

            @pl.when(kj > 0)
            def _():
                dq_s[rows, :] += dq_part
                dcq_s[rows, :] += dc_part

        pl.when(qi > kj)(lambda: step(False))
        pl.when(qi == kj)(lambda: step(True))

        @pl.when(qi == nq - 1)
        def _():
            dk_ref[...] = dk_s[...].astype(dk_ref.dtype)
            dv_ref[...] = dv_s[...].astype(dv_ref.dtype)
            dck_ref[...] = dck_s[...].reshape(dck_ref.shape)

        @pl.when((qi == nq - 1) & (kj == nq - 1))
        def _():
            dq_ref[...] = dq_s[...].astype(dq_ref.dtype)
            dcq_ref[...] = jnp.broadcast_to(dcq_s[...], dcq_ref.shape)

    qspec = lambda off: pl.BlockSpec((blk, E), lambda h, t: (pair(t)[1], off + h))
    kspec = lambda off: pl.BlockSpec((blk, E), lambda h, t: (pair(t)[0], off + h))
    head = pl.BlockSpec((S, E), lambda h, t: (0, h))
    return pl.pallas_call(
        body, grid=(H, nq * (nq + 1) // 2),
        in_specs=[qspec(0), qspec(0), kspec(H), kspec(0), kspec(2 * H), qspec(0), qspec(0), qspec(0)],
        out_specs=[head, head, kspec(0), kspec(0), pl.BlockSpec((1, 1, blk), lambda h, t: (h, 0, pair(t)[0]))],
        out_shape=[jax.ShapeDtypeStruct((S, H * E), CDT), jax.ShapeDtypeStruct((S, H * E), F32),
                   jax.ShapeDtypeStruct((S, H * E), CDT), jax.ShapeDtypeStruct((S, H * E), CDT),
                   jax.ShapeDtypeStruct((H, 1, S), F32)],
        scratch_shapes=[pltpu.VMEM((S, E), F32), pltpu.VMEM((S, 1), F32), pltpu.VMEM((blk, E), F32),
                        pltpu.VMEM((blk, E), F32), pltpu.VMEM((1, blk), F32)],
        name=name, compiler_params=_cparams(("parallel", "arbitrary")),
    )(proj, qc, proj, kc, proj, lse, o, do)


DIL_SLAB = 16 * DIL_BLOCK
DIL_UNROLL = 16


def _rel_bucket(dist):
    max_exact = REL_BUCKETS // 2
    d = jnp.maximum(dist.astype(F32), 1.0)
    large = max_exact + (jnp.log(d / max_exact) / jnp.log(jnp.float32(REL_MAX_DISTANCE / max_exact))
                         * (REL_BUCKETS - max_exact)).astype(jnp.int32)
    large = jnp.minimum(large, REL_BUCKETS - 1)
    return jnp.where(dist < max_exact, dist, large)


def _bucket_table():
    i = jnp.arange(DIL_BLOCK)[:, None]
    j = jnp.arange(2 * DIL_BLOCK)[None, :]
    rel = DIL_BLOCK + i - j
    tabs = [_rel_bucket(jnp.clip(rel, 0, w // d) * d) for w, d in DIL_PATTERNS]
    return jnp.stack(tabs).astype(jnp.int32)


def _bias_table(rel_bias, buckets, name):
    P = buckets.shape[0]
    H = rel_bias.shape[1]

    def body(rb_ref, bk_ref, out_ref):
        h = pl.program_id(1)
        bk = bk_ref[0]
        val = jnp.zeros(bk.shape, F32)
        for b in range(REL_BUCKETS):
            val = jnp.where(bk == b, rb_ref[b, h], val)
        out_ref[0, 0] = val

    return pl.pallas_call(
        body, grid=(P, H),
        in_specs=[pl.BlockSpec(memory_space=pltpu.SMEM),
                  pl.BlockSpec((1, DIL_BLOCK, 2 * DIL_BLOCK), lambda p, h: (p, 0, 0))],
        out_specs=pl.BlockSpec((1, 1, DIL_BLOCK, 2 * DIL_BLOCK), lambda p, h: (p, h, 0, 0)),
        out_shape=jax.ShapeDtypeStruct((P, H, DIL_BLOCK, 2 * DIL_BLOCK), F32),
        name=name, compiler_params=_cparams(("parallel", "parallel")),
    )(rel_bias, buckets)


def _bias_table_bwd(dbias, buckets, name):
    P, H = dbias.shape[:2]

    def body(db_ref, bk_ref, out_ref):
        lane = lax.broadcasted_iota(jnp.int32, (1, REL_BUCKETS), 1)
        acc = jnp.zeros((1, REL_BUCKETS), F32)
        bk = bk_ref[...]
        db = db_ref[:, 0]
        for b in range(REL_BUCKETS):
            tot = jnp.sum(jnp.where(bk == b, db, 0.0))
            acc = jnp.where(lane == b, tot, acc)
        out_ref[0] = acc

    return pl.pallas_call(
        body, grid=(H,),
        in_specs=[pl.BlockSpec((P, 1, DIL_BLOCK, 2 * DIL_BLOCK), lambda h: (0, h, 0, 0)),
                  pl.BlockSpec((P, DIL_BLOCK, 2 * DIL_BLOCK), lambda h: (0, 0, 0))],
        out_specs=pl.BlockSpec((1, 1, REL_BUCKETS), lambda h: (h, 0, 0)),
        out_shape=jax.ShapeDtypeStruct((H, 1, REL_BUCKETS), F32),
        name=name, compiler_params=_cparams(("parallel",)),
    )(dbias, buckets)


def _bdot(a, b, contract_b):
    return lax.dot_general(a, b, (((2,), (contract_b,)), ((0,), (0,))), preferred_element_type=F32)


def _dil_units(first, d):
    units = []
    for t in range(DIL_UNROLL):
        u = first + t
        sg = u // d
        units.append((sg, sg * (DIL_BLOCK * d) + u % d))
    return units


def _dil_rows(ref, starts, d, dtype=None):
    t = jnp.stack([ref[pl.ds(s, DIL_BLOCK, stride=d), :] for s in starts])
    return t if dtype is None else t.astype(dtype)


def _dil_keys(ref, units, d):
    B, SL = DIL_BLOCK, DIL_SLAB
    return jnp.stack([jnp.concatenate([ref[pl.ds(SL + b - B * d, B, stride=d), :], ref[pl.ds(SL + b, B, stride=d), :]],
                                      axis=0) for _, b in units]).astype(CDT)


def _dil_logits(q, keys, bias_pc, first, d, has_before):
    T, B = q.shape[0], DIL_BLOCK
    ii = lax.broadcasted_iota(jnp.int32, (T, B, 2 * B), 1)
    jj = lax.broadcasted_iota(jnp.int32, (T, B, 2 * B), 2)
    sg = (first + lax.broadcasted_iota(jnp.int32, (T, B, 2 * B), 0)) // d
    mask = (jj >= ii) & (jj <= ii + B) & ((jj >= B) | (sg > 0) | has_before)
    return jnp.where(mask, _bdot(q, keys, 2) * HEAD_DIM ** -0.5 + bias_pc[None], NEG_INF)


def _dil_specs(H):
    E, SL = DIL_BLOCK, DIL_SLAB
    cur = lambda off: pl.BlockSpec((SL, E), lambda h, g: (g, off + h))
    prev = lambda off: pl.BlockSpec((SL, E), lambda h, g: (jnp.maximum(g - 1, 0), off + h))
    bias = pl.BlockSpec((len(DIL_PATTERNS), 1, E, 2 * E), lambda h, g: (0, h, 0, 0))
    return cur, prev, bias


def _dil_fwd(proj, bias, H, name):
    S = proj.shape[0]
    E = B = DIL_BLOCK
    SL = DIL_SLAB
    P = len(DIL_PATTERNS)
    assert S % SL == 0
    n_slabs = S // SL

    def body(q_ref, kc_ref, kp_ref, vc_ref, vp_ref, b_ref, y_ref, lse_ref, kj, vj, o_s, l_s):
        g = pl.program_id(1)
        kj[0:SL, :] = kp_ref[...]
        kj[SL:2 * SL, :] = kc_ref[...]
        vj[0:SL, :] = vp_ref[...]
        vj[SL:2 * SL, :] = vc_ref[...]
        for p, (_, d) in enumerate(DIL_PATTERNS):
            def batch(it, carry, p=p, d=d):
                first = it * DIL_UNROLL
                units = _dil_units(first, d)
                q = _dil_rows(q_ref, [b for _, b in units], d, CDT)
                s = _dil_logits(q, _dil_keys(kj, units, d), b_ref[p, 0], first, d, g > 0)
                m = jnp.max(s, axis=-1, keepdims=True)
                e = jnp.exp(s - m)
                ssum = jnp.sum(e, axis=-1, keepdims=True)
                o = _bdot(e.astype(CDT), _dil_keys(vj, units, d), 1) / ssum
                lse = jnp.broadcast_to(m + jnp.log(ssum), o.shape)
                for t, (_, b) in enumerate(units):
                    o_s[p, pl.ds(b, B, stride=d), :] = o[t]
                    l_s[p, pl.ds(b, B, stride=d), :] = lse[t]
                return carry

            lax.fori_loop(0, SL // B // DIL_UNROLL, batch, 0)
        ls = [l_s[p] for p in range(P)]
        m = functools.reduce(jnp.maximum, ls)
        w = [jnp.exp(l - m) for l in ls]
        tot = functools.reduce(jnp.add, w)
        y_ref[...] = functools.reduce(jnp.add, [(w[p] / tot) * o_s[p] for p in range(P)])
        lse_ref[...] = m + jnp.log(tot)

    cur, prev, bspec = _dil_specs(H)
    return pl.pallas_call(
        body, grid=(H, n_slabs),
        in_specs=[cur(0), cur(H), prev(H), cur(2 * H), prev(2 * H), bspec],
        out_specs=[cur(0), cur(0)],
        out_shape=[jax.ShapeDtypeStruct((S, H * E), F32)] * 2,
        scratch_shapes=[pltpu.VMEM((2 * SL, E), F32), pltpu.VMEM((2 * SL, E), F32),
                        pltpu.VMEM((P, SL, E), F32), pltpu.VMEM((P, SL, E), F32)],
        name=name, compiler_params=_cparams(("parallel", "parallel")),
    )(proj, proj, proj, proj, proj, bias)


def _dil_bwd(proj, bias, y, dy, lse, H, name):
    S = proj.shape[0]
    E = B = DIL_BLOCK
    SL = DIL_SLAB
    P = len(DIL_PATTERNS)
    assert S % SL == 0
    n_slabs = S // SL
    scale = E ** -0.5

    def body(q_ref, kc_ref, kp_ref, vc_ref, vp_ref, b_ref, y_ref, dy_ref, lse_ref,
             dq_ref, dk_ref, dv_ref, db_ref, kj, vj, dq_s, dk_own, dv_own, dk_held, dv_held, dk_back, dv_back, dl_s):
        g = pl.program_id(1)

        @pl.when(g == 0)
        def _():
            db_ref[...] = jnp.zeros(db_ref.shape, F32)

        @pl.when(g > 0)
        def _():
            dk_held[...] = dk_own[...]
            dv_held[...] = dv_own[...]
            dk_back[...] = jnp.zeros(dk_back.shape, F32)
            dv_back[...] = jnp.zeros(dv_back.shape, F32)

        @pl.when(g < n_slabs)
        def _():
            kj[0:SL, :] = kp_ref[...]
            kj[SL:2 * SL, :] = kc_ref[...]
            vj[0:SL, :] = vp_ref[...]
            vj[SL:2 * SL, :] = vc_ref[...]
            dq_s[...] = jnp.zeros(dq_s.shape, F32)
            dk_own[...] = jnp.zeros(dk_own.shape, F32)
            dv_own[...] = jnp.zeros(dv_own.shape, F32)
            dl_s[...] = jnp.broadcast_to(jnp.sum(dy_ref[...] * y_ref[...], axis=-1, keepdims=True), (SL, E))
            tr = lambda t: jnp.swapaxes(t, 1, 2).astype(CDT)
            for p, (_, d) in enumerate(DIL_PATTERNS):
                def batch(it, carry, p=p, d=d):
                    first = it * DIL_UNROLL
                    units = _dil_units(first, d)
                    starts = [b for _, b in units]
                    q = _dil_rows(q_ref, starts, d, CDT)
                    dyc = _dil_rows(dy_ref, starts, d, CDT)
                    keys, vals = _dil_keys(kj, units, d), _dil_keys(vj, units, d)
                    s = _dil_logits(q, keys, b_ref[p, 0], first, d, g > 0)
                    e = jnp.exp(s - _dil_rows(lse_ref, starts, d)[:, :, 0:1])
                    ds = e * (_bdot(dyc, vals, 2) - _dil_rows(dl_s, starts, d)[:, :, 0:1])
                    dss = ds * scale
                    dq = _bdot(dss.astype(CDT), keys, 1)
                    dk = _bdot(tr(dss), q, 1)
                    dv = _bdot(tr(e), dyc, 1)
                    for t, (sg, b) in enumerate(units):
                        rows = pl.ds(b, B, stride=d)
                        dq_s[rows, :] += dq[t]
                        dk_own[rows, :] += dk[t, B:]
                        dv_own[rows, :] += dv[t, B:]

                        if B * d < SL:
                            @pl.when(sg > 0)
                            def _(t=t, b=b):
                                before = pl.ds(b - B * d, B, stride=d)
                                dk_own[before, :] += dk[t, :B]
                                dv_own[before, :] += dv[t, :B]

                        @pl.when((sg == 0) & (g > 0))
                        def _(t=t, b=b):
                            before = pl.ds(SL + b - B * d, B, stride=d)
                            dk_back[before, :] += dk[t, :B]
                            dv_back[before, :] += dv[t, :B]

                    db_ref[p, 0] += jnp.sum(ds, axis=0)
                    return carry

                lax.fori_loop(0, SL // B // DIL_UNROLL, batch, 0)
            dq_ref[...] = dq_s[...].astype(dq_ref.dtype)

        @pl.when(g > 0)
        def _():
            dk_ref[...] = (dk_held[...] + dk_back[...]).astype(dk_ref.dtype)
            dv_ref[...] = (dv_held[...] + dv_back[...]).astype(dv_ref.dtype)

    last = n_slabs - 1
    cur = lambda off: pl.BlockSpec((SL, E), lambda h, g: (jnp.minimum(g, last), off + h))
    prev = lambda off: pl.BlockSpec((SL, E), lambda h, g: (jnp.maximum(jnp.minimum(g, last) - 1, 0), off + h))
    late = pl.BlockSpec((SL, E), lambda h, g: (jnp.maximum(g - 1, 0), h))
    bspec = pl.BlockSpec((P, 1, B, 2 * B), lambda h, g: (0, h, 0, 0))
    slab = pltpu.VMEM((SL, E), F32)
    return pl.pallas_call(
        body, grid=(H, n_slabs + 1),
        in_specs=[cur(0), cur(H), prev(H), cur(2 * H), prev(2 * H), bspec, cur(0), cur(0), cur(0)],
        out_specs=[cur(0), late, late, bspec],
        out_shape=[jax.ShapeDtypeStruct((S, H * E), CDT)] * 3 + [jax.ShapeDtypeStruct((P, H, B, 2 * B), F32)],
        scratch_shapes=[pltpu.VMEM((2 * SL, E), F32), pltpu.VMEM((2 * SL, E), F32)] + [slab] * 8,
        name=name, compiler_params=_cparams(("parallel", "arbitrary")),
    )(proj, proj, proj, proj, proj, bias, y, dy, lse)


def _adamw_tile(g_, w_ref, m_ref, v_ref, g_out, d_ref, nm_ref, nv_ref):
    g_out[...] = g_
    m_ = ADAM_B1 * m_ref[...] + (1.0 - ADAM_B1) * g_
    v_ = ADAM_B2 * v_ref[...] + (1.0 - ADAM_B2) * jnp.square(g_)
    m_hat = m_ / (1.0 - ADAM_B1 ** ADAM_STEP)
    v_hat = v_ / (1.0 - ADAM_B2 ** ADAM_STEP)
    d_ref[...] = -ADAM_LR * (m_hat / (jnp.sqrt(v_hat) + ADAM_EPS) + ADAM_WD * w_ref[...])
    nm_ref[...] = m_
    nv_ref[...] = v_


def _adamw(w, g, m, v, name, br=256):
    R, C = w.shape
    br = br if R % br == 0 else R

    def body(w_ref, g_ref, m_ref, v_ref, *outs):
        _adamw_tile(g_ref[...], w_ref, m_ref, v_ref, *outs)

    blk = pl.BlockSpec((br, C), lambda i: (i, 0))
    return pl.pallas_call(
        body, grid=(R // br,), in_specs=[blk] * 4, out_specs=[blk] * 4,
        out_shape=[jax.ShapeDtypeStruct((R, C), F32)] * 4,
        name=name, compiler_params=_cparams(("parallel",)),
    )(w, g, m, v)


_HBM = pl.BlockSpec(memory_space=pltpu.HBM)
_SEM = pl.BlockSpec(memory_space=pltpu.SEMAPHORE)
_ANY = pl.BlockSpec(memory_space=pl.ANY)
_VMEM = pl.BlockSpec(memory_space=pltpu.VMEM)
_TOKEN = jax.ShapeDtypeStruct((8, 128), F32)


def _split_params():
    return pltpu.CompilerParams(has_side_effects=pltpu.SideEffectType.DATAFLOW_SIDE_EFFECTING)


def _place():
    x, y, c = lax.axis_index("x"), lax.axis_index("y"), lax.axis_index("c")
    chips = [(1 - x, y), (x, 1 - y), (1 - x, 1 - y)]
    return x, y, c, chips


def _tie(v, tokens, name):
    flat = v.reshape(1, -1)

    def body(v_ref, *rest):
        rest[-1][...] = v_ref[...]

    return pl.pallas_call(body, in_specs=[_VMEM] + [_ANY] * len(tokens), out_specs=_VMEM,
                          out_shape=jax.ShapeDtypeStruct(flat.shape, flat.dtype), name=name,
                          compiler_params=_cparams())(flat, *tokens).reshape(v.shape)


def _row_block(R, pref=256):
    return _tile(R, pref) if R % 128 == 0 else R


def _slot():
    return 2 * lax.axis_index("x") + lax.axis_index("y")


def _cast_into_slot(w, layer, name):
    _, R, C = w.shape
    br = _row_block(R)

    def body(w_ref, out_ref):
        out_ref[...] = w_ref[...].astype(out_ref.dtype)

    return pl.pallas_call(
        body, grid=(R // br,),
        in_specs=[pl.BlockSpec((None, br, C), lambda i: (layer, i, 0))],
        out_specs=pl.BlockSpec((None, br, C), lambda i: (_slot(), i, 0)),
        out_shape=jax.ShapeDtypeStruct((N_CHIPS, R, C), CDT),
        name=name, compiler_params=_cparams(("parallel",)),
    )(w)


def _gather_copies(src_ref, dst_ref, send_sems, recv_sems, incoming):
    Rh = src_ref.shape[1] // 2
    x, y, c, chips = _place()
    slot = 2 * x + y

    def half(ref, s, hf):
        return ref.at[s, pl.ds(hf * Rh, Rh), :]

    copies = []
    for j, (cx, cy) in enumerate(chips):
        for e in range(2):
            copies.append(pltpu.make_async_remote_copy(
                src_ref=half(src_ref, slot, c), dst_ref=half(dst_ref, 2 * cx + cy, e) if incoming else half(dst_ref, slot, c),
                send_sem=send_sems.at[2 * j + e], recv_sem=recv_sems.at[2 * j + (e if incoming else c)],
                device_id=(cx, cy, e), device_id_type=MESH))
    return copies


def _gather_start(buf, after, name):
    n_after = len(after)

    def body(*refs):
        buf_ref = refs[0]
        send_sems, recv_sems, out_ref, token = refs[1 + n_after:]
        for cp in _gather_copies(buf_ref, out_ref, send_sems, recv_sems, incoming=False):
            cp.start()
        token[...] = jnp.zeros(token.shape, token.dtype)

    return pl.pallas_call(
        body, in_specs=[_HBM] + [_ANY] * n_after, out_specs=(_SEM, _SEM, _HBM, _VMEM),
        out_shape=(pltpu.SemaphoreType.DMA((6,)), pltpu.SemaphoreType.DMA((6,)), pltpu.HBM(buf.shape, buf.dtype), _TOKEN),
        input_output_aliases={0: 2}, name=name, compiler_params=_split_params(),
    )(pltpu.with_memory_space_constraint(buf, pltpu.HBM), *after)


def _gather_wait(send_sems, recv_sems, buf, after, name):
    def body(buf_ref, send_sems, recv_sems, after_ref, out_ref):
        for cp in _gather_copies(buf_ref, out_ref, send_sems, recv_sems, incoming=False):
            cp.wait_send()
        for cp in _gather_copies(buf_ref, out_ref, send_sems, recv_sems, incoming=True):
            cp.wait_recv()

    return pl.pallas_call(
        body, in_specs=[_HBM, _SEM, _SEM, _ANY], out_specs=_HBM, out_shape=pltpu.HBM(buf.shape, buf.dtype),
        input_output_aliases={0: 0}, name=name, compiler_params=_split_params(),
    )(buf, send_sems, recv_sems, after)


def _relay_copies(src_ref, dst_ref, send_sems, recv_sems, stage, incoming):
    Rh = src_ref.shape[1] // 2
    x, y, c, chips = _place()
    copies = []
    for j, (cx, cy) in enumerate(chips):
        if stage == 0:
            src_slot, src_half, peer = 2 * x + y, c, (cx, cy, c)
            dst_slot, dst_half = (2 * cx + cy, c) if incoming else (src_slot, c)
        else:
            src_slot, src_half, peer = 2 * cx + cy, c, (x, y, 1 - c)
            dst_slot, dst_half = src_slot, (1 - c if incoming else c)
        copies.append(pltpu.make_async_remote_copy(
            src_ref=src_ref.at[src_slot, pl.ds(src_half * Rh, Rh), :],
            dst_ref=dst_ref.at[dst_slot, pl.ds(dst_half * Rh, Rh), :],
            send_sem=send_sems.at[j], recv_sem=recv_sems.at[j], device_id=peer, device_id_type=MESH))
    return copies


def _relay_start(buf, after, name):
    n_after = len(after)

    def body(*refs):
        buf_ref = refs[0]
        send_sems, recv_sems, out_ref, token = refs[1 + n_after:]
        for cp in _relay_copies(buf_ref, out_ref, send_sems, recv_sems, 0, incoming=False):
            cp.start()
        token[...] = jnp.zeros(token.shape, token.dtype)

    return pl.pallas_call(
        body, in_specs=[_HBM] + [_ANY] * n_after, out_specs=(_SEM, _SEM, _HBM, _VMEM),
        out_shape=(pltpu.SemaphoreType.DMA((3,)), pltpu.SemaphoreType.DMA((3,)), pltpu.HBM(buf.shape, buf.dtype), _TOKEN),
        input_output_aliases={0: 2}, name=name, compiler_params=_split_params(),
    )(pltpu.with_memory_space_constraint(buf, pltpu.HBM), *after)


def _relay_pass(send_sems, recv_sems, buf, after, name):
    def body(buf_ref, send0, recv0, after_ref, send1, recv1, out_ref):
        for cp in _relay_copies(buf_ref, out_ref, send0, recv0, 0, incoming=False):
            cp.wait_send()
        for cp in _relay_copies(buf_ref, out_ref, send0, recv0, 0, incoming=True):
            cp.wait_recv()
        for cp in _relay_copies(out_ref, out_ref, send1, recv1, 1, incoming=False):
            cp.start()

    return pl.pallas_call(
        body, in_specs=[_HBM, _SEM, _SEM, _ANY], out_specs=(_SEM, _SEM, _HBM),
        out_shape=(pltpu.SemaphoreType.DMA((3,)), pltpu.SemaphoreType.DMA((3,)), pltpu.HBM(buf.shape, buf.dtype)),
        input_output_aliases={0: 2}, name=name, compiler_params=_split_params(),
    )(buf, send_sems, recv_sems, after)


def _relay_wait(send_sems, recv_sems, buf, after, name):
    def body(buf_ref, send1, recv1, after_ref, out_ref):
        for cp in _relay_copies(buf_ref, out_ref, send1, recv1, 1, incoming=False):
            cp.wait_send()
        for cp in _relay_copies(buf_ref, out_ref, send1, recv1, 1, incoming=True):
            cp.wait_recv()

    return pl.pallas_call(
        body, in_specs=[_HBM, _SEM, _SEM, _ANY], out_specs=_HBM, out_shape=pltpu.HBM(buf.shape, buf.dtype),
        input_output_aliases={0: 0}, name=name, compiler_params=_split_params(),
    )(buf, send_sems, recv_sems, after)


def _scatter_copies(g_ref, land_ref, send_sems, recv_sems, incoming):
    Rh = g_ref.shape[1] // 2
    x, y, c, _ = _place()
    me = 4 * x + 2 * y + c
    copies = []
    for k in range(1, N_DEV):
        px, py, pc = (x + (k >> 2)) % 2, (y + ((k >> 1) & 1)) % 2, (c + (k & 1)) % 2
        copies.append(pltpu.make_async_remote_copy(
            src_ref=g_ref.at[2 * px + py, pl.ds(pc * Rh, Rh), :],
            dst_ref=land_ref.at[4 * px + 2 * py + pc if incoming else me],
            send_sem=send_sems.at[k - 1], recv_sem=recv_sems.at[k - 1], device_id=(px, py, pc), device_id_type=MESH))
    return copies


def _scatter_start(g, name):
    ns, R, C = g.shape

    def body(g_ref, land_ref, send_sems, recv_sems, g_thru, land_thru, token):
        for cp in _scatter_copies(g_ref, land_thru, send_sems, recv_sems, incoming=False):
            cp.start()
        token[...] = jnp.zeros(token.shape, token.dtype)

    land = lax.empty((N_DEV, R // 2, C), g.dtype)
    n = N_DEV - 1
    return pl.pallas_call(
        body, in_specs=[_HBM, _HBM], out_specs=(_SEM, _SEM, _HBM, _HBM, _VMEM),
        out_shape=(pltpu.SemaphoreType.DMA((n,)), pltpu.SemaphoreType.DMA((n,)), pltpu.HBM(g.shape, g.dtype),
                   pltpu.HBM(land.shape, land.dtype), _TOKEN),
        input_output_aliases={0: 2, 1: 3}, name=name, compiler_params=_split_params(),
    )(pltpu.with_memory_space_constraint(g, pltpu.HBM), pltpu.with_memory_space_constraint(land, pltpu.HBM))


def _scatter_wait(send_sems, recv_sems, g, land, after, name):
    def body(g_ref, land_ref, send_sems, recv_sems, after_ref, g_out, land_out):
        for cp in _scatter_copies(g_ref, land_out, send_sems, recv_sems, incoming=False):
            cp.wait_send()
        for cp in _scatter_copies(g_ref, land_out, send_sems, recv_sems, incoming=True):
            cp.wait_recv()

    return pl.pallas_call(
        body, in_specs=[_HBM, _HBM, _SEM, _SEM, _ANY], out_specs=(_HBM, _HBM),
        out_shape=(pltpu.HBM(g.shape, g.dtype), pltpu.HBM(land.shape, land.dtype)),
        input_output_aliases={0: 0, 1: 1}, name=name, compiler_params=_split_params(),
    )(g, land, send_sems, recv_sems, after)


def _device_sum(land, g, layer, n_layers, prev, name):
    nd, Rh, C = land.shape
    br = _row_block(Rh)
    nb = Rh // br
    core = lambda: lax.axis_index("c")
    me = lambda: 2 * _slot() + core()

    def body(*refs):
        own = refs[nd][...]
        acc = None
        for d in range(nd):
            t = jnp.where(me() == d, own, refs[d][...]).astype(F32)
            acc = t if acc is None else acc + t
        refs[-1][...] = acc

    def piece(d):
        return pl.BlockSpec((None, br, C), lambda i: (jnp.where(me() == d, (d + 1) % nd, d), i, 0))

    ins = [land] * nd + [g] + ([prev] if prev is not None else [])
    return pl.pallas_call(
        body, grid=(nb,),
        in_specs=[piece(d) for d in range(nd)]
        + [pl.BlockSpec((None, br, C), lambda i: (_slot(), core() * nb + i, 0))]
        + ([_ANY] if prev is not None else []),
        out_specs=pl.BlockSpec((None, br, C), lambda i: (layer, core() * nb + i, 0)),
        out_shape=jax.ShapeDtypeStruct((n_layers, 2 * Rh, C), F32),
        input_output_aliases={nd + 1: 0} if prev is not None else {},
        name=name, compiler_params=_cparams(("parallel",)),
    )(*ins)


def _join_copy(src_ref, dst_ref, layer, send_sem, recv_sem, incoming):
    Rh = src_ref.shape[1] // 2
    x, y, c, _ = _place()
    mine, other = pl.ds(c * Rh, Rh), pl.ds((1 - c) * Rh, Rh)
    return pltpu.make_async_remote_copy(src_ref=src_ref.at[layer, mine, :],
                                        dst_ref=dst_ref.at[layer, other if incoming else mine, :],
                                        send_sem=send_sem, recv_sem=recv_sem, device_id=(x, y, 1 - c),
                                        device_id_type=MESH)


def _join_start(g, layer, name):
    def body(g_ref, send_sem, recv_sem, out_ref, token):
        _join_copy(g_ref, out_ref, layer, send_sem, recv_sem, incoming=False).start()
        token[...] = jnp.zeros(token.shape, token.dtype)

    return pl.pallas_call(
        body, in_specs=[_HBM], out_specs=(_SEM, _SEM, _HBM, _VMEM),
        out_shape=(pltpu.SemaphoreType.DMA(()), pltpu.SemaphoreType.DMA(()), pltpu.HBM(g.shape, g.dtype), _TOKEN),
        input_output_aliases={0: 2}, name=name, compiler_params=_split_params(),
    )(pltpu.with_memory_space_constraint(g, pltpu.HBM))


def _join_wait(send_sem, recv_sem, g, layer, after, name):
    def body(g_ref, send_sem, recv_sem, after_ref, out_ref):
        _join_copy(g_ref, out_ref, layer, send_sem, recv_sem, incoming=False).wait_send()
        _join_copy(g_ref, out_ref, layer, send_sem, recv_sem, incoming=True).wait_recv()

    return pl.pallas_call(
        body, in_specs=[_HBM, _SEM, _SEM, _ANY], out_specs=_HBM, out_shape=pltpu.HBM(g.shape, g.dtype),
        input_output_aliases={0: 0}, name=name, compiler_params=_split_params(),
    )(g, send_sem, recv_sem, after)


def _all_reduce_small(v, name):
    rows, cols = v.shape

    def body(v_ref, out_ref, buf, send_sems, recv_sems):
        x, y, c, _ = _place()
        me = 4 * x + 2 * y + c
        buf[me] = v_ref[...]
        peers = []
        for k in range(1, N_DEV):
            px, py, pc = (x + (k >> 2)) % 2, (y + ((k >> 1) & 1)) % 2, (c + (k & 1)) % 2
            peers.append((px, py, pc))
        sends = []
        for k, peer in enumerate(peers):
            cp = pltpu.make_async_remote_copy(src_ref=v_ref, dst_ref=buf.at[me], send_sem=send_sems.at[k],
                                              recv_sem=recv_sems.at[k], device_id=peer, device_id_type=MESH)
            cp.start()
            sends.append(cp)
        for k, (px, py, pc) in enumerate(peers):
            pltpu.make_async_remote_copy(src_ref=v_ref, dst_ref=buf.at[4 * px + 2 * py + pc], send_sem=send_sems.at[k],
                                         recv_sem=recv_sems.at[k], device_id=(px, py, pc),
                                         device_id_type=MESH).wait_recv()
        for cp in sends:
            cp.wait_send()
        acc = buf[0]
        for i in range(1, N_DEV):
            acc = acc + buf[i]
        out_ref[...] = acc

    vmem = pl.BlockSpec(memory_space=pltpu.VMEM)
    return pl.pallas_call(
        body, in_specs=[vmem], out_specs=vmem, out_shape=jax.ShapeDtypeStruct((rows, cols), F32),
        scratch_shapes=[pltpu.VMEM((N_DEV, rows, cols), F32), pltpu.SemaphoreType.DMA((N_DEV - 1,)),
                        pltpu.SemaphoreType.DMA((N_DEV - 1,))],
        name=name, compiler_params=pltpu.CompilerParams(),
    )(v)


def _reduce_scatter_sum(started, after, layer, n_layers, prev, tag):
    send_sems, recv_sems, g, land, _ = started
    g, land = _scatter_wait(send_sems, recv_sems, g, land, after, f"rs_wait_{tag}")
    f = _device_sum(land, g, layer, n_layers, prev, f"rs_sum_{tag}")
    return _join_start(f, layer, f"rs_join_start_{tag}")


def _split_w_in(wg, Hf, name):
    ns, D, cols = wg.shape
    a = 3 * Hf * HEAD_DIM
    n6 = ns * cols - Hf
    br = _row_block(D)

    def body(w_ref, w6_ref, wf_ref):
        nat = jnp.concatenate([w_ref[s] for s in range(ns)], axis=1)
        w6_ref[...] = jnp.concatenate([nat[:, :a], nat[:, a + Hf:]], axis=1)
        wf_ref[...] = nat[:, a:a + Hf]

    w6, wf = pl.pallas_call(
        body, grid=(D // br,), in_specs=[pl.BlockSpec((ns, br, cols), lambda i: (0, i, 0))],
        out_specs=[pl.BlockSpec((br, n6), lambda i: (i, 0)), pl.BlockSpec((br, Hf), lambda i: (i, 0))],
        out_shape=[jax.ShapeDtypeStruct((D, n6), wg.dtype), jax.ShapeDtypeStruct((D, Hf), wg.dtype)],
        name=name, compiler_params=_cparams(("parallel",)),
    )(wg)
    return w6, wf.T


def _join_dw_in(dw6, dwf_t, Hf, name):
    D, n6 = dw6.shape
    a = 3 * Hf * HEAD_DIM
    cols = (n6 + Hf) // N_CHIPS
    br = _row_block(D)

    def body(w6_ref, wf_ref, out_ref):
        w6 = w6_ref[...]
        nat = jnp.concatenate([w6[:, :a], wf_ref[...], w6[:, a:]], axis=1)
        for s in range(N_CHIPS):
            out_ref[s] = nat[:, s * cols:(s + 1) * cols]

    return pl.pallas_call(
        body, grid=(D // br,),
        in_specs=[pl.BlockSpec((br, n6), lambda i: (i, 0)), pl.BlockSpec((br, Hf), lambda i: (i, 0))],
        out_specs=pl.BlockSpec((N_CHIPS, br, cols), lambda i: (0, i, 0)),
        out_shape=jax.ShapeDtypeStruct((N_CHIPS, D, cols), dw6.dtype),
        name=name, compiler_params=_cparams(("parallel",)),
    )(dw6, dwf_t.T.astype(dw6.dtype))


def _tied(v, tokens, name):
    return _tie(v, tokens, name) if tokens else v


def _layer_fwd(x, p, weight, bias, tokens, tag):
    Hf, Hd = p["forget_b"].shape[0], bias.shape[1]
    h1 = _rms_fwd(x, _tied(p["norm1_g"], tokens, f"tie_norm1_{tag}"), f"norm1_{tag}")
    w6, wf_t = _split_w_in(weight("w_in", h1), Hf, f"split_w_in_{tag}")
    n_a = 3 * Hf * HEAD_DIM
    proj_a = _mm_nn(h1, w6, f"proj_a_{tag}", [CDT], epi=lambda acc: (acc,), b_cols=(0, n_a))[0]
    proj_b = _mm_nn(h1, w6, f"proj_b_{tag}", [F32], b_cols=(n_a, w6.shape[1] - n_a))[0]
    f_t = _mm_nt(wf_t, h1, f"fproj_{tag}", [F32])[0]
    qc, kc = _fox_bias_operands(_gates_fwd(f_t, p["forget_b"], f"gates_{tag}"), f"fox_operands_{tag}")
    y_a, lse_a = _fox_fwd(proj_a, qc, kc, Hf, f"fox_{tag}")
    y_b, lse_b = _dil_fwd(proj_b, bias, Hd, f"dil_{tag}")
    mixed = _pair_norm_fwd(y_a, y_b, p["outnorm_a_g"], p["outnorm_b_g"], f"norm_ab_{tag}")
    w_out = weight("w_out", mixed)
    w_out = w_out.reshape(-1, w_out.shape[2])
    x1 = _mm_nn(mixed, w_out, f"attn_out_{tag}", [F32], extras=[x])[0]
    h2 = _rms_fwd(x1, p["norm2_g"], f"norm2_{tag}")
    w_mi = weight("w_mlp_in", h2)
    u, act = _mm_nn(h2, w_mi, f"mlp_in_{tag}", [CDT, CDT], b_slots=True,
                    epi=lambda acc: (acc, jnp.square(jnp.maximum(acc, 0.0))))
    w_mo = weight("w_mlp_out", act)
    w_mo = w_mo.reshape(-1, w_mo.shape[2])
    x2 = _mm_nn(act, w_mo, f"mlp_out_{tag}", [F32], extras=[x1])[0]
    saved = dict(x=x, h1=h1, proj_a=proj_a, proj_b=proj_b, f_t=f_t, qc=qc, kc=kc, y_a=y_a, lse_a=lse_a, y_b=y_b,
                 lse_b=lse_b, mixed=mixed, x1=x1, h2=h2, u=u, act=act, w6=w6, wf_t=wf_t, w_out=w_out, w_mi=w_mi,
                 w_mo=w_mo)
    return x2, saved


def _layer_bwd(dx2, dx2c, p, send, bias, sv, defer_w_out, tag):
    Hf, Hd = p["forget_b"].shape[0], bias.shape[1]
    E = HEAD_DIM
    rows = lambda g: g.reshape(N_CHIPS, -1, g.shape[1])
    du = _mm_nt(dx2c, sv["w_mo"], f"d_act_{tag}", [CDT], extras=[sv["u"]],
                epi=lambda acc, u: (acc * (2.0 * jnp.maximum(u.astype(F32), 0.0)),))[0]
    tokens = send("w_mlp_out", rows(_mm_tn(sv["act"], dx2c, f"dw_mlp_out_{tag}", CDT)))
    dh2 = _mm_nt(du, sv["w_mi"], f"d_h2_{tag}", [F32], b_slots=True)[0]
    tokens = tokens + send("w_mlp_in", _mm_tn(sv["h2"], du, f"dw_mlp_in_{tag}", CDT, out_slots=N_CHIPS))
    dx1, dx1c, g_norm2 = _rms_bwd(sv["x1"], _tied(p["norm2_g"], tokens, f"tie_norm2_{tag}"), dh2, dx2,
                                  f"d_norm2_{tag}")
    dmixed = _mm_nt(dx1c, sv["w_out"], f"d_mixed_{tag}", [F32])[0]
    send_w_out = lambda: send("w_out", rows(_mm_tn(sv["mixed"], dx1c, f"dw_out_{tag}", CDT)))
    tokens = [] if defer_w_out else send_w_out()
    dy_a, dy_b, g_na, g_nb = _pair_norm_bwd(sv["y_a"], sv["y_b"], _tied(p["outnorm_a_g"], tokens, f"tie_norm_a_{tag}"),
                                            p["outnorm_b_g"], dmixed, f"d_norm_ab_{tag}")
    dq_a, dcq, dk_a, dv_a, dck = _fox_bwd(sv["proj_a"], sv["qc"], sv["kc"], sv["lse_a"], sv["y_a"], dy_a, Hf,
                                          f"fox_bwd_{tag}")
    df, dfc, g_fb = _gates_bwd(sv["f_t"], p["forget_b"], dcq[:, ::E].T, dck.reshape(Hf, -1), f"d_gates_{tag}")
    dq_b, dk_b, dv_b, dbias = _dil_bwd(sv["proj_b"], bias, sv["y_b"], dy_b, sv["lse_b"], Hd, f"dil_bwd_{tag}")
    dproj = jnp.concatenate([dq_a, dk_a, dv_a, dq_b, dk_b, dv_b], axis=1)
    g_w6 = _mm_tn(sv["h1"], dproj, f"dw_in_{tag}", CDT)
    g_wf_t = _mm_nn(dfc, sv["h1"], f"dw_f_{tag}", [F32])[0]
    tokens = send("w_in", _join_dw_in(g_w6, g_wf_t, Hf, f"join_dw_in_{tag}"))
    dh1_f = _mm_tn(dfc, _tied(sv["wf_t"], tokens, f"tie_wf_{tag}"), f"d_h1_f_{tag}", F32)
    dh1 = _mm_nt(dproj, sv["w6"], f"d_h1_{tag}", [F32], extras=[dh1_f])[0]
    dx, dxc, g_norm1 = _rms_bwd(sv["x"], p["norm1_g"], dh1, dx1, f"d_norm1_{tag}")
    grads = dict(norm1_g=g_norm1[0], norm2_g=g_norm2[0], outnorm_a_g=g_na[0], outnorm_b_g=g_nb[0],
                 forget_b=g_fb[:, 0], dbias=dbias)
    return dx, dxc, grads, (send_w_out if defer_w_out else None)


_LAYER_SMALL = ("norm1_g", "forget_b", "outnorm_a_g", "outnorm_b_g", "norm2_g")


def _local_step(x, target, small, weight, send, tokens):
    depth = small["norm1_g"].shape[0]
    buckets = _bucket_table()
    bias = _bias_table(small["rel_bias"], buckets, "bias_table")
    layers, saved = [], []
    for l in range(depth):
        p = {k: small[k][l] for k in _LAYER_SMALL}
        layers.append(p)
        x, sv = _layer_fwd(x, p, functools.partial(weight, l), bias, tokens if l == 0 else [], f"l{l}")
        saved.append(sv)
    dx, dxc, g_final, loss = _loss_bwd(x, small["final_norm_g"], target, "loss")
    layer_grads = [None] * depth
    for l in reversed(range(depth)):
        dx, dxc, layer_grads[l], last = _layer_bwd(dx, dxc, layers[l], functools.partial(send, l), bias, saved[l],
                                                   l == 0, f"l{l}")
    tokens = last()
    dbias = functools.reduce(jnp.add, [g["dbias"] for g in layer_grads])
    g_rel = _bias_table_bwd(dbias, buckets, "d_bias_table")[:, 0, :].T
    small_grads = dict(final_norm_g=g_final[0], rel_bias=g_rel,
                       **{k: jnp.stack([g[k] for g in layer_grads]) for k in _LAYER_SMALL})
    return loss[0, 0], dx, small_grads, tokens


_BIG = ("w_in", "w_out", "w_mlp_in", "w_mlp_out")
_SMALL = ("norm1_g", "forget_b", "rel_bias", "outnorm_a_g", "outnorm_b_g", "norm2_g", "final_norm_g")
_ORDER = ("norm1_g", "w_in", "forget_b", "rel_bias", "outnorm_a_g", "outnorm_b_g", "w_out", "norm2_g", "w_mlp_in",
          "w_mlp_out", "final_norm_g")


def _pack_small(d):
    flat = jnp.concatenate([d[k].reshape(-1) for k in _SMALL])
    rows = -(-flat.shape[0] // (8 * SMALL_COLS)) * 8
    return jnp.pad(flat, (0, rows * SMALL_COLS - flat.shape[0])).reshape(rows, SMALL_COLS)


def _unpack_small(packed, like):
    flat, out, at = packed.reshape(-1), {}, 0
    for k in _SMALL:
        n = like[k].size
        out[k] = flat[at:at + n].reshape(like[k].shape)
        at += n
    return out


def kernel(x, norm1_g, w_in, forget_b, rel_bias, outnorm_a_g, outnorm_b_g, w_out, norm2_g, w_mlp_in, w_mlp_out, final_norm_g, loss_target, m_norm1_g, m_w_in, m_forget_b, m_rel_bias, m_outnorm_a_g, m_outnorm_b_g, m_w_out, m_norm2_g, m_w_mlp_in, m_w_mlp_out, m_final_norm_g, v_norm1_g, v_w_in, v_forget_b, v_rel_bias, v_outnorm_a_g, v_outnorm_b_g, v_w_out, v_norm2_g, v_w_mlp_in, v_w_mlp_out, v_final_norm_g):
    w = dict(norm1_g=norm1_g, w_in=w_in, forget_b=forget_b, rel_bias=rel_bias, outnorm_a_g=outnorm_a_g,
             outnorm_b_g=outnorm_b_g, w_out=w_out, norm2_g=norm2_g, w_mlp_in=w_mlp_in, w_mlp_out=w_mlp_out,
             final_norm_g=final_norm_g)
    m = dict(norm1_g=m_norm1_g, w_in=m_w_in, forget_b=m_forget_b, rel_bias=m_rel_bias, outnorm_a_g=m_outnorm_a_g,
             outnorm_b_g=m_outnorm_b_g, w_out=m_w_out, norm2_g=m_norm2_g, w_mlp_in=m_w_mlp_in,
             w_mlp_out=m_w_mlp_out, final_norm_g=m_final_norm_g)
    v = dict(norm1_g=v_norm1_g, w_in=v_w_in, forget_b=v_forget_b, rel_bias=v_rel_bias, outnorm_a_g=v_outnorm_a_g,
             outnorm_b_g=v_outnorm_b_g, w_out=v_w_out, norm2_g=v_norm2_g, w_mlp_in=v_w_mlp_in,
             w_mlp_out=v_w_mlp_out, final_norm_g=v_final_norm_g)
    depth = w_in.shape[0]
    small = {k: w[k] for k in _SMALL}

    gathers, passed, tokens = {}, {}, []
    for l in range(depth):
        for k in _BIG:
            buf = _cast_into_slot(w[k], l, f"cast_{k}_l{l}")
            start = _relay_start if k == _BIG[0] else _gather_start
            send_sems, recv_sems, buf, token = start(buf, tokens, f"gather_start_{k}_l{l}")
            gathers[l, k], tokens = (send_sems, recv_sems, buf), [token]

    def weight(l, k, after):
        if k == _BIG[-1] and l + 1 < depth:
            passed[l + 1] = _relay_pass(*gathers[l + 1, _BIG[0]], after, f"gather_pass_{_BIG[0]}_l{l + 1}")
        if k != _BIG[0]:
            return _gather_wait(*gathers[l, k], after, f"gather_wait_{k}_l{l}")
        if l not in passed:
            passed[l] = _relay_pass(*gathers[l, k], after, f"gather_pass_{k}_l{l}")
        return _relay_wait(*passed[l], after, f"gather_wait_{k}_l{l}")

    scatters = {}

    def send(l, k, g):
        scatters[l, k] = _scatter_start(g, f"rs_start_{k}_l{l}")
        return [scatters[l, k][4]]

    loss, grad_x, small_grads, tokens = _local_step(x[0], loss_target[0], small, weight, send, tokens)
    loss = lax.psum(loss, ("x", "y", "c"))

    grads, delta, new_m, new_v = {}, {}, {}, {}
    packed = _tied(_pack_small(small_grads), tokens, "tie_small")
    after, seen, joining = packed, {k: 0 for k in _BIG}, None

    def joined(after):
        (l, k), (send_sem, recv_sem, g) = joining
        grads[k] = _join_wait(send_sem, recv_sem, g, l, after, f"rs_join_wait_{k}_l{l}")
        seen[k] += 1
        if seen[k] < depth:
            return after
        shape = w[k].shape
        flat = lambda t: t.reshape(-1, shape[-1])
        outs = _adamw(flat(w[k]), flat(grads[k]), flat(m[k]), flat(v[k]), f"adamw_{k}")
        grads[k], delta[k], new_m[k], new_v[k] = (t.reshape(shape) for t in outs)
        return outs[1]

    for (l, k), started in scatters.items():
        assert joining is None or joining[0][1] != k
        send_sem, recv_sem, g, token = _reduce_scatter_sum(started, after, l, depth, grads.get(k), f"{k}_l{l}")
        if joining is not None:
            after = joined(token)
        joining = ((l, k), (send_sem, recv_sem, g))
    after = joined(after)
    small_sums = _all_reduce_small(_tied(packed, [after], "tie_small_sums"), "small_all_reduce")
    grads.update(_unpack_small(small_sums, small))
    _, d_, m_, v_ = _adamw(_pack_small(small), _pack_small({k: grads[k] for k in _SMALL}),
                           _pack_small({k: m[k] for k in _SMALL}), _pack_small({k: v[k] for k in _SMALL}), "adamw_small")
    delta.update(_unpack_small(d_, small))
    new_m.update(_unpack_small(m_, small))
    new_v.update(_unpack_small(v_, small))

    return (loss, grad_x[None], *[grads[k] for k in _ORDER], *[delta[k] for k in _ORDER],
            *[new_m[k] for k in _ORDER], *[new_v[k] for k in _ORDER])
```
